```python
import jax, jax.numpy as jnp
from jax import lax
import numpy as np

D_MODEL = 1024
BATCH = 8
SEQ = 8192
DEPTH = 2

CHUNK = 64
D_CONV = D_MODEL // 4
CONV_WIDTH = 31
SB_HEAD_DIM = 64
D_SB = D_MODEL // 2
N_SB_HEADS = D_SB // SB_HEAD_DIM
RET_HEAD_DIM = 64
D_RET = D_MODEL // 4
N_RET_HEADS = D_RET // RET_HEAD_DIM
D_MIX = D_CONV + D_SB + D_RET
D_IN_PROJ = 2 * D_CONV + 3 * D_SB + 4 * D_RET
D_FF = ((8 * D_MODEL // 3 + 127) // 128) * 128
Q_BLOCK = 128
ROPE_BASE = 10000.0
EPS = 1e-6

kernel_name = "hybrid_conv_stickbreak_retention_macaron"


def rms_norm(x, g):
    xf = x.astype(jnp.float32)
    y = xf * lax.rsqrt(jnp.mean(xf * xf, axis=-1, keepdims=True) + EPS)
    return (y * g.astype(jnp.float32)).astype(x.dtype)


def layer_norm(x, g, b):
    xf = x.astype(jnp.float32)
    mu = jnp.mean(xf, axis=-1, keepdims=True)
    xc = xf - mu
    var = jnp.mean(xc * xc, axis=-1, keepdims=True)
    return (xc * lax.rsqrt(var + EPS) * g.astype(jnp.float32) + b.astype(jnp.float32)).astype(x.dtype)


def swiglu(h, w_in, w_out):
    gate, up = jnp.split(h @ w_in, 2, axis=-1)
    return (jax.nn.silu(gate) * up) @ w_out


def conv_module(u, conv_w, conv_b, ln_g, ln_b):
    a, b = jnp.split(u, 2, axis=-1)
    v = a * jax.nn.sigmoid(b)
    v = jnp.pad(v, ((0, 0), (CONV_WIDTH - 1, 0), (0, 0)))
    y = lax.conv_general_dilated(
        v, conv_w[:, None, :].astype(v.dtype), window_strides=(1,), padding="VALID",
        dimension_numbers=("NWC", "WIO", "NWC"), feature_group_count=D_CONV)
    y = y + conv_b
    return jax.nn.silu(layer_norm(y, ln_g, ln_b))


def stick_breaking(q, k, v):
    B, S, H, Dh = q.shape
    nb = S // Q_BLOCK
    qb = q.reshape(B, nb, Q_BLOCK, H, Dh).transpose(1, 0, 2, 3, 4)
    kpos = jnp.arange(S)
    scale = Dh ** -0.5

    def block(args):
        qi, i = args
        z = jnp.einsum("bqhd,bkhd->bhqk", qi, k,
                       preferred_element_type=jnp.float32) * scale
        qpos = i * Q_BLOCK + jnp.arange(Q_BLOCK)
        mask = kpos[None, :] < qpos[:, None]
        log_beta = jax.nn.log_sigmoid(z)
        log_not = jnp.where(mask, jax.nn.log_sigmoid(-z), 0.0)
        remain = lax.cumsum(log_not, axis=3, reverse=True) - log_not
        w = jnp.where(mask, jnp.exp(log_beta + remain), 0.0)
        return jnp.einsum("bhqk,bkhd->bqhd", w.astype(v.dtype), v)

    out = lax.map(block, (qb, jnp.arange(nb)))
    return out.transpose(1, 0, 2, 3, 4).reshape(B, S, H, Dh)


def rotary(x, pos):
    half = x.shape[-1] // 2
    inv = 1.0 / (ROPE_BASE ** (jnp.arange(half, dtype=jnp.float32) / half))
    ang = pos.astype(jnp.float32)[:, None] * inv[None, :]
    cos = jnp.cos(ang)[None, :, None, :]
    sin = jnp.sin(ang)[None, :, None, :]
    x1 = x[..., :half].astype(jnp.float32)
    x2 = x[..., half:].astype(jnp.float32)
    return jnp.concatenate([x1 * cos - x2 * sin, x1 * sin + x2 * cos], axis=-1).astype(x.dtype)


def retention(q, k, v):
    B, S, H, Dh = q.shape
    nc = S // CHUNK
    log_gamma = jnp.log1p(-jnp.exp2(-5.0 - jnp.arange(H, dtype=jnp.float32)))
    qc = (q * (Dh ** -0.5)).reshape(B, nc, CHUNK, H, Dh)
    kc = k.reshape(B, nc, CHUNK, H, Dh)
    vc = v.reshape(B, nc, CHUNK, H, Dh)
    idx = jnp.arange(CHUNK, dtype=jnp.float32)
    d_intra = jnp.exp(log_gamma[:, None, None] * jnp.abs(idx[:, None] - idx[None, :]))
    scores = jnp.einsum("bnihd,bnjhd->bnhij", qc, kc,
                        preferred_element_type=jnp.float32) * d_intra
    y_intra = jnp.einsum("bnhij,bnjhe->bnihe", scores, vc.astype(jnp.float32))
    k_decay = jnp.exp(log_gamma[None, :] * (CHUNK - 1 - idx)[:, None])
    kv = jnp.einsum("bnjhd,jh,bnjhe->bnhde", kc.astype(jnp.float32), k_decay,
                    vc.astype(jnp.float32))
    chunk_decay = jnp.exp(log_gamma * CHUNK)[None, :, None, None]

    def step(state, kv_n):
        return chunk_decay * state + kv_n, state

    _, s_prev = lax.scan(step, jnp.zeros((B, H, Dh, Dh), jnp.float32),
                         kv.transpose(1, 0, 2, 3, 4))
    s_prev = s_prev.transpose(1, 0, 2, 3, 4)
    q_decay = jnp.exp(log_gamma[None, :] * (idx + 1.0)[:, None])
    y_cross = jnp.einsum("bnihd,ih,bnhde->bnihe", qc.astype(jnp.float32), q_decay, s_prev)
    return (y_intra + y_cross).reshape(B, S, H, Dh)


def head_norm(y, g):
    B, S, H, Dh = y.shape
    mu = jnp.mean(y, axis=-1, keepdims=True)
    yc = y - mu
    var = jnp.mean(yc * yc, axis=-1, keepdims=True)
    return (yc * lax.rsqrt(var + EPS)).reshape(B, S, H * Dh) * g.astype(jnp.float32)


def hybrid_mixer(h, w_in, conv_w, conv_b, conv_ln_g, conv_ln_b, ret_norm_g, w_out, pos):
    B, S, _ = h.shape
    o1 = 2 * D_CONV
    o2 = o1 + D_SB
    o3 = o2 + D_SB
    o4 = o3 + D_SB
    o5 = o4 + D_RET
    o6 = o5 + D_RET
    o7 = o6 + D_RET
    u_conv, q_sb, k_sb, v_sb, q_r, k_r, v_r, g_r = jnp.split(
        h @ w_in, [o1, o2, o3, o4, o5, o6, o7], axis=-1)
    y_conv = conv_module(u_conv, conv_w, conv_b, conv_ln_g, conv_ln_b)
    sb = lambda t: t.reshape(B, S, N_SB_HEADS, SB_HEAD_DIM)
    y_sb = stick_breaking(sb(q_sb), sb(k_sb), sb(v_sb)).reshape(B, S, D_SB)
    rt = lambda t: t.reshape(B, S, N_RET_HEADS, RET_HEAD_DIM)
    y_r = retention(rotary(rt(q_r), pos), rotary(rt(k_r), pos), rt(v_r))
    y_r = jax.nn.silu(g_r.astype(jnp.float32)) * head_norm(y_r, ret_norm_g)
    y = jnp.concatenate([y_conv, y_sb, y_r.astype(h.dtype)], axis=-1)
    return y @ w_out


def _fwd_setup_inputs(seed: int = 0) -> dict:
    key = jax.random.key(seed)
    ks = jax.random.split(key, 20)
    f32 = jnp.float32
    nrm = lambda k, shape, scale: jax.random.normal(k, shape, f32) * scale
    gain = lambda k, shape: 1.0 + 0.02 * jax.random.normal(k, shape, f32)
    return {
        "x": jax.random.normal(ks[0], (BATCH, SEQ, D_MODEL), f32),
        "ffn1_norm": gain(ks[1], (DEPTH, D_MODEL)),
        "ffn1_w_in": nrm(ks[2], (DEPTH, D_MODEL, 2 * D_FF), D_MODEL ** -0.5),
        "ffn1_w_out": nrm(ks[3], (DEPTH, D_FF, D_MODEL), D_FF ** -0.5),
        "mix_norm": gain(ks[4], (DEPTH, D_MODEL)),
        "mix_w_in": nrm(ks[5], (DEPTH, D_MODEL, D_IN_PROJ), D_MODEL ** -0.5),
        "conv_w": nrm(ks[6], (DEPTH, CONV_WIDTH, D_CONV), CONV_WIDTH ** -0.5),
        "conv_b": nrm(ks[7], (DEPTH, D_CONV), 0.02),
        "conv_ln_g": gain(ks[8], (DEPTH, D_CONV)),
        "conv_ln_b": nrm(ks[9], (DEPTH, D_CONV), 0.02),
        "ret_norm_g": gain(ks[10], (DEPTH, D_RET)),
        "mix_w_out": nrm(ks[11], (DEPTH, D_MIX, D_MODEL), D_MIX ** -0.5),
        "ffn2_norm": gain(ks[12], (DEPTH, D_MODEL)),
        "ffn2_w_in": nrm(ks[13], (DEPTH, D_MODEL, 2 * D_FF), D_MODEL ** -0.5),
        "ffn2_w_out": nrm(ks[14], (DEPTH, D_FF, D_MODEL), D_FF ** -0.5),
        "final_norm": gain(ks[15], (D_MODEL,)),
    }


def _fwd_reference(x, ffn1_norm, ffn1_w_in, ffn1_w_out, mix_norm, mix_w_in, conv_w, conv_b,
              conv_ln_g, conv_ln_b, ret_norm_g, mix_w_out, ffn2_norm, ffn2_w_in, ffn2_w_out,
              final_norm):
    S = x.shape[1]
    pos = jnp.arange(S)
    for l in range(DEPTH):
        x = x + 0.5 * swiglu(rms_norm(x, ffn1_norm[l]), ffn1_w_in[l], ffn1_w_out[l])
        x = x + hybrid_mixer(rms_norm(x, mix_norm[l]), mix_w_in[l], conv_w[l], conv_b[l],
                             conv_ln_g[l], conv_ln_b[l], ret_norm_g[l], mix_w_out[l], pos)
        x = x + 0.5 * swiglu(rms_norm(x, ffn2_norm[l]), ffn2_w_in[l], ffn2_w_out[l])
    return rms_norm(x, final_norm)


import jax as _jax
import jax.numpy as _jnp

TWIN_FORMAT = 'train_step'
FWD_PARAMS = ['x', 'ffn1_norm', 'ffn1_w_in', 'ffn1_w_out', 'mix_norm', 'mix_w_in', 'conv_w', 'conv_b', 'conv_ln_g', 'conv_ln_b', 'ret_norm_g', 'mix_w_out', 'ffn2_norm', 'ffn2_w_in', 'ffn2_w_out', 'final_norm']
TWIN_WEIGHTS = ['ffn1_norm', 'ffn1_w_in', 'ffn1_w_out', 'mix_norm', 'mix_w_in', 'conv_w', 'conv_b', 'conv_ln_g', 'conv_ln_b', 'ret_norm_g', 'mix_w_out', 'ffn2_norm', 'ffn2_w_in', 'ffn2_w_out', 'final_norm']
TWIN_DIFF_INPUT = 'x'
TWIN_INPUTS = ['x', 'ffn1_norm', 'ffn1_w_in', 'ffn1_w_out', 'mix_norm', 'mix_w_in', 'conv_w', 'conv_b', 'conv_ln_g', 'conv_ln_b', 'ret_norm_g', 'mix_w_out', 'ffn2_norm', 'ffn2_w_in', 'ffn2_w_out', 'final_norm', 'loss_target', 'm_ffn1_norm', 'm_ffn1_w_in', 'm_ffn1_w_out', 'm_mix_norm', 'm_mix_w_in', 'm_conv_w', 'm_conv_b', 'm_conv_ln_g', 'm_conv_ln_b', 'm_ret_norm_g', 'm_mix_w_out', 'm_ffn2_norm', 'm_ffn2_w_in', 'm_ffn2_w_out', 'm_final_norm', 'v_ffn1_norm', 'v_ffn1_w_in', 'v_ffn1_w_out', 'v_mix_norm', 'v_mix_w_in', 'v_conv_w', 'v_conv_b', 'v_conv_ln_g', 'v_conv_ln_b', 'v_ret_norm_g', 'v_mix_w_out', 'v_ffn2_norm', 'v_ffn2_w_in', 'v_ffn2_w_out', 'v_final_norm']
TWIN_OUTPUTS = ['loss', 'grad_x', 'grad_ffn1_norm', 'grad_ffn1_w_in', 'grad_ffn1_w_out', 'grad_mix_norm', 'grad_mix_w_in', 'grad_conv_w', 'grad_conv_b', 'grad_conv_ln_g', 'grad_conv_ln_b', 'grad_ret_norm_g', 'grad_mix_w_out', 'grad_ffn2_norm', 'grad_ffn2_w_in', 'grad_ffn2_w_out', 'grad_final_norm', 'delta_ffn1_norm', 'delta_ffn1_w_in', 'delta_ffn1_w_out', 'delta_mix_norm', 'delta_mix_w_in', 'delta_conv_w', 'delta_conv_b', 'delta_conv_ln_g', 'delta_conv_ln_b', 'delta_ret_norm_g', 'delta_mix_w_out', 'delta_ffn2_norm', 'delta_ffn2_w_in', 'delta_ffn2_w_out', 'delta_final_norm', 'new_m_ffn1_norm', 'new_m_ffn1_w_in', 'new_m_ffn1_w_out', 'new_m_mix_norm', 'new_m_mix_w_in', 'new_m_conv_w', 'new_m_conv_b', 'new_m_conv_ln_g', 'new_m_conv_ln_b', 'new_m_ret_norm_g', 'new_m_mix_w_out', 'new_m_ffn2_norm', 'new_m_ffn2_w_in', 'new_m_ffn2_w_out', 'new_m_final_norm', 'new_v_ffn1_norm', 'new_v_ffn1_w_in', 'new_v_ffn1_w_out', 'new_v_mix_norm', 'new_v_mix_w_in', 'new_v_conv_w', 'new_v_conv_b', 'new_v_conv_ln_g', 'new_v_conv_ln_b', 'new_v_ret_norm_g', 'new_v_mix_w_out', 'new_v_ffn2_norm', 'new_v_ffn2_w_in', 'new_v_ffn2_w_out', 'new_v_final_norm']
TWIN_LEAF_KINDS = {'loss': 'loss', 'grad_x': 'grad_x', 'grad_ffn1_norm': 'grad_w', 'grad_ffn1_w_in': 'grad_w', 'grad_ffn1_w_out': 'grad_w', 'grad_mix_norm': 'grad_w', 'grad_mix_w_in': 'grad_w', 'grad_conv_w': 'grad_w', 'grad_conv_b': 'grad_w', 'grad_conv_ln_g': 'grad_w', 'grad_conv_ln_b': 'grad_w', 'grad_ret_norm_g': 'grad_w', 'grad_mix_w_out': 'grad_w', 'grad_ffn2_norm': 'grad_w', 'grad_ffn2_w_in': 'grad_w', 'grad_ffn2_w_out': 'grad_w', 'grad_final_norm': 'grad_w', 'delta_ffn1_norm': 'delta_w', 'delta_ffn1_w_in': 'delta_w', 'delta_ffn1_w_out': 'delta_w', 'delta_mix_norm': 'delta_w', 'delta_mix_w_in': 'delta_w', 'delta_conv_w': 'delta_w', 'delta_conv_b': 'delta_w', 'delta_conv_ln_g': 'delta_w', 'delta_conv_ln_b': 'delta_w', 'delta_ret_norm_g': 'delta_w', 'delta_mix_w_out': 'delta_w', 'delta_ffn2_norm': 'delta_w', 'delta_ffn2_w_in': 'delta_w', 'delta_ffn2_w_out': 'delta_w', 'delta_final_norm': 'delta_w', 'new_m_ffn1_norm': 'new_m', 'new_m_ffn1_w_in': 'new_m', 'new_m_ffn1_w_out': 'new_m', 'new_m_mix_norm': 'new_m', 'new_m_mix_w_in': 'new_m', 'new_m_conv_w': 'new_m', 'new_m_conv_b': 'new_m', 'new_m_conv_ln_g': 'new_m', 'new_m_conv_ln_b': 'new_m', 'new_m_ret_norm_g': 'new_m', 'new_m_mix_w_out': 'new_m', 'new_m_ffn2_norm': 'new_m', 'new_m_ffn2_w_in': 'new_m', 'new_m_ffn2_w_out': 'new_m', 'new_m_final_norm': 'new_m', 'new_v_ffn1_norm': 'new_v', 'new_v_ffn1_w_in': 'new_v', 'new_v_ffn1_w_out': 'new_v', 'new_v_mix_norm': 'new_v', 'new_v_mix_w_in': 'new_v', 'new_v_conv_w': 'new_v', 'new_v_conv_b': 'new_v', 'new_v_conv_ln_g': 'new_v', 'new_v_conv_ln_b': 'new_v', 'new_v_ret_norm_g': 'new_v', 'new_v_mix_w_out': 'new_v', 'new_v_ffn2_norm': 'new_v', 'new_v_ffn2_w_in': 'new_v', 'new_v_ffn2_w_out': 'new_v', 'new_v_final_norm': 'new_v'}


def _forward(args):
    return _fwd_reference(*[args[k] for k in FWD_PARAMS])


def _output_shape():
    def fwd():
        inp = _fwd_setup_inputs(0)
        return _fwd_reference(*[inp[k] for k in FWD_PARAMS])
    out = _jax.eval_shape(fwd)
    return out.shape, out.dtype

N_MICROBATCH = 1
ADAM_LR = 0.001
ADAM_B1 = 0.9
ADAM_B2 = 0.999
ADAM_EPS = 1e-08
ADAM_WD = 0.01
ADAM_STEP = 10
PER_EXAMPLE_BATCH_AXIS = {'x': 0, 'loss_target': 0}
SHARED_INPUTS = []
_WEIGHT_DTYPES = {'ffn1_norm': _jnp.float32, 'ffn1_w_in': _jnp.float32, 'ffn1_w_out': _jnp.float32, 'mix_norm': _jnp.float32, 'mix_w_in': _jnp.float32, 'conv_w': _jnp.float32, 'conv_b': _jnp.float32, 'conv_ln_g': _jnp.float32, 'conv_ln_b': _jnp.float32, 'ret_norm_g': _jnp.float32, 'mix_w_out': _jnp.float32, 'ffn2_norm': _jnp.float32, 'ffn2_w_in': _jnp.float32, 'ffn2_w_out': _jnp.float32, 'final_norm': _jnp.float32}
MOMENT_SCALE = {'ffn1_norm': 1.198596e-01, 'ffn1_w_in': 4.957715e-02, 'ffn1_w_out': 8.077608e-02, 'mix_norm': 1.931585e-01, 'mix_w_in': 1.088600e-01, 'conv_w': 1.339329e-01, 'conv_b': 2.856931e-01, 'conv_ln_g': 1.841014e-01, 'conv_ln_b': 1.666954e-01, 'ret_norm_g': 1.263603e-01, 'mix_w_out': 1.352619e-01, 'ffn2_norm': 8.716333e-02, 'ffn2_w_in': 3.704449e-02, 'ffn2_w_out': 6.041037e-02, 'final_norm': 6.400695e+01}


def _to_microbatches(a, axis):
    t = _jnp.moveaxis(a, axis, 0)
    t = t.reshape((N_MICROBATCH, t.shape[0] // N_MICROBATCH) + t.shape[1:])
    return _jnp.moveaxis(t, 1, axis + 1)


def setup_inputs(seed: int = 0) -> dict:
    inp = _fwd_setup_inputs(seed)
    key = _jax.random.fold_in(_jax.random.key(seed), 7919)
    shape, _ = _output_shape()
    out = dict(inp)
    out["loss_target"] = _jax.random.normal(_jax.random.fold_in(key, 0), shape, _jnp.float32)
    for i, name in enumerate(TWIN_WEIGHTS):
        w = inp[name].astype(_jnp.float32)
        if MOMENT_SCALE is None:
            s = _jnp.sqrt(_jnp.mean(_jnp.square(w)) + 1e-30)
        else:
            s = MOMENT_SCALE[name]
        km, kv = _jax.random.split(_jax.random.fold_in(key, i + 1))
        out[name] = w
        out["m_" + name] = s * _jax.random.normal(km, w.shape, _jnp.float32)
        out["v_" + name] = (s * s) * _jax.random.uniform(kv, w.shape, _jnp.float32, 0.5, 1.5)
    if N_MICROBATCH > 1:
        for name, axis in PER_EXAMPLE_BATCH_AXIS.items():
            out[name] = _to_microbatches(out[name], axis)
    return {'x': out['x'], 'ffn1_norm': out['ffn1_norm'], 'ffn1_w_in': out['ffn1_w_in'], 'ffn1_w_out': out['ffn1_w_out'], 'mix_norm': out['mix_norm'], 'mix_w_in': out['mix_w_in'], 'conv_w': out['conv_w'], 'conv_b': out['conv_b'], 'conv_ln_g': out['conv_ln_g'], 'conv_ln_b': out['conv_ln_b'], 'ret_norm_g': out['ret_norm_g'], 'mix_w_out': out['mix_w_out'], 'ffn2_norm': out['ffn2_norm'], 'ffn2_w_in': out['ffn2_w_in'], 'ffn2_w_out': out['ffn2_w_out'], 'final_norm': out['final_norm'], 'loss_target': out['loss_target'], 'm_ffn1_norm': out['m_ffn1_norm'], 'm_ffn1_w_in': out['m_ffn1_w_in'], 'm_ffn1_w_out': out['m_ffn1_w_out'], 'm_mix_norm': out['m_mix_norm'], 'm_mix_w_in': out['m_mix_w_in'], 'm_conv_w': out['m_conv_w'], 'm_conv_b': out['m_conv_b'], 'm_conv_ln_g': out['m_conv_ln_g'], 'm_conv_ln_b': out['m_conv_ln_b'], 'm_ret_norm_g': out['m_ret_norm_g'], 'm_mix_w_out': out['m_mix_w_out'], 'm_ffn2_norm': out['m_ffn2_norm'], 'm_ffn2_w_in': out['m_ffn2_w_in'], 'm_ffn2_w_out': out['m_ffn2_w_out'], 'm_final_norm': out['m_final_norm'], 'v_ffn1_norm': out['v_ffn1_norm'], 'v_ffn1_w_in': out['v_ffn1_w_in'], 'v_ffn1_w_out': out['v_ffn1_w_out'], 'v_mix_norm': out['v_mix_norm'], 'v_mix_w_in': out['v_mix_w_in'], 'v_conv_w': out['v_conv_w'], 'v_conv_b': out['v_conv_b'], 'v_conv_ln_g': out['v_conv_ln_g'], 'v_conv_ln_b': out['v_conv_ln_b'], 'v_ret_norm_g': out['v_ret_norm_g'], 'v_mix_w_out': out['v_mix_w_out'], 'v_ffn2_norm': out['v_ffn2_norm'], 'v_ffn2_w_in': out['v_ffn2_w_in'], 'v_ffn2_w_out': out['v_ffn2_w_out'], 'v_final_norm': out['v_final_norm']}


def _loss(weights, diff, rest, loss_target):
    with _jax.named_scope("forward"):
        args = {**rest, TWIN_DIFF_INPUT: diff, **{k: w.astype(_WEIGHT_DTYPES[k]) for k, w in weights.items()}}
        y = _forward(args)
    with _jax.named_scope("loss_head"):
        err = _jnp.square(y.astype(_jnp.float32) - loss_target)
        return 0.5 * _jnp.sum(_jnp.mean(err, axis=-1)) if err.ndim else 0.5 * err


def _adamw(w, g, m, v):
    m = ADAM_B1 * m + (1.0 - ADAM_B1) * g
    v = ADAM_B2 * v + (1.0 - ADAM_B2) * _jnp.square(g)
    m_hat = m / (1.0 - ADAM_B1 ** ADAM_STEP)
    v_hat = v / (1.0 - ADAM_B2 ** ADAM_STEP)
    delta = -ADAM_LR * (m_hat / (_jnp.sqrt(v_hat) + ADAM_EPS) + ADAM_WD * w)
    return delta, m, v


def reference(x, ffn1_norm, ffn1_w_in, ffn1_w_out, mix_norm, mix_w_in, conv_w, conv_b, conv_ln_g, conv_ln_b, ret_norm_g, mix_w_out, ffn2_norm, ffn2_w_in, ffn2_w_out, final_norm, loss_target, m_ffn1_norm, m_ffn1_w_in, m_ffn1_w_out, m_mix_norm, m_mix_w_in, m_conv_w, m_conv_b, m_conv_ln_g, m_conv_ln_b, m_ret_norm_g, m_mix_w_out, m_ffn2_norm, m_ffn2_w_in, m_ffn2_w_out, m_final_norm, v_ffn1_norm, v_ffn1_w_in, v_ffn1_w_out, v_mix_norm, v_mix_w_in, v_conv_w, v_conv_b, v_conv_ln_g, v_conv_ln_b, v_ret_norm_g, v_mix_w_out, v_ffn2_norm, v_ffn2_w_in, v_ffn2_w_out, v_final_norm):
    given = dict(x=x, ffn1_norm=ffn1_norm, ffn1_w_in=ffn1_w_in, ffn1_w_out=ffn1_w_out, mix_norm=mix_norm, mix_w_in=mix_w_in, conv_w=conv_w, conv_b=conv_b, conv_ln_g=conv_ln_g, conv_ln_b=conv_ln_b, ret_norm_g=ret_norm_g, mix_w_out=mix_w_out, ffn2_norm=ffn2_norm, ffn2_w_in=ffn2_w_in, ffn2_w_out=ffn2_w_out, final_norm=final_norm, loss_target=loss_target, m_ffn1_norm=m_ffn1_norm, m_ffn1_w_in=m_ffn1_w_in, m_ffn1_w_out=m_ffn1_w_out, m_mix_norm=m_mix_norm, m_mix_w_in=m_mix_w_in, m_conv_w=m_conv_w, m_conv_b=m_conv_b, m_conv_ln_g=m_conv_ln_g, m_conv_ln_b=m_conv_ln_b, m_ret_norm_g=m_ret_norm_g, m_mix_w_out=m_mix_w_out, m_ffn2_norm=m_ffn2_norm, m_ffn2_w_in=m_ffn2_w_in, m_ffn2_w_out=m_ffn2_w_out, m_final_norm=m_final_norm, v_ffn1_norm=v_ffn1_norm, v_ffn1_w_in=v_ffn1_w_in, v_ffn1_w_out=v_ffn1_w_out, v_mix_norm=v_mix_norm, v_mix_w_in=v_mix_w_in, v_conv_w=v_conv_w, v_conv_b=v_conv_b, v_conv_ln_g=v_conv_ln_g, v_conv_ln_b=v_conv_ln_b, v_ret_norm_g=v_ret_norm_g, v_mix_w_out=v_mix_w_out, v_ffn2_norm=v_ffn2_norm, v_ffn2_w_in=v_ffn2_w_in, v_ffn2_w_out=v_ffn2_w_out, v_final_norm=v_final_norm)
    weights = {n: given[n] for n in TWIN_WEIGHTS}
    shared = {n: given[n] for n in SHARED_INPUTS}
    per_example = {n: given[n] for n in ['x']}
    grad_fn = _jax.value_and_grad(_loss, argnums=(0, 1))

    def one_microbatch(ex, loss_target):
        ex = dict(ex)
        diff = ex.pop(TWIN_DIFF_INPUT)
        return grad_fn(weights, diff, {**shared, **ex}, loss_target)

    if N_MICROBATCH == 1:
        loss, (grad_w, grad_x) = one_microbatch(per_example, given["loss_target"])
    else:
        def body(carry, xs):
            loss_sum, grad_sum = carry
            l_k, (gw_k, gx_k) = one_microbatch(xs[0], xs[1])
            with _jax.named_scope("update"):
                return (loss_sum + l_k, _jax.tree.map(_jnp.add, grad_sum, gw_k)), gx_k

        init = (_jnp.zeros((), _jnp.float32), _jax.tree.map(_jnp.zeros_like, weights))
        (loss, grad_w), grad_x = _jax.lax.scan(body, init, (per_example, given["loss_target"]))
    with _jax.named_scope("update"):
        delta_w, new_m, new_v = {}, {}, {}
        for n in TWIN_WEIGHTS:
            delta_w[n], new_m[n], new_v[n] = _adamw(weights[n], grad_w[n], given["m_" + n], given["v_" + n])
    return (loss, grad_x, *[grad_w[n] for n in TWIN_WEIGHTS], *[delta_w[n] for n in TWIN_WEIGHTS],
            *[new_m[n] for n in TWIN_WEIGHTS], *[new_v[n] for n in TWIN_WEIGHTS])
```

```python
import functools

import numpy as np
import jax
import jax.numpy as jnp
from jax import lax
from jax.experimental import pallas as pl
from jax.experimental.pallas import tpu as pltpu

F32 = jnp.float32
BF16 = jnp.bfloat16

D_MODEL = 1024
DEPTH = 2
D_FF = 2816
D_CONV = 256
CONV_WIDTH = 31
CONV_HALO = 32
D_SB = 512
N_SB_HEADS = 8
D_RET = 256
N_RET_HEADS = 4
HEAD_DIM = 64
D_IN_PROJ = 3072
ROPE_BASE = 10000.0
EPS = 1e-6
N_DEV = 8

ADAM_LR = 0.001
ADAM_B1 = 0.9
ADAM_B2 = 0.999
ADAM_EPS = 1e-08
ADAM_WD = 0.01
ADAM_STEP = 10

VMEM_LIMIT = 56 * 1024 * 1024
ROW_TILE = 512
FF_TILE = 1408
SB_TILE = 256
RET_TILE = 256
CONV_TILE = 256

NT_DIMS = (((1,), (1,)), ((), ()))
TN_DIMS = (((0,), (0,)), ((), ()))


def _params(n_axes, vmem=VMEM_LIMIT):
    return pltpu.CompilerParams(dimension_semantics=("arbitrary",) * n_axes, vmem_limit_bytes=vmem)


def _dot(a, b):
    return jnp.dot(a, b, preferred_element_type=F32)


def _dot_nt(a, b):
    return lax.dot_general(a, b, NT_DIMS, preferred_element_type=F32)


def _dot_tn(a, b):
    return lax.dot_general(a, b, TN_DIMS, preferred_element_type=F32)


def _sigmoid(z):
    return 1.0 / (1.0 + jnp.exp(-z))


def _rms_stats(xv):
    r = lax.rsqrt(jnp.mean(xv * xv, axis=-1, keepdims=True) + EPS)
    return r, xv * r


def _rms_bwd(xv, g, dh):
    r, xhat = _rms_stats(xv)
    dxhat = dh * g
    dx = r * (dxhat - xhat * jnp.mean(dxhat * xhat, axis=-1, keepdims=True))
    dg = jnp.sum(dh * xhat, axis=0, keepdims=True)
    return dx, (xhat * g).astype(BF16), dg


def ffn_fwd(x, g, w_in, w_out, tm=ROW_TILE):
    S = x.shape[0]
    nj = D_FF // FF_TILE

    def body(x_ref, g_ref, wg_ref, wu_ref, wo_ref, y_ref, gate_ref, up_ref, h_sc, acc_sc):
        j = pl.program_id(1)

        @pl.when(j == 0)
        def _():
            _, xhat = _rms_stats(x_ref[...])
            h_sc[...] = (xhat * g_ref[...]).astype(BF16)
            acc_sc[...] = jnp.zeros_like(acc_sc)

        h = h_sc[...]
        gt = _dot(h, wg_ref[...])
        up = _dot(h, wu_ref[...])
        gate_ref[...] = gt.astype(BF16)
        up_ref[...] = up.astype(BF16)
        hid = (gt * _sigmoid(gt) * up).astype(BF16)
        acc_sc[...] += _dot(hid, wo_ref[...])

        @pl.when(j == nj - 1)
        def _():
            y_ref[...] = x_ref[...] + 0.5 * acc_sc[...]

    return pl.pallas_call(
        body, name="ffn_fwd",
        grid=(S // tm, nj),
        in_specs=[
            pl.BlockSpec((tm, D_MODEL), lambda i, j: (i, 0)),
            pl.BlockSpec((1, D_MODEL), lambda i, j: (0, 0)),
            pl.BlockSpec((D_MODEL, FF_TILE), lambda i, j: (0, j)),
            pl.BlockSpec((D_MODEL, FF_TILE), lambda i, j: (0, j + nj)),
            pl.BlockSpec((FF_TILE, D_MODEL), lambda i, j: (j, 0)),
        ],
        out_specs=[
            pl.BlockSpec((tm, D_MODEL), lambda i, j: (i, 0)),
            pl.BlockSpec((tm, FF_TILE), lambda i, j: (i, j)),
            pl.BlockSpec((tm, FF_TILE), lambda i, j: (i, j)),
        ],
        out_shape=[
            jax.ShapeDtypeStruct((S, D_MODEL), F32),
            jax.ShapeDtypeStruct((S, D_FF), BF16),
            jax.ShapeDtypeStruct((S, D_FF), BF16),
        ],
        scratch_shapes=[pltpu.VMEM((tm, D_MODEL), BF16), pltpu.VMEM((tm, D_MODEL), F32)],
        compiler_params=_params(2),
    )(x, g, w_in, w_in, w_out)


def ffn_bwd(dy, x, g, gate, up, w_in, w_out, tm=ROW_TILE // 2):
    S = x.shape[0]
    nj = D_FF // FF_TILE

    def body(dy_ref, x_ref, g_ref, gate_ref, up_ref, wg_ref, wu_ref, wo_ref,
             dx_ref, h_ref, dyh_ref, dgate_ref, dup_ref, hid_ref, dg_ref, d2_sc, dh_sc):
        i = pl.program_id(0)
        j = pl.program_id(1)

        @pl.when(j == 0)
        def _():
            d2 = (0.5 * dy_ref[...]).astype(BF16)
            d2_sc[...] = d2
            dyh_ref[...] = d2
            dh_sc[...] = jnp.zeros_like(dh_sc)

        dhid = _dot_nt(d2_sc[...], wo_ref[...])
        gt = gate_ref[...].astype(F32)
        u = up_ref[...].astype(F32)
        sig = _sigmoid(gt)
        sl = gt * sig
        dgate = (dhid * u * (sig * (1.0 + gt * (1.0 - sig)))).astype(BF16)
        dup = (dhid * sl).astype(BF16)
        dgate_ref[...] = dgate
        dup_ref[...] = dup
        hid_ref[...] = (sl * u).astype(BF16)
        dh_sc[...] += _dot_nt(dgate, wg_ref[...]) + _dot_nt(dup, wu_ref[...])

        @pl.when(j == nj - 1)
        def _():
            dx, h, dg = _rms_bwd(x_ref[...], g_ref[...], dh_sc[...])
            dx_ref[...] = dy_ref[...] + dx
            h_ref[...] = h

            @pl.when(i == 0)
            def _():
                dg_ref[...] = dg

            @pl.when(i > 0)
            def _():
                dg_ref[...] += dg

    row = lambda i, j: (i, 0)
    blk = lambda i, j: (i, j)
    return pl.pallas_call(
        body, name="ffn_bwd",
        grid=(S // tm, nj),
        in_specs=[
            pl.BlockSpec((tm, D_MODEL), row),
            pl.BlockSpec((tm, D_MODEL), row),
            pl.BlockSpec((1, D_MODEL), lambda i, j: (0, 0)),
            pl.BlockSpec((tm, FF_TILE), blk),
            pl.BlockSpec((tm, FF_TILE), blk),
            pl.BlockSpec((D_MODEL, FF_TILE), lambda i, j: (0, j)),
            pl.BlockSpec((D_MODEL, FF_TILE), lambda i, j: (0, j + nj)),
            pl.BlockSpec((FF_TILE, D_MODEL), lambda i, j: (j, 0)),
        ],
        out_specs=[
            pl.BlockSpec((tm, D_MODEL), row),
            pl.BlockSpec((tm, D_MODEL), row),
            pl.BlockSpec((tm, D_MODEL), row),
            pl.BlockSpec((tm, FF_TILE), blk),
            pl.BlockSpec((tm, FF_TILE), blk),
            pl.BlockSpec((tm, FF_TILE), blk),
            pl.BlockSpec((1, D_MODEL), lambda i, j: (0, 0)),
        ],
        out_shape=[
            jax.ShapeDtypeStruct((S, D_MODEL), F32),
            jax.ShapeDtypeStruct((S, D_MODEL), BF16),
            jax.ShapeDtypeStruct((S, D_MODEL), BF16),
            jax.ShapeDtypeStruct((S, D_FF), BF16),
            jax.ShapeDtypeStruct((S, D_FF), BF16),
            jax.ShapeDtypeStruct((S, D_FF), BF16),
            jax.ShapeDtypeStruct((1, D_MODEL), F32),
        ],
        scratch_shapes=[pltpu.VMEM((tm, D_MODEL), BF16), pltpu.VMEM((tm, D_MODEL), F32)],
        compiler_params=_params(2),
    )(dy, x, g, gate, up, w_in, w_in, w_out)


def matmul_tn(a, b, ta, tn, tk=ROW_TILE, name="matmul_tn"):
    S, ka = a.shape
    nb = b.shape[1]

    def body(a_ref, b_ref, o_ref):
        k = pl.program_id(2)

        @pl.when(k == 0)
        def _():
            o_ref[...] = jnp.zeros_like(o_ref)

        o_ref[...] += _dot_tn(a_ref[...], b_ref[...])

    return pl.pallas_call(
        body, name=name,
        grid=(ka // ta, nb // tn, S // tk),
        in_specs=[
            pl.BlockSpec((tk, ta), lambda i, j, k: (k, i)),
            pl.BlockSpec((tk, tn), lambda i, j, k: (k, j)),
        ],
        out_specs=pl.BlockSpec((ta, tn), lambda i, j, k: (i, j)),
        out_shape=jax.ShapeDtypeStruct((ka, nb), F32),
        compiler_params=_params(3),
    )(a, b)


def mix_in_fwd(x, g, w, tm=ROW_TILE):
    S = x.shape[0]

    def body(x_ref, g_ref, w_ref, o_ref):
        _, xhat = _rms_stats(x_ref[...])
        o_ref[...] = _dot((xhat * g_ref[...]).astype(BF16), w_ref[...])

    return pl.pallas_call(
        body, name="mix_in_fwd",
        grid=(S // tm,),
        in_specs=[
            pl.BlockSpec((tm, D_MODEL), lambda i: (i, 0)),
            pl.BlockSpec((1, D_MODEL), lambda i: (0, 0)),
            pl.BlockSpec((D_MODEL, D_IN_PROJ), lambda i: (0, 0)),
        ],
        out_specs=pl.BlockSpec((tm, D_IN_PROJ), lambda i: (i, 0)),
        out_shape=jax.ShapeDtypeStruct((S, D_IN_PROJ), F32),
        compiler_params=_params(1),
    )(x, g, w)


def mix_in_bwd(dproj, w, x, g, dy, tm=ROW_TILE):
    S = x.shape[0]

    def body(dp_ref, w_ref, x_ref, g_ref, dy_ref, dx_ref, h_ref, dg_ref):
        i = pl.program_id(0)
        dh = _dot_nt(dp_ref[...], w_ref[...])
        dx, h, dg = _rms_bwd(x_ref[...], g_ref[...], dh)
        dx_ref[...] = dy_ref[...] + dx
        h_ref[...] = h

        @pl.when(i == 0)
        def _():
            dg_ref[...] = dg

        @pl.when(i > 0)
        def _():
            dg_ref[...] += dg

    row = lambda i: (i, 0)
    return pl.pallas_call(
        body, name="mix_in_bwd",
        grid=(S // tm,),
        in_specs=[
            pl.BlockSpec((tm, D_IN_PROJ), row),
            pl.BlockSpec((D_MODEL, D_IN_PROJ), lambda i: (0, 0)),
            pl.BlockSpec((tm, D_MODEL), row),
            pl.BlockSpec((1, D_MODEL), lambda i: (0, 0)),
            pl.BlockSpec((tm, D_MODEL), row),
        ],
        out_specs=[
            pl.BlockSpec((tm, D_MODEL), row),
            pl.BlockSpec((tm, D_MODEL), row),
            pl.BlockSpec((1, D_MODEL), lambda i: (0, 0)),
        ],
        out_shape=[
            jax.ShapeDtypeStruct((S, D_MODEL), F32),
            jax.ShapeDtypeStruct((S, D_MODEL), BF16),
            jax.ShapeDtypeStruct((1, D_MODEL), F32),
        ],
        compiler_params=_params(1),
    )(dproj, w, x, g, dy)


def mix_out_fwd(ycat, w, x, tm=ROW_TILE):
    S = x.shape[0]

    def body(y_ref, w_ref, x_ref, o_ref):
        o_ref[...] = x_ref[...] + _dot(y_ref[...], w_ref[...])

    row = lambda i: (i, 0)
    return pl.pallas_call(
        body, name="mix_out_fwd",
        grid=(S // tm,),
        in_specs=[
            pl.BlockSpec((tm, D_MODEL), row),
            pl.BlockSpec((D_MODEL, D_MODEL), lambda i: (0, 0)),
            pl.BlockSpec((tm, D_MODEL), row),
        ],
        out_specs=pl.BlockSpec((tm, D_MODEL), row),
        out_shape=jax.ShapeDtypeStruct((S, D_MODEL), F32),
        compiler_params=_params(1),
    )(ycat, w, x)


def mix_out_bwd(dy, w, tm=ROW_TILE):
    S = dy.shape[0]

    def body(dy_ref, w_ref, o_ref, dyb_ref):
        d = dy_ref[...].astype(BF16)
        dyb_ref[...] = d
        o_ref[...] = _dot_nt(d, w_ref[...])

    row = lambda i: (i, 0)
    return pl.pallas_call(
        body, name="mix_out_bwd",
        grid=(S // tm,),
        in_specs=[
            pl.BlockSpec((tm, D_MODEL), row),
            pl.BlockSpec((D_MODEL, D_MODEL), lambda i: (0, 0)),
        ],
        out_specs=[pl.BlockSpec((tm, D_MODEL), row), pl.BlockSpec((tm, D_MODEL), row)],
        out_shape=[jax.ShapeDtypeStruct((S, D_MODEL), F32), jax.ShapeDtypeStruct((S, D_MODEL), BF16)],
        compiler_params=_params(1),
    )(dy, w)


def _conv_ln(ypre, ln_g, ln_b):
    mu = jnp.mean(ypre, axis=-1, keepdims=True)
    yc = ypre - mu
    rstd = lax.rsqrt(jnp.mean(yc * yc, axis=-1, keepdims=True) + EPS)
    yn = yc * rstd
    return yn, rstd, yn * ln_g + ln_b


def conv_fwd(proj, cw, cb, ln_g, ln_b, tm=CONV_TILE):
    S = proj.shape[0]
    hb = tm // CONV_HALO

    def body(a_ref, b_ref, ap_ref, bp_ref, cw_ref, cb_ref, g_ref, bb_ref, y_ref, ypre_ref, v_sc):
        i = pl.program_id(0)
        prev = ap_ref[...] * _sigmoid(bp_ref[...])
        v_sc[pl.ds(0, CONV_HALO), :] = jnp.where(i > 0, prev, 0.0)
        v_sc[pl.ds(CONV_HALO, tm), :] = a_ref[...] * _sigmoid(b_ref[...])
        acc = jnp.zeros((tm, D_CONV), F32)
        for j in range(CONV_WIDTH):
            acc = acc + cw_ref[pl.ds(j, 1), :] * v_sc[pl.ds(CONV_HALO - (CONV_WIDTH - 1) + j, tm), :]
        ypre = acc + cb_ref[...]
        ypre_ref[...] = ypre
        _, _, z = _conv_ln(ypre, g_ref[...], bb_ref[...])
        y_ref[...] = (z * _sigmoid(z)).astype(BF16)

    one = lambda i: (0, 0)
    return pl.pallas_call(
        body, name="conv_fwd",
        grid=(S // tm,),
        in_specs=[
            pl.BlockSpec((tm, D_CONV), lambda i: (i, 0)),
            pl.BlockSpec((tm, D_CONV), lambda i: (i, 1)),
            pl.BlockSpec((CONV_HALO, D_CONV), lambda i: (jnp.maximum(i * hb - 1, 0), 0)),
            pl.BlockSpec((CONV_HALO, D_CONV), lambda i: (jnp.maximum(i * hb - 1, 0), 1)),
            pl.BlockSpec((CONV_HALO, D_CONV), one),
            pl.BlockSpec((1, D_CONV), one),
            pl.BlockSpec((1, D_CONV), one),
            pl.BlockSpec((1, D_CONV), one),
        ],
        out_specs=[pl.BlockSpec((tm, D_CONV), lambda i: (i, 0)), pl.BlockSpec((tm, D_CONV), lambda i: (i, 0))],
        out_shape=[jax.ShapeDtypeStruct((S, D_CONV), BF16), jax.ShapeDtypeStruct((S, D_CONV), F32)],
        scratch_shapes=[pltpu.VMEM((tm + CONV_HALO, D_CONV), F32)],
        compiler_params=_params(1),
    )(proj, proj, proj, proj, cw, cb, ln_g, ln_b)


def conv_bwd(dyc, ypre, proj, cw, ln_g, ln_b, tm=CONV_TILE):
    S = ypre.shape[0]
    hb = tm // CONV_HALO
    nblk = S // tm
    last_halo = S // CONV_HALO - 1

    def dpre(dy, yp, g, bb):
        yn, rstd, z = _conv_ln(yp, g, bb)
        sg = _sigmoid(z)
        dz = dy * (sg * (1.0 + z * (1.0 - sg)))
        dyn = dz * g
        d = rstd * (dyn - jnp.mean(dyn, axis=-1, keepdims=True) - yn * jnp.mean(dyn * yn, axis=-1, keepdims=True))
        return d, dz * yn, dz

    def body(dy_ref, yp_ref, dyn_ref, ypn_ref, a_ref, b_ref, ap_ref, bp_ref, cw_ref, g_ref, bb_ref,
             du_ref, dcw_ref, dsm_ref, d_sc, v_sc):
        i = pl.program_id(0)
        g = g_ref[...]
        bb = bb_ref[...]
        d_main, dgn, dz = dpre(dy_ref[...], yp_ref[...], g, bb)
        d_next, _, _ = dpre(dyn_ref[...], ypn_ref[...], g, bb)
        d_sc[pl.ds(0, tm), :] = d_main
        d_sc[pl.ds(tm, CONV_HALO), :] = jnp.where(i < nblk - 1, d_next, 0.0)
        a = a_ref[...]
        sb = _sigmoid(b_ref[...])
        prev = ap_ref[...] * _sigmoid(bp_ref[...])
        v_sc[pl.ds(0, CONV_HALO), :] = jnp.where(i > 0, prev, 0.0)
        v_sc[pl.ds(CONV_HALO, tm), :] = a * sb

        @pl.when(i == 0)
        def _():
            dcw_ref[...] = jnp.zeros_like(dcw_ref)
            dsm_ref[...] = jnp.zeros_like(dsm_ref)

        dv = jnp.zeros((tm, D_CONV), F32)
        for j in range(CONV_WIDTH):
            dv = dv + cw_ref[pl.ds(j, 1), :] * d_sc[pl.ds(CONV_WIDTH - 1 - j, tm), :]
            shifted = v_sc[pl.ds(CONV_HALO - (CONV_WIDTH - 1) + j, tm), :]
            dcw_ref[pl.ds(j, 1), :] += jnp.sum(d_main * shifted, axis=0, keepdims=True)
        du_ref[:, pl.ds(0, D_CONV)] = dv * sb
        du_ref[:, pl.ds(D_CONV, D_CONV)] = dv * a * sb * (1.0 - sb)
        dsm_ref[pl.ds(0, 1), :] += jnp.sum(d_main, axis=0, keepdims=True)
        dsm_ref[pl.ds(1, 1), :] += jnp.sum(dgn, axis=0, keepdims=True)
        dsm_ref[pl.ds(2, 1), :] += jnp.sum(dz, axis=0, keepdims=True)

    one = lambda i: (0, 0)
    prev_map = lambda c: (lambda i: (jnp.maximum(i * hb - 1, 0), c))
    next_map = lambda i: (jnp.minimum((i + 1) * hb, last_halo), 0)
    return pl.pallas_call(
        body, name="conv_bwd",
        grid=(nblk,),
        in_specs=[
            pl.BlockSpec((tm, D_CONV), lambda i: (i, 0)),
            pl.BlockSpec((tm, D_CONV), lambda i: (i, 0)),
            pl.BlockSpec((CONV_HALO, D_CONV), next_map),
            pl.BlockSpec((CONV_HALO, D_CONV), next_map),
            pl.BlockSpec((tm, D_CONV), lambda i: (i, 0)),
            pl.BlockSpec((tm, D_CONV), lambda i: (i, 1)),
            pl.BlockSpec((CONV_HALO, D_CONV), prev_map(0)),
            pl.BlockSpec((CONV_HALO, D_CONV), prev_map(1)),
            pl.BlockSpec((CONV_HALO, D_CONV), one),
            pl.BlockSpec((1, D_CONV), one),
            pl.BlockSpec((1, D_CONV), one),
        ],
        out_specs=[
            pl.BlockSpec((tm, 2 * D_CONV), lambda i: (i, 0)),
            pl.BlockSpec((CONV_HALO, D_CONV), one),
            pl.BlockSpec((8, D_CONV), one),
        ],
        out_shape=[
            jax.ShapeDtypeStruct((S, 2 * D_CONV), F32),
            jax.ShapeDtypeStruct((CONV_HALO, D_CONV), F32),
            jax.ShapeDtypeStruct((8, D_CONV), F32),
        ],
        scratch_shapes=[pltpu.VMEM((tm + CONV_HALO, D_CONV), F32), pltpu.VMEM((tm + CONV_HALO, D_CONV), F32)],
        compiler_params=_params(1),
    )(dyc, ypre, dyc, ypre, proj, proj, proj, proj, cw, ln_g, ln_b)


def _split_bf16(v):
    hi = v.astype(BF16)
    return hi, (v - hi.astype(F32)).astype(BF16)


def _tri_dot(v, tri):
    hi, lo = _split_bf16(v)
    return _dot(hi, tri) + _dot(lo, tri)


def _sb_logs(z):
    sp = jnp.log(1.0 + jnp.exp(-jnp.abs(z)))
    return jnp.minimum(z, 0.0) - sp, -jnp.maximum(z, 0.0) - sp


def sb_fwd(q, k, v, T=SB_TILE):
    H, S, dh = q.shape

    def body(q_ref, k_ref, v_ref, o_ref, tot_ref, acc_sc, car_sc):
        qb = pl.program_id(1)
        qv = q_ref[...]
        row = lax.broadcasted_iota(jnp.int32, (T, T), 0)
        col = lax.broadcasted_iota(jnp.int32, (T, T), 1)
        after = jnp.where(row > col, 1.0, 0.0).astype(BF16)
        causal = col < row
        acc_sc[...] = jnp.zeros_like(acc_sc)
        car_sc[...] = jnp.zeros_like(car_sc)

        def tile(kb, masked):
            start = pl.multiple_of(kb * T, T)
            ks = k_ref[pl.ds(start, T), :]
            vs = v_ref[pl.ds(start, T), :]
            z = _dot_nt(qv, ks)
            a, b = _sb_logs(z)
            if masked:
                b = jnp.where(causal, b, 0.0)
            rem = _tri_dot(b, after)
            w = jnp.exp(a + rem + car_sc[...])
            if masked:
                w = jnp.where(causal, w, 0.0)
            acc_sc[...] += _dot(w.astype(BF16), vs)
            car_sc[...] += rem[:, 0:1] + b[:, 0:1]

        tile(qb, True)

        def step(t, c):
            tile(qb - 1 - t, False)
            return c

        lax.fori_loop(0, qb, step, 0)
        o_ref[...] = acc_sc[...]
        tot_ref[...] = car_sc[...]

    return pl.pallas_call(
        body, name="sb_fwd",
        grid=(H, S // T),
        in_specs=[
            pl.BlockSpec((None, T, dh), lambda h, i: (h, i, 0)),
            pl.BlockSpec((None, S, dh), lambda h, i: (h, 0, 0)),
            pl.BlockSpec((None, S, dh), lambda h, i: (h, 0, 0)),
        ],
        out_specs=[
            pl.BlockSpec((None, T, dh), lambda h, i: (h, i, 0)),
            pl.BlockSpec((None, T, 1), lambda h, i: (h, i, 0)),
        ],
        out_shape=[jax.ShapeDtypeStruct((H, S, dh), F32), jax.ShapeDtypeStruct((H, S, 1), F32)],
        scratch_shapes=[pltpu.VMEM((T, dh), F32), pltpu.VMEM((T, 1), F32)],
        compiler_params=_params(2),
    )(q, k, v)


def sb_bwd(q, k, v, do, tot, T=SB_TILE):
    H, S, dh = q.shape

    def body(q_ref, k_ref, v_ref, do_ref, tot_ref, dq_ref, dk_ref, dv_ref, acc_sc, pc_sc, gc_sc):
        qb = pl.program_id(1)
        qv = q_ref[...]
        dov = do_ref[...]
        row = lax.broadcasted_iota(jnp.int32, (T, T), 0)
        col = lax.broadcasted_iota(jnp.int32, (T, T), 1)
        upto = jnp.where(row <= col, 1.0, 0.0).astype(BF16)
        before = jnp.where(row < col, 1.0, 0.0).astype(BF16)
        causal = col < row
        acc_sc[...] = jnp.zeros_like(acc_sc)
        pc_sc[...] = tot_ref[...]
        gc_sc[...] = jnp.zeros_like(gc_sc)

        @pl.when(qb == 0)
        def _():
            dk_ref[...] = jnp.zeros_like(dk_ref)
            dv_ref[...] = jnp.zeros_like(dv_ref)

        def tile(kb, masked):
            start = pl.multiple_of(kb * T, T)
            ks = k_ref[pl.ds(start, T), :]
            vs = v_ref[pl.ds(start, T), :]
            z = _dot_nt(qv, ks)
            e = jnp.exp(-jnp.abs(z))
            sp = jnp.log(1.0 + e)
            a = jnp.minimum(z, 0.0) - sp
            b = -jnp.maximum(z, 0.0) - sp
            if masked:
                b = jnp.where(causal, b, 0.0)
            pin = _tri_dot(b, upto)
            w = jnp.exp(a + (pc_sc[...] - pin))
            if masked:
                w = jnp.where(causal, w, 0.0)
            pc_sc[...] -= pin[:, T - 1:T]
            g = _dot_nt(dov, vs) * w
            gex = _tri_dot(g, before) + gc_sc[...]
            gc_sc[...] = gex[:, T - 1:T] + g[:, T - 1:T]
            r = 1.0 / (1.0 + e)
            er = e * r
            pos = z >= 0.0
            dz = g * jnp.where(pos, er, r) - gex * jnp.where(pos, r, er)
            if masked:
                dz = jnp.where(causal, dz, 0.0)
            dzb = dz.astype(BF16)
            acc_sc[...] += _dot(dzb, ks)
            dk_ref[pl.ds(start, T), :] += _dot_tn(dzb, qv)
            dv_ref[pl.ds(start, T), :] += _dot_tn(w.astype(BF16), dov)

        def step(kb, c):
            tile(kb, False)
            return c

        lax.fori_loop(0, qb, step, 0)
        tile(qb, True)
        dq_ref[...] = acc_sc[...]

    return pl.pallas_call(
        body, name="sb_bwd",
        grid=(H, S // T),
        in_specs=[
            pl.BlockSpec((None, T, dh), lambda h, i: (h, i, 0)),
            pl.BlockSpec((None, S, dh), lambda h, i: (h, 0, 0)),
            pl.BlockSpec((None, S, dh), lambda h, i: (h, 0, 0)),
            pl.BlockSpec((None, T, dh), lambda h, i: (h, i, 0)),
            pl.BlockSpec((None, T, 1), lambda h, i: (h, i, 0)),
        ],
        out_specs=[
            pl.BlockSpec((None, T, dh), lambda h, i: (h, i, 0)),
            pl.BlockSpec((None, S, dh), lambda h, i: (h, 0, 0)),
            pl.BlockSpec((None, S, dh), lambda h, i: (h, 0, 0)),
        ],
        out_shape=[jax.ShapeDtypeStruct((H, S, dh), F32)] * 3,
        scratch_shapes=[pltpu.VMEM((T, dh), F32), pltpu.VMEM((T, 1), F32), pltpu.VMEM((T, 1), F32)],
        compiler_params=_params(2),
    )(q, k, v, do, tot)


def _ret_tables(T=RET_TILE):
    hh = jnp.arange(N_RET_HEADS, dtype=F32)
    log_gamma = jnp.log1p(-jnp.exp2(-5.0 - hh))
    idx = jnp.arange(T, dtype=F32)
    diff = idx[:, None] - idx[None, :]
    ci = (jnp.arange(T) // 64)
    same = ci[:, None] == ci[None, :]
    earlier = ci[None, :] < ci[:, None]
    dist = jnp.where(same, jnp.abs(diff), diff)
    dmat = jnp.where(same | earlier, jnp.exp(log_gamma[:, None, None] * dist[None]), 0.0)
    ones = jnp.ones((1, 1, HEAD_DIM), F32)
    qdec = jnp.exp(log_gamma[:, None] * (idx + 1.0)[None, :])[:, :, None] * ones
    kdec = jnp.exp(log_gamma[:, None] * (T - 1.0 - idx)[None, :])[:, :, None] * ones
    bdec = jnp.exp(log_gamma * T)[:, None, None] * jnp.ones((1, HEAD_DIM, HEAD_DIM), F32)
    return dmat, qdec, kdec, bdec


def _rope_tables(S):
    half = HEAD_DIM // 2
    inv = 1.0 / (ROPE_BASE ** (jnp.arange(half, dtype=F32) / half))
    ang = jnp.arange(S).astype(F32)[:, None] * inv[None, :]
    c = jnp.cos(ang)
    s = jnp.sin(ang)
    cos = jnp.tile(jnp.concatenate([c, c], axis=1), (1, N_RET_HEADS))
    sin = jnp.tile(jnp.concatenate([-s, s], axis=1), (1, N_RET_HEADS))
    return cos, sin


def _swap_halves(x):
    n = x.shape[1]
    lane = lax.broadcasted_iota(jnp.int32, x.shape, 1)
    first = (lane % HEAD_DIM) < (HEAD_DIM // 2)
    return jnp.where(first, pltpu.roll(x, n - HEAD_DIM // 2, 1), pltpu.roll(x, HEAD_DIM // 2, 1))


def rope_fwd(proj, cos, sin, tm=ROW_TILE):
    S = proj.shape[0]

    def body(q_ref, k_ref, c_ref, s_ref, qo_ref, ko_ref):
        c = c_ref[...]
        s = s_ref[...]
        qv = q_ref[...]
        kv = k_ref[...]
        qo_ref[...] = ((qv * c + _swap_halves(qv) * s) * 0.125).astype(BF16)
        ko_ref[...] = (kv * c + _swap_halves(kv) * s).astype(BF16)

    row = lambda i: (i, 0)
    return pl.pallas_call(
        body, name="rope_fwd",
        grid=(S // tm,),
        in_specs=[
            pl.BlockSpec((tm, D_RET), lambda i: (i, 8)),
            pl.BlockSpec((tm, D_RET), lambda i: (i, 9)),
            pl.BlockSpec((tm, D_RET), row),
            pl.BlockSpec((tm, D_RET), row),
        ],
        out_specs=[pl.BlockSpec((tm, D_RET), row), pl.BlockSpec((tm, D_RET), row)],
        out_shape=[jax.ShapeDtypeStruct((S, D_RET), BF16)] * 2,
        compiler_params=_params(1),
    )(proj, proj, cos, sin)


def rope_bwd(dq, dk, cos, sin, tm=ROW_TILE):
    S = dq.shape[0]

    def body(dq_ref, dk_ref, c_ref, s_ref, qo_ref, ko_ref):
        c = c_ref[...]
        s = s_ref[...]
        dqv = dq_ref[...] * 0.125
        dkv = dk_ref[...]
        qo_ref[...] = dqv * c - _swap_halves(dqv) * s
        ko_ref[...] = dkv * c - _swap_halves(dkv) * s

    row = lambda i: (i, 0)
    return pl.pallas_call(
        body, name="rope_bwd",
        grid=(S // tm,),
        in_specs=[pl.BlockSpec((tm, D_RET), row)] * 4,
        out_specs=[pl.BlockSpec((tm, D_RET), row)] * 2,
        out_shape=[jax.ShapeDtypeStruct((S, D_RET), F32)] * 2,
        compiler_params=_params(1),
    )(dq, dk, cos, sin)


def ret_fwd(q, k, v, gate, ng, tables, T=RET_TILE):
    H, S, dh = q.shape
    dmat, qdec, kdec, bdec = tables

    def body(q_ref, k_ref, v_ref, gt_ref, ng_ref, dm_ref, qd_ref, kd_ref, bd_ref, o_ref, y_ref, st_ref, s_sc):
        n = pl.program_id(1)

        @pl.when(n == 0)
        def _():
            s_sc[...] = jnp.zeros_like(s_sc)

        qv = q_ref[...]
        kv = k_ref[...]
        vv = v_ref[...]
        state = s_sc[...]
        st_ref[...] = state
        sc = (_dot_nt(qv, kv) * dm_ref[...]).astype(BF16)
        qd = (qv.astype(F32) * qd_ref[...]).astype(BF16)
        y = _dot(sc, vv) + _dot(qd, state.astype(BF16))
        y_ref[...] = y
        kd = (kv.astype(F32) * kd_ref[...]).astype(BF16)
        s_sc[...] = bd_ref[...] * state + _dot_tn(kd, vv)
        mu = jnp.mean(y, axis=-1, keepdims=True)
        yc = y - mu
        yn = yc * lax.rsqrt(jnp.mean(yc * yc, axis=-1, keepdims=True) + EPS)
        gt = gt_ref[...]
        o_ref[...] = gt * _sigmoid(gt) * (yn * ng_ref[...])

    blk = lambda h, n: (h, n, 0)
    head = lambda h, n: (h, 0, 0)
    return pl.pallas_call(
        body, name="ret_fwd",
        grid=(H, S // T),
        in_specs=[
            pl.BlockSpec((None, T, dh), blk),
            pl.BlockSpec((None, T, dh), blk),
            pl.BlockSpec((None, T, dh), blk),
            pl.BlockSpec((None, T, dh), blk),
            pl.BlockSpec((None, 1, dh), head),
            pl.BlockSpec((None, T, T), head),
            pl.BlockSpec((None, T, dh), head),
            pl.BlockSpec((None, T, dh), head),
            pl.BlockSpec((None, dh, dh), head),
        ],
        out_specs=[
            pl.BlockSpec((None, T, dh), blk),
            pl.BlockSpec((None, T, dh), blk),
            pl.BlockSpec((None, None, dh, dh), lambda h, n: (h, n, 0, 0)),
        ],
        out_shape=[
            jax.ShapeDtypeStruct((H, S, dh), F32),
            jax.ShapeDtypeStruct((H, S, dh), F32),
            jax.ShapeDtypeStruct((H, S // T, dh, dh), F32),
        ],
        scratch_shapes=[pltpu.VMEM((dh, dh), F32)],
        compiler_params=_params(2),
    )(q, k, v, gate, ng, dmat, qdec, kdec, bdec)


def ret_bwd(do, q, k, v, gate, ng, y, states, tables, T=RET_TILE):
    H, S, dh = q.shape
    nb = S // T
    dmat, qdec, kdec, bdec = tables

    def body(do_ref, q_ref, k_ref, v_ref, gt_ref, ng_ref, y_ref, st_ref, dm_ref, qd_ref, kd_ref, bd_ref,
             dq_ref, dk_ref, dv_ref, dgt_ref, dng_ref, u_sc):
        n = pl.program_id(1)

        @pl.when(n == 0)
        def _():
            u_sc[...] = jnp.zeros_like(u_sc)
            dng_ref[...] = jnp.zeros_like(dng_ref)

        yv = y_ref[...]
        mu = jnp.mean(yv, axis=-1, keepdims=True)
        yc = yv - mu
        rstd = lax.rsqrt(jnp.mean(yc * yc, axis=-1, keepdims=True) + EPS)
        yn = yc * rstd
        gt = gt_ref[...]
        sg = _sigmoid(gt)
        ngv = ng_ref[...]
        dout = do_ref[...]
        dgt_ref[...] = dout * (yn * ngv) * (sg * (1.0 + gt * (1.0 - sg)))
        dn = dout * (gt * sg)
        dng_ref[...] += jnp.sum(dn * yn, axis=0, keepdims=True)
        dyn = dn * ngv
        dy = rstd * (dyn - jnp.mean(dyn, axis=-1, keepdims=True) - yn * jnp.mean(dyn * yn, axis=-1, keepdims=True))
        dyb = dy.astype(BF16)

        qv = q_ref[...]
        kv = k_ref[...]
        vv = v_ref[...]
        dm = dm_ref[...]
        qdt = qd_ref[...]
        kdt = kd_ref[...]
        sb = st_ref[...].astype(BF16)
        u = u_sc[...]
        ub = u.astype(BF16)
        dqk = (_dot_nt(dyb, vv) * dm).astype(BF16)
        sc = (_dot_nt(qv, kv) * dm).astype(BF16)
        qd = (qv.astype(F32) * qdt).astype(BF16)
        kd = (kv.astype(F32) * kdt).astype(BF16)
        dq_ref[...] = _dot(dqk, kv) + qdt * _dot_nt(dyb, sb)
        dk_ref[...] = _dot_tn(dqk, qv) + kdt * _dot_nt(vv, ub)
        dv_ref[...] = _dot_tn(sc, dyb) + _dot(kd, ub)
        u_sc[...] = bd_ref[...] * u + _dot_tn(qd, dyb)

    blk = lambda h, n: (h, nb - 1 - n, 0)
    head = lambda h, n: (h, 0, 0)
    return pl.pallas_call(
        body, name="ret_bwd",
        grid=(H, nb),
        in_specs=[
            pl.BlockSpec((None, T, dh), blk),
            pl.BlockSpec((None, T, dh), blk),
            pl.BlockSpec((None, T, dh), blk),
            pl.BlockSpec((None, T, dh), blk),
            pl.BlockSpec((None, T, dh), blk),
            pl.BlockSpec((None, 1, dh), head),
            pl.BlockSpec((None, T, dh), blk),
            pl.BlockSpec((None, None, dh, dh), lambda h, n: (h, nb - 1 - n, 0, 0)),
            pl.BlockSpec((None, T, T), head),
            pl.BlockSpec((None, T, dh), head),
            pl.BlockSpec((None, T, dh), head),
            pl.BlockSpec((None, dh, dh), head),
        ],
        out_specs=[
            pl.BlockSpec((None, T, dh), blk),
            pl.BlockSpec((None, T, dh), blk),
            pl.BlockSpec((None, T, dh), blk),
            pl.BlockSpec((None, T, dh), blk),
            pl.BlockSpec((None, 1, dh), head),
        ],
        out_shape=[jax.ShapeDtypeStruct((H, S, dh), F32)] * 4 + [jax.ShapeDtypeStruct((H, 1, dh), F32)],
        scratch_shapes=[pltpu.VMEM((dh, dh), F32)],
        compiler_params=_params(2),
    )(do, q, k, v, gate, ng, y, states, dmat, qdec, kdec, bdec)


def loss_head(x, g, target, tm=ROW_TILE):
    S = x.shape[0]

    def body(x_ref, g_ref, t_ref, loss_ref, dx_ref, dg_ref):
        i = pl.program_id(0)
        xv = x_ref[...]
        gv = g_ref[...]
        _, xhat = _rms_stats(xv)
        err = xhat * gv - t_ref[...]
        part = 0.5 * jnp.sum(jnp.mean(err * err, axis=-1, keepdims=True), axis=0, keepdims=True)
        dx, _, dg = _rms_bwd(xv, gv, err * (1.0 / D_MODEL))
        dx_ref[...] = dx
        part = jnp.broadcast_to(part, (1, 128))

        @pl.when(i == 0)
        def _():
            loss_ref[...] = part
            dg_ref[...] = dg

        @pl.when(i > 0)
        def _():
            loss_ref[...] += part
            dg_ref[...] += dg

    row = lambda i: (i, 0)
    one = lambda i: (0, 0)
    return pl.pallas_call(
        body, name="loss_head",
        grid=(S // tm,),
        in_specs=[pl.BlockSpec((tm, D_MODEL), row), pl.BlockSpec((1, D_MODEL), one), pl.BlockSpec((tm, D_MODEL), row)],
        out_specs=[pl.BlockSpec((1, 128), one), pl.BlockSpec((tm, D_MODEL), row), pl.BlockSpec((1, D_MODEL), one)],
        out_shape=[
            jax.ShapeDtypeStruct((1, 128), F32),
            jax.ShapeDtypeStruct((S, D_MODEL), F32),
            jax.ShapeDtypeStruct((1, D_MODEL), F32),
        ],
        compiler_params=_params(1),
    )(x, g, target)


def adamw(parts, w, m, v, tr):
    R, C = w.shape
    c1 = 1.0 / (1.0 - ADAM_B1 ** ADAM_STEP)
    c2 = 1.0 / (1.0 - ADAM_B2 ** ADAM_STEP)

    def body(p_ref, w_ref, m_ref, v_ref, g_ref, d_ref, mo_ref, vo_ref):
        g = p_ref[0]
        for d in range(1, N_DEV):
            g = g + p_ref[d]
        mn = ADAM_B1 * m_ref[...] + (1.0 - ADAM_B1) * g
        vn = ADAM_B2 * v_ref[...] + (1.0 - ADAM_B2) * (g * g)
        g_ref[...] = g
        mo_ref[...] = mn
        vo_ref[...] = vn
        d_ref[...] = -ADAM_LR * ((mn * c1) / (jnp.sqrt(vn * c2) + ADAM_EPS) + ADAM_WD * w_ref[...])

    row = lambda i: (i, 0)
    return pl.pallas_call(
        body, name="adamw",
        grid=(R // tr,),
        in_specs=[pl.BlockSpec((N_DEV, tr, C), lambda i: (0, i, 0))] + [pl.BlockSpec((tr, C), row)] * 3,
        out_specs=[pl.BlockSpec((tr, C), row)] * 4,
        out_shape=[jax.ShapeDtypeStruct((R, C), F32)] * 4,
        compiler_params=_params(1),
    )(parts, w, m, v)


def _my_id():
    return lax.axis_index("x") * 4 + lax.axis_index("y") * 2 + lax.axis_index("c")


def _peer(k):
    x, y, c = lax.axis_index("x"), lax.axis_index("y"), lax.axis_index("c")
    px = 1 - x if k & 4 else x
    py = 1 - y if k & 2 else y
    pc = 1 - c if k & 1 else c
    return (px, py, pc), px * 4 + py * 2 + pc


def all_gather(shards):
    n = len(shards)

    def body(*refs):
        ins, outs = refs[:n], refs[n:2 * n]
        send_sems, recv_sems, local_sems = refs[2 * n:]
        me = _my_id()
        local = [pltpu.make_async_copy(ins[t], outs[t].at[me], local_sems.at[t]) for t in range(n)]
        for cp in local:
            cp.start()
        sends = []
        for k in range(1, N_DEV):
            dev, _ = _peer(k)
            for t in range(n):
                cp = pltpu.make_async_remote_copy(
                    src_ref=ins[t], dst_ref=outs[t].at[me],
                    send_sem=send_sems.at[t, k - 1], recv_sem=recv_sems.at[t, k - 1],
                    device_id=dev, device_id_type=pl.DeviceIdType.MESH)
                cp.start()
                sends.append(cp)
        for k in range(1, N_DEV):
            dev, pid = _peer(k)
            for t in range(n):
                pltpu.make_async_remote_copy(
                    src_ref=ins[t], dst_ref=outs[t].at[pid],
                    send_sem=send_sems.at[t, k - 1], recv_sem=recv_sems.at[t, k - 1],
                    device_id=dev, device_id_type=pl.DeviceIdType.MESH).wait_recv()
        for cp in sends:
            cp.wait_send()
        for cp in local:
            cp.wait()

    any_spec = pl.BlockSpec(memory_space=pl.ANY)
    return pl.pallas_call(
        body, name="all_gather",
        in_specs=[any_spec] * n,
        out_specs=[any_spec] * n,
        out_shape=[jax.ShapeDtypeStruct((N_DEV,) + s.shape, s.dtype) for s in shards],
        scratch_shapes=[
            pltpu.SemaphoreType.DMA((n, N_DEV - 1)),
            pltpu.SemaphoreType.DMA((n, N_DEV - 1)),
            pltpu.SemaphoreType.DMA((n,)),
        ],
    )(*shards)


def exchange(parts):
    n = len(parts)

    def body(*refs):
        ins, outs = refs[:n], refs[n:2 * n]
        send_sems, recv_sems, local_sems = refs[2 * n:]
        me = _my_id()
        local = [pltpu.make_async_copy(ins[t].at[me], outs[t].at[me], local_sems.at[t]) for t in range(n)]
        for cp in local:
            cp.start()
        sends = []
        for k in range(1, N_DEV):
            dev, pid = _peer(k)
            for t in range(n):
                cp = pltpu.make_async_remote_copy(
                    src_ref=ins[t].at[pid], dst_ref=outs[t].at[me],
                    send_sem=send_sems.at[t, k - 1], recv_sem=recv_sems.at[t, k - 1],
                    device_id=dev, device_id_type=pl.DeviceIdType.MESH)
                cp.start()
                sends.append(cp)
        for k in range(1, N_DEV):
            dev, pid = _peer(k)
            for t in range(n):
                pltpu.make_async_remote_copy(
                    src_ref=ins[t].at[me], dst_ref=outs[t].at[pid],
                    send_sem=send_sems.at[t, k - 1], recv_sem=recv_sems.at[t, k - 1],
                    device_id=dev, device_id_type=pl.DeviceIdType.MESH).wait_recv()
        for cp in sends:
            cp.wait_send()
        for cp in local:
            cp.wait()

    any_spec = pl.BlockSpec(memory_space=pl.ANY)
    return pl.pallas_call(
        body, name="exchange",
        in_specs=[any_spec] * n,
        out_specs=[any_spec] * n,
        out_shape=[jax.ShapeDtypeStruct(p.shape, p.dtype) for p in parts],
        scratch_shapes=[
            pltpu.SemaphoreType.DMA((n, N_DEV - 1)),
            pltpu.SemaphoreType.DMA((n, N_DEV - 1)),
            pltpu.SemaphoreType.DMA((n,)),
        ],
    )(*parts)


def _heads(t, n_heads):
    S = t.shape[0]
    return t.reshape(S, n_heads, HEAD_DIM).transpose(1, 0, 2)


def _unheads(t):
    H, S, _ = t.shape
    return t.transpose(1, 0, 2).reshape(S, H * HEAD_DIM)


def _row(v):
    return v.reshape(1, -1)


def _pad_taps(cw):
    return jnp.concatenate([cw, jnp.zeros((CONV_HALO - CONV_WIDTH, D_CONV), F32)], axis=0)


def local_step(x, target, W):
    S = x.shape[0]
    cos, sin = _rope_tables(S)
    tables = _ret_tables()
    saved = []
    for l in range(DEPTH):
        sv = {"x0": x}
        x, sv["gate1"], sv["up1"] = ffn_fwd(x, _row(W["ffn1_norm"][l]), W["ffn1_w_in"][l], W["ffn1_w_out"][l])
        sv["x1"] = x
        proj = mix_in_fwd(x, _row(W["mix_norm"][l]), W["mix_w_in"][l])
        sv["proj"] = proj
        cw = _pad_taps(W["conv_w"][l])
        y_conv, sv["ypre"] = conv_fwd(proj, cw, _row(W["conv_b"][l]), _row(W["conv_ln_g"][l]), _row(W["conv_ln_b"][l]))
        sv["q_sb"] = _heads((proj[:, 512:1024] * 0.125).astype(BF16), N_SB_HEADS)
        sv["k_sb"] = _heads(proj[:, 1024:1536].astype(BF16), N_SB_HEADS)
        sv["v_sb"] = _heads(proj[:, 1536:2048].astype(BF16), N_SB_HEADS)
        o_sb, sv["tot"] = sb_fwd(sv["q_sb"], sv["k_sb"], sv["v_sb"])
        q_rot, k_rot = rope_fwd(proj, cos, sin)
        sv["q_r"] = _heads(q_rot, N_RET_HEADS)
        sv["k_r"] = _heads(k_rot, N_RET_HEADS)
        sv["v_r"] = _heads(proj[:, 2560:2816].astype(BF16), N_RET_HEADS)
        sv["g_r"] = _heads(proj[:, 2816:3072], N_RET_HEADS)
        ng = W["ret_norm_g"][l].reshape(N_RET_HEADS, 1, HEAD_DIM)
        o_r, sv["y_r"], sv["states"] = ret_fwd(sv["q_r"], sv["k_r"], sv["v_r"], sv["g_r"], ng, tables)
        ycat = jnp.concatenate([y_conv, _unheads(o_sb).astype(BF16), _unheads(o_r).astype(BF16)], axis=1)
        sv["ycat"] = ycat
        x = mix_out_fwd(ycat, W["mix_w_out"][l], x)
        sv["x2"] = x
        x, sv["gate2"], sv["up2"] = ffn_fwd(x, _row(W["ffn2_norm"][l]), W["ffn2_w_in"][l], W["ffn2_w_out"][l])
        saved.append(sv)

    loss_acc, dx, dg_final = loss_head(x, _row(W["final_norm"]), target)
    grads = {"final_norm": dg_final.reshape(D_MODEL)}
    per_layer = []

    def ffn_back(dx, x_in, gate, up, norm, w_in, w_out):
        dx, h, dyh, dgate, dup, hid, dg = ffn_bwd(dx, x_in, _row(norm), gate, up, w_in, w_out)
        d_in = jnp.concatenate([
            matmul_tn(h, dgate, D_MODEL, FF_TILE, name="ffn_dw_gate"),
            matmul_tn(h, dup, D_MODEL, FF_TILE, name="ffn_dw_up")], axis=1)
        d_out = matmul_tn(hid, dyh, FF_TILE, D_MODEL, name="ffn_dw_out")
        return dx, dg.reshape(D_MODEL), d_in, d_out

    for l in reversed(range(DEPTH)):
        sv = saved[l]
        gl = {}
        dx, gl["ffn2_norm"], gl["ffn2_w_in"], gl["ffn2_w_out"] = ffn_back(
            dx, sv["x2"], sv["gate2"], sv["up2"], W["ffn2_norm"][l], W["ffn2_w_in"][l], W["ffn2_w_out"][l])
        dycat, dxb = mix_out_bwd(dx, W["mix_w_out"][l])
        gl["mix_w_out"] = matmul_tn(sv["ycat"], dxb, D_MODEL, D_MODEL, name="mix_dw_out")
        cw = _pad_taps(W["conv_w"][l])
        du_conv, dcw, dsm = conv_bwd(dycat, sv["ypre"], sv["proj"], cw, _row(W["conv_ln_g"][l]), _row(W["conv_ln_b"][l]))
        gl["conv_w"] = dcw[:CONV_WIDTH]
        gl["conv_b"], gl["conv_ln_g"], gl["conv_ln_b"] = dsm[0], dsm[1], dsm[2]
        do_sb = _heads(dycat[:, 256:768].astype(BF16), N_SB_HEADS)
        dq_sb, dk_sb, dv_sb = sb_bwd(sv["q_sb"], sv["k_sb"], sv["v_sb"], do_sb, sv["tot"])
        do_r = _heads(dycat[:, 768:1024], N_RET_HEADS)
        ng = W["ret_norm_g"][l].reshape(N_RET_HEADS, 1, HEAD_DIM)
        dq_r, dk_r, dv_r, dg_r, dng = ret_bwd(do_r, sv["q_r"], sv["k_r"], sv["v_r"], sv["g_r"], ng, sv["y_r"],
                                              sv["states"], tables)
        gl["ret_norm_g"] = dng.reshape(D_RET)
        dq_rr, dk_rr = rope_bwd(_unheads(dq_r), _unheads(dk_r), cos, sin)
        dproj = jnp.concatenate([
            du_conv.astype(BF16),
            (_unheads(dq_sb) * 0.125).astype(BF16), _unheads(dk_sb).astype(BF16), _unheads(dv_sb).astype(BF16),
            dq_rr.astype(BF16), dk_rr.astype(BF16), _unheads(dv_r).astype(BF16), _unheads(dg_r).astype(BF16)], axis=1)
        dx, h, dg = mix_in_bwd(dproj, W["mix_w_in"][l], sv["x1"], _row(W["mix_norm"][l]), dx)
        gl["mix_norm"] = dg.reshape(D_MODEL)
        gl["mix_w_in"] = matmul_tn(h, dproj, D_MODEL, D_MODEL, name="mix_dw_in")
        dx, gl["ffn1_norm"], gl["ffn1_w_in"], gl["ffn1_w_out"] = ffn_back(
            dx, sv["x0"], sv["gate1"], sv["up1"], W["ffn1_norm"][l], W["ffn1_w_in"][l], W["ffn1_w_out"][l])
        per_layer.append(gl)

    per_layer = per_layer[::-1]
    for name in per_layer[0]:
        grads[name] = jnp.stack([per_layer[l][name] for l in range(DEPTH)], axis=0)
    return loss_acc[0, 0], dx, grads


COL_SHARDED = ("ffn1_w_in", "mix_w_in", "ffn2_w_in")
ROW_SHARDED = ("ffn1_w_out", "mix_w_out", "ffn2_w_out")
SMALL = ("ffn1_norm", "mix_norm", "conv_b", "conv_ln_g", "conv_ln_b", "ret_norm_g", "ffn2_norm", "final_norm")
WEIGHTS = ("ffn1_norm", "ffn1_w_in", "ffn1_w_out", "mix_norm", "mix_w_in", "conv_w", "conv_b", "conv_ln_g",
           "conv_ln_b", "ret_norm_g", "mix_w_out", "ffn2_norm", "ffn2_w_in", "ffn2_w_out", "final_norm")
SMALL_ROWS = 32


def _gather_weights(w):
    big = COL_SHARDED + ROW_SHARDED
    got = all_gather([w[n].astype(BF16) for n in big] + [w["conv_w"]])
    full = {}
    for n, g in zip(big, got[:-1]):
        if n in COL_SHARDED:
            full[n] = g.transpose(1, 2, 0, 3).reshape(DEPTH, D_MODEL, -1)
        else:
            full[n] = g.transpose(1, 0, 2, 3).reshape(DEPTH, -1, D_MODEL)
    full["conv_w"] = got[-1].transpose(1, 2, 0, 3).reshape(DEPTH, CONV_WIDTH, D_CONV)
    for n in SMALL:
        full[n] = w[n]
    return full


def _pack_small(g):
    flat = jnp.concatenate([g[n].reshape(-1) for n in SMALL] + [g["conv_w"].reshape(-1)])
    flat = jnp.concatenate([flat, jnp.zeros((SMALL_ROWS * D_MODEL - flat.shape[0],), F32)])
    return flat.reshape(SMALL_ROWS, D_MODEL)


def _unpack_small(buf, like):
    flat = buf.reshape(-1)
    out, off = {}, 0
    for n in SMALL:
        size = int(np.prod(like[n].shape))
        out[n] = flat[off:off + size].reshape(like[n].shape)
        off += size
    size = DEPTH * CONV_WIDTH * D_CONV
    out["conv_w"] = flat[off:off + size].reshape(DEPTH, CONV_WIDTH, D_CONV)
    return out


def kernel(x, ffn1_norm, ffn1_w_in, ffn1_w_out, mix_norm, mix_w_in, conv_w, conv_b, conv_ln_g, conv_ln_b, ret_norm_g, mix_w_out, ffn2_norm, ffn2_w_in, ffn2_w_out, final_norm, loss_target, m_ffn1_norm, m_ffn1_w_in, m_ffn1_w_out, m_mix_norm, m_mix_w_in, m_conv_w, m_conv_b, m_conv_ln_g, m_conv_ln_b, m_ret_norm_g, m_mix_w_out, m_ffn2_norm, m_ffn2_w_in, m_ffn2_w_out, m_final_norm, v_ffn1_norm, v_ffn1_w_in, v_ffn1_w_out, v_mix_norm, v_mix_w_in, v_conv_w, v_conv_b, v_conv_ln_g, v_conv_ln_b, v_ret_norm_g, v_mix_w_out, v_ffn2_norm, v_ffn2_w_in, v_ffn2_w_out, v_final_norm):
    args = locals()
    w = {n: args[n] for n in WEIGHTS}
    m = {n: args["m_" + n] for n in WEIGHTS}
    v = {n: args["v_" + n] for n in WEIGHTS}
    me = _my_id()

    full = _gather_weights(w)
    loss, grad_x, g = local_step(x[0], loss_target[0], full)
    loss = lax.psum(loss, ("x", "y", "c"))

    parts, shapes = [], []
    for n in COL_SHARDED + ROW_SHARDED:
        if n in COL_SHARDED:
            p = g[n].reshape(DEPTH, D_MODEL, N_DEV, -1).transpose(2, 0, 1, 3)
        else:
            p = g[n].reshape(DEPTH, N_DEV, -1, D_MODEL).transpose(1, 0, 2, 3)
        shapes.append(p.shape[1:])
        parts.append(p.reshape(N_DEV, -1, p.shape[-1]))
    small = _pack_small(g)
    parts.append(jnp.broadcast_to(small[None], (N_DEV, SMALL_ROWS, D_MODEL)))
    got = exchange(parts)

    grad, delta, new_m, new_v = {}, {}, {}, {}
    for n, p, shp in zip(COL_SHARDED + ROW_SHARDED, got[:-1], shapes):
        r, c = p.shape[1:]
        outs = adamw(p, w[n].reshape(r, c), m[n].reshape(r, c), v[n].reshape(r, c), tr=r // 8)
        grad[n], delta[n], new_m[n], new_v[n] = [o.reshape(shp) for o in outs]

    def small_pack(d):
        mine = dict(d)
        cwf = jnp.zeros((DEPTH, CONV_WIDTH, D_CONV), F32)
        mine["conv_w"] = lax.dynamic_update_slice(cwf, d["conv_w"], (0, 0, me * (D_CONV // N_DEV)))
        return _pack_small(mine)

    outs = adamw(got[-1], small_pack(w), small_pack(m), small_pack(v), tr=SMALL_ROWS)
    for dst, o in zip((grad, delta, new_m, new_v), outs):
        un = _unpack_small(o, w)
        un["conv_w"] = lax.dynamic_slice(un["conv_w"], (0, 0, me * (D_CONV // N_DEV)),
                                         (DEPTH, CONV_WIDTH, D_CONV // N_DEV))
        dst.update(un)

    return (loss, grad_x[None], *[grad[n] for n in WEIGHTS], *[delta[n] for n in WEIGHTS],
            *[new_m[n] for n in WEIGHTS], *[new_v[n] for n in WEIGHTS])
```

```python
import functools

import numpy as np
import jax
import jax.numpy as jnp
from jax import lax
from jax.experimental import pallas as pl
from jax.experimental.pallas import tpu as pltpu

F32 = jnp.float32
BF16 = jnp.bfloat16

D_MODEL = 1024
DEPTH = 2
D_FF = 2816
D_CONV = 256
CONV_WIDTH = 31
CONV_HALO = 32
D_SB = 512
N_SB_HEADS = 8
D_RET = 256
N_RET_HEADS = 4
HEAD_DIM = 64
D_IN_PROJ = 3072
ROPE_BASE = 10000.0
EPS = 1e-6
N_DEV = 8

ADAM_LR = 0.001
ADAM_B1 = 0.9
ADAM_B2 = 0.999
ADAM_EPS = 1e-08
ADAM_WD = 0.01
ADAM_STEP = 10

VMEM_LIMIT = 56 * 1024 * 1024
ROW_TILE = 512
FF_TILE = 1408
SB_TILE = 256
RET_TILE = 256
CONV_TILE = 256

NT_DIMS = (((1,), (1,)), ((), ()))
TN_DIMS = (((0,), (0,)), ((), ()))


def _params(n_axes, vmem=VMEM_LIMIT):
    return pltpu.CompilerParams(dimension_semantics=("arbitrary",) * n_axes, vmem_limit_bytes=vmem)


def _dot(a, b):
    return jnp.dot(a, b, preferred_element_type=F32)


def _dot_nt(a, b):
    return lax.dot_general(a, b, NT_DIMS, preferred_element_type=F32)


def _dot_tn(a, b):
    return lax.dot_general(a, b, TN_DIMS, preferred_element_type=F32)


def _sigmoid(z):
    return 1.0 / (1.0 + jnp.exp(-z))


def _rms_stats(xv):
    r = lax.rsqrt(jnp.mean(xv * xv, axis=-1, keepdims=True) + EPS)
    return r, xv * r


def _rms_bwd(xv, g, dh):
    r, xhat = _rms_stats(xv)
    dxhat = dh * g
    dx = r * (dxhat - xhat * jnp.mean(dxhat * xhat, axis=-1, keepdims=True))
    dg = jnp.sum(dh * xhat, axis=0, keepdims=True)
    return dx, (xhat * g).astype(BF16), dg


def ffn_fwd(x, g, w_in, w_out, tm=ROW_TILE):
    S = x.shape[0]
    nj = D_FF // FF_TILE

    def body(x_ref, g_ref, wg_ref, wu_ref, wo_ref, y_ref, gate_ref, up_ref, h_sc, acc_sc):
        j = pl.program_id(1)

        @pl.when(j == 0)
        def _():
            _, xhat = _rms_stats(x_ref[...])
            h_sc[...] = (xhat * g_ref[...]).astype(BF16)
            acc_sc[...] = jnp.zeros_like(acc_sc)

        h = h_sc[...]
        gt = _dot(h, wg_ref[...])
        up = _dot(h, wu_ref[...])
        gate_ref[...] = gt.astype(BF16)
        up_ref[...] = up.astype(BF16)
        hid = (gt * _sigmoid(gt) * up).astype(BF16)
        acc_sc[...] += _dot(hid, wo_ref[...])

        @pl.when(j == nj - 1)
        def _():
            y_ref[...] = x_ref[...] + 0.5 * acc_sc[...]

    return pl.pallas_call(
        body, name="ffn_fwd",
        grid=(S // tm, nj),
        in_specs=[
            pl.BlockSpec((tm, D_MODEL), lambda i, j: (i, 0)),
            pl.BlockSpec((1, D_MODEL), lambda i, j: (0, 0)),
            pl.BlockSpec((D_MODEL, FF_TILE), lambda i, j: (0, j)),
            pl.BlockSpec((D_MODEL, FF_TILE), lambda i, j: (0, j + nj)),
            pl.BlockSpec((FF_TILE, D_MODEL), lambda i, j: (j, 0)),
        ],
        out_specs=[
            pl.BlockSpec((tm, D_MODEL), lambda i, j: (i, 0)),
            pl.BlockSpec((tm, FF_TILE), lambda i, j: (i, j)),
            pl.BlockSpec((tm, FF_TILE), lambda i, j: (i, j)),
        ],
        out_shape=[
            jax.ShapeDtypeStruct((S, D_MODEL), F32),
            jax.ShapeDtypeStruct((S, D_FF), BF16),
            jax.ShapeDtypeStruct((S, D_FF), BF16),
        ],
        scratch_shapes=[pltpu.VMEM((tm, D_MODEL), BF16), pltpu.VMEM((tm, D_MODEL), F32)],
        compiler_params=_params(2),
    )(x, g, w_in, w_in, w_out)


def ffn_bwd(dy, x, g, gate, up, w_in, w_out, tm=ROW_TILE // 2):
    S = x.shape[0]
    nj = D_FF // FF_TILE

    def body(dy_ref, x_ref, g_ref, gate_ref, up_ref, wg_ref, wu_ref, wo_ref,
             dx_ref, h_ref, dyh_ref, dgate_ref, dup_ref, hid_ref, dg_ref, d2_sc, dh_sc):
        i = pl.program_id(0)
        j = pl.program_id(1)

        @pl.when(j == 0)
        def _():
            d2 = (0.5 * dy_ref[...]).astype(BF16)
            d2_sc[...] = d2
            dyh_ref[...] = d2
            dh_sc[...] = jnp.zeros_like(dh_sc)

        dhid = _dot_nt(d2_sc[...], wo_ref[...])
        gt = gate_ref[...].astype(F32)
        u = up_ref[...].astype(F32)
        sig = _sigmoid(gt)
        sl = gt * sig
        dgate = (dhid * u * (sig * (1.0 + gt * (1.0 - sig)))).astype(BF16)
        dup = (dhid * sl).astype(BF16)
        dgate_ref[...] = dgate
        dup_ref[...] = dup
        hid_ref[...] = (sl * u).astype(BF16)
        dh_sc[...] += _dot_nt(dgate, wg_ref[...]) + _dot_nt(dup, wu_ref[...])

        @pl.when(j == nj - 1)
        def _():
            dx, h, dg = _rms_bwd(x_ref[...], g_ref[...], dh_sc[...])
            dx_ref[...] = dy_ref[...] + dx
            h_ref[...] = h

            @pl.when(i == 0)
            def _():
                dg_ref[...] = dg

            @pl.when(i > 0)
            def _():
                dg_ref[...] += dg

    row = lambda i, j: (i, 0)
    blk = lambda i, j: (i, j)
    return pl.pallas_call(
        body, name="ffn_bwd",
        grid=(S // tm, nj),
        in_specs=[
            pl.BlockSpec((tm, D_MODEL), row),
            pl.BlockSpec((tm, D_MODEL), row),
            pl.BlockSpec((1, D_MODEL), lambda i, j: (0, 0)),
            pl.BlockSpec((tm, FF_TILE), blk),
            pl.BlockSpec((tm, FF_TILE), blk),
            pl.BlockSpec((D_MODEL, FF_TILE), lambda i, j: (0, j)),
            pl.BlockSpec((D_MODEL, FF_TILE), lambda i, j: (0, j + nj)),
            pl.BlockSpec((FF_TILE, D_MODEL), lambda i, j: (j, 0)),
        ],
        out_specs=[
            pl.BlockSpec((tm, D_MODEL), row),
            pl.BlockSpec((tm, D_MODEL), row),
            pl.BlockSpec((tm, D_MODEL), row),
            pl.BlockSpec((tm, FF_TILE), blk),
            pl.BlockSpec((tm, FF_TILE), blk),
            pl.BlockSpec((tm, FF_TILE), blk),
            pl.BlockSpec((1, D_MODEL), lambda i, j: (0, 0)),
        ],
        out_shape=[
            jax.ShapeDtypeStruct((S, D_MODEL), F32),
            jax.ShapeDtypeStruct((S, D_MODEL), BF16),
            jax.ShapeDtypeStruct((S, D_MODEL), BF16),
            jax.ShapeDtypeStruct((S, D_FF), BF16),
            jax.ShapeDtypeStruct((S, D_FF), BF16),
            jax.ShapeDtypeStruct((S, D_FF), BF16),
            jax.ShapeDtypeStruct((1, D_MODEL), F32),
        ],
        scratch_shapes=[pltpu.VMEM((tm, D_MODEL), BF16), pltpu.VMEM((tm, D_MODEL), F32)],
        compiler_params=_params(2),
    )(dy, x, g, gate, up, w_in, w_in, w_out)


def matmul_tn(a, b, ta, tn, tk=ROW_TILE, name="matmul_tn"):
    S, ka = a.shape
    nb = b.shape[1]

    def body(a_ref, b_ref, o_ref):
        k = pl.program_id(2)

        @pl.when(k == 0)
        def _():
            o_ref[...] = jnp.zeros_like(o_ref)

        o_ref[...] += _dot_tn(a_ref[...], b_ref[...])

    return pl.pallas_call(
        body, name=name,
        grid=(ka // ta, nb // tn, S // tk),
        in_specs=[
            pl.BlockSpec((tk, ta), lambda i, j, k: (k, i)),
            pl.BlockSpec((tk, tn), lambda i, j, k: (k, j)),
        ],
        out_specs=pl.BlockSpec((ta, tn), lambda i, j, k: (i, j)),
        out_shape=jax.ShapeDtypeStruct((ka, nb), F32),
        compiler_params=_params(3),
    )(a, b)


def mix_in_fwd(x, g, w, tm=ROW_TILE):
    S = x.shape[0]

    def body(x_ref, g_ref, w_ref, o_ref):
        _, xhat = _rms_stats(x_ref[...])
        o_ref[...] = _dot((xhat * g_ref[...]).astype(BF16), w_ref[...])

    return pl.pallas_call(
        body, name="mix_in_fwd",
        grid=(S // tm,),
        in_specs=[
            pl.BlockSpec((tm, D_MODEL), lambda i: (i, 0)),
            pl.BlockSpec((1, D_MODEL), lambda i: (0, 0)),
            pl.BlockSpec((D_MODEL, D_IN_PROJ), lambda i: (0, 0)),
        ],
        out_specs=pl.BlockSpec((tm, D_IN_PROJ), lambda i: (i, 0)),
        out_shape=jax.ShapeDtypeStruct((S, D_IN_PROJ), F32),
        compiler_params=_params(1),
    )(x, g, w)


def mix_in_bwd(dproj, w, x, g, dy, tm=ROW_TILE):
    S = x.shape[0]

    def body(dp_ref, w_ref, x_ref, g_ref, dy_ref, dx_ref, h_ref, dg_ref):
        i = pl.program_id(0)
        dh = _dot_nt(dp_ref[...], w_ref[...])
        dx, h, dg = _rms_bwd(x_ref[...], g_ref[...], dh)
        dx_ref[...] = dy_ref[...] + dx
        h_ref[...] = h

        @pl.when(i == 0)
        def _():
            dg_ref[...] = dg

        @pl.when(i > 0)
        def _():
            dg_ref[...] += dg

    row = lambda i: (i, 0)
    return pl.pallas_call(
        body, name="mix_in_bwd",
        grid=(S // tm,),
        in_specs=[
            pl.BlockSpec((tm, D_IN_PROJ), row),
            pl.BlockSpec((D_MODEL, D_IN_PROJ), lambda i: (0, 0)),
            pl.BlockSpec((tm, D_MODEL), row),
            pl.BlockSpec((1, D_MODEL), lambda i: (0, 0)),
            pl.BlockSpec((tm, D_MODEL), row),
        ],
        out_specs=[
            pl.BlockSpec((tm, D_MODEL), row),
            pl.BlockSpec((tm, D_MODEL), row),
            pl.BlockSpec((1, D_MODEL), lambda i: (0, 0)),
        ],
        out_shape=[
            jax.ShapeDtypeStruct((S, D_MODEL), F32),
            jax.ShapeDtypeStruct((S, D_MODEL), BF16),
            jax.ShapeDtypeStruct((1, D_MODEL), F32),
        ],
        compiler_params=_params(1),
    )(dproj, w, x, g, dy)


def mix_out_fwd(ycat, w, x, tm=ROW_TILE):
    S = x.shape[0]

    def body(y_ref, w_ref, x_ref, o_ref):
        o_ref[...] = x_ref[...] + _dot(y_ref[...], w_ref[...])

    row = lambda i: (i, 0)
    return pl.pallas_call(
        body, name="mix_out_fwd",
        grid=(S // tm,),
        in_specs=[
            pl.BlockSpec((tm, D_MODEL), row),
            pl.BlockSpec((D_MODEL, D_MODEL), lambda i: (0, 0)),
            pl.BlockSpec((tm, D_MODEL), row),
        ],
        out_specs=pl.BlockSpec((tm, D_MODEL), row),
        out_shape=jax.ShapeDtypeStruct((S, D_MODEL), F32),
        compiler_params=_params(1),
    )(ycat, w, x)


def mix_out_bwd(dy, w, tm=ROW_TILE):
    S = dy.shape[0]

    def body(dy_ref, w_ref, o_ref, dyb_ref):
        d = dy_ref[...].astype(BF16)
        dyb_ref[...] = d
        o_ref[...] = _dot_nt(d, w_ref[...])

    row = lambda i: (i, 0)
    return pl.pallas_call(
        body, name="mix_out_bwd",
        grid=(S // tm,),
        in_specs=[
            pl.BlockSpec((tm, D_MODEL), row),
            pl.BlockSpec((D_MODEL, D_MODEL), lambda i: (0, 0)),
        ],
        out_specs=[pl.BlockSpec((tm, D_MODEL), row), pl.BlockSpec((tm, D_MODEL), row)],
        out_shape=[jax.ShapeDtypeStruct((S, D_MODEL), F32), jax.ShapeDtypeStruct((S, D_MODEL), BF16)],
        compiler_params=_params(1),
    )(dy, w)


def _conv_ln(ypre, ln_g, ln_b):
    mu = jnp.mean(ypre, axis=-1, keepdims=True)
    yc = ypre - mu
    rstd = lax.rsqrt(jnp.mean(yc * yc, axis=-1, keepdims=True) + EPS)
    yn = yc * rstd
    return yn, rstd, yn * ln_g + ln_b


def conv_fwd(proj, cw, cb, ln_g, ln_b, tm=CONV_TILE):
    S = proj.shape[0]
    hb = tm // CONV_HALO

    def body(a_ref, b_ref, ap_ref, bp_ref, cw_ref, cb_ref, g_ref, bb_ref, y_ref, ypre_ref, v_sc):
        i = pl.program_id(0)
        prev = ap_ref[...] * _sigmoid(bp_ref[...])
        v_sc[pl.ds(0, CONV_HALO), :] = jnp.where(i > 0, prev, 0.0)
        v_sc[pl.ds(CONV_HALO, tm), :] = a_ref[...] * _sigmoid(b_ref[...])
        acc = jnp.zeros((tm, D_CONV), F32)
        for j in range(CONV_WIDTH):
            acc = acc + cw_ref[pl.ds(j, 1), :] * v_sc[pl.ds(CONV_HALO - (CONV_WIDTH - 1) + j, tm), :]
        ypre = acc + cb_ref[...]
        ypre_ref[...] = ypre
        _, _, z = _conv_ln(ypre, g_ref[...], bb_ref[...])
        y_ref[...] = (z * _sigmoid(z)).astype(BF16)

    one = lambda i: (0, 0)
    return pl.pallas_call(
        body, name="conv_fwd",
        grid=(S // tm,),
        in_specs=[
            pl.BlockSpec((tm, D_CONV), lambda i: (i, 0)),
            pl.BlockSpec((tm, D_CONV), lambda i: (i, 1)),
            pl.BlockSpec((CONV_HALO, D_CONV), lambda i: (jnp.maximum(i * hb - 1, 0), 0)),
            pl.BlockSpec((CONV_HALO, D_CONV), lambda i: (jnp.maximum(i * hb - 1, 0), 1)),
            pl.BlockSpec((CONV_HALO, D_CONV), one),
            pl.BlockSpec((1, D_CONV), one),
            pl.BlockSpec((1, D_CONV), one),
            pl.BlockSpec((1, D_CONV), one),
        ],
        out_specs=[pl.BlockSpec((tm, D_CONV), lambda i: (i, 0)), pl.BlockSpec((tm, D_CONV), lambda i: (i, 0))],
        out_shape=[jax.ShapeDtypeStruct((S, D_CONV), BF16), jax.ShapeDtypeStruct((S, D_CONV), F32)],
        scratch_shapes=[pltpu.VMEM((tm + CONV_HALO, D_CONV), F32)],
        compiler_params=_params(1),
    )(proj, proj, proj, proj, cw, cb, ln_g, ln_b)


def conv_bwd(dyc, ypre, proj, cw, ln_g, ln_b, tm=CONV_TILE):
    S = ypre.shape[0]
    hb = tm // CONV_HALO
    nblk = S // tm
    last_halo = S // CONV_HALO - 1

    def dpre(dy, yp, g, bb):
        yn, rstd, z = _conv_ln(yp, g, bb)
        sg = _sigmoid(z)
        dz = dy * (sg * (1.0 + z * (1.0 - sg)))
        dyn = dz * g
        d = rstd * (dyn - jnp.mean(dyn, axis=-1, keepdims=True) - yn * jnp.mean(dyn * yn, axis=-1, keepdims=True))
        return d, dz * yn, dz

    def body(dy_ref, yp_ref, dyn_ref, ypn_ref, a_ref, b_ref, ap_ref, bp_ref, cw_ref, g_ref, bb_ref,
             du_ref, dcw_ref, dsm_ref, d_sc, v_sc):
        i = pl.program_id(0)
        g = g_ref[...]
        bb = bb_ref[...]
        d_main, dgn, dz = dpre(dy_ref[...], yp_ref[...], g, bb)
        d_next, _, _ = dpre(dyn_ref[...], ypn_ref[...], g, bb)
        d_sc[pl.ds(0, tm), :] = d_main
        d_sc[pl.ds(tm, CONV_HALO), :] = jnp.where(i < nblk - 1, d_next, 0.0)
        a = a_ref[...]
        sb = _sigmoid(b_ref[...])
        prev = ap_ref[...] * _sigmoid(bp_ref[...])
        v_sc[pl.ds(0, CONV_HALO), :] = jnp.where(i > 0, prev, 0.0)
        v_sc[pl.ds(CONV_HALO, tm), :] = a * sb

        @pl.when(i == 0)
        def _():
            dcw_ref[...] = jnp.zeros_like(dcw_ref)
            dsm_ref[...] = jnp.zeros_like(dsm_ref)

        dv = jnp.zeros((tm, D_CONV), F32)
        for j in range(CONV_WIDTH):
            dv = dv + cw_ref[pl.ds(j, 1), :] * d_sc[pl.ds(CONV_WIDTH - 1 - j, tm), :]
            shifted = v_sc[pl.ds(CONV_HALO - (CONV_WIDTH - 1) + j, tm), :]
            dcw_ref[pl.ds(j, 1), :] += jnp.sum(d_main * shifted, axis=0, keepdims=True)
        du_ref[:, pl.ds(0, D_CONV)] = dv * sb
        du_ref[:, pl.ds(D_CONV, D_CONV)] = dv * a * sb * (1.0 - sb)
        dsm_ref[pl.ds(0, 1), :] += jnp.sum(d_main, axis=0, keepdims=True)
        dsm_ref[pl.ds(1, 1), :] += jnp.sum(dgn, axis=0, keepdims=True)
        dsm_ref[pl.ds(2, 1), :] += jnp.sum(dz, axis=0, keepdims=True)

    one = lambda i: (0, 0)
    prev_map = lambda c: (lambda i: (jnp.maximum(i * hb - 1, 0), c))
    next_map = lambda i: (jnp.minimum((i + 1) * hb, last_halo), 0)
    return pl.pallas_call(
        body, name="conv_bwd",
        grid=(nblk,),
        in_specs=[
            pl.BlockSpec((tm, D_CONV), lambda i: (i, 0)),
            pl.BlockSpec((tm, D_CONV), lambda i: (i, 0)),
            pl.BlockSpec((CONV_HALO, D_CONV), next_map),
            pl.BlockSpec((CONV_HALO, D_CONV), next_map),
            pl.BlockSpec((tm, D_CONV), lambda i: (i, 0)),
            pl.BlockSpec((tm, D_CONV), lambda i: (i, 1)),
            pl.BlockSpec((CONV_HALO, D_CONV), prev_map(0)),
            pl.BlockSpec((CONV_HALO, D_CONV), prev_map(1)),
            pl.BlockSpec((CONV_HALO, D_CONV), one),
            pl.BlockSpec((1, D_CONV), one),
            pl.BlockSpec((1, D_CONV), one),
        ],
        out_specs=[
            pl.BlockSpec((tm, 2 * D_CONV), lambda i: (i, 0)),
            pl.BlockSpec((CONV_HALO, D_CONV), one),
            pl.BlockSpec((8, D_CONV), one),
        ],
        out_shape=[
            jax.ShapeDtypeStruct((S, 2 * D_CONV), F32),
            jax.ShapeDtypeStruct((CONV_HALO, D_CONV), F32),
            jax.ShapeDtypeStruct((8, D_CONV), F32),
        ],
        scratch_shapes=[pltpu.VMEM((tm + CONV_HALO, D_CONV), F32), pltpu.VMEM((tm + CONV_HALO, D_CONV), F32)],
        compiler_params=_params(1),
    )(dyc, ypre, dyc, ypre, proj, proj, proj, proj, cw, ln_g, ln_b)


SB_GROUPS = (8, 4, 2, 1)


def _softplus(z):
    neg_abs = lax.bitcast_convert_type(lax.bitcast_convert_type(z, jnp.uint32) | jnp.uint32(0x80000000), F32)
    return jnp.maximum(z, 0.0) + jnp.log(1.0 + jnp.exp(neg_abs))


def _run_groups(n, body):
    g0 = SB_GROUPS[0]
    nbig = lax.div(n, g0)

    def big(t, c):
        body([t * g0 + j for j in range(g0)])
        return c

    lax.fori_loop(0, nbig, big, 0)
    done = nbig * g0
    for g in SB_GROUPS[1:]:
        has = lax.rem(lax.div(n, g), 2) == 1

        @pl.when(has)
        def _(done=done, g=g):
            body([done + j for j in range(g)])

        done = done + jnp.where(has, g, 0)


def _rows(xs):
    return xs[0] if len(xs) == 1 else jnp.concatenate(xs, axis=0)


def sb_fwd(q, k, v, T=SB_TILE):
    H, S, dh = q.shape

    def body(q_ref, k_ref, v_ref, o_ref, tot_ref, acc_sc, car_sc):
        qb = pl.program_id(1)
        qv = q_ref[...]
        row = lax.broadcasted_iota(jnp.int32, (T, T), 0)
        col = lax.broadcasted_iota(jnp.int32, (T, T), 1)
        tri = jnp.where(row >= col, 1.0, 0.0).astype(BF16)
        causal = col < row
        acc_sc[...] = jnp.zeros_like(acc_sc)
        car_sc[...] = jnp.zeros_like(car_sc)

        def logits(kb, masked):
            ks = k_ref[pl.ds(pl.multiple_of(kb * T, T), T), :]
            z = _dot_nt(qv, ks)
            nb = _softplus(z)
            if masked:
                nb = jnp.where(causal, nb, 0.0)
            return z, nb.astype(BF16)

        def group(kbs, masked):
            parts = [logits(kb, masked) for kb in kbs]
            pall = _dot(_rows([nb for _, nb in parts]), tri)
            carry = car_sc[...]
            out = None
            for j, kb in enumerate(kbs):
                p = pall[j * T:(j + 1) * T]
                vs = v_ref[pl.ds(pl.multiple_of(kb * T, T), T), :]
                w = jnp.exp((parts[j][0] - carry) - p)
                if masked:
                    w = jnp.where(causal, w, 0.0)
                o = _dot(w.astype(BF16), vs)
                out = o if out is None else out + o
                carry = carry + p[:, 0:1]
            acc_sc[...] += out
            car_sc[...] = carry

        group([qb], True)
        _run_groups(qb, lambda offs: group([qb - 1 - o for o in offs], False))
        o_ref[...] = acc_sc[...]
        tot_ref[...] = car_sc[...]

    return pl.pallas_call(
        body, name="sb_fwd",
        grid=(H, S // T),
        in_specs=[
            pl.BlockSpec((None, T, dh), lambda h, i: (h, i, 0)),
            pl.BlockSpec((None, S, dh), lambda h, i: (h, 0, 0)),
            pl.BlockSpec((None, S, dh), lambda h, i: (h, 0, 0)),
        ],
        out_specs=[
            pl.BlockSpec((None, T, dh), lambda h, i: (h, i, 0)),
            pl.BlockSpec((None, T, 1), lambda h, i: (h, i, 0)),
        ],
        out_shape=[jax.ShapeDtypeStruct((H, S, dh), F32), jax.ShapeDtypeStruct((H, S, 1), F32)],
        scratch_shapes=[pltpu.VMEM((T, dh), F32), pltpu.VMEM((T, 1), F32)],
        compiler_params=_params(2),
    )(q, k, v)


def sb_bwd(q, k, v, do, qt, dot, tot, T=SB_TILE):
    H, S, dh = q.shape
    nt = S // T

    def body(q_ref, k_ref, v_ref, do_ref, qt_ref, dot_ref, tot_ref, dq_ref, dk_ref, dv_ref, acc_sc, rc_sc, gc_sc):
        qb = pl.program_id(1)
        qv = q_ref[...]
        dov = do_ref[...]
        qtv = qt_ref[...]
        dotv = dot_ref[...]
        row = lax.broadcasted_iota(jnp.int32, (T, T), 0)
        col = lax.broadcasted_iota(jnp.int32, (T, T), 1)
        before = jnp.where(row < col, 1.0, 0.0).astype(BF16)
        causal = col < row
        acc_sc[...] = jnp.zeros_like(acc_sc)
        rc_sc[...] = tot_ref[...]
        gc_sc[...] = jnp.zeros_like(gc_sc)

        @pl.when(qb == 0)
        def _():
            dk_ref[...] = jnp.zeros_like(dk_ref)
            dv_ref[...] = jnp.zeros_like(dv_ref)

        def first(kb, masked):
            start = pl.multiple_of(kb * T, T)
            z = _dot_nt(qv, k_ref[pl.ds(start, T), :])
            nb = _softplus(z)
            sig = jnp.exp(z - nb)
            if masked:
                nb = jnp.where(causal, nb, 0.0)
            dw = _dot_nt(dov, v_ref[pl.ds(start, T), :])
            return z, sig, nb.astype(BF16), dw

        def group(kbs, masked):
            parts = [first(kb, masked) for kb in kbs]
            pall = _dot(_rows([p[2] for p in parts]), before)
            rc = rc_sc[...]
            ws, gs, ghs = [], [], []
            for j in range(len(kbs)):
                z, _, nbh, dw = parts[j]
                p = pall[j * T:(j + 1) * T]
                w = jnp.exp((z - rc) + p)
                rc = rc - (p[:, T - 1:T] + nbh[:, T - 1:T].astype(F32))
                if masked:
                    w = jnp.where(causal, w, 0.0)
                g = dw * w
                ws.append(w.astype(BF16))
                gs.append(g)
                ghs.append(g.astype(BF16))
            glall = _dot(_rows(ghs), before)
            gc = gc_sc[...]
            dq = None
            for j, kb in enumerate(kbs):
                ks = k_ref[pl.ds(pl.multiple_of(kb * T, T), T), :]
                gl = glall[j * T:(j + 1) * T]
                dz = gs[j] - parts[j][1] * (gs[j] + (gl + gc))
                gc = gc + gl[:, T - 1:T] + ghs[j][:, T - 1:T].astype(F32)
                if masked:
                    dz = jnp.where(causal, dz, 0.0)
                dzb = dz.astype(BF16)
                d = _dot(dzb, ks)
                dq = d if dq is None else dq + d
                dk_ref[kb] += _dot(qtv, dzb)
                dv_ref[kb] += _dot(dotv, ws[j])
            acc_sc[...] += dq
            rc_sc[...] = rc
            gc_sc[...] = gc

        _run_groups(qb, lambda offs: group(offs, False))
        group([qb], True)
        dq_ref[...] = acc_sc[...]

    return pl.pallas_call(
        body, name="sb_bwd",
        grid=(H, nt),
        in_specs=[
            pl.BlockSpec((None, T, dh), lambda h, i: (h, i, 0)),
            pl.BlockSpec((None, S, dh), lambda h, i: (h, 0, 0)),
            pl.BlockSpec((None, S, dh), lambda h, i: (h, 0, 0)),
            pl.BlockSpec((None, T, dh), lambda h, i: (h, i, 0)),
            pl.BlockSpec((None, dh, T), lambda h, i: (h, 0, i)),
            pl.BlockSpec((None, dh, T), lambda h, i: (h, 0, i)),
            pl.BlockSpec((None, T, 1), lambda h, i: (h, i, 0)),
        ],
        out_specs=[
            pl.BlockSpec((None, T, dh), lambda h, i: (h, i, 0)),
            pl.BlockSpec((None, nt, dh, T), lambda h, i: (h, 0, 0, 0)),
            pl.BlockSpec((None, nt, dh, T), lambda h, i: (h, 0, 0, 0)),
        ],
        out_shape=[jax.ShapeDtypeStruct((H, S, dh), F32), jax.ShapeDtypeStruct((H, nt, dh, T), F32),
                   jax.ShapeDtypeStruct((H, nt, dh, T), F32)],
        scratch_shapes=[pltpu.VMEM((T, dh), F32), pltpu.VMEM((T, 1), F32), pltpu.VMEM((T, 1), F32)],
        compiler_params=_params(2),
    )(q, k, v, do, qt, dot, tot)


def _ret_tables(T=RET_TILE):
    hh = jnp.arange(N_RET_HEADS, dtype=F32)
    log_gamma = jnp.log1p(-jnp.exp2(-5.0 - hh))
    idx = jnp.arange(T, dtype=F32)
    diff = idx[:, None] - idx[None, :]
    ci = (jnp.arange(T) // 64)
    same = ci[:, None] == ci[None, :]
    earlier = ci[None, :] < ci[:, None]
    dist = jnp.where(same, jnp.abs(diff), diff)
    dmat = jnp.where(same | earlier, jnp.exp(log_gamma[:, None, None] * dist[None]), 0.0)
    ones = jnp.ones((1, 1, HEAD_DIM), F32)
    qdec = jnp.exp(log_gamma[:, None] * (idx + 1.0)[None, :])[:, :, None] * ones
    kdec = jnp.exp(log_gamma[:, None] * (T - 1.0 - idx)[None, :])[:, :, None] * ones
    bdec = jnp.exp(log_gamma * T)[:, None, None] * jnp.ones((1, HEAD_DIM, HEAD_DIM), F32)
    return dmat, qdec, kdec, bdec


def _rope_tables(S):
    half = HEAD_DIM // 2
    inv = 1.0 / (ROPE_BASE ** (jnp.arange(half, dtype=F32) / half))
    ang = jnp.arange(S).astype(F32)[:, None] * inv[None, :]
    c = jnp.cos(ang)
    s = jnp.sin(ang)
    cos = jnp.tile(jnp.concatenate([c, c], axis=1), (1, N_RET_HEADS))
    sin = jnp.tile(jnp.concatenate([-s, s], axis=1), (1, N_RET_HEADS))
    return cos, sin


def _swap_halves(x):
    n = x.shape[1]
    lane = lax.broadcasted_iota(jnp.int32, x.shape, 1)
    first = (lane % HEAD_DIM) < (HEAD_DIM // 2)
    return jnp.where(first, pltpu.roll(x, n - HEAD_DIM // 2, 1), pltpu.roll(x, HEAD_DIM // 2, 1))


def rope_fwd(proj, cos, sin, tm=ROW_TILE):
    S = proj.shape[0]

    def body(q_ref, k_ref, c_ref, s_ref, qo_ref, ko_ref):
        c = c_ref[...]
        s = s_ref[...]
        qv = q_ref[...]
        kv = k_ref[...]
        qo_ref[...] = ((qv * c + _swap_halves(qv) * s) * 0.125).astype(BF16)
        ko_ref[...] = (kv * c + _swap_halves(kv) * s).astype(BF16)

    row = lambda i: (i, 0)
    return pl.pallas_call(
        body, name="rope_fwd",
        grid=(S // tm,),
        in_specs=[
            pl.BlockSpec((tm, D_RET), lambda i: (i, 8)),
            pl.BlockSpec((tm, D_RET), lambda i: (i, 9)),
            pl.BlockSpec((tm, D_RET), row),
            pl.BlockSpec((tm, D_RET), row),
        ],
        out_specs=[pl.BlockSpec((tm, D_RET), row), pl.BlockSpec((tm, D_RET), row)],
        out_shape=[jax.ShapeDtypeStruct((S, D_RET), BF16)] * 2,
        compiler_params=_params(1),
    )(proj, proj, cos, sin)


def rope_bwd(dq, dk, cos, sin, tm=ROW_TILE):
    S = dq.shape[0]

    def body(dq_ref, dk_ref, c_ref, s_ref, qo_ref, ko_ref):
        c = c_ref[...]
        s = s_ref[...]
        dqv = dq_ref[...] * 0.125
        dkv = dk_ref[...]
        qo_ref[...] = dqv * c - _swap_halves(dqv) * s
        ko_ref[...] = dkv * c - _swap_halves(dkv) * s

    row = lambda i: (i, 0)
    return pl.pallas_call(
        body, name="rope_bwd",
        grid=(S // tm,),
        in_specs=[pl.BlockSpec((tm, D_RET), row)] * 4,
        out_specs=[pl.BlockSpec((tm, D_RET), row)] * 2,
        out_shape=[jax.ShapeDtypeStruct((S, D_RET), F32)] * 2,
        compiler_params=_params(1),
    )(dq, dk, cos, sin)


def ret_fwd(q, k, v, gate, ng, tables, T=RET_TILE):
    H, S, dh = q.shape
    dmat, qdec, kdec, bdec = tables

    def body(q_ref, k_ref, v_ref, gt_ref, ng_ref, dm_ref, qd_ref, kd_ref, bd_ref, o_ref, y_ref, st_ref, s_sc):
        n = pl.program_id(1)

        @pl.when(n == 0)
        def _():
            s_sc[...] = jnp.zeros_like(s_sc)

        qv = q_ref[...]
        kv = k_ref[...]
        vv = v_ref[...]
        state = s_sc[...]
        st_ref[...] = state
        sc = (_dot_nt(qv, kv) * dm_ref[...]).astype(BF16)
        qd = (qv.astype(F32) * qd_ref[...]).astype(BF16)
        y = _dot(sc, vv) + _dot(qd, state.astype(BF16))
        y_ref[...] = y
        kd = (kv.astype(F32) * kd_ref[...]).astype(BF16)
        s_sc[...] = bd_ref[...] * state + _dot_tn(kd, vv)
        mu = jnp.mean(y, axis=-1, keepdims=True)
        yc = y - mu
        yn = yc * lax.rsqrt(jnp.mean(yc * yc, axis=-1, keepdims=True) + EPS)
        gt = gt_ref[...]
        o_ref[...] = gt * _sigmoid(gt) * (yn * ng_ref[...])

    blk = lambda h, n: (h, n, 0)
    head = lambda h, n: (h, 0, 0)
    return pl.pallas_call(
        body, name="ret_fwd",
        grid=(H, S // T),
        in_specs=[
            pl.BlockSpec((None, T, dh), blk),
            pl.BlockSpec((None, T, dh), blk),
            pl.BlockSpec((None, T, dh), blk),
            pl.BlockSpec((None, T, dh), blk),
            pl.BlockSpec((None, 1, dh), head),
            pl.BlockSpec((None, T, T), head),
            pl.BlockSpec((None, T, dh), head),
            pl.BlockSpec((None, T, dh), head),
            pl.BlockSpec((None, dh, dh), head),
        ],
        out_specs=[
            pl.BlockSpec((None, T, dh), blk),
            pl.BlockSpec((None, T, dh), blk),
            pl.BlockSpec((None, None, dh, dh), lambda h, n: (h, n, 0, 0)),
        ],
        out_shape=[
            jax.ShapeDtypeStruct((H, S, dh), F32),
            jax.ShapeDtypeStruct((H, S, dh), F32),
            jax.ShapeDtypeStruct((H, S // T, dh, dh), F32),
        ],
        scratch_shapes=[pltpu.VMEM((dh, dh), F32)],
        compiler_params=_params(2),
    )(q, k, v, gate, ng, dmat, qdec, kdec, bdec)


def ret_bwd(do, q, k, v, gate, ng, y, states, tables, T=RET_TILE):
    H, S, dh = q.shape
    nb = S // T
    dmat, qdec, kdec, bdec = tables

    def body(do_ref, q_ref, k_ref, v_ref, gt_ref, ng_ref, y_ref, st_ref, dm_ref, qd_ref, kd_ref, bd_ref,
             dq_ref, dk_ref, dv_ref, dgt_ref, dng_ref, u_sc):
        n = pl.program_id(1)

        @pl.when(n == 0)
        def _():
            u_sc[...] = jnp.zeros_like(u_sc)
            dng_ref[...] = jnp.zeros_like(dng_ref)

        yv = y_ref[...]
        mu = jnp.mean(yv, axis=-1, keepdims=True)
        yc = yv - mu
        rstd = lax.rsqrt(jnp.mean(yc * yc, axis=-1, keepdims=True) + EPS)
        yn = yc * rstd
        gt = gt_ref[...]
        sg = _sigmoid(gt)
        ngv = ng_ref[...]
        dout = do_ref[...]
        dgt_ref[...] = dout * (yn * ngv) * (sg * (1.0 + gt * (1.0 - sg)))
        dn = dout * (gt * sg)
        dng_ref[...] += jnp.sum(dn * yn, axis=0, keepdims=True)
        dyn = dn * ngv
        dy = rstd * (dyn - jnp.mean(dyn, axis=-1, keepdims=True) - yn * jnp.mean(dyn * yn, axis=-1, keepdims=True))
        dyb = dy.astype(BF16)

        qv = q_ref[...]
        kv = k_ref[...]
        vv = v_ref[...]
        dm = dm_ref[...]
        qdt = qd_ref[...]
        kdt = kd_ref[...]
        sb = st_ref[...].astype(BF16)
        u = u_sc[...]
        ub = u.astype(BF16)
        dqk = (_dot_nt(dyb, vv) * dm).astype(BF16)
        sc = (_dot_nt(qv, kv) * dm).astype(BF16)
        qd = (qv.astype(F32) * qdt).astype(BF16)
        kd = (kv.astype(F32) * kdt).astype(BF16)
        dq_ref[...] = _dot(dqk, kv) + qdt * _dot_nt(dyb, sb)
        dk_ref[...] = _dot_tn(dqk, qv) + kdt * _dot_nt(vv, ub)
        dv_ref[...] = _dot_tn(sc, dyb) + _dot(kd, ub)
        u_sc[...] = bd_ref[...] * u + _dot_tn(qd, dyb)

    blk = lambda h, n: (h, nb - 1 - n, 0)
    head = lambda h, n: (h, 0, 0)
    return pl.pallas_call(
        body, name="ret_bwd",
        grid=(H, nb),
        in_specs=[
            pl.BlockSpec((None, T, dh), blk),
            pl.BlockSpec((None, T, dh), blk),
            pl.BlockSpec((None, T, dh), blk),
            pl.BlockSpec((None, T, dh), blk),
            pl.BlockSpec((None, T, dh), blk),
            pl.BlockSpec((None, 1, dh), head),
            pl.BlockSpec((None, T, dh), blk),
            pl.BlockSpec((None, None, dh, dh), lambda h, n: (h, nb - 1 - n, 0, 0)),
            pl.BlockSpec((None, T, T), head),
            pl.BlockSpec((None, T, dh), head),
            pl.BlockSpec((None, T, dh), head),
            pl.BlockSpec((None, dh, dh), head),
        ],
        out_specs=[
            pl.BlockSpec((None, T, dh), blk),
            pl.BlockSpec((None, T, dh), blk),
            pl.BlockSpec((None, T, dh), blk),
            pl.BlockSpec((None, T, dh), blk),
            pl.BlockSpec((None, 1, dh), head),
        ],
        out_shape=[jax.ShapeDtypeStruct((H, S, dh), F32)] * 4 + [jax.ShapeDtypeStruct((H, 1, dh), F32)],
        scratch_shapes=[pltpu.VMEM((dh, dh), F32)],
        compiler_params=_params(2),
    )(do, q, k, v, gate, ng, y, states, dmat, qdec, kdec, bdec)


def loss_head(x, g, target, tm=ROW_TILE):
    S = x.shape[0]

    def body(x_ref, g_ref, t_ref, loss_ref, dx_ref, dg_ref):
        i = pl.program_id(0)
        xv = x_ref[...]
        gv = g_ref[...]
        _, xhat = _rms_stats(xv)
        err = xhat * gv - t_ref[...]
        part = 0.5 * jnp.sum(jnp.mean(err * err, axis=-1, keepdims=True), axis=0, keepdims=True)
        dx, _, dg = _rms_bwd(xv, gv, err * (1.0 / D_MODEL))
        dx_ref[...] = dx
        part = jnp.broadcast_to(part, (1, 128))

        @pl.when(i == 0)
        def _():
            loss_ref[...] = part
            dg_ref[...] = dg

        @pl.when(i > 0)
        def _():
            loss_ref[...] += part
            dg_ref[...] += dg

    row = lambda i: (i, 0)
    one = lambda i: (0, 0)
    return pl.pallas_call(
        body, name="loss_head",
        grid=(S // tm,),
        in_specs=[pl.BlockSpec((tm, D_MODEL), row), pl.BlockSpec((1, D_MODEL), one), pl.BlockSpec((tm, D_MODEL), row)],
        out_specs=[pl.BlockSpec((1, 128), one), pl.BlockSpec((tm, D_MODEL), row), pl.BlockSpec((1, D_MODEL), one)],
        out_shape=[
            jax.ShapeDtypeStruct((1, 128), F32),
            jax.ShapeDtypeStruct((S, D_MODEL), F32),
            jax.ShapeDtypeStruct((1, D_MODEL), F32),
        ],
        compiler_params=_params(1),
    )(x, g, target)


def adamw(parts, w, m, v, tr):
    R, C = w.shape
    c1 = 1.0 / (1.0 - ADAM_B1 ** ADAM_STEP)
    c2 = 1.0 / (1.0 - ADAM_B2 ** ADAM_STEP)

    def body(p_ref, w_ref, m_ref, v_ref, g_ref, d_ref, mo_ref, vo_ref):
        g = p_ref[0]
        for d in range(1, N_DEV):
            g = g + p_ref[d]
        mn = ADAM_B1 * m_ref[...] + (1.0 - ADAM_B1) * g
        vn = ADAM_B2 * v_ref[...] + (1.0 - ADAM_B2) * (g * g)
        g_ref[...] = g
        mo_ref[...] = mn
        vo_ref[...] = vn
        d_ref[...] = -ADAM_LR * ((mn * c1) / (jnp.sqrt(vn * c2) + ADAM_EPS) + ADAM_WD * w_ref[...])

    row = lambda i: (i, 0)
    return pl.pallas_call(
        body, name="adamw",
        grid=(R // tr,),
        in_specs=[pl.BlockSpec((N_DEV, tr, C), lambda i: (0, i, 0))] + [pl.BlockSpec((tr, C), row)] * 3,
        out_specs=[pl.BlockSpec((tr, C), row)] * 4,
        out_shape=[jax.ShapeDtypeStruct((R, C), F32)] * 4,
        compiler_params=_params(1),
    )(parts, w, m, v)


def _my_id():
    return lax.axis_index("x") * 4 + lax.axis_index("y") * 2 + lax.axis_index("c")


def _peer(k):
    x, y, c = lax.axis_index("x"), lax.axis_index("y"), lax.axis_index("c")
    px = 1 - x if k & 4 else x
    py = 1 - y if k & 2 else y
    pc = 1 - c if k & 1 else c
    return (px, py, pc), px * 4 + py * 2 + pc


def all_gather(shards):
    n = len(shards)

    def body(*refs):
        ins, outs = refs[:n], refs[n:2 * n]
        send_sems, recv_sems, local_sems = refs[2 * n:]
        me = _my_id()
        local = [pltpu.make_async_copy(ins[t], outs[t].at[me], local_sems.at[t]) for t in range(n)]
        for cp in local:
            cp.start()
        sends = []
        for k in range(1, N_DEV):
            dev, _ = _peer(k)
            for t in range(n):
                cp = pltpu.make_async_remote_copy(
                    src_ref=ins[t], dst_ref=outs[t].at[me],
                    send_sem=send_sems.at[t, k - 1], recv_sem=recv_sems.at[t, k - 1],
                    device_id=dev, device_id_type=pl.DeviceIdType.MESH)
                cp.start()
                sends.append(cp)
        for k in range(1, N_DEV):
            dev, pid = _peer(k)
            for t in range(n):
                pltpu.make_async_remote_copy(
                    src_ref=ins[t], dst_ref=outs[t].at[pid],
                    send_sem=send_sems.at[t, k - 1], recv_sem=recv_sems.at[t, k - 1],
                    device_id=dev, device_id_type=pl.DeviceIdType.MESH).wait_recv()
        for cp in sends:
            cp.wait_send()
        for cp in local:
            cp.wait()

    any_spec = pl.BlockSpec(memory_space=pl.ANY)
    return pl.pallas_call(
        body, name="all_gather",
        in_specs=[any_spec] * n,
        out_specs=[any_spec] * n,
        out_shape=[jax.ShapeDtypeStruct((N_DEV,) + s.shape, s.dtype) for s in shards],
        scratch_shapes=[
            pltpu.SemaphoreType.DMA((n, N_DEV - 1)),
            pltpu.SemaphoreType.DMA((n, N_DEV - 1)),
            pltpu.SemaphoreType.DMA((n,)),
        ],
    )(*shards)


def exchange(parts):
    n = len(parts)

    def body(*refs):
        ins, outs = refs[:n], refs[n:2 * n]
        send_sems, recv_sems, local_sems = refs[2 * n:]
        me = _my_id()
        local = [pltpu.make_async_copy(ins[t].at[me], outs[t].at[me], local_sems.at[t]) for t in range(n)]
        for cp in local:
            cp.start()
        sends = []
        for k in range(1, N_DEV):
            dev, pid = _peer(k)
            for t in range(n):
                cp = pltpu.make_async_remote_copy(
                    src_ref=ins[t].at[pid], dst_ref=outs[t].at[me],
                    send_sem=send_sems.at[t, k - 1], recv_sem=recv_sems.at[t, k - 1],
                    device_id=dev, device_id_type=pl.DeviceIdType.MESH)
                cp.start()
                sends.append(cp)
        for k in range(1, N_DEV):
            dev, pid = _peer(k)
            for t in range(n):
                pltpu.make_async_remote_copy(
                    src_ref=ins[t].at[me], dst_ref=outs[t].at[pid],
                    send_sem=send_sems.at[t, k - 1], recv_sem=recv_sems.at[t, k - 1],
                    device_id=dev, device_id_type=pl.DeviceIdType.MESH).wait_recv()
        for cp in sends:
            cp.wait_send()
        for cp in local:
            cp.wait()

    any_spec = pl.BlockSpec(memory_space=pl.ANY)
    return pl.pallas_call(
        body, name="exchange",
        in_specs=[any_spec] * n,
        out_specs=[any_spec] * n,
        out_shape=[jax.ShapeDtypeStruct(p.shape, p.dtype) for p in parts],
        scratch_shapes=[
            pltpu.SemaphoreType.DMA((n, N_DEV - 1)),
            pltpu.SemaphoreType.DMA((n, N_DEV - 1)),
            pltpu.SemaphoreType.DMA((n,)),
        ],
    )(*parts)


def _heads(t, n_heads):
    S = t.shape[0]
    return t.reshape(S, n_heads, HEAD_DIM).transpose(1, 0, 2)


def _unheads(t):
    H, S, _ = t.shape
    return t.transpose(1, 0, 2).reshape(S, H * HEAD_DIM)


def _row(v):
    return v.reshape(1, -1)


def _pad_taps(cw):
    return jnp.concatenate([cw, jnp.zeros((CONV_HALO - CONV_WIDTH, D_CONV), F32)], axis=0)


def local_step(x, target, W):
    S = x.shape[0]
    cos, sin = _rope_tables(S)
    tables = _ret_tables()
    saved = []
    for l in range(DEPTH):
        sv = {"x0": x}
        x, sv["gate1"], sv["up1"] = ffn_fwd(x, _row(W["ffn1_norm"][l]), W["ffn1_w_in"][l], W["ffn1_w_out"][l])
        sv["x1"] = x
        proj = mix_in_fwd(x, _row(W["mix_norm"][l]), W["mix_w_in"][l])
        sv["proj"] = proj
        cw = _pad_taps(W["conv_w"][l])
        y_conv, sv["ypre"] = conv_fwd(proj, cw, _row(W["conv_b"][l]), _row(W["conv_ln_g"][l]), _row(W["conv_ln_b"][l]))
        sv["q_sb"] = _heads((proj[:, 512:1024] * 0.125).astype(BF16), N_SB_HEADS)
        sv["k_sb"] = _heads(proj[:, 1024:1536].astype(BF16), N_SB_HEADS)
        sv["v_sb"] = _heads(proj[:, 1536:2048].astype(BF16), N_SB_HEADS)
        o_sb, sv["tot"] = sb_fwd(sv["q_sb"], sv["k_sb"], sv["v_sb"])
        q_rot, k_rot = rope_fwd(proj, cos, sin)
        sv["q_r"] = _heads(q_rot, N_RET_HEADS)
        sv["k_r"] = _heads(k_rot, N_RET_HEADS)
        sv["v_r"] = _heads(proj[:, 2560:2816].astype(BF16), N_RET_HEADS)
        sv["g_r"] = _heads(proj[:, 2816:3072], N_RET_HEADS)
        ng = W["ret_norm_g"][l].reshape(N_RET_HEADS, 1, HEAD_DIM)
        o_r, sv["y_r"], sv["states"] = ret_fwd(sv["q_r"], sv["k_r"], sv["v_r"], sv["g_r"], ng, tables)
        ycat = jnp.concatenate([y_conv, _unheads(o_sb).astype(BF16), _unheads(o_r).astype(BF16)], axis=1)
        sv["ycat"] = ycat
        x = mix_out_fwd(ycat, W["mix_w_out"][l], x)
        sv["x2"] = x
        x, sv["gate2"], sv["up2"] = ffn_fwd(x, _row(W["ffn2_norm"][l]), W["ffn2_w_in"][l], W["ffn2_w_out"][l])
        saved.append(sv)

    loss_acc, dx, dg_final = loss_head(x, _row(W["final_norm"]), target)
    grads = {"final_norm": dg_final.reshape(D_MODEL)}
    per_layer = []

    def ffn_back(dx, x_in, gate, up, norm, w_in, w_out):
        dx, h, dyh, dgate, dup, hid, dg = ffn_bwd(dx, x_in, _row(norm), gate, up, w_in, w_out)
        d_in = jnp.concatenate([
            matmul_tn(h, dgate, D_MODEL, FF_TILE, name="ffn_dw_gate"),
            matmul_tn(h, dup, D_MODEL, FF_TILE, name="ffn_dw_up")], axis=1)
        d_out = matmul_tn(hid, dyh, FF_TILE, D_MODEL, name="ffn_dw_out")
        return dx, dg.reshape(D_MODEL), d_in, d_out

    for l in reversed(range(DEPTH)):
        sv = saved[l]
        gl = {}
        dx, gl["ffn2_norm"], gl["ffn2_w_in"], gl["ffn2_w_out"] = ffn_back(
            dx, sv["x2"], sv["gate2"], sv["up2"], W["ffn2_norm"][l], W["ffn2_w_in"][l], W["ffn2_w_out"][l])
        dycat, dxb = mix_out_bwd(dx, W["mix_w_out"][l])
        gl["mix_w_out"] = matmul_tn(sv["ycat"], dxb, D_MODEL, D_MODEL, name="mix_dw_out")
        cw = _pad_taps(W["conv_w"][l])
        du_conv, dcw, dsm = conv_bwd(dycat, sv["ypre"], sv["proj"], cw, _row(W["conv_ln_g"][l]), _row(W["conv_ln_b"][l]))
        gl["conv_w"] = dcw[:CONV_WIDTH]
        gl["conv_b"], gl["conv_ln_g"], gl["conv_ln_b"] = dsm[0], dsm[1], dsm[2]
        do_sb = _heads(dycat[:, 256:768].astype(BF16), N_SB_HEADS)
        dq_sb, dk_t, dv_t = sb_bwd(sv["q_sb"], sv["k_sb"], sv["v_sb"], do_sb, sv["q_sb"].transpose(0, 2, 1),
                                   do_sb.transpose(0, 2, 1), sv["tot"])
        dk_sb = dk_t.transpose(1, 3, 0, 2).reshape(S, D_SB)
        dv_sb = dv_t.transpose(1, 3, 0, 2).reshape(S, D_SB)
        do_r = _heads(dycat[:, 768:1024], N_RET_HEADS)
        ng = W["ret_norm_g"][l].reshape(N_RET_HEADS, 1, HEAD_DIM)
        dq_r, dk_r, dv_r, dg_r, dng = ret_bwd(do_r, sv["q_r"], sv["k_r"], sv["v_r"], sv["g_r"], ng, sv["y_r"],
                                              sv["states"], tables)
        gl["ret_norm_g"] = dng.reshape(D_RET)
        dq_rr, dk_rr = rope_bwd(_unheads(dq_r), _unheads(dk_r), cos, sin)
        dproj = jnp.concatenate([
            du_conv.astype(BF16),
            (_unheads(dq_sb) * 0.125).astype(BF16), dk_sb.astype(BF16), dv_sb.astype(BF16),
            dq_rr.astype(BF16), dk_rr.astype(BF16), _unheads(dv_r).astype(BF16), _unheads(dg_r).astype(BF16)], axis=1)
        dx, h, dg = mix_in_bwd(dproj, W["mix_w_in"][l], sv["x1"], _row(W["mix_norm"][l]), dx)
        gl["mix_norm"] = dg.reshape(D_MODEL)
        gl["mix_w_in"] = matmul_tn(h, dproj, D_MODEL, D_MODEL, name="mix_dw_in")
        dx, gl["ffn1_norm"], gl["ffn1_w_in"], gl["ffn1_w_out"] = ffn_back(
            dx, sv["x0"], sv["gate1"], sv["up1"], W["ffn1_norm"][l], W["ffn1_w_in"][l], W["ffn1_w_out"][l])
        per_layer.append(gl)

    per_layer = per_layer[::-1]
    for name in per_layer[0]:
        grads[name] = jnp.stack([per_layer[l][name] for l in range(DEPTH)], axis=0)
    return loss_acc[0, 0], dx, grads


COL_SHARDED = ("ffn1_w_in", "mix_w_in", "ffn2_w_in")
ROW_SHARDED = ("ffn1_w_out", "mix_w_out", "ffn2_w_out")
SMALL = ("ffn1_norm", "mix_norm", "conv_b", "conv_ln_g", "conv_ln_b", "ret_norm_g", "ffn2_norm", "final_norm")
WEIGHTS = ("ffn1_norm", "ffn1_w_in", "ffn1_w_out", "mix_norm", "mix_w_in", "conv_w", "conv_b", "conv_ln_g",
           "conv_ln_b", "ret_norm_g", "mix_w_out", "ffn2_norm", "ffn2_w_in", "ffn2_w_out", "final_norm")
SMALL_ROWS = 32


def _gather_weights(w):
    big = COL_SHARDED + ROW_SHARDED
    got = all_gather([w[n].astype(BF16) for n in big] + [w["conv_w"]])
    full = {}
    for n, g in zip(big, got[:-1]):
        if n in COL_SHARDED:
            full[n] = g.transpose(1, 2, 0, 3).reshape(DEPTH, D_MODEL, -1)
        else:
            full[n] = g.transpose(1, 0, 2, 3).reshape(DEPTH, -1, D_MODEL)
    full["conv_w"] = got[-1].transpose(1, 2, 0, 3).reshape(DEPTH, CONV_WIDTH, D_CONV)
    for n in SMALL:
        full[n] = w[n]
    return full


def _pack_small(g):
    flat = jnp.concatenate([g[n].reshape(-1) for n in SMALL] + [g["conv_w"].reshape(-1)])
    flat = jnp.concatenate([flat, jnp.zeros((SMALL_ROWS * D_MODEL - flat.shape[0],), F32)])
    return flat.reshape(SMALL_ROWS, D_MODEL)


def _unpack_small(buf, like):
    flat = buf.reshape(-1)
    out, off = {}, 0
    for n in SMALL:
        size = int(np.prod(like[n].shape))
        out[n] = flat[off:off + size].reshape(like[n].shape)
        off += size
    size = DEPTH * CONV_WIDTH * D_CONV
    out["conv_w"] = flat[off:off + size].reshape(DEPTH, CONV_WIDTH, D_CONV)
    return out


def kernel(x, ffn1_norm, ffn1_w_in, ffn1_w_out, mix_norm, mix_w_in, conv_w, conv_b, conv_ln_g, conv_ln_b, ret_norm_g, mix_w_out, ffn2_norm, ffn2_w_in, ffn2_w_out, final_norm, loss_target, m_ffn1_norm, m_ffn1_w_in, m_ffn1_w_out, m_mix_norm, m_mix_w_in, m_conv_w, m_conv_b, m_conv_ln_g, m_conv_ln_b, m_ret_norm_g, m_mix_w_out, m_ffn2_norm, m_ffn2_w_in, m_ffn2_w_out, m_final_norm, v_ffn1_norm, v_ffn1_w_in, v_ffn1_w_out, v_mix_norm, v_mix_w_in, v_conv_w, v_conv_b, v_conv_ln_g, v_conv_ln_b, v_ret_norm_g, v_mix_w_out, v_ffn2_norm, v_ffn2_w_in, v_ffn2_w_out, v_final_norm):
    args = locals()
    w = {n: args[n] for n in WEIGHTS}
    m = {n: args["m_" + n] for n in WEIGHTS}
    v = {n: args["v_" + n] for n in WEIGHTS}
    me = _my_id()

    full = _gather_weights(w)
    loss, grad_x, g = local_step(x[0], loss_target[0], full)
    loss = lax.psum(loss, ("x", "y", "c"))

    parts, shapes = [], []
    for n in COL_SHARDED + ROW_SHARDED:
        if n in COL_SHARDED:
            p = g[n].reshape(DEPTH, D_MODEL, N_DEV, -1).transpose(2, 0, 1, 3)
        else:
            p = g[n].reshape(DEPTH, N_DEV, -1, D_MODEL).transpose(1, 0, 2, 3)
        shapes.append(p.shape[1:])
        parts.append(p.reshape(N_DEV, -1, p.shape[-1]))
    small = _pack_small(g)
    parts.append(jnp.broadcast_to(small[None], (N_DEV, SMALL_ROWS, D_MODEL)))
    got = exchange(parts)

    grad, delta, new_m, new_v = {}, {}, {}, {}
    for n, p, shp in zip(COL_SHARDED + ROW_SHARDED, got[:-1], shapes):
        r, c = p.shape[1:]
        outs = adamw(p, w[n].reshape(r, c), m[n].reshape(r, c), v[n].reshape(r, c), tr=r // 8)
        grad[n], delta[n], new_m[n], new_v[n] = [o.reshape(shp) for o in outs]

    def small_pack(d):
        mine = dict(d)
        cwf = jnp.zeros((DEPTH, CONV_WIDTH, D_CONV), F32)
        mine["conv_w"] = lax.dynamic_update_slice(cwf, d["conv_w"], (0, 0, me * (D_CONV // N_DEV)))
        return _pack_small(mine)

    outs = adamw(got[-1], small_pack(w), small_pack(m), small_pack(v), tr=SMALL_ROWS)
    for dst, o in zip((grad, delta, new_m, new_v), outs):
        un = _unpack_small(o, w)
        un["conv_w"] = lax.dynamic_slice(un["conv_w"], (0, 0, me * (D_CONV // N_DEV)),
                                         (DEPTH, CONV_WIDTH, D_CONV // N_DEV))
        dst.update(un)

    return (loss, grad_x[None], *[grad[n] for n in WEIGHTS], *[delta[n] for n in WEIGHTS],
            *[new_m[n] for n in WEIGHTS], *[new_v[n] for n in WEIGHTS])
```

```python
import functools

import numpy as np
import jax
import jax.numpy as jnp
from jax import lax
from jax.experimental import pallas as pl
from jax.experimental.pallas import tpu as pltpu

F32 = jnp.float32
BF16 = jnp.bfloat16

D_MODEL = 1024
DEPTH = 2
D_FF = 2816
D_CONV = 256
CONV_WIDTH = 31
CONV_HALO = 32
D_SB = 512
N_SB_HEADS = 8
D_RET = 256
N_RET_HEADS = 4
HEAD_DIM = 64
D_IN_PROJ = 3072
ROPE_BASE = 10000.0
EPS = 1e-6
N_DEV = 8

ADAM_LR = 0.001
ADAM_B1 = 0.9
ADAM_B2 = 0.999
ADAM_EPS = 1e-08
ADAM_WD = 0.01
ADAM_STEP = 10

VMEM_LIMIT = 56 * 1024 * 1024
ROW_TILE = 512
FF_TILE = 1408
SB_TILE = 256
RET_TILE = 256
CONV_TILE = 256

NT_DIMS = (((1,), (1,)), ((), ()))
TN_DIMS = (((0,), (0,)), ((), ()))


def _params(n_axes, vmem=VMEM_LIMIT):
    return pltpu.CompilerParams(dimension_semantics=("arbitrary",) * n_axes, vmem_limit_bytes=vmem)


def _dot(a, b):
    return jnp.dot(a, b, preferred_element_type=F32)


def _dot_nt(a, b):
    return lax.dot_general(a, b, NT_DIMS, preferred_element_type=F32)


def _dot_tn(a, b):
    return lax.dot_general(a, b, TN_DIMS, preferred_element_type=F32)


def _sigmoid(z):
    return 1.0 / (1.0 + jnp.exp(-z))


def _rms_stats(xv):
    r = lax.rsqrt(jnp.mean(xv * xv, axis=-1, keepdims=True) + EPS)
    return r, xv * r


def _rms_bwd(xv, g, dh):
    r, xhat = _rms_stats(xv)
    dxhat = dh * g
    dx = r * (dxhat - xhat * jnp.mean(dxhat * xhat, axis=-1, keepdims=True))
    dg = jnp.sum(dh * xhat, axis=0, keepdims=True)
    return dx, (xhat * g).astype(BF16), dg


def ffn_fwd(x, g, w_in, w_out, comm=None, tm=ROW_TILE):
    S = x.shape[0]
    nj = D_FF // FF_TILE

    def body(x_ref, g_ref, wg_ref, wu_ref, wo_ref, y_ref, gate_ref, up_ref, h_sc, acc_sc):
        j = pl.program_id(1)

        @pl.when(j == 0)
        def _():
            _, xhat = _rms_stats(x_ref[...])
            h_sc[...] = (xhat * g_ref[...]).astype(BF16)
            acc_sc[...] = jnp.zeros_like(acc_sc)

        h = h_sc[...]
        gt = _dot(h, wg_ref[...])
        up = _dot(h, wu_ref[...])
        gate_ref[...] = gt.astype(BF16)
        up_ref[...] = up.astype(BF16)
        hid = (gt * _sigmoid(gt) * up).astype(BF16)
        acc_sc[...] += _dot(hid, wo_ref[...])

        @pl.when(j == nj - 1)
        def _():
            y_ref[...] = x_ref[...] + 0.5 * acc_sc[...]

    return _call(
        body, (x, g, w_in, w_in, w_out), comm, name="ffn_fwd",
        grid=(S // tm, nj),
        in_specs=[
            pl.BlockSpec((tm, D_MODEL), lambda i, j: (i, 0)),
            pl.BlockSpec((1, D_MODEL), lambda i, j: (0, 0)),
            pl.BlockSpec((D_MODEL, FF_TILE), lambda i, j: (0, j)),
            pl.BlockSpec((D_MODEL, FF_TILE), lambda i, j: (0, j + nj)),
            pl.BlockSpec((FF_TILE, D_MODEL), lambda i, j: (j, 0)),
        ],
        out_specs=[
            pl.BlockSpec((tm, D_MODEL), lambda i, j: (i, 0)),
            pl.BlockSpec((tm, FF_TILE), lambda i, j: (i, j)),
            pl.BlockSpec((tm, FF_TILE), lambda i, j: (i, j)),
        ],
        out_shape=[
            jax.ShapeDtypeStruct((S, D_MODEL), F32),
            jax.ShapeDtypeStruct((S, D_FF), BF16),
            jax.ShapeDtypeStruct((S, D_FF), BF16),
        ],
        scratch_shapes=[pltpu.VMEM((tm, D_MODEL), BF16), pltpu.VMEM((tm, D_MODEL), F32)],
    )


def ffn_bwd(dy, x, g, gate, up, w_in, w_out, comm=None, tm=ROW_TILE // 2):
    S = x.shape[0]
    nj = D_FF // FF_TILE

    def body(dy_ref, x_ref, g_ref, gate_ref, up_ref, wg_ref, wu_ref, wo_ref,
             dx_ref, h_ref, dyh_ref, dgate_ref, dup_ref, hid_ref, dg_ref, d2_sc, dh_sc):
        i = pl.program_id(0)
        j = pl.program_id(1)

        @pl.when(j == 0)
        def _():
            d2 = (0.5 * dy_ref[...]).astype(BF16)
            d2_sc[...] = d2
            dyh_ref[...] = d2
            dh_sc[...] = jnp.zeros_like(dh_sc)

        dhid = _dot_nt(d2_sc[...], wo_ref[...])
        gt = gate_ref[...].astype(F32)
        u = up_ref[...].astype(F32)
        sig = _sigmoid(gt)
        sl = gt * sig
        dgate = (dhid * u * (sig * (1.0 + gt * (1.0 - sig)))).astype(BF16)
        dup = (dhid * sl).astype(BF16)
        dgate_ref[...] = dgate
        dup_ref[...] = dup
        hid_ref[...] = (sl * u).astype(BF16)
        dh_sc[...] += _dot_nt(dgate, wg_ref[...]) + _dot_nt(dup, wu_ref[...])

        @pl.when(j == nj - 1)
        def _():
            dx, h, dg = _rms_bwd(x_ref[...], g_ref[...], dh_sc[...])
            dx_ref[...] = dy_ref[...] + dx
            h_ref[...] = h

            @pl.when(i == 0)
            def _():
                dg_ref[...] = dg

            @pl.when(i > 0)
            def _():
                dg_ref[...] += dg

    row = lambda i, j: (i, 0)
    blk = lambda i, j: (i, j)
    return _call(
        body, (dy, x, g, gate, up, w_in, w_in, w_out), comm, name="ffn_bwd",
        grid=(S // tm, nj),
        in_specs=[
            pl.BlockSpec((tm, D_MODEL), row),
            pl.BlockSpec((tm, D_MODEL), row),
            pl.BlockSpec((1, D_MODEL), lambda i, j: (0, 0)),
            pl.BlockSpec((tm, FF_TILE), blk),
            pl.BlockSpec((tm, FF_TILE), blk),
            pl.BlockSpec((D_MODEL, FF_TILE), lambda i, j: (0, j)),
            pl.BlockSpec((D_MODEL, FF_TILE), lambda i, j: (0, j + nj)),
            pl.BlockSpec((FF_TILE, D_MODEL), lambda i, j: (j, 0)),
        ],
        out_specs=[
            pl.BlockSpec((tm, D_MODEL), row),
            pl.BlockSpec((tm, D_MODEL), row),
            pl.BlockSpec((tm, D_MODEL), row),
            pl.BlockSpec((tm, FF_TILE), blk),
            pl.BlockSpec((tm, FF_TILE), blk),
            pl.BlockSpec((tm, FF_TILE), blk),
            pl.BlockSpec((1, D_MODEL), lambda i, j: (0, 0)),
        ],
        out_shape=[
            jax.ShapeDtypeStruct((S, D_MODEL), F32),
            jax.ShapeDtypeStruct((S, D_MODEL), BF16),
            jax.ShapeDtypeStruct((S, D_MODEL), BF16),
            jax.ShapeDtypeStruct((S, D_FF), BF16),
            jax.ShapeDtypeStruct((S, D_FF), BF16),
            jax.ShapeDtypeStruct((S, D_FF), BF16),
            jax.ShapeDtypeStruct((1, D_MODEL), F32),
        ],
        scratch_shapes=[pltpu.VMEM((tm, D_MODEL), BF16), pltpu.VMEM((tm, D_MODEL), F32)],
    )


def matmul_tn(a, b, ta, tn, tk=ROW_TILE, name="matmul_tn"):
    S, ka = a.shape
    nb = b.shape[1]

    def body(a_ref, b_ref, o_ref):
        k = pl.program_id(2)

        @pl.when(k == 0)
        def _():
            o_ref[...] = jnp.zeros_like(o_ref)

        o_ref[...] += _dot_tn(a_ref[...], b_ref[...])

    return pl.pallas_call(
        body, name=name,
        grid=(ka // ta, nb // tn, S // tk),
        in_specs=[
            pl.BlockSpec((tk, ta), lambda i, j, k: (k, i)),
            pl.BlockSpec((tk, tn), lambda i, j, k: (k, j)),
        ],
        out_specs=pl.BlockSpec((ta, tn), lambda i, j, k: (i, j)),
        out_shape=jax.ShapeDtypeStruct((ka, nb), F32),
        compiler_params=_params(3),
    )(a, b)


def mix_in_fwd(x, g, w, tm=ROW_TILE):
    S = x.shape[0]

    def body(x_ref, g_ref, w_ref, o_ref):
        _, xhat = _rms_stats(x_ref[...])
        o_ref[...] = _dot((xhat * g_ref[...]).astype(BF16), w_ref[...])

    return pl.pallas_call(
        body, name="mix_in_fwd",
        grid=(S // tm,),
        in_specs=[
            pl.BlockSpec((tm, D_MODEL), lambda i: (i, 0)),
            pl.BlockSpec((1, D_MODEL), lambda i: (0, 0)),
            pl.BlockSpec((D_MODEL, D_IN_PROJ), lambda i: (0, 0)),
        ],
        out_specs=pl.BlockSpec((tm, D_IN_PROJ), lambda i: (i, 0)),
        out_shape=jax.ShapeDtypeStruct((S, D_IN_PROJ), F32),
        compiler_params=_params(1),
    )(x, g, w)


def mix_in_bwd(dproj, w, x, g, dy, tm=ROW_TILE):
    S = x.shape[0]

    def body(dp_ref, w_ref, x_ref, g_ref, dy_ref, dx_ref, h_ref, dg_ref):
        i = pl.program_id(0)
        dh = _dot_nt(dp_ref[...], w_ref[...])
        dx, h, dg = _rms_bwd(x_ref[...], g_ref[...], dh)
        dx_ref[...] = dy_ref[...] + dx
        h_ref[...] = h

        @pl.when(i == 0)
        def _():
            dg_ref[...] = dg

        @pl.when(i > 0)
        def _():
            dg_ref[...] += dg

    row = lambda i: (i, 0)
    return pl.pallas_call(
        body, name="mix_in_bwd",
        grid=(S // tm,),
        in_specs=[
            pl.BlockSpec((tm, D_IN_PROJ), row),
            pl.BlockSpec((D_MODEL, D_IN_PROJ), lambda i: (0, 0)),
            pl.BlockSpec((tm, D_MODEL), row),
            pl.BlockSpec((1, D_MODEL), lambda i: (0, 0)),
            pl.BlockSpec((tm, D_MODEL), row),
        ],
        out_specs=[
            pl.BlockSpec((tm, D_MODEL), row),
            pl.BlockSpec((tm, D_MODEL), row),
            pl.BlockSpec((1, D_MODEL), lambda i: (0, 0)),
        ],
        out_shape=[
            jax.ShapeDtypeStruct((S, D_MODEL), F32),
            jax.ShapeDtypeStruct((S, D_MODEL), BF16),
            jax.ShapeDtypeStruct((1, D_MODEL), F32),
        ],
        compiler_params=_params(1),
    )(dproj, w, x, g, dy)


def mix_out_fwd(ycat, w, x, tm=ROW_TILE):
    S = x.shape[0]

    def body(y_ref, w_ref, x_ref, o_ref):
        o_ref[...] = x_ref[...] + _dot(y_ref[...], w_ref[...])

    row = lambda i: (i, 0)
    return pl.pallas_call(
        body, name="mix_out_fwd",
        grid=(S // tm,),
        in_specs=[
            pl.BlockSpec((tm, D_MODEL), row),
            pl.BlockSpec((D_MODEL, D_MODEL), lambda i: (0, 0)),
            pl.BlockSpec((tm, D_MODEL), row),
        ],
        out_specs=pl.BlockSpec((tm, D_MODEL), row),
        out_shape=jax.ShapeDtypeStruct((S, D_MODEL), F32),
        compiler_params=_params(1),
    )(ycat, w, x)


def mix_out_bwd(dy, w, tm=ROW_TILE):
    S = dy.shape[0]

    def body(dy_ref, w_ref, o_ref, dyb_ref):
        d = dy_ref[...].astype(BF16)
        dyb_ref[...] = d
        o_ref[...] = _dot_nt(d, w_ref[...])

    row = lambda i: (i, 0)
    return pl.pallas_call(
        body, name="mix_out_bwd",
        grid=(S // tm,),
        in_specs=[
            pl.BlockSpec((tm, D_MODEL), row),
            pl.BlockSpec((D_MODEL, D_MODEL), lambda i: (0, 0)),
        ],
        out_specs=[pl.BlockSpec((tm, D_MODEL), row), pl.BlockSpec((tm, D_MODEL), row)],
        out_shape=[jax.ShapeDtypeStruct((S, D_MODEL), F32), jax.ShapeDtypeStruct((S, D_MODEL), BF16)],
        compiler_params=_params(1),
    )(dy, w)


def _conv_ln(ypre, ln_g, ln_b):
    mu = jnp.mean(ypre, axis=-1, keepdims=True)
    yc = ypre - mu
    rstd = lax.rsqrt(jnp.mean(yc * yc, axis=-1, keepdims=True) + EPS)
    yn = yc * rstd
    return yn, rstd, yn * ln_g + ln_b


def conv_fwd(proj, cw, cb, ln_g, ln_b, tm=CONV_TILE):
    S = proj.shape[0]
    hb = tm // CONV_HALO

    def body(a_ref, b_ref, ap_ref, bp_ref, cw_ref, cb_ref, g_ref, bb_ref, y_ref, ypre_ref, v_sc):
        i = pl.program_id(0)
        prev = ap_ref[...] * _sigmoid(bp_ref[...])
        v_sc[pl.ds(0, CONV_HALO), :] = jnp.where(i > 0, prev, 0.0)
        v_sc[pl.ds(CONV_HALO, tm), :] = a_ref[...] * _sigmoid(b_ref[...])
        acc = jnp.zeros((tm, D_CONV), F32)
        for j in range(CONV_WIDTH):
            acc = acc + cw_ref[pl.ds(j, 1), :] * v_sc[pl.ds(CONV_HALO - (CONV_WIDTH - 1) + j, tm), :]
        ypre = acc + cb_ref[...]
        ypre_ref[...] = ypre
        _, _, z = _conv_ln(ypre, g_ref[...], bb_ref[...])
        y_ref[...] = (z * _sigmoid(z)).astype(BF16)

    one = lambda i: (0, 0)
    return pl.pallas_call(
        body, name="conv_fwd",
        grid=(S // tm,),
        in_specs=[
            pl.BlockSpec((tm, D_CONV), lambda i: (i, 0)),
            pl.BlockSpec((tm, D_CONV), lambda i: (i, 1)),
            pl.BlockSpec((CONV_HALO, D_CONV), lambda i: (jnp.maximum(i * hb - 1, 0), 0)),
            pl.BlockSpec((CONV_HALO, D_CONV), lambda i: (jnp.maximum(i * hb - 1, 0), 1)),
            pl.BlockSpec((CONV_HALO, D_CONV), one),
            pl.BlockSpec((1, D_CONV), one),
            pl.BlockSpec((1, D_CONV), one),
            pl.BlockSpec((1, D_CONV), one),
        ],
        out_specs=[pl.BlockSpec((tm, D_CONV), lambda i: (i, 0)), pl.BlockSpec((tm, D_CONV), lambda i: (i, 0))],
        out_shape=[jax.ShapeDtypeStruct((S, D_CONV), BF16), jax.ShapeDtypeStruct((S, D_CONV), F32)],
        scratch_shapes=[pltpu.VMEM((tm + CONV_HALO, D_CONV), F32)],
        compiler_params=_params(1),
    )(proj, proj, proj, proj, cw, cb, ln_g, ln_b)


def conv_bwd(dyc, ypre, proj, cw, ln_g, ln_b, tm=CONV_TILE):
    S = ypre.shape[0]
    hb = tm // CONV_HALO
    nblk = S // tm
    last_halo = S // CONV_HALO - 1

    def dpre(dy, yp, g, bb):
        yn, rstd, z = _conv_ln(yp, g, bb)
        sg = _sigmoid(z)
        dz = dy * (sg * (1.0 + z * (1.0 - sg)))
        dyn = dz * g
        d = rstd * (dyn - jnp.mean(dyn, axis=-1, keepdims=True) - yn * jnp.mean(dyn * yn, axis=-1, keepdims=True))
        return d, dz * yn, dz

    def body(dy_ref, yp_ref, dyn_ref, ypn_ref, a_ref, b_ref, ap_ref, bp_ref, cw_ref, g_ref, bb_ref,
             du_ref, dcw_ref, dsm_ref, d_sc, v_sc):
        i = pl.program_id(0)
        g = g_ref[...]
        bb = bb_ref[...]
        d_main, dgn, dz = dpre(dy_ref[...], yp_ref[...], g, bb)
        d_next, _, _ = dpre(dyn_ref[...], ypn_ref[...], g, bb)
        d_sc[pl.ds(0, tm), :] = d_main
        d_sc[pl.ds(tm, CONV_HALO), :] = jnp.where(i < nblk - 1, d_next, 0.0)
        a = a_ref[...]
        sb = _sigmoid(b_ref[...])
        prev = ap_ref[...] * _sigmoid(bp_ref[...])
        v_sc[pl.ds(0, CONV_HALO), :] = jnp.where(i > 0, prev, 0.0)
        v_sc[pl.ds(CONV_HALO, tm), :] = a * sb

        @pl.when(i == 0)
        def _():
            dcw_ref[...] = jnp.zeros_like(dcw_ref)
            dsm_ref[...] = jnp.zeros_like(dsm_ref)

        dv = jnp.zeros((tm, D_CONV), F32)
        for j in range(CONV_WIDTH):
            dv = dv + cw_ref[pl.ds(j, 1), :] * d_sc[pl.ds(CONV_WIDTH - 1 - j, tm), :]
            shifted = v_sc[pl.ds(CONV_HALO - (CONV_WIDTH - 1) + j, tm), :]
            dcw_ref[pl.ds(j, 1), :] += jnp.sum(d_main * shifted, axis=0, keepdims=True)
        du_ref[:, pl.ds(0, D_CONV)] = dv * sb
        du_ref[:, pl.ds(D_CONV, D_CONV)] = dv * a * sb * (1.0 - sb)
        dsm_ref[pl.ds(0, 1), :] += jnp.sum(d_main, axis=0, keepdims=True)
        dsm_ref[pl.ds(1, 1), :] += jnp.sum(dgn, axis=0, keepdims=True)
        dsm_ref[pl.ds(2, 1), :] += jnp.sum(dz, axis=0, keepdims=True)

    one = lambda i: (0, 0)
    prev_map = lambda c: (lambda i: (jnp.maximum(i * hb - 1, 0), c))
    next_map = lambda i: (jnp.minimum((i + 1) * hb, last_halo), 0)
    return pl.pallas_call(
        body, name="conv_bwd",
        grid=(nblk,),
        in_specs=[
            pl.BlockSpec((tm, D_CONV), lambda i: (i, 0)),
            pl.BlockSpec((tm, D_CONV), lambda i: (i, 0)),
            pl.BlockSpec((CONV_HALO, D_CONV), next_map),
            pl.BlockSpec((CONV_HALO, D_CONV), next_map),
            pl.BlockSpec((tm, D_CONV), lambda i: (i, 0)),
            pl.BlockSpec((tm, D_CONV), lambda i: (i, 1)),
            pl.BlockSpec((CONV_HALO, D_CONV), prev_map(0)),
            pl.BlockSpec((CONV_HALO, D_CONV), prev_map(1)),
            pl.BlockSpec((CONV_HALO, D_CONV), one),
            pl.BlockSpec((1, D_CONV), one),
            pl.BlockSpec((1, D_CONV), one),
        ],
        out_specs=[
            pl.BlockSpec((tm, 2 * D_CONV), lambda i: (i, 0)),
            pl.BlockSpec((CONV_HALO, D_CONV), one),
            pl.BlockSpec((8, D_CONV), one),
        ],
        out_shape=[
            jax.ShapeDtypeStruct((S, 2 * D_CONV), F32),
            jax.ShapeDtypeStruct((CONV_HALO, D_CONV), F32),
            jax.ShapeDtypeStruct((8, D_CONV), F32),
        ],
        scratch_shapes=[pltpu.VMEM((tm + CONV_HALO, D_CONV), F32), pltpu.VMEM((tm + CONV_HALO, D_CONV), F32)],
        compiler_params=_params(1),
    )(dyc, ypre, dyc, ypre, proj, proj, proj, proj, cw, ln_g, ln_b)


SB_GROUPS = (8, 4, 2, 1)


def _softplus(z):
    neg_abs = lax.bitcast_convert_type(lax.bitcast_convert_type(z, jnp.uint32) | jnp.uint32(0x80000000), F32)
    return jnp.maximum(z, 0.0) + jnp.log(1.0 + jnp.exp(neg_abs))


def _run_groups(n, body):
    g0 = SB_GROUPS[0]
    nbig = lax.div(n, g0)

    def big(t, c):
        body([t * g0 + j for j in range(g0)])
        return c

    lax.fori_loop(0, nbig, big, 0)
    done = nbig * g0
    for g in SB_GROUPS[1:]:
        has = lax.rem(lax.div(n, g), 2) == 1

        @pl.when(has)
        def _(done=done, g=g):
            body([done + j for j in range(g)])

        done = done + jnp.where(has, g, 0)


def _rows(xs):
    return xs[0] if len(xs) == 1 else jnp.concatenate(xs, axis=0)


def sb_fwd(q, k, v, comm=None, T=SB_TILE):
    H, S, dh = q.shape

    def body(q_ref, k_ref, v_ref, o_ref, tot_ref, acc_sc, car_sc):
        qb = pl.program_id(1)
        qv = q_ref[...]
        row = lax.broadcasted_iota(jnp.int32, (T, T), 0)
        col = lax.broadcasted_iota(jnp.int32, (T, T), 1)
        tri = jnp.where(row >= col, 1.0, 0.0).astype(BF16)
        causal = col < row
        acc_sc[...] = jnp.zeros_like(acc_sc)
        car_sc[...] = jnp.zeros_like(car_sc)

        def logits(kb, masked):
            ks = k_ref[pl.ds(pl.multiple_of(kb * T, T), T), :]
            z = _dot_nt(qv, ks)
            nb = _softplus(z)
            if masked:
                nb = jnp.where(causal, nb, 0.0)
            return z, nb.astype(BF16)

        def group(kbs, masked):
            parts = [logits(kb, masked) for kb in kbs]
            pall = _dot(_rows([nb for _, nb in parts]), tri)
            carry = car_sc[...]
            out = None
            for j, kb in enumerate(kbs):
                p = pall[j * T:(j + 1) * T]
                vs = v_ref[pl.ds(pl.multiple_of(kb * T, T), T), :]
                w = jnp.exp((parts[j][0] - carry) - p)
                if masked:
                    w = jnp.where(causal, w, 0.0)
                o = _dot(w.astype(BF16), vs)
                out = o if out is None else out + o
                carry = carry + p[:, 0:1]
            acc_sc[...] += out
            car_sc[...] = carry

        group([qb], True)
        _run_groups(qb, lambda offs: group([qb - 1 - o for o in offs], False))
        o_ref[...] = acc_sc[...]
        tot_ref[...] = car_sc[...]

    return _call(
        body, (q, k, v), comm, name="sb_fwd",
        grid=(H, S // T),
        in_specs=[
            pl.BlockSpec((None, T, dh), lambda h, i: (h, i, 0)),
            pl.BlockSpec((None, S, dh), lambda h, i: (h, 0, 0)),
            pl.BlockSpec((None, S, dh), lambda h, i: (h, 0, 0)),
        ],
        out_specs=[
            pl.BlockSpec((None, T, dh), lambda h, i: (h, i, 0)),
            pl.BlockSpec((None, T, 1), lambda h, i: (h, i, 0)),
        ],
        out_shape=[jax.ShapeDtypeStruct((H, S, dh), F32), jax.ShapeDtypeStruct((H, S, 1), F32)],
        scratch_shapes=[pltpu.VMEM((T, dh), F32), pltpu.VMEM((T, 1), F32)],
    )


def sb_bwd(q, k, v, do, qt, dot, tot, comm=None, T=SB_TILE):
    H, S, dh = q.shape
    nt = S // T

    def body(q_ref, k_ref, v_ref, do_ref, qt_ref, dot_ref, tot_ref, dq_ref, dk_ref, dv_ref, acc_sc, rc_sc, gc_sc):
        qb = pl.program_id(1)
        qv = q_ref[...]
        dov = do_ref[...]
        qtv = qt_ref[...]
        dotv = dot_ref[...]
        row = lax.broadcasted_iota(jnp.int32, (T, T), 0)
        col = lax.broadcasted_iota(jnp.int32, (T, T), 1)
        before = jnp.where(row < col, 1.0, 0.0).astype(BF16)
        causal = col < row
        acc_sc[...] = jnp.zeros_like(acc_sc)
        rc_sc[...] = tot_ref[...]
        gc_sc[...] = jnp.zeros_like(gc_sc)

        @pl.when(qb == 0)
        def _():
            dk_ref[...] = jnp.zeros_like(dk_ref)
            dv_ref[...] = jnp.zeros_like(dv_ref)

        def first(kb, masked):
            start = pl.multiple_of(kb * T, T)
            z = _dot_nt(qv, k_ref[pl.ds(start, T), :])
            nb = _softplus(z)
            sig = jnp.exp(z - nb)
            if masked:
                nb = jnp.where(causal, nb, 0.0)
            dw = _dot_nt(dov, v_ref[pl.ds(start, T), :])
            return z, sig, nb.astype(BF16), dw

        def group(kbs, masked):
            parts = [first(kb, masked) for kb in kbs]
            pall = _dot(_rows([p[2] for p in parts]), before)
            rc = rc_sc[...]
            ws, gs, ghs = [], [], []
            for j in range(len(kbs)):
                z, _, nbh, dw = parts[j]
                p = pall[j * T:(j + 1) * T]
                w = jnp.exp((z - rc) + p)
                rc = rc - (p[:, T - 1:T] + nbh[:, T - 1:T].astype(F32))
                if masked:
                    w = jnp.where(causal, w, 0.0)
                g = dw * w
                ws.append(w.astype(BF16))
                gs.append(g)
                ghs.append(g.astype(BF16))
            glall = _dot(_rows(ghs), before)
            gc = gc_sc[...]
            dq = None
            for j, kb in enumerate(kbs):
                ks = k_ref[pl.ds(pl.multiple_of(kb * T, T), T), :]
                gl = glall[j * T:(j + 1) * T]
                dz = gs[j] - parts[j][1] * (gs[j] + (gl + gc))
                gc = gc + gl[:, T - 1:T] + ghs[j][:, T - 1:T].astype(F32)
                if masked:
                    dz = jnp.where(causal, dz, 0.0)
                dzb = dz.astype(BF16)
                d = _dot(dzb, ks)
                dq = d if dq is None else dq + d
                dk_ref[kb] += _dot(qtv, dzb)
                dv_ref[kb] += _dot(dotv, ws[j])
            acc_sc[...] += dq
            rc_sc[...] = rc
            gc_sc[...] = gc

        _run_groups(qb, lambda offs: group(offs, False))
        group([qb], True)
        dq_ref[...] = acc_sc[...]

    return _call(
        body, (q, k, v, do, qt, dot, tot), comm, name="sb_bwd",
        grid=(H, nt),
        in_specs=[
            pl.BlockSpec((None, T, dh), lambda h, i: (h, i, 0)),
            pl.BlockSpec((None, S, dh), lambda h, i: (h, 0, 0)),
            pl.BlockSpec((None, S, dh), lambda h, i: (h, 0, 0)),
            pl.BlockSpec((None, T, dh), lambda h, i: (h, i, 0)),
            pl.BlockSpec((None, dh, T), lambda h, i: (h, 0, i)),
            pl.BlockSpec((None, dh, T), lambda h, i: (h, 0, i)),
            pl.BlockSpec((None, T, 1), lambda h, i: (h, i, 0)),
        ],
        out_specs=[
            pl.BlockSpec((None, T, dh), lambda h, i: (h, i, 0)),
            pl.BlockSpec((None, nt, dh, T), lambda h, i: (h, 0, 0, 0)),
            pl.BlockSpec((None, nt, dh, T), lambda h, i: (h, 0, 0, 0)),
        ],
        out_shape=[jax.ShapeDtypeStruct((H, S, dh), F32), jax.ShapeDtypeStruct((H, nt, dh, T), F32),
                   jax.ShapeDtypeStruct((H, nt, dh, T), F32)],
        scratch_shapes=[pltpu.VMEM((T, dh), F32), pltpu.VMEM((T, 1), F32), pltpu.VMEM((T, 1), F32)],
    )


def _ret_tables(T=RET_TILE):
    hh = jnp.arange(N_RET_HEADS, dtype=F32)
    log_gamma = jnp.log1p(-jnp.exp2(-5.0 - hh))
    idx = jnp.arange(T, dtype=F32)
    diff = idx[:, None] - idx[None, :]
    ci = (jnp.arange(T) // 64)
    same = ci[:, None] == ci[None, :]
    earlier = ci[None, :] < ci[:, None]
    dist = jnp.where(same, jnp.abs(diff), diff)
    dmat = jnp.where(same | earlier, jnp.exp(log_gamma[:, None, None] * dist[None]), 0.0)
    ones = jnp.ones((1, 1, HEAD_DIM), F32)
    qdec = jnp.exp(log_gamma[:, None] * (idx + 1.0)[None, :])[:, :, None] * ones
    kdec = jnp.exp(log_gamma[:, None] * (T - 1.0 - idx)[None, :])[:, :, None] * ones
    bdec = jnp.exp(log_gamma * T)[:, None, None] * jnp.ones((1, HEAD_DIM, HEAD_DIM), F32)
    return dmat, qdec, kdec, bdec


def _rope_tables(S):
    half = HEAD_DIM // 2
    inv = 1.0 / (ROPE_BASE ** (jnp.arange(half, dtype=F32) / half))
    ang = jnp.arange(S).astype(F32)[:, None] * inv[None, :]
    c = jnp.cos(ang)
    s = jnp.sin(ang)
    cos = jnp.tile(jnp.concatenate([c, c], axis=1), (1, N_RET_HEADS))
    sin = jnp.tile(jnp.concatenate([-s, s], axis=1), (1, N_RET_HEADS))
    return cos, sin


def _swap_halves(x):
    n = x.shape[1]
    lane = lax.broadcasted_iota(jnp.int32, x.shape, 1)
    first = (lane % HEAD_DIM) < (HEAD_DIM // 2)
    return jnp.where(first, pltpu.roll(x, n - HEAD_DIM // 2, 1), pltpu.roll(x, HEAD_DIM // 2, 1))


def rope_fwd(proj, cos, sin, tm=ROW_TILE):
    S = proj.shape[0]

    def body(q_ref, k_ref, c_ref, s_ref, qo_ref, ko_ref):
        c = c_ref[...]
        s = s_ref[...]
        qv = q_ref[...]
        kv = k_ref[...]
        qo_ref[...] = ((qv * c + _swap_halves(qv) * s) * 0.125).astype(BF16)
        ko_ref[...] = (kv * c + _swap_halves(kv) * s).astype(BF16)

    row = lambda i: (i, 0)
    return pl.pallas_call(
        body, name="rope_fwd",
        grid=(S // tm,),
        in_specs=[
            pl.BlockSpec((tm, D_RET), lambda i: (i, 8)),
            pl.BlockSpec((tm, D_RET), lambda i: (i, 9)),
            pl.BlockSpec((tm, D_RET), row),
            pl.BlockSpec((tm, D_RET), row),
        ],
        out_specs=[pl.BlockSpec((tm, D_RET), row), pl.BlockSpec((tm, D_RET), row)],
        out_shape=[jax.ShapeDtypeStruct((S, D_RET), BF16)] * 2,
        compiler_params=_params(1),
    )(proj, proj, cos, sin)


def rope_bwd(dq, dk, cos, sin, tm=ROW_TILE):
    S = dq.shape[0]

    def body(dq_ref, dk_ref, c_ref, s_ref, qo_ref, ko_ref):
        c = c_ref[...]
        s = s_ref[...]
        dqv = dq_ref[...] * 0.125
        dkv = dk_ref[...]
        qo_ref[...] = dqv * c - _swap_halves(dqv) * s
        ko_ref[...] = dkv * c - _swap_halves(dkv) * s

    row = lambda i: (i, 0)
    return pl.pallas_call(
        body, name="rope_bwd",
        grid=(S // tm,),
        in_specs=[pl.BlockSpec((tm, D_RET), row)] * 4,
        out_specs=[pl.BlockSpec((tm, D_RET), row)] * 2,
        out_shape=[jax.ShapeDtypeStruct((S, D_RET), F32)] * 2,
        compiler_params=_params(1),
    )(dq, dk, cos, sin)


def ret_fwd(q, k, v, gate, ng, tables, T=RET_TILE):
    H, S, dh = q.shape
    dmat, qdec, kdec, bdec = tables

    def body(q_ref, k_ref, v_ref, gt_ref, ng_ref, dm_ref, qd_ref, kd_ref, bd_ref, o_ref, y_ref, st_ref, s_sc):
        n = pl.program_id(1)

        @pl.when(n == 0)
        def _():
            s_sc[...] = jnp.zeros_like(s_sc)

        qv = q_ref[...]
        kv = k_ref[...]
        vv = v_ref[...]
        state = s_sc[...]
        st_ref[...] = state
        sc = (_dot_nt(qv, kv) * dm_ref[...]).astype(BF16)
        qd = (qv.astype(F32) * qd_ref[...]).astype(BF16)
        y = _dot(sc, vv) + _dot(qd, state.astype(BF16))
        y_ref[...] = y
        kd = (kv.astype(F32) * kd_ref[...]).astype(BF16)
        s_sc[...] = bd_ref[...] * state + _dot_tn(kd, vv)
        mu = jnp.mean(y, axis=-1, keepdims=True)
        yc = y - mu
        yn = yc * lax.rsqrt(jnp.mean(yc * yc, axis=-1, keepdims=True) + EPS)
        gt = gt_ref[...]
        o_ref[...] = gt * _sigmoid(gt) * (yn * ng_ref[...])

    blk = lambda h, n: (h, n, 0)
    head = lambda h, n: (h, 0, 0)
    return pl.pallas_call(
        body, name="ret_fwd",
        grid=(H, S // T),
        in_specs=[
            pl.BlockSpec((None, T, dh), blk),
            pl.BlockSpec((None, T, dh), blk),
            pl.BlockSpec((None, T, dh), blk),
            pl.BlockSpec((None, T, dh), blk),
            pl.BlockSpec((None, 1, dh), head),
            pl.BlockSpec((None, T, T), head),
            pl.BlockSpec((None, T, dh), head),
            pl.BlockSpec((None, T, dh), head),
            pl.BlockSpec((None, dh, dh), head),
        ],
        out_specs=[
            pl.BlockSpec((None, T, dh), blk),
            pl.BlockSpec((None, T, dh), blk),
            pl.BlockSpec((None, None, dh, dh), lambda h, n: (h, n, 0, 0)),
        ],
        out_shape=[
            jax.ShapeDtypeStruct((H, S, dh), F32),
            jax.ShapeDtypeStruct((H, S, dh), F32),
            jax.ShapeDtypeStruct((H, S // T, dh, dh), F32),
        ],
        scratch_shapes=[pltpu.VMEM((dh, dh), F32)],
        compiler_params=_params(2),
    )(q, k, v, gate, ng, dmat, qdec, kdec, bdec)


def ret_bwd(do, q, k, v, gate, ng, y, states, tables, T=RET_TILE):
    H, S, dh = q.shape
    nb = S // T
    dmat, qdec, kdec, bdec = tables

    def body(do_ref, q_ref, k_ref, v_ref, gt_ref, ng_ref, y_ref, st_ref, dm_ref, qd_ref, kd_ref, bd_ref,
             dq_ref, dk_ref, dv_ref, dgt_ref, dng_ref, u_sc):
        n = pl.program_id(1)

        @pl.when(n == 0)
        def _():
            u_sc[...] = jnp.zeros_like(u_sc)
            dng_ref[...] = jnp.zeros_like(dng_ref)

        yv = y_ref[...]
        mu = jnp.mean(yv, axis=-1, keepdims=True)
        yc = yv - mu
        rstd = lax.rsqrt(jnp.mean(yc * yc, axis=-1, keepdims=True) + EPS)
        yn = yc * rstd
        gt = gt_ref[...]
        sg = _sigmoid(gt)
        ngv = ng_ref[...]
        dout = do_ref[...]
        dgt_ref[...] = dout * (yn * ngv) * (sg * (1.0 + gt * (1.0 - sg)))
        dn = dout * (gt * sg)
        dng_ref[...] += jnp.sum(dn * yn, axis=0, keepdims=True)
        dyn = dn * ngv
        dy = rstd * (dyn - jnp.mean(dyn, axis=-1, keepdims=True) - yn * jnp.mean(dyn * yn, axis=-1, keepdims=True))
        dyb = dy.astype(BF16)

        qv = q_ref[...]
        kv = k_ref[...]
        vv = v_ref[...]
        dm = dm_ref[...]
        qdt = qd_ref[...]
        kdt = kd_ref[...]
        sb = st_ref[...].astype(BF16)
        u = u_sc[...]
        ub = u.astype(BF16)
        dqk = (_dot_nt(dyb, vv) * dm).astype(BF16)
        sc = (_dot_nt(qv, kv) * dm).astype(BF16)
        qd = (qv.astype(F32) * qdt).astype(BF16)
        kd = (kv.astype(F32) * kdt).astype(BF16)
        dq_ref[...] = _dot(dqk, kv) + qdt * _dot_nt(dyb, sb)
        dk_ref[...] = _dot_tn(dqk, qv) + kdt * _dot_nt(vv, ub)
        dv_ref[...] = _dot_tn(sc, dyb) + _dot(kd, ub)
        u_sc[...] = bd_ref[...] * u + _dot_tn(qd, dyb)

    blk = lambda h, n: (h, nb - 1 - n, 0)
    head = lambda h, n: (h, 0, 0)
    return pl.pallas_call(
        body, name="ret_bwd",
        grid=(H, nb),
        in_specs=[
            pl.BlockSpec((None, T, dh), blk),
            pl.BlockSpec((None, T, dh), blk),
            pl.BlockSpec((None, T, dh), blk),
            pl.BlockSpec((None, T, dh), blk),
            pl.BlockSpec((None, T, dh), blk),
            pl.BlockSpec((None, 1, dh), head),
            pl.BlockSpec((None, T, dh), blk),
            pl.BlockSpec((None, None, dh, dh), lambda h, n: (h, nb - 1 - n, 0, 0)),
            pl.BlockSpec((None, T, T), head),
            pl.BlockSpec((None, T, dh), head),
            pl.BlockSpec((None, T, dh), head),
            pl.BlockSpec((None, dh, dh), head),
        ],
        out_specs=[
            pl.BlockSpec((None, T, dh), blk),
            pl.BlockSpec((None, T, dh), blk),
            pl.BlockSpec((None, T, dh), blk),
            pl.BlockSpec((None, T, dh), blk),
            pl.BlockSpec((None, 1, dh), head),
        ],
        out_shape=[jax.ShapeDtypeStruct((H, S, dh), F32)] * 4 + [jax.ShapeDtypeStruct((H, 1, dh), F32)],
        scratch_shapes=[pltpu.VMEM((dh, dh), F32)],
        compiler_params=_params(2),
    )(do, q, k, v, gate, ng, y, states, dmat, qdec, kdec, bdec)


def loss_head(x, g, target, tm=ROW_TILE):
    S = x.shape[0]

    def body(x_ref, g_ref, t_ref, loss_ref, dx_ref, dg_ref):
        i = pl.program_id(0)
        xv = x_ref[...]
        gv = g_ref[...]
        _, xhat = _rms_stats(xv)
        err = xhat * gv - t_ref[...]
        part = 0.5 * jnp.sum(jnp.mean(err * err, axis=-1, keepdims=True), axis=0, keepdims=True)
        dx, _, dg = _rms_bwd(xv, gv, err * (1.0 / D_MODEL))
        dx_ref[...] = dx
        part = jnp.broadcast_to(part, (1, 128))

        @pl.when(i == 0)
        def _():
            loss_ref[...] = part
            dg_ref[...] = dg

        @pl.when(i > 0)
        def _():
            loss_ref[...] += part
            dg_ref[...] += dg

    row = lambda i: (i, 0)
    one = lambda i: (0, 0)
    return pl.pallas_call(
        body, name="loss_head",
        grid=(S // tm,),
        in_specs=[pl.BlockSpec((tm, D_MODEL), row), pl.BlockSpec((1, D_MODEL), one), pl.BlockSpec((tm, D_MODEL), row)],
        out_specs=[pl.BlockSpec((1, 128), one), pl.BlockSpec((tm, D_MODEL), row), pl.BlockSpec((1, D_MODEL), one)],
        out_shape=[
            jax.ShapeDtypeStruct((1, 128), F32),
            jax.ShapeDtypeStruct((S, D_MODEL), F32),
            jax.ShapeDtypeStruct((1, D_MODEL), F32),
        ],
        compiler_params=_params(1),
    )(x, g, target)


def adamw(parts, w, m, v, tr):
    R, C = w.shape
    c1 = 1.0 / (1.0 - ADAM_B1 ** ADAM_STEP)
    c2 = 1.0 / (1.0 - ADAM_B2 ** ADAM_STEP)

    def body(p_ref, w_ref, m_ref, v_ref, g_ref, d_ref, mo_ref, vo_ref):
        g = p_ref[0]
        for d in range(1, N_DEV):
            g = g + p_ref[d]
        mn = ADAM_B1 * m_ref[...] + (1.0 - ADAM_B1) * g
        vn = ADAM_B2 * v_ref[...] + (1.0 - ADAM_B2) * (g * g)
        g_ref[...] = g
        mo_ref[...] = mn
        vo_ref[...] = vn
        d_ref[...] = -ADAM_LR * ((mn * c1) / (jnp.sqrt(vn * c2) + ADAM_EPS) + ADAM_WD * w_ref[...])

    row = lambda i: (i, 0)
    return pl.pallas_call(
        body, name="adamw",
        grid=(R // tr,),
        in_specs=[pl.BlockSpec((N_DEV, tr, C), lambda i: (0, i, 0))] + [pl.BlockSpec((tr, C), row)] * 3,
        out_specs=[pl.BlockSpec((tr, C), row)] * 4,
        out_shape=[jax.ShapeDtypeStruct((R, C), F32)] * 4,
        compiler_params=_params(1),
    )(parts, w, m, v)


def _my_id():
    return lax.axis_index("x") * 4 + lax.axis_index("y") * 2 + lax.axis_index("c")


def _peer(k):
    x, y, c = lax.axis_index("x"), lax.axis_index("y"), lax.axis_index("c")
    px = 1 - x if k & 4 else x
    py = 1 - y if k & 2 else y
    pc = 1 - c if k & 1 else c
    return (px, py, pc), px * 4 + py * 2 + pc


GATHER = "gather"
EXCHANGE = "exchange"


def _copies(kind, ins, outs, send_sems, recv_sems, local_sems, receive_side):
    me = _my_id()
    local, sends, recvs = [], [], []
    for t in range(len(ins)):
        src = ins[t] if kind == GATHER else ins[t].at[me]
        local.append(pltpu.make_async_copy(src, outs[t].at[me], local_sems.at[t]))
    for k in range(1, N_DEV):
        dev, pid = _peer(k)
        for t in range(len(ins)):
            sems = dict(send_sem=send_sems.at[t, k - 1], recv_sem=recv_sems.at[t, k - 1],
                        device_id=dev, device_id_type=pl.DeviceIdType.MESH)
            src = ins[t] if kind == GATHER else ins[t].at[pid]
            sends.append(pltpu.make_async_remote_copy(src_ref=src, dst_ref=outs[t].at[me], **sems))
            if receive_side:
                recvs.append(pltpu.make_async_remote_copy(src_ref=src, dst_ref=outs[t].at[pid], **sems))
    return local, sends, recvs


def _comm_start(kind, ins, outs, sems):
    local, sends, _ = _copies(kind, ins, outs, *sems, receive_side=False)
    for cp in local + sends:
        cp.start()


def _comm_wait(kind, ins, outs, sems):
    local, sends, recvs = _copies(kind, ins, outs, *sems, receive_side=True)
    for cp in recvs:
        cp.wait_recv()
    for cp in sends:
        cp.wait_send()
    for cp in local:
        cp.wait()


def _comm_shapes(kind, arrays):
    n = len(arrays)
    out_shape = [jax.ShapeDtypeStruct(((N_DEV,) + a.shape) if kind == GATHER else a.shape, a.dtype) for a in arrays]
    sems = [pltpu.SemaphoreType.DMA((n, N_DEV - 1)), pltpu.SemaphoreType.DMA((n, N_DEV - 1)),
            pltpu.SemaphoreType.DMA((n,))]
    return out_shape, sems


def communicate(kind, arrays):
    n = len(arrays)

    def body(*refs):
        ins, outs, sems = refs[:n], refs[n:2 * n], refs[2 * n:]
        _comm_start(kind, ins, outs, sems)
        _comm_wait(kind, ins, outs, sems)

    out_shape, sems = _comm_shapes(kind, arrays)
    any_spec = pl.BlockSpec(memory_space=pl.ANY)
    return pl.pallas_call(
        body, name=kind, in_specs=[any_spec] * n, out_specs=[any_spec] * n, out_shape=out_shape, scratch_shapes=sems,
    )(*arrays)


def _call(body, operands, comm, *, name, grid, in_specs, out_specs, out_shape, scratch_shapes):
    if comm is None:
        outs = pl.pallas_call(body, name=name, grid=grid, in_specs=in_specs, out_specs=out_specs, out_shape=out_shape,
                              scratch_shapes=scratch_shapes, compiler_params=_params(len(grid)))(*operands)
        return outs, []
    kind, arrays = comm
    n, n_in, n_out, n_sc = len(arrays), len(in_specs), len(out_specs), len(scratch_shapes)

    def carrier(*refs):
        ins, cins = refs[:n_in], refs[n_in:n_in + n]
        refs = refs[n_in + n:]
        outs, couts = refs[:n_out], refs[n_out:n_out + n]
        scratch, sems = refs[n_out + n:n_out + n + n_sc], refs[n_out + n + n_sc:]
        steps = [pl.program_id(a) for a in range(len(grid))]
        first = functools.reduce(jnp.logical_and, [s == 0 for s in steps])
        last = functools.reduce(jnp.logical_and, [s == g - 1 for s, g in zip(steps, grid)])

        @pl.when(first)
        def _():
            _comm_start(kind, cins, couts, sems)

        body(*ins, *outs, *scratch)

        @pl.when(last)
        def _():
            _comm_wait(kind, cins, couts, sems)

    comm_shape, sems = _comm_shapes(kind, arrays)
    any_spec = pl.BlockSpec(memory_space=pl.ANY)
    outs = pl.pallas_call(
        carrier, name=f"{name}_{kind}", grid=grid,
        in_specs=list(in_specs) + [any_spec] * n,
        out_specs=list(out_specs) + [any_spec] * n,
        out_shape=list(out_shape) + comm_shape,
        scratch_shapes=list(scratch_shapes) + sems,
        compiler_params=_params(len(grid)),
    )(*operands, *arrays)
    return outs[:n_out], outs[n_out:]


def _heads(t, n_heads):
    S = t.shape[0]
    return t.reshape(S, n_heads, HEAD_DIM).transpose(1, 0, 2)


def _unheads(t):
    H, S, _ = t.shape
    return t.transpose(1, 0, 2).reshape(S, H * HEAD_DIM)


def _row(v):
    return v.reshape(1, -1)


def _pad_taps(cw):
    return jnp.concatenate([cw, jnp.zeros((CONV_HALO - CONV_WIDTH, D_CONV), F32)], axis=0)


COL_SHARDED = ("ffn1_w_in", "mix_w_in", "ffn2_w_in")
ROW_SHARDED = ("ffn1_w_out", "mix_w_out", "ffn2_w_out")
SMALL = ("ffn1_norm", "mix_norm", "conv_b", "conv_ln_g", "conv_ln_b", "ret_norm_g", "ffn2_norm", "final_norm")
WEIGHTS = ("ffn1_norm", "ffn1_w_in", "ffn1_w_out", "mix_norm", "mix_w_in", "conv_w", "conv_b", "conv_ln_g",
           "conv_ln_b", "ret_norm_g", "mix_w_out", "ffn2_norm", "ffn2_w_in", "ffn2_w_out", "final_norm")
SMALL_ROWS = 32

FFN1 = ("ffn1_w_in", "ffn1_w_out")
MIX = ("mix_w_in", "mix_w_out")
FFN2 = ("ffn2_w_in", "ffn2_w_out")
STAGE_A = [(n, 0) for n in FFN1]
STAGE_B = [(n, 0) for n in MIX] + [("conv_w", None)]
STAGE_C = [(n, 0) for n in FFN2] + [(n, 1) for n in FFN1 + MIX + FFN2]
STAGE_D = [(n, 1) for n in FFN2]
STAGE_E = [(n, 1) for n in MIX + FFN1] + [(n, 0) for n in FFN2]
STAGE_F = [(n, 0) for n in MIX]
STAGE_G = [(n, 0) for n in FFN1]


def _natural(name, got):
    if name in COL_SHARDED:
        return got.transpose(1, 0, 2).reshape(D_MODEL, -1)
    if name in ROW_SHARDED:
        return got.reshape(-1, D_MODEL)
    return got.transpose(1, 2, 0, 3).reshape(DEPTH, CONV_WIDTH, D_CONV)


def _by_device(name, grad):
    if name in COL_SHARDED:
        return grad.reshape(D_MODEL, N_DEV, -1).transpose(1, 0, 2)
    return grad.reshape(N_DEV, -1, D_MODEL)


def _pack_small(g):
    flat = jnp.concatenate([g[n].reshape(-1) for n in SMALL] + [g["conv_w"].reshape(-1)])
    flat = jnp.concatenate([flat, jnp.zeros((SMALL_ROWS * D_MODEL - flat.shape[0],), F32)])
    return flat.reshape(SMALL_ROWS, D_MODEL)


def _unpack_small(buf, like):
    flat = buf.reshape(-1)
    out, off = {}, 0
    for n in SMALL:
        size = int(np.prod(like[n].shape))
        out[n] = flat[off:off + size].reshape(like[n].shape)
        off += size
    size = DEPTH * CONV_WIDTH * D_CONV
    out["conv_w"] = flat[off:off + size].reshape(DEPTH, CONV_WIDTH, D_CONV)
    return out


def kernel(x, ffn1_norm, ffn1_w_in, ffn1_w_out, mix_norm, mix_w_in, conv_w, conv_b, conv_ln_g, conv_ln_b, ret_norm_g, mix_w_out, ffn2_norm, ffn2_w_in, ffn2_w_out, final_norm, loss_target, m_ffn1_norm, m_ffn1_w_in, m_ffn1_w_out, m_mix_norm, m_mix_w_in, m_conv_w, m_conv_b, m_conv_ln_g, m_conv_ln_b, m_ret_norm_g, m_mix_w_out, m_ffn2_norm, m_ffn2_w_in, m_ffn2_w_out, m_final_norm, v_ffn1_norm, v_ffn1_w_in, v_ffn1_w_out, v_mix_norm, v_mix_w_in, v_conv_w, v_conv_b, v_conv_ln_g, v_conv_ln_b, v_ret_norm_g, v_mix_w_out, v_ffn2_norm, v_ffn2_w_in, v_ffn2_w_out, v_final_norm):
    args = locals()
    w = {n: args[n] for n in WEIGHTS}
    m = {n: args["m_" + n] for n in WEIGHTS}
    v = {n: args["v_" + n] for n in WEIGHTS}
    me = _my_id()
    x = x[0]
    target = loss_target[0]
    S = x.shape[0]
    cos, sin = _rope_tables(S)
    tables = _ret_tables()

    full = {}

    def gather(keys):
        return GATHER, [w["conv_w"] if n == "conv_w" else w[n][l].astype(BF16) for n, l in keys]

    def gathered(keys, got):
        for (n, l), g in zip(keys, got):
            full[(n, l)] = _natural(n, g)

    gathered(STAGE_A, communicate(*gather(STAGE_A)))

    saved = []
    for l in range(DEPTH):
        sv = {"x0": x}
        (x, sv["gate1"], sv["up1"]), got = ffn_fwd(x, _row(w["ffn1_norm"][l]), full[("ffn1_w_in", l)],
                                                   full[("ffn1_w_out", l)], gather(STAGE_B) if l == 0 else None)
        gathered(STAGE_B if l == 0 else [], got)
        sv["x1"] = x
        proj = mix_in_fwd(x, _row(w["mix_norm"][l]), full[("mix_w_in", l)])
        sv["proj"] = proj
        cw = _pad_taps(full[("conv_w", None)][l])
        y_conv, sv["ypre"] = conv_fwd(proj, cw, _row(w["conv_b"][l]), _row(w["conv_ln_g"][l]), _row(w["conv_ln_b"][l]))
        sv["q_sb"] = _heads((proj[:, 512:1024] * 0.125).astype(BF16), N_SB_HEADS)
        sv["k_sb"] = _heads(proj[:, 1024:1536].astype(BF16), N_SB_HEADS)
        sv["v_sb"] = _heads(proj[:, 1536:2048].astype(BF16), N_SB_HEADS)
        (o_sb, sv["tot"]), got = sb_fwd(sv["q_sb"], sv["k_sb"], sv["v_sb"], gather(STAGE_C) if l == 0 else None)
        gathered(STAGE_C if l == 0 else [], got)
        q_rot, k_rot = rope_fwd(proj, cos, sin)
        sv["q_r"] = _heads(q_rot, N_RET_HEADS)
        sv["k_r"] = _heads(k_rot, N_RET_HEADS)
        sv["v_r"] = _heads(proj[:, 2560:2816].astype(BF16), N_RET_HEADS)
        sv["g_r"] = _heads(proj[:, 2816:3072], N_RET_HEADS)
        ng = w["ret_norm_g"][l].reshape(N_RET_HEADS, 1, HEAD_DIM)
        o_r, sv["y_r"], sv["states"] = ret_fwd(sv["q_r"], sv["k_r"], sv["v_r"], sv["g_r"], ng, tables)
        sv["ycat"] = jnp.concatenate([y_conv, _unheads(o_sb).astype(BF16), _unheads(o_r).astype(BF16)], axis=1)
        x = mix_out_fwd(sv["ycat"], full[("mix_w_out", l)], x)
        sv["x2"] = x
        (x, sv["gate2"], sv["up2"]), _ = ffn_fwd(x, _row(w["ffn2_norm"][l]), full[("ffn2_w_in", l)],
                                                 full[("ffn2_w_out", l)])
        saved.append(sv)

    loss_acc, dx, dg_final = loss_head(x, _row(w["final_norm"]), target)
    loss = lax.psum(loss_acc[0, 0], ("x", "y", "c"))

    g = {"final_norm": dg_final.reshape(D_MODEL)}
    received = {}

    def exchange(keys, extra=()):
        return EXCHANGE, [_by_device(n, g[(n, l)]) for n, l in keys] + list(extra)

    def exchanged(keys, got):
        for key, p in zip(keys, got):
            received[key] = p

    def ffn_back(dx, x_in, gate, up, norm, names, l, comm=None):
        (dx, h, dyh, dgate, dup, hid, dg), got = ffn_bwd(dx, x_in, _row(norm), gate, up, full[(names[0], l)],
                                                         full[(names[1], l)], comm)
        g[(names[0], l)] = jnp.concatenate([
            matmul_tn(h, dgate, D_MODEL, FF_TILE, name="ffn_dw_gate"),
            matmul_tn(h, dup, D_MODEL, FF_TILE, name="ffn_dw_up")], axis=1)
        g[(names[1], l)] = matmul_tn(hid, dyh, FF_TILE, D_MODEL, name="ffn_dw_out")
        return dx, dg.reshape(D_MODEL), got

    for l in reversed(range(DEPTH)):
        sv = saved[l]
        dx, g[("ffn2_norm", l)], _ = ffn_back(dx, sv["x2"], sv["gate2"], sv["up2"], w["ffn2_norm"][l], FFN2, l)
        dycat, dxb = mix_out_bwd(dx, full[("mix_w_out", l)])
        g[("mix_w_out", l)] = matmul_tn(sv["ycat"], dxb, D_MODEL, D_MODEL, name="mix_dw_out")
        cw = _pad_taps(full[("conv_w", None)][l])
        du_conv, dcw, dsm = conv_bwd(dycat, sv["ypre"], sv["proj"], cw, _row(w["conv_ln_g"][l]), _row(w["conv_ln_b"][l]))
        g[("conv_w", l)] = dcw[:CONV_WIDTH]
        g[("conv_b", l)], g[("conv_ln_g", l)], g[("conv_ln_b", l)] = dsm[0], dsm[1], dsm[2]
        do_sb = _heads(dycat[:, 256:768].astype(BF16), N_SB_HEADS)
        stage = STAGE_D if l == DEPTH - 1 else STAGE_E
        (dq_sb, dk_t, dv_t), got = sb_bwd(sv["q_sb"], sv["k_sb"], sv["v_sb"], do_sb, sv["q_sb"].transpose(0, 2, 1),
                                          do_sb.transpose(0, 2, 1), sv["tot"], exchange(stage))
        exchanged(stage, got)
        dk_sb = dk_t.transpose(1, 3, 0, 2).reshape(S, D_SB)
        dv_sb = dv_t.transpose(1, 3, 0, 2).reshape(S, D_SB)
        do_r = _heads(dycat[:, 768:1024], N_RET_HEADS)
        ng = w["ret_norm_g"][l].reshape(N_RET_HEADS, 1, HEAD_DIM)
        dq_r, dk_r, dv_r, dg_r, dng = ret_bwd(do_r, sv["q_r"], sv["k_r"], sv["v_r"], sv["g_r"], ng, sv["y_r"],
                                              sv["states"], tables)
        g[("ret_norm_g", l)] = dng.reshape(D_RET)
        dq_rr, dk_rr = rope_bwd(_unheads(dq_r), _unheads(dk_r), cos, sin)
        dproj = jnp.concatenate([
            du_conv.astype(BF16),
            (_unheads(dq_sb) * 0.125).astype(BF16), dk_sb.astype(BF16), dv_sb.astype(BF16),
            dq_rr.astype(BF16), dk_rr.astype(BF16), _unheads(dv_r).astype(BF16), _unheads(dg_r).astype(BF16)], axis=1)
        dx, h, dg = mix_in_bwd(dproj, full[("mix_w_in", l)], sv["x1"], _row(w["mix_norm"][l]), dx)
        g[("mix_norm", l)] = dg.reshape(D_MODEL)
        g[("mix_w_in", l)] = matmul_tn(h, dproj, D_MODEL, D_MODEL, name="mix_dw_in")
        dx, g[("ffn1_norm", l)], got = ffn_back(dx, sv["x0"], sv["gate1"], sv["up1"], w["ffn1_norm"][l], FFN1, l,
                                                exchange(STAGE_F) if l == 0 else None)
        exchanged(STAGE_F if l == 0 else [], got)
    grad_x = dx

    small_names = [n for n in SMALL if n != "final_norm"] + ["conv_w"]
    gs = {n: jnp.stack([g[(n, l)] for l in range(DEPTH)], axis=0) for n in small_names}
    gs["final_norm"] = g["final_norm"]
    small = _pack_small(gs)
    got = communicate(*exchange(STAGE_G, [jnp.broadcast_to(small[None], (N_DEV, SMALL_ROWS, D_MODEL))]))
    exchanged(STAGE_G, got[:-1])

    grad, delta, new_m, new_v = {}, {}, {}, {}
    for n in COL_SHARDED + ROW_SHARDED:
        per_layer = []
        for l in range(DEPTH):
            p = received[(n, l)]
            per_layer.append(adamw(p, w[n][l], m[n][l], v[n][l], tr=p.shape[1] // 4))
        grad[n], delta[n], new_m[n], new_v[n] = [jnp.stack(o, axis=0) for o in zip(*per_layer)]

    def small_pack(d):
        mine = dict(d)
        cwf = jnp.zeros((DEPTH, CONV_WIDTH, D_CONV), F32)
        mine["conv_w"] = lax.dynamic_update_slice(cwf, d["conv_w"], (0, 0, me * (D_CONV // N_DEV)))
        return _pack_small(mine)

    outs = adamw(got[-1], small_pack(w), small_pack(m), small_pack(v), tr=SMALL_ROWS)
    for dst, o in zip((grad, delta, new_m, new_v), outs):
        un = _unpack_small(o, w)
        un["conv_w"] = lax.dynamic_slice(un["conv_w"], (0, 0, me * (D_CONV // N_DEV)),
                                         (DEPTH, CONV_WIDTH, D_CONV // N_DEV))
        dst.update(un)

    return (loss, grad_x[None], *[grad[n] for n in WEIGHTS], *[delta[n] for n in WEIGHTS],
            *[new_m[n] for n in WEIGHTS], *[new_v[n] for n in WEIGHTS])
```

```python
import functools

import numpy as np
import jax
import jax.numpy as jnp
from jax import lax
from jax.experimental import pallas as pl
from jax.experimental.pallas import tpu as pltpu

F32 = jnp.float32
BF16 = jnp.bfloat16

D_MODEL = 1024
DEPTH = 2
D_FF = 2816
D_CONV = 256
CONV_WIDTH = 31
CONV_HALO = 32
D_SB = 512
N_SB_HEADS = 8
D_RET = 256
N_RET_HEADS = 4
HEAD_DIM = 64
D_IN_PROJ = 3072
ROPE_BASE = 10000.0
EPS = 1e-6
N_DEV = 8

ADAM_LR = 0.001
ADAM_B1 = 0.9
ADAM_B2 = 0.999
ADAM_EPS = 1e-08
ADAM_WD = 0.01
ADAM_STEP = 10

VMEM_LIMIT = 56 * 1024 * 1024
ROW_TILE = 512
FF_TILE = 1408
SB_TILE = 256
RET_TILE = 256
CONV_TILE = 256

NT_DIMS = (((1,), (1,)), ((), ()))
TN_DIMS = (((0,), (0,)), ((), ()))


def _params(n_axes, vmem=VMEM_LIMIT):
    return pltpu.CompilerParams(dimension_semantics=("arbitrary",) * n_axes, vmem_limit_bytes=vmem)


def _dot(a, b):
    return jnp.dot(a, b, preferred_element_type=F32)


def _dot_nt(a, b):
    return lax.dot_general(a, b, NT_DIMS, preferred_element_type=F32)


def _dot_tn(a, b):
    return lax.dot_general(a, b, TN_DIMS, preferred_element_type=F32)


def _sigmoid(z):
    return 1.0 / (1.0 + jnp.exp(-z))


def _rms_stats(xv):
    r = lax.rsqrt(jnp.mean(xv * xv, axis=-1, keepdims=True) + EPS)
    return r, xv * r


def _rms_bwd(xv, g, dh):
    r, xhat = _rms_stats(xv)
    dxhat = dh * g
    dx = r * (dxhat - xhat * jnp.mean(dxhat * xhat, axis=-1, keepdims=True))
    dg = jnp.sum(dh * xhat, axis=0, keepdims=True)
    return dx, (xhat * g).astype(BF16), dg


def ffn_fwd(x, g, w_in, w_out, comm=None, tm=ROW_TILE):
    S = x.shape[0]
    nj = D_FF // FF_TILE

    def body(x_ref, g_ref, wg_ref, wu_ref, wo_ref, y_ref, gate_ref, up_ref, h_sc, acc_sc):
        j = pl.program_id(1)

        @pl.when(j == 0)
        def _():
            _, xhat = _rms_stats(x_ref[...])
            h_sc[...] = (xhat * g_ref[...]).astype(BF16)
            acc_sc[...] = jnp.zeros_like(acc_sc)

        h = h_sc[...]
        gt = _dot(h, wg_ref[...])
        up = _dot(h, wu_ref[...])
        gate_ref[...] = gt.astype(BF16)
        up_ref[...] = up.astype(BF16)
        hid = (gt * _sigmoid(gt) * up).astype(BF16)
        acc_sc[...] += _dot(hid, wo_ref[...])

        @pl.when(j == nj - 1)
        def _():
            y_ref[...] = x_ref[...] + 0.5 * acc_sc[...]

    return _call(
        body, (x, g, w_in, w_in, w_out), comm, name="ffn_fwd",
        grid=(S // tm, nj),
        in_specs=[
            pl.BlockSpec((tm, D_MODEL), lambda i, j: (i, 0)),
            pl.BlockSpec((1, D_MODEL), lambda i, j: (0, 0)),
            pl.BlockSpec((D_MODEL, FF_TILE), lambda i, j: (0, j)),
            pl.BlockSpec((D_MODEL, FF_TILE), lambda i, j: (0, j + nj)),
            pl.BlockSpec((FF_TILE, D_MODEL), lambda i, j: (j, 0)),
        ],
        out_specs=[
            pl.BlockSpec((tm, D_MODEL), lambda i, j: (i, 0)),
            pl.BlockSpec((tm, FF_TILE), lambda i, j: (i, j)),
            pl.BlockSpec((tm, FF_TILE), lambda i, j: (i, j)),
        ],
        out_shape=[
            jax.ShapeDtypeStruct((S, D_MODEL), F32),
            jax.ShapeDtypeStruct((S, D_FF), BF16),
            jax.ShapeDtypeStruct((S, D_FF), BF16),
        ],
        scratch_shapes=[pltpu.VMEM((tm, D_MODEL), BF16), pltpu.VMEM((tm, D_MODEL), F32)],
    )


def ffn_bwd(dy, x, g, gate, up, w_in, w_out, comm=None, tm=ROW_TILE // 2):
    S = x.shape[0]
    nj = D_FF // FF_TILE

    def body(dy_ref, x_ref, g_ref, gate_ref, up_ref, wg_ref, wu_ref, wo_ref,
             dx_ref, h_ref, dyh_ref, dgate_ref, dup_ref, hid_ref, dg_ref, d2_sc, dh_sc):
        i = pl.program_id(0)
        j = pl.program_id(1)

        @pl.when(j == 0)
        def _():
            d2 = (0.5 * dy_ref[...]).astype(BF16)
            d2_sc[...] = d2
            dyh_ref[...] = d2
            dh_sc[...] = jnp.zeros_like(dh_sc)

        dhid = _dot_nt(d2_sc[...], wo_ref[...])
        gt = gate_ref[...].astype(F32)
        u = up_ref[...].astype(F32)
        sig = _sigmoid(gt)
        sl = gt * sig
        dgate = (dhid * u * (sig * (1.0 + gt * (1.0 - sig)))).astype(BF16)
        dup = (dhid * sl).astype(BF16)
        dgate_ref[...] = dgate
        dup_ref[...] = dup
        hid_ref[...] = (sl * u).astype(BF16)
        dh_sc[...] += _dot_nt(dgate, wg_ref[...]) + _dot_nt(dup, wu_ref[...])

        @pl.when(j == nj - 1)
        def _():
            dx, h, dg = _rms_bwd(x_ref[...], g_ref[...], dh_sc[...])
            dx_ref[...] = dy_ref[...] + dx
            h_ref[...] = h

            @pl.when(i == 0)
            def _():
                dg_ref[...] = dg

            @pl.when(i > 0)
            def _():
                dg_ref[...] += dg

    row = lambda i, j: (i, 0)
    blk = lambda i, j: (i, j)
    return _call(
        body, (dy, x, g, gate, up, w_in, w_in, w_out), comm, name="ffn_bwd",
        grid=(S // tm, nj),
        in_specs=[
            pl.BlockSpec((tm, D_MODEL), row),
            pl.BlockSpec((tm, D_MODEL), row),
            pl.BlockSpec((1, D_MODEL), lambda i, j: (0, 0)),
            pl.BlockSpec((tm, FF_TILE), blk),
            pl.BlockSpec((tm, FF_TILE), blk),
            pl.BlockSpec((D_MODEL, FF_TILE), lambda i, j: (0, j)),
            pl.BlockSpec((D_MODEL, FF_TILE), lambda i, j: (0, j + nj)),
            pl.BlockSpec((FF_TILE, D_MODEL), lambda i, j: (j, 0)),
        ],
        out_specs=[
            pl.BlockSpec((tm, D_MODEL), row),
            pl.BlockSpec((tm, D_MODEL), row),
            pl.BlockSpec((tm, D_MODEL), row),
            pl.BlockSpec((tm, FF_TILE), blk),
            pl.BlockSpec((tm, FF_TILE), blk),
            pl.BlockSpec((tm, FF_TILE), blk),
            pl.BlockSpec((1, D_MODEL), lambda i, j: (0, 0)),
        ],
        out_shape=[
            jax.ShapeDtypeStruct((S, D_MODEL), F32),
            jax.ShapeDtypeStruct((S, D_MODEL), BF16),
            jax.ShapeDtypeStruct((S, D_MODEL), BF16),
            jax.ShapeDtypeStruct((S, D_FF), BF16),
            jax.ShapeDtypeStruct((S, D_FF), BF16),
            jax.ShapeDtypeStruct((S, D_FF), BF16),
            jax.ShapeDtypeStruct((1, D_MODEL), F32),
        ],
        scratch_shapes=[pltpu.VMEM((tm, D_MODEL), BF16), pltpu.VMEM((tm, D_MODEL), F32)],
    )


def matmul_tn(a, bs, ta, tn, tk=ROW_TILE, name="matmul_tn"):
    S, ka = a.shape
    nb = bs[0].shape[1]
    per = nb // tn

    def body(*refs):
        a_ref, b_refs, o_ref = refs[0], refs[1:-1], refs[-1]
        j = pl.program_id(1)
        k = pl.program_id(2)

        @pl.when(k == 0)
        def _():
            o_ref[...] = jnp.zeros_like(o_ref)

        for t, b_ref in enumerate(b_refs):
            @pl.when(lax.div(j, per) == t)
            def _(b_ref=b_ref):
                o_ref[...] += _dot_tn(a_ref[...], b_ref[...])

    def b_spec(t):
        def index(i, j, k):
            mine = lax.div(j, per) == t
            return jnp.where(mine, k, 0), jnp.where(mine, j - t * per, 0)
        return pl.BlockSpec((tk, tn), index)

    return pl.pallas_call(
        body, name=name,
        grid=(ka // ta, per * len(bs), S // tk),
        in_specs=[pl.BlockSpec((tk, ta), lambda i, j, k: (k, i))] + [b_spec(t) for t in range(len(bs))],
        out_specs=pl.BlockSpec((ta, tn), lambda i, j, k: (i, j)),
        out_shape=jax.ShapeDtypeStruct((ka, nb * len(bs)), F32),
        compiler_params=_params(3),
    )(a, *bs)


def mix_in_fwd(x, g, w, tm=ROW_TILE):
    S = x.shape[0]

    def body(x_ref, g_ref, w_ref, o_ref):
        _, xhat = _rms_stats(x_ref[...])
        o_ref[...] = _dot((xhat * g_ref[...]).astype(BF16), w_ref[...])

    return pl.pallas_call(
        body, name="mix_in_fwd",
        grid=(S // tm,),
        in_specs=[
            pl.BlockSpec((tm, D_MODEL), lambda i: (i, 0)),
            pl.BlockSpec((1, D_MODEL), lambda i: (0, 0)),
            pl.BlockSpec((D_MODEL, D_IN_PROJ), lambda i: (0, 0)),
        ],
        out_specs=pl.BlockSpec((tm, D_IN_PROJ), lambda i: (i, 0)),
        out_shape=jax.ShapeDtypeStruct((S, D_IN_PROJ), F32),
        compiler_params=_params(1),
    )(x, g, w)


def mix_in_bwd(dproj, w, x, g, dy, tm=ROW_TILE):
    S = x.shape[0]

    def body(dp_ref, w_ref, x_ref, g_ref, dy_ref, dx_ref, h_ref, dg_ref):
        i = pl.program_id(0)
        dh = _dot_nt(dp_ref[...], w_ref[...])
        dx, h, dg = _rms_bwd(x_ref[...], g_ref[...], dh)
        dx_ref[...] = dy_ref[...] + dx
        h_ref[...] = h

        @pl.when(i == 0)
        def _():
            dg_ref[...] = dg

        @pl.when(i > 0)
        def _():
            dg_ref[...] += dg

    row = lambda i: (i, 0)
    return pl.pallas_call(
        body, name="mix_in_bwd",
        grid=(S // tm,),
        in_specs=[
            pl.BlockSpec((tm, D_IN_PROJ), row),
            pl.BlockSpec((D_MODEL, D_IN_PROJ), lambda i: (0, 0)),
            pl.BlockSpec((tm, D_MODEL), row),
            pl.BlockSpec((1, D_MODEL), lambda i: (0, 0)),
            pl.BlockSpec((tm, D_MODEL), row),
        ],
        out_specs=[
            pl.BlockSpec((tm, D_MODEL), row),
            pl.BlockSpec((tm, D_MODEL), row),
            pl.BlockSpec((1, D_MODEL), lambda i: (0, 0)),
        ],
        out_shape=[
            jax.ShapeDtypeStruct((S, D_MODEL), F32),
            jax.ShapeDtypeStruct((S, D_MODEL), BF16),
            jax.ShapeDtypeStruct((1, D_MODEL), F32),
        ],
        compiler_params=_params(1),
    )(dproj, w, x, g, dy)


def mix_out_fwd(ycat, w, x, tm=ROW_TILE):
    S = x.shape[0]

    def body(y_ref, w_ref, x_ref, o_ref):
        o_ref[...] = x_ref[...] + _dot(y_ref[...], w_ref[...])

    row = lambda i: (i, 0)
    return pl.pallas_call(
        body, name="mix_out_fwd",
        grid=(S // tm,),
        in_specs=[
            pl.BlockSpec((tm, D_MODEL), row),
            pl.BlockSpec((D_MODEL, D_MODEL), lambda i: (0, 0)),
            pl.BlockSpec((tm, D_MODEL), row),
        ],
        out_specs=pl.BlockSpec((tm, D_MODEL), row),
        out_shape=jax.ShapeDtypeStruct((S, D_MODEL), F32),
        compiler_params=_params(1),
    )(ycat, w, x)


def mix_out_bwd(dy, w, tm=ROW_TILE):
    S = dy.shape[0]

    def body(dy_ref, w_ref, o_ref, dyb_ref):
        d = dy_ref[...].astype(BF16)
        dyb_ref[...] = d
        o_ref[...] = _dot_nt(d, w_ref[...])

    row = lambda i: (i, 0)
    return pl.pallas_call(
        body, name="mix_out_bwd",
        grid=(S // tm,),
        in_specs=[
            pl.BlockSpec((tm, D_MODEL), row),
            pl.BlockSpec((D_MODEL, D_MODEL), lambda i: (0, 0)),
        ],
        out_specs=[pl.BlockSpec((tm, D_MODEL), row), pl.BlockSpec((tm, D_MODEL), row)],
        out_shape=[jax.ShapeDtypeStruct((S, D_MODEL), F32), jax.ShapeDtypeStruct((S, D_MODEL), BF16)],
        compiler_params=_params(1),
    )(dy, w)


def _conv_ln(ypre, ln_g, ln_b):
    mu = jnp.mean(ypre, axis=-1, keepdims=True)
    yc = ypre - mu
    rstd = lax.rsqrt(jnp.mean(yc * yc, axis=-1, keepdims=True) + EPS)
    yn = yc * rstd
    return yn, rstd, yn * ln_g + ln_b


def conv_fwd(proj, cw, cb, ln_g, ln_b, tm=CONV_TILE):
    S = proj.shape[0]
    hb = tm // CONV_HALO

    def body(a_ref, b_ref, ap_ref, bp_ref, cw_ref, cb_ref, g_ref, bb_ref, y_ref, ypre_ref, v_sc):
        i = pl.program_id(0)
        prev = ap_ref[...] * _sigmoid(bp_ref[...])
        v_sc[pl.ds(0, CONV_HALO), :] = jnp.where(i > 0, prev, 0.0)
        v_sc[pl.ds(CONV_HALO, tm), :] = a_ref[...] * _sigmoid(b_ref[...])
        acc = jnp.zeros((tm, D_CONV), F32)
        for j in range(CONV_WIDTH):
            acc = acc + cw_ref[pl.ds(j, 1), :] * v_sc[pl.ds(CONV_HALO - (CONV_WIDTH - 1) + j, tm), :]
        ypre = acc + cb_ref[...]
        ypre_ref[...] = ypre
        _, _, z = _conv_ln(ypre, g_ref[...], bb_ref[...])
        y_ref[...] = (z * _sigmoid(z)).astype(BF16)

    one = lambda i: (0, 0)
    return pl.pallas_call(
        body, name="conv_fwd",
        grid=(S // tm,),
        in_specs=[
            pl.BlockSpec((tm, D_CONV), lambda i: (i, 0)),
            pl.BlockSpec((tm, D_CONV), lambda i: (i, 1)),
            pl.BlockSpec((CONV_HALO, D_CONV), lambda i: (jnp.maximum(i * hb - 1, 0), 0)),
            pl.BlockSpec((CONV_HALO, D_CONV), lambda i: (jnp.maximum(i * hb - 1, 0), 1)),
            pl.BlockSpec((CONV_HALO, D_CONV), one),
            pl.BlockSpec((1, D_CONV), one),
            pl.BlockSpec((1, D_CONV), one),
            pl.BlockSpec((1, D_CONV), one),
        ],
        out_specs=[pl.BlockSpec((tm, D_CONV), lambda i: (i, 0)), pl.BlockSpec((tm, D_CONV), lambda i: (i, 0))],
        out_shape=[jax.ShapeDtypeStruct((S, D_CONV), BF16), jax.ShapeDtypeStruct((S, D_CONV), F32)],
        scratch_shapes=[pltpu.VMEM((tm + CONV_HALO, D_CONV), F32)],
        compiler_params=_params(1),
    )(proj, proj, proj, proj, cw, cb, ln_g, ln_b)


def conv_bwd(dyc, ypre, proj, cw, ln_g, ln_b, tm=CONV_TILE):
    S = ypre.shape[0]
    hb = tm // CONV_HALO
    nblk = S // tm
    last_halo = S // CONV_HALO - 1

    def dpre(dy, yp, g, bb):
        yn, rstd, z = _conv_ln(yp, g, bb)
        sg = _sigmoid(z)
        dz = dy * (sg * (1.0 + z * (1.0 - sg)))
        dyn = dz * g
        d = rstd * (dyn - jnp.mean(dyn, axis=-1, keepdims=True) - yn * jnp.mean(dyn * yn, axis=-1, keepdims=True))
        return d, dz * yn, dz

    def body(dy_ref, yp_ref, dyn_ref, ypn_ref, a_ref, b_ref, ap_ref, bp_ref, cw_ref, g_ref, bb_ref,
             du_ref, dcw_ref, dsm_ref, d_sc, v_sc):
        i = pl.program_id(0)
        g = g_ref[...]
        bb = bb_ref[...]
        d_main, dgn, dz = dpre(dy_ref[...], yp_ref[...], g, bb)
        d_next, _, _ = dpre(dyn_ref[...], ypn_ref[...], g, bb)
        d_sc[pl.ds(0, tm), :] = d_main
        d_sc[pl.ds(tm, CONV_HALO), :] = jnp.where(i < nblk - 1, d_next, 0.0)
        a = a_ref[...]
        sb = _sigmoid(b_ref[...])
        prev = ap_ref[...] * _sigmoid(bp_ref[...])
        v_sc[pl.ds(0, CONV_HALO), :] = jnp.where(i > 0, prev, 0.0)
        v_sc[pl.ds(CONV_HALO, tm), :] = a * sb

        @pl.when(i == 0)
        def _():
            dcw_ref[...] = jnp.zeros_like(dcw_ref)
            dsm_ref[...] = jnp.zeros_like(dsm_ref)

        dv = jnp.zeros((tm, D_CONV), F32)
        for j in range(CONV_WIDTH):
            dv = dv + cw_ref[pl.ds(j, 1), :] * d_sc[pl.ds(CONV_WIDTH - 1 - j, tm), :]
            shifted = v_sc[pl.ds(CONV_HALO - (CONV_WIDTH - 1) + j, tm), :]
            dcw_ref[pl.ds(j, 1), :] += jnp.sum(d_main * shifted, axis=0, keepdims=True)
        du_ref[:, pl.ds(0, D_CONV)] = dv * sb
        du_ref[:, pl.ds(D_CONV, D_CONV)] = dv * a * sb * (1.0 - sb)
        dsm_ref[pl.ds(0, 1), :] += jnp.sum(d_main, axis=0, keepdims=True)
        dsm_ref[pl.ds(1, 1), :] += jnp.sum(dgn, axis=0, keepdims=True)
        dsm_ref[pl.ds(2, 1), :] += jnp.sum(dz, axis=0, keepdims=True)

    one = lambda i: (0, 0)
    prev_map = lambda c: (lambda i: (jnp.maximum(i * hb - 1, 0), c))
    next_map = lambda i: (jnp.minimum((i + 1) * hb, last_halo), 0)
    return pl.pallas_call(
        body, name="conv_bwd",
        grid=(nblk,),
        in_specs=[
            pl.BlockSpec((tm, D_CONV), lambda i: (i, 0)),
            pl.BlockSpec((tm, D_CONV), lambda i: (i, 0)),
            pl.BlockSpec((CONV_HALO, D_CONV), next_map),
            pl.BlockSpec((CONV_HALO, D_CONV), next_map),
            pl.BlockSpec((tm, D_CONV), lambda i: (i, 0)),
            pl.BlockSpec((tm, D_CONV), lambda i: (i, 1)),
            pl.BlockSpec((CONV_HALO, D_CONV), prev_map(0)),
            pl.BlockSpec((CONV_HALO, D_CONV), prev_map(1)),
            pl.BlockSpec((CONV_HALO, D_CONV), one),
            pl.BlockSpec((1, D_CONV), one),
            pl.BlockSpec((1, D_CONV), one),
        ],
        out_specs=[
            pl.BlockSpec((tm, 2 * D_CONV), lambda i: (i, 0)),
            pl.BlockSpec((CONV_HALO, D_CONV), one),
            pl.BlockSpec((8, D_CONV), one),
        ],
        out_shape=[
            jax.ShapeDtypeStruct((S, 2 * D_CONV), F32),
            jax.ShapeDtypeStruct((CONV_HALO, D_CONV), F32),
            jax.ShapeDtypeStruct((8, D_CONV), F32),
        ],
        scratch_shapes=[pltpu.VMEM((tm + CONV_HALO, D_CONV), F32), pltpu.VMEM((tm + CONV_HALO, D_CONV), F32)],
        compiler_params=_params(1),
    )(dyc, ypre, dyc, ypre, proj, proj, proj, proj, cw, ln_g, ln_b)


SB_GROUP = 8


def _softplus(z):
    neg_abs = lax.bitcast_convert_type(lax.bitcast_convert_type(z, jnp.uint32) | jnp.uint32(0x80000000), F32)
    return jnp.maximum(z, 0.0) + jnp.log(1.0 + jnp.exp(neg_abs))


def _full_groups(n, body):
    def step(t, c):
        body(t * SB_GROUP)
        return c

    lax.fori_loop(0, lax.div(n, SB_GROUP), step, 0)


def _last_group(n, body):
    r = lax.rem(n, SB_GROUP)
    for k in range(SB_GROUP):
        @pl.when(r == k)
        def _(k=k):
            body(k)


def _rows(xs):
    return xs[0] if len(xs) == 1 else jnp.concatenate(xs, axis=0)


def sb_fwd(q, k, v, comm=None, T=SB_TILE):
    H, S, dh = q.shape

    def body(q_ref, k_ref, v_ref, o_ref, tot_ref, acc_sc, car_sc):
        qb = pl.program_id(1)
        qv = q_ref[...]
        row = lax.broadcasted_iota(jnp.int32, (T, T), 0)
        col = lax.broadcasted_iota(jnp.int32, (T, T), 1)
        tri = jnp.where(row >= col, 1.0, 0.0).astype(BF16)
        causal = col < row
        acc_sc[...] = jnp.zeros_like(acc_sc)
        car_sc[...] = jnp.zeros_like(car_sc)

        def logits(kb, masked):
            ks = k_ref[pl.ds(pl.multiple_of(kb * T, T), T), :]
            z = _dot_nt(qv, ks)
            nb = _softplus(z)
            if masked:
                nb = jnp.where(causal, nb, 0.0)
            return z, nb.astype(BF16)

        def group(kbs, diag):
            parts = [logits(kb, d) for kb, d in zip(kbs, diag)]
            pall = _dot(_rows([nb for _, nb in parts]), tri)
            carry = car_sc[...]
            out = None
            for j, kb in enumerate(kbs):
                p = pall[j * T:(j + 1) * T]
                vs = v_ref[pl.ds(pl.multiple_of(kb * T, T), T), :]
                w = jnp.exp((parts[j][0] - carry) - p)
                if diag[j]:
                    w = jnp.where(causal, w, 0.0)
                o = _dot(w.astype(BF16), vs)
                out = o if out is None else out + o
                carry = carry + p[:, 0:1]
            acc_sc[...] += out
            car_sc[...] = carry

        _last_group(qb, lambda r: group([qb] + [qb - 1 - o for o in range(r)], [True] + [False] * r))
        rest = qb - lax.rem(qb, SB_GROUP)
        _full_groups(rest, lambda o: group([rest - 1 - o - j for j in range(SB_GROUP)], [False] * SB_GROUP))
        o_ref[...] = acc_sc[...]
        tot_ref[...] = car_sc[...]

    return _call(
        body, (q, k, v), comm, name="sb_fwd",
        grid=(H, S // T),
        in_specs=[
            pl.BlockSpec((None, T, dh), lambda h, i: (h, i, 0)),
            pl.BlockSpec((None, S, dh), lambda h, i: (h, 0, 0)),
            pl.BlockSpec((None, S, dh), lambda h, i: (h, 0, 0)),
        ],
        out_specs=[
            pl.BlockSpec((None, T, dh), lambda h, i: (h, i, 0)),
            pl.BlockSpec((None, T, 1), lambda h, i: (h, i, 0)),
        ],
        out_shape=[jax.ShapeDtypeStruct((H, S, dh), F32), jax.ShapeDtypeStruct((H, S, 1), F32)],
        scratch_shapes=[pltpu.VMEM((T, dh), F32), pltpu.VMEM((T, 1), F32)],
    )


def sb_bwd(q, k, v, do, qt, dot, tot, comm=None, T=SB_TILE):
    H, S, dh = q.shape
    nt = S // T

    def body(q_ref, k_ref, v_ref, do_ref, qt_ref, dot_ref, tot_ref, dq_ref, dk_ref, dv_ref, acc_sc, rc_sc, gc_sc):
        qb = pl.program_id(1)
        qv = q_ref[...]
        dov = do_ref[...]
        qtv = qt_ref[...]
        dotv = dot_ref[...]
        row = lax.broadcasted_iota(jnp.int32, (T, T), 0)
        col = lax.broadcasted_iota(jnp.int32, (T, T), 1)
        before = jnp.where(row < col, 1.0, 0.0).astype(BF16)
        causal = col < row
        acc_sc[...] = jnp.zeros_like(acc_sc)
        rc_sc[...] = tot_ref[...]
        gc_sc[...] = jnp.zeros_like(gc_sc)

        @pl.when(qb == 0)
        def _():
            dk_ref[...] = jnp.zeros_like(dk_ref)
            dv_ref[...] = jnp.zeros_like(dv_ref)

        def first(kb, masked):
            start = pl.multiple_of(kb * T, T)
            z = _dot_nt(qv, k_ref[pl.ds(start, T), :])
            nb = _softplus(z)
            sig = jnp.exp(z - nb)
            if masked:
                nb = jnp.where(causal, nb, 0.0)
            dw = _dot_nt(dov, v_ref[pl.ds(start, T), :])
            return z, sig, nb.astype(BF16), dw

        def group(kbs, diag):
            parts = [first(kb, d) for kb, d in zip(kbs, diag)]
            pall = _dot(_rows([p[2] for p in parts]), before)
            rc = rc_sc[...]
            ws, gs, ghs = [], [], []
            for j in range(len(kbs)):
                z, _, nbh, dw = parts[j]
                p = pall[j * T:(j + 1) * T]
                w = jnp.exp((z - rc) + p)
                rc = rc - (p[:, T - 1:T] + nbh[:, T - 1:T].astype(F32))
                if diag[j]:
                    w = jnp.where(causal, w, 0.0)
                g = dw * w
                ws.append(w.astype(BF16))
                gs.append(g)
                ghs.append(g.astype(BF16))
            glall = _dot(_rows(ghs), before)
            gc = gc_sc[...]
            dq = None
            for j, kb in enumerate(kbs):
                ks = k_ref[pl.ds(pl.multiple_of(kb * T, T), T), :]
                gl = glall[j * T:(j + 1) * T]
                dz = gs[j] - parts[j][1] * (gs[j] + (gl + gc))
                gc = gc + gl[:, T - 1:T] + ghs[j][:, T - 1:T].astype(F32)
                if diag[j]:
                    dz = jnp.where(causal, dz, 0.0)
                dzb = dz.astype(BF16)
                d = _dot(dzb, ks)
                dq = d if dq is None else dq + d
                dk_ref[kb] += _dot(qtv, dzb)
                dv_ref[kb] += _dot(dotv, ws[j])
            acc_sc[...] += dq
            rc_sc[...] = rc
            gc_sc[...] = gc

        _full_groups(qb, lambda o: group([o + j for j in range(SB_GROUP)], [False] * SB_GROUP))
        rest = qb - lax.rem(qb, SB_GROUP)
        _last_group(qb, lambda r: group([rest + j for j in range(r)] + [qb], [False] * r + [True]))
        dq_ref[...] = acc_sc[...]

    return _call(
        body, (q, k, v, do, qt, dot, tot), comm, name="sb_bwd",
        grid=(H, nt),
        in_specs=[
            pl.BlockSpec((None, T, dh), lambda h, i: (h, i, 0)),
            pl.BlockSpec((None, S, dh), lambda h, i: (h, 0, 0)),
            pl.BlockSpec((None, S, dh), lambda h, i: (h, 0, 0)),
            pl.BlockSpec((None, T, dh), lambda h, i: (h, i, 0)),
            pl.BlockSpec((None, dh, T), lambda h, i: (h, 0, i)),
            pl.BlockSpec((None, dh, T), lambda h, i: (h, 0, i)),
            pl.BlockSpec((None, T, 1), lambda h, i: (h, i, 0)),
        ],
        out_specs=[
            pl.BlockSpec((None, T, dh), lambda h, i: (h, i, 0)),
            pl.BlockSpec((None, nt, dh, T), lambda h, i: (h, 0, 0, 0)),
            pl.BlockSpec((None, nt, dh, T), lambda h, i: (h, 0, 0, 0)),
        ],
        out_shape=[jax.ShapeDtypeStruct((H, S, dh), F32), jax.ShapeDtypeStruct((H, nt, dh, T), F32),
                   jax.ShapeDtypeStruct((H, nt, dh, T), F32)],
        scratch_shapes=[pltpu.VMEM((T, dh), F32), pltpu.VMEM((T, 1), F32), pltpu.VMEM((T, 1), F32)],
    )


def _ret_tables(T=RET_TILE):
    hh = jnp.arange(N_RET_HEADS, dtype=F32)
    log_gamma = jnp.log1p(-jnp.exp2(-5.0 - hh))
    idx = jnp.arange(T, dtype=F32)
    diff = idx[:, None] - idx[None, :]
    ci = (jnp.arange(T) // 64)
    same = ci[:, None] == ci[None, :]
    earlier = ci[None, :] < ci[:, None]
    dist = jnp.where(same, jnp.abs(diff), diff)
    dmat = jnp.where(same | earlier, jnp.exp(log_gamma[:, None, None] * dist[None]), 0.0)
    ones = jnp.ones((1, 1, HEAD_DIM), F32)
    qdec = jnp.exp(log_gamma[:, None] * (idx + 1.0)[None, :])[:, :, None] * ones
    kdec = jnp.exp(log_gamma[:, None] * (T - 1.0 - idx)[None, :])[:, :, None] * ones
    bdec = jnp.exp(log_gamma * T)[:, None, None] * jnp.ones((1, HEAD_DIM, HEAD_DIM), F32)
    return dmat, qdec, kdec, bdec


def _rope_tables(S):
    half = HEAD_DIM // 2
    inv = 1.0 / (ROPE_BASE ** (jnp.arange(half, dtype=F32) / half))
    ang = jnp.arange(S).astype(F32)[:, None] * inv[None, :]
    c = jnp.cos(ang)
    s = jnp.sin(ang)
    cos = jnp.tile(jnp.concatenate([c, c], axis=1), (1, N_RET_HEADS))
    sin = jnp.tile(jnp.concatenate([-s, s], axis=1), (1, N_RET_HEADS))
    return cos, sin


def _swap_halves(x):
    n = x.shape[1]
    lane = lax.broadcasted_iota(jnp.int32, x.shape, 1)
    first = (lane % HEAD_DIM) < (HEAD_DIM // 2)
    return jnp.where(first, pltpu.roll(x, n - HEAD_DIM // 2, 1), pltpu.roll(x, HEAD_DIM // 2, 1))


def rope_fwd(proj, cos, sin, tm=ROW_TILE):
    S = proj.shape[0]

    def body(q_ref, k_ref, c_ref, s_ref, qo_ref, ko_ref):
        c = c_ref[...]
        s = s_ref[...]
        qv = q_ref[...]
        kv = k_ref[...]
        qo_ref[...] = ((qv * c + _swap_halves(qv) * s) * 0.125).astype(BF16)
        ko_ref[...] = (kv * c + _swap_halves(kv) * s).astype(BF16)

    row = lambda i: (i, 0)
    return pl.pallas_call(
        body, name="rope_fwd",
        grid=(S // tm,),
        in_specs=[
            pl.BlockSpec((tm, D_RET), lambda i: (i, 8)),
            pl.BlockSpec((tm, D_RET), lambda i: (i, 9)),
            pl.BlockSpec((tm, D_RET), row),
            pl.BlockSpec((tm, D_RET), row),
        ],
        out_specs=[pl.BlockSpec((tm, D_RET), row), pl.BlockSpec((tm, D_RET), row)],
        out_shape=[jax.ShapeDtypeStruct((S, D_RET), BF16)] * 2,
        compiler_params=_params(1),
    )(proj, proj, cos, sin)


def rope_bwd(dq, dk, cos, sin, tm=ROW_TILE):
    S = dq.shape[0]

    def body(dq_ref, dk_ref, c_ref, s_ref, qo_ref, ko_ref):
        c = c_ref[...]
        s = s_ref[...]
        dqv = dq_ref[...] * 0.125
        dkv = dk_ref[...]
        qo_ref[...] = dqv * c - _swap_halves(dqv) * s
        ko_ref[...] = dkv * c - _swap_halves(dkv) * s

    row = lambda i: (i, 0)
    return pl.pallas_call(
        body, name="rope_bwd",
        grid=(S // tm,),
        in_specs=[pl.BlockSpec((tm, D_RET), row)] * 4,
        out_specs=[pl.BlockSpec((tm, D_RET), row)] * 2,
        out_shape=[jax.ShapeDtypeStruct((S, D_RET), F32)] * 2,
        compiler_params=_params(1),
    )(dq, dk, cos, sin)


def ret_fwd(q, k, v, gate, ng, tables, T=RET_TILE):
    H, S, dh = q.shape
    dmat, qdec, kdec, bdec = tables

    def body(q_ref, k_ref, v_ref, gt_ref, ng_ref, dm_ref, qd_ref, kd_ref, bd_ref, o_ref, y_ref, st_ref, s_sc):
        n = pl.program_id(1)

        @pl.when(n == 0)
        def _():
            s_sc[...] = jnp.zeros_like(s_sc)

        qv = q_ref[...]
        kv = k_ref[...]
        vv = v_ref[...]
        state = s_sc[...]
        st_ref[...] = state
        sc = (_dot_nt(qv, kv) * dm_ref[...]).astype(BF16)
        qd = (qv.astype(F32) * qd_ref[...]).astype(BF16)
        y = _dot(sc, vv) + _dot(qd, state.astype(BF16))
        y_ref[...] = y
        kd = (kv.astype(F32) * kd_ref[...]).astype(BF16)
        s_sc[...] = bd_ref[...] * state + _dot_tn(kd, vv)
        mu = jnp.mean(y, axis=-1, keepdims=True)
        yc = y - mu
        yn = yc * lax.rsqrt(jnp.mean(yc * yc, axis=-1, keepdims=True) + EPS)
        gt = gt_ref[...]
        o_ref[...] = gt * _sigmoid(gt) * (yn * ng_ref[...])

    blk = lambda h, n: (h, n, 0)
    head = lambda h, n: (h, 0, 0)
    return pl.pallas_call(
        body, name="ret_fwd",
        grid=(H, S // T),
        in_specs=[
            pl.BlockSpec((None, T, dh), blk),
            pl.BlockSpec((None, T, dh), blk),
            pl.BlockSpec((None, T, dh), blk),
            pl.BlockSpec((None, T, dh), blk),
            pl.BlockSpec((None, 1, dh), head),
            pl.BlockSpec((None, T, T), head),
            pl.BlockSpec((None, T, dh), head),
            pl.BlockSpec((None, T, dh), head),
            pl.BlockSpec((None, dh, dh), head),
        ],
        out_specs=[
            pl.BlockSpec((None, T, dh), blk),
            pl.BlockSpec((None, T, dh), blk),
            pl.BlockSpec((None, None, dh, dh), lambda h, n: (h, n, 0, 0)),
        ],
        out_shape=[
            jax.ShapeDtypeStruct((H, S, dh), F32),
            jax.ShapeDtypeStruct((H, S, dh), F32),
            jax.ShapeDtypeStruct((H, S // T, dh, dh), F32),
        ],
        scratch_shapes=[pltpu.VMEM((dh, dh), F32)],
        compiler_params=_params(2),
    )(q, k, v, gate, ng, dmat, qdec, kdec, bdec)


def ret_bwd(do, q, k, v, gate, ng, y, states, tables, T=RET_TILE):
    H, S, dh = q.shape
    nb = S // T
    dmat, qdec, kdec, bdec = tables

    def body(do_ref, q_ref, k_ref, v_ref, gt_ref, ng_ref, y_ref, st_ref, dm_ref, qd_ref, kd_ref, bd_ref,
             dq_ref, dk_ref, dv_ref, dgt_ref, dng_ref, u_sc):
        n = pl.program_id(1)

        @pl.when(n == 0)
        def _():
            u_sc[...] = jnp.zeros_like(u_sc)
            dng_ref[...] = jnp.zeros_like(dng_ref)

        yv = y_ref[...]
        mu = jnp.mean(yv, axis=-1, keepdims=True)
        yc = yv - mu
        rstd = lax.rsqrt(jnp.mean(yc * yc, axis=-1, keepdims=True) + EPS)
        yn = yc * rstd
        gt = gt_ref[...]
        sg = _sigmoid(gt)
        ngv = ng_ref[...]
        dout = do_ref[...]
        dgt_ref[...] = dout * (yn * ngv) * (sg * (1.0 + gt * (1.0 - sg)))
        dn = dout * (gt * sg)
        dng_ref[...] += jnp.sum(dn * yn, axis=0, keepdims=True)
        dyn = dn * ngv
        dy = rstd * (dyn - jnp.mean(dyn, axis=-1, keepdims=True) - yn * jnp.mean(dyn * yn, axis=-1, keepdims=True))
        dyb = dy.astype(BF16)

        qv = q_ref[...]
        kv = k_ref[...]
        vv = v_ref[...]
        dm = dm_ref[...]
        qdt = qd_ref[...]
        kdt = kd_ref[...]
        sb = st_ref[...].astype(BF16)
        u = u_sc[...]
        ub = u.astype(BF16)
        dqk = (_dot_nt(dyb, vv) * dm).astype(BF16)
        sc = (_dot_nt(qv, kv) * dm).astype(BF16)
        qd = (qv.astype(F32) * qdt).astype(BF16)
        kd = (kv.astype(F32) * kdt).astype(BF16)
        dq_ref[...] = _dot(dqk, kv) + qdt * _dot_nt(dyb, sb)
        dk_ref[...] = _dot_tn(dqk, qv) + kdt * _dot_nt(vv, ub)
        dv_ref[...] = _dot_tn(sc, dyb) + _dot(kd, ub)
        u_sc[...] = bd_ref[...] * u + _dot_tn(qd, dyb)

    blk = lambda h, n: (h, nb - 1 - n, 0)
    head = lambda h, n: (h, 0, 0)
    return pl.pallas_call(
        body, name="ret_bwd",
        grid=(H, nb),
        in_specs=[
            pl.BlockSpec((None, T, dh), blk),
            pl.BlockSpec((None, T, dh), blk),
            pl.BlockSpec((None, T, dh), blk),
            pl.BlockSpec((None, T, dh), blk),
            pl.BlockSpec((None, T, dh), blk),
            pl.BlockSpec((None, 1, dh), head),
            pl.BlockSpec((None, T, dh), blk),
            pl.BlockSpec((None, None, dh, dh), lambda h, n: (h, nb - 1 - n, 0, 0)),
            pl.BlockSpec((None, T, T), head),
            pl.BlockSpec((None, T, dh), head),
            pl.BlockSpec((None, T, dh), head),
            pl.BlockSpec((None, dh, dh), head),
        ],
        out_specs=[
            pl.BlockSpec((None, T, dh), blk),
            pl.BlockSpec((None, T, dh), blk),
            pl.BlockSpec((None, T, dh), blk),
            pl.BlockSpec((None, T, dh), blk),
            pl.BlockSpec((None, 1, dh), head),
        ],
        out_shape=[jax.ShapeDtypeStruct((H, S, dh), F32)] * 4 + [jax.ShapeDtypeStruct((H, 1, dh), F32)],
        scratch_shapes=[pltpu.VMEM((dh, dh), F32)],
        compiler_params=_params(2),
    )(do, q, k, v, gate, ng, y, states, dmat, qdec, kdec, bdec)


def loss_head(x, g, target, tm=ROW_TILE):
    S = x.shape[0]

    def body(x_ref, g_ref, t_ref, loss_ref, dx_ref, dg_ref):
        i = pl.program_id(0)
        xv = x_ref[...]
        gv = g_ref[...]
        _, xhat = _rms_stats(xv)
        err = xhat * gv - t_ref[...]
        part = 0.5 * jnp.sum(jnp.mean(err * err, axis=-1, keepdims=True), axis=0, keepdims=True)
        dx, _, dg = _rms_bwd(xv, gv, err * (1.0 / D_MODEL))
        dx_ref[...] = dx
        part = jnp.broadcast_to(part, (1, 128))

        @pl.when(i == 0)
        def _():
            loss_ref[...] = part
            dg_ref[...] = dg

        @pl.when(i > 0)
        def _():
            loss_ref[...] += part
            dg_ref[...] += dg

    row = lambda i: (i, 0)
    one = lambda i: (0, 0)
    return pl.pallas_call(
        body, name="loss_head",
        grid=(S // tm,),
        in_specs=[pl.BlockSpec((tm, D_MODEL), row), pl.BlockSpec((1, D_MODEL), one), pl.BlockSpec((tm, D_MODEL), row)],
        out_specs=[pl.BlockSpec((1, 128), one), pl.BlockSpec((tm, D_MODEL), row), pl.BlockSpec((1, D_MODEL), one)],
        out_shape=[
            jax.ShapeDtypeStruct((1, 128), F32),
            jax.ShapeDtypeStruct((S, D_MODEL), F32),
            jax.ShapeDtypeStruct((1, D_MODEL), F32),
        ],
        compiler_params=_params(1),
    )(x, g, target)


def adamw(parts, w, m, v, tr):
    L, R, C = w.shape
    nr = R // tr
    c1 = 1.0 / (1.0 - ADAM_B1 ** ADAM_STEP)
    c2 = 1.0 / (1.0 - ADAM_B2 ** ADAM_STEP)

    def body(*refs):
        p_refs = refs[:L]
        w_ref, m_ref, v_ref, g_ref, d_ref, mo_ref, vo_ref = refs[L:]
        l = pl.program_id(0)
        g = None
        for d in range(N_DEV):
            pd = p_refs[0][d].astype(F32)
            for ll in range(1, L):
                pd = jnp.where(l == ll, p_refs[ll][d].astype(F32), pd)
            g = pd if g is None else g + pd
        mn = ADAM_B1 * m_ref[...] + (1.0 - ADAM_B1) * g
        vn = ADAM_B2 * v_ref[...] + (1.0 - ADAM_B2) * (g * g)
        g_ref[...] = g
        mo_ref[...] = mn
        vo_ref[...] = vn
        d_ref[...] = -ADAM_LR * ((mn * c1) / (jnp.sqrt(vn * c2) + ADAM_EPS) + ADAM_WD * w_ref[...])

    def part_spec(ll):
        return pl.BlockSpec((N_DEV, tr, C), lambda l, i: (0, jnp.where(l == ll, i, jnp.where(l < ll, 0, nr - 1)), 0))

    blk = pl.BlockSpec((None, tr, C), lambda l, i: (l, i, 0))
    return pl.pallas_call(
        body, name="adamw",
        grid=(L, nr),
        in_specs=[part_spec(ll) for ll in range(L)] + [blk] * 3,
        out_specs=[blk] * 4,
        out_shape=[jax.ShapeDtypeStruct((L, R, C), F32)] * 4,
        compiler_params=_params(2),
    )(*parts, w, m, v)


def _my_id():
    return lax.axis_index("x") * 4 + lax.axis_index("y") * 2 + lax.axis_index("c")


def _peer(k):
    x, y, c = lax.axis_index("x"), lax.axis_index("y"), lax.axis_index("c")
    px = 1 - x if k & 4 else x
    py = 1 - y if k & 2 else y
    pc = 1 - c if k & 1 else c
    return (px, py, pc), px * 4 + py * 2 + pc


GATHER = "gather"
EXCHANGE = "exchange"


def _copies(kind, ins, outs, send_sems, recv_sems, local_sems, receive_side):
    me = _my_id()
    local, sends, recvs = [], [], []
    for t in range(len(ins)):
        src = ins[t] if kind == GATHER else ins[t].at[me]
        local.append(pltpu.make_async_copy(src, outs[t].at[me], local_sems.at[t]))
    for k in range(1, N_DEV):
        dev, pid = _peer(k)
        for t in range(len(ins)):
            sems = dict(send_sem=send_sems.at[t, k - 1], recv_sem=recv_sems.at[t, k - 1],
                        device_id=dev, device_id_type=pl.DeviceIdType.MESH)
            src = ins[t] if kind == GATHER else ins[t].at[pid]
            sends.append(pltpu.make_async_remote_copy(src_ref=src, dst_ref=outs[t].at[me], **sems))
            if receive_side:
                recvs.append(pltpu.make_async_remote_copy(src_ref=src, dst_ref=outs[t].at[pid], **sems))
    return local, sends, recvs


def _comm_start(kind, ins, outs, sems):
    local, sends, _ = _copies(kind, ins, outs, *sems, receive_side=False)
    for cp in local + sends:
        cp.start()


def _comm_wait(kind, ins, outs, sems):
    local, sends, recvs = _copies(kind, ins, outs, *sems, receive_side=True)
    for cp in recvs:
        cp.wait_recv()
    for cp in sends:
        cp.wait_send()
    for cp in local:
        cp.wait()


def _comm_shapes(kind, arrays):
    n = len(arrays)
    out_shape = [jax.ShapeDtypeStruct(((N_DEV,) + a.shape) if kind == GATHER else a.shape, a.dtype) for a in arrays]
    sems = [pltpu.SemaphoreType.DMA((n, N_DEV - 1)), pltpu.SemaphoreType.DMA((n, N_DEV - 1)),
            pltpu.SemaphoreType.DMA((n,))]
    return out_shape, sems


def communicate(kind, arrays):
    n = len(arrays)

    def body(*refs):
        ins, outs, sems = refs[:n], refs[n:2 * n], refs[2 * n:]
        _comm_start(kind, ins, outs, sems)
        _comm_wait(kind, ins, outs, sems)

    out_shape, sems = _comm_shapes(kind, arrays)
    any_spec = pl.BlockSpec(memory_space=pl.ANY)
    return pl.pallas_call(
        body, name=kind, in_specs=[any_spec] * n, out_specs=[any_spec] * n, out_shape=out_shape, scratch_shapes=sems,
    )(*arrays)


def _call(body, operands, comm, *, name, grid, in_specs, out_specs, out_shape, scratch_shapes):
    if comm is None:
        outs = pl.pallas_call(body, name=name, grid=grid, in_specs=in_specs, out_specs=out_specs, out_shape=out_shape,
                              scratch_shapes=scratch_shapes, compiler_params=_params(len(grid)))(*operands)
        return outs, []
    kind, arrays = comm
    n, n_in, n_out, n_sc = len(arrays), len(in_specs), len(out_specs), len(scratch_shapes)

    def carrier(*refs):
        ins, cins = refs[:n_in], refs[n_in:n_in + n]
        refs = refs[n_in + n:]
        outs, couts = refs[:n_out], refs[n_out:n_out + n]
        scratch, sems = refs[n_out + n:n_out + n + n_sc], refs[n_out + n + n_sc:]
        steps = [pl.program_id(a) for a in range(len(grid))]
        first = functools.reduce(jnp.logical_and, [s == 0 for s in steps])
        last = functools.reduce(jnp.logical_and, [s == g - 1 for s, g in zip(steps, grid)])

        @pl.when(first)
        def _():
            _comm_start(kind, cins, couts, sems)

        body(*ins, *outs, *scratch)

        @pl.when(last)
        def _():
            _comm_wait(kind, cins, couts, sems)

    comm_shape, sems = _comm_shapes(kind, arrays)
    any_spec = pl.BlockSpec(memory_space=pl.ANY)
    outs = pl.pallas_call(
        carrier, name=f"{name}_{kind}", grid=grid,
        in_specs=list(in_specs) + [any_spec] * n,
        out_specs=list(out_specs) + [any_spec] * n,
        out_shape=list(out_shape) + comm_shape,
        scratch_shapes=list(scratch_shapes) + sems,
        compiler_params=_params(len(grid)),
    )(*operands, *arrays)
    return outs[:n_out], outs[n_out:]


def _heads(t, n_heads):
    S = t.shape[0]
    return t.reshape(S, n_heads, HEAD_DIM).transpose(1, 0, 2)


def _unheads(t):
    H, S, _ = t.shape
    return t.transpose(1, 0, 2).reshape(S, H * HEAD_DIM)


def _row(v):
    return v.reshape(1, -1)


def _pad_taps(cw):
    return jnp.concatenate([cw, jnp.zeros((CONV_HALO - CONV_WIDTH, D_CONV), F32)], axis=0)


COL_SHARDED = ("ffn1_w_in", "mix_w_in", "ffn2_w_in")
ROW_SHARDED = ("ffn1_w_out", "mix_w_out", "ffn2_w_out")
SMALL = ("ffn1_norm", "mix_norm", "conv_b", "conv_ln_g", "conv_ln_b", "ret_norm_g", "ffn2_norm", "final_norm")
WEIGHTS = ("ffn1_norm", "ffn1_w_in", "ffn1_w_out", "mix_norm", "mix_w_in", "conv_w", "conv_b", "conv_ln_g",
           "conv_ln_b", "ret_norm_g", "mix_w_out", "ffn2_norm", "ffn2_w_in", "ffn2_w_out", "final_norm")
SMALL_ROWS = 32

FFN1 = ("ffn1_w_in", "ffn1_w_out")
MIX = ("mix_w_in", "mix_w_out")
FFN2 = ("ffn2_w_in", "ffn2_w_out")
STAGE_A = [(n, 0) for n in FFN1]
STAGE_B = [(n, 0) for n in MIX] + [("conv_w", None)]
STAGE_C = [(n, 0) for n in FFN2] + [(n, 1) for n in FFN1 + MIX + FFN2]
STAGE_D = [(n, 1) for n in FFN2]
STAGE_E = [(n, 1) for n in MIX + FFN1] + [(n, 0) for n in FFN2]
STAGE_F = [(n, 0) for n in MIX]
STAGE_G = [(n, 0) for n in FFN1]


def _natural(name, got):
    if name in COL_SHARDED:
        return got.transpose(1, 0, 2).reshape(D_MODEL, -1)
    if name in ROW_SHARDED:
        return got.reshape(-1, D_MODEL)
    return got.transpose(1, 2, 0, 3).reshape(DEPTH, CONV_WIDTH, D_CONV)


def _by_device(name, grad):
    if name in COL_SHARDED:
        return grad.reshape(D_MODEL, N_DEV, -1).transpose(1, 0, 2)
    return grad.reshape(N_DEV, -1, D_MODEL)


def _pack_small(g):
    flat = jnp.concatenate([g[n].reshape(-1) for n in SMALL] + [g["conv_w"].reshape(-1)])
    flat = jnp.concatenate([flat, jnp.zeros((SMALL_ROWS * D_MODEL - flat.shape[0],), F32)])
    return flat.reshape(SMALL_ROWS, D_MODEL)


def _unpack_small(buf, like):
    flat = buf.reshape(-1)
    out, off = {}, 0
    for n in SMALL:
        size = int(np.prod(like[n].shape))
        out[n] = flat[off:off + size].reshape(like[n].shape)
        off += size
    size = DEPTH * CONV_WIDTH * D_CONV
    out["conv_w"] = flat[off:off + size].reshape(DEPTH, CONV_WIDTH, D_CONV)
    return out


def kernel(x, ffn1_norm, ffn1_w_in, ffn1_w_out, mix_norm, mix_w_in, conv_w, conv_b, conv_ln_g, conv_ln_b, ret_norm_g, mix_w_out, ffn2_norm, ffn2_w_in, ffn2_w_out, final_norm, loss_target, m_ffn1_norm, m_ffn1_w_in, m_ffn1_w_out, m_mix_norm, m_mix_w_in, m_conv_w, m_conv_b, m_conv_ln_g, m_conv_ln_b, m_ret_norm_g, m_mix_w_out, m_ffn2_norm, m_ffn2_w_in, m_ffn2_w_out, m_final_norm, v_ffn1_norm, v_ffn1_w_in, v_ffn1_w_out, v_mix_norm, v_mix_w_in, v_conv_w, v_conv_b, v_conv_ln_g, v_conv_ln_b, v_ret_norm_g, v_mix_w_out, v_ffn2_norm, v_ffn2_w_in, v_ffn2_w_out, v_final_norm):
    args = locals()
    w = {n: args[n] for n in WEIGHTS}
    m = {n: args["m_" + n] for n in WEIGHTS}
    v = {n: args["v_" + n] for n in WEIGHTS}
    me = _my_id()
    x = x[0]
    target = loss_target[0]
    S = x.shape[0]
    cos, sin = _rope_tables(S)
    tables = _ret_tables()

    full = {}

    def gather(keys):
        return GATHER, [w["conv_w"] if n == "conv_w" else w[n][l].astype(BF16) for n, l in keys]

    def gathered(keys, got):
        for (n, l), g in zip(keys, got):
            full[(n, l)] = _natural(n, g)

    gathered(STAGE_A, communicate(*gather(STAGE_A)))

    saved = []
    for l in range(DEPTH):
        sv = {"x0": x}
        (x, sv["gate1"], sv["up1"]), got = ffn_fwd(x, _row(w["ffn1_norm"][l]), full[("ffn1_w_in", l)],
                                                   full[("ffn1_w_out", l)], gather(STAGE_B) if l == 0 else None)
        gathered(STAGE_B if l == 0 else [], got)
        sv["x1"] = x
        proj = mix_in_fwd(x, _row(w["mix_norm"][l]), full[("mix_w_in", l)])
        sv["proj"] = proj
        cw = _pad_taps(full[("conv_w", None)][l])
        y_conv, sv["ypre"] = conv_fwd(proj, cw, _row(w["conv_b"][l]), _row(w["conv_ln_g"][l]), _row(w["conv_ln_b"][l]))
        sv["q_sb"] = _heads((proj[:, 512:1024] * 0.125).astype(BF16), N_SB_HEADS)
        sv["k_sb"] = _heads(proj[:, 1024:1536].astype(BF16), N_SB_HEADS)
        sv["v_sb"] = _heads(proj[:, 1536:2048].astype(BF16), N_SB_HEADS)
        (o_sb, sv["tot"]), got = sb_fwd(sv["q_sb"], sv["k_sb"], sv["v_sb"], gather(STAGE_C) if l == 0 else None)
        gathered(STAGE_C if l == 0 else [], got)
        q_rot, k_rot = rope_fwd(proj, cos, sin)
        sv["q_r"] = _heads(q_rot, N_RET_HEADS)
        sv["k_r"] = _heads(k_rot, N_RET_HEADS)
        sv["v_r"] = _heads(proj[:, 2560:2816].astype(BF16), N_RET_HEADS)
        sv["g_r"] = _heads(proj[:, 2816:3072], N_RET_HEADS)
        ng = w["ret_norm_g"][l].reshape(N_RET_HEADS, 1, HEAD_DIM)
        o_r, sv["y_r"], sv["states"] = ret_fwd(sv["q_r"], sv["k_r"], sv["v_r"], sv["g_r"], ng, tables)
        sv["ycat"] = jnp.concatenate([y_conv, _unheads(o_sb).astype(BF16), _unheads(o_r).astype(BF16)], axis=1)
        x = mix_out_fwd(sv["ycat"], full[("mix_w_out", l)], x)
        sv["x2"] = x
        (x, sv["gate2"], sv["up2"]), _ = ffn_fwd(x, _row(w["ffn2_norm"][l]), full[("ffn2_w_in", l)],
                                                 full[("ffn2_w_out", l)])
        saved.append(sv)

    loss_acc, dx, dg_final = loss_head(x, _row(w["final_norm"]), target)
    loss = lax.psum(loss_acc[0, 0], ("x", "y", "c"))

    g = {"final_norm": dg_final.reshape(D_MODEL)}
    received = {}

    def exchange(keys, extra=(), dtype=F32):
        return EXCHANGE, [_by_device(n, g[(n, l)]).astype(dtype) for n, l in keys] + list(extra)

    def exchanged(keys, got):
        for key, p in zip(keys, got):
            received[key] = p

    def ffn_back(dx, x_in, gate, up, norm, names, l, comm=None):
        (dx, h, dyh, dgate, dup, hid, dg), got = ffn_bwd(dx, x_in, _row(norm), gate, up, full[(names[0], l)],
                                                         full[(names[1], l)], comm)
        g[(names[0], l)] = matmul_tn(h, [dgate, dup], D_MODEL, FF_TILE, name="ffn_dw_in")
        g[(names[1], l)] = matmul_tn(hid, [dyh], FF_TILE, D_MODEL, name="ffn_dw_out")
        return dx, dg.reshape(D_MODEL), got

    for l in reversed(range(DEPTH)):
        sv = saved[l]
        dx, g[("ffn2_norm", l)], _ = ffn_back(dx, sv["x2"], sv["gate2"], sv["up2"], w["ffn2_norm"][l], FFN2, l)
        dycat, dxb = mix_out_bwd(dx, full[("mix_w_out", l)])
        g[("mix_w_out", l)] = matmul_tn(sv["ycat"], [dxb], D_MODEL, D_MODEL, name="mix_dw_out")
        cw = _pad_taps(full[("conv_w", None)][l])
        du_conv, dcw, dsm = conv_bwd(dycat, sv["ypre"], sv["proj"], cw, _row(w["conv_ln_g"][l]), _row(w["conv_ln_b"][l]))
        g[("conv_w", l)] = dcw[:CONV_WIDTH]
        g[("conv_b", l)], g[("conv_ln_g", l)], g[("conv_ln_b", l)] = dsm[0], dsm[1], dsm[2]
        do_sb = _heads(dycat[:, 256:768].astype(BF16), N_SB_HEADS)
        stage = STAGE_D if l == DEPTH - 1 else STAGE_E
        (dq_sb, dk_t, dv_t), got = sb_bwd(sv["q_sb"], sv["k_sb"], sv["v_sb"], do_sb, sv["q_sb"].transpose(0, 2, 1),
                                          do_sb.transpose(0, 2, 1), sv["tot"], exchange(stage))
        exchanged(stage, got)
        dk_sb = dk_t.transpose(1, 3, 0, 2).reshape(S, D_SB)
        dv_sb = dv_t.transpose(1, 3, 0, 2).reshape(S, D_SB)
        do_r = _heads(dycat[:, 768:1024], N_RET_HEADS)
        ng = w["ret_norm_g"][l].reshape(N_RET_HEADS, 1, HEAD_DIM)
        dq_r, dk_r, dv_r, dg_r, dng = ret_bwd(do_r, sv["q_r"], sv["k_r"], sv["v_r"], sv["g_r"], ng, sv["y_r"],
                                              sv["states"], tables)
        g[("ret_norm_g", l)] = dng.reshape(D_RET)
        dq_rr, dk_rr = rope_bwd(_unheads(dq_r), _unheads(dk_r), cos, sin)
        dproj = jnp.concatenate([
            du_conv.astype(BF16),
            (_unheads(dq_sb) * 0.125).astype(BF16), dk_sb.astype(BF16), dv_sb.astype(BF16),
            dq_rr.astype(BF16), dk_rr.astype(BF16), _unheads(dv_r).astype(BF16), _unheads(dg_r).astype(BF16)], axis=1)
        dx, h, dg = mix_in_bwd(dproj, full[("mix_w_in", l)], sv["x1"], _row(w["mix_norm"][l]), dx)
        g[("mix_norm", l)] = dg.reshape(D_MODEL)
        g[("mix_w_in", l)] = matmul_tn(h, [dproj], D_MODEL, D_MODEL, name="mix_dw_in")
        dx, g[("ffn1_norm", l)], got = ffn_back(dx, sv["x0"], sv["gate1"], sv["up1"], w["ffn1_norm"][l], FFN1, l,
                                                exchange(STAGE_F) if l == 0 else None)
        exchanged(STAGE_F if l == 0 else [], got)
    grad_x = dx

    small_names = [n for n in SMALL if n != "final_norm"] + ["conv_w"]
    gs = {n: jnp.stack([g[(n, l)] for l in range(DEPTH)], axis=0) for n in small_names}
    gs["final_norm"] = g["final_norm"]
    small = _pack_small(gs)
    got = communicate(*exchange(STAGE_G, [jnp.broadcast_to(small[None], (N_DEV, SMALL_ROWS, D_MODEL))], dtype=BF16))
    exchanged(STAGE_G, got[:-1])

    grad, delta, new_m, new_v = {}, {}, {}, {}
    for n in COL_SHARDED + ROW_SHARDED:
        rows = w[n].shape[1]
        grad[n], delta[n], new_m[n], new_v[n] = adamw([received[(n, l)] for l in range(DEPTH)], w[n], m[n], v[n],
                                                      tr=min(rows // 2, 256))

    def small_pack(d):
        mine = dict(d)
        cwf = jnp.zeros((DEPTH, CONV_WIDTH, D_CONV), F32)
        mine["conv_w"] = lax.dynamic_update_slice(cwf, d["conv_w"], (0, 0, me * (D_CONV // N_DEV)))
        return _pack_small(mine)

    outs = adamw([got[-1]], small_pack(w)[None], small_pack(m)[None], small_pack(v)[None], tr=SMALL_ROWS)
    for dst, o in zip((grad, delta, new_m, new_v), outs):
        un = _unpack_small(o[0], w)
        un["conv_w"] = lax.dynamic_slice(un["conv_w"], (0, 0, me * (D_CONV // N_DEV)),
                                         (DEPTH, CONV_WIDTH, D_CONV // N_DEV))
        dst.update(un)

    return (loss, grad_x[None], *[grad[n] for n in WEIGHTS], *[delta[n] for n in WEIGHTS],
            *[new_m[n] for n in WEIGHTS], *[new_v[n] for n in WEIGHTS])
```

```python
import functools

import numpy as np
import jax
import jax.numpy as jnp
from jax import lax
from jax.experimental import pallas as pl
from jax.experimental.pallas import tpu as pltpu

F32 = jnp.float32
BF16 = jnp.bfloat16

D_MODEL = 1024
DEPTH = 2
D_FF = 2816
D_CONV = 256
CONV_WIDTH = 31
CONV_HALO = 32
D_SB = 512
N_SB_HEADS = 8
D_RET = 256
N_RET_HEADS = 4
HEAD_DIM = 64
D_IN_PROJ = 3072
ROPE_BASE = 10000.0
EPS = 1e-6
N_DEV = 8

ADAM_LR = 0.001
ADAM_B1 = 0.9
ADAM_B2 = 0.999
ADAM_EPS = 1e-08
ADAM_WD = 0.01
ADAM_STEP = 10

VMEM_LIMIT = 56 * 1024 * 1024
ROW_TILE = 512
FF_TILE = 1408
SB_TILE = 256
RET_TILE = 512
CONV_TILE = 256

NT_DIMS = (((1,), (1,)), ((), ()))
TN_DIMS = (((0,), (0,)), ((), ()))


def _params(n_axes, vmem=VMEM_LIMIT):
    return pltpu.CompilerParams(dimension_semantics=("arbitrary",) * n_axes, vmem_limit_bytes=vmem)


def _dot(a, b):
    return jnp.dot(a, b, preferred_element_type=F32)


def _dot_nt(a, b):
    return lax.dot_general(a, b, NT_DIMS, preferred_element_type=F32)


def _dot_tn(a, b):
    return lax.dot_general(a, b, TN_DIMS, preferred_element_type=F32)


def _sigmoid(z):
    return 1.0 / (1.0 + jnp.exp(-z))


def _rms_stats(xv):
    r = lax.rsqrt(jnp.mean(xv * xv, axis=-1, keepdims=True) + EPS)
    return r, xv * r


def _rms_bwd(xv, g, dh):
    r, xhat = _rms_stats(xv)
    dxhat = dh * g
    dx = r * (dxhat - xhat * jnp.mean(dxhat * xhat, axis=-1, keepdims=True))
    dg = jnp.sum(dh * xhat, axis=0, keepdims=True)
    return dx, (xhat * g).astype(BF16), dg


def ffn_fwd(x, g, w_in, w_out, comm=None, tm=ROW_TILE):
    S = x.shape[0]
    nj = D_FF // FF_TILE

    def body(x_ref, g_ref, wg_ref, wu_ref, wo_ref, y_ref, gate_ref, up_ref, h_sc, acc_sc):
        j = pl.program_id(1)

        @pl.when(j == 0)
        def _():
            _, xhat = _rms_stats(x_ref[...])
            h_sc[...] = (xhat * g_ref[...]).astype(BF16)
            acc_sc[...] = jnp.zeros_like(acc_sc)

        h = h_sc[...]
        gt = _dot(h, wg_ref[...])
        up = _dot(h, wu_ref[...])
        gate_ref[...] = gt.astype(BF16)
        up_ref[...] = up.astype(BF16)
        hid = (gt * _sigmoid(gt) * up).astype(BF16)
        acc_sc[...] += _dot(hid, wo_ref[...])

        @pl.when(j == nj - 1)
        def _():
            y_ref[...] = x_ref[...] + 0.5 * acc_sc[...]

    return _call(
        body, (x, g, w_in, w_in, w_out), comm, name="ffn_fwd",
        grid=(S // tm, nj),
        in_specs=[
            pl.BlockSpec((tm, D_MODEL), lambda i, j: (i, 0)),
            pl.BlockSpec((1, D_MODEL), lambda i, j: (0, 0)),
            pl.BlockSpec((D_MODEL, FF_TILE), lambda i, j: (0, j)),
            pl.BlockSpec((D_MODEL, FF_TILE), lambda i, j: (0, j + nj)),
            pl.BlockSpec((FF_TILE, D_MODEL), lambda i, j: (j, 0)),
        ],
        out_specs=[
            pl.BlockSpec((tm, D_MODEL), lambda i, j: (i, 0)),
            pl.BlockSpec((tm, FF_TILE), lambda i, j: (i, j)),
            pl.BlockSpec((tm, FF_TILE), lambda i, j: (i, j)),
        ],
        out_shape=[
            jax.ShapeDtypeStruct((S, D_MODEL), F32),
            jax.ShapeDtypeStruct((S, D_FF), BF16),
            jax.ShapeDtypeStruct((S, D_FF), BF16),
        ],
        scratch_shapes=[pltpu.VMEM((tm, D_MODEL), BF16), pltpu.VMEM((tm, D_MODEL), F32)],
    )


def ffn_bwd(dy, x, g, gate, up, w_in, w_out, comm=None, tm=ROW_TILE // 2):
    S = x.shape[0]
    nj = D_FF // FF_TILE

    def body(dy_ref, x_ref, g_ref, gate_ref, up_ref, w_ref, wo_ref,
             dx_ref, h_ref, dyh_ref, dgate_ref, dup_ref, hid_ref, dg_ref):
        i = pl.program_id(0)
        d2 = (0.5 * dy_ref[...]).astype(BF16)
        dyh_ref[...] = d2
        dh = None
        for j in range(nj):
            cols = pl.ds(j * FF_TILE, FF_TILE)
            dhid = _dot_nt(d2, wo_ref[cols, :])
            gt = gate_ref[:, cols].astype(F32)
            u = up_ref[:, cols].astype(F32)
            sig = _sigmoid(gt)
            sl = gt * sig
            dgate = (dhid * u * (sig * (1.0 + gt * (1.0 - sig)))).astype(BF16)
            dup = (dhid * sl).astype(BF16)
            dgate_ref[:, cols] = dgate
            dup_ref[:, cols] = dup
            hid_ref[:, cols] = (sl * u).astype(BF16)
            part = _dot_nt(dgate, w_ref[:, cols]) + _dot_nt(dup, w_ref[:, pl.ds(D_FF + j * FF_TILE, FF_TILE)])
            dh = part if dh is None else dh + part
        dx, h, dg = _rms_bwd(x_ref[...], g_ref[...], dh)
        dx_ref[...] = dy_ref[...] + dx
        h_ref[...] = h

        @pl.when(i == 0)
        def _():
            dg_ref[...] = dg

        @pl.when(i > 0)
        def _():
            dg_ref[...] += dg

    row = lambda i: (i, 0)
    one = lambda i: (0, 0)
    resident = pl.Buffered(1)
    return _call(
        body, (dy, x, g, gate, up, w_in, w_out), comm, name="ffn_bwd",
        grid=(S // tm,),
        in_specs=[
            pl.BlockSpec((tm, D_MODEL), row),
            pl.BlockSpec((tm, D_MODEL), row),
            pl.BlockSpec((1, D_MODEL), one),
            pl.BlockSpec((tm, D_FF), row),
            pl.BlockSpec((tm, D_FF), row),
            pl.BlockSpec((D_MODEL, 2 * D_FF), one, pipeline_mode=resident),
            pl.BlockSpec((D_FF, D_MODEL), one, pipeline_mode=resident),
        ],
        out_specs=[
            pl.BlockSpec((tm, D_MODEL), row),
            pl.BlockSpec((tm, D_MODEL), row),
            pl.BlockSpec((tm, D_MODEL), row),
            pl.BlockSpec((tm, D_FF), row),
            pl.BlockSpec((tm, D_FF), row),
            pl.BlockSpec((tm, D_FF), row),
            pl.BlockSpec((1, D_MODEL), one),
        ],
        out_shape=[
            jax.ShapeDtypeStruct((S, D_MODEL), F32),
            jax.ShapeDtypeStruct((S, D_MODEL), BF16),
            jax.ShapeDtypeStruct((S, D_MODEL), BF16),
            jax.ShapeDtypeStruct((S, D_FF), BF16),
            jax.ShapeDtypeStruct((S, D_FF), BF16),
            jax.ShapeDtypeStruct((S, D_FF), BF16),
            jax.ShapeDtypeStruct((1, D_MODEL), F32),
        ],
        scratch_shapes=[],
    )


def matmul_tn(a, bs, ta, tn, tk=ROW_TILE, name="matmul_tn", comm=None):
    S, ka = a.shape
    nb = bs[0].shape[1]
    per = nb // tn

    def body(*refs):
        a_ref, b_refs, o_ref = refs[0], refs[1:-1], refs[-1]
        j = pl.program_id(1)
        k = pl.program_id(2)

        @pl.when(k == 0)
        def _():
            o_ref[...] = jnp.zeros_like(o_ref)

        for t, b_ref in enumerate(b_refs):
            @pl.when(lax.div(j, per) == t)
            def _(b_ref=b_ref):
                o_ref[...] += _dot_tn(a_ref[...], b_ref[...])

    def b_spec(t):
        def index(i, j, k):
            mine = lax.div(j, per) == t
            return jnp.where(mine, k, 0), jnp.where(mine, j - t * per, 0)
        return pl.BlockSpec((tk, tn), index)

    (out,), got = _call(
        body, (a, *bs), comm, name=name,
        grid=(ka // ta, per * len(bs), S // tk),
        in_specs=[pl.BlockSpec((tk, ta), lambda i, j, k: (k, i))] + [b_spec(t) for t in range(len(bs))],
        out_specs=[pl.BlockSpec((ta, tn), lambda i, j, k: (i, j))],
        out_shape=[jax.ShapeDtypeStruct((ka, nb * len(bs)), F32)],
        scratch_shapes=[],
    )
    return (out, got) if comm is not None else out


def mix_in_fwd(x, g, w, tm=ROW_TILE):
    S = x.shape[0]

    def body(x_ref, g_ref, w_ref, o_ref):
        _, xhat = _rms_stats(x_ref[...])
        o_ref[...] = _dot((xhat * g_ref[...]).astype(BF16), w_ref[...])

    return pl.pallas_call(
        body, name="mix_in_fwd",
        grid=(S // tm,),
        in_specs=[
            pl.BlockSpec((tm, D_MODEL), lambda i: (i, 0)),
            pl.BlockSpec((1, D_MODEL), lambda i: (0, 0)),
            pl.BlockSpec((D_MODEL, D_IN_PROJ), lambda i: (0, 0)),
        ],
        out_specs=pl.BlockSpec((tm, D_IN_PROJ), lambda i: (i, 0)),
        out_shape=jax.ShapeDtypeStruct((S, D_IN_PROJ), F32),
        compiler_params=_params(1),
    )(x, g, w)


def mix_in_bwd(dproj, w, x, g, dy, tm=ROW_TILE):
    S = x.shape[0]

    def body(dp_ref, w_ref, x_ref, g_ref, dy_ref, dx_ref, h_ref, dg_ref):
        i = pl.program_id(0)
        dh = _dot_nt(dp_ref[...], w_ref[...])
        dx, h, dg = _rms_bwd(x_ref[...], g_ref[...], dh)
        dx_ref[...] = dy_ref[...] + dx
        h_ref[...] = h

        @pl.when(i == 0)
        def _():
            dg_ref[...] = dg

        @pl.when(i > 0)
        def _():
            dg_ref[...] += dg

    row = lambda i: (i, 0)
    return pl.pallas_call(
        body, name="mix_in_bwd",
        grid=(S // tm,),
        in_specs=[
            pl.BlockSpec((tm, D_IN_PROJ), row),
            pl.BlockSpec((D_MODEL, D_IN_PROJ), lambda i: (0, 0)),
            pl.BlockSpec((tm, D_MODEL), row),
            pl.BlockSpec((1, D_MODEL), lambda i: (0, 0)),
            pl.BlockSpec((tm, D_MODEL), row),
        ],
        out_specs=[
            pl.BlockSpec((tm, D_MODEL), row),
            pl.BlockSpec((tm, D_MODEL), row),
            pl.BlockSpec((1, D_MODEL), lambda i: (0, 0)),
        ],
        out_shape=[
            jax.ShapeDtypeStruct((S, D_MODEL), F32),
            jax.ShapeDtypeStruct((S, D_MODEL), BF16),
            jax.ShapeDtypeStruct((1, D_MODEL), F32),
        ],
        compiler_params=_params(1),
    )(dproj, w, x, g, dy)


def mix_out_fwd(ycat, w, x, tm=ROW_TILE):
    S = x.shape[0]

    def body(y_ref, w_ref, x_ref, o_ref):
        o_ref[...] = x_ref[...] + _dot(y_ref[...], w_ref[...])

    row = lambda i: (i, 0)
    return pl.pallas_call(
        body, name="mix_out_fwd",
        grid=(S // tm,),
        in_specs=[
            pl.BlockSpec((tm, D_MODEL), row),
            pl.BlockSpec((D_MODEL, D_MODEL), lambda i: (0, 0)),
            pl.BlockSpec((tm, D_MODEL), row),
        ],
        out_specs=pl.BlockSpec((tm, D_MODEL), row),
        out_shape=jax.ShapeDtypeStruct((S, D_MODEL), F32),
        compiler_params=_params(1),
    )(ycat, w, x)


def mix_out_bwd(dy, w, tm=ROW_TILE):
    S = dy.shape[0]

    def body(dy_ref, w_ref, o_ref, dyb_ref):
        d = dy_ref[...].astype(BF16)
        dyb_ref[...] = d
        o_ref[...] = _dot_nt(d, w_ref[...])

    row = lambda i: (i, 0)
    return pl.pallas_call(
        body, name="mix_out_bwd",
        grid=(S // tm,),
        in_specs=[
            pl.BlockSpec((tm, D_MODEL), row),
            pl.BlockSpec((D_MODEL, D_MODEL), lambda i: (0, 0)),
        ],
        out_specs=[pl.BlockSpec((tm, D_MODEL), row), pl.BlockSpec((tm, D_MODEL), row)],
        out_shape=[jax.ShapeDtypeStruct((S, D_MODEL), F32), jax.ShapeDtypeStruct((S, D_MODEL), BF16)],
        compiler_params=_params(1),
    )(dy, w)


def _conv_ln(ypre, ln_g, ln_b):
    mu = jnp.mean(ypre, axis=-1, keepdims=True)
    yc = ypre - mu
    rstd = lax.rsqrt(jnp.mean(yc * yc, axis=-1, keepdims=True) + EPS)
    yn = yc * rstd
    return yn, rstd, yn * ln_g + ln_b


def conv_fwd(proj, cw, cb, ln_g, ln_b, tm=CONV_TILE):
    S = proj.shape[0]
    hb = tm // CONV_HALO

    def body(a_ref, b_ref, ap_ref, bp_ref, cw_ref, cb_ref, g_ref, bb_ref, y_ref, ypre_ref, v_sc):
        i = pl.program_id(0)
        prev = ap_ref[...] * _sigmoid(bp_ref[...])
        v_sc[pl.ds(0, CONV_HALO), :] = jnp.where(i > 0, prev, 0.0)
        v_sc[pl.ds(CONV_HALO, tm), :] = a_ref[...] * _sigmoid(b_ref[...])
        acc = jnp.zeros((tm, D_CONV), F32)
        for j in range(CONV_WIDTH):
            acc = acc + cw_ref[pl.ds(j, 1), :] * v_sc[pl.ds(CONV_HALO - (CONV_WIDTH - 1) + j, tm), :]
        ypre = acc + cb_ref[...]
        ypre_ref[...] = ypre
        _, _, z = _conv_ln(ypre, g_ref[...], bb_ref[...])
        y_ref[...] = (z * _sigmoid(z)).astype(BF16)

    one = lambda i: (0, 0)
    return pl.pallas_call(
        body, name="conv_fwd",
        grid=(S // tm,),
        in_specs=[
            pl.BlockSpec((tm, D_CONV), lambda i: (i, 0)),
            pl.BlockSpec((tm, D_CONV), lambda i: (i, 1)),
            pl.BlockSpec((CONV_HALO, D_CONV), lambda i: (jnp.maximum(i * hb - 1, 0), 0)),
            pl.BlockSpec((CONV_HALO, D_CONV), lambda i: (jnp.maximum(i * hb - 1, 0), 1)),
            pl.BlockSpec((CONV_HALO, D_CONV), one),
            pl.BlockSpec((1, D_CONV), one),
            pl.BlockSpec((1, D_CONV), one),
            pl.BlockSpec((1, D_CONV), one),
        ],
        out_specs=[pl.BlockSpec((tm, D_CONV), lambda i: (i, 0)), pl.BlockSpec((tm, D_CONV), lambda i: (i, 0))],
        out_shape=[jax.ShapeDtypeStruct((S, D_CONV), BF16), jax.ShapeDtypeStruct((S, D_CONV), F32)],
        scratch_shapes=[pltpu.VMEM((tm + CONV_HALO, D_CONV), F32)],
        compiler_params=_params(1),
    )(proj, proj, proj, proj, cw, cb, ln_g, ln_b)


def conv_bwd(dyc, ypre, proj, cw, ln_g, ln_b, tm=CONV_TILE):
    S = ypre.shape[0]
    hb = tm // CONV_HALO
    nblk = S // tm
    last_halo = S // CONV_HALO - 1

    def dpre(dy, yp, g, bb):
        yn, rstd, z = _conv_ln(yp, g, bb)
        sg = _sigmoid(z)
        dz = dy * (sg * (1.0 + z * (1.0 - sg)))
        dyn = dz * g
        d = rstd * (dyn - jnp.mean(dyn, axis=-1, keepdims=True) - yn * jnp.mean(dyn * yn, axis=-1, keepdims=True))
        return d, dz * yn, dz

    def body(dy_ref, yp_ref, dyn_ref, ypn_ref, a_ref, b_ref, ap_ref, bp_ref, cw_ref, g_ref, bb_ref,
             du_ref, dcw_ref, dsm_ref, d_sc, v_sc):
        i = pl.program_id(0)
        g = g_ref[...]
        bb = bb_ref[...]
        d_main, dgn, dz = dpre(dy_ref[...], yp_ref[...], g, bb)
        d_next, _, _ = dpre(dyn_ref[...], ypn_ref[...], g, bb)
        d_sc[pl.ds(0, tm), :] = d_main
        d_sc[pl.ds(tm, CONV_HALO), :] = jnp.where(i < nblk - 1, d_next, 0.0)
        a = a_ref[...]
        sb = _sigmoid(b_ref[...])
        prev = ap_ref[...] * _sigmoid(bp_ref[...])
        v_sc[pl.ds(0, CONV_HALO), :] = jnp.where(i > 0, prev, 0.0)
        v_sc[pl.ds(CONV_HALO, tm), :] = a * sb

        @pl.when(i == 0)
        def _():
            dcw_ref[...] = jnp.zeros_like(dcw_ref)
            dsm_ref[...] = jnp.zeros_like(dsm_ref)

        dv = jnp.zeros((tm, D_CONV), F32)
        for j in range(CONV_WIDTH):
            dv = dv + cw_ref[pl.ds(j, 1), :] * d_sc[pl.ds(CONV_WIDTH - 1 - j, tm), :]
            shifted = v_sc[pl.ds(CONV_HALO - (CONV_WIDTH - 1) + j, tm), :]
            dcw_ref[pl.ds(j, 1), :] += jnp.sum(d_main * shifted, axis=0, keepdims=True)
        du_ref[:, pl.ds(0, D_CONV)] = dv * sb
        du_ref[:, pl.ds(D_CONV, D_CONV)] = dv * a * sb * (1.0 - sb)
        dsm_ref[pl.ds(0, 1), :] += jnp.sum(d_main, axis=0, keepdims=True)
        dsm_ref[pl.ds(1, 1), :] += jnp.sum(dgn, axis=0, keepdims=True)
        dsm_ref[pl.ds(2, 1), :] += jnp.sum(dz, axis=0, keepdims=True)

    one = lambda i: (0, 0)
    prev_map = lambda c: (lambda i: (jnp.maximum(i * hb - 1, 0), c))
    next_map = lambda i: (jnp.minimum((i + 1) * hb, last_halo), 0)
    return pl.pallas_call(
        body, name="conv_bwd",
        grid=(nblk,),
        in_specs=[
            pl.BlockSpec((tm, D_CONV), lambda i: (i, 0)),
            pl.BlockSpec((tm, D_CONV), lambda i: (i, 0)),
            pl.BlockSpec((CONV_HALO, D_CONV), next_map),
            pl.BlockSpec((CONV_HALO, D_CONV), next_map),
            pl.BlockSpec((tm, D_CONV), lambda i: (i, 0)),
            pl.BlockSpec((tm, D_CONV), lambda i: (i, 1)),
            pl.BlockSpec((CONV_HALO, D_CONV), prev_map(0)),
            pl.BlockSpec((CONV_HALO, D_CONV), prev_map(1)),
            pl.BlockSpec((CONV_HALO, D_CONV), one),
            pl.BlockSpec((1, D_CONV), one),
            pl.BlockSpec((1, D_CONV), one),
        ],
        out_specs=[
            pl.BlockSpec((tm, 2 * D_CONV), lambda i: (i, 0)),
            pl.BlockSpec((CONV_HALO, D_CONV), one),
            pl.BlockSpec((8, D_CONV), one),
        ],
        out_shape=[
            jax.ShapeDtypeStruct((S, 2 * D_CONV), F32),
            jax.ShapeDtypeStruct((CONV_HALO, D_CONV), F32),
            jax.ShapeDtypeStruct((8, D_CONV), F32),
        ],
        scratch_shapes=[pltpu.VMEM((tm + CONV_HALO, D_CONV), F32), pltpu.VMEM((tm + CONV_HALO, D_CONV), F32)],
        compiler_params=_params(1),
    )(dyc, ypre, dyc, ypre, proj, proj, proj, proj, cw, ln_g, ln_b)


SB_GROUP = 8


def _softplus(z):
    neg_abs = lax.bitcast_convert_type(lax.bitcast_convert_type(z, jnp.uint32) | jnp.uint32(0x80000000), F32)
    return jnp.maximum(z, 0.0) + jnp.log(1.0 + jnp.exp(neg_abs))


def _full_groups(n, body):
    def step(t, c):
        body(t * SB_GROUP)
        return c

    lax.fori_loop(0, lax.div(n, SB_GROUP), step, 0)


def _last_group(n, body):
    r = lax.rem(n, SB_GROUP)
    for k in range(SB_GROUP):
        @pl.when(r == k)
        def _(k=k):
            body(k)


def _rows(xs):
    return xs[0] if len(xs) == 1 else jnp.concatenate(xs, axis=0)


def sb_fwd(q, k, v, comm=None, T=SB_TILE):
    H, S, dh = q.shape

    def body(q_ref, k_ref, v_ref, o_ref, tot_ref, acc_sc, car_sc):
        qb = pl.program_id(1)
        qv = q_ref[...]
        row = lax.broadcasted_iota(jnp.int32, (T, T), 0)
        col = lax.broadcasted_iota(jnp.int32, (T, T), 1)
        tri = jnp.where(row >= col, 1.0, 0.0).astype(BF16)
        causal = col < row
        acc_sc[...] = jnp.zeros_like(acc_sc)
        car_sc[...] = jnp.zeros_like(car_sc)

        def logits(kb, masked):
            ks = k_ref[pl.ds(pl.multiple_of(kb * T, T), T), :]
            z = _dot_nt(qv, ks)
            nb = _softplus(z)
            if masked:
                nb = jnp.where(causal, nb, 0.0)
            return z, nb.astype(BF16)

        def group(kbs, diag):
            parts = [logits(kb, d) for kb, d in zip(kbs, diag)]
            pall = _dot(_rows([nb for _, nb in parts]), tri)
            carry = car_sc[...]
            out = None
            for j, kb in enumerate(kbs):
                p = pall[j * T:(j + 1) * T]
                vs = v_ref[pl.ds(pl.multiple_of(kb * T, T), T), :]
                w = jnp.exp((parts[j][0] - carry) - p)
                if diag[j]:
                    w = jnp.where(causal, w, 0.0)
                o = _dot(w.astype(BF16), vs)
                out = o if out is None else out + o
                carry = carry + p[:, 0:1]
            acc_sc[...] += out
            car_sc[...] = carry

        _last_group(qb, lambda r: group([qb] + [qb - 1 - o for o in range(r)], [True] + [False] * r))
        rest = qb - lax.rem(qb, SB_GROUP)
        _full_groups(rest, lambda o: group([rest - 1 - o - j for j in range(SB_GROUP)], [False] * SB_GROUP))
        o_ref[...] = acc_sc[...]
        tot_ref[...] = car_sc[...]

    return _call(
        body, (q, k, v), comm, name="sb_fwd",
        grid=(H, S // T),
        in_specs=[
            pl.BlockSpec((None, T, dh), lambda h, i: (h, i, 0)),
            pl.BlockSpec((None, S, dh), lambda h, i: (h, 0, 0)),
            pl.BlockSpec((None, S, dh), lambda h, i: (h, 0, 0)),
        ],
        out_specs=[
            pl.BlockSpec((None, T, dh), lambda h, i: (h, i, 0)),
            pl.BlockSpec((None, T, 1), lambda h, i: (h, i, 0)),
        ],
        out_shape=[jax.ShapeDtypeStruct((H, S, dh), F32), jax.ShapeDtypeStruct((H, S, 1), F32)],
        scratch_shapes=[pltpu.VMEM((T, dh), F32), pltpu.VMEM((T, 1), F32)],
    )


def sb_bwd(q, k, v, do, qt, dot, tot, comm=None, T=SB_TILE):
    H, S, dh = q.shape
    nt = S // T

    def body(q_ref, k_ref, v_ref, do_ref, qt_ref, dot_ref, tot_ref, dq_ref, dk_ref, dv_ref, acc_sc, rc_sc, gc_sc):
        qb = pl.program_id(1)
        qv = q_ref[...]
        dov = do_ref[...]
        qtv = qt_ref[...]
        dotv = dot_ref[...]
        row = lax.broadcasted_iota(jnp.int32, (T, T), 0)
        col = lax.broadcasted_iota(jnp.int32, (T, T), 1)
        before = jnp.where(row < col, 1.0, 0.0).astype(BF16)
        causal = col < row
        acc_sc[...] = jnp.zeros_like(acc_sc)
        rc_sc[...] = tot_ref[...]
        gc_sc[...] = jnp.zeros_like(gc_sc)

        @pl.when(qb == 0)
        def _():
            dk_ref[...] = jnp.zeros_like(dk_ref)
            dv_ref[...] = jnp.zeros_like(dv_ref)

        def first(kb, masked):
            start = pl.multiple_of(kb * T, T)
            z = _dot_nt(qv, k_ref[pl.ds(start, T), :])
            nb = _softplus(z)
            sig = jnp.exp(z - nb)
            if masked:
                nb = jnp.where(causal, nb, 0.0)
            dw = _dot_nt(dov, v_ref[pl.ds(start, T), :])
            return z, sig, nb.astype(BF16), dw

        def group(kbs, diag):
            parts = [first(kb, d) for kb, d in zip(kbs, diag)]
            pall = _dot(_rows([p[2] for p in parts]), before)
            rc = rc_sc[...]
            ws, gs, ghs = [], [], []
            for j in range(len(kbs)):
                z, _, nbh, dw = parts[j]
                p = pall[j * T:(j + 1) * T]
                w = jnp.exp((z - rc) + p)
                rc = rc - (p[:, T - 1:T] + nbh[:, T - 1:T].astype(F32))
                if diag[j]:
                    w = jnp.where(causal, w, 0.0)
                g = dw * w
                ws.append(w.astype(BF16))
                gs.append(g)
                ghs.append(g.astype(BF16))
            glall = _dot(_rows(ghs), before)
            gc = gc_sc[...]
            dq = None
            for j, kb in enumerate(kbs):
                ks = k_ref[pl.ds(pl.multiple_of(kb * T, T), T), :]
                gl = glall[j * T:(j + 1) * T]
                dz = gs[j] - parts[j][1] * (gs[j] + (gl + gc))
                gc = gc + gl[:, T - 1:T] + ghs[j][:, T - 1:T].astype(F32)
                if diag[j]:
                    dz = jnp.where(causal, dz, 0.0)
                dzb = dz.astype(BF16)
                d = _dot(dzb, ks)
                dq = d if dq is None else dq + d
                dk_ref[kb] += _dot(qtv, dzb)
                dv_ref[kb] += _dot(dotv, ws[j])
            acc_sc[...] += dq
            rc_sc[...] = rc
            gc_sc[...] = gc

        _full_groups(qb, lambda o: group([o + j for j in range(SB_GROUP)], [False] * SB_GROUP))
        rest = qb - lax.rem(qb, SB_GROUP)
        _last_group(qb, lambda r: group([rest + j for j in range(r)] + [qb], [False] * r + [True]))
        dq_ref[...] = acc_sc[...]

    return _call(
        body, (q, k, v, do, qt, dot, tot), comm, name="sb_bwd",
        grid=(H, nt),
        in_specs=[
            pl.BlockSpec((None, T, dh), lambda h, i: (h, i, 0)),
            pl.BlockSpec((None, S, dh), lambda h, i: (h, 0, 0)),
            pl.BlockSpec((None, S, dh), lambda h, i: (h, 0, 0)),
            pl.BlockSpec((None, T, dh), lambda h, i: (h, i, 0)),
            pl.BlockSpec((None, dh, T), lambda h, i: (h, 0, i)),
            pl.BlockSpec((None, dh, T), lambda h, i: (h, 0, i)),
            pl.BlockSpec((None, T, 1), lambda h, i: (h, i, 0)),
        ],
        out_specs=[
            pl.BlockSpec((None, T, dh), lambda h, i: (h, i, 0)),
            pl.BlockSpec((None, nt, dh, T), lambda h, i: (h, 0, 0, 0)),
            pl.BlockSpec((None, nt, dh, T), lambda h, i: (h, 0, 0, 0)),
        ],
        out_shape=[jax.ShapeDtypeStruct((H, S, dh), F32), jax.ShapeDtypeStruct((H, nt, dh, T), F32),
                   jax.ShapeDtypeStruct((H, nt, dh, T), F32)],
        scratch_shapes=[pltpu.VMEM((T, dh), F32), pltpu.VMEM((T, 1), F32), pltpu.VMEM((T, 1), F32)],
    )


def _ret_tables(T=RET_TILE):
    hh = jnp.arange(N_RET_HEADS, dtype=F32)
    log_gamma = jnp.log1p(-jnp.exp2(-5.0 - hh))
    idx = jnp.arange(T, dtype=F32)
    diff = idx[:, None] - idx[None, :]
    ci = (jnp.arange(T) // 64)
    same = ci[:, None] == ci[None, :]
    earlier = ci[None, :] < ci[:, None]
    dist = jnp.where(same, jnp.abs(diff), diff)
    dmat = jnp.where(same | earlier, jnp.exp(log_gamma[:, None, None] * dist[None]), 0.0)
    ones = jnp.ones((1, 1, HEAD_DIM), F32)
    qdec = jnp.exp(log_gamma[:, None] * (idx + 1.0)[None, :])[:, :, None] * ones
    kdec = jnp.exp(log_gamma[:, None] * (T - 1.0 - idx)[None, :])[:, :, None] * ones
    bdec = jnp.exp(log_gamma * T)[:, None, None] * jnp.ones((1, HEAD_DIM, HEAD_DIM), F32)
    return dmat, qdec, kdec, bdec


def _rope_tables(S):
    half = HEAD_DIM // 2
    inv = 1.0 / (ROPE_BASE ** (jnp.arange(half, dtype=F32) / half))
    ang = jnp.arange(S).astype(F32)[:, None] * inv[None, :]
    c = jnp.cos(ang)
    s = jnp.sin(ang)
    cos = jnp.tile(jnp.concatenate([c, c], axis=1), (1, N_RET_HEADS))
    sin = jnp.tile(jnp.concatenate([-s, s], axis=1), (1, N_RET_HEADS))
    return cos, sin


def _swap_halves(x):
    n = x.shape[1]
    lane = lax.broadcasted_iota(jnp.int32, x.shape, 1)
    first = (lane % HEAD_DIM) < (HEAD_DIM // 2)
    return jnp.where(first, pltpu.roll(x, n - HEAD_DIM // 2, 1), pltpu.roll(x, HEAD_DIM // 2, 1))


def rope_fwd(proj, cos, sin, tm=ROW_TILE):
    S = proj.shape[0]

    def body(q_ref, k_ref, c_ref, s_ref, qo_ref, ko_ref):
        c = c_ref[...]
        s = s_ref[...]
        qv = q_ref[...]
        kv = k_ref[...]
        qo_ref[...] = ((qv * c + _swap_halves(qv) * s) * 0.125).astype(BF16)
        ko_ref[...] = (kv * c + _swap_halves(kv) * s).astype(BF16)

    row = lambda i: (i, 0)
    return pl.pallas_call(
        body, name="rope_fwd",
        grid=(S // tm,),
        in_specs=[
            pl.BlockSpec((tm, D_RET), lambda i: (i, 8)),
            pl.BlockSpec((tm, D_RET), lambda i: (i, 9)),
            pl.BlockSpec((tm, D_RET), row),
            pl.BlockSpec((tm, D_RET), row),
        ],
        out_specs=[pl.BlockSpec((tm, D_RET), row), pl.BlockSpec((tm, D_RET), row)],
        out_shape=[jax.ShapeDtypeStruct((S, D_RET), BF16)] * 2,
        compiler_params=_params(1),
    )(proj, proj, cos, sin)


def rope_bwd(dq, dk, cos, sin, tm=ROW_TILE):
    S = dq.shape[0]

    def body(dq_ref, dk_ref, c_ref, s_ref, qo_ref, ko_ref):
        c = c_ref[...]
        s = s_ref[...]
        dqv = dq_ref[...] * 0.125
        dkv = dk_ref[...]
        qo_ref[...] = dqv * c - _swap_halves(dqv) * s
        ko_ref[...] = dkv * c - _swap_halves(dkv) * s

    row = lambda i: (i, 0)
    return pl.pallas_call(
        body, name="rope_bwd",
        grid=(S // tm,),
        in_specs=[pl.BlockSpec((tm, D_RET), row)] * 4,
        out_specs=[pl.BlockSpec((tm, D_RET), row)] * 2,
        out_shape=[jax.ShapeDtypeStruct((S, D_RET), F32)] * 2,
        compiler_params=_params(1),
    )(dq, dk, cos, sin)


def ret_fwd(q, k, v, gate, ng, tables, T=RET_TILE):
    H, S, dh = q.shape
    dmat, qdec, kdec, bdec = tables

    def body(q_ref, k_ref, v_ref, gt_ref, ng_ref, dm_ref, qd_ref, kd_ref, bd_ref, o_ref, y_ref, st_ref, s_sc):
        n = pl.program_id(1)

        @pl.when(n == 0)
        def _():
            s_sc[...] = jnp.zeros_like(s_sc)

        qv = q_ref[...]
        kv = k_ref[...]
        vv = v_ref[...]
        state = s_sc[...]
        st_ref[...] = state
        sc = (_dot_nt(qv, kv) * dm_ref[...]).astype(BF16)
        qd = (qv.astype(F32) * qd_ref[...]).astype(BF16)
        y = _dot(sc, vv) + _dot(qd, state.astype(BF16))
        y_ref[...] = y
        kd = (kv.astype(F32) * kd_ref[...]).astype(BF16)
        s_sc[...] = bd_ref[...] * state + _dot_tn(kd, vv)
        mu = jnp.mean(y, axis=-1, keepdims=True)
        yc = y - mu
        yn = yc * lax.rsqrt(jnp.mean(yc * yc, axis=-1, keepdims=True) + EPS)
        gt = gt_ref[...]
        o_ref[...] = gt * _sigmoid(gt) * (yn * ng_ref[...])

    blk = lambda h, n: (h, n, 0)
    head = lambda h, n: (h, 0, 0)
    return pl.pallas_call(
        body, name="ret_fwd",
        grid=(H, S // T),
        in_specs=[
            pl.BlockSpec((None, T, dh), blk),
            pl.BlockSpec((None, T, dh), blk),
            pl.BlockSpec((None, T, dh), blk),
            pl.BlockSpec((None, T, dh), blk),
            pl.BlockSpec((None, 1, dh), head),
            pl.BlockSpec((None, T, T), head),
            pl.BlockSpec((None, T, dh), head),
            pl.BlockSpec((None, T, dh), head),
            pl.BlockSpec((None, dh, dh), head),
        ],
        out_specs=[
            pl.BlockSpec((None, T, dh), blk),
            pl.BlockSpec((None, T, dh), blk),
            pl.BlockSpec((None, None, dh, dh), lambda h, n: (h, n, 0, 0)),
        ],
        out_shape=[
            jax.ShapeDtypeStruct((H, S, dh), F32),
            jax.ShapeDtypeStruct((H, S, dh), F32),
            jax.ShapeDtypeStruct((H, S // T, dh, dh), F32),
        ],
        scratch_shapes=[pltpu.VMEM((dh, dh), F32)],
        compiler_params=_params(2),
    )(q, k, v, gate, ng, dmat, qdec, kdec, bdec)


def ret_bwd(do, q, k, v, gate, ng, y, states, tables, T=RET_TILE):
    H, S, dh = q.shape
    nb = S // T
    dmat, qdec, kdec, bdec = tables

    def body(do_ref, q_ref, k_ref, v_ref, gt_ref, ng_ref, y_ref, st_ref, dm_ref, qd_ref, kd_ref, bd_ref,
             dq_ref, dk_ref, dv_ref, dgt_ref, dng_ref, u_sc):
        n = pl.program_id(1)

        @pl.when(n == 0)
        def _():
            u_sc[...] = jnp.zeros_like(u_sc)
            dng_ref[...] = jnp.zeros_like(dng_ref)

        yv = y_ref[...]
        mu = jnp.mean(yv, axis=-1, keepdims=True)
        yc = yv - mu
        rstd = lax.rsqrt(jnp.mean(yc * yc, axis=-1, keepdims=True) + EPS)
        yn = yc * rstd
        gt = gt_ref[...]
        sg = _sigmoid(gt)
        ngv = ng_ref[...]
        dout = do_ref[...]
        dgt_ref[...] = dout * (yn * ngv) * (sg * (1.0 + gt * (1.0 - sg)))
        dn = dout * (gt * sg)
        dng_ref[...] += jnp.sum(dn * yn, axis=0, keepdims=True)
        dyn = dn * ngv
        dy = rstd * (dyn - jnp.mean(dyn, axis=-1, keepdims=True) - yn * jnp.mean(dyn * yn, axis=-1, keepdims=True))
        dyb = dy.astype(BF16)

        qv = q_ref[...]
        kv = k_ref[...]
        vv = v_ref[...]
        dm = dm_ref[...]
        qdt = qd_ref[...]
        kdt = kd_ref[...]
        sb = st_ref[...].astype(BF16)
        u = u_sc[...]
        ub = u.astype(BF16)
        dqk = (_dot_nt(dyb, vv) * dm).astype(BF16)
        sc = (_dot_nt(qv, kv) * dm).astype(BF16)
        qd = (qv.astype(F32) * qdt).astype(BF16)
        kd = (kv.astype(F32) * kdt).astype(BF16)
        dq_ref[...] = _dot(dqk, kv) + qdt * _dot_nt(dyb, sb)
        dk_ref[...] = _dot_tn(dqk, qv) + kdt * _dot_nt(vv, ub)
        dv_ref[...] = _dot_tn(sc, dyb) + _dot(kd, ub)
        u_sc[...] = bd_ref[...] * u + _dot_tn(qd, dyb)

    blk = lambda h, n: (h, nb - 1 - n, 0)
    head = lambda h, n: (h, 0, 0)
    return pl.pallas_call(
        body, name="ret_bwd",
        grid=(H, nb),
        in_specs=[
            pl.BlockSpec((None, T, dh), blk),
            pl.BlockSpec((None, T, dh), blk),
            pl.BlockSpec((None, T, dh), blk),
            pl.BlockSpec((None, T, dh), blk),
            pl.BlockSpec((None, T, dh), blk),
            pl.BlockSpec((None, 1, dh), head),
            pl.BlockSpec((None, T, dh), blk),
            pl.BlockSpec((None, None, dh, dh), lambda h, n: (h, nb - 1 - n, 0, 0)),
            pl.BlockSpec((None, T, T), head),
            pl.BlockSpec((None, T, dh), head),
            pl.BlockSpec((None, T, dh), head),
            pl.BlockSpec((None, dh, dh), head),
        ],
        out_specs=[
            pl.BlockSpec((None, T, dh), blk),
            pl.BlockSpec((None, T, dh), blk),
            pl.BlockSpec((None, T, dh), blk),
            pl.BlockSpec((None, T, dh), blk),
            pl.BlockSpec((None, 1, dh), head),
        ],
        out_shape=[jax.ShapeDtypeStruct((H, S, dh), F32)] * 4 + [jax.ShapeDtypeStruct((H, 1, dh), F32)],
        scratch_shapes=[pltpu.VMEM((dh, dh), F32)],
        compiler_params=_params(2),
    )(do, q, k, v, gate, ng, y, states, dmat, qdec, kdec, bdec)


def loss_head(x, g, target, tm=ROW_TILE):
    S = x.shape[0]

    def body(x_ref, g_ref, t_ref, loss_ref, dx_ref, dg_ref):
        i = pl.program_id(0)
        xv = x_ref[...]
        gv = g_ref[...]
        _, xhat = _rms_stats(xv)
        err = xhat * gv - t_ref[...]
        part = 0.5 * jnp.sum(jnp.mean(err * err, axis=-1, keepdims=True), axis=0, keepdims=True)
        dx, _, dg = _rms_bwd(xv, gv, err * (1.0 / D_MODEL))
        dx_ref[...] = dx
        part = jnp.broadcast_to(part, (1, 128))

        @pl.when(i == 0)
        def _():
            loss_ref[...] = part
            dg_ref[...] = dg

        @pl.when(i > 0)
        def _():
            loss_ref[...] += part
            dg_ref[...] += dg

    row = lambda i: (i, 0)
    one = lambda i: (0, 0)
    return pl.pallas_call(
        body, name="loss_head",
        grid=(S // tm,),
        in_specs=[pl.BlockSpec((tm, D_MODEL), row), pl.BlockSpec((1, D_MODEL), one), pl.BlockSpec((tm, D_MODEL), row)],
        out_specs=[pl.BlockSpec((1, 128), one), pl.BlockSpec((tm, D_MODEL), row), pl.BlockSpec((1, D_MODEL), one)],
        out_shape=[
            jax.ShapeDtypeStruct((1, 128), F32),
            jax.ShapeDtypeStruct((S, D_MODEL), F32),
            jax.ShapeDtypeStruct((1, D_MODEL), F32),
        ],
        compiler_params=_params(1),
    )(x, g, target)


def adamw(parts, w, m, v, tr):
    L, R, C = w.shape
    nr = R // tr
    c1 = 1.0 / (1.0 - ADAM_B1 ** ADAM_STEP)
    c2 = 1.0 / (1.0 - ADAM_B2 ** ADAM_STEP)

    def body(*refs):
        p_refs = refs[:L]
        w_ref, m_ref, v_ref, g_ref, d_ref, mo_ref, vo_ref = refs[L:]
        l = pl.program_id(0)
        g = None
        for d in range(N_DEV):
            pd = p_refs[0][d].astype(F32)
            for ll in range(1, L):
                pd = jnp.where(l == ll, p_refs[ll][d].astype(F32), pd)
            g = pd if g is None else g + pd
        mn = ADAM_B1 * m_ref[...] + (1.0 - ADAM_B1) * g
        vn = ADAM_B2 * v_ref[...] + (1.0 - ADAM_B2) * (g * g)
        g_ref[...] = g
        mo_ref[...] = mn
        vo_ref[...] = vn
        d_ref[...] = -ADAM_LR * ((mn * c1) / (jnp.sqrt(vn * c2) + ADAM_EPS) + ADAM_WD * w_ref[...])

    def part_spec(ll):
        return pl.BlockSpec((N_DEV, tr, C), lambda l, i: (0, jnp.where(l == ll, i, jnp.where(l < ll, 0, nr - 1)), 0))

    blk = pl.BlockSpec((None, tr, C), lambda l, i: (l, i, 0))
    return pl.pallas_call(
        body, name="adamw",
        grid=(L, nr),
        in_specs=[part_spec(ll) for ll in range(L)] + [blk] * 3,
        out_specs=[blk] * 4,
        out_shape=[jax.ShapeDtypeStruct((L, R, C), F32)] * 4,
        compiler_params=_params(2),
    )(*parts, w, m, v)


def _my_id():
    return lax.axis_index("x") * 4 + lax.axis_index("y") * 2 + lax.axis_index("c")


def _peer(k):
    x, y, c = lax.axis_index("x"), lax.axis_index("y"), lax.axis_index("c")
    px = 1 - x if k & 4 else x
    py = 1 - y if k & 2 else y
    pc = 1 - c if k & 1 else c
    return (px, py, pc), px * 4 + py * 2 + pc


GATHER = "gather"
EXCHANGE = "exchange"


def _copies(kind, ins, outs, send_sems, recv_sems, local_sems, receive_side):
    me = _my_id()
    local, sends, recvs = [], [], []
    for t in range(len(ins)):
        src = ins[t] if kind == GATHER else ins[t].at[me]
        local.append(pltpu.make_async_copy(src, outs[t].at[me], local_sems.at[t]))
    for k in range(1, N_DEV):
        dev, pid = _peer(k)
        for t in range(len(ins)):
            sems = dict(send_sem=send_sems.at[t, k - 1], recv_sem=recv_sems.at[t, k - 1],
                        device_id=dev, device_id_type=pl.DeviceIdType.MESH)
            src = ins[t] if kind == GATHER else ins[t].at[pid]
            sends.append(pltpu.make_async_remote_copy(src_ref=src, dst_ref=outs[t].at[me], **sems))
            if receive_side:
                recvs.append(pltpu.make_async_remote_copy(src_ref=src, dst_ref=outs[t].at[pid], **sems))
    return local, sends, recvs


def _comm_start(kind, ins, outs, sems):
    local, sends, _ = _copies(kind, ins, outs, *sems, receive_side=False)
    for cp in local + sends:
        cp.start()


def _comm_wait(kind, ins, outs, sems):
    local, sends, recvs = _copies(kind, ins, outs, *sems, receive_side=True)
    for cp in recvs:
        cp.wait_recv()
    for cp in sends:
        cp.wait_send()
    for cp in local:
        cp.wait()


def _comm_shapes(kind, arrays):
    n = len(arrays)
    out_shape = [jax.ShapeDtypeStruct(((N_DEV,) + a.shape) if kind == GATHER else a.shape, a.dtype) for a in arrays]
    sems = [pltpu.SemaphoreType.DMA((n, N_DEV - 1)), pltpu.SemaphoreType.DMA((n, N_DEV - 1)),
            pltpu.SemaphoreType.DMA((n,))]
    return out_shape, sems


def communicate(kind, arrays):
    n = len(arrays)

    def body(*refs):
        ins, outs, sems = refs[:n], refs[n:2 * n], refs[2 * n:]
        _comm_start(kind, ins, outs, sems)
        _comm_wait(kind, ins, outs, sems)

    out_shape, sems = _comm_shapes(kind, arrays)
    any_spec = pl.BlockSpec(memory_space=pl.ANY)
    return pl.pallas_call(
        body, name=kind, in_specs=[any_spec] * n, out_specs=[any_spec] * n, out_shape=out_shape, scratch_shapes=sems,
    )(*arrays)


def _call(body, operands, comm, *, name, grid, in_specs, out_specs, out_shape, scratch_shapes):
    if comm is None:
        outs = pl.pallas_call(body, name=name, grid=grid, in_specs=in_specs, out_specs=out_specs, out_shape=out_shape,
                              scratch_shapes=scratch_shapes, compiler_params=_params(len(grid)))(*operands)
        return outs, []
    kind, arrays = comm
    n, n_in, n_out, n_sc = len(arrays), len(in_specs), len(out_specs), len(scratch_shapes)

    def carrier(*refs):
        ins, cins = refs[:n_in], refs[n_in:n_in + n]
        refs = refs[n_in + n:]
        outs, couts = refs[:n_out], refs[n_out:n_out + n]
        scratch, sems = refs[n_out + n:n_out + n + n_sc], refs[n_out + n + n_sc:]
        steps = [pl.program_id(a) for a in range(len(grid))]
        first = functools.reduce(jnp.logical_and, [s == 0 for s in steps])
        last = functools.reduce(jnp.logical_and, [s == g - 1 for s, g in zip(steps, grid)])

        @pl.when(first)
        def _():
            _comm_start(kind, cins, couts, sems)

        body(*ins, *outs, *scratch)

        @pl.when(last)
        def _():
            _comm_wait(kind, cins, couts, sems)

    comm_shape, sems = _comm_shapes(kind, arrays)
    any_spec = pl.BlockSpec(memory_space=pl.ANY)
    outs = pl.pallas_call(
        carrier, name=f"{name}_{kind}", grid=grid,
        in_specs=list(in_specs) + [any_spec] * n,
        out_specs=list(out_specs) + [any_spec] * n,
        out_shape=list(out_shape) + comm_shape,
        scratch_shapes=list(scratch_shapes) + sems,
        compiler_params=_params(len(grid)),
    )(*operands, *arrays)
    return outs[:n_out], outs[n_out:]


def _heads(t, n_heads):
    S = t.shape[0]
    return t.reshape(S, n_heads, HEAD_DIM).transpose(1, 0, 2)


def _unheads(t):
    H, S, _ = t.shape
    return t.transpose(1, 0, 2).reshape(S, H * HEAD_DIM)


def _row(v):
    return v.reshape(1, -1)


def _pad_taps(cw):
    return jnp.concatenate([cw, jnp.zeros((CONV_HALO - CONV_WIDTH, D_CONV), F32)], axis=0)


COL_SHARDED = ("ffn1_w_in", "mix_w_in", "ffn2_w_in")
ROW_SHARDED = ("ffn1_w_out", "mix_w_out", "ffn2_w_out")
SMALL = ("ffn1_norm", "mix_norm", "conv_b", "conv_ln_g", "conv_ln_b", "ret_norm_g", "ffn2_norm", "final_norm")
WEIGHTS = ("ffn1_norm", "ffn1_w_in", "ffn1_w_out", "mix_norm", "mix_w_in", "conv_w", "conv_b", "conv_ln_g",
           "conv_ln_b", "ret_norm_g", "mix_w_out", "ffn2_norm", "ffn2_w_in", "ffn2_w_out", "final_norm")
SMALL_ROWS = 32

FFN1 = ("ffn1_w_in", "ffn1_w_out")
MIX = ("mix_w_in", "mix_w_out")
FFN2 = ("ffn2_w_in", "ffn2_w_out")
STAGE_A = [(n, 0) for n in FFN1]
STAGE_B = [(n, 0) for n in MIX] + [("conv_w", None)]
STAGE_C = [(n, 0) for n in FFN2] + [(n, 1) for n in FFN1 + MIX + FFN2]
STAGE_D = [(n, 1) for n in FFN2]
STAGE_E = [(n, 1) for n in MIX + FFN1] + [(n, 0) for n in FFN2]
STAGE_F = [(n, 0) for n in MIX]
STAGE_G = [("ffn1_w_in", 0)]
STAGE_H = [("ffn1_w_out", 0)]


def _natural(name, got):
    if name in COL_SHARDED:
        return got.transpose(1, 0, 2).reshape(D_MODEL, -1)
    if name in ROW_SHARDED:
        return got.reshape(-1, D_MODEL)
    return got.transpose(1, 2, 0, 3).reshape(DEPTH, CONV_WIDTH, D_CONV)


def _by_device(name, grad):
    if name in COL_SHARDED:
        return grad.reshape(D_MODEL, N_DEV, -1).transpose(1, 0, 2)
    return grad.reshape(N_DEV, -1, D_MODEL)


def _pack_small(g):
    flat = jnp.concatenate([g[n].reshape(-1) for n in SMALL] + [g["conv_w"].reshape(-1)])
    flat = jnp.concatenate([flat, jnp.zeros((SMALL_ROWS * D_MODEL - flat.shape[0],), F32)])
    return flat.reshape(SMALL_ROWS, D_MODEL)


def _unpack_small(buf, like):
    flat = buf.reshape(-1)
    out, off = {}, 0
    for n in SMALL:
        size = int(np.prod(like[n].shape))
        out[n] = flat[off:off + size].reshape(like[n].shape)
        off += size
    size = DEPTH * CONV_WIDTH * D_CONV
    out["conv_w"] = flat[off:off + size].reshape(DEPTH, CONV_WIDTH, D_CONV)
    return out


def kernel(x, ffn1_norm, ffn1_w_in, ffn1_w_out, mix_norm, mix_w_in, conv_w, conv_b, conv_ln_g, conv_ln_b, ret_norm_g, mix_w_out, ffn2_norm, ffn2_w_in, ffn2_w_out, final_norm, loss_target, m_ffn1_norm, m_ffn1_w_in, m_ffn1_w_out, m_mix_norm, m_mix_w_in, m_conv_w, m_conv_b, m_conv_ln_g, m_conv_ln_b, m_ret_norm_g, m_mix_w_out, m_ffn2_norm, m_ffn2_w_in, m_ffn2_w_out, m_final_norm, v_ffn1_norm, v_ffn1_w_in, v_ffn1_w_out, v_mix_norm, v_mix_w_in, v_conv_w, v_conv_b, v_conv_ln_g, v_conv_ln_b, v_ret_norm_g, v_mix_w_out, v_ffn2_norm, v_ffn2_w_in, v_ffn2_w_out, v_final_norm):
    args = locals()
    w = {n: args[n] for n in WEIGHTS}
    m = {n: args["m_" + n] for n in WEIGHTS}
    v = {n: args["v_" + n] for n in WEIGHTS}
    me = _my_id()
    x = x[0]
    target = loss_target[0]
    S = x.shape[0]
    cos, sin = _rope_tables(S)
    tables = _ret_tables()

    full = {}

    def gather(keys):
        return GATHER, [w["conv_w"] if n == "conv_w" else w[n][l].astype(BF16) for n, l in keys]

    def gathered(keys, got):
        for (n, l), g in zip(keys, got):
            full[(n, l)] = _natural(n, g)

    gathered(STAGE_A, communicate(*gather(STAGE_A)))

    saved = []
    for l in range(DEPTH):
        sv = {"x0": x}
        (x, sv["gate1"], sv["up1"]), got = ffn_fwd(x, _row(w["ffn1_norm"][l]), full[("ffn1_w_in", l)],
                                                   full[("ffn1_w_out", l)], gather(STAGE_B) if l == 0 else None)
        gathered(STAGE_B if l == 0 else [], got)
        sv["x1"] = x
        proj = mix_in_fwd(x, _row(w["mix_norm"][l]), full[("mix_w_in", l)])
        sv["proj"] = proj
        cw = _pad_taps(full[("conv_w", None)][l])
        y_conv, sv["ypre"] = conv_fwd(proj, cw, _row(w["conv_b"][l]), _row(w["conv_ln_g"][l]), _row(w["conv_ln_b"][l]))
        sv["q_sb"] = _heads((proj[:, 512:1024] * 0.125).astype(BF16), N_SB_HEADS)
        sv["k_sb"] = _heads(proj[:, 1024:1536].astype(BF16), N_SB_HEADS)
        sv["v_sb"] = _heads(proj[:, 1536:2048].astype(BF16), N_SB_HEADS)
        (o_sb, sv["tot"]), got = sb_fwd(sv["q_sb"], sv["k_sb"], sv["v_sb"], gather(STAGE_C) if l == 0 else None)
        gathered(STAGE_C if l == 0 else [], got)
        q_rot, k_rot = rope_fwd(proj, cos, sin)
        sv["q_r"] = _heads(q_rot, N_RET_HEADS)
        sv["k_r"] = _heads(k_rot, N_RET_HEADS)
        sv["v_r"] = _heads(proj[:, 2560:2816].astype(BF16), N_RET_HEADS)
        sv["g_r"] = _heads(proj[:, 2816:3072], N_RET_HEADS)
        ng = w["ret_norm_g"][l].reshape(N_RET_HEADS, 1, HEAD_DIM)
        o_r, sv["y_r"], sv["states"] = ret_fwd(sv["q_r"], sv["k_r"], sv["v_r"], sv["g_r"], ng, tables)
        sv["ycat"] = jnp.concatenate([y_conv, _unheads(o_sb).astype(BF16), _unheads(o_r).astype(BF16)], axis=1)
        x = mix_out_fwd(sv["ycat"], full[("mix_w_out", l)], x)
        sv["x2"] = x
        (x, sv["gate2"], sv["up2"]), _ = ffn_fwd(x, _row(w["ffn2_norm"][l]), full[("ffn2_w_in", l)],
                                                 full[("ffn2_w_out", l)])
        saved.append(sv)

    loss_acc, dx, dg_final = loss_head(x, _row(w["final_norm"]), target)
    loss = lax.psum(loss_acc[0, 0], ("x", "y", "c"))

    g = {"final_norm": dg_final.reshape(D_MODEL)}
    received = {}

    def exchange(keys, extra=(), dtype=F32):
        return EXCHANGE, [_by_device(n, g[(n, l)]).astype(dtype) for n, l in keys] + list(extra)

    def exchanged(keys, got):
        for key, p in zip(keys, got):
            received[key] = p

    def ffn_back(dx, x_in, gate, up, norm, names, l, comm=None):
        (dx, h, dyh, dgate, dup, hid, dg), got = ffn_bwd(dx, x_in, _row(norm), gate, up, full[(names[0], l)],
                                                         full[(names[1], l)], comm)
        g[(names[0], l)] = matmul_tn(h, [dgate, dup], D_MODEL, FF_TILE, name="ffn_dw_in")
        if [(names[0], l)] == STAGE_G:
            g[(names[1], l)], got_g = matmul_tn(hid, [dyh], FF_TILE, D_MODEL, name="ffn_dw_out",
                                                comm=exchange(STAGE_G, dtype=BF16))
            exchanged(STAGE_G, got_g)
        else:
            g[(names[1], l)] = matmul_tn(hid, [dyh], FF_TILE, D_MODEL, name="ffn_dw_out")
        return dx, dg.reshape(D_MODEL), got

    for l in reversed(range(DEPTH)):
        sv = saved[l]
        dx, g[("ffn2_norm", l)], _ = ffn_back(dx, sv["x2"], sv["gate2"], sv["up2"], w["ffn2_norm"][l], FFN2, l)
        dycat, dxb = mix_out_bwd(dx, full[("mix_w_out", l)])
        g[("mix_w_out", l)] = matmul_tn(sv["ycat"], [dxb], D_MODEL, D_MODEL, name="mix_dw_out")
        cw = _pad_taps(full[("conv_w", None)][l])
        du_conv, dcw, dsm = conv_bwd(dycat, sv["ypre"], sv["proj"], cw, _row(w["conv_ln_g"][l]), _row(w["conv_ln_b"][l]))
        g[("conv_w", l)] = dcw[:CONV_WIDTH]
        g[("conv_b", l)], g[("conv_ln_g", l)], g[("conv_ln_b", l)] = dsm[0], dsm[1], dsm[2]
        do_sb = _heads(dycat[:, 256:768].astype(BF16), N_SB_HEADS)
        stage = STAGE_D if l == DEPTH - 1 else STAGE_E
        (dq_sb, dk_t, dv_t), got = sb_bwd(sv["q_sb"], sv["k_sb"], sv["v_sb"], do_sb, sv["q_sb"].transpose(0, 2, 1),
                                          do_sb.transpose(0, 2, 1), sv["tot"], exchange(stage))
        exchanged(stage, got)
        dk_sb = dk_t.transpose(1, 3, 0, 2).reshape(S, D_SB)
        dv_sb = dv_t.transpose(1, 3, 0, 2).reshape(S, D_SB)
        do_r = _heads(dycat[:, 768:1024], N_RET_HEADS)
        ng = w["ret_norm_g"][l].reshape(N_RET_HEADS, 1, HEAD_DIM)
        dq_r, dk_r, dv_r, dg_r, dng = ret_bwd(do_r, sv["q_r"], sv["k_r"], sv["v_r"], sv["g_r"], ng, sv["y_r"],
                                              sv["states"], tables)
        g[("ret_norm_g", l)] = dng.reshape(D_RET)
        dq_rr, dk_rr = rope_bwd(_unheads(dq_r), _unheads(dk_r), cos, sin)
        dproj = jnp.concatenate([
            du_conv.astype(BF16),
            (_unheads(dq_sb) * 0.125).astype(BF16), dk_sb.astype(BF16), dv_sb.astype(BF16),
            dq_rr.astype(BF16), dk_rr.astype(BF16), _unheads(dv_r).astype(BF16), _unheads(dg_r).astype(BF16)], axis=1)
        dx, h, dg = mix_in_bwd(dproj, full[("mix_w_in", l)], sv["x1"], _row(w["mix_norm"][l]), dx)
        g[("mix_norm", l)] = dg.reshape(D_MODEL)
        g[("mix_w_in", l)] = matmul_tn(h, [dproj], D_MODEL, D_MODEL, name="mix_dw_in")
        dx, g[("ffn1_norm", l)], got = ffn_back(dx, sv["x0"], sv["gate1"], sv["up1"], w["ffn1_norm"][l], FFN1, l,
                                                exchange(STAGE_F) if l == 0 else None)
        exchanged(STAGE_F if l == 0 else [], got)
    grad_x = dx

    small_names = [n for n in SMALL if n != "final_norm"] + ["conv_w"]
    gs = {n: jnp.stack([g[(n, l)] for l in range(DEPTH)], axis=0) for n in small_names}
    gs["final_norm"] = g["final_norm"]
    small = _pack_small(gs)
    got = communicate(*exchange(STAGE_H, [jnp.broadcast_to(small[None], (N_DEV, SMALL_ROWS, D_MODEL))], dtype=BF16))
    exchanged(STAGE_H, got[:-1])

    grad, delta, new_m, new_v = {}, {}, {}, {}
    for n in COL_SHARDED + ROW_SHARDED:
        rows = w[n].shape[1]
        grad[n], delta[n], new_m[n], new_v[n] = adamw([received[(n, l)] for l in range(DEPTH)], w[n], m[n], v[n],
                                                      tr=min(rows // 2, 256))

    def small_pack(d):
        mine = dict(d)
        cwf = jnp.zeros((DEPTH, CONV_WIDTH, D_CONV), F32)
        mine["conv_w"] = lax.dynamic_update_slice(cwf, d["conv_w"], (0, 0, me * (D_CONV // N_DEV)))
        return _pack_small(mine)

    outs = adamw([got[-1]], small_pack(w)[None], small_pack(m)[None], small_pack(v)[None], tr=SMALL_ROWS)
    for dst, o in zip((grad, delta, new_m, new_v), outs):
        un = _unpack_small(o[0], w)
        un["conv_w"] = lax.dynamic_slice(un["conv_w"], (0, 0, me * (D_CONV // N_DEV)),
                                         (DEPTH, CONV_WIDTH, D_CONV // N_DEV))
        dst.update(un)

    return (loss, grad_x[None], *[grad[n] for n in WEIGHTS], *[delta[n] for n in WEIGHTS],
            *[new_m[n] for n in WEIGHTS], *[new_v[n] for n in WEIGHTS])
```

```python
import functools

import numpy as np
import jax
import jax.numpy as jnp
from jax import lax
from jax.experimental import pallas as pl
from jax.experimental.pallas import tpu as pltpu

F32 = jnp.float32
BF16 = jnp.bfloat16

D_MODEL = 1024
DEPTH = 2
D_FF = 2816
D_CONV = 256
CONV_WIDTH = 31
CONV_HALO = 32
D_SB = 512
N_SB_HEADS = 8
D_RET = 256
N_RET_HEADS = 4
HEAD_DIM = 64
D_IN_PROJ = 3072
ROPE_BASE = 10000.0
EPS = 1e-6
N_DEV = 8

ADAM_LR = 0.001
ADAM_B1 = 0.9
ADAM_B2 = 0.999
ADAM_EPS = 1e-08
ADAM_WD = 0.01
ADAM_STEP = 10

VMEM_LIMIT = 56 * 1024 * 1024
ROW_TILE = 512
FF_TILE = 1408
SB_TILE = 256
RET_TILE = 512
CONV_TILE = 256

NT_DIMS = (((1,), (1,)), ((), ()))
TN_DIMS = (((0,), (0,)), ((), ()))


def _params(n_axes, vmem=VMEM_LIMIT):
    return pltpu.CompilerParams(dimension_semantics=("arbitrary",) * n_axes, vmem_limit_bytes=vmem)


def _dot(a, b):
    return jnp.dot(a, b, preferred_element_type=F32)


def _dot_nt(a, b):
    return lax.dot_general(a, b, NT_DIMS, preferred_element_type=F32)


def _dot_tn(a, b):
    return lax.dot_general(a, b, TN_DIMS, preferred_element_type=F32)


def _sigmoid(z):
    return 1.0 / (1.0 + jnp.exp(-z))


def _rms_stats(xv):
    r = lax.rsqrt(jnp.mean(xv * xv, axis=-1, keepdims=True) + EPS)
    return r, xv * r


def _rms_bwd(xv, g, dh):
    r, xhat = _rms_stats(xv)
    dxhat = dh * g
    dx = r * (dxhat - xhat * jnp.mean(dxhat * xhat, axis=-1, keepdims=True))
    dg = jnp.sum(dh * xhat, axis=0, keepdims=True)
    return dx, (xhat * g).astype(BF16), dg


def ffn_fwd(x, g, w_in, w_out, comm=None, tm=ROW_TILE):
    S = x.shape[0]
    nj = D_FF // FF_TILE

    def body(x_ref, g_ref, wg_ref, wu_ref, wo_ref, y_ref, gate_ref, up_ref, h_sc, acc_sc):
        j = pl.program_id(1)

        @pl.when(j == 0)
        def _():
            _, xhat = _rms_stats(x_ref[...])
            h_sc[...] = (xhat * g_ref[...]).astype(BF16)
            acc_sc[...] = jnp.zeros_like(acc_sc)

        h = h_sc[...]
        gt = _dot(h, wg_ref[...])
        up = _dot(h, wu_ref[...])
        gate_ref[...] = gt.astype(BF16)
        up_ref[...] = up.astype(BF16)
        hid = (gt * _sigmoid(gt) * up).astype(BF16)
        acc_sc[...] += _dot(hid, wo_ref[...])

        @pl.when(j == nj - 1)
        def _():
            y_ref[...] = x_ref[...] + 0.5 * acc_sc[...]

    return _call(
        body, (x, g, w_in, w_in, w_out), comm, name="ffn_fwd",
        grid=(S // tm, nj),
        in_specs=[
            pl.BlockSpec((tm, D_MODEL), lambda i, j: (i, 0)),
            pl.BlockSpec((1, D_MODEL), lambda i, j: (0, 0)),
            pl.BlockSpec((D_MODEL, FF_TILE), lambda i, j: (0, j)),
            pl.BlockSpec((D_MODEL, FF_TILE), lambda i, j: (0, j + nj)),
            pl.BlockSpec((FF_TILE, D_MODEL), lambda i, j: (j, 0)),
        ],
        out_specs=[
            pl.BlockSpec((tm, D_MODEL), lambda i, j: (i, 0)),
            pl.BlockSpec((tm, FF_TILE), lambda i, j: (i, j)),
            pl.BlockSpec((tm, FF_TILE), lambda i, j: (i, j)),
        ],
        out_shape=[
            jax.ShapeDtypeStruct((S, D_MODEL), F32),
            jax.ShapeDtypeStruct((S, D_FF), BF16),
            jax.ShapeDtypeStruct((S, D_FF), BF16),
        ],
        scratch_shapes=[pltpu.VMEM((tm, D_MODEL), BF16), pltpu.VMEM((tm, D_MODEL), F32)],
    )


def ffn_bwd(dy, x, g, gate, up, w_in, w_out, comm=None, tm=ROW_TILE // 2):
    S = x.shape[0]
    nj = D_FF // FF_TILE

    def body(dy_ref, x_ref, g_ref, gate_ref, up_ref, w_ref, wo_ref,
             dx_ref, h_ref, dyh_ref, dgate_ref, dup_ref, hid_ref, dg_ref):
        i = pl.program_id(0)
        d2 = (0.5 * dy_ref[...]).astype(BF16)
        dyh_ref[...] = d2
        dh = None
        for j in range(nj):
            cols = pl.ds(j * FF_TILE, FF_TILE)
            dhid = _dot_nt(d2, wo_ref[cols, :])
            gt = gate_ref[:, cols].astype(F32)
            u = up_ref[:, cols].astype(F32)
            sig = _sigmoid(gt)
            sl = gt * sig
            dgate = (dhid * u * (sig * (1.0 + gt * (1.0 - sig)))).astype(BF16)
            dup = (dhid * sl).astype(BF16)
            dgate_ref[:, cols] = dgate
            dup_ref[:, cols] = dup
            hid_ref[:, cols] = (sl * u).astype(BF16)
            part = _dot_nt(dgate, w_ref[:, cols]) + _dot_nt(dup, w_ref[:, pl.ds(D_FF + j * FF_TILE, FF_TILE)])
            dh = part if dh is None else dh + part
        dx, h, dg = _rms_bwd(x_ref[...], g_ref[...], dh)
        dx_ref[...] = dy_ref[...] + dx
        h_ref[...] = h

        @pl.when(i == 0)
        def _():
            dg_ref[...] = dg

        @pl.when(i > 0)
        def _():
            dg_ref[...] += dg

    row = lambda i: (i, 0)
    one = lambda i: (0, 0)
    resident = pl.Buffered(1)
    return _call(
        body, (dy, x, g, gate, up, w_in, w_out), comm, name="ffn_bwd",
        grid=(S // tm,),
        in_specs=[
            pl.BlockSpec((tm, D_MODEL), row),
            pl.BlockSpec((tm, D_MODEL), row),
            pl.BlockSpec((1, D_MODEL), one),
            pl.BlockSpec((tm, D_FF), row),
            pl.BlockSpec((tm, D_FF), row),
            pl.BlockSpec((D_MODEL, 2 * D_FF), one, pipeline_mode=resident),
            pl.BlockSpec((D_FF, D_MODEL), one, pipeline_mode=resident),
        ],
        out_specs=[
            pl.BlockSpec((tm, D_MODEL), row),
            pl.BlockSpec((tm, D_MODEL), row),
            pl.BlockSpec((tm, D_MODEL), row),
            pl.BlockSpec((tm, D_FF), row),
            pl.BlockSpec((tm, D_FF), row),
            pl.BlockSpec((tm, D_FF), row),
            pl.BlockSpec((1, D_MODEL), one),
        ],
        out_shape=[
            jax.ShapeDtypeStruct((S, D_MODEL), F32),
            jax.ShapeDtypeStruct((S, D_MODEL), BF16),
            jax.ShapeDtypeStruct((S, D_MODEL), BF16),
            jax.ShapeDtypeStruct((S, D_FF), BF16),
            jax.ShapeDtypeStruct((S, D_FF), BF16),
            jax.ShapeDtypeStruct((S, D_FF), BF16),
            jax.ShapeDtypeStruct((1, D_MODEL), F32),
        ],
        scratch_shapes=[],
    )


def matmul_tn(a, bs, ta, tn, tk=ROW_TILE, name="matmul_tn", comm=None):
    S, ka = a.shape
    nb = bs[0].shape[1]
    per = nb // tn

    def body(*refs):
        a_ref, b_refs, o_ref = refs[0], refs[1:-1], refs[-1]
        j = pl.program_id(1)
        k = pl.program_id(2)

        @pl.when(k == 0)
        def _():
            o_ref[...] = jnp.zeros_like(o_ref)

        for t, b_ref in enumerate(b_refs):
            @pl.when(lax.div(j, per) == t)
            def _(b_ref=b_ref):
                o_ref[...] += _dot_tn(a_ref[...], b_ref[...])

    def b_spec(t):
        def index(i, j, k):
            mine = lax.div(j, per) == t
            return jnp.where(mine, k, 0), jnp.where(mine, j - t * per, 0)
        return pl.BlockSpec((tk, tn), index)

    (out,), got = _call(
        body, (a, *bs), comm, name=name,
        grid=(ka // ta, per * len(bs), S // tk),
        in_specs=[pl.BlockSpec((tk, ta), lambda i, j, k: (k, i))] + [b_spec(t) for t in range(len(bs))],
        out_specs=[pl.BlockSpec((ta, tn), lambda i, j, k: (i, j))],
        out_shape=[jax.ShapeDtypeStruct((ka, nb * len(bs)), F32)],
        scratch_shapes=[],
    )
    return (out, got) if comm is not None else out


SB_COLS = (2 * D_CONV, 2 * D_CONV + D_SB, 2 * D_CONV + 2 * D_SB)
RET_COLS = tuple(2 * D_CONV + 3 * D_SB + j * D_RET for j in range(4))


def _swap_halves(x):
    n = x.shape[1]
    lane = lax.broadcasted_iota(jnp.int32, x.shape, 1)
    first = (lane % HEAD_DIM) < (HEAD_DIM // 2)
    return jnp.where(first, pltpu.roll(x, n - HEAD_DIM // 2, 1), pltpu.roll(x, HEAD_DIM // 2, 1))


def _head(x, h):
    return x[:, h * HEAD_DIM:(h + 1) * HEAD_DIM]


def _heads_spec(n_heads, tm):
    return pl.BlockSpec((n_heads, tm, HEAD_DIM), lambda i: (0, i, 0))


def mix_in_fwd(x, g, w, cos, sin, tm=ROW_TILE):
    S = x.shape[0]

    def body(x_ref, g_ref, w_ref, c_ref, s_ref, u_ref, q_ref, k_ref, v_ref, qt_ref, qr_ref, kr_ref, vr_ref, gr_ref):
        _, xhat = _rms_stats(x_ref[...])
        proj = _dot((xhat * g_ref[...]).astype(BF16), w_ref[...])
        u_ref[...] = proj[:, :2 * D_CONV]
        for h in range(N_SB_HEADS):
            q = (_head(proj[:, SB_COLS[0]:SB_COLS[1]], h) * 0.125).astype(BF16)
            q_ref[h] = q
            qt_ref[h] = q.T
            k_ref[h] = _head(proj[:, SB_COLS[1]:SB_COLS[2]], h).astype(BF16)
            v_ref[h] = _head(proj[:, SB_COLS[2]:RET_COLS[0]], h).astype(BF16)
        c = c_ref[...]
        s = s_ref[...]
        qv = proj[:, RET_COLS[0]:RET_COLS[1]]
        kv = proj[:, RET_COLS[1]:RET_COLS[2]]
        q_rot = ((qv * c + _swap_halves(qv) * s) * 0.125).astype(BF16)
        k_rot = (kv * c + _swap_halves(kv) * s).astype(BF16)
        for h in range(N_RET_HEADS):
            qr_ref[h] = _head(q_rot, h)
            kr_ref[h] = _head(k_rot, h)
            vr_ref[h] = _head(proj[:, RET_COLS[2]:RET_COLS[3]], h).astype(BF16)
            gr_ref[h] = _head(proj[:, RET_COLS[3]:], h)

    row = lambda i: (i, 0)
    one = lambda i: (0, 0)
    sb = jax.ShapeDtypeStruct((N_SB_HEADS, S, HEAD_DIM), BF16)
    ret = jax.ShapeDtypeStruct((N_RET_HEADS, S, HEAD_DIM), BF16)
    return pl.pallas_call(
        body, name="mix_in_fwd",
        grid=(S // tm,),
        in_specs=[
            pl.BlockSpec((tm, D_MODEL), row),
            pl.BlockSpec((1, D_MODEL), one),
            pl.BlockSpec((D_MODEL, D_IN_PROJ), one, pipeline_mode=pl.Buffered(1)),
            pl.BlockSpec((tm, D_RET), row),
            pl.BlockSpec((tm, D_RET), row),
        ],
        out_specs=[
            pl.BlockSpec((tm, 2 * D_CONV), row),
            _heads_spec(N_SB_HEADS, tm), _heads_spec(N_SB_HEADS, tm), _heads_spec(N_SB_HEADS, tm),
            pl.BlockSpec((N_SB_HEADS, HEAD_DIM, tm), lambda i: (0, 0, i)),
            _heads_spec(N_RET_HEADS, tm), _heads_spec(N_RET_HEADS, tm), _heads_spec(N_RET_HEADS, tm),
            _heads_spec(N_RET_HEADS, tm),
        ],
        out_shape=[
            jax.ShapeDtypeStruct((S, 2 * D_CONV), F32), sb, sb, sb,
            jax.ShapeDtypeStruct((N_SB_HEADS, HEAD_DIM, S), BF16),
            ret, ret, ret, jax.ShapeDtypeStruct((N_RET_HEADS, S, HEAD_DIM), F32),
        ],
        compiler_params=_params(1),
    )(x, g, w, cos, sin)


def mix_in_bwd(du, dq, dkt, dvt, dqr, dkr, dvr, dgr, cos, sin, w, x, g, dy, tm=SB_TILE):
    S = x.shape[0]
    assert dkt.shape[-1] == tm

    def body(du_ref, dq_ref, dkt_ref, dvt_ref, dqr_ref, dkr_ref, dvr_ref, dgr_ref, c_ref, s_ref, w_ref, x_ref, g_ref,
             dy_ref, dx_ref, h_ref, dp_ref, dg_ref):
        i = pl.program_id(0)
        sb_heads = range(N_SB_HEADS)
        ret_heads = range(N_RET_HEADS)
        c = c_ref[...]
        s = s_ref[...]
        dq_rot = jnp.concatenate([dqr_ref[h] for h in ret_heads], axis=1) * 0.125
        dk_rot = jnp.concatenate([dkr_ref[h] for h in ret_heads], axis=1)
        dproj = jnp.concatenate([
            du_ref[...].astype(BF16),
            jnp.concatenate([dq_ref[h] * 0.125 for h in sb_heads], axis=1).astype(BF16),
            jnp.concatenate([dkt_ref[h, 0].T for h in sb_heads], axis=1).astype(BF16),
            jnp.concatenate([dvt_ref[h, 0].T for h in sb_heads], axis=1).astype(BF16),
            (dq_rot * c - _swap_halves(dq_rot) * s).astype(BF16),
            (dk_rot * c - _swap_halves(dk_rot) * s).astype(BF16),
            jnp.concatenate([dvr_ref[h] for h in ret_heads], axis=1).astype(BF16),
            jnp.concatenate([dgr_ref[h] for h in ret_heads], axis=1).astype(BF16)], axis=1)
        dp_ref[...] = dproj
        dh = _dot_nt(dproj, w_ref[...])
        dx, h, dg = _rms_bwd(x_ref[...], g_ref[...], dh)
        dx_ref[...] = dy_ref[...] + dx
        h_ref[...] = h

        @pl.when(i == 0)
        def _():
            dg_ref[...] = dg

        @pl.when(i > 0)
        def _():
            dg_ref[...] += dg

    row = lambda i: (i, 0)
    one = lambda i: (0, 0)
    tiles = pl.BlockSpec((N_SB_HEADS, 1, HEAD_DIM, tm), lambda i: (0, i, 0, 0))
    return pl.pallas_call(
        body, name="mix_in_bwd",
        grid=(S // tm,),
        in_specs=[
            pl.BlockSpec((tm, 2 * D_CONV), row),
            _heads_spec(N_SB_HEADS, tm), tiles, tiles,
            _heads_spec(N_RET_HEADS, tm), _heads_spec(N_RET_HEADS, tm), _heads_spec(N_RET_HEADS, tm),
            _heads_spec(N_RET_HEADS, tm),
            pl.BlockSpec((tm, D_RET), row),
            pl.BlockSpec((tm, D_RET), row),
            pl.BlockSpec((D_MODEL, D_IN_PROJ), one, pipeline_mode=pl.Buffered(1)),
            pl.BlockSpec((tm, D_MODEL), row),
            pl.BlockSpec((1, D_MODEL), one),
            pl.BlockSpec((tm, D_MODEL), row),
        ],
        out_specs=[
            pl.BlockSpec((tm, D_MODEL), row),
            pl.BlockSpec((tm, D_MODEL), row),
            pl.BlockSpec((tm, D_IN_PROJ), row),
            pl.BlockSpec((1, D_MODEL), one),
        ],
        out_shape=[
            jax.ShapeDtypeStruct((S, D_MODEL), F32),
            jax.ShapeDtypeStruct((S, D_MODEL), BF16),
            jax.ShapeDtypeStruct((S, D_IN_PROJ), BF16),
            jax.ShapeDtypeStruct((1, D_MODEL), F32),
        ],
        compiler_params=_params(1),
    )(du, dq, dkt, dvt, dqr, dkr, dvr, dgr, cos, sin, w, x, g, dy)


def mix_out_fwd(y_conv, o_sb, o_ret, w, x, tm=ROW_TILE):
    S = x.shape[0]

    def body(yc_ref, sb_ref, rt_ref, w_ref, x_ref, o_ref, ycat_ref):
        ycat = jnp.concatenate(
            [yc_ref[...]] + [sb_ref[h].astype(BF16) for h in range(N_SB_HEADS)]
            + [rt_ref[h].astype(BF16) for h in range(N_RET_HEADS)], axis=1)
        ycat_ref[...] = ycat
        o_ref[...] = x_ref[...] + _dot(ycat, w_ref[...])

    row = lambda i: (i, 0)
    return pl.pallas_call(
        body, name="mix_out_fwd",
        grid=(S // tm,),
        in_specs=[
            pl.BlockSpec((tm, D_CONV), row),
            _heads_spec(N_SB_HEADS, tm),
            _heads_spec(N_RET_HEADS, tm),
            pl.BlockSpec((D_MODEL, D_MODEL), lambda i: (0, 0)),
            pl.BlockSpec((tm, D_MODEL), row),
        ],
        out_specs=[pl.BlockSpec((tm, D_MODEL), row), pl.BlockSpec((tm, D_MODEL), row)],
        out_shape=[jax.ShapeDtypeStruct((S, D_MODEL), F32), jax.ShapeDtypeStruct((S, D_MODEL), BF16)],
        compiler_params=_params(1),
    )(y_conv, o_sb, o_ret, w, x)


def mix_out_bwd(dy, w, tm=ROW_TILE):
    S = dy.shape[0]

    def body(dy_ref, w_ref, dyb_ref, dc_ref, do_ref, dot_ref, dr_ref):
        d = dy_ref[...].astype(BF16)
        dyb_ref[...] = d
        dycat = _dot_nt(d, w_ref[...])
        dc_ref[...] = dycat[:, :D_CONV]
        for h in range(N_SB_HEADS):
            do = _head(dycat[:, D_CONV:D_CONV + D_SB], h).astype(BF16)
            do_ref[h] = do
            dot_ref[h] = do.T
        for h in range(N_RET_HEADS):
            dr_ref[h] = _head(dycat[:, D_CONV + D_SB:], h)

    row = lambda i: (i, 0)
    return pl.pallas_call(
        body, name="mix_out_bwd",
        grid=(S // tm,),
        in_specs=[
            pl.BlockSpec((tm, D_MODEL), row),
            pl.BlockSpec((D_MODEL, D_MODEL), lambda i: (0, 0)),
        ],
        out_specs=[
            pl.BlockSpec((tm, D_MODEL), row),
            pl.BlockSpec((tm, D_CONV), row),
            _heads_spec(N_SB_HEADS, tm),
            pl.BlockSpec((N_SB_HEADS, HEAD_DIM, tm), lambda i: (0, 0, i)),
            _heads_spec(N_RET_HEADS, tm),
        ],
        out_shape=[
            jax.ShapeDtypeStruct((S, D_MODEL), BF16),
            jax.ShapeDtypeStruct((S, D_CONV), F32),
            jax.ShapeDtypeStruct((N_SB_HEADS, S, HEAD_DIM), BF16),
            jax.ShapeDtypeStruct((N_SB_HEADS, HEAD_DIM, S), BF16),
            jax.ShapeDtypeStruct((N_RET_HEADS, S, HEAD_DIM), F32),
        ],
        compiler_params=_params(1),
    )(dy, w)


def _conv_ln(ypre, ln_g, ln_b):
    mu = jnp.mean(ypre, axis=-1, keepdims=True)
    yc = ypre - mu
    rstd = lax.rsqrt(jnp.mean(yc * yc, axis=-1, keepdims=True) + EPS)
    yn = yc * rstd
    return yn, rstd, yn * ln_g + ln_b


def conv_fwd(proj, cw, cb, ln_g, ln_b, tm=CONV_TILE):
    S = proj.shape[0]
    hb = tm // CONV_HALO

    def body(a_ref, b_ref, ap_ref, bp_ref, cw_ref, cb_ref, g_ref, bb_ref, y_ref, ypre_ref, v_sc):
        i = pl.program_id(0)
        prev = ap_ref[...] * _sigmoid(bp_ref[...])
        v_sc[pl.ds(0, CONV_HALO), :] = jnp.where(i > 0, prev, 0.0)
        v_sc[pl.ds(CONV_HALO, tm), :] = a_ref[...] * _sigmoid(b_ref[...])
        acc = jnp.zeros((tm, D_CONV), F32)
        for j in range(CONV_WIDTH):
            acc = acc + cw_ref[pl.ds(j, 1), :] * v_sc[pl.ds(CONV_HALO - (CONV_WIDTH - 1) + j, tm), :]
        ypre = acc + cb_ref[...]
        ypre_ref[...] = ypre
        _, _, z = _conv_ln(ypre, g_ref[...], bb_ref[...])
        y_ref[...] = (z * _sigmoid(z)).astype(BF16)

    one = lambda i: (0, 0)
    return pl.pallas_call(
        body, name="conv_fwd",
        grid=(S // tm,),
        in_specs=[
            pl.BlockSpec((tm, D_CONV), lambda i: (i, 0)),
            pl.BlockSpec((tm, D_CONV), lambda i: (i, 1)),
            pl.BlockSpec((CONV_HALO, D_CONV), lambda i: (jnp.maximum(i * hb - 1, 0), 0)),
            pl.BlockSpec((CONV_HALO, D_CONV), lambda i: (jnp.maximum(i * hb - 1, 0), 1)),
            pl.BlockSpec((CONV_HALO, D_CONV), one),
            pl.BlockSpec((1, D_CONV), one),
            pl.BlockSpec((1, D_CONV), one),
            pl.BlockSpec((1, D_CONV), one),
        ],
        out_specs=[pl.BlockSpec((tm, D_CONV), lambda i: (i, 0)), pl.BlockSpec((tm, D_CONV), lambda i: (i, 0))],
        out_shape=[jax.ShapeDtypeStruct((S, D_CONV), BF16), jax.ShapeDtypeStruct((S, D_CONV), F32)],
        scratch_shapes=[pltpu.VMEM((tm + CONV_HALO, D_CONV), F32)],
        compiler_params=_params(1),
    )(proj, proj, proj, proj, cw, cb, ln_g, ln_b)


def conv_bwd(dyc, ypre, proj, cw, ln_g, ln_b, tm=CONV_TILE):
    S = ypre.shape[0]
    hb = tm // CONV_HALO
    nblk = S // tm
    last_halo = S // CONV_HALO - 1

    def dpre(dy, yp, g, bb):
        yn, rstd, z = _conv_ln(yp, g, bb)
        sg = _sigmoid(z)
        dz = dy * (sg * (1.0 + z * (1.0 - sg)))
        dyn = dz * g
        d = rstd * (dyn - jnp.mean(dyn, axis=-1, keepdims=True) - yn * jnp.mean(dyn * yn, axis=-1, keepdims=True))
        return d, dz * yn, dz

    def body(dy_ref, yp_ref, dyn_ref, ypn_ref, a_ref, b_ref, ap_ref, bp_ref, cw_ref, g_ref, bb_ref,
             du_ref, dcw_ref, dsm_ref, d_sc, v_sc):
        i = pl.program_id(0)
        g = g_ref[...]
        bb = bb_ref[...]
        d_main, dgn, dz = dpre(dy_ref[...], yp_ref[...], g, bb)
        d_next, _, _ = dpre(dyn_ref[...], ypn_ref[...], g, bb)
        d_sc[pl.ds(0, tm), :] = d_main
        d_sc[pl.ds(tm, CONV_HALO), :] = jnp.where(i < nblk - 1, d_next, 0.0)
        a = a_ref[...]
        sb = _sigmoid(b_ref[...])
        prev = ap_ref[...] * _sigmoid(bp_ref[...])
        v_sc[pl.ds(0, CONV_HALO), :] = jnp.where(i > 0, prev, 0.0)
        v_sc[pl.ds(CONV_HALO, tm), :] = a * sb

        @pl.when(i == 0)
        def _():
            dcw_ref[...] = jnp.zeros_like(dcw_ref)
            dsm_ref[...] = jnp.zeros_like(dsm_ref)

        dv = jnp.zeros((tm, D_CONV), F32)
        for j in range(CONV_WIDTH):
            dv = dv + cw_ref[pl.ds(j, 1), :] * d_sc[pl.ds(CONV_WIDTH - 1 - j, tm), :]
            shifted = v_sc[pl.ds(CONV_HALO - (CONV_WIDTH - 1) + j, tm), :]
            dcw_ref[pl.ds(j, 1), :] += jnp.sum(d_main * shifted, axis=0, keepdims=True)
        du_ref[:, pl.ds(0, D_CONV)] = dv * sb
        du_ref[:, pl.ds(D_CONV, D_CONV)] = dv * a * sb * (1.0 - sb)
        dsm_ref[pl.ds(0, 1), :] += jnp.sum(d_main, axis=0, keepdims=True)
        dsm_ref[pl.ds(1, 1), :] += jnp.sum(dgn, axis=0, keepdims=True)
        dsm_ref[pl.ds(2, 1), :] += jnp.sum(dz, axis=0, keepdims=True)

    one = lambda i: (0, 0)
    prev_map = lambda c: (lambda i: (jnp.maximum(i * hb - 1, 0), c))
    next_map = lambda i: (jnp.minimum((i + 1) * hb, last_halo), 0)
    return pl.pallas_call(
        body, name="conv_bwd",
        grid=(nblk,),
        in_specs=[
            pl.BlockSpec((tm, D_CONV), lambda i: (i, 0)),
            pl.BlockSpec((tm, D_CONV), lambda i: (i, 0)),
            pl.BlockSpec((CONV_HALO, D_CONV), next_map),
            pl.BlockSpec((CONV_HALO, D_CONV), next_map),
            pl.BlockSpec((tm, D_CONV), lambda i: (i, 0)),
            pl.BlockSpec((tm, D_CONV), lambda i: (i, 1)),
            pl.BlockSpec((CONV_HALO, D_CONV), prev_map(0)),
            pl.BlockSpec((CONV_HALO, D_CONV), prev_map(1)),
            pl.BlockSpec((CONV_HALO, D_CONV), one),
            pl.BlockSpec((1, D_CONV), one),
            pl.BlockSpec((1, D_CONV), one),
        ],
        out_specs=[
            pl.BlockSpec((tm, 2 * D_CONV), lambda i: (i, 0)),
            pl.BlockSpec((CONV_HALO, D_CONV), one),
            pl.BlockSpec((8, D_CONV), one),
        ],
        out_shape=[
            jax.ShapeDtypeStruct((S, 2 * D_CONV), F32),
            jax.ShapeDtypeStruct((CONV_HALO, D_CONV), F32),
            jax.ShapeDtypeStruct((8, D_CONV), F32),
        ],
        scratch_shapes=[pltpu.VMEM((tm + CONV_HALO, D_CONV), F32), pltpu.VMEM((tm + CONV_HALO, D_CONV), F32)],
        compiler_params=_params(1),
    )(dyc, ypre, dyc, ypre, proj, proj, proj, proj, cw, ln_g, ln_b)


SB_GROUP = 8


def _softplus(z):
    neg_abs = lax.bitcast_convert_type(lax.bitcast_convert_type(z, jnp.uint32) | jnp.uint32(0x80000000), F32)
    return jnp.maximum(z, 0.0) + jnp.log(1.0 + jnp.exp(neg_abs))


def _full_groups(n, body):
    def step(t, c):
        body(t * SB_GROUP)
        return c

    lax.fori_loop(0, lax.div(n, SB_GROUP), step, 0)


def _last_group(n, body):
    r = lax.rem(n, SB_GROUP)
    for k in range(SB_GROUP):
        @pl.when(r == k)
        def _(k=k):
            body(k)


def _rows(xs):
    return xs[0] if len(xs) == 1 else jnp.concatenate(xs, axis=0)


def sb_fwd(q, k, v, comm=None, T=SB_TILE):
    H, S, dh = q.shape

    def body(q_ref, k_ref, v_ref, o_ref, tot_ref, acc_sc, car_sc):
        qb = pl.program_id(1)
        qv = q_ref[...]
        row = lax.broadcasted_iota(jnp.int32, (T, T), 0)
        col = lax.broadcasted_iota(jnp.int32, (T, T), 1)
        tri = jnp.where(row >= col, 1.0, 0.0).astype(BF16)
        causal = col < row
        acc_sc[...] = jnp.zeros_like(acc_sc)
        car_sc[...] = jnp.zeros_like(car_sc)

        def logits(kb, masked):
            ks = k_ref[pl.ds(pl.multiple_of(kb * T, T), T), :]
            z = _dot_nt(qv, ks)
            nb = _softplus(z)
            if masked:
                nb = jnp.where(causal, nb, 0.0)
            return z, nb.astype(BF16)

        def group(kbs, diag):
            parts = [logits(kb, d) for kb, d in zip(kbs, diag)]
            pall = _dot(_rows([nb for _, nb in parts]), tri)
            carry = car_sc[...]
            out = None
            for j, kb in enumerate(kbs):
                p = pall[j * T:(j + 1) * T]
                vs = v_ref[pl.ds(pl.multiple_of(kb * T, T), T), :]
                w = jnp.exp((parts[j][0] - carry) - p)
                if diag[j]:
                    w = jnp.where(causal, w, 0.0)
                o = _dot(w.astype(BF16), vs)
                out = o if out is None else out + o
                carry = carry + p[:, 0:1]
            acc_sc[...] += out
            car_sc[...] = carry

        _last_group(qb, lambda r: group([qb] + [qb - 1 - o for o in range(r)], [True] + [False] * r))
        rest = qb - lax.rem(qb, SB_GROUP)
        _full_groups(rest, lambda o: group([rest - 1 - o - j for j in range(SB_GROUP)], [False] * SB_GROUP))
        o_ref[...] = acc_sc[...]
        tot_ref[...] = car_sc[...]

    return _call(
        body, (q, k, v), comm, name="sb_fwd",
        grid=(H, S // T),
        in_specs=[
            pl.BlockSpec((None, T, dh), lambda h, i: (h, i, 0)),
            pl.BlockSpec((None, S, dh), lambda h, i: (h, 0, 0)),
            pl.BlockSpec((None, S, dh), lambda h, i: (h, 0, 0)),
        ],
        out_specs=[
            pl.BlockSpec((None, T, dh), lambda h, i: (h, i, 0)),
            pl.BlockSpec((None, T, 1), lambda h, i: (h, i, 0)),
        ],
        out_shape=[jax.ShapeDtypeStruct((H, S, dh), F32), jax.ShapeDtypeStruct((H, S, 1), F32)],
        scratch_shapes=[pltpu.VMEM((T, dh), F32), pltpu.VMEM((T, 1), F32)],
    )


def sb_bwd(q, k, v, do, qt, dot, tot, comm=None, T=SB_TILE):
    H, S, dh = q.shape
    nt = S // T

    def body(q_ref, k_ref, v_ref, do_ref, qt_ref, dot_ref, tot_ref, dq_ref, dk_ref, dv_ref, acc_sc, rc_sc, gc_sc):
        qb = pl.program_id(1)
        qv = q_ref[...]
        dov = do_ref[...]
        qtv = qt_ref[...]
        dotv = dot_ref[...]
        row = lax.broadcasted_iota(jnp.int32, (T, T), 0)
        col = lax.broadcasted_iota(jnp.int32, (T, T), 1)
        before = jnp.where(row < col, 1.0, 0.0).astype(BF16)
        causal = col < row
        acc_sc[...] = jnp.zeros_like(acc_sc)
        rc_sc[...] = tot_ref[...]
        gc_sc[...] = jnp.zeros_like(gc_sc)

        @pl.when(qb == 0)
        def _():
            dk_ref[...] = jnp.zeros_like(dk_ref)
            dv_ref[...] = jnp.zeros_like(dv_ref)

        def first(kb, masked):
            start = pl.multiple_of(kb * T, T)
            z = _dot_nt(qv, k_ref[pl.ds(start, T), :])
            nb = _softplus(z)
            sig = jnp.exp(z - nb)
            if masked:
                nb = jnp.where(causal, nb, 0.0)
            dw = _dot_nt(dov, v_ref[pl.ds(start, T), :])
            return z, sig, nb.astype(BF16), dw

        def group(kbs, diag):
            parts = [first(kb, d) for kb, d in zip(kbs, diag)]
            pall = _dot(_rows([p[2] for p in parts]), before)
            rc = rc_sc[...]
            ws, gs, ghs = [], [], []
            for j in range(len(kbs)):
                z, _, nbh, dw = parts[j]
                p = pall[j * T:(j + 1) * T]
                w = jnp.exp((z - rc) + p)
                rc = rc - (p[:, T - 1:T] + nbh[:, T - 1:T].astype(F32))
                if diag[j]:
                    w = jnp.where(causal, w, 0.0)
                g = dw * w
                ws.append(w.astype(BF16))
                gs.append(g)
                ghs.append(g.astype(BF16))
            glall = _dot(_rows(ghs), before)
            gc = gc_sc[...]
            dq = None
            for j, kb in enumerate(kbs):
                ks = k_ref[pl.ds(pl.multiple_of(kb * T, T), T), :]
                gl = glall[j * T:(j + 1) * T]
                dz = gs[j] - parts[j][1] * (gs[j] + (gl + gc))
                gc = gc + gl[:, T - 1:T] + ghs[j][:, T - 1:T].astype(F32)
                if diag[j]:
                    dz = jnp.where(causal, dz, 0.0)
                dzb = dz.astype(BF16)
                d = _dot(dzb, ks)
                dq = d if dq is None else dq + d
                dk_ref[kb] += _dot(qtv, dzb)
                dv_ref[kb] += _dot(dotv, ws[j])
            acc_sc[...] += dq
            rc_sc[...] = rc
            gc_sc[...] = gc

        _full_groups(qb, lambda o: group([o + j for j in range(SB_GROUP)], [False] * SB_GROUP))
        rest = qb - lax.rem(qb, SB_GROUP)
        _last_group(qb, lambda r: group([rest + j for j in range(r)] + [qb], [False] * r + [True]))
        dq_ref[...] = acc_sc[...]

    return _call(
        body, (q, k, v, do, qt, dot, tot), comm, name="sb_bwd",
        grid=(H, nt),
        in_specs=[
            pl.BlockSpec((None, T, dh), lambda h, i: (h, i, 0)),
            pl.BlockSpec((None, S, dh), lambda h, i: (h, 0, 0)),
            pl.BlockSpec((None, S, dh), lambda h, i: (h, 0, 0)),
            pl.BlockSpec((None, T, dh), lambda h, i: (h, i, 0)),
            pl.BlockSpec((None, dh, T), lambda h, i: (h, 0, i)),
            pl.BlockSpec((None, dh, T), lambda h, i: (h, 0, i)),
            pl.BlockSpec((None, T, 1), lambda h, i: (h, i, 0)),
        ],
        out_specs=[
            pl.BlockSpec((None, T, dh), lambda h, i: (h, i, 0)),
            pl.BlockSpec((None, nt, dh, T), lambda h, i: (h, 0, 0, 0)),
            pl.BlockSpec((None, nt, dh, T), lambda h, i: (h, 0, 0, 0)),
        ],
        out_shape=[jax.ShapeDtypeStruct((H, S, dh), F32), jax.ShapeDtypeStruct((H, nt, dh, T), F32),
                   jax.ShapeDtypeStruct((H, nt, dh, T), F32)],
        scratch_shapes=[pltpu.VMEM((T, dh), F32), pltpu.VMEM((T, 1), F32), pltpu.VMEM((T, 1), F32)],
    )


def _ret_tables(T=RET_TILE):
    hh = jnp.arange(N_RET_HEADS, dtype=F32)
    log_gamma = jnp.log1p(-jnp.exp2(-5.0 - hh))
    idx = jnp.arange(T, dtype=F32)
    diff = idx[:, None] - idx[None, :]
    ci = (jnp.arange(T) // 64)
    same = ci[:, None] == ci[None, :]
    earlier = ci[None, :] < ci[:, None]
    dist = jnp.where(same, jnp.abs(diff), diff)
    dmat = jnp.where(same | earlier, jnp.exp(log_gamma[:, None, None] * dist[None]), 0.0)
    ones = jnp.ones((1, 1, HEAD_DIM), F32)
    qdec = jnp.exp(log_gamma[:, None] * (idx + 1.0)[None, :])[:, :, None] * ones
    kdec = jnp.exp(log_gamma[:, None] * (T - 1.0 - idx)[None, :])[:, :, None] * ones
    bdec = jnp.exp(log_gamma * T)[:, None, None] * jnp.ones((1, HEAD_DIM, HEAD_DIM), F32)
    return dmat, qdec, kdec, bdec


def _rope_tables(S):
    half = HEAD_DIM // 2
    inv = 1.0 / (ROPE_BASE ** (jnp.arange(half, dtype=F32) / half))
    ang = jnp.arange(S).astype(F32)[:, None] * inv[None, :]
    c = jnp.cos(ang)
    s = jnp.sin(ang)
    cos = jnp.tile(jnp.concatenate([c, c], axis=1), (1, N_RET_HEADS))
    sin = jnp.tile(jnp.concatenate([-s, s], axis=1), (1, N_RET_HEADS))
    return cos, sin


def ret_fwd(q, k, v, gate, ng, tables, T=RET_TILE):
    H, S, dh = q.shape
    dmat, qdec, kdec, bdec = tables

    def body(q_ref, k_ref, v_ref, gt_ref, ng_ref, dm_ref, qd_ref, kd_ref, bd_ref, o_ref, y_ref, st_ref, s_sc):
        n = pl.program_id(1)

        @pl.when(n == 0)
        def _():
            s_sc[...] = jnp.zeros_like(s_sc)

        qv = q_ref[...]
        kv = k_ref[...]
        vv = v_ref[...]
        state = s_sc[...]
        st_ref[...] = state
        sc = (_dot_nt(qv, kv) * dm_ref[...]).astype(BF16)
        qd = (qv.astype(F32) * qd_ref[...]).astype(BF16)
        y = _dot(sc, vv) + _dot(qd, state.astype(BF16))
        y_ref[...] = y
        kd = (kv.astype(F32) * kd_ref[...]).astype(BF16)
        s_sc[...] = bd_ref[...] * state + _dot_tn(kd, vv)
        mu = jnp.mean(y, axis=-1, keepdims=True)
        yc = y - mu
        yn = yc * lax.rsqrt(jnp.mean(yc * yc, axis=-1, keepdims=True) + EPS)
        gt = gt_ref[...]
        o_ref[...] = gt * _sigmoid(gt) * (yn * ng_ref[...])

    blk = lambda h, n: (h, n, 0)
    head = lambda h, n: (h, 0, 0)
    return pl.pallas_call(
        body, name="ret_fwd",
        grid=(H, S // T),
        in_specs=[
            pl.BlockSpec((None, T, dh), blk),
            pl.BlockSpec((None, T, dh), blk),
            pl.BlockSpec((None, T, dh), blk),
            pl.BlockSpec((None, T, dh), blk),
            pl.BlockSpec((None, 1, dh), head),
            pl.BlockSpec((None, T, T), head),
            pl.BlockSpec((None, T, dh), head),
            pl.BlockSpec((None, T, dh), head),
            pl.BlockSpec((None, dh, dh), head),
        ],
        out_specs=[
            pl.BlockSpec((None, T, dh), blk),
            pl.BlockSpec((None, T, dh), blk),
            pl.BlockSpec((None, None, dh, dh), lambda h, n: (h, n, 0, 0)),
        ],
        out_shape=[
            jax.ShapeDtypeStruct((H, S, dh), F32),
            jax.ShapeDtypeStruct((H, S, dh), F32),
            jax.ShapeDtypeStruct((H, S // T, dh, dh), F32),
        ],
        scratch_shapes=[pltpu.VMEM((dh, dh), F32)],
        compiler_params=_params(2),
    )(q, k, v, gate, ng, dmat, qdec, kdec, bdec)


def ret_bwd(do, q, k, v, gate, ng, y, states, tables, T=RET_TILE):
    H, S, dh = q.shape
    nb = S // T
    dmat, qdec, kdec, bdec = tables

    def body(do_ref, q_ref, k_ref, v_ref, gt_ref, ng_ref, y_ref, st_ref, dm_ref, qd_ref, kd_ref, bd_ref,
             dq_ref, dk_ref, dv_ref, dgt_ref, dng_ref, u_sc):
        n = pl.program_id(1)

        @pl.when(n == 0)
        def _():
            u_sc[...] = jnp.zeros_like(u_sc)
            dng_ref[...] = jnp.zeros_like(dng_ref)

        yv = y_ref[...]
        mu = jnp.mean(yv, axis=-1, keepdims=True)
        yc = yv - mu
        rstd = lax.rsqrt(jnp.mean(yc * yc, axis=-1, keepdims=True) + EPS)
        yn = yc * rstd
        gt = gt_ref[...]
        sg = _sigmoid(gt)
        ngv = ng_ref[...]
        dout = do_ref[...]
        dgt_ref[...] = dout * (yn * ngv) * (sg * (1.0 + gt * (1.0 - sg)))
        dn = dout * (gt * sg)
        dng_ref[...] += jnp.sum(dn * yn, axis=0, keepdims=True)
        dyn = dn * ngv
        dy = rstd * (dyn - jnp.mean(dyn, axis=-1, keepdims=True) - yn * jnp.mean(dyn * yn, axis=-1, keepdims=True))
        dyb = dy.astype(BF16)

        qv = q_ref[...]
        kv = k_ref[...]
        vv = v_ref[...]
        dm = dm_ref[...]
        qdt = qd_ref[...]
        kdt = kd_ref[...]
        sb = st_ref[...].astype(BF16)
        u = u_sc[...]
        ub = u.astype(BF16)
        dqk = (_dot_nt(dyb, vv) * dm).astype(BF16)
        sc = (_dot_nt(qv, kv) * dm).astype(BF16)
        qd = (qv.astype(F32) * qdt).astype(BF16)
        kd = (kv.astype(F32) * kdt).astype(BF16)
        dq_ref[...] = _dot(dqk, kv) + qdt * _dot_nt(dyb, sb)
        dk_ref[...] = _dot_tn(dqk, qv) + kdt * _dot_nt(vv, ub)
        dv_ref[...] = _dot_tn(sc, dyb) + _dot(kd, ub)
        u_sc[...] = bd_ref[...] * u + _dot_tn(qd, dyb)

    blk = lambda h, n: (h, nb - 1 - n, 0)
    head = lambda h, n: (h, 0, 0)
    return pl.pallas_call(
        body, name="ret_bwd",
        grid=(H, nb),
        in_specs=[
            pl.BlockSpec((None, T, dh), blk),
            pl.BlockSpec((None, T, dh), blk),
            pl.BlockSpec((None, T, dh), blk),
            pl.BlockSpec((None, T, dh), blk),
            pl.BlockSpec((None, T, dh), blk),
            pl.BlockSpec((None, 1, dh), head),
            pl.BlockSpec((None, T, dh), blk),
            pl.BlockSpec((None, None, dh, dh), lambda h, n: (h, nb - 1 - n, 0, 0)),
            pl.BlockSpec((None, T, T), head),
            pl.BlockSpec((None, T, dh), head),
            pl.BlockSpec((None, T, dh), head),
            pl.BlockSpec((None, dh, dh), head),
        ],
        out_specs=[
            pl.BlockSpec((None, T, dh), blk),
            pl.BlockSpec((None, T, dh), blk),
            pl.BlockSpec((None, T, dh), blk),
            pl.BlockSpec((None, T, dh), blk),
            pl.BlockSpec((None, 1, dh), head),
        ],
        out_shape=[jax.ShapeDtypeStruct((H, S, dh), F32)] * 4 + [jax.ShapeDtypeStruct((H, 1, dh), F32)],
        scratch_shapes=[pltpu.VMEM((dh, dh), F32)],
        compiler_params=_params(2),
    )(do, q, k, v, gate, ng, y, states, dmat, qdec, kdec, bdec)


def loss_head(x, g, target, tm=ROW_TILE):
    S = x.shape[0]

    def body(x_ref, g_ref, t_ref, loss_ref, dx_ref, dg_ref):
        i = pl.program_id(0)
        xv = x_ref[...]
        gv = g_ref[...]
        _, xhat = _rms_stats(xv)
        err = xhat * gv - t_ref[...]
        part = 0.5 * jnp.sum(jnp.mean(err * err, axis=-1, keepdims=True), axis=0, keepdims=True)
        dx, _, dg = _rms_bwd(xv, gv, err * (1.0 / D_MODEL))
        dx_ref[...] = dx
        part = jnp.broadcast_to(part, (1, 128))

        @pl.when(i == 0)
        def _():
            loss_ref[...] = part
            dg_ref[...] = dg

        @pl.when(i > 0)
        def _():
            loss_ref[...] += part
            dg_ref[...] += dg

    row = lambda i: (i, 0)
    one = lambda i: (0, 0)
    return pl.pallas_call(
        body, name="loss_head",
        grid=(S // tm,),
        in_specs=[pl.BlockSpec((tm, D_MODEL), row), pl.BlockSpec((1, D_MODEL), one), pl.BlockSpec((tm, D_MODEL), row)],
        out_specs=[pl.BlockSpec((1, 128), one), pl.BlockSpec((tm, D_MODEL), row), pl.BlockSpec((1, D_MODEL), one)],
        out_shape=[
            jax.ShapeDtypeStruct((1, 128), F32),
            jax.ShapeDtypeStruct((S, D_MODEL), F32),
            jax.ShapeDtypeStruct((1, D_MODEL), F32),
        ],
        compiler_params=_params(1),
    )(x, g, target)


def adamw(parts, w, m, v, tr):
    L, R, C = w.shape
    nr = R // tr
    c1 = 1.0 / (1.0 - ADAM_B1 ** ADAM_STEP)
    c2 = 1.0 / (1.0 - ADAM_B2 ** ADAM_STEP)

    def body(*refs):
        p_refs = refs[:L]
        w_ref, m_ref, v_ref, g_ref, d_ref, mo_ref, vo_ref = refs[L:]
        l = pl.program_id(0)
        g = None
        for d in range(N_DEV):
            pd = p_refs[0][d].astype(F32)
            for ll in range(1, L):
                pd = jnp.where(l == ll, p_refs[ll][d].astype(F32), pd)
            g = pd if g is None else g + pd
        mn = ADAM_B1 * m_ref[...] + (1.0 - ADAM_B1) * g
        vn = ADAM_B2 * v_ref[...] + (1.0 - ADAM_B2) * (g * g)
        g_ref[...] = g
        mo_ref[...] = mn
        vo_ref[...] = vn
        d_ref[...] = -ADAM_LR * ((mn * c1) / (jnp.sqrt(vn * c2) + ADAM_EPS) + ADAM_WD * w_ref[...])

    def part_spec(ll):
        return pl.BlockSpec((N_DEV, tr, C), lambda l, i: (0, jnp.where(l == ll, i, jnp.where(l < ll, 0, nr - 1)), 0))

    blk = pl.BlockSpec((None, tr, C), lambda l, i: (l, i, 0))
    return pl.pallas_call(
        body, name="adamw",
        grid=(L, nr),
        in_specs=[part_spec(ll) for ll in range(L)] + [blk] * 3,
        out_specs=[blk] * 4,
        out_shape=[jax.ShapeDtypeStruct((L, R, C), F32)] * 4,
        compiler_params=_params(2),
    )(*parts, w, m, v)


def _my_id():
    return lax.axis_index("x") * 4 + lax.axis_index("y") * 2 + lax.axis_index("c")


def _peer(k):
    x, y, c = lax.axis_index("x"), lax.axis_index("y"), lax.axis_index("c")
    px = 1 - x if k & 4 else x
    py = 1 - y if k & 2 else y
    pc = 1 - c if k & 1 else c
    return (px, py, pc), px * 4 + py * 2 + pc


GATHER = "gather"
EXCHANGE = "exchange"


def _copies(kind, ins, outs, send_sems, recv_sems, local_sems, receive_side):
    me = _my_id()
    local, sends, recvs = [], [], []
    for t in range(len(ins)):
        src = ins[t] if kind == GATHER else ins[t].at[me]
        local.append(pltpu.make_async_copy(src, outs[t].at[me], local_sems.at[t]))
    for k in range(1, N_DEV):
        dev, pid = _peer(k)
        for t in range(len(ins)):
            sems = dict(send_sem=send_sems.at[t, k - 1], recv_sem=recv_sems.at[t, k - 1],
                        device_id=dev, device_id_type=pl.DeviceIdType.MESH)
            src = ins[t] if kind == GATHER else ins[t].at[pid]
            sends.append(pltpu.make_async_remote_copy(src_ref=src, dst_ref=outs[t].at[me], **sems))
            if receive_side:
                recvs.append(pltpu.make_async_remote_copy(src_ref=src, dst_ref=outs[t].at[pid], **sems))
    return local, sends, recvs


def _comm_start(kind, ins, outs, sems):
    local, sends, _ = _copies(kind, ins, outs, *sems, receive_side=False)
    for cp in local + sends:
        cp.start()


def _comm_wait(kind, ins, outs, sems):
    local, sends, recvs = _copies(kind, ins, outs, *sems, receive_side=True)
    for cp in recvs:
        cp.wait_recv()
    for cp in sends:
        cp.wait_send()
    for cp in local:
        cp.wait()


def _comm_shapes(kind, arrays):
    n = len(arrays)
    out_shape = [jax.ShapeDtypeStruct(((N_DEV,) + a.shape) if kind == GATHER else a.shape, a.dtype) for a in arrays]
    sems = [pltpu.SemaphoreType.DMA((n, N_DEV - 1)), pltpu.SemaphoreType.DMA((n, N_DEV - 1)),
            pltpu.SemaphoreType.DMA((n,))]
    return out_shape, sems


def communicate(kind, arrays):
    n = len(arrays)

    def body(*refs):
        ins, outs, sems = refs[:n], refs[n:2 * n], refs[2 * n:]
        _comm_start(kind, ins, outs, sems)
        _comm_wait(kind, ins, outs, sems)

    out_shape, sems = _comm_shapes(kind, arrays)
    any_spec = pl.BlockSpec(memory_space=pl.ANY)
    return pl.pallas_call(
        body, name=kind, in_specs=[any_spec] * n, out_specs=[any_spec] * n, out_shape=out_shape, scratch_shapes=sems,
    )(*arrays)


def _call(body, operands, comm, *, name, grid, in_specs, out_specs, out_shape, scratch_shapes):
    if comm is None:
        outs = pl.pallas_call(body, name=name, grid=grid, in_specs=in_specs, out_specs=out_specs, out_shape=out_shape,
                              scratch_shapes=scratch_shapes, compiler_params=_params(len(grid)))(*operands)
        return outs, []
    kind, arrays = comm
    n, n_in, n_out, n_sc = len(arrays), len(in_specs), len(out_specs), len(scratch_shapes)

    def carrier(*refs):
        ins, cins = refs[:n_in], refs[n_in:n_in + n]
        refs = refs[n_in + n:]
        outs, couts = refs[:n_out], refs[n_out:n_out + n]
        scratch, sems = refs[n_out + n:n_out + n + n_sc], refs[n_out + n + n_sc:]
        steps = [pl.program_id(a) for a in range(len(grid))]
        first = functools.reduce(jnp.logical_and, [s == 0 for s in steps])
        last = functools.reduce(jnp.logical_and, [s == g - 1 for s, g in zip(steps, grid)])

        @pl.when(first)
        def _():
            _comm_start(kind, cins, couts, sems)

        body(*ins, *outs, *scratch)

        @pl.when(last)
        def _():
            _comm_wait(kind, cins, couts, sems)

    comm_shape, sems = _comm_shapes(kind, arrays)
    any_spec = pl.BlockSpec(memory_space=pl.ANY)
    outs = pl.pallas_call(
        carrier, name=f"{name}_{kind}", grid=grid,
        in_specs=list(in_specs) + [any_spec] * n,
        out_specs=list(out_specs) + [any_spec] * n,
        out_shape=list(out_shape) + comm_shape,
        scratch_shapes=list(scratch_shapes) + sems,
        compiler_params=_params(len(grid)),
    )(*operands, *arrays)
    return outs[:n_out], outs[n_out:]


def _row(v):
    return v.reshape(1, -1)


def _pad_taps(cw):
    return jnp.concatenate([cw, jnp.zeros((CONV_HALO - CONV_WIDTH, D_CONV), F32)], axis=0)


COL_SHARDED = ("ffn1_w_in", "mix_w_in", "ffn2_w_in")
ROW_SHARDED = ("ffn1_w_out", "mix_w_out", "ffn2_w_out")
SMALL = ("ffn1_norm", "mix_norm", "conv_b", "conv_ln_g", "conv_ln_b", "ret_norm_g", "ffn2_norm", "final_norm")
WEIGHTS = ("ffn1_norm", "ffn1_w_in", "ffn1_w_out", "mix_norm", "mix_w_in", "conv_w", "conv_b", "conv_ln_g",
           "conv_ln_b", "ret_norm_g", "mix_w_out", "ffn2_norm", "ffn2_w_in", "ffn2_w_out", "final_norm")
SMALL_ROWS = 32

FFN1 = ("ffn1_w_in", "ffn1_w_out")
MIX = ("mix_w_in", "mix_w_out")
FFN2 = ("ffn2_w_in", "ffn2_w_out")
STAGE_A = [(n, 0) for n in FFN1]
STAGE_B = [(n, 0) for n in MIX] + [("conv_w", None)]
STAGE_C = [(n, 0) for n in FFN2] + [(n, 1) for n in FFN1 + MIX + FFN2]
STAGE_D = [(n, 1) for n in FFN2]
STAGE_E = [(n, 1) for n in MIX + FFN1] + [(n, 0) for n in FFN2]
STAGE_F = [(n, 0) for n in MIX]
STAGE_G = [("ffn1_w_in", 0)]
STAGE_H = [("ffn1_w_out", 0)]


def _natural(name, got):
    if name in COL_SHARDED:
        return got.transpose(1, 0, 2).reshape(D_MODEL, -1)
    if name in ROW_SHARDED:
        return got.reshape(-1, D_MODEL)
    return got.transpose(1, 2, 0, 3).reshape(DEPTH, CONV_WIDTH, D_CONV)


def _by_device(name, grad):
    if name in COL_SHARDED:
        return grad.reshape(D_MODEL, N_DEV, -1).transpose(1, 0, 2)
    return grad.reshape(N_DEV, -1, D_MODEL)


def _pack_small(g):
    flat = jnp.concatenate([g[n].reshape(-1) for n in SMALL] + [g["conv_w"].reshape(-1)])
    flat = jnp.concatenate([flat, jnp.zeros((SMALL_ROWS * D_MODEL - flat.shape[0],), F32)])
    return flat.reshape(SMALL_ROWS, D_MODEL)


def _unpack_small(buf, like):
    flat = buf.reshape(-1)
    out, off = {}, 0
    for n in SMALL:
        size = int(np.prod(like[n].shape))
        out[n] = flat[off:off + size].reshape(like[n].shape)
        off += size
    size = DEPTH * CONV_WIDTH * D_CONV
    out["conv_w"] = flat[off:off + size].reshape(DEPTH, CONV_WIDTH, D_CONV)
    return out


def kernel(x, ffn1_norm, ffn1_w_in, ffn1_w_out, mix_norm, mix_w_in, conv_w, conv_b, conv_ln_g, conv_ln_b, ret_norm_g, mix_w_out, ffn2_norm, ffn2_w_in, ffn2_w_out, final_norm, loss_target, m_ffn1_norm, m_ffn1_w_in, m_ffn1_w_out, m_mix_norm, m_mix_w_in, m_conv_w, m_conv_b, m_conv_ln_g, m_conv_ln_b, m_ret_norm_g, m_mix_w_out, m_ffn2_norm, m_ffn2_w_in, m_ffn2_w_out, m_final_norm, v_ffn1_norm, v_ffn1_w_in, v_ffn1_w_out, v_mix_norm, v_mix_w_in, v_conv_w, v_conv_b, v_conv_ln_g, v_conv_ln_b, v_ret_norm_g, v_mix_w_out, v_ffn2_norm, v_ffn2_w_in, v_ffn2_w_out, v_final_norm):
    args = locals()
    w = {n: args[n] for n in WEIGHTS}
    m = {n: args["m_" + n] for n in WEIGHTS}
    v = {n: args["v_" + n] for n in WEIGHTS}
    me = _my_id()
    x = x[0]
    target = loss_target[0]
    S = x.shape[0]
    cos, sin = _rope_tables(S)
    tables = _ret_tables()

    full = {}

    def gather(keys):
        return GATHER, [w["conv_w"] if n == "conv_w" else w[n][l].astype(BF16) for n, l in keys]

    def gathered(keys, got):
        for (n, l), g in zip(keys, got):
            full[(n, l)] = _natural(n, g)

    gathered(STAGE_A, communicate(*gather(STAGE_A)))

    saved = []
    for l in range(DEPTH):
        sv = {"x0": x}
        (x, sv["gate1"], sv["up1"]), got = ffn_fwd(x, _row(w["ffn1_norm"][l]), full[("ffn1_w_in", l)],
                                                   full[("ffn1_w_out", l)], gather(STAGE_B) if l == 0 else None)
        gathered(STAGE_B if l == 0 else [], got)
        sv["x1"] = x
        (sv["u"], sv["q_sb"], sv["k_sb"], sv["v_sb"], sv["qt_sb"], sv["q_r"], sv["k_r"], sv["v_r"],
         sv["g_r"]) = mix_in_fwd(x, _row(w["mix_norm"][l]), full[("mix_w_in", l)], cos, sin)
        cw = _pad_taps(full[("conv_w", None)][l])
        y_conv, sv["ypre"] = conv_fwd(sv["u"], cw, _row(w["conv_b"][l]), _row(w["conv_ln_g"][l]), _row(w["conv_ln_b"][l]))
        (o_sb, sv["tot"]), got = sb_fwd(sv["q_sb"], sv["k_sb"], sv["v_sb"], gather(STAGE_C) if l == 0 else None)
        gathered(STAGE_C if l == 0 else [], got)
        ng = w["ret_norm_g"][l].reshape(N_RET_HEADS, 1, HEAD_DIM)
        o_r, sv["y_r"], sv["states"] = ret_fwd(sv["q_r"], sv["k_r"], sv["v_r"], sv["g_r"], ng, tables)
        x, sv["ycat"] = mix_out_fwd(y_conv, o_sb, o_r, full[("mix_w_out", l)], x)
        sv["x2"] = x
        (x, sv["gate2"], sv["up2"]), _ = ffn_fwd(x, _row(w["ffn2_norm"][l]), full[("ffn2_w_in", l)],
                                                 full[("ffn2_w_out", l)])
        saved.append(sv)

    loss_acc, dx, dg_final = loss_head(x, _row(w["final_norm"]), target)
    loss = lax.psum(loss_acc[0, 0], ("x", "y", "c"))

    g = {"final_norm": dg_final.reshape(D_MODEL)}
    received = {}

    def exchange(keys, extra=(), dtype=F32):
        return EXCHANGE, [_by_device(n, g[(n, l)]).astype(dtype) for n, l in keys] + list(extra)

    def exchanged(keys, got):
        for key, p in zip(keys, got):
            received[key] = p

    def ffn_back(dx, x_in, gate, up, norm, names, l, comm=None):
        (dx, h, dyh, dgate, dup, hid, dg), got = ffn_bwd(dx, x_in, _row(norm), gate, up, full[(names[0], l)],
                                                         full[(names[1], l)], comm)
        g[(names[0], l)] = matmul_tn(h, [dgate, dup], D_MODEL, FF_TILE, name="ffn_dw_in")
        if [(names[0], l)] == STAGE_G:
            g[(names[1], l)], got_g = matmul_tn(hid, [dyh], FF_TILE, D_MODEL, name="ffn_dw_out",
                                                comm=exchange(STAGE_G, dtype=BF16))
            exchanged(STAGE_G, got_g)
        else:
            g[(names[1], l)] = matmul_tn(hid, [dyh], FF_TILE, D_MODEL, name="ffn_dw_out")
        return dx, dg.reshape(D_MODEL), got

    for l in reversed(range(DEPTH)):
        sv = saved[l]
        dx, g[("ffn2_norm", l)], _ = ffn_back(dx, sv["x2"], sv["gate2"], sv["up2"], w["ffn2_norm"][l], FFN2, l)
        dxb, dy_conv, do_sb, dot_sb, do_r = mix_out_bwd(dx, full[("mix_w_out", l)])
        g[("mix_w_out", l)] = matmul_tn(sv["ycat"], [dxb], D_MODEL, D_MODEL, name="mix_dw_out")
        cw = _pad_taps(full[("conv_w", None)][l])
        du_conv, dcw, dsm = conv_bwd(dy_conv, sv["ypre"], sv["u"], cw, _row(w["conv_ln_g"][l]), _row(w["conv_ln_b"][l]))
        g[("conv_w", l)] = dcw[:CONV_WIDTH]
        g[("conv_b", l)], g[("conv_ln_g", l)], g[("conv_ln_b", l)] = dsm[0], dsm[1], dsm[2]
        stage = STAGE_D if l == DEPTH - 1 else STAGE_E
        (dq_sb, dk_t, dv_t), got = sb_bwd(sv["q_sb"], sv["k_sb"], sv["v_sb"], do_sb, sv["qt_sb"], dot_sb, sv["tot"],
                                          exchange(stage))
        exchanged(stage, got)
        ng = w["ret_norm_g"][l].reshape(N_RET_HEADS, 1, HEAD_DIM)
        dq_r, dk_r, dv_r, dg_r, dng = ret_bwd(do_r, sv["q_r"], sv["k_r"], sv["v_r"], sv["g_r"], ng, sv["y_r"],
                                              sv["states"], tables)
        g[("ret_norm_g", l)] = dng.reshape(D_RET)
        dx, h, dproj, dg = mix_in_bwd(du_conv, dq_sb, dk_t, dv_t, dq_r, dk_r, dv_r, dg_r, cos, sin,
                                      full[("mix_w_in", l)], sv["x1"], _row(w["mix_norm"][l]), dx)
        g[("mix_norm", l)] = dg.reshape(D_MODEL)
        g[("mix_w_in", l)] = matmul_tn(h, [dproj], D_MODEL, D_MODEL, name="mix_dw_in")
        dx, g[("ffn1_norm", l)], got = ffn_back(dx, sv["x0"], sv["gate1"], sv["up1"], w["ffn1_norm"][l], FFN1, l,
                                                exchange(STAGE_F) if l == 0 else None)
        exchanged(STAGE_F if l == 0 else [], got)
    grad_x = dx

    small_names = [n for n in SMALL if n != "final_norm"] + ["conv_w"]
    gs = {n: jnp.stack([g[(n, l)] for l in range(DEPTH)], axis=0) for n in small_names}
    gs["final_norm"] = g["final_norm"]
    small = _pack_small(gs)
    got = communicate(*exchange(STAGE_H, [jnp.broadcast_to(small[None], (N_DEV, SMALL_ROWS, D_MODEL))], dtype=BF16))
    exchanged(STAGE_H, got[:-1])

    grad, delta, new_m, new_v = {}, {}, {}, {}
    for n in COL_SHARDED + ROW_SHARDED:
        rows = w[n].shape[1]
        grad[n], delta[n], new_m[n], new_v[n] = adamw([received[(n, l)] for l in range(DEPTH)], w[n], m[n], v[n],
                                                      tr=min(rows // 2, 256))

    def small_pack(d):
        mine = dict(d)
        cwf = jnp.zeros((DEPTH, CONV_WIDTH, D_CONV), F32)
        mine["conv_w"] = lax.dynamic_update_slice(cwf, d["conv_w"], (0, 0, me * (D_CONV // N_DEV)))
        return _pack_small(mine)

    outs = adamw([got[-1]], small_pack(w)[None], small_pack(m)[None], small_pack(v)[None], tr=SMALL_ROWS)
    for dst, o in zip((grad, delta, new_m, new_v), outs):
        un = _unpack_small(o[0], w)
        un["conv_w"] = lax.dynamic_slice(un["conv_w"], (0, 0, me * (D_CONV // N_DEV)),
                                         (DEPTH, CONV_WIDTH, D_CONV // N_DEV))
        dst.update(un)

    return (loss, grad_x[None], *[grad[n] for n in WEIGHTS], *[delta[n] for n in WEIGHTS],
            *[new_m[n] for n in WEIGHTS], *[new_v[n] for n in WEIGHTS])
```

```python
import functools

import numpy as np
import jax
import jax.numpy as jnp
from jax import lax
from jax.experimental import pallas as pl
from jax.experimental.pallas import tpu as pltpu

F32 = jnp.float32
BF16 = jnp.bfloat16

D_MODEL = 1024
DEPTH = 2
D_FF = 2816
D_CONV = 256
CONV_WIDTH = 31
CONV_HALO = 32
D_SB = 512
N_SB_HEADS = 8
D_RET = 256
N_RET_HEADS = 4
HEAD_DIM = 64
D_IN_PROJ = 3072
ROPE_BASE = 10000.0
EPS = 1e-6
N_DEV = 8

ADAM_LR = 0.001
ADAM_B1 = 0.9
ADAM_B2 = 0.999
ADAM_EPS = 1e-08
ADAM_WD = 0.01
ADAM_STEP = 10

VMEM_LIMIT = 56 * 1024 * 1024
ROW_TILE = 512
FF_TILE = 1408
SB_TILE = 256
SB_ROWS = 512
RET_TILE = 512
CONV_TILE = 256

NT_DIMS = (((1,), (1,)), ((), ()))
TN_DIMS = (((0,), (0,)), ((), ()))


def _params(n_axes, vmem=VMEM_LIMIT):
    return pltpu.CompilerParams(dimension_semantics=("arbitrary",) * n_axes, vmem_limit_bytes=vmem)


def _dot(a, b):
    return jnp.dot(a, b, preferred_element_type=F32)


def _dot_nt(a, b):
    return lax.dot_general(a, b, NT_DIMS, preferred_element_type=F32)


def _dot_tn(a, b):
    return lax.dot_general(a, b, TN_DIMS, preferred_element_type=F32)


def _sigmoid(z):
    return 1.0 / (1.0 + jnp.exp(-z))


def _rms_stats(xv):
    r = lax.rsqrt(jnp.mean(xv * xv, axis=-1, keepdims=True) + EPS)
    return r, xv * r


def _rms_bwd(xv, g, dh):
    r, xhat = _rms_stats(xv)
    dxhat = dh * g
    dx = r * (dxhat - xhat * jnp.mean(dxhat * xhat, axis=-1, keepdims=True))
    dg = jnp.sum(dh * xhat, axis=0, keepdims=True)
    return dx, (xhat * g).astype(BF16), dg


def ffn_fwd(x, g, w_in, w_out, comm=None, tm=ROW_TILE):
    S = x.shape[0]
    nj = D_FF // FF_TILE

    def body(x_ref, g_ref, wg_ref, wu_ref, wo_ref, y_ref, gate_ref, up_ref, h_sc, acc_sc):
        j = pl.program_id(1)

        @pl.when(j == 0)
        def _():
            _, xhat = _rms_stats(x_ref[...])
            h_sc[...] = (xhat * g_ref[...]).astype(BF16)
            acc_sc[...] = jnp.zeros_like(acc_sc)

        h = h_sc[...]
        gt = _dot(h, wg_ref[...])
        up = _dot(h, wu_ref[...])
        gate_ref[...] = gt.astype(BF16)
        up_ref[...] = up.astype(BF16)
        hid = (gt * _sigmoid(gt) * up).astype(BF16)
        acc_sc[...] += _dot(hid, wo_ref[...])

        @pl.when(j == nj - 1)
        def _():
            y_ref[...] = x_ref[...] + 0.5 * acc_sc[...]

    return _call(
        body, (x, g, w_in, w_in, w_out), comm, name="ffn_fwd",
        grid=(S // tm, nj),
        in_specs=[
            pl.BlockSpec((tm, D_MODEL), lambda i, j: (i, 0)),
            pl.BlockSpec((1, D_MODEL), lambda i, j: (0, 0)),
            pl.BlockSpec((D_MODEL, FF_TILE), lambda i, j: (0, j)),
            pl.BlockSpec((D_MODEL, FF_TILE), lambda i, j: (0, j + nj)),
            pl.BlockSpec((FF_TILE, D_MODEL), lambda i, j: (j, 0)),
        ],
        out_specs=[
            pl.BlockSpec((tm, D_MODEL), lambda i, j: (i, 0)),
            pl.BlockSpec((tm, FF_TILE), lambda i, j: (i, j)),
            pl.BlockSpec((tm, FF_TILE), lambda i, j: (i, j)),
        ],
        out_shape=[
            jax.ShapeDtypeStruct((S, D_MODEL), F32),
            jax.ShapeDtypeStruct((S, D_FF), BF16),
            jax.ShapeDtypeStruct((S, D_FF), BF16),
        ],
        scratch_shapes=[pltpu.VMEM((tm, D_MODEL), BF16), pltpu.VMEM((tm, D_MODEL), F32)],
    )


def ffn_bwd(dy, x, g, gate, up, w_in, w_out, comm=None, tm=ROW_TILE // 2):
    S = x.shape[0]
    nj = D_FF // FF_TILE

    def body(dy_ref, x_ref, g_ref, gate_ref, up_ref, w_ref, wo_ref,
             dx_ref, h_ref, dyh_ref, dgate_ref, dup_ref, hid_ref, dg_ref):
        i = pl.program_id(0)
        d2 = (0.5 * dy_ref[...]).astype(BF16)
        dyh_ref[...] = d2
        dh = None
        for j in range(nj):
            cols = pl.ds(j * FF_TILE, FF_TILE)
            dhid = _dot_nt(d2, wo_ref[cols, :])
            gt = gate_ref[:, cols].astype(F32)
            u = up_ref[:, cols].astype(F32)
            sig = _sigmoid(gt)
            sl = gt * sig
            dgate = (dhid * u * (sig * (1.0 + gt * (1.0 - sig)))).astype(BF16)
            dup = (dhid * sl).astype(BF16)
            dgate_ref[:, cols] = dgate
            dup_ref[:, cols] = dup
            hid_ref[:, cols] = (sl * u).astype(BF16)
            part = _dot_nt(dgate, w_ref[:, cols]) + _dot_nt(dup, w_ref[:, pl.ds(D_FF + j * FF_TILE, FF_TILE)])
            dh = part if dh is None else dh + part
        dx, h, dg = _rms_bwd(x_ref[...], g_ref[...], dh)
        dx_ref[...] = dy_ref[...] + dx
        h_ref[...] = h

        @pl.when(i == 0)
        def _():
            dg_ref[...] = dg

        @pl.when(i > 0)
        def _():
            dg_ref[...] += dg

    row = lambda i: (i, 0)
    one = lambda i: (0, 0)
    resident = pl.Buffered(1)
    return _call(
        body, (dy, x, g, gate, up, w_in, w_out), comm, name="ffn_bwd",
        grid=(S // tm,),
        in_specs=[
            pl.BlockSpec((tm, D_MODEL), row),
            pl.BlockSpec((tm, D_MODEL), row),
            pl.BlockSpec((1, D_MODEL), one),
            pl.BlockSpec((tm, D_FF), row),
            pl.BlockSpec((tm, D_FF), row),
            pl.BlockSpec((D_MODEL, 2 * D_FF), one, pipeline_mode=resident),
            pl.BlockSpec((D_FF, D_MODEL), one, pipeline_mode=resident),
        ],
        out_specs=[
            pl.BlockSpec((tm, D_MODEL), row),
            pl.BlockSpec((tm, D_MODEL), row),
            pl.BlockSpec((tm, D_MODEL), row),
            pl.BlockSpec((tm, D_FF), row),
            pl.BlockSpec((tm, D_FF), row),
            pl.BlockSpec((tm, D_FF), row),
            pl.BlockSpec((1, D_MODEL), one),
        ],
        out_shape=[
            jax.ShapeDtypeStruct((S, D_MODEL), F32),
            jax.ShapeDtypeStruct((S, D_MODEL), BF16),
            jax.ShapeDtypeStruct((S, D_MODEL), BF16),
            jax.ShapeDtypeStruct((S, D_FF), BF16),
            jax.ShapeDtypeStruct((S, D_FF), BF16),
            jax.ShapeDtypeStruct((S, D_FF), BF16),
            jax.ShapeDtypeStruct((1, D_MODEL), F32),
        ],
        scratch_shapes=[],
    )


def matmul_tn(a, bs, ta, tn, tk=ROW_TILE, name="matmul_tn", comm=None):
    S, ka = a.shape
    nb = bs[0].shape[1]
    per = nb // tn

    def body(*refs):
        a_ref, b_refs, o_ref = refs[0], refs[1:-1], refs[-1]
        j = pl.program_id(1)
        k = pl.program_id(2)

        @pl.when(k == 0)
        def _():
            o_ref[...] = jnp.zeros_like(o_ref)

        for t, b_ref in enumerate(b_refs):
            @pl.when(lax.div(j, per) == t)
            def _(b_ref=b_ref):
                o_ref[...] += _dot_tn(a_ref[...], b_ref[...])

    def b_spec(t):
        def index(i, j, k):
            mine = lax.div(j, per) == t
            return jnp.where(mine, k, 0), jnp.where(mine, j - t * per, 0)
        return pl.BlockSpec((tk, tn), index)

    (out,), got = _call(
        body, (a, *bs), comm, name=name,
        grid=(ka // ta, per * len(bs), S // tk),
        in_specs=[pl.BlockSpec((tk, ta), lambda i, j, k: (k, i))] + [b_spec(t) for t in range(len(bs))],
        out_specs=[pl.BlockSpec((ta, tn), lambda i, j, k: (i, j))],
        out_shape=[jax.ShapeDtypeStruct((ka, nb * len(bs)), F32)],
        scratch_shapes=[],
    )
    return (out, got) if comm is not None else out


SB_COLS = (2 * D_CONV, 2 * D_CONV + D_SB, 2 * D_CONV + 2 * D_SB)
RET_COLS = tuple(2 * D_CONV + 3 * D_SB + j * D_RET for j in range(4))


def _swap_halves(x):
    n = x.shape[1]
    lane = lax.broadcasted_iota(jnp.int32, x.shape, 1)
    first = (lane % HEAD_DIM) < (HEAD_DIM // 2)
    return jnp.where(first, pltpu.roll(x, n - HEAD_DIM // 2, 1), pltpu.roll(x, HEAD_DIM // 2, 1))


def _head(x, h):
    return x[:, h * HEAD_DIM:(h + 1) * HEAD_DIM]


def _heads_spec(n_heads, tm):
    return pl.BlockSpec((n_heads, tm, HEAD_DIM), lambda i: (0, i, 0))


def mix_in_fwd(x, g, w, cos, sin, tm=ROW_TILE):
    S = x.shape[0]

    def body(x_ref, g_ref, w_ref, c_ref, s_ref, u_ref, q_ref, k_ref, v_ref, qt_ref, qr_ref, kr_ref, vr_ref, gr_ref):
        _, xhat = _rms_stats(x_ref[...])
        proj = _dot((xhat * g_ref[...]).astype(BF16), w_ref[...])
        u_ref[...] = proj[:, :2 * D_CONV]
        for h in range(N_SB_HEADS):
            q = (_head(proj[:, SB_COLS[0]:SB_COLS[1]], h) * 0.125).astype(BF16)
            q_ref[h] = q
            qt_ref[h] = q.T
            k_ref[h] = _head(proj[:, SB_COLS[1]:SB_COLS[2]], h).astype(BF16)
            v_ref[h] = _head(proj[:, SB_COLS[2]:RET_COLS[0]], h).astype(BF16)
        c = c_ref[...]
        s = s_ref[...]
        qv = proj[:, RET_COLS[0]:RET_COLS[1]]
        kv = proj[:, RET_COLS[1]:RET_COLS[2]]
        q_rot = ((qv * c + _swap_halves(qv) * s) * 0.125).astype(BF16)
        k_rot = (kv * c + _swap_halves(kv) * s).astype(BF16)
        for h in range(N_RET_HEADS):
            qr_ref[h] = _head(q_rot, h)
            kr_ref[h] = _head(k_rot, h)
            vr_ref[h] = _head(proj[:, RET_COLS[2]:RET_COLS[3]], h).astype(BF16)
            gr_ref[h] = _head(proj[:, RET_COLS[3]:], h)

    row = lambda i: (i, 0)
    one = lambda i: (0, 0)
    sb = jax.ShapeDtypeStruct((N_SB_HEADS, S, HEAD_DIM), BF16)
    ret = jax.ShapeDtypeStruct((N_RET_HEADS, S, HEAD_DIM), BF16)
    return pl.pallas_call(
        body, name="mix_in_fwd",
        grid=(S // tm,),
        in_specs=[
            pl.BlockSpec((tm, D_MODEL), row),
            pl.BlockSpec((1, D_MODEL), one),
            pl.BlockSpec((D_MODEL, D_IN_PROJ), one, pipeline_mode=pl.Buffered(1)),
            pl.BlockSpec((tm, D_RET), row),
            pl.BlockSpec((tm, D_RET), row),
        ],
        out_specs=[
            pl.BlockSpec((tm, 2 * D_CONV), row),
            _heads_spec(N_SB_HEADS, tm), _heads_spec(N_SB_HEADS, tm), _heads_spec(N_SB_HEADS, tm),
            pl.BlockSpec((N_SB_HEADS, HEAD_DIM, tm), lambda i: (0, 0, i)),
            _heads_spec(N_RET_HEADS, tm), _heads_spec(N_RET_HEADS, tm), _heads_spec(N_RET_HEADS, tm),
            _heads_spec(N_RET_HEADS, tm),
        ],
        out_shape=[
            jax.ShapeDtypeStruct((S, 2 * D_CONV), F32), sb, sb, sb,
            jax.ShapeDtypeStruct((N_SB_HEADS, HEAD_DIM, S), BF16),
            ret, ret, ret, jax.ShapeDtypeStruct((N_RET_HEADS, S, HEAD_DIM), F32),
        ],
        compiler_params=_params(1),
    )(x, g, w, cos, sin)


def mix_in_bwd(du, dq, dkt, dvt, dqr, dkr, dvr, dgr, cos, sin, w, x, g, dy, tm=SB_TILE):
    S = x.shape[0]
    assert dkt.shape[-1] == tm

    def body(du_ref, dq_ref, dkt_ref, dvt_ref, dqr_ref, dkr_ref, dvr_ref, dgr_ref, c_ref, s_ref, w_ref, x_ref, g_ref,
             dy_ref, dx_ref, h_ref, dp_ref, dg_ref):
        i = pl.program_id(0)
        sb_heads = range(N_SB_HEADS)
        ret_heads = range(N_RET_HEADS)
        c = c_ref[...]
        s = s_ref[...]
        dq_rot = jnp.concatenate([dqr_ref[h] for h in ret_heads], axis=1) * 0.125
        dk_rot = jnp.concatenate([dkr_ref[h] for h in ret_heads], axis=1)
        dproj = jnp.concatenate([
            du_ref[...].astype(BF16),
            jnp.concatenate([dq_ref[h] * 0.125 for h in sb_heads], axis=1).astype(BF16),
            jnp.concatenate([dkt_ref[h, 0].T for h in sb_heads], axis=1).astype(BF16),
            jnp.concatenate([dvt_ref[h, 0].T for h in sb_heads], axis=1).astype(BF16),
            (dq_rot * c - _swap_halves(dq_rot) * s).astype(BF16),
            (dk_rot * c - _swap_halves(dk_rot) * s).astype(BF16),
            jnp.concatenate([dvr_ref[h] for h in ret_heads], axis=1).astype(BF16),
            jnp.concatenate([dgr_ref[h] for h in ret_heads], axis=1).astype(BF16)], axis=1)
        dp_ref[...] = dproj
        dh = _dot_nt(dproj, w_ref[...])
        dx, h, dg = _rms_bwd(x_ref[...], g_ref[...], dh)
        dx_ref[...] = dy_ref[...] + dx
        h_ref[...] = h

        @pl.when(i == 0)
        def _():
            dg_ref[...] = dg

        @pl.when(i > 0)
        def _():
            dg_ref[...] += dg

    row = lambda i: (i, 0)
    one = lambda i: (0, 0)
    tiles = pl.BlockSpec((N_SB_HEADS, 1, HEAD_DIM, tm), lambda i: (0, i, 0, 0))
    return pl.pallas_call(
        body, name="mix_in_bwd",
        grid=(S // tm,),
        in_specs=[
            pl.BlockSpec((tm, 2 * D_CONV), row),
            _heads_spec(N_SB_HEADS, tm), tiles, tiles,
            _heads_spec(N_RET_HEADS, tm), _heads_spec(N_RET_HEADS, tm), _heads_spec(N_RET_HEADS, tm),
            _heads_spec(N_RET_HEADS, tm),
            pl.BlockSpec((tm, D_RET), row),
            pl.BlockSpec((tm, D_RET), row),
            pl.BlockSpec((D_MODEL, D_IN_PROJ), one, pipeline_mode=pl.Buffered(1)),
            pl.BlockSpec((tm, D_MODEL), row),
            pl.BlockSpec((1, D_MODEL), one),
            pl.BlockSpec((tm, D_MODEL), row),
        ],
        out_specs=[
            pl.BlockSpec((tm, D_MODEL), row),
            pl.BlockSpec((tm, D_MODEL), row),
            pl.BlockSpec((tm, D_IN_PROJ), row),
            pl.BlockSpec((1, D_MODEL), one),
        ],
        out_shape=[
            jax.ShapeDtypeStruct((S, D_MODEL), F32),
            jax.ShapeDtypeStruct((S, D_MODEL), BF16),
            jax.ShapeDtypeStruct((S, D_IN_PROJ), BF16),
            jax.ShapeDtypeStruct((1, D_MODEL), F32),
        ],
        compiler_params=_params(1),
    )(du, dq, dkt, dvt, dqr, dkr, dvr, dgr, cos, sin, w, x, g, dy)


def mix_out_fwd(y_conv, o_sb, o_ret, w, x, tm=ROW_TILE):
    S = x.shape[0]

    def body(yc_ref, sb_ref, rt_ref, w_ref, x_ref, o_ref, ycat_ref):
        ycat = jnp.concatenate(
            [yc_ref[...]] + [sb_ref[h].astype(BF16) for h in range(N_SB_HEADS)]
            + [rt_ref[h].astype(BF16) for h in range(N_RET_HEADS)], axis=1)
        ycat_ref[...] = ycat
        o_ref[...] = x_ref[...] + _dot(ycat, w_ref[...])

    row = lambda i: (i, 0)
    return pl.pallas_call(
        body, name="mix_out_fwd",
        grid=(S // tm,),
        in_specs=[
            pl.BlockSpec((tm, D_CONV), row),
            _heads_spec(N_SB_HEADS, tm),
            _heads_spec(N_RET_HEADS, tm),
            pl.BlockSpec((D_MODEL, D_MODEL), lambda i: (0, 0)),
            pl.BlockSpec((tm, D_MODEL), row),
        ],
        out_specs=[pl.BlockSpec((tm, D_MODEL), row), pl.BlockSpec((tm, D_MODEL), row)],
        out_shape=[jax.ShapeDtypeStruct((S, D_MODEL), F32), jax.ShapeDtypeStruct((S, D_MODEL), BF16)],
        compiler_params=_params(1),
    )(y_conv, o_sb, o_ret, w, x)


def mix_out_bwd(dy, w, tm=ROW_TILE):
    S = dy.shape[0]

    def body(dy_ref, w_ref, dyb_ref, dc_ref, do_ref, dot_ref, dr_ref):
        d = dy_ref[...].astype(BF16)
        dyb_ref[...] = d
        dycat = _dot_nt(d, w_ref[...])
        dc_ref[...] = dycat[:, :D_CONV]
        for h in range(N_SB_HEADS):
            do = _head(dycat[:, D_CONV:D_CONV + D_SB], h).astype(BF16)
            do_ref[h] = do
            dot_ref[h] = do.T
        for h in range(N_RET_HEADS):
            dr_ref[h] = _head(dycat[:, D_CONV + D_SB:], h)

    row = lambda i: (i, 0)
    return pl.pallas_call(
        body, name="mix_out_bwd",
        grid=(S // tm,),
        in_specs=[
            pl.BlockSpec((tm, D_MODEL), row),
            pl.BlockSpec((D_MODEL, D_MODEL), lambda i: (0, 0)),
        ],
        out_specs=[
            pl.BlockSpec((tm, D_MODEL), row),
            pl.BlockSpec((tm, D_CONV), row),
            _heads_spec(N_SB_HEADS, tm),
            pl.BlockSpec((N_SB_HEADS, HEAD_DIM, tm), lambda i: (0, 0, i)),
            _heads_spec(N_RET_HEADS, tm),
        ],
        out_shape=[
            jax.ShapeDtypeStruct((S, D_MODEL), BF16),
            jax.ShapeDtypeStruct((S, D_CONV), F32),
            jax.ShapeDtypeStruct((N_SB_HEADS, S, HEAD_DIM), BF16),
            jax.ShapeDtypeStruct((N_SB_HEADS, HEAD_DIM, S), BF16),
            jax.ShapeDtypeStruct((N_RET_HEADS, S, HEAD_DIM), F32),
        ],
        compiler_params=_params(1),
    )(dy, w)


def _conv_ln(ypre, ln_g, ln_b):
    mu = jnp.mean(ypre, axis=-1, keepdims=True)
    yc = ypre - mu
    rstd = lax.rsqrt(jnp.mean(yc * yc, axis=-1, keepdims=True) + EPS)
    yn = yc * rstd
    return yn, rstd, yn * ln_g + ln_b


def conv_fwd(proj, cw, cb, ln_g, ln_b, tm=CONV_TILE):
    S = proj.shape[0]
    hb = tm // CONV_HALO

    def body(a_ref, b_ref, ap_ref, bp_ref, cw_ref, cb_ref, g_ref, bb_ref, y_ref, ypre_ref, v_sc):
        i = pl.program_id(0)
        prev = ap_ref[...] * _sigmoid(bp_ref[...])
        v_sc[pl.ds(0, CONV_HALO), :] = jnp.where(i > 0, prev, 0.0)
        v_sc[pl.ds(CONV_HALO, tm), :] = a_ref[...] * _sigmoid(b_ref[...])
        acc = jnp.zeros((tm, D_CONV), F32)
        for j in range(CONV_WIDTH):
            acc = acc + cw_ref[pl.ds(j, 1), :] * v_sc[pl.ds(CONV_HALO - (CONV_WIDTH - 1) + j, tm), :]
        ypre = acc + cb_ref[...]
        ypre_ref[...] = ypre
        _, _, z = _conv_ln(ypre, g_ref[...], bb_ref[...])
        y_ref[...] = (z * _sigmoid(z)).astype(BF16)

    one = lambda i: (0, 0)
    return pl.pallas_call(
        body, name="conv_fwd",
        grid=(S // tm,),
        in_specs=[
            pl.BlockSpec((tm, D_CONV), lambda i: (i, 0)),
            pl.BlockSpec((tm, D_CONV), lambda i: (i, 1)),
            pl.BlockSpec((CONV_HALO, D_CONV), lambda i: (jnp.maximum(i * hb - 1, 0), 0)),
            pl.BlockSpec((CONV_HALO, D_CONV), lambda i: (jnp.maximum(i * hb - 1, 0), 1)),
            pl.BlockSpec((CONV_HALO, D_CONV), one),
            pl.BlockSpec((1, D_CONV), one),
            pl.BlockSpec((1, D_CONV), one),
            pl.BlockSpec((1, D_CONV), one),
        ],
        out_specs=[pl.BlockSpec((tm, D_CONV), lambda i: (i, 0)), pl.BlockSpec((tm, D_CONV), lambda i: (i, 0))],
        out_shape=[jax.ShapeDtypeStruct((S, D_CONV), BF16), jax.ShapeDtypeStruct((S, D_CONV), F32)],
        scratch_shapes=[pltpu.VMEM((tm + CONV_HALO, D_CONV), F32)],
        compiler_params=_params(1),
    )(proj, proj, proj, proj, cw, cb, ln_g, ln_b)


def conv_bwd(dyc, ypre, proj, cw, ln_g, ln_b, tm=CONV_TILE):
    S = ypre.shape[0]
    hb = tm // CONV_HALO
    nblk = S // tm
    last_halo = S // CONV_HALO - 1

    def dpre(dy, yp, g, bb):
        yn, rstd, z = _conv_ln(yp, g, bb)
        sg = _sigmoid(z)
        dz = dy * (sg * (1.0 + z * (1.0 - sg)))
        dyn = dz * g
        d = rstd * (dyn - jnp.mean(dyn, axis=-1, keepdims=True) - yn * jnp.mean(dyn * yn, axis=-1, keepdims=True))
        return d, dz * yn, dz

    def body(dy_ref, yp_ref, dyn_ref, ypn_ref, a_ref, b_ref, ap_ref, bp_ref, cw_ref, g_ref, bb_ref,
             du_ref, dcw_ref, dsm_ref, d_sc, v_sc):
        i = pl.program_id(0)
        g = g_ref[...]
        bb = bb_ref[...]
        d_main, dgn, dz = dpre(dy_ref[...], yp_ref[...], g, bb)
        d_next, _, _ = dpre(dyn_ref[...], ypn_ref[...], g, bb)
        d_sc[pl.ds(0, tm), :] = d_main
        d_sc[pl.ds(tm, CONV_HALO), :] = jnp.where(i < nblk - 1, d_next, 0.0)
        a = a_ref[...]
        sb = _sigmoid(b_ref[...])
        prev = ap_ref[...] * _sigmoid(bp_ref[...])
        v_sc[pl.ds(0, CONV_HALO), :] = jnp.where(i > 0, prev, 0.0)
        v_sc[pl.ds(CONV_HALO, tm), :] = a * sb

        @pl.when(i == 0)
        def _():
            dcw_ref[...] = jnp.zeros_like(dcw_ref)
            dsm_ref[...] = jnp.zeros_like(dsm_ref)

        dv = jnp.zeros((tm, D_CONV), F32)
        for j in range(CONV_WIDTH):
            dv = dv + cw_ref[pl.ds(j, 1), :] * d_sc[pl.ds(CONV_WIDTH - 1 - j, tm), :]
            shifted = v_sc[pl.ds(CONV_HALO - (CONV_WIDTH - 1) + j, tm), :]
            dcw_ref[pl.ds(j, 1), :] += jnp.sum(d_main * shifted, axis=0, keepdims=True)
        du_ref[:, pl.ds(0, D_CONV)] = dv * sb
        du_ref[:, pl.ds(D_CONV, D_CONV)] = dv * a * sb * (1.0 - sb)
        dsm_ref[pl.ds(0, 1), :] += jnp.sum(d_main, axis=0, keepdims=True)
        dsm_ref[pl.ds(1, 1), :] += jnp.sum(dgn, axis=0, keepdims=True)
        dsm_ref[pl.ds(2, 1), :] += jnp.sum(dz, axis=0, keepdims=True)

    one = lambda i: (0, 0)
    prev_map = lambda c: (lambda i: (jnp.maximum(i * hb - 1, 0), c))
    next_map = lambda i: (jnp.minimum((i + 1) * hb, last_halo), 0)
    return pl.pallas_call(
        body, name="conv_bwd",
        grid=(nblk,),
        in_specs=[
            pl.BlockSpec((tm, D_CONV), lambda i: (i, 0)),
            pl.BlockSpec((tm, D_CONV), lambda i: (i, 0)),
            pl.BlockSpec((CONV_HALO, D_CONV), next_map),
            pl.BlockSpec((CONV_HALO, D_CONV), next_map),
            pl.BlockSpec((tm, D_CONV), lambda i: (i, 0)),
            pl.BlockSpec((tm, D_CONV), lambda i: (i, 1)),
            pl.BlockSpec((CONV_HALO, D_CONV), prev_map(0)),
            pl.BlockSpec((CONV_HALO, D_CONV), prev_map(1)),
            pl.BlockSpec((CONV_HALO, D_CONV), one),
            pl.BlockSpec((1, D_CONV), one),
            pl.BlockSpec((1, D_CONV), one),
        ],
        out_specs=[
            pl.BlockSpec((tm, 2 * D_CONV), lambda i: (i, 0)),
            pl.BlockSpec((CONV_HALO, D_CONV), one),
            pl.BlockSpec((8, D_CONV), one),
        ],
        out_shape=[
            jax.ShapeDtypeStruct((S, 2 * D_CONV), F32),
            jax.ShapeDtypeStruct((CONV_HALO, D_CONV), F32),
            jax.ShapeDtypeStruct((8, D_CONV), F32),
        ],
        scratch_shapes=[pltpu.VMEM((tm + CONV_HALO, D_CONV), F32), pltpu.VMEM((tm + CONV_HALO, D_CONV), F32)],
        compiler_params=_params(1),
    )(dyc, ypre, dyc, ypre, proj, proj, proj, proj, cw, ln_g, ln_b)


SB_GROUP = 8


def _softplus(z):
    neg_abs = lax.bitcast_convert_type(lax.bitcast_convert_type(z, jnp.uint32) | jnp.uint32(0x80000000), F32)
    return jnp.maximum(z, 0.0) + jnp.log(1.0 + jnp.exp(neg_abs))


def _full_groups(n, body):
    def step(t, c):
        body(t * SB_GROUP)
        return c

    lax.fori_loop(0, lax.div(n, SB_GROUP), step, 0)


def _last_group(n, step, body):
    r = lax.rem(n, SB_GROUP)
    for k in range(0, SB_GROUP, step):
        @pl.when(r == k)
        def _(k=k):
            body(k)


def _rows(xs):
    return xs[0] if len(xs) == 1 else jnp.concatenate(xs, axis=0)


def sb_fwd(q, k, v, comm=None, T=SB_TILE, Q=SB_ROWS):
    H, S, dh = q.shape
    M = Q // T

    def body(q_ref, k_ref, v_ref, o_ref, tot_ref, acc_sc, car_sc):
        qb = pl.program_id(1)
        qv = q_ref[...]
        row = lax.broadcasted_iota(jnp.int32, (T, T), 0)
        col = lax.broadcasted_iota(jnp.int32, (T, T), 1)
        tri = jnp.where(row >= col, 1.0, 0.0).astype(BF16)
        qrow = lax.broadcasted_iota(jnp.int32, (Q, T), 0)
        kcol = lax.broadcasted_iota(jnp.int32, (Q, T), 1)
        causal = {d + 1: kcol + d * T < qrow for d in range(M)}
        acc_sc[...] = jnp.zeros_like(acc_sc)
        car_sc[...] = jnp.zeros_like(car_sc)

        def logits(kb, masked):
            ks = k_ref[pl.ds(pl.multiple_of(kb * T, T), T), :]
            z = _dot_nt(qv, ks)
            nb = _softplus(z)
            if masked:
                nb = jnp.where(causal[masked], nb, 0.0)
            return z, nb.astype(BF16)

        def group(kbs, diag):
            parts = [logits(kb, d) for kb, d in zip(kbs, diag)]
            pall = _dot(_rows([nb for _, nb in parts]), tri)
            carry = car_sc[...]
            out = None
            for j, kb in enumerate(kbs):
                p = pall[j * Q:(j + 1) * Q]
                vs = v_ref[pl.ds(pl.multiple_of(kb * T, T), T), :]
                w = jnp.exp((parts[j][0] - carry) - p)
                if diag[j]:
                    w = jnp.where(causal[diag[j]], w, 0.0)
                o = _dot(w.astype(BF16), vs)
                out = o if out is None else out + o
                carry = carry + p[:, 0:1]
            acc_sc[...] += out
            car_sc[...] = carry

        full = M * qb
        _last_group(full, M, lambda r: group([full + d for d in reversed(range(M))] + [full - 1 - o for o in range(r)],
                                             [d + 1 for d in reversed(range(M))] + [0] * r))
        rest = full - lax.rem(full, SB_GROUP)
        _full_groups(rest, lambda o: group([rest - 1 - o - j for j in range(SB_GROUP)], [0] * SB_GROUP))
        o_ref[...] = acc_sc[...]
        tot_ref[...] = car_sc[...]

    return _call(
        body, (q, k, v), comm, name="sb_fwd",
        grid=(H, S // Q),
        in_specs=[
            pl.BlockSpec((None, Q, dh), lambda h, i: (h, i, 0)),
            pl.BlockSpec((None, S, dh), lambda h, i: (h, 0, 0)),
            pl.BlockSpec((None, S, dh), lambda h, i: (h, 0, 0)),
        ],
        out_specs=[
            pl.BlockSpec((None, Q, dh), lambda h, i: (h, i, 0)),
            pl.BlockSpec((None, Q, 1), lambda h, i: (h, i, 0)),
        ],
        out_shape=[jax.ShapeDtypeStruct((H, S, dh), F32), jax.ShapeDtypeStruct((H, S, 1), F32)],
        scratch_shapes=[pltpu.VMEM((Q, dh), F32), pltpu.VMEM((Q, 1), F32)],
    )


def sb_bwd(q, k, v, do, qt, dot, tot, comm=None, T=SB_TILE, Q=SB_ROWS):
    H, S, dh = q.shape
    nt = S // T
    M = Q // T

    def body(q_ref, k_ref, v_ref, do_ref, qt_ref, dot_ref, tot_ref, dq_ref, dk_ref, dv_ref, acc_sc, rc_sc, gc_sc):
        qb = pl.program_id(1)
        qv = q_ref[...]
        dov = do_ref[...]
        qtv = qt_ref[...]
        dotv = dot_ref[...]
        row = lax.broadcasted_iota(jnp.int32, (T, T), 0)
        col = lax.broadcasted_iota(jnp.int32, (T, T), 1)
        before = jnp.where(row < col, 1.0, 0.0).astype(BF16)
        qrow = lax.broadcasted_iota(jnp.int32, (Q, T), 0)
        kcol = lax.broadcasted_iota(jnp.int32, (Q, T), 1)
        causal = {d + 1: kcol + d * T < qrow for d in range(M)}
        acc_sc[...] = jnp.zeros_like(acc_sc)
        rc_sc[...] = tot_ref[...]
        gc_sc[...] = jnp.zeros_like(gc_sc)

        @pl.when(qb == 0)
        def _():
            dk_ref[...] = jnp.zeros_like(dk_ref)
            dv_ref[...] = jnp.zeros_like(dv_ref)

        def first(kb, masked):
            start = pl.multiple_of(kb * T, T)
            z = _dot_nt(qv, k_ref[pl.ds(start, T), :])
            nb = _softplus(z)
            sig = jnp.exp(z - nb)
            if masked:
                nb = jnp.where(causal[masked], nb, 0.0)
            dw = _dot_nt(dov, v_ref[pl.ds(start, T), :])
            return z, sig, nb.astype(BF16), dw

        def group(kbs, diag):
            parts = [first(kb, d) for kb, d in zip(kbs, diag)]
            pall = _dot(_rows([p[2] for p in parts]), before)
            rc = rc_sc[...]
            ws, gs, ghs = [], [], []
            for j in range(len(kbs)):
                z, _, nbh, dw = parts[j]
                p = pall[j * Q:(j + 1) * Q]
                w = jnp.exp((z - rc) + p)
                rc = rc - (p[:, T - 1:T] + nbh[:, T - 1:T].astype(F32))
                if diag[j]:
                    w = jnp.where(causal[diag[j]], w, 0.0)
                g = dw * w
                ws.append(w.astype(BF16))
                gs.append(g)
                ghs.append(g.astype(BF16))
            glall = _dot(_rows(ghs), before)
            gc = gc_sc[...]
            dq = None
            for j, kb in enumerate(kbs):
                ks = k_ref[pl.ds(pl.multiple_of(kb * T, T), T), :]
                gl = glall[j * Q:(j + 1) * Q]
                dz = gs[j] - parts[j][1] * (gs[j] + (gl + gc))
                gc = gc + gl[:, T - 1:T] + ghs[j][:, T - 1:T].astype(F32)
                if diag[j]:
                    dz = jnp.where(causal[diag[j]], dz, 0.0)
                dzb = dz.astype(BF16)
                d = _dot(dzb, ks)
                dq = d if dq is None else dq + d
                dk_ref[kb] += _dot(qtv, dzb)
                dv_ref[kb] += _dot(dotv, ws[j])
            acc_sc[...] += dq
            rc_sc[...] = rc
            gc_sc[...] = gc

        full = M * qb
        _full_groups(full, lambda o: group([o + j for j in range(SB_GROUP)], [0] * SB_GROUP))
        rest = full - lax.rem(full, SB_GROUP)
        _last_group(full, M, lambda r: group([rest + j for j in range(r)] + [full + d for d in range(M)],
                                             [0] * r + [d + 1 for d in range(M)]))
        dq_ref[...] = acc_sc[...]

    return _call(
        body, (q, k, v, do, qt, dot, tot), comm, name="sb_bwd",
        grid=(H, S // Q),
        in_specs=[
            pl.BlockSpec((None, Q, dh), lambda h, i: (h, i, 0)),
            pl.BlockSpec((None, S, dh), lambda h, i: (h, 0, 0)),
            pl.BlockSpec((None, S, dh), lambda h, i: (h, 0, 0)),
            pl.BlockSpec((None, Q, dh), lambda h, i: (h, i, 0)),
            pl.BlockSpec((None, dh, Q), lambda h, i: (h, 0, i)),
            pl.BlockSpec((None, dh, Q), lambda h, i: (h, 0, i)),
            pl.BlockSpec((None, Q, 1), lambda h, i: (h, i, 0)),
        ],
        out_specs=[
            pl.BlockSpec((None, Q, dh), lambda h, i: (h, i, 0)),
            pl.BlockSpec((None, nt, dh, T), lambda h, i: (h, 0, 0, 0)),
            pl.BlockSpec((None, nt, dh, T), lambda h, i: (h, 0, 0, 0)),
        ],
        out_shape=[jax.ShapeDtypeStruct((H, S, dh), F32), jax.ShapeDtypeStruct((H, nt, dh, T), F32),
                   jax.ShapeDtypeStruct((H, nt, dh, T), F32)],
        scratch_shapes=[pltpu.VMEM((Q, dh), F32), pltpu.VMEM((Q, 1), F32), pltpu.VMEM((Q, 1), F32)],
    )


def _ret_tables(T=RET_TILE):
    hh = jnp.arange(N_RET_HEADS, dtype=F32)
    log_gamma = jnp.log1p(-jnp.exp2(-5.0 - hh))
    idx = jnp.arange(T, dtype=F32)
    diff = idx[:, None] - idx[None, :]
    ci = (jnp.arange(T) // 64)
    same = ci[:, None] == ci[None, :]
    earlier = ci[None, :] < ci[:, None]
    dist = jnp.where(same, jnp.abs(diff), diff)
    dmat = jnp.where(same | earlier, jnp.exp(log_gamma[:, None, None] * dist[None]), 0.0)
    ones = jnp.ones((1, 1, HEAD_DIM), F32)
    qdec = jnp.exp(log_gamma[:, None] * (idx + 1.0)[None, :])[:, :, None] * ones
    kdec = jnp.exp(log_gamma[:, None] * (T - 1.0 - idx)[None, :])[:, :, None] * ones
    bdec = jnp.exp(log_gamma * T)[:, None, None] * jnp.ones((1, HEAD_DIM, HEAD_DIM), F32)
    return dmat, qdec, kdec, bdec


def _rope_tables(S):
    half = HEAD_DIM // 2
    inv = 1.0 / (ROPE_BASE ** (jnp.arange(half, dtype=F32) / half))
    ang = jnp.arange(S).astype(F32)[:, None] * inv[None, :]
    c = jnp.cos(ang)
    s = jnp.sin(ang)
    cos = jnp.tile(jnp.concatenate([c, c], axis=1), (1, N_RET_HEADS))
    sin = jnp.tile(jnp.concatenate([-s, s], axis=1), (1, N_RET_HEADS))
    return cos, sin


def ret_fwd(q, k, v, gate, ng, tables, T=RET_TILE):
    H, S, dh = q.shape
    dmat, qdec, kdec, bdec = tables

    def body(q_ref, k_ref, v_ref, gt_ref, ng_ref, dm_ref, qd_ref, kd_ref, bd_ref, o_ref, y_ref, st_ref, s_sc):
        n = pl.program_id(1)

        @pl.when(n == 0)
        def _():
            s_sc[...] = jnp.zeros_like(s_sc)

        qv = q_ref[...]
        kv = k_ref[...]
        vv = v_ref[...]
        state = s_sc[...]
        st_ref[...] = state
        sc = (_dot_nt(qv, kv) * dm_ref[...]).astype(BF16)
        qd = (qv.astype(F32) * qd_ref[...]).astype(BF16)
        y = _dot(sc, vv) + _dot(qd, state.astype(BF16))
        y_ref[...] = y
        kd = (kv.astype(F32) * kd_ref[...]).astype(BF16)
        s_sc[...] = bd_ref[...] * state + _dot_tn(kd, vv)
        mu = jnp.mean(y, axis=-1, keepdims=True)
        yc = y - mu
        yn = yc * lax.rsqrt(jnp.mean(yc * yc, axis=-1, keepdims=True) + EPS)
        gt = gt_ref[...]
        o_ref[...] = gt * _sigmoid(gt) * (yn * ng_ref[...])

    blk = lambda h, n: (h, n, 0)
    head = lambda h, n: (h, 0, 0)
    return pl.pallas_call(
        body, name="ret_fwd",
        grid=(H, S // T),
        in_specs=[
            pl.BlockSpec((None, T, dh), blk),
            pl.BlockSpec((None, T, dh), blk),
            pl.BlockSpec((None, T, dh), blk),
            pl.BlockSpec((None, T, dh), blk),
            pl.BlockSpec((None, 1, dh), head),
            pl.BlockSpec((None, T, T), head),
            pl.BlockSpec((None, T, dh), head),
            pl.BlockSpec((None, T, dh), head),
            pl.BlockSpec((None, dh, dh), head),
        ],
        out_specs=[
            pl.BlockSpec((None, T, dh), blk),
            pl.BlockSpec((None, T, dh), blk),
            pl.BlockSpec((None, None, dh, dh), lambda h, n: (h, n, 0, 0)),
        ],
        out_shape=[
            jax.ShapeDtypeStruct((H, S, dh), F32),
            jax.ShapeDtypeStruct((H, S, dh), F32),
            jax.ShapeDtypeStruct((H, S // T, dh, dh), F32),
        ],
        scratch_shapes=[pltpu.VMEM((dh, dh), F32)],
        compiler_params=_params(2),
    )(q, k, v, gate, ng, dmat, qdec, kdec, bdec)


def ret_bwd(do, q, k, v, gate, ng, y, states, tables, T=RET_TILE):
    H, S, dh = q.shape
    nb = S // T
    dmat, qdec, kdec, bdec = tables

    def body(do_ref, q_ref, k_ref, v_ref, gt_ref, ng_ref, y_ref, st_ref, dm_ref, qd_ref, kd_ref, bd_ref,
             dq_ref, dk_ref, dv_ref, dgt_ref, dng_ref, u_sc):
        n = pl.program_id(1)

        @pl.when(n == 0)
        def _():
            u_sc[...] = jnp.zeros_like(u_sc)
            dng_ref[...] = jnp.zeros_like(dng_ref)

        yv = y_ref[...]
        mu = jnp.mean(yv, axis=-1, keepdims=True)
        yc = yv - mu
        rstd = lax.rsqrt(jnp.mean(yc * yc, axis=-1, keepdims=True) + EPS)
        yn = yc * rstd
        gt = gt_ref[...]
        sg = _sigmoid(gt)
        ngv = ng_ref[...]
        dout = do_ref[...]
        dgt_ref[...] = dout * (yn * ngv) * (sg * (1.0 + gt * (1.0 - sg)))
        dn = dout * (gt * sg)
        dng_ref[...] += jnp.sum(dn * yn, axis=0, keepdims=True)
        dyn = dn * ngv
        dy = rstd * (dyn - jnp.mean(dyn, axis=-1, keepdims=True) - yn * jnp.mean(dyn * yn, axis=-1, keepdims=True))
        dyb = dy.astype(BF16)

        qv = q_ref[...]
        kv = k_ref[...]
        vv = v_ref[...]
        dm = dm_ref[...]
        qdt = qd_ref[...]
        kdt = kd_ref[...]
        sb = st_ref[...].astype(BF16)
        u = u_sc[...]
        ub = u.astype(BF16)
        dqk = (_dot_nt(dyb, vv) * dm).astype(BF16)
        sc = (_dot_nt(qv, kv) * dm).astype(BF16)
        qd = (qv.astype(F32) * qdt).astype(BF16)
        kd = (kv.astype(F32) * kdt).astype(BF16)
        dq_ref[...] = _dot(dqk, kv) + qdt * _dot_nt(dyb, sb)
        dk_ref[...] = _dot_tn(dqk, qv) + kdt * _dot_nt(vv, ub)
        dv_ref[...] = _dot_tn(sc, dyb) + _dot(kd, ub)
        u_sc[...] = bd_ref[...] * u + _dot_tn(qd, dyb)

    blk = lambda h, n: (h, nb - 1 - n, 0)
    head = lambda h, n: (h, 0, 0)
    return pl.pallas_call(
        body, name="ret_bwd",
        grid=(H, nb),
        in_specs=[
            pl.BlockSpec((None, T, dh), blk),
            pl.BlockSpec((None, T, dh), blk),
            pl.BlockSpec((None, T, dh), blk),
            pl.BlockSpec((None, T, dh), blk),
            pl.BlockSpec((None, T, dh), blk),
            pl.BlockSpec((None, 1, dh), head),
            pl.BlockSpec((None, T, dh), blk),
            pl.BlockSpec((None, None, dh, dh), lambda h, n: (h, nb - 1 - n, 0, 0)),
            pl.BlockSpec((None, T, T), head),
            pl.BlockSpec((None, T, dh), head),
            pl.BlockSpec((None, T, dh), head),
            pl.BlockSpec((None, dh, dh), head),
        ],
        out_specs=[
            pl.BlockSpec((None, T, dh), blk),
            pl.BlockSpec((None, T, dh), blk),
            pl.BlockSpec((None, T, dh), blk),
            pl.BlockSpec((None, T, dh), blk),
            pl.BlockSpec((None, 1, dh), head),
        ],
        out_shape=[jax.ShapeDtypeStruct((H, S, dh), F32)] * 4 + [jax.ShapeDtypeStruct((H, 1, dh), F32)],
        scratch_shapes=[pltpu.VMEM((dh, dh), F32)],
        compiler_params=_params(2),
    )(do, q, k, v, gate, ng, y, states, dmat, qdec, kdec, bdec)


def loss_head(x, g, target, tm=ROW_TILE):
    S = x.shape[0]

    def body(x_ref, g_ref, t_ref, loss_ref, dx_ref, dg_ref):
        i = pl.program_id(0)
        xv = x_ref[...]
        gv = g_ref[...]
        _, xhat = _rms_stats(xv)
        err = xhat * gv - t_ref[...]
        part = 0.5 * jnp.sum(jnp.mean(err * err, axis=-1, keepdims=True), axis=0, keepdims=True)
        dx, _, dg = _rms_bwd(xv, gv, err * (1.0 / D_MODEL))
        dx_ref[...] = dx
        part = jnp.broadcast_to(part, (1, 128))

        @pl.when(i == 0)
        def _():
            loss_ref[...] = part
            dg_ref[...] = dg

        @pl.when(i > 0)
        def _():
            loss_ref[...] += part
            dg_ref[...] += dg

    row = lambda i: (i, 0)
    one = lambda i: (0, 0)
    return pl.pallas_call(
        body, name="loss_head",
        grid=(S // tm,),
        in_specs=[pl.BlockSpec((tm, D_MODEL), row), pl.BlockSpec((1, D_MODEL), one), pl.BlockSpec((tm, D_MODEL), row)],
        out_specs=[pl.BlockSpec((1, 128), one), pl.BlockSpec((tm, D_MODEL), row), pl.BlockSpec((1, D_MODEL), one)],
        out_shape=[
            jax.ShapeDtypeStruct((1, 128), F32),
            jax.ShapeDtypeStruct((S, D_MODEL), F32),
            jax.ShapeDtypeStruct((1, D_MODEL), F32),
        ],
        compiler_params=_params(1),
    )(x, g, target)


def adamw(parts, w, m, v, tr):
    L, R, C = w.shape
    nr = R // tr
    c1 = 1.0 / (1.0 - ADAM_B1 ** ADAM_STEP)
    c2 = 1.0 / (1.0 - ADAM_B2 ** ADAM_STEP)

    def body(*refs):
        p_refs = refs[:L]
        w_ref, m_ref, v_ref, g_ref, d_ref, mo_ref, vo_ref = refs[L:]
        l = pl.program_id(0)
        g = None
        for d in range(N_DEV):
            pd = p_refs[0][d].astype(F32)
            for ll in range(1, L):
                pd = jnp.where(l == ll, p_refs[ll][d].astype(F32), pd)
            g = pd if g is None else g + pd
        mn = ADAM_B1 * m_ref[...] + (1.0 - ADAM_B1) * g
        vn = ADAM_B2 * v_ref[...] + (1.0 - ADAM_B2) * (g * g)
        g_ref[...] = g
        mo_ref[...] = mn
        vo_ref[...] = vn
        d_ref[...] = -ADAM_LR * ((mn * c1) / (jnp.sqrt(vn * c2) + ADAM_EPS) + ADAM_WD * w_ref[...])

    def part_spec(ll):
        return pl.BlockSpec((N_DEV, tr, C), lambda l, i: (0, jnp.where(l == ll, i, jnp.where(l < ll, 0, nr - 1)), 0))

    blk = pl.BlockSpec((None, tr, C), lambda l, i: (l, i, 0))
    return pl.pallas_call(
        body, name="adamw",
        grid=(L, nr),
        in_specs=[part_spec(ll) for ll in range(L)] + [blk] * 3,
        out_specs=[blk] * 4,
        out_shape=[jax.ShapeDtypeStruct((L, R, C), F32)] * 4,
        compiler_params=_params(2),
    )(*parts, w, m, v)


def _my_id():
    return lax.axis_index("x") * 4 + lax.axis_index("y") * 2 + lax.axis_index("c")


def _peer(k):
    x, y, c = lax.axis_index("x"), lax.axis_index("y"), lax.axis_index("c")
    px = 1 - x if k & 4 else x
    py = 1 - y if k & 2 else y
    pc = 1 - c if k & 1 else c
    return (px, py, pc), px * 4 + py * 2 + pc


GATHER = "gather"
EXCHANGE = "exchange"


def _copies(kind, ins, outs, send_sems, recv_sems, local_sems, receive_side):
    me = _my_id()
    local, sends, recvs = [], [], []
    for t in range(len(ins)):
        src = ins[t] if kind == GATHER else ins[t].at[me]
        local.append(pltpu.make_async_copy(src, outs[t].at[me], local_sems.at[t]))
    for k in range(1, N_DEV):
        dev, pid = _peer(k)
        for t in range(len(ins)):
            sems = dict(send_sem=send_sems.at[t, k - 1], recv_sem=recv_sems.at[t, k - 1],
                        device_id=dev, device_id_type=pl.DeviceIdType.MESH)
            src = ins[t] if kind == GATHER else ins[t].at[pid]
            sends.append(pltpu.make_async_remote_copy(src_ref=src, dst_ref=outs[t].at[me], **sems))
            if receive_side:
                recvs.append(pltpu.make_async_remote_copy(src_ref=src, dst_ref=outs[t].at[pid], **sems))
    return local, sends, recvs


def _comm_start(kind, ins, outs, sems):
    local, sends, _ = _copies(kind, ins, outs, *sems, receive_side=False)
    for cp in local + sends:
        cp.start()


def _comm_wait(kind, ins, outs, sems):
    local, sends, recvs = _copies(kind, ins, outs, *sems, receive_side=True)
    for cp in recvs:
        cp.wait_recv()
    for cp in sends:
        cp.wait_send()
    for cp in local:
        cp.wait()


def _comm_shapes(kind, arrays):
    n = len(arrays)
    out_shape = [jax.ShapeDtypeStruct(((N_DEV,) + a.shape) if kind == GATHER else a.shape, a.dtype) for a in arrays]
    sems = [pltpu.SemaphoreType.DMA((n, N_DEV - 1)), pltpu.SemaphoreType.DMA((n, N_DEV - 1)),
            pltpu.SemaphoreType.DMA((n,))]
    return out_shape, sems


def communicate(kind, arrays):
    n = len(arrays)

    def body(*refs):
        ins, outs, sems = refs[:n], refs[n:2 * n], refs[2 * n:]
        _comm_start(kind, ins, outs, sems)
        _comm_wait(kind, ins, outs, sems)

    out_shape, sems = _comm_shapes(kind, arrays)
    any_spec = pl.BlockSpec(memory_space=pl.ANY)
    return pl.pallas_call(
        body, name=kind, in_specs=[any_spec] * n, out_specs=[any_spec] * n, out_shape=out_shape, scratch_shapes=sems,
    )(*arrays)


def _call(body, operands, comm, *, name, grid, in_specs, out_specs, out_shape, scratch_shapes):
    if comm is None:
        outs = pl.pallas_call(body, name=name, grid=grid, in_specs=in_specs, out_specs=out_specs, out_shape=out_shape,
                              scratch_shapes=scratch_shapes, compiler_params=_params(len(grid)))(*operands)
        return outs, []
    kind, arrays = comm
    n, n_in, n_out, n_sc = len(arrays), len(in_specs), len(out_specs), len(scratch_shapes)

    def carrier(*refs):
        ins, cins = refs[:n_in], refs[n_in:n_in + n]
        refs = refs[n_in + n:]
        outs, couts = refs[:n_out], refs[n_out:n_out + n]
        scratch, sems = refs[n_out + n:n_out + n + n_sc], refs[n_out + n + n_sc:]
        steps = [pl.program_id(a) for a in range(len(grid))]
        first = functools.reduce(jnp.logical_and, [s == 0 for s in steps])
        last = functools.reduce(jnp.logical_and, [s == g - 1 for s, g in zip(steps, grid)])

        @pl.when(first)
        def _():
            _comm_start(kind, cins, couts, sems)

        body(*ins, *outs, *scratch)

        @pl.when(last)
        def _():
            _comm_wait(kind, cins, couts, sems)

    comm_shape, sems = _comm_shapes(kind, arrays)
    any_spec = pl.BlockSpec(memory_space=pl.ANY)
    outs = pl.pallas_call(
        carrier, name=f"{name}_{kind}", grid=grid,
        in_specs=list(in_specs) + [any_spec] * n,
        out_specs=list(out_specs) + [any_spec] * n,
        out_shape=list(out_shape) + comm_shape,
        scratch_shapes=list(scratch_shapes) + sems,
        compiler_params=_params(len(grid)),
    )(*operands, *arrays)
    return outs[:n_out], outs[n_out:]


def _row(v):
    return v.reshape(1, -1)


def _pad_taps(cw):
    return jnp.concatenate([cw, jnp.zeros((CONV_HALO - CONV_WIDTH, D_CONV), F32)], axis=0)


COL_SHARDED = ("ffn1_w_in", "mix_w_in", "ffn2_w_in")
ROW_SHARDED = ("ffn1_w_out", "mix_w_out", "ffn2_w_out")
SMALL = ("ffn1_norm", "mix_norm", "conv_b", "conv_ln_g", "conv_ln_b", "ret_norm_g", "ffn2_norm", "final_norm")
WEIGHTS = ("ffn1_norm", "ffn1_w_in", "ffn1_w_out", "mix_norm", "mix_w_in", "conv_w", "conv_b", "conv_ln_g",
           "conv_ln_b", "ret_norm_g", "mix_w_out", "ffn2_norm", "ffn2_w_in", "ffn2_w_out", "final_norm")
SMALL_ROWS = 32

FFN1 = ("ffn1_w_in", "ffn1_w_out")
MIX = ("mix_w_in", "mix_w_out")
FFN2 = ("ffn2_w_in", "ffn2_w_out")
STAGE_A = [(n, 0) for n in FFN1]
STAGE_B = [(n, 0) for n in MIX] + [("conv_w", None)]
STAGE_C = [(n, 0) for n in FFN2] + [(n, 1) for n in FFN1 + MIX + FFN2]
STAGE_D = [(n, 1) for n in FFN2]
STAGE_E = [(n, 1) for n in MIX + FFN1] + [(n, 0) for n in FFN2]
STAGE_F = [(n, 0) for n in MIX]
STAGE_G = [("ffn1_w_in", 0)]
STAGE_H = [("ffn1_w_out", 0)]


def _natural(name, got):
    if name in COL_SHARDED:
        return got.transpose(1, 0, 2).reshape(D_MODEL, -1)
    if name in ROW_SHARDED:
        return got.reshape(-1, D_MODEL)
    return got.transpose(1, 2, 0, 3).reshape(DEPTH, CONV_WIDTH, D_CONV)


def _by_device(name, grad):
    if name in COL_SHARDED:
        return grad.reshape(D_MODEL, N_DEV, -1).transpose(1, 0, 2)
    return grad.reshape(N_DEV, -1, D_MODEL)


def _pack_small(g):
    flat = jnp.concatenate([g[n].reshape(-1) for n in SMALL] + [g["conv_w"].reshape(-1)])
    flat = jnp.concatenate([flat, jnp.zeros((SMALL_ROWS * D_MODEL - flat.shape[0],), F32)])
    return flat.reshape(SMALL_ROWS, D_MODEL)


def _unpack_small(buf, like):
    flat = buf.reshape(-1)
    out, off = {}, 0
    for n in SMALL:
        size = int(np.prod(like[n].shape))
        out[n] = flat[off:off + size].reshape(like[n].shape)
        off += size
    size = DEPTH * CONV_WIDTH * D_CONV
    out["conv_w"] = flat[off:off + size].reshape(DEPTH, CONV_WIDTH, D_CONV)
    return out


def kernel(x, ffn1_norm, ffn1_w_in, ffn1_w_out, mix_norm, mix_w_in, conv_w, conv_b, conv_ln_g, conv_ln_b, ret_norm_g, mix_w_out, ffn2_norm, ffn2_w_in, ffn2_w_out, final_norm, loss_target, m_ffn1_norm, m_ffn1_w_in, m_ffn1_w_out, m_mix_norm, m_mix_w_in, m_conv_w, m_conv_b, m_conv_ln_g, m_conv_ln_b, m_ret_norm_g, m_mix_w_out, m_ffn2_norm, m_ffn2_w_in, m_ffn2_w_out, m_final_norm, v_ffn1_norm, v_ffn1_w_in, v_ffn1_w_out, v_mix_norm, v_mix_w_in, v_conv_w, v_conv_b, v_conv_ln_g, v_conv_ln_b, v_ret_norm_g, v_mix_w_out, v_ffn2_norm, v_ffn2_w_in, v_ffn2_w_out, v_final_norm):
    args = locals()
    w = {n: args[n] for n in WEIGHTS}
    m = {n: args["m_" + n] for n in WEIGHTS}
    v = {n: args["v_" + n] for n in WEIGHTS}
    me = _my_id()
    x = x[0]
    target = loss_target[0]
    S = x.shape[0]
    cos, sin = _rope_tables(S)
    tables = _ret_tables()

    full = {}

    def gather(keys):
        return GATHER, [w["conv_w"] if n == "conv_w" else w[n][l].astype(BF16) for n, l in keys]

    def gathered(keys, got):
        for (n, l), g in zip(keys, got):
            full[(n, l)] = _natural(n, g)

    gathered(STAGE_A, communicate(*gather(STAGE_A)))

    saved = []
    for l in range(DEPTH):
        sv = {"x0": x}
        (x, sv["gate1"], sv["up1"]), got = ffn_fwd(x, _row(w["ffn1_norm"][l]), full[("ffn1_w_in", l)],
                                                   full[("ffn1_w_out", l)], gather(STAGE_B) if l == 0 else None)
        gathered(STAGE_B if l == 0 else [], got)
        sv["x1"] = x
        (sv["u"], sv["q_sb"], sv["k_sb"], sv["v_sb"], sv["qt_sb"], sv["q_r"], sv["k_r"], sv["v_r"],
         sv["g_r"]) = mix_in_fwd(x, _row(w["mix_norm"][l]), full[("mix_w_in", l)], cos, sin)
        cw = _pad_taps(full[("conv_w", None)][l])
        y_conv, sv["ypre"] = conv_fwd(sv["u"], cw, _row(w["conv_b"][l]), _row(w["conv_ln_g"][l]), _row(w["conv_ln_b"][l]))
        (o_sb, sv["tot"]), got = sb_fwd(sv["q_sb"], sv["k_sb"], sv["v_sb"], gather(STAGE_C) if l == 0 else None)
        gathered(STAGE_C if l == 0 else [], got)
        ng = w["ret_norm_g"][l].reshape(N_RET_HEADS, 1, HEAD_DIM)
        o_r, sv["y_r"], sv["states"] = ret_fwd(sv["q_r"], sv["k_r"], sv["v_r"], sv["g_r"], ng, tables)
        x, sv["ycat"] = mix_out_fwd(y_conv, o_sb, o_r, full[("mix_w_out", l)], x)
        sv["x2"] = x
        (x, sv["gate2"], sv["up2"]), _ = ffn_fwd(x, _row(w["ffn2_norm"][l]), full[("ffn2_w_in", l)],
                                                 full[("ffn2_w_out", l)])
        saved.append(sv)

    loss_acc, dx, dg_final = loss_head(x, _row(w["final_norm"]), target)
    loss = lax.psum(loss_acc[0, 0], ("x", "y", "c"))

    g = {"final_norm": dg_final.reshape(D_MODEL)}
    received = {}

    def exchange(keys, extra=(), dtype=F32):
        return EXCHANGE, [_by_device(n, g[(n, l)]).astype(dtype) for n, l in keys] + list(extra)

    def exchanged(keys, got):
        for key, p in zip(keys, got):
            received[key] = p

    def ffn_back(dx, x_in, gate, up, norm, names, l, comm=None):
        (dx, h, dyh, dgate, dup, hid, dg), got = ffn_bwd(dx, x_in, _row(norm), gate, up, full[(names[0], l)],
                                                         full[(names[1], l)], comm)
        g[(names[0], l)] = matmul_tn(h, [dgate, dup], D_MODEL, FF_TILE, name="ffn_dw_in")
        if [(names[0], l)] == STAGE_G:
            g[(names[1], l)], got_g = matmul_tn(hid, [dyh], FF_TILE, D_MODEL, name="ffn_dw_out",
                                                comm=exchange(STAGE_G, dtype=BF16))
            exchanged(STAGE_G, got_g)
        else:
            g[(names[1], l)] = matmul_tn(hid, [dyh], FF_TILE, D_MODEL, name="ffn_dw_out")
        return dx, dg.reshape(D_MODEL), got

    for l in reversed(range(DEPTH)):
        sv = saved[l]
        dx, g[("ffn2_norm", l)], _ = ffn_back(dx, sv["x2"], sv["gate2"], sv["up2"], w["ffn2_norm"][l], FFN2, l)
        dxb, dy_conv, do_sb, dot_sb, do_r = mix_out_bwd(dx, full[("mix_w_out", l)])
        g[("mix_w_out", l)] = matmul_tn(sv["ycat"], [dxb], D_MODEL, D_MODEL, name="mix_dw_out")
        cw = _pad_taps(full[("conv_w", None)][l])
        du_conv, dcw, dsm = conv_bwd(dy_conv, sv["ypre"], sv["u"], cw, _row(w["conv_ln_g"][l]), _row(w["conv_ln_b"][l]))
        g[("conv_w", l)] = dcw[:CONV_WIDTH]
        g[("conv_b", l)], g[("conv_ln_g", l)], g[("conv_ln_b", l)] = dsm[0], dsm[1], dsm[2]
        stage = STAGE_D if l == DEPTH - 1 else STAGE_E
        (dq_sb, dk_t, dv_t), got = sb_bwd(sv["q_sb"], sv["k_sb"], sv["v_sb"], do_sb, sv["qt_sb"], dot_sb, sv["tot"],
                                          exchange(stage))
        exchanged(stage, got)
        ng = w["ret_norm_g"][l].reshape(N_RET_HEADS, 1, HEAD_DIM)
        dq_r, dk_r, dv_r, dg_r, dng = ret_bwd(do_r, sv["q_r"], sv["k_r"], sv["v_r"], sv["g_r"], ng, sv["y_r"],
                                              sv["states"], tables)
        g[("ret_norm_g", l)] = dng.reshape(D_RET)
        dx, h, dproj, dg = mix_in_bwd(du_conv, dq_sb, dk_t, dv_t, dq_r, dk_r, dv_r, dg_r, cos, sin,
                                      full[("mix_w_in", l)], sv["x1"], _row(w["mix_norm"][l]), dx)
        g[("mix_norm", l)] = dg.reshape(D_MODEL)
        g[("mix_w_in", l)] = matmul_tn(h, [dproj], D_MODEL, D_MODEL, name="mix_dw_in")
        dx, g[("ffn1_norm", l)], got = ffn_back(dx, sv["x0"], sv["gate1"], sv["up1"], w["ffn1_norm"][l], FFN1, l,
                                                exchange(STAGE_F) if l == 0 else None)
        exchanged(STAGE_F if l == 0 else [], got)
    grad_x = dx

    small_names = [n for n in SMALL if n != "final_norm"] + ["conv_w"]
    gs = {n: jnp.stack([g[(n, l)] for l in range(DEPTH)], axis=0) for n in small_names}
    gs["final_norm"] = g["final_norm"]
    small = _pack_small(gs)
    got = communicate(*exchange(STAGE_H, [jnp.broadcast_to(small[None], (N_DEV, SMALL_ROWS, D_MODEL))], dtype=BF16))
    exchanged(STAGE_H, got[:-1])

    grad, delta, new_m, new_v = {}, {}, {}, {}
    for n in COL_SHARDED + ROW_SHARDED:
        rows = w[n].shape[1]
        grad[n], delta[n], new_m[n], new_v[n] = adamw([received[(n, l)] for l in range(DEPTH)], w[n], m[n], v[n],
                                                      tr=min(rows // 2, 256))

    def small_pack(d):
        mine = dict(d)
        cwf = jnp.zeros((DEPTH, CONV_WIDTH, D_CONV), F32)
        mine["conv_w"] = lax.dynamic_update_slice(cwf, d["conv_w"], (0, 0, me * (D_CONV // N_DEV)))
        return _pack_small(mine)

    outs = adamw([got[-1]], small_pack(w)[None], small_pack(m)[None], small_pack(v)[None], tr=SMALL_ROWS)
    for dst, o in zip((grad, delta, new_m, new_v), outs):
        un = _unpack_small(o[0], w)
        un["conv_w"] = lax.dynamic_slice(un["conv_w"], (0, 0, me * (D_CONV // N_DEV)),
                                         (DEPTH, CONV_WIDTH, D_CONV // N_DEV))
        dst.update(un)

    return (loss, grad_x[None], *[grad[n] for n in WEIGHTS], *[delta[n] for n in WEIGHTS],
            *[new_m[n] for n in WEIGHTS], *[new_v[n] for n in WEIGHTS])
```

```python
import functools

import numpy as np
import jax
import jax.numpy as jnp
from jax import lax
from jax.experimental import pallas as pl
from jax.experimental.pallas import tpu as pltpu

F32 = jnp.float32
BF16 = jnp.bfloat16

D_MODEL = 1024
DEPTH = 2
D_FF = 2816
D_CONV = 256
CONV_WIDTH = 31
CONV_HALO = 32
D_SB = 512
N_SB_HEADS = 8
D_RET = 256
N_RET_HEADS = 4
HEAD_DIM = 64
D_IN_PROJ = 3072
ROPE_BASE = 10000.0
EPS = 1e-6
N_DEV = 8

ADAM_LR = 0.001
ADAM_B1 = 0.9
ADAM_B2 = 0.999
ADAM_EPS = 1e-08
ADAM_WD = 0.01
ADAM_STEP = 10

VMEM_LIMIT = 56 * 1024 * 1024
ROW_TILE = 512
FF_TILE = 1408
SB_TILE = 256
SB_ROWS = 512
RET_TILE = 512
CONV_TILE = 256

NT_DIMS = (((1,), (1,)), ((), ()))
TN_DIMS = (((0,), (0,)), ((), ()))


def _params(n_axes, vmem=VMEM_LIMIT):
    return pltpu.CompilerParams(dimension_semantics=("arbitrary",) * n_axes, vmem_limit_bytes=vmem)


def _dot(a, b):
    return jnp.dot(a, b, preferred_element_type=F32)


def _dot_nt(a, b):
    return lax.dot_general(a, b, NT_DIMS, preferred_element_type=F32)


def _dot_tn(a, b):
    return lax.dot_general(a, b, TN_DIMS, preferred_element_type=F32)


def _sigmoid(z):
    return 1.0 / (1.0 + jnp.exp(-z))


def _rms_stats(xv):
    r = lax.rsqrt(jnp.mean(xv * xv, axis=-1, keepdims=True) + EPS)
    return r, xv * r


def _rms_bwd(xv, g, dh):
    r, xhat = _rms_stats(xv)
    dxhat = dh * g
    dx = r * (dxhat - xhat * jnp.mean(dxhat * xhat, axis=-1, keepdims=True))
    dg = jnp.sum(dh * xhat, axis=0, keepdims=True)
    return dx, (xhat * g).astype(BF16), dg


def ffn_fwd(x, g, w_in, w_out, comm=None, tm=ROW_TILE):
    S = x.shape[0]
    nj = D_FF // FF_TILE

    def body(x_ref, g_ref, wg_ref, wu_ref, wo_ref, y_ref, gate_ref, up_ref, h_sc, acc_sc):
        j = pl.program_id(1)

        @pl.when(j == 0)
        def _():
            _, xhat = _rms_stats(x_ref[...])
            h_sc[...] = (xhat * g_ref[...]).astype(BF16)
            acc_sc[...] = jnp.zeros_like(acc_sc)

        h = h_sc[...]
        gt = _dot_nt(h, wg_ref[...])
        up = _dot_nt(h, wu_ref[...])
        gate_ref[...] = gt.astype(BF16)
        up_ref[...] = up.astype(BF16)
        hid = (gt * _sigmoid(gt) * up).astype(BF16)
        acc_sc[...] += _dot(hid, wo_ref[...])

        @pl.when(j == nj - 1)
        def _():
            y_ref[...] = x_ref[...] + 0.5 * acc_sc[...]

    return _call(
        body, (x, g, w_in, w_in, w_out), comm, name="ffn_fwd",
        grid=(S // tm, nj),
        in_specs=[
            pl.BlockSpec((tm, D_MODEL), lambda i, j: (i, 0)),
            pl.BlockSpec((1, D_MODEL), lambda i, j: (0, 0)),
            pl.BlockSpec((FF_TILE, D_MODEL), lambda i, j: (j, 0)),
            pl.BlockSpec((FF_TILE, D_MODEL), lambda i, j: (j + nj, 0)),
            pl.BlockSpec((FF_TILE, D_MODEL), lambda i, j: (j, 0)),
        ],
        out_specs=[
            pl.BlockSpec((tm, D_MODEL), lambda i, j: (i, 0)),
            pl.BlockSpec((tm, FF_TILE), lambda i, j: (i, j)),
            pl.BlockSpec((tm, FF_TILE), lambda i, j: (i, j)),
        ],
        out_shape=[
            jax.ShapeDtypeStruct((S, D_MODEL), F32),
            jax.ShapeDtypeStruct((S, D_FF), BF16),
            jax.ShapeDtypeStruct((S, D_FF), BF16),
        ],
        scratch_shapes=[pltpu.VMEM((tm, D_MODEL), BF16), pltpu.VMEM((tm, D_MODEL), F32)],
    )


def ffn_bwd(dy, x, g, gate, up, w_in, w_out, comm=None, tm=ROW_TILE // 2):
    S = x.shape[0]
    nj = D_FF // FF_TILE

    def body(dy_ref, x_ref, g_ref, gate_ref, up_ref, w_ref, wo_ref,
             dx_ref, h_ref, dyh_ref, dgate_ref, dup_ref, hid_ref, dg_ref):
        i = pl.program_id(0)
        d2 = (0.5 * dy_ref[...]).astype(BF16)
        dyh_ref[...] = d2
        dh = None
        for j in range(nj):
            cols = pl.ds(j * FF_TILE, FF_TILE)
            dhid = _dot_nt(d2, wo_ref[cols, :])
            gt = gate_ref[:, cols].astype(F32)
            u = up_ref[:, cols].astype(F32)
            sig = _sigmoid(gt)
            sl = gt * sig
            dgate = (dhid * u * (sig * (1.0 + gt * (1.0 - sig)))).astype(BF16)
            dup = (dhid * sl).astype(BF16)
            dgate_ref[:, cols] = dgate
            dup_ref[:, cols] = dup
            hid_ref[:, cols] = (sl * u).astype(BF16)
            part = _dot(dgate, w_ref[cols, :]) + _dot(dup, w_ref[pl.ds(D_FF + j * FF_TILE, FF_TILE), :])
            dh = part if dh is None else dh + part
        dx, h, dg = _rms_bwd(x_ref[...], g_ref[...], dh)
        dx_ref[...] = dy_ref[...] + dx
        h_ref[...] = h

        @pl.when(i == 0)
        def _():
            dg_ref[...] = dg

        @pl.when(i > 0)
        def _():
            dg_ref[...] += dg

    row = lambda i: (i, 0)
    one = lambda i: (0, 0)
    resident = pl.Buffered(1)
    return _call(
        body, (dy, x, g, gate, up, w_in, w_out), comm, name="ffn_bwd",
        grid=(S // tm,),
        in_specs=[
            pl.BlockSpec((tm, D_MODEL), row),
            pl.BlockSpec((tm, D_MODEL), row),
            pl.BlockSpec((1, D_MODEL), one),
            pl.BlockSpec((tm, D_FF), row),
            pl.BlockSpec((tm, D_FF), row),
            pl.BlockSpec((2 * D_FF, D_MODEL), one, pipeline_mode=resident),
            pl.BlockSpec((D_FF, D_MODEL), one, pipeline_mode=resident),
        ],
        out_specs=[
            pl.BlockSpec((tm, D_MODEL), row),
            pl.BlockSpec((tm, D_MODEL), row),
            pl.BlockSpec((tm, D_MODEL), row),
            pl.BlockSpec((tm, D_FF), row),
            pl.BlockSpec((tm, D_FF), row),
            pl.BlockSpec((tm, D_FF), row),
            pl.BlockSpec((1, D_MODEL), one),
        ],
        out_shape=[
            jax.ShapeDtypeStruct((S, D_MODEL), F32),
            jax.ShapeDtypeStruct((S, D_MODEL), BF16),
            jax.ShapeDtypeStruct((S, D_MODEL), BF16),
            jax.ShapeDtypeStruct((S, D_FF), BF16),
            jax.ShapeDtypeStruct((S, D_FF), BF16),
            jax.ShapeDtypeStruct((S, D_FF), BF16),
            jax.ShapeDtypeStruct((1, D_MODEL), F32),
        ],
        scratch_shapes=[],
    )


def matmul_tn(a_list, b, ta, tn, tk=ROW_TILE, name="matmul_tn", comm=None):
    S, ka = a_list[0].shape
    nb = b.shape[1]
    per = ka // ta

    def body(*refs):
        a_refs, b_ref, o_ref = refs[:-2], refs[-2], refs[-1]
        i = pl.program_id(0)
        k = pl.program_id(2)

        @pl.when(k == 0)
        def _():
            o_ref[...] = jnp.zeros_like(o_ref)

        for t, a_ref in enumerate(a_refs):
            @pl.when(lax.div(i, per) == t)
            def _(a_ref=a_ref):
                o_ref[...] += _dot_tn(a_ref[...], b_ref[...])

    def a_spec(t):
        def index(i, j, k):
            mine = lax.div(i, per) == t
            return jnp.where(mine, k, 0), jnp.where(mine, i - t * per, 0)
        return pl.BlockSpec((tk, ta), index)

    (out,), got = _call(
        body, (*a_list, b), comm, name=name,
        grid=(per * len(a_list), nb // tn, S // tk),
        in_specs=[a_spec(t) for t in range(len(a_list))] + [pl.BlockSpec((tk, tn), lambda i, j, k: (k, j))],
        out_specs=[pl.BlockSpec((ta, tn), lambda i, j, k: (i, j))],
        out_shape=[jax.ShapeDtypeStruct((ka * len(a_list), nb), F32)],
        scratch_shapes=[],
    )
    return (out, got) if comm is not None else out


SB_COLS = (2 * D_CONV, 2 * D_CONV + D_SB, 2 * D_CONV + 2 * D_SB)
RET_COLS = tuple(2 * D_CONV + 3 * D_SB + j * D_RET for j in range(4))


def _swap_halves(x):
    n = x.shape[1]
    lane = lax.broadcasted_iota(jnp.int32, x.shape, 1)
    first = (lane % HEAD_DIM) < (HEAD_DIM // 2)
    return jnp.where(first, pltpu.roll(x, n - HEAD_DIM // 2, 1), pltpu.roll(x, HEAD_DIM // 2, 1))


def _head(x, h):
    return x[:, h * HEAD_DIM:(h + 1) * HEAD_DIM]


def _heads_spec(n_heads, tm):
    return pl.BlockSpec((n_heads, tm, HEAD_DIM), lambda i: (0, i, 0))


def mix_in_fwd(x, g, w, cos, sin, tm=ROW_TILE):
    S = x.shape[0]

    def body(x_ref, g_ref, w_ref, c_ref, s_ref, u_ref, q_ref, k_ref, v_ref, qt_ref, qr_ref, kr_ref, vr_ref, gr_ref):
        _, xhat = _rms_stats(x_ref[...])
        proj = _dot_nt((xhat * g_ref[...]).astype(BF16), w_ref[...])
        u_ref[...] = proj[:, :2 * D_CONV]
        for h in range(N_SB_HEADS):
            q = (_head(proj[:, SB_COLS[0]:SB_COLS[1]], h) * 0.125).astype(BF16)
            q_ref[h] = q
            qt_ref[h] = q.T
            k_ref[h] = _head(proj[:, SB_COLS[1]:SB_COLS[2]], h).astype(BF16)
            v_ref[h] = _head(proj[:, SB_COLS[2]:RET_COLS[0]], h).astype(BF16)
        c = c_ref[...]
        s = s_ref[...]
        qv = proj[:, RET_COLS[0]:RET_COLS[1]]
        kv = proj[:, RET_COLS[1]:RET_COLS[2]]
        q_rot = ((qv * c + _swap_halves(qv) * s) * 0.125).astype(BF16)
        k_rot = (kv * c + _swap_halves(kv) * s).astype(BF16)
        for h in range(N_RET_HEADS):
            qr_ref[h] = _head(q_rot, h)
            kr_ref[h] = _head(k_rot, h)
            vr_ref[h] = _head(proj[:, RET_COLS[2]:RET_COLS[3]], h).astype(BF16)
            gr_ref[h] = _head(proj[:, RET_COLS[3]:], h)

    row = lambda i: (i, 0)
    one = lambda i: (0, 0)
    sb = jax.ShapeDtypeStruct((N_SB_HEADS, S, HEAD_DIM), BF16)
    ret = jax.ShapeDtypeStruct((N_RET_HEADS, S, HEAD_DIM), BF16)
    return pl.pallas_call(
        body, name="mix_in_fwd",
        grid=(S // tm,),
        in_specs=[
            pl.BlockSpec((tm, D_MODEL), row),
            pl.BlockSpec((1, D_MODEL), one),
            pl.BlockSpec((D_IN_PROJ, D_MODEL), one, pipeline_mode=pl.Buffered(1)),
            pl.BlockSpec((tm, D_RET), row),
            pl.BlockSpec((tm, D_RET), row),
        ],
        out_specs=[
            pl.BlockSpec((tm, 2 * D_CONV), row),
            _heads_spec(N_SB_HEADS, tm), _heads_spec(N_SB_HEADS, tm), _heads_spec(N_SB_HEADS, tm),
            pl.BlockSpec((N_SB_HEADS, HEAD_DIM, tm), lambda i: (0, 0, i)),
            _heads_spec(N_RET_HEADS, tm), _heads_spec(N_RET_HEADS, tm), _heads_spec(N_RET_HEADS, tm),
            _heads_spec(N_RET_HEADS, tm),
        ],
        out_shape=[
            jax.ShapeDtypeStruct((S, 2 * D_CONV), F32), sb, sb, sb,
            jax.ShapeDtypeStruct((N_SB_HEADS, HEAD_DIM, S), BF16),
            ret, ret, ret, jax.ShapeDtypeStruct((N_RET_HEADS, S, HEAD_DIM), F32),
        ],
        compiler_params=_params(1),
    )(x, g, w, cos, sin)


def mix_in_bwd(du, dq, dkt, dvt, dqr, dkr, dvr, dgr, cos, sin, w, x, g, dy, tm=SB_TILE):
    S = x.shape[0]
    assert dkt.shape[-1] == tm

    def body(du_ref, dq_ref, dkt_ref, dvt_ref, dqr_ref, dkr_ref, dvr_ref, dgr_ref, c_ref, s_ref, w_ref, x_ref, g_ref,
             dy_ref, dx_ref, h_ref, dp_ref, dg_ref):
        i = pl.program_id(0)
        sb_heads = range(N_SB_HEADS)
        ret_heads = range(N_RET_HEADS)
        c = c_ref[...]
        s = s_ref[...]
        dq_rot = jnp.concatenate([dqr_ref[h] for h in ret_heads], axis=1) * 0.125
        dk_rot = jnp.concatenate([dkr_ref[h] for h in ret_heads], axis=1)
        dproj = jnp.concatenate([
            du_ref[...].astype(BF16),
            jnp.concatenate([dq_ref[h] * 0.125 for h in sb_heads], axis=1).astype(BF16),
            jnp.concatenate([dkt_ref[h, 0].T for h in sb_heads], axis=1).astype(BF16),
            jnp.concatenate([dvt_ref[h, 0].T for h in sb_heads], axis=1).astype(BF16),
            (dq_rot * c - _swap_halves(dq_rot) * s).astype(BF16),
            (dk_rot * c - _swap_halves(dk_rot) * s).astype(BF16),
            jnp.concatenate([dvr_ref[h] for h in ret_heads], axis=1).astype(BF16),
            jnp.concatenate([dgr_ref[h] for h in ret_heads], axis=1).astype(BF16)], axis=1)
        dp_ref[...] = dproj
        dh = _dot(dproj, w_ref[...])
        dx, h, dg = _rms_bwd(x_ref[...], g_ref[...], dh)
        dx_ref[...] = dy_ref[...] + dx
        h_ref[...] = h

        @pl.when(i == 0)
        def _():
            dg_ref[...] = dg

        @pl.when(i > 0)
        def _():
            dg_ref[...] += dg

    row = lambda i: (i, 0)
    one = lambda i: (0, 0)
    tiles = pl.BlockSpec((N_SB_HEADS, 1, HEAD_DIM, tm), lambda i: (0, i, 0, 0))
    return pl.pallas_call(
        body, name="mix_in_bwd",
        grid=(S // tm,),
        in_specs=[
            pl.BlockSpec((tm, 2 * D_CONV), row),
            _heads_spec(N_SB_HEADS, tm), tiles, tiles,
            _heads_spec(N_RET_HEADS, tm), _heads_spec(N_RET_HEADS, tm), _heads_spec(N_RET_HEADS, tm),
            _heads_spec(N_RET_HEADS, tm),
            pl.BlockSpec((tm, D_RET), row),
            pl.BlockSpec((tm, D_RET), row),
            pl.BlockSpec((D_IN_PROJ, D_MODEL), one, pipeline_mode=pl.Buffered(1)),
            pl.BlockSpec((tm, D_MODEL), row),
            pl.BlockSpec((1, D_MODEL), one),
            pl.BlockSpec((tm, D_MODEL), row),
        ],
        out_specs=[
            pl.BlockSpec((tm, D_MODEL), row),
            pl.BlockSpec((tm, D_MODEL), row),
            pl.BlockSpec((tm, D_IN_PROJ), row),
            pl.BlockSpec((1, D_MODEL), one),
        ],
        out_shape=[
            jax.ShapeDtypeStruct((S, D_MODEL), F32),
            jax.ShapeDtypeStruct((S, D_MODEL), BF16),
            jax.ShapeDtypeStruct((S, D_IN_PROJ), BF16),
            jax.ShapeDtypeStruct((1, D_MODEL), F32),
        ],
        compiler_params=_params(1),
    )(du, dq, dkt, dvt, dqr, dkr, dvr, dgr, cos, sin, w, x, g, dy)


def mix_out_fwd(y_conv, o_sb, o_ret, w, x, tm=ROW_TILE):
    S = x.shape[0]

    def body(yc_ref, sb_ref, rt_ref, w_ref, x_ref, o_ref, ycat_ref):
        ycat = jnp.concatenate(
            [yc_ref[...]] + [sb_ref[h].astype(BF16) for h in range(N_SB_HEADS)]
            + [rt_ref[h].astype(BF16) for h in range(N_RET_HEADS)], axis=1)
        ycat_ref[...] = ycat
        o_ref[...] = x_ref[...] + _dot(ycat, w_ref[...])

    row = lambda i: (i, 0)
    return pl.pallas_call(
        body, name="mix_out_fwd",
        grid=(S // tm,),
        in_specs=[
            pl.BlockSpec((tm, D_CONV), row),
            _heads_spec(N_SB_HEADS, tm),
            _heads_spec(N_RET_HEADS, tm),
            pl.BlockSpec((D_MODEL, D_MODEL), lambda i: (0, 0)),
            pl.BlockSpec((tm, D_MODEL), row),
        ],
        out_specs=[pl.BlockSpec((tm, D_MODEL), row), pl.BlockSpec((tm, D_MODEL), row)],
        out_shape=[jax.ShapeDtypeStruct((S, D_MODEL), F32), jax.ShapeDtypeStruct((S, D_MODEL), BF16)],
        compiler_params=_params(1),
    )(y_conv, o_sb, o_ret, w, x)


def mix_out_bwd(dy, w, tm=ROW_TILE):
    S = dy.shape[0]

    def body(dy_ref, w_ref, dyb_ref, dc_ref, do_ref, dot_ref, dr_ref):
        d = dy_ref[...].astype(BF16)
        dyb_ref[...] = d
        dycat = _dot_nt(d, w_ref[...])
        dc_ref[...] = dycat[:, :D_CONV]
        for h in range(N_SB_HEADS):
            do = _head(dycat[:, D_CONV:D_CONV + D_SB], h).astype(BF16)
            do_ref[h] = do
            dot_ref[h] = do.T
        for h in range(N_RET_HEADS):
            dr_ref[h] = _head(dycat[:, D_CONV + D_SB:], h)

    row = lambda i: (i, 0)
    return pl.pallas_call(
        body, name="mix_out_bwd",
        grid=(S // tm,),
        in_specs=[
            pl.BlockSpec((tm, D_MODEL), row),
            pl.BlockSpec((D_MODEL, D_MODEL), lambda i: (0, 0)),
        ],
        out_specs=[
            pl.BlockSpec((tm, D_MODEL), row),
            pl.BlockSpec((tm, D_CONV), row),
            _heads_spec(N_SB_HEADS, tm),
            pl.BlockSpec((N_SB_HEADS, HEAD_DIM, tm), lambda i: (0, 0, i)),
            _heads_spec(N_RET_HEADS, tm),
        ],
        out_shape=[
            jax.ShapeDtypeStruct((S, D_MODEL), BF16),
            jax.ShapeDtypeStruct((S, D_CONV), F32),
            jax.ShapeDtypeStruct((N_SB_HEADS, S, HEAD_DIM), BF16),
            jax.ShapeDtypeStruct((N_SB_HEADS, HEAD_DIM, S), BF16),
            jax.ShapeDtypeStruct((N_RET_HEADS, S, HEAD_DIM), F32),
        ],
        compiler_params=_params(1),
    )(dy, w)


def _conv_ln(ypre, ln_g, ln_b):
    mu = jnp.mean(ypre, axis=-1, keepdims=True)
    yc = ypre - mu
    rstd = lax.rsqrt(jnp.mean(yc * yc, axis=-1, keepdims=True) + EPS)
    yn = yc * rstd
    return yn, rstd, yn * ln_g + ln_b


def conv_fwd(proj, cw, cb, ln_g, ln_b, tm=CONV_TILE):
    S = proj.shape[0]
    hb = tm // CONV_HALO

    def body(a_ref, b_ref, ap_ref, bp_ref, cw_ref, cb_ref, g_ref, bb_ref, y_ref, ypre_ref, v_sc):
        i = pl.program_id(0)
        prev = ap_ref[...] * _sigmoid(bp_ref[...])
        v_sc[pl.ds(0, CONV_HALO), :] = jnp.where(i > 0, prev, 0.0)
        v_sc[pl.ds(CONV_HALO, tm), :] = a_ref[...] * _sigmoid(b_ref[...])
        acc = jnp.zeros((tm, D_CONV), F32)
        for j in range(CONV_WIDTH):
            acc = acc + cw_ref[pl.ds(j, 1), :] * v_sc[pl.ds(CONV_HALO - (CONV_WIDTH - 1) + j, tm), :]
        ypre = acc + cb_ref[...]
        ypre_ref[...] = ypre
        _, _, z = _conv_ln(ypre, g_ref[...], bb_ref[...])
        y_ref[...] = (z * _sigmoid(z)).astype(BF16)

    one = lambda i: (0, 0)
    return pl.pallas_call(
        body, name="conv_fwd",
        grid=(S // tm,),
        in_specs=[
            pl.BlockSpec((tm, D_CONV), lambda i: (i, 0)),
            pl.BlockSpec((tm, D_CONV), lambda i: (i, 1)),
            pl.BlockSpec((CONV_HALO, D_CONV), lambda i: (jnp.maximum(i * hb - 1, 0), 0)),
            pl.BlockSpec((CONV_HALO, D_CONV), lambda i: (jnp.maximum(i * hb - 1, 0), 1)),
            pl.BlockSpec((CONV_HALO, D_CONV), one),
            pl.BlockSpec((1, D_CONV), one),
            pl.BlockSpec((1, D_CONV), one),
            pl.BlockSpec((1, D_CONV), one),
        ],
        out_specs=[pl.BlockSpec((tm, D_CONV), lambda i: (i, 0)), pl.BlockSpec((tm, D_CONV), lambda i: (i, 0))],
        out_shape=[jax.ShapeDtypeStruct((S, D_CONV), BF16), jax.ShapeDtypeStruct((S, D_CONV), F32)],
        scratch_shapes=[pltpu.VMEM((tm + CONV_HALO, D_CONV), F32)],
        compiler_params=_params(1),
    )(proj, proj, proj, proj, cw, cb, ln_g, ln_b)


def conv_bwd(dyc, ypre, proj, cw, ln_g, ln_b, tm=CONV_TILE):
    S = ypre.shape[0]
    hb = tm // CONV_HALO
    nblk = S // tm
    last_halo = S // CONV_HALO - 1

    def dpre(dy, yp, g, bb):
        yn, rstd, z = _conv_ln(yp, g, bb)
        sg = _sigmoid(z)
        dz = dy * (sg * (1.0 + z * (1.0 - sg)))
        dyn = dz * g
        d = rstd * (dyn - jnp.mean(dyn, axis=-1, keepdims=True) - yn * jnp.mean(dyn * yn, axis=-1, keepdims=True))
        return d, dz * yn, dz

    def body(dy_ref, yp_ref, dyn_ref, ypn_ref, a_ref, b_ref, ap_ref, bp_ref, cw_ref, g_ref, bb_ref,
             du_ref, dcw_ref, dsm_ref, d_sc, v_sc):
        i = pl.program_id(0)
        g = g_ref[...]
        bb = bb_ref[...]
        d_main, dgn, dz = dpre(dy_ref[...], yp_ref[...], g, bb)
        d_next, _, _ = dpre(dyn_ref[...], ypn_ref[...], g, bb)
        d_sc[pl.ds(0, tm), :] = d_main
        d_sc[pl.ds(tm, CONV_HALO), :] = jnp.where(i < nblk - 1, d_next, 0.0)
        a = a_ref[...]
        sb = _sigmoid(b_ref[...])
        prev = ap_ref[...] * _sigmoid(bp_ref[...])
        v_sc[pl.ds(0, CONV_HALO), :] = jnp.where(i > 0, prev, 0.0)
        v_sc[pl.ds(CONV_HALO, tm), :] = a * sb

        @pl.when(i == 0)
        def _():
            dcw_ref[...] = jnp.zeros_like(dcw_ref)
            dsm_ref[...] = jnp.zeros_like(dsm_ref)

        dv = jnp.zeros((tm, D_CONV), F32)
        for j in range(CONV_WIDTH):
            dv = dv + cw_ref[pl.ds(j, 1), :] * d_sc[pl.ds(CONV_WIDTH - 1 - j, tm), :]
            shifted = v_sc[pl.ds(CONV_HALO - (CONV_WIDTH - 1) + j, tm), :]
            dcw_ref[pl.ds(j, 1), :] += jnp.sum(d_main * shifted, axis=0, keepdims=True)
        du_ref[:, pl.ds(0, D_CONV)] = dv * sb
        du_ref[:, pl.ds(D_CONV, D_CONV)] = dv * a * sb * (1.0 - sb)
        dsm_ref[pl.ds(0, 1), :] += jnp.sum(d_main, axis=0, keepdims=True)
        dsm_ref[pl.ds(1, 1), :] += jnp.sum(dgn, axis=0, keepdims=True)
        dsm_ref[pl.ds(2, 1), :] += jnp.sum(dz, axis=0, keepdims=True)

    one = lambda i: (0, 0)
    prev_map = lambda c: (lambda i: (jnp.maximum(i * hb - 1, 0), c))
    next_map = lambda i: (jnp.minimum((i + 1) * hb, last_halo), 0)
    return pl.pallas_call(
        body, name="conv_bwd",
        grid=(nblk,),
        in_specs=[
            pl.BlockSpec((tm, D_CONV), lambda i: (i, 0)),
            pl.BlockSpec((tm, D_CONV), lambda i: (i, 0)),
            pl.BlockSpec((CONV_HALO, D_CONV), next_map),
            pl.BlockSpec((CONV_HALO, D_CONV), next_map),
            pl.BlockSpec((tm, D_CONV), lambda i: (i, 0)),
            pl.BlockSpec((tm, D_CONV), lambda i: (i, 1)),
            pl.BlockSpec((CONV_HALO, D_CONV), prev_map(0)),
            pl.BlockSpec((CONV_HALO, D_CONV), prev_map(1)),
            pl.BlockSpec((CONV_HALO, D_CONV), one),
            pl.BlockSpec((1, D_CONV), one),
            pl.BlockSpec((1, D_CONV), one),
        ],
        out_specs=[
            pl.BlockSpec((tm, 2 * D_CONV), lambda i: (i, 0)),
            pl.BlockSpec((CONV_HALO, D_CONV), one),
            pl.BlockSpec((8, D_CONV), one),
        ],
        out_shape=[
            jax.ShapeDtypeStruct((S, 2 * D_CONV), F32),
            jax.ShapeDtypeStruct((CONV_HALO, D_CONV), F32),
            jax.ShapeDtypeStruct((8, D_CONV), F32),
        ],
        scratch_shapes=[pltpu.VMEM((tm + CONV_HALO, D_CONV), F32), pltpu.VMEM((tm + CONV_HALO, D_CONV), F32)],
        compiler_params=_params(1),
    )(dyc, ypre, dyc, ypre, proj, proj, proj, proj, cw, ln_g, ln_b)


SB_GROUP = 8


def _softplus(z):
    neg_abs = lax.bitcast_convert_type(lax.bitcast_convert_type(z, jnp.uint32) | jnp.uint32(0x80000000), F32)
    return jnp.maximum(z, 0.0) + jnp.log(1.0 + jnp.exp(neg_abs))


def _full_groups(n, body):
    def step(t, c):
        body(t * SB_GROUP)
        return c

    lax.fori_loop(0, lax.div(n, SB_GROUP), step, 0)


def _last_group(n, step, body):
    r = lax.rem(n, SB_GROUP)
    for k in range(0, SB_GROUP, step):
        @pl.when(r == k)
        def _(k=k):
            body(k)


def _rows(xs):
    return xs[0] if len(xs) == 1 else jnp.concatenate(xs, axis=0)


def sb_fwd(q, k, v, comm=None, T=SB_TILE, Q=SB_ROWS):
    H, S, dh = q.shape
    M = Q // T

    def body(q_ref, k_ref, v_ref, o_ref, tot_ref, acc_sc, car_sc):
        qb = pl.program_id(1)
        qv = q_ref[...]
        row = lax.broadcasted_iota(jnp.int32, (T, T), 0)
        col = lax.broadcasted_iota(jnp.int32, (T, T), 1)
        tri = jnp.where(row >= col, 1.0, 0.0).astype(BF16)
        qrow = lax.broadcasted_iota(jnp.int32, (Q, T), 0)
        kcol = lax.broadcasted_iota(jnp.int32, (Q, T), 1)
        causal = {d + 1: kcol + d * T < qrow for d in range(M)}
        acc_sc[...] = jnp.zeros_like(acc_sc)
        car_sc[...] = jnp.zeros_like(car_sc)

        def logits(kb, masked):
            ks = k_ref[pl.ds(pl.multiple_of(kb * T, T), T), :]
            z = _dot_nt(qv, ks)
            nb = _softplus(z)
            if masked:
                nb = jnp.where(causal[masked], nb, 0.0)
            return z, nb.astype(BF16)

        def group(kbs, diag):
            parts = [logits(kb, d) for kb, d in zip(kbs, diag)]
            pall = _dot(_rows([nb for _, nb in parts]), tri)
            carry = car_sc[...]
            out = None
            for j, kb in enumerate(kbs):
                p = pall[j * Q:(j + 1) * Q]
                vs = v_ref[pl.ds(pl.multiple_of(kb * T, T), T), :]
                w = jnp.exp((parts[j][0] - carry) - p)
                if diag[j]:
                    w = jnp.where(causal[diag[j]], w, 0.0)
                o = _dot(w.astype(BF16), vs)
                out = o if out is None else out + o
                carry = carry + p[:, 0:1]
            acc_sc[...] += out
            car_sc[...] = carry

        full = M * qb
        _last_group(full, M, lambda r: group([full + d for d in reversed(range(M))] + [full - 1 - o for o in range(r)],
                                             [d + 1 for d in reversed(range(M))] + [0] * r))
        rest = full - lax.rem(full, SB_GROUP)
        _full_groups(rest, lambda o: group([rest - 1 - o - j for j in range(SB_GROUP)], [0] * SB_GROUP))
        o_ref[...] = acc_sc[...]
        tot_ref[...] = car_sc[...]

    return _call(
        body, (q, k, v), comm, name="sb_fwd",
        grid=(H, S // Q),
        in_specs=[
            pl.BlockSpec((None, Q, dh), lambda h, i: (h, i, 0)),
            pl.BlockSpec((None, S, dh), lambda h, i: (h, 0, 0)),
            pl.BlockSpec((None, S, dh), lambda h, i: (h, 0, 0)),
        ],
        out_specs=[
            pl.BlockSpec((None, Q, dh), lambda h, i: (h, i, 0)),
            pl.BlockSpec((None, Q, 1), lambda h, i: (h, i, 0)),
        ],
        out_shape=[jax.ShapeDtypeStruct((H, S, dh), F32), jax.ShapeDtypeStruct((H, S, 1), F32)],
        scratch_shapes=[pltpu.VMEM((Q, dh), F32), pltpu.VMEM((Q, 1), F32)],
    )


def sb_bwd(q, k, v, do, qt, dot, tot, comm=None, T=SB_TILE, Q=SB_ROWS):
    H, S, dh = q.shape
    nt = S // T
    M = Q // T

    def body(q_ref, k_ref, v_ref, do_ref, qt_ref, dot_ref, tot_ref, dq_ref, dk_ref, dv_ref, acc_sc, rc_sc, gc_sc):
        qb = pl.program_id(1)
        qv = q_ref[...]
        dov = do_ref[...]
        qtv = qt_ref[...]
        dotv = dot_ref[...]
        row = lax.broadcasted_iota(jnp.int32, (T, T), 0)
        col = lax.broadcasted_iota(jnp.int32, (T, T), 1)
        before = jnp.where(row < col, 1.0, 0.0).astype(BF16)
        qrow = lax.broadcasted_iota(jnp.int32, (Q, T), 0)
        kcol = lax.broadcasted_iota(jnp.int32, (Q, T), 1)
        causal = {d + 1: kcol + d * T < qrow for d in range(M)}
        acc_sc[...] = jnp.zeros_like(acc_sc)
        rc_sc[...] = tot_ref[...]
        gc_sc[...] = jnp.zeros_like(gc_sc)

        @pl.when(qb == 0)
        def _():
            dk_ref[...] = jnp.zeros_like(dk_ref)
            dv_ref[...] = jnp.zeros_like(dv_ref)

        def first(kb, masked):
            start = pl.multiple_of(kb * T, T)
            z = _dot_nt(qv, k_ref[pl.ds(start, T), :])
            nb = _softplus(z)
            sig = jnp.exp(z - nb)
            if masked:
                nb = jnp.where(causal[masked], nb, 0.0)
            dw = _dot_nt(dov, v_ref[pl.ds(start, T), :])
            return z, sig, nb.astype(BF16), dw

        def group(kbs, diag):
            parts = [first(kb, d) for kb, d in zip(kbs, diag)]
            pall = _dot(_rows([p[2] for p in parts]), before)
            rc = rc_sc[...]
            ws, gs, ghs = [], [], []
            for j in range(len(kbs)):
                z, _, nbh, dw = parts[j]
                p = pall[j * Q:(j + 1) * Q]
                w = jnp.exp((z - rc) + p)
                rc = rc - (p[:, T - 1:T] + nbh[:, T - 1:T].astype(F32))
                if diag[j]:
                    w = jnp.where(causal[diag[j]], w, 0.0)
                g = dw * w
                ws.append(w.astype(BF16))
                gs.append(g)
                ghs.append(g.astype(BF16))
            glall = _dot(_rows(ghs), before)
            gc = gc_sc[...]
            dq = None
            for j, kb in enumerate(kbs):
                ks = k_ref[pl.ds(pl.multiple_of(kb * T, T), T), :]
                gl = glall[j * Q:(j + 1) * Q]
                dz = gs[j] - parts[j][1] * (gs[j] + (gl + gc))
                gc = gc + gl[:, T - 1:T] + ghs[j][:, T - 1:T].astype(F32)
                if diag[j]:
                    dz = jnp.where(causal[diag[j]], dz, 0.0)
                dzb = dz.astype(BF16)
                d = _dot(dzb, ks)
                dq = d if dq is None else dq + d
                dk_ref[kb] += _dot(qtv, dzb)
                dv_ref[kb] += _dot(dotv, ws[j])
            acc_sc[...] += dq
            rc_sc[...] = rc
            gc_sc[...] = gc

        full = M * qb
        _full_groups(full, lambda o: group([o + j for j in range(SB_GROUP)], [0] * SB_GROUP))
        rest = full - lax.rem(full, SB_GROUP)
        _last_group(full, M, lambda r: group([rest + j for j in range(r)] + [full + d for d in range(M)],
                                             [0] * r + [d + 1 for d in range(M)]))
        dq_ref[...] = acc_sc[...]

    return _call(
        body, (q, k, v, do, qt, dot, tot), comm, name="sb_bwd",
        grid=(H, S // Q),
        in_specs=[
            pl.BlockSpec((None, Q, dh), lambda h, i: (h, i, 0)),
            pl.BlockSpec((None, S, dh), lambda h, i: (h, 0, 0)),
            pl.BlockSpec((None, S, dh), lambda h, i: (h, 0, 0)),
            pl.BlockSpec((None, Q, dh), lambda h, i: (h, i, 0)),
            pl.BlockSpec((None, dh, Q), lambda h, i: (h, 0, i)),
            pl.BlockSpec((None, dh, Q), lambda h, i: (h, 0, i)),
            pl.BlockSpec((None, Q, 1), lambda h, i: (h, i, 0)),
        ],
        out_specs=[
            pl.BlockSpec((None, Q, dh), lambda h, i: (h, i, 0)),
            pl.BlockSpec((None, nt, dh, T), lambda h, i: (h, 0, 0, 0)),
            pl.BlockSpec((None, nt, dh, T), lambda h, i: (h, 0, 0, 0)),
        ],
        out_shape=[jax.ShapeDtypeStruct((H, S, dh), F32), jax.ShapeDtypeStruct((H, nt, dh, T), F32),
                   jax.ShapeDtypeStruct((H, nt, dh, T), F32)],
        scratch_shapes=[pltpu.VMEM((Q, dh), F32), pltpu.VMEM((Q, 1), F32), pltpu.VMEM((Q, 1), F32)],
    )


def _ret_tables(T=RET_TILE):
    hh = jnp.arange(N_RET_HEADS, dtype=F32)
    log_gamma = jnp.log1p(-jnp.exp2(-5.0 - hh))
    idx = jnp.arange(T, dtype=F32)
    diff = idx[:, None] - idx[None, :]
    ci = (jnp.arange(T) // 64)
    same = ci[:, None] == ci[None, :]
    earlier = ci[None, :] < ci[:, None]
    dist = jnp.where(same, jnp.abs(diff), diff)
    dmat = jnp.where(same | earlier, jnp.exp(log_gamma[:, None, None] * dist[None]), 0.0)
    ones = jnp.ones((1, 1, HEAD_DIM), F32)
    qdec = jnp.exp(log_gamma[:, None] * (idx + 1.0)[None, :])[:, :, None] * ones
    kdec = jnp.exp(log_gamma[:, None] * (T - 1.0 - idx)[None, :])[:, :, None] * ones
    bdec = jnp.exp(log_gamma * T)[:, None, None] * jnp.ones((1, HEAD_DIM, HEAD_DIM), F32)
    return dmat, qdec, kdec, bdec


def _rope_tables(S):
    half = HEAD_DIM // 2
    inv = 1.0 / (ROPE_BASE ** (jnp.arange(half, dtype=F32) / half))
    ang = jnp.arange(S).astype(F32)[:, None] * inv[None, :]
    c = jnp.cos(ang)
    s = jnp.sin(ang)
    cos = jnp.tile(jnp.concatenate([c, c], axis=1), (1, N_RET_HEADS))
    sin = jnp.tile(jnp.concatenate([-s, s], axis=1), (1, N_RET_HEADS))
    return cos, sin


def ret_fwd(q, k, v, gate, ng, tables, T=RET_TILE):
    H, S, dh = q.shape
    dmat, qdec, kdec, bdec = tables

    def body(q_ref, k_ref, v_ref, gt_ref, ng_ref, dm_ref, qd_ref, kd_ref, bd_ref, o_ref, y_ref, st_ref, s_sc):
        n = pl.program_id(1)

        @pl.when(n == 0)
        def _():
            s_sc[...] = jnp.zeros_like(s_sc)

        qv = q_ref[...]
        kv = k_ref[...]
        vv = v_ref[...]
        state = s_sc[...]
        st_ref[...] = state
        sc = (_dot_nt(qv, kv) * dm_ref[...]).astype(BF16)
        qd = (qv.astype(F32) * qd_ref[...]).astype(BF16)
        y = _dot(sc, vv) + _dot(qd, state.astype(BF16))
        y_ref[...] = y
        kd = (kv.astype(F32) * kd_ref[...]).astype(BF16)
        s_sc[...] = bd_ref[...] * state + _dot_tn(kd, vv)
        mu = jnp.mean(y, axis=-1, keepdims=True)
        yc = y - mu
        yn = yc * lax.rsqrt(jnp.mean(yc * yc, axis=-1, keepdims=True) + EPS)
        gt = gt_ref[...]
        o_ref[...] = gt * _sigmoid(gt) * (yn * ng_ref[...])

    blk = lambda h, n: (h, n, 0)
    head = lambda h, n: (h, 0, 0)
    return pl.pallas_call(
        body, name="ret_fwd",
        grid=(H, S // T),
        in_specs=[
            pl.BlockSpec((None, T, dh), blk),
            pl.BlockSpec((None, T, dh), blk),
            pl.BlockSpec((None, T, dh), blk),
            pl.BlockSpec((None, T, dh), blk),
            pl.BlockSpec((None, 1, dh), head),
            pl.BlockSpec((None, T, T), head),
            pl.BlockSpec((None, T, dh), head),
            pl.BlockSpec((None, T, dh), head),
            pl.BlockSpec((None, dh, dh), head),
        ],
        out_specs=[
            pl.BlockSpec((None, T, dh), blk),
            pl.BlockSpec((None, T, dh), blk),
            pl.BlockSpec((None, None, dh, dh), lambda h, n: (h, n, 0, 0)),
        ],
        out_shape=[
            jax.ShapeDtypeStruct((H, S, dh), F32),
            jax.ShapeDtypeStruct((H, S, dh), F32),
            jax.ShapeDtypeStruct((H, S // T, dh, dh), F32),
        ],
        scratch_shapes=[pltpu.VMEM((dh, dh), F32)],
        compiler_params=_params(2),
    )(q, k, v, gate, ng, dmat, qdec, kdec, bdec)


def ret_bwd(do, q, k, v, gate, ng, y, states, tables, T=RET_TILE):
    H, S, dh = q.shape
    nb = S // T
    dmat, qdec, kdec, bdec = tables

    def body(do_ref, q_ref, k_ref, v_ref, gt_ref, ng_ref, y_ref, st_ref, dm_ref, qd_ref, kd_ref, bd_ref,
             dq_ref, dk_ref, dv_ref, dgt_ref, dng_ref, u_sc):
        n = pl.program_id(1)

        @pl.when(n == 0)
        def _():
            u_sc[...] = jnp.zeros_like(u_sc)
            dng_ref[...] = jnp.zeros_like(dng_ref)

        yv = y_ref[...]
        mu = jnp.mean(yv, axis=-1, keepdims=True)
        yc = yv - mu
        rstd = lax.rsqrt(jnp.mean(yc * yc, axis=-1, keepdims=True) + EPS)
        yn = yc * rstd
        gt = gt_ref[...]
        sg = _sigmoid(gt)
        ngv = ng_ref[...]
        dout = do_ref[...]
        dgt_ref[...] = dout * (yn * ngv) * (sg * (1.0 + gt * (1.0 - sg)))
        dn = dout * (gt * sg)
        dng_ref[...] += jnp.sum(dn * yn, axis=0, keepdims=True)
        dyn = dn * ngv
        dy = rstd * (dyn - jnp.mean(dyn, axis=-1, keepdims=True) - yn * jnp.mean(dyn * yn, axis=-1, keepdims=True))
        dyb = dy.astype(BF16)

        qv = q_ref[...]
        kv = k_ref[...]
        vv = v_ref[...]
        dm = dm_ref[...]
        qdt = qd_ref[...]
        kdt = kd_ref[...]
        sb = st_ref[...].astype(BF16)
        u = u_sc[...]
        ub = u.astype(BF16)
        dqk = (_dot_nt(dyb, vv) * dm).astype(BF16)
        sc = (_dot_nt(qv, kv) * dm).astype(BF16)
        qd = (qv.astype(F32) * qdt).astype(BF16)
        kd = (kv.astype(F32) * kdt).astype(BF16)
        dq_ref[...] = _dot(dqk, kv) + qdt * _dot_nt(dyb, sb)
        dk_ref[...] = _dot_tn(dqk, qv) + kdt * _dot_nt(vv, ub)
        dv_ref[...] = _dot_tn(sc, dyb) + _dot(kd, ub)
        u_sc[...] = bd_ref[...] * u + _dot_tn(qd, dyb)

    blk = lambda h, n: (h, nb - 1 - n, 0)
    head = lambda h, n: (h, 0, 0)
    return pl.pallas_call(
        body, name="ret_bwd",
        grid=(H, nb),
        in_specs=[
            pl.BlockSpec((None, T, dh), blk),
            pl.BlockSpec((None, T, dh), blk),
            pl.BlockSpec((None, T, dh), blk),
            pl.BlockSpec((None, T, dh), blk),
            pl.BlockSpec((None, T, dh), blk),
            pl.BlockSpec((None, 1, dh), head),
            pl.BlockSpec((None, T, dh), blk),
            pl.BlockSpec((None, None, dh, dh), lambda h, n: (h, nb - 1 - n, 0, 0)),
            pl.BlockSpec((None, T, T), head),
            pl.BlockSpec((None, T, dh), head),
            pl.BlockSpec((None, T, dh), head),
            pl.BlockSpec((None, dh, dh), head),
        ],
        out_specs=[
            pl.BlockSpec((None, T, dh), blk),
            pl.BlockSpec((None, T, dh), blk),
            pl.BlockSpec((None, T, dh), blk),
            pl.BlockSpec((None, T, dh), blk),
            pl.BlockSpec((None, 1, dh), head),
        ],
        out_shape=[jax.ShapeDtypeStruct((H, S, dh), F32)] * 4 + [jax.ShapeDtypeStruct((H, 1, dh), F32)],
        scratch_shapes=[pltpu.VMEM((dh, dh), F32)],
        compiler_params=_params(2),
    )(do, q, k, v, gate, ng, y, states, dmat, qdec, kdec, bdec)


def loss_head(x, g, target, tm=ROW_TILE):
    S = x.shape[0]

    def body(x_ref, g_ref, t_ref, loss_ref, dx_ref, dg_ref):
        i = pl.program_id(0)
        xv = x_ref[...]
        gv = g_ref[...]
        _, xhat = _rms_stats(xv)
        err = xhat * gv - t_ref[...]
        part = 0.5 * jnp.sum(jnp.mean(err * err, axis=-1, keepdims=True), axis=0, keepdims=True)
        dx, _, dg = _rms_bwd(xv, gv, err * (1.0 / D_MODEL))
        dx_ref[...] = dx
        part = jnp.broadcast_to(part, (1, 128))

        @pl.when(i == 0)
        def _():
            loss_ref[...] = part
            dg_ref[...] = dg

        @pl.when(i > 0)
        def _():
            loss_ref[...] += part
            dg_ref[...] += dg

    row = lambda i: (i, 0)
    one = lambda i: (0, 0)
    return pl.pallas_call(
        body, name="loss_head",
        grid=(S // tm,),
        in_specs=[pl.BlockSpec((tm, D_MODEL), row), pl.BlockSpec((1, D_MODEL), one), pl.BlockSpec((tm, D_MODEL), row)],
        out_specs=[pl.BlockSpec((1, 128), one), pl.BlockSpec((tm, D_MODEL), row), pl.BlockSpec((1, D_MODEL), one)],
        out_shape=[
            jax.ShapeDtypeStruct((1, 128), F32),
            jax.ShapeDtypeStruct((S, D_MODEL), F32),
            jax.ShapeDtypeStruct((1, D_MODEL), F32),
        ],
        compiler_params=_params(1),
    )(x, g, target)


def adamw(parts, w, m, v, tr, transposed=False):
    L, R, C = w.shape
    nr = R // tr
    c1 = 1.0 / (1.0 - ADAM_B1 ** ADAM_STEP)
    c2 = 1.0 / (1.0 - ADAM_B2 ** ADAM_STEP)

    def body(*refs):
        p_refs = refs[:L]
        w_ref, m_ref, v_ref, g_ref, d_ref, mo_ref, vo_ref = refs[L:]
        l = pl.program_id(0)
        g = None
        for d in range(N_DEV):
            pd = p_refs[0][d].astype(F32)
            for ll in range(1, L):
                pd = jnp.where(l == ll, p_refs[ll][d].astype(F32), pd)
            g = pd if g is None else g + pd
        if transposed:
            g = g.T
        mn = ADAM_B1 * m_ref[...] + (1.0 - ADAM_B1) * g
        vn = ADAM_B2 * v_ref[...] + (1.0 - ADAM_B2) * (g * g)
        g_ref[...] = g
        mo_ref[...] = mn
        vo_ref[...] = vn
        d_ref[...] = -ADAM_LR * ((mn * c1) / (jnp.sqrt(vn * c2) + ADAM_EPS) + ADAM_WD * w_ref[...])

    def part_spec(ll):
        def block(l, i):
            return jnp.where(l == ll, i, jnp.where(l < ll, 0, nr - 1))
        if transposed:
            return pl.BlockSpec((N_DEV, C, tr), lambda l, i: (0, 0, block(l, i)))
        return pl.BlockSpec((N_DEV, tr, C), lambda l, i: (0, block(l, i), 0))

    blk = pl.BlockSpec((None, tr, C), lambda l, i: (l, i, 0))
    return pl.pallas_call(
        body, name="adamw",
        grid=(L, nr),
        in_specs=[part_spec(ll) for ll in range(L)] + [blk] * 3,
        out_specs=[blk] * 4,
        out_shape=[jax.ShapeDtypeStruct((L, R, C), F32)] * 4,
        compiler_params=_params(2),
    )(*parts, w, m, v)


def _my_id():
    return lax.axis_index("x") * 4 + lax.axis_index("y") * 2 + lax.axis_index("c")


def _peer(k):
    x, y, c = lax.axis_index("x"), lax.axis_index("y"), lax.axis_index("c")
    px = 1 - x if k & 4 else x
    py = 1 - y if k & 2 else y
    pc = 1 - c if k & 1 else c
    return (px, py, pc), px * 4 + py * 2 + pc


GATHER = "gather"
EXCHANGE = "exchange"


def _copies(kind, ins, outs, send_sems, recv_sems, local_sems, receive_side):
    me = _my_id()
    local, sends, recvs = [], [], []
    for t in range(len(ins)):
        src = ins[t] if kind == GATHER else ins[t].at[me]
        local.append(pltpu.make_async_copy(src, outs[t].at[me], local_sems.at[t]))
    for k in range(1, N_DEV):
        dev, pid = _peer(k)
        for t in range(len(ins)):
            sems = dict(send_sem=send_sems.at[t, k - 1], recv_sem=recv_sems.at[t, k - 1],
                        device_id=dev, device_id_type=pl.DeviceIdType.MESH)
            src = ins[t] if kind == GATHER else ins[t].at[pid]
            sends.append(pltpu.make_async_remote_copy(src_ref=src, dst_ref=outs[t].at[me], **sems))
            if receive_side:
                recvs.append(pltpu.make_async_remote_copy(src_ref=src, dst_ref=outs[t].at[pid], **sems))
    return local, sends, recvs


def _comm_start(kind, ins, outs, sems):
    local, sends, _ = _copies(kind, ins, outs, *sems, receive_side=False)
    for cp in local + sends:
        cp.start()


def _comm_wait(kind, ins, outs, sems):
    local, sends, recvs = _copies(kind, ins, outs, *sems, receive_side=True)
    for cp in recvs:
        cp.wait_recv()
    for cp in sends:
        cp.wait_send()
    for cp in local:
        cp.wait()


def _comm_shapes(kind, arrays):
    n = len(arrays)
    out_shape = [jax.ShapeDtypeStruct(((N_DEV,) + a.shape) if kind == GATHER else a.shape, a.dtype) for a in arrays]
    sems = [pltpu.SemaphoreType.DMA((n, N_DEV - 1)), pltpu.SemaphoreType.DMA((n, N_DEV - 1)),
            pltpu.SemaphoreType.DMA((n,))]
    return out_shape, sems


def communicate(kind, arrays):
    n = len(arrays)

    def body(*refs):
        ins, outs, sems = refs[:n], refs[n:2 * n], refs[2 * n:]
        _comm_start(kind, ins, outs, sems)
        _comm_wait(kind, ins, outs, sems)

    out_shape, sems = _comm_shapes(kind, arrays)
    any_spec = pl.BlockSpec(memory_space=pl.ANY)
    return pl.pallas_call(
        body, name=kind, in_specs=[any_spec] * n, out_specs=[any_spec] * n, out_shape=out_shape, scratch_shapes=sems,
    )(*arrays)


def _call(body, operands, comm, *, name, grid, in_specs, out_specs, out_shape, scratch_shapes):
    if comm is None:
        outs = pl.pallas_call(body, name=name, grid=grid, in_specs=in_specs, out_specs=out_specs, out_shape=out_shape,
                              scratch_shapes=scratch_shapes, compiler_params=_params(len(grid)))(*operands)
        return outs, []
    kind, arrays = comm
    n, n_in, n_out, n_sc = len(arrays), len(in_specs), len(out_specs), len(scratch_shapes)

    def carrier(*refs):
        ins, cins = refs[:n_in], refs[n_in:n_in + n]
        refs = refs[n_in + n:]
        outs, couts = refs[:n_out], refs[n_out:n_out + n]
        scratch, sems = refs[n_out + n:n_out + n + n_sc], refs[n_out + n + n_sc:]
        steps = [pl.program_id(a) for a in range(len(grid))]
        first = functools.reduce(jnp.logical_and, [s == 0 for s in steps])
        last = functools.reduce(jnp.logical_and, [s == g - 1 for s, g in zip(steps, grid)])

        @pl.when(first)
        def _():
            _comm_start(kind, cins, couts, sems)

        body(*ins, *outs, *scratch)

        @pl.when(last)
        def _():
            _comm_wait(kind, cins, couts, sems)

    comm_shape, sems = _comm_shapes(kind, arrays)
    any_spec = pl.BlockSpec(memory_space=pl.ANY)
    outs = pl.pallas_call(
        carrier, name=f"{name}_{kind}", grid=grid,
        in_specs=list(in_specs) + [any_spec] * n,
        out_specs=list(out_specs) + [any_spec] * n,
        out_shape=list(out_shape) + comm_shape,
        scratch_shapes=list(scratch_shapes) + sems,
        compiler_params=_params(len(grid)),
    )(*operands, *arrays)
    return outs[:n_out], outs[n_out:]


def _row(v):
    return v.reshape(1, -1)


def _pad_taps(cw):
    return jnp.concatenate([cw, jnp.zeros((CONV_HALO - CONV_WIDTH, D_CONV), F32)], axis=0)


COL_SHARDED = ("ffn1_w_in", "mix_w_in", "ffn2_w_in")
ROW_SHARDED = ("ffn1_w_out", "mix_w_out", "ffn2_w_out")
SMALL = ("ffn1_norm", "mix_norm", "conv_b", "conv_ln_g", "conv_ln_b", "ret_norm_g", "ffn2_norm", "final_norm")
WEIGHTS = ("ffn1_norm", "ffn1_w_in", "ffn1_w_out", "mix_norm", "mix_w_in", "conv_w", "conv_b", "conv_ln_g",
           "conv_ln_b", "ret_norm_g", "mix_w_out", "ffn2_norm", "ffn2_w_in", "ffn2_w_out", "final_norm")
SMALL_ROWS = 32

FFN1 = ("ffn1_w_in", "ffn1_w_out")
MIX = ("mix_w_in", "mix_w_out")
FFN2 = ("ffn2_w_in", "ffn2_w_out")
STAGE_A = [(n, 0) for n in FFN1]
STAGE_B = [(n, 0) for n in MIX] + [("conv_w", None)]
STAGE_C = [(n, 0) for n in FFN2] + [(n, 1) for n in FFN1 + MIX + FFN2]
STAGE_D = [(n, 1) for n in FFN2]
STAGE_E = [(n, 1) for n in MIX + FFN1] + [(n, 0) for n in FFN2]
STAGE_F = [(n, 0) for n in MIX]
STAGE_G = [("ffn1_w_in", 0)]
STAGE_H = [("ffn1_w_out", 0)]


def _natural(name, got):
    if name == "conv_w":
        return got.transpose(1, 2, 0, 3).reshape(DEPTH, CONV_WIDTH, D_CONV)
    return got.reshape(-1, D_MODEL)


def _by_device(grad):
    return grad.reshape(N_DEV, -1, D_MODEL)


def _pack_small(g):
    flat = jnp.concatenate([g[n].reshape(-1) for n in SMALL] + [g["conv_w"].reshape(-1)])
    flat = jnp.concatenate([flat, jnp.zeros((SMALL_ROWS * D_MODEL - flat.shape[0],), F32)])
    return flat.reshape(SMALL_ROWS, D_MODEL)


def _unpack_small(buf, like):
    flat = buf.reshape(-1)
    out, off = {}, 0
    for n in SMALL:
        size = int(np.prod(like[n].shape))
        out[n] = flat[off:off + size].reshape(like[n].shape)
        off += size
    size = DEPTH * CONV_WIDTH * D_CONV
    out["conv_w"] = flat[off:off + size].reshape(DEPTH, CONV_WIDTH, D_CONV)
    return out


def kernel(x, ffn1_norm, ffn1_w_in, ffn1_w_out, mix_norm, mix_w_in, conv_w, conv_b, conv_ln_g, conv_ln_b, ret_norm_g, mix_w_out, ffn2_norm, ffn2_w_in, ffn2_w_out, final_norm, loss_target, m_ffn1_norm, m_ffn1_w_in, m_ffn1_w_out, m_mix_norm, m_mix_w_in, m_conv_w, m_conv_b, m_conv_ln_g, m_conv_ln_b, m_ret_norm_g, m_mix_w_out, m_ffn2_norm, m_ffn2_w_in, m_ffn2_w_out, m_final_norm, v_ffn1_norm, v_ffn1_w_in, v_ffn1_w_out, v_mix_norm, v_mix_w_in, v_conv_w, v_conv_b, v_conv_ln_g, v_conv_ln_b, v_ret_norm_g, v_mix_w_out, v_ffn2_norm, v_ffn2_w_in, v_ffn2_w_out, v_final_norm):
    args = locals()
    w = {n: args[n] for n in WEIGHTS}
    m = {n: args["m_" + n] for n in WEIGHTS}
    v = {n: args["v_" + n] for n in WEIGHTS}
    me = _my_id()
    x = x[0]
    target = loss_target[0]
    S = x.shape[0]
    cos, sin = _rope_tables(S)
    tables = _ret_tables()

    full = {}

    def shard(n, l):
        if n == "conv_w":
            return w[n]
        return (w[n][l].T if n in COL_SHARDED else w[n][l]).astype(BF16)

    def gather(keys):
        return GATHER, [shard(n, l) for n, l in keys]

    def gathered(keys, got):
        for (n, l), g in zip(keys, got):
            full[(n, l)] = _natural(n, g)

    gathered(STAGE_A, communicate(*gather(STAGE_A)))

    saved = []
    for l in range(DEPTH):
        sv = {"x0": x}
        (x, sv["gate1"], sv["up1"]), got = ffn_fwd(x, _row(w["ffn1_norm"][l]), full[("ffn1_w_in", l)],
                                                   full[("ffn1_w_out", l)], gather(STAGE_B) if l == 0 else None)
        gathered(STAGE_B if l == 0 else [], got)
        sv["x1"] = x
        (sv["u"], sv["q_sb"], sv["k_sb"], sv["v_sb"], sv["qt_sb"], sv["q_r"], sv["k_r"], sv["v_r"],
         sv["g_r"]) = mix_in_fwd(x, _row(w["mix_norm"][l]), full[("mix_w_in", l)], cos, sin)
        cw = _pad_taps(full[("conv_w", None)][l])
        y_conv, sv["ypre"] = conv_fwd(sv["u"], cw, _row(w["conv_b"][l]), _row(w["conv_ln_g"][l]), _row(w["conv_ln_b"][l]))
        (o_sb, sv["tot"]), got = sb_fwd(sv["q_sb"], sv["k_sb"], sv["v_sb"], gather(STAGE_C) if l == 0 else None)
        gathered(STAGE_C if l == 0 else [], got)
        ng = w["ret_norm_g"][l].reshape(N_RET_HEADS, 1, HEAD_DIM)
        o_r, sv["y_r"], sv["states"] = ret_fwd(sv["q_r"], sv["k_r"], sv["v_r"], sv["g_r"], ng, tables)
        x, sv["ycat"] = mix_out_fwd(y_conv, o_sb, o_r, full[("mix_w_out", l)], x)
        sv["x2"] = x
        (x, sv["gate2"], sv["up2"]), _ = ffn_fwd(x, _row(w["ffn2_norm"][l]), full[("ffn2_w_in", l)],
                                                 full[("ffn2_w_out", l)])
        saved.append(sv)

    loss_acc, dx, dg_final = loss_head(x, _row(w["final_norm"]), target)
    loss = lax.psum(loss_acc[0, 0], ("x", "y", "c"))

    g = {"final_norm": dg_final.reshape(D_MODEL)}
    received = {}

    def exchange(keys, extra=(), dtype=F32):
        return EXCHANGE, [_by_device(g[(n, l)]).astype(dtype) for n, l in keys] + list(extra)

    def exchanged(keys, got):
        for key, p in zip(keys, got):
            received[key] = p

    def ffn_back(dx, x_in, gate, up, norm, names, l, comm=None):
        (dx, h, dyh, dgate, dup, hid, dg), got = ffn_bwd(dx, x_in, _row(norm), gate, up, full[(names[0], l)],
                                                         full[(names[1], l)], comm)
        g[(names[0], l)] = matmul_tn([dgate, dup], h, FF_TILE, D_MODEL, name="ffn_dw_in")
        if [(names[0], l)] == STAGE_G:
            g[(names[1], l)], got_g = matmul_tn([hid], dyh, FF_TILE, D_MODEL, name="ffn_dw_out",
                                                comm=exchange(STAGE_G, dtype=BF16))
            exchanged(STAGE_G, got_g)
        else:
            g[(names[1], l)] = matmul_tn([hid], dyh, FF_TILE, D_MODEL, name="ffn_dw_out")
        return dx, dg.reshape(D_MODEL), got

    for l in reversed(range(DEPTH)):
        sv = saved[l]
        dx, g[("ffn2_norm", l)], _ = ffn_back(dx, sv["x2"], sv["gate2"], sv["up2"], w["ffn2_norm"][l], FFN2, l)
        dxb, dy_conv, do_sb, dot_sb, do_r = mix_out_bwd(dx, full[("mix_w_out", l)])
        g[("mix_w_out", l)] = matmul_tn([sv["ycat"]], dxb, D_MODEL, D_MODEL, name="mix_dw_out")
        cw = _pad_taps(full[("conv_w", None)][l])
        du_conv, dcw, dsm = conv_bwd(dy_conv, sv["ypre"], sv["u"], cw, _row(w["conv_ln_g"][l]), _row(w["conv_ln_b"][l]))
        g[("conv_w", l)] = dcw[:CONV_WIDTH]
        g[("conv_b", l)], g[("conv_ln_g", l)], g[("conv_ln_b", l)] = dsm[0], dsm[1], dsm[2]
        stage = STAGE_D if l == DEPTH - 1 else STAGE_E
        (dq_sb, dk_t, dv_t), got = sb_bwd(sv["q_sb"], sv["k_sb"], sv["v_sb"], do_sb, sv["qt_sb"], dot_sb, sv["tot"],
                                          exchange(stage))
        exchanged(stage, got)
        ng = w["ret_norm_g"][l].reshape(N_RET_HEADS, 1, HEAD_DIM)
        dq_r, dk_r, dv_r, dg_r, dng = ret_bwd(do_r, sv["q_r"], sv["k_r"], sv["v_r"], sv["g_r"], ng, sv["y_r"],
                                              sv["states"], tables)
        g[("ret_norm_g", l)] = dng.reshape(D_RET)
        dx, h, dproj, dg = mix_in_bwd(du_conv, dq_sb, dk_t, dv_t, dq_r, dk_r, dv_r, dg_r, cos, sin,
                                      full[("mix_w_in", l)], sv["x1"], _row(w["mix_norm"][l]), dx)
        g[("mix_norm", l)] = dg.reshape(D_MODEL)
        g[("mix_w_in", l)] = matmul_tn([dproj], h, D_MODEL, D_MODEL, name="mix_dw_in")
        dx, g[("ffn1_norm", l)], got = ffn_back(dx, sv["x0"], sv["gate1"], sv["up1"], w["ffn1_norm"][l], FFN1, l,
                                                exchange(STAGE_F) if l == 0 else None)
        exchanged(STAGE_F if l == 0 else [], got)
    grad_x = dx

    small_names = [n for n in SMALL if n != "final_norm"] + ["conv_w"]
    gs = {n: jnp.stack([g[(n, l)] for l in range(DEPTH)], axis=0) for n in small_names}
    gs["final_norm"] = g["final_norm"]
    small = _pack_small(gs)
    got = communicate(*exchange(STAGE_H, [jnp.broadcast_to(small[None], (N_DEV, SMALL_ROWS, D_MODEL))], dtype=BF16))
    exchanged(STAGE_H, got[:-1])

    grad, delta, new_m, new_v = {}, {}, {}, {}
    for n in COL_SHARDED + ROW_SHARDED:
        rows = w[n].shape[1]
        col = n in COL_SHARDED
        grad[n], delta[n], new_m[n], new_v[n] = adamw([received[(n, l)] for l in range(DEPTH)], w[n], m[n], v[n],
                                                      tr=128 if col else rows // 2, transposed=col)

    def small_pack(d):
        mine = dict(d)
        cwf = jnp.zeros((DEPTH, CONV_WIDTH, D_CONV), F32)
        mine["conv_w"] = lax.dynamic_update_slice(cwf, d["conv_w"], (0, 0, me * (D_CONV // N_DEV)))
        return _pack_small(mine)

    outs = adamw([got[-1]], small_pack(w)[None], small_pack(m)[None], small_pack(v)[None], tr=SMALL_ROWS)
    for dst, o in zip((grad, delta, new_m, new_v), outs):
        un = _unpack_small(o[0], w)
        un["conv_w"] = lax.dynamic_slice(un["conv_w"], (0, 0, me * (D_CONV // N_DEV)),
                                         (DEPTH, CONV_WIDTH, D_CONV // N_DEV))
        dst.update(un)

    return (loss, grad_x[None], *[grad[n] for n in WEIGHTS], *[delta[n] for n in WEIGHTS],
            *[new_m[n] for n in WEIGHTS], *[new_v[n] for n in WEIGHTS])
```

```python
import functools

import numpy as np
import jax
import jax.numpy as jnp
from jax import lax
from jax.experimental import pallas as pl
from jax.experimental.pallas import tpu as pltpu

F32 = jnp.float32
BF16 = jnp.bfloat16

D_MODEL = 1024
DEPTH = 2
D_FF = 2816
D_CONV = 256
CONV_WIDTH = 31
CONV_HALO = 32
D_SB = 512
N_SB_HEADS = 8
D_RET = 256
N_RET_HEADS = 4
HEAD_DIM = 64
D_IN_PROJ = 3072
ROPE_BASE = 10000.0
EPS = 1e-6
N_DEV = 8

ADAM_LR = 0.001
ADAM_B1 = 0.9
ADAM_B2 = 0.999
ADAM_EPS = 1e-08
ADAM_WD = 0.01
ADAM_STEP = 10

VMEM_LIMIT = 56 * 1024 * 1024
ROW_TILE = 512
FF_TILE = 1408
SB_TILE = 256
SB_ROWS = 512
RET_TILE = 512
CONV_TILE = 256

NT_DIMS = (((1,), (1,)), ((), ()))
TN_DIMS = (((0,), (0,)), ((), ()))


def _params(n_axes, vmem=VMEM_LIMIT):
    return pltpu.CompilerParams(dimension_semantics=("arbitrary",) * n_axes, vmem_limit_bytes=vmem)


def _dot(a, b):
    return jnp.dot(a, b, preferred_element_type=F32)


def _dot_nt(a, b):
    return lax.dot_general(a, b, NT_DIMS, preferred_element_type=F32)


def _dot_tn(a, b):
    return lax.dot_general(a, b, TN_DIMS, preferred_element_type=F32)


def _sigmoid(z):
    return 1.0 / (1.0 + jnp.exp(-z))


def _rms_stats(xv):
    r = lax.rsqrt(jnp.mean(xv * xv, axis=-1, keepdims=True) + EPS)
    return r, xv * r


def _rms_bwd(xv, g, dh):
    r, xhat = _rms_stats(xv)
    dxhat = dh * g
    dx = r * (dxhat - xhat * jnp.mean(dxhat * xhat, axis=-1, keepdims=True))
    dg = jnp.sum(dh * xhat, axis=0, keepdims=True)
    return dx, (xhat * g).astype(BF16), dg


def ffn_fwd(x, g, w_in, w_out, comm=None, tm=ROW_TILE):
    S = x.shape[0]
    nj = D_FF // FF_TILE

    def body(x_ref, g_ref, wg_ref, wu_ref, wo_ref, y_ref, gate_ref, up_ref, h_sc, acc_sc):
        j = pl.program_id(1)

        @pl.when(j == 0)
        def _():
            _, xhat = _rms_stats(x_ref[...])
            h_sc[...] = (xhat * g_ref[...]).astype(BF16)
            acc_sc[...] = jnp.zeros_like(acc_sc)

        h = h_sc[...]
        gt = _dot_nt(h, wg_ref[...])
        up = _dot_nt(h, wu_ref[...])
        gate_ref[...] = gt.astype(BF16)
        up_ref[...] = up.astype(BF16)
        hid = (gt * _sigmoid(gt) * up).astype(BF16)
        acc_sc[...] += _dot(hid, wo_ref[...])

        @pl.when(j == nj - 1)
        def _():
            y_ref[...] = x_ref[...] + 0.5 * acc_sc[...]

    return _call(
        body, (x, g, w_in, w_in, w_out), comm, name="ffn_fwd",
        grid=(S // tm, nj),
        in_specs=[
            pl.BlockSpec((tm, D_MODEL), lambda i, j: (i, 0)),
            pl.BlockSpec((1, D_MODEL), lambda i, j: (0, 0)),
            pl.BlockSpec((FF_TILE, D_MODEL), lambda i, j: (j, 0)),
            pl.BlockSpec((FF_TILE, D_MODEL), lambda i, j: (j + nj, 0)),
            pl.BlockSpec((FF_TILE, D_MODEL), lambda i, j: (j, 0)),
        ],
        out_specs=[
            pl.BlockSpec((tm, D_MODEL), lambda i, j: (i, 0)),
            pl.BlockSpec((tm, FF_TILE), lambda i, j: (i, j)),
            pl.BlockSpec((tm, FF_TILE), lambda i, j: (i, j)),
        ],
        out_shape=[
            jax.ShapeDtypeStruct((S, D_MODEL), F32),
            jax.ShapeDtypeStruct((S, D_FF), BF16),
            jax.ShapeDtypeStruct((S, D_FF), BF16),
        ],
        scratch_shapes=[pltpu.VMEM((tm, D_MODEL), BF16), pltpu.VMEM((tm, D_MODEL), F32)],
    )


def ffn_bwd(dy, x, g, gate, up, w_in, w_out, comm=None, tm=ROW_TILE // 2):
    S = x.shape[0]
    nj = D_FF // FF_TILE

    def body(dy_ref, x_ref, g_ref, gate_ref, up_ref, w_ref, wo_ref,
             dx_ref, h_ref, dyh_ref, dgate_ref, dup_ref, hid_ref, dg_ref):
        i = pl.program_id(0)
        d2 = (0.5 * dy_ref[...]).astype(BF16)
        dyh_ref[...] = d2
        dh = None
        for j in range(nj):
            cols = pl.ds(j * FF_TILE, FF_TILE)
            dhid = _dot_nt(d2, wo_ref[cols, :])
            gt = gate_ref[:, cols].astype(F32)
            u = up_ref[:, cols].astype(F32)
            sig = _sigmoid(gt)
            sl = gt * sig
            dgate = (dhid * u * (sig * (1.0 + gt * (1.0 - sig)))).astype(BF16)
            dup = (dhid * sl).astype(BF16)
            dgate_ref[:, cols] = dgate
            dup_ref[:, cols] = dup
            hid_ref[:, cols] = (sl * u).astype(BF16)
            part = _dot(dgate, w_ref[cols, :]) + _dot(dup, w_ref[pl.ds(D_FF + j * FF_TILE, FF_TILE), :])
            dh = part if dh is None else dh + part
        dx, h, dg = _rms_bwd(x_ref[...], g_ref[...], dh)
        dx_ref[...] = dy_ref[...] + dx
        h_ref[...] = h

        @pl.when(i == 0)
        def _():
            dg_ref[...] = dg

        @pl.when(i > 0)
        def _():
            dg_ref[...] += dg

    row = lambda i: (i, 0)
    one = lambda i: (0, 0)
    resident = pl.Buffered(1)
    return _call(
        body, (dy, x, g, gate, up, w_in, w_out), comm, name="ffn_bwd",
        grid=(S // tm,),
        in_specs=[
            pl.BlockSpec((tm, D_MODEL), row),
            pl.BlockSpec((tm, D_MODEL), row),
            pl.BlockSpec((1, D_MODEL), one),
            pl.BlockSpec((tm, D_FF), row),
            pl.BlockSpec((tm, D_FF), row),
            pl.BlockSpec((2 * D_FF, D_MODEL), one, pipeline_mode=resident),
            pl.BlockSpec((D_FF, D_MODEL), one, pipeline_mode=resident),
        ],
        out_specs=[
            pl.BlockSpec((tm, D_MODEL), row),
            pl.BlockSpec((tm, D_MODEL), row),
            pl.BlockSpec((tm, D_MODEL), row),
            pl.BlockSpec((tm, D_FF), row),
            pl.BlockSpec((tm, D_FF), row),
            pl.BlockSpec((tm, D_FF), row),
            pl.BlockSpec((1, D_MODEL), one),
        ],
        out_shape=[
            jax.ShapeDtypeStruct((S, D_MODEL), F32),
            jax.ShapeDtypeStruct((S, D_MODEL), BF16),
            jax.ShapeDtypeStruct((S, D_MODEL), BF16),
            jax.ShapeDtypeStruct((S, D_FF), BF16),
            jax.ShapeDtypeStruct((S, D_FF), BF16),
            jax.ShapeDtypeStruct((S, D_FF), BF16),
            jax.ShapeDtypeStruct((1, D_MODEL), F32),
        ],
        scratch_shapes=[],
    )


def matmul_tn(a_list, b, ta, tn, tk=ROW_TILE, name="matmul_tn", comm=None):
    S, ka = a_list[0].shape
    nb = b.shape[1]
    per = ka // ta

    def body(*refs):
        a_refs, b_ref, o_ref = refs[:-2], refs[-2], refs[-1]
        i = pl.program_id(0)
        k = pl.program_id(2)

        @pl.when(k == 0)
        def _():
            o_ref[...] = jnp.zeros_like(o_ref)

        for t, a_ref in enumerate(a_refs):
            @pl.when(lax.div(i, per) == t)
            def _(a_ref=a_ref):
                o_ref[...] += _dot_tn(a_ref[...], b_ref[...])

    def a_spec(t):
        def index(i, j, k):
            mine = lax.div(i, per) == t
            return jnp.where(mine, k, 0), jnp.where(mine, i - t * per, 0)
        return pl.BlockSpec((tk, ta), index)

    (out,), got = _call(
        body, (*a_list, b), comm, name=name,
        grid=(per * len(a_list), nb // tn, S // tk),
        in_specs=[a_spec(t) for t in range(len(a_list))] + [pl.BlockSpec((tk, tn), lambda i, j, k: (k, j))],
        out_specs=[pl.BlockSpec((ta, tn), lambda i, j, k: (i, j))],
        out_shape=[jax.ShapeDtypeStruct((ka * len(a_list), nb), F32)],
        scratch_shapes=[],
    )
    return (out, got) if comm is not None else out


SB_COLS = (2 * D_CONV, 2 * D_CONV + D_SB, 2 * D_CONV + 2 * D_SB)
RET_COLS = tuple(2 * D_CONV + 3 * D_SB + j * D_RET for j in range(4))


def _swap_halves(x):
    n = x.shape[1]
    lane = lax.broadcasted_iota(jnp.int32, x.shape, 1)
    first = (lane % HEAD_DIM) < (HEAD_DIM // 2)
    return jnp.where(first, pltpu.roll(x, n - HEAD_DIM // 2, 1), pltpu.roll(x, HEAD_DIM // 2, 1))


def _head(x, h):
    return x[:, h * HEAD_DIM:(h + 1) * HEAD_DIM]


def _heads_spec(n_heads, tm):
    return pl.BlockSpec((n_heads, tm, HEAD_DIM), lambda i: (0, i, 0))


def mix_in_fwd(x, g, w, cos, sin, tm=ROW_TILE):
    S = x.shape[0]

    def body(x_ref, g_ref, w_ref, c_ref, s_ref, u_ref, q_ref, k_ref, v_ref, qt_ref, qr_ref, kr_ref, vr_ref, gr_ref):
        _, xhat = _rms_stats(x_ref[...])
        proj = _dot_nt((xhat * g_ref[...]).astype(BF16), w_ref[...])
        u_ref[...] = proj[:, :2 * D_CONV]
        for h in range(N_SB_HEADS):
            q = (_head(proj[:, SB_COLS[0]:SB_COLS[1]], h) * 0.125).astype(BF16)
            q_ref[h] = q
            qt_ref[h] = q.T
            k_ref[h] = _head(proj[:, SB_COLS[1]:SB_COLS[2]], h).astype(BF16)
            v_ref[h] = _head(proj[:, SB_COLS[2]:RET_COLS[0]], h).astype(BF16)
        c = c_ref[...]
        s = s_ref[...]
        qv = proj[:, RET_COLS[0]:RET_COLS[1]]
        kv = proj[:, RET_COLS[1]:RET_COLS[2]]
        q_rot = ((qv * c + _swap_halves(qv) * s) * 0.125).astype(BF16)
        k_rot = (kv * c + _swap_halves(kv) * s).astype(BF16)
        for h in range(N_RET_HEADS):
            qr_ref[h] = _head(q_rot, h)
            kr_ref[h] = _head(k_rot, h)
            vr_ref[h] = _head(proj[:, RET_COLS[2]:RET_COLS[3]], h).astype(BF16)
            gr_ref[h] = _head(proj[:, RET_COLS[3]:], h)

    row = lambda i: (i, 0)
    one = lambda i: (0, 0)
    sb = jax.ShapeDtypeStruct((N_SB_HEADS, S, HEAD_DIM), BF16)
    ret = jax.ShapeDtypeStruct((N_RET_HEADS, S, HEAD_DIM), BF16)
    return pl.pallas_call(
        body, name="mix_in_fwd",
        grid=(S // tm,),
        in_specs=[
            pl.BlockSpec((tm, D_MODEL), row),
            pl.BlockSpec((1, D_MODEL), one),
            pl.BlockSpec((D_IN_PROJ, D_MODEL), one, pipeline_mode=pl.Buffered(1)),
            pl.BlockSpec((tm, D_RET), row),
            pl.BlockSpec((tm, D_RET), row),
        ],
        out_specs=[
            pl.BlockSpec((tm, 2 * D_CONV), row),
            _heads_spec(N_SB_HEADS, tm), _heads_spec(N_SB_HEADS, tm), _heads_spec(N_SB_HEADS, tm),
            pl.BlockSpec((N_SB_HEADS, HEAD_DIM, tm), lambda i: (0, 0, i)),
            _heads_spec(N_RET_HEADS, tm), _heads_spec(N_RET_HEADS, tm), _heads_spec(N_RET_HEADS, tm),
            _heads_spec(N_RET_HEADS, tm),
        ],
        out_shape=[
            jax.ShapeDtypeStruct((S, 2 * D_CONV), F32), sb, sb, sb,
            jax.ShapeDtypeStruct((N_SB_HEADS, HEAD_DIM, S), BF16),
            ret, ret, ret, jax.ShapeDtypeStruct((N_RET_HEADS, S, HEAD_DIM), F32),
        ],
        compiler_params=_params(1),
    )(x, g, w, cos, sin)


def mix_in_bwd(du, dq, dkt, dvt, dqr, dkr, dvr, dgr, cos, sin, w, x, g, dy, tm=SB_TILE):
    S = x.shape[0]
    assert dkt.shape[-1] == tm

    def body(du_ref, dq_ref, dkt_ref, dvt_ref, dqr_ref, dkr_ref, dvr_ref, dgr_ref, c_ref, s_ref, w_ref, x_ref, g_ref,
             dy_ref, dx_ref, h_ref, dp_ref, dg_ref):
        i = pl.program_id(0)
        sb_heads = range(N_SB_HEADS)
        ret_heads = range(N_RET_HEADS)
        c = c_ref[...]
        s = s_ref[...]
        dq_rot = jnp.concatenate([dqr_ref[h] for h in ret_heads], axis=1) * 0.125
        dk_rot = jnp.concatenate([dkr_ref[h] for h in ret_heads], axis=1)
        dproj = jnp.concatenate([
            du_ref[...].astype(BF16),
            jnp.concatenate([dq_ref[h] * 0.125 for h in sb_heads], axis=1).astype(BF16),
            jnp.concatenate([dkt_ref[h, 0].T for h in sb_heads], axis=1).astype(BF16),
            jnp.concatenate([dvt_ref[h, 0].T for h in sb_heads], axis=1).astype(BF16),
            (dq_rot * c - _swap_halves(dq_rot) * s).astype(BF16),
            (dk_rot * c - _swap_halves(dk_rot) * s).astype(BF16),
            jnp.concatenate([dvr_ref[h] for h in ret_heads], axis=1).astype(BF16),
            jnp.concatenate([dgr_ref[h] for h in ret_heads], axis=1).astype(BF16)], axis=1)
        dp_ref[...] = dproj
        dh = _dot(dproj, w_ref[...])
        dx, h, dg = _rms_bwd(x_ref[...], g_ref[...], dh)
        dx_ref[...] = dy_ref[...] + dx
        h_ref[...] = h

        @pl.when(i == 0)
        def _():
            dg_ref[...] = dg

        @pl.when(i > 0)
        def _():
            dg_ref[...] += dg

    row = lambda i: (i, 0)
    one = lambda i: (0, 0)
    tiles = pl.BlockSpec((N_SB_HEADS, 1, HEAD_DIM, tm), lambda i: (0, i, 0, 0))
    return pl.pallas_call(
        body, name="mix_in_bwd",
        grid=(S // tm,),
        in_specs=[
            pl.BlockSpec((tm, 2 * D_CONV), row),
            _heads_spec(N_SB_HEADS, tm), tiles, tiles,
            _heads_spec(N_RET_HEADS, tm), _heads_spec(N_RET_HEADS, tm), _heads_spec(N_RET_HEADS, tm),
            _heads_spec(N_RET_HEADS, tm),
            pl.BlockSpec((tm, D_RET), row),
            pl.BlockSpec((tm, D_RET), row),
            pl.BlockSpec((D_IN_PROJ, D_MODEL), one, pipeline_mode=pl.Buffered(1)),
            pl.BlockSpec((tm, D_MODEL), row),
            pl.BlockSpec((1, D_MODEL), one),
            pl.BlockSpec((tm, D_MODEL), row),
        ],
        out_specs=[
            pl.BlockSpec((tm, D_MODEL), row),
            pl.BlockSpec((tm, D_MODEL), row),
            pl.BlockSpec((tm, D_IN_PROJ), row),
            pl.BlockSpec((1, D_MODEL), one),
        ],
        out_shape=[
            jax.ShapeDtypeStruct((S, D_MODEL), F32),
            jax.ShapeDtypeStruct((S, D_MODEL), BF16),
            jax.ShapeDtypeStruct((S, D_IN_PROJ), BF16),
            jax.ShapeDtypeStruct((1, D_MODEL), F32),
        ],
        compiler_params=_params(1),
    )(du, dq, dkt, dvt, dqr, dkr, dvr, dgr, cos, sin, w, x, g, dy)


def mix_out_fwd(y_conv, o_sb, o_ret, w, x, tm=ROW_TILE):
    S = x.shape[0]

    def body(yc_ref, sb_ref, rt_ref, w_ref, x_ref, o_ref, ycat_ref):
        ycat = jnp.concatenate(
            [yc_ref[...]] + [sb_ref[h].astype(BF16) for h in range(N_SB_HEADS)]
            + [rt_ref[h].astype(BF16) for h in range(N_RET_HEADS)], axis=1)
        ycat_ref[...] = ycat
        o_ref[...] = x_ref[...] + _dot(ycat, w_ref[...])

    row = lambda i: (i, 0)
    return pl.pallas_call(
        body, name="mix_out_fwd",
        grid=(S // tm,),
        in_specs=[
            pl.BlockSpec((tm, D_CONV), row),
            _heads_spec(N_SB_HEADS, tm),
            _heads_spec(N_RET_HEADS, tm),
            pl.BlockSpec((D_MODEL, D_MODEL), lambda i: (0, 0)),
            pl.BlockSpec((tm, D_MODEL), row),
        ],
        out_specs=[pl.BlockSpec((tm, D_MODEL), row), pl.BlockSpec((tm, D_MODEL), row)],
        out_shape=[jax.ShapeDtypeStruct((S, D_MODEL), F32), jax.ShapeDtypeStruct((S, D_MODEL), BF16)],
        compiler_params=_params(1),
    )(y_conv, o_sb, o_ret, w, x)


def mix_out_bwd(dy, w, tm=ROW_TILE):
    S = dy.shape[0]

    def body(dy_ref, w_ref, dyb_ref, dc_ref, do_ref, dot_ref, dr_ref):
        d = dy_ref[...].astype(BF16)
        dyb_ref[...] = d
        dycat = _dot_nt(d, w_ref[...])
        dc_ref[...] = dycat[:, :D_CONV]
        for h in range(N_SB_HEADS):
            do = _head(dycat[:, D_CONV:D_CONV + D_SB], h).astype(BF16)
            do_ref[h] = do
            dot_ref[h] = do.T
        for h in range(N_RET_HEADS):
            dr_ref[h] = _head(dycat[:, D_CONV + D_SB:], h)

    row = lambda i: (i, 0)
    return pl.pallas_call(
        body, name="mix_out_bwd",
        grid=(S // tm,),
        in_specs=[
            pl.BlockSpec((tm, D_MODEL), row),
            pl.BlockSpec((D_MODEL, D_MODEL), lambda i: (0, 0)),
        ],
        out_specs=[
            pl.BlockSpec((tm, D_MODEL), row),
            pl.BlockSpec((tm, D_CONV), row),
            _heads_spec(N_SB_HEADS, tm),
            pl.BlockSpec((N_SB_HEADS, HEAD_DIM, tm), lambda i: (0, 0, i)),
            _heads_spec(N_RET_HEADS, tm),
        ],
        out_shape=[
            jax.ShapeDtypeStruct((S, D_MODEL), BF16),
            jax.ShapeDtypeStruct((S, D_CONV), F32),
            jax.ShapeDtypeStruct((N_SB_HEADS, S, HEAD_DIM), BF16),
            jax.ShapeDtypeStruct((N_SB_HEADS, HEAD_DIM, S), BF16),
            jax.ShapeDtypeStruct((N_RET_HEADS, S, HEAD_DIM), F32),
        ],
        compiler_params=_params(1),
    )(dy, w)


def _rows_from(x, start, n):
    return pltpu.roll(x, (x.shape[0] - start) % x.shape[0], 0)[:n]


def _conv_ln(ypre, ln_g, ln_b):
    mu = jnp.mean(ypre, axis=-1, keepdims=True)
    yc = ypre - mu
    rstd = lax.rsqrt(jnp.mean(yc * yc, axis=-1, keepdims=True) + EPS)
    yn = yc * rstd
    return yn, rstd, yn * ln_g + ln_b


def conv_fwd(proj, cw, cb, ln_g, ln_b, tm=CONV_TILE):
    S = proj.shape[0]
    hb = tm // CONV_HALO

    def body(a_ref, b_ref, ap_ref, bp_ref, cw_ref, cb_ref, g_ref, bb_ref, y_ref, ypre_ref, v_sc):
        i = pl.program_id(0)
        prev = ap_ref[...] * _sigmoid(bp_ref[...])
        v_sc[pl.ds(0, CONV_HALO), :] = jnp.where(i > 0, prev, 0.0)
        v_sc[pl.ds(CONV_HALO, tm), :] = a_ref[...] * _sigmoid(b_ref[...])
        vext = v_sc[...]
        acc = jnp.zeros((tm, D_CONV), F32)
        for j in range(CONV_WIDTH):
            acc = acc + cw_ref[pl.ds(j, 1), :] * _rows_from(vext, CONV_HALO - (CONV_WIDTH - 1) + j, tm)
        ypre = acc + cb_ref[...]
        ypre_ref[...] = ypre
        _, _, z = _conv_ln(ypre, g_ref[...], bb_ref[...])
        y_ref[...] = (z * _sigmoid(z)).astype(BF16)

    one = lambda i: (0, 0)
    return pl.pallas_call(
        body, name="conv_fwd",
        grid=(S // tm,),
        in_specs=[
            pl.BlockSpec((tm, D_CONV), lambda i: (i, 0)),
            pl.BlockSpec((tm, D_CONV), lambda i: (i, 1)),
            pl.BlockSpec((CONV_HALO, D_CONV), lambda i: (jnp.maximum(i * hb - 1, 0), 0)),
            pl.BlockSpec((CONV_HALO, D_CONV), lambda i: (jnp.maximum(i * hb - 1, 0), 1)),
            pl.BlockSpec((CONV_HALO, D_CONV), one),
            pl.BlockSpec((1, D_CONV), one),
            pl.BlockSpec((1, D_CONV), one),
            pl.BlockSpec((1, D_CONV), one),
        ],
        out_specs=[pl.BlockSpec((tm, D_CONV), lambda i: (i, 0)), pl.BlockSpec((tm, D_CONV), lambda i: (i, 0))],
        out_shape=[jax.ShapeDtypeStruct((S, D_CONV), BF16), jax.ShapeDtypeStruct((S, D_CONV), F32)],
        scratch_shapes=[pltpu.VMEM((tm + CONV_HALO, D_CONV), F32)],
        compiler_params=_params(1),
    )(proj, proj, proj, proj, cw, cb, ln_g, ln_b)


def conv_bwd(dyc, ypre, proj, cw, ln_g, ln_b, tm=CONV_TILE):
    S = ypre.shape[0]
    hb = tm // CONV_HALO
    nblk = S // tm
    last_halo = S // CONV_HALO - 1

    def dpre(dy, yp, g, bb):
        yn, rstd, z = _conv_ln(yp, g, bb)
        sg = _sigmoid(z)
        dz = dy * (sg * (1.0 + z * (1.0 - sg)))
        dyn = dz * g
        d = rstd * (dyn - jnp.mean(dyn, axis=-1, keepdims=True) - yn * jnp.mean(dyn * yn, axis=-1, keepdims=True))
        return d, dz * yn, dz

    def body(dy_ref, yp_ref, dyn_ref, ypn_ref, a_ref, b_ref, ap_ref, bp_ref, cw_ref, g_ref, bb_ref,
             du_ref, dcw_ref, dsm_ref, d_sc, v_sc):
        i = pl.program_id(0)
        g = g_ref[...]
        bb = bb_ref[...]
        d_main, dgn, dz = dpre(dy_ref[...], yp_ref[...], g, bb)
        d_next, _, _ = dpre(dyn_ref[...], ypn_ref[...], g, bb)
        d_sc[pl.ds(0, tm), :] = d_main
        d_sc[pl.ds(tm, CONV_HALO), :] = jnp.where(i < nblk - 1, d_next, 0.0)
        a = a_ref[...]
        sb = _sigmoid(b_ref[...])
        prev = ap_ref[...] * _sigmoid(bp_ref[...])
        v_sc[pl.ds(0, CONV_HALO), :] = jnp.where(i > 0, prev, 0.0)
        v_sc[pl.ds(CONV_HALO, tm), :] = a * sb

        @pl.when(i == 0)
        def _():
            dcw_ref[...] = jnp.zeros_like(dcw_ref)
            dsm_ref[...] = jnp.zeros_like(dsm_ref)

        dext = d_sc[...]
        vext = v_sc[...]
        dv = jnp.zeros((tm, D_CONV), F32)
        for j in range(CONV_WIDTH):
            dv = dv + cw_ref[pl.ds(j, 1), :] * _rows_from(dext, CONV_WIDTH - 1 - j, tm)
            shifted = _rows_from(vext, CONV_HALO - (CONV_WIDTH - 1) + j, tm)
            dcw_ref[pl.ds(j, 1), :] += jnp.sum(d_main * shifted, axis=0, keepdims=True)
        du_ref[:, pl.ds(0, D_CONV)] = dv * sb
        du_ref[:, pl.ds(D_CONV, D_CONV)] = dv * a * sb * (1.0 - sb)
        dsm_ref[pl.ds(0, 1), :] += jnp.sum(d_main, axis=0, keepdims=True)
        dsm_ref[pl.ds(1, 1), :] += jnp.sum(dgn, axis=0, keepdims=True)
        dsm_ref[pl.ds(2, 1), :] += jnp.sum(dz, axis=0, keepdims=True)

    one = lambda i: (0, 0)
    prev_map = lambda c: (lambda i: (jnp.maximum(i * hb - 1, 0), c))
    next_map = lambda i: (jnp.minimum((i + 1) * hb, last_halo), 0)
    return pl.pallas_call(
        body, name="conv_bwd",
        grid=(nblk,),
        in_specs=[
            pl.BlockSpec((tm, D_CONV), lambda i: (i, 0)),
            pl.BlockSpec((tm, D_CONV), lambda i: (i, 0)),
            pl.BlockSpec((CONV_HALO, D_CONV), next_map),
            pl.BlockSpec((CONV_HALO, D_CONV), next_map),
            pl.BlockSpec((tm, D_CONV), lambda i: (i, 0)),
            pl.BlockSpec((tm, D_CONV), lambda i: (i, 1)),
            pl.BlockSpec((CONV_HALO, D_CONV), prev_map(0)),
            pl.BlockSpec((CONV_HALO, D_CONV), prev_map(1)),
            pl.BlockSpec((CONV_HALO, D_CONV), one),
            pl.BlockSpec((1, D_CONV), one),
            pl.BlockSpec((1, D_CONV), one),
        ],
        out_specs=[
            pl.BlockSpec((tm, 2 * D_CONV), lambda i: (i, 0)),
            pl.BlockSpec((CONV_HALO, D_CONV), one),
            pl.BlockSpec((8, D_CONV), one),
        ],
        out_shape=[
            jax.ShapeDtypeStruct((S, 2 * D_CONV), F32),
            jax.ShapeDtypeStruct((CONV_HALO, D_CONV), F32),
            jax.ShapeDtypeStruct((8, D_CONV), F32),
        ],
        scratch_shapes=[pltpu.VMEM((tm + CONV_HALO, D_CONV), F32), pltpu.VMEM((tm + CONV_HALO, D_CONV), F32)],
        compiler_params=_params(1),
    )(dyc, ypre, dyc, ypre, proj, proj, proj, proj, cw, ln_g, ln_b)


SB_GROUP = 8


def _softplus(z):
    neg_abs = lax.bitcast_convert_type(lax.bitcast_convert_type(z, jnp.uint32) | jnp.uint32(0x80000000), F32)
    return jnp.maximum(z, 0.0) + jnp.log(1.0 + jnp.exp(neg_abs))


def _full_groups(n, body):
    def step(t, c):
        body(t * SB_GROUP)
        return c

    lax.fori_loop(0, lax.div(n, SB_GROUP), step, 0)


def _last_group(n, step, body):
    r = lax.rem(n, SB_GROUP)
    for k in range(0, SB_GROUP, step):
        @pl.when(r == k)
        def _(k=k):
            body(k)


def _rows(xs):
    return xs[0] if len(xs) == 1 else jnp.concatenate(xs, axis=0)


def sb_fwd(q, k, v, comm=None, T=SB_TILE, Q=SB_ROWS):
    H, S, dh = q.shape
    M = Q // T

    def body(q_ref, k_ref, v_ref, o_ref, tot_ref, acc_sc, car_sc):
        qb = pl.program_id(1)
        qv = q_ref[...]
        row = lax.broadcasted_iota(jnp.int32, (T, T), 0)
        col = lax.broadcasted_iota(jnp.int32, (T, T), 1)
        tri = jnp.where(row >= col, 1.0, 0.0).astype(BF16)
        qrow = lax.broadcasted_iota(jnp.int32, (Q, T), 0)
        kcol = lax.broadcasted_iota(jnp.int32, (Q, T), 1)
        causal = {d + 1: kcol + d * T < qrow for d in range(M)}
        acc_sc[...] = jnp.zeros_like(acc_sc)
        car_sc[...] = jnp.zeros_like(car_sc)

        def logits(kb, masked):
            ks = k_ref[pl.ds(pl.multiple_of(kb * T, T), T), :]
            z = _dot_nt(qv, ks)
            nb = _softplus(z)
            if masked:
                nb = jnp.where(causal[masked], nb, 0.0)
            return z, nb.astype(BF16)

        def group(kbs, diag):
            parts = [logits(kb, d) for kb, d in zip(kbs, diag)]
            pall = _dot(_rows([nb for _, nb in parts]), tri)
            carry = car_sc[...]
            out = None
            for j, kb in enumerate(kbs):
                p = pall[j * Q:(j + 1) * Q]
                vs = v_ref[pl.ds(pl.multiple_of(kb * T, T), T), :]
                w = jnp.exp((parts[j][0] - carry) - p)
                if diag[j]:
                    w = jnp.where(causal[diag[j]], w, 0.0)
                o = _dot(w.astype(BF16), vs)
                out = o if out is None else out + o
                carry = carry + p[:, 0:1]
            acc_sc[...] += out
            car_sc[...] = carry

        full = M * qb
        _last_group(full, M, lambda r: group([full + d for d in reversed(range(M))] + [full - 1 - o for o in range(r)],
                                             [d + 1 for d in reversed(range(M))] + [0] * r))
        rest = full - lax.rem(full, SB_GROUP)
        _full_groups(rest, lambda o: group([rest - 1 - o - j for j in range(SB_GROUP)], [0] * SB_GROUP))
        o_ref[...] = acc_sc[...]
        tot_ref[...] = car_sc[...]

    return _call(
        body, (q, k, v), comm, name="sb_fwd",
        grid=(H, S // Q),
        in_specs=[
            pl.BlockSpec((None, Q, dh), lambda h, i: (h, i, 0)),
            pl.BlockSpec((None, S, dh), lambda h, i: (h, 0, 0)),
            pl.BlockSpec((None, S, dh), lambda h, i: (h, 0, 0)),
        ],
        out_specs=[
            pl.BlockSpec((None, Q, dh), lambda h, i: (h, i, 0)),
            pl.BlockSpec((None, Q, 1), lambda h, i: (h, i, 0)),
        ],
        out_shape=[jax.ShapeDtypeStruct((H, S, dh), F32), jax.ShapeDtypeStruct((H, S, 1), F32)],
        scratch_shapes=[pltpu.VMEM((Q, dh), F32), pltpu.VMEM((Q, 1), F32)],
    )


def sb_bwd(q, k, v, do, qt, dot, tot, comm=None, T=SB_TILE, Q=SB_ROWS):
    H, S, dh = q.shape
    nt = S // T
    M = Q // T

    def body(q_ref, k_ref, v_ref, do_ref, qt_ref, dot_ref, tot_ref, dq_ref, dk_ref, dv_ref, acc_sc, rc_sc, gc_sc):
        qb = pl.program_id(1)
        qv = q_ref[...]
        dov = do_ref[...]
        qtv = qt_ref[...]
        dotv = dot_ref[...]
        row = lax.broadcasted_iota(jnp.int32, (T, T), 0)
        col = lax.broadcasted_iota(jnp.int32, (T, T), 1)
        before = jnp.where(row < col, 1.0, 0.0).astype(BF16)
        qrow = lax.broadcasted_iota(jnp.int32, (Q, T), 0)
        kcol = lax.broadcasted_iota(jnp.int32, (Q, T), 1)
        causal = {d + 1: kcol + d * T < qrow for d in range(M)}
        acc_sc[...] = jnp.zeros_like(acc_sc)
        rc_sc[...] = tot_ref[...]
        gc_sc[...] = jnp.zeros_like(gc_sc)

        @pl.when(qb == 0)
        def _():
            dk_ref[...] = jnp.zeros_like(dk_ref)
            dv_ref[...] = jnp.zeros_like(dv_ref)

        def first(kb, masked):
            start = pl.multiple_of(kb * T, T)
            z = _dot_nt(qv, k_ref[pl.ds(start, T), :])
            nb = _softplus(z)
            sig = jnp.exp(z - nb)
            if masked:
                nb = jnp.where(causal[masked], nb, 0.0)
            dw = _dot_nt(dov, v_ref[pl.ds(start, T), :])
            return z, sig, nb.astype(BF16), dw

        def group(kbs, diag):
            parts = [first(kb, d) for kb, d in zip(kbs, diag)]
            pall = _dot(_rows([p[2] for p in parts]), before)
            rc = rc_sc[...]
            ws, gs, ghs = [], [], []
            for j in range(len(kbs)):
                z, _, nbh, dw = parts[j]
                p = pall[j * Q:(j + 1) * Q]
                w = jnp.exp((z - rc) + p)
                rc = rc - (p[:, T - 1:T] + nbh[:, T - 1:T].astype(F32))
                if diag[j]:
                    w = jnp.where(causal[diag[j]], w, 0.0)
                g = dw * w
                ws.append(w.astype(BF16))
                gs.append(g)
                ghs.append(g.astype(BF16))
            glall = _dot(_rows(ghs), before)
            gc = gc_sc[...]
            dq = None
            for j, kb in enumerate(kbs):
                ks = k_ref[pl.ds(pl.multiple_of(kb * T, T), T), :]
                gl = glall[j * Q:(j + 1) * Q]
                dz = gs[j] - parts[j][1] * (gs[j] + (gl + gc))
                gc = gc + gl[:, T - 1:T] + ghs[j][:, T - 1:T].astype(F32)
                if diag[j]:
                    dz = jnp.where(causal[diag[j]], dz, 0.0)
                dzb = dz.astype(BF16)
                d = _dot(dzb, ks)
                dq = d if dq is None else dq + d
                dk_ref[kb] += _dot(qtv, dzb)
                dv_ref[kb] += _dot(dotv, ws[j])
            acc_sc[...] += dq
            rc_sc[...] = rc
            gc_sc[...] = gc

        full = M * qb
        _full_groups(full, lambda o: group([o + j for j in range(SB_GROUP)], [0] * SB_GROUP))
        rest = full - lax.rem(full, SB_GROUP)
        _last_group(full, M, lambda r: group([rest + j for j in range(r)] + [full + d for d in range(M)],
                                             [0] * r + [d + 1 for d in range(M)]))
        dq_ref[...] = acc_sc[...]

    return _call(
        body, (q, k, v, do, qt, dot, tot), comm, name="sb_bwd",
        grid=(H, S // Q),
        in_specs=[
            pl.BlockSpec((None, Q, dh), lambda h, i: (h, i, 0)),
            pl.BlockSpec((None, S, dh), lambda h, i: (h, 0, 0)),
            pl.BlockSpec((None, S, dh), lambda h, i: (h, 0, 0)),
            pl.BlockSpec((None, Q, dh), lambda h, i: (h, i, 0)),
            pl.BlockSpec((None, dh, Q), lambda h, i: (h, 0, i)),
            pl.BlockSpec((None, dh, Q), lambda h, i: (h, 0, i)),
            pl.BlockSpec((None, Q, 1), lambda h, i: (h, i, 0)),
        ],
        out_specs=[
            pl.BlockSpec((None, Q, dh), lambda h, i: (h, i, 0)),
            pl.BlockSpec((None, nt, dh, T), lambda h, i: (h, 0, 0, 0)),
            pl.BlockSpec((None, nt, dh, T), lambda h, i: (h, 0, 0, 0)),
        ],
        out_shape=[jax.ShapeDtypeStruct((H, S, dh), F32), jax.ShapeDtypeStruct((H, nt, dh, T), F32),
                   jax.ShapeDtypeStruct((H, nt, dh, T), F32)],
        scratch_shapes=[pltpu.VMEM((Q, dh), F32), pltpu.VMEM((Q, 1), F32), pltpu.VMEM((Q, 1), F32)],
    )


def _ret_tables(T=RET_TILE):
    hh = jnp.arange(N_RET_HEADS, dtype=F32)
    log_gamma = jnp.log1p(-jnp.exp2(-5.0 - hh))
    idx = jnp.arange(T, dtype=F32)
    diff = idx[:, None] - idx[None, :]
    ci = (jnp.arange(T) // 64)
    same = ci[:, None] == ci[None, :]
    earlier = ci[None, :] < ci[:, None]
    dist = jnp.where(same, jnp.abs(diff), diff)
    dmat = jnp.where(same | earlier, jnp.exp(log_gamma[:, None, None] * dist[None]), 0.0)
    ones = jnp.ones((1, 1, HEAD_DIM), F32)
    qdec = jnp.exp(log_gamma[:, None] * (idx + 1.0)[None, :])[:, :, None] * ones
    kdec = jnp.exp(log_gamma[:, None] * (T - 1.0 - idx)[None, :])[:, :, None] * ones
    bdec = jnp.exp(log_gamma * T)[:, None, None] * jnp.ones((1, HEAD_DIM, HEAD_DIM), F32)
    return dmat, qdec, kdec, bdec


def _rope_tables(S):
    half = HEAD_DIM // 2
    inv = 1.0 / (ROPE_BASE ** (jnp.arange(half, dtype=F32) / half))
    ang = jnp.arange(S).astype(F32)[:, None] * inv[None, :]
    c = jnp.cos(ang)
    s = jnp.sin(ang)
    cos = jnp.tile(jnp.concatenate([c, c], axis=1), (1, N_RET_HEADS))
    sin = jnp.tile(jnp.concatenate([-s, s], axis=1), (1, N_RET_HEADS))
    return cos, sin


def ret_fwd(q, k, v, gate, ng, tables, T=RET_TILE):
    H, S, dh = q.shape
    dmat, qdec, kdec, bdec = tables

    def body(q_ref, k_ref, v_ref, gt_ref, ng_ref, dm_ref, qd_ref, kd_ref, bd_ref, o_ref, y_ref, st_ref, s_sc):
        n = pl.program_id(1)

        @pl.when(n == 0)
        def _():
            s_sc[...] = jnp.zeros_like(s_sc)

        qv = q_ref[...]
        kv = k_ref[...]
        vv = v_ref[...]
        state = s_sc[...]
        st_ref[...] = state
        sc = (_dot_nt(qv, kv) * dm_ref[...]).astype(BF16)
        qd = (qv.astype(F32) * qd_ref[...]).astype(BF16)
        y = _dot(sc, vv) + _dot(qd, state.astype(BF16))
        y_ref[...] = y
        kd = (kv.astype(F32) * kd_ref[...]).astype(BF16)
        s_sc[...] = bd_ref[...] * state + _dot_tn(kd, vv)
        mu = jnp.mean(y, axis=-1, keepdims=True)
        yc = y - mu
        yn = yc * lax.rsqrt(jnp.mean(yc * yc, axis=-1, keepdims=True) + EPS)
        gt = gt_ref[...]
        o_ref[...] = gt * _sigmoid(gt) * (yn * ng_ref[...])

    blk = lambda h, n: (h, n, 0)
    head = lambda h, n: (h, 0, 0)
    return pl.pallas_call(
        body, name="ret_fwd",
        grid=(H, S // T),
        in_specs=[
            pl.BlockSpec((None, T, dh), blk),
            pl.BlockSpec((None, T, dh), blk),
            pl.BlockSpec((None, T, dh), blk),
            pl.BlockSpec((None, T, dh), blk),
            pl.BlockSpec((None, 1, dh), head),
            pl.BlockSpec((None, T, T), head),
            pl.BlockSpec((None, T, dh), head),
            pl.BlockSpec((None, T, dh), head),
            pl.BlockSpec((None, dh, dh), head),
        ],
        out_specs=[
            pl.BlockSpec((None, T, dh), blk),
            pl.BlockSpec((None, T, dh), blk),
            pl.BlockSpec((None, None, dh, dh), lambda h, n: (h, n, 0, 0)),
        ],
        out_shape=[
            jax.ShapeDtypeStruct((H, S, dh), F32),
            jax.ShapeDtypeStruct((H, S, dh), F32),
            jax.ShapeDtypeStruct((H, S // T, dh, dh), F32),
        ],
        scratch_shapes=[pltpu.VMEM((dh, dh), F32)],
        compiler_params=_params(2),
    )(q, k, v, gate, ng, dmat, qdec, kdec, bdec)


def ret_bwd(do, q, k, v, gate, ng, y, states, tables, T=RET_TILE):
    H, S, dh = q.shape
    nb = S // T
    dmat, qdec, kdec, bdec = tables

    def body(do_ref, q_ref, k_ref, v_ref, gt_ref, ng_ref, y_ref, st_ref, dm_ref, qd_ref, kd_ref, bd_ref,
             dq_ref, dk_ref, dv_ref, dgt_ref, dng_ref, u_sc):
        n = pl.program_id(1)

        @pl.when(n == 0)
        def _():
            u_sc[...] = jnp.zeros_like(u_sc)
            dng_ref[...] = jnp.zeros_like(dng_ref)

        yv = y_ref[...]
        mu = jnp.mean(yv, axis=-1, keepdims=True)
        yc = yv - mu
        rstd = lax.rsqrt(jnp.mean(yc * yc, axis=-1, keepdims=True) + EPS)
        yn = yc * rstd
        gt = gt_ref[...]
        sg = _sigmoid(gt)
        ngv = ng_ref[...]
        dout = do_ref[...]
        dgt_ref[...] = dout * (yn * ngv) * (sg * (1.0 + gt * (1.0 - sg)))
        dn = dout * (gt * sg)
        dng_ref[...] += jnp.sum(dn * yn, axis=0, keepdims=True)
        dyn = dn * ngv
        dy = rstd * (dyn - jnp.mean(dyn, axis=-1, keepdims=True) - yn * jnp.mean(dyn * yn, axis=-1, keepdims=True))
        dyb = dy.astype(BF16)

        qv = q_ref[...]
        kv = k_ref[...]
        vv = v_ref[...]
        dm = dm_ref[...]
        qdt = qd_ref[...]
        kdt = kd_ref[...]
        sb = st_ref[...].astype(BF16)
        u = u_sc[...]
        ub = u.astype(BF16)
        dqk = (_dot_nt(dyb, vv) * dm).astype(BF16)
        sc = (_dot_nt(qv, kv) * dm).astype(BF16)
        qd = (qv.astype(F32) * qdt).astype(BF16)
        kd = (kv.astype(F32) * kdt).astype(BF16)
        dq_ref[...] = _dot(dqk, kv) + qdt * _dot_nt(dyb, sb)
        dk_ref[...] = _dot_tn(dqk, qv) + kdt * _dot_nt(vv, ub)
        dv_ref[...] = _dot_tn(sc, dyb) + _dot(kd, ub)
        u_sc[...] = bd_ref[...] * u + _dot_tn(qd, dyb)

    blk = lambda h, n: (h, nb - 1 - n, 0)
    head = lambda h, n: (h, 0, 0)
    return pl.pallas_call(
        body, name="ret_bwd",
        grid=(H, nb),
        in_specs=[
            pl.BlockSpec((None, T, dh), blk),
            pl.BlockSpec((None, T, dh), blk),
            pl.BlockSpec((None, T, dh), blk),
            pl.BlockSpec((None, T, dh), blk),
            pl.BlockSpec((None, T, dh), blk),
            pl.BlockSpec((None, 1, dh), head),
            pl.BlockSpec((None, T, dh), blk),
            pl.BlockSpec((None, None, dh, dh), lambda h, n: (h, nb - 1 - n, 0, 0)),
            pl.BlockSpec((None, T, T), head),
            pl.BlockSpec((None, T, dh), head),
            pl.BlockSpec((None, T, dh), head),
            pl.BlockSpec((None, dh, dh), head),
        ],
        out_specs=[
            pl.BlockSpec((None, T, dh), blk),
            pl.BlockSpec((None, T, dh), blk),
            pl.BlockSpec((None, T, dh), blk),
            pl.BlockSpec((None, T, dh), blk),
            pl.BlockSpec((None, 1, dh), head),
        ],
        out_shape=[jax.ShapeDtypeStruct((H, S, dh), F32)] * 4 + [jax.ShapeDtypeStruct((H, 1, dh), F32)],
        scratch_shapes=[pltpu.VMEM((dh, dh), F32)],
        compiler_params=_params(2),
    )(do, q, k, v, gate, ng, y, states, dmat, qdec, kdec, bdec)


def loss_head(x, g, target, tm=ROW_TILE):
    S = x.shape[0]

    def body(x_ref, g_ref, t_ref, loss_ref, dx_ref, dg_ref):
        i = pl.program_id(0)
        xv = x_ref[...]
        gv = g_ref[...]
        _, xhat = _rms_stats(xv)
        err = xhat * gv - t_ref[...]
        part = 0.5 * jnp.sum(jnp.mean(err * err, axis=-1, keepdims=True), axis=0, keepdims=True)
        dx, _, dg = _rms_bwd(xv, gv, err * (1.0 / D_MODEL))
        dx_ref[...] = dx
        part = jnp.broadcast_to(part, (1, 128))

        @pl.when(i == 0)
        def _():
            loss_ref[...] = part
            dg_ref[...] = dg

        @pl.when(i > 0)
        def _():
            loss_ref[...] += part
            dg_ref[...] += dg

    row = lambda i: (i, 0)
    one = lambda i: (0, 0)
    return pl.pallas_call(
        body, name="loss_head",
        grid=(S // tm,),
        in_specs=[pl.BlockSpec((tm, D_MODEL), row), pl.BlockSpec((1, D_MODEL), one), pl.BlockSpec((tm, D_MODEL), row)],
        out_specs=[pl.BlockSpec((1, 128), one), pl.BlockSpec((tm, D_MODEL), row), pl.BlockSpec((1, D_MODEL), one)],
        out_shape=[
            jax.ShapeDtypeStruct((1, 128), F32),
            jax.ShapeDtypeStruct((S, D_MODEL), F32),
            jax.ShapeDtypeStruct((1, D_MODEL), F32),
        ],
        compiler_params=_params(1),
    )(x, g, target)


def adamw(parts, w, m, v, tr, transposed=False):
    L, R, C = w.shape
    nr = R // tr
    c1 = 1.0 / (1.0 - ADAM_B1 ** ADAM_STEP)
    c2 = 1.0 / (1.0 - ADAM_B2 ** ADAM_STEP)

    def body(*refs):
        p_refs = refs[:L]
        w_ref, m_ref, v_ref, g_ref, d_ref, mo_ref, vo_ref = refs[L:]
        l = pl.program_id(0)
        g = None
        for d in range(N_DEV):
            pd = p_refs[0][d].astype(F32)
            for ll in range(1, L):
                pd = jnp.where(l == ll, p_refs[ll][d].astype(F32), pd)
            g = pd if g is None else g + pd
        if transposed:
            g = g.T
        mn = ADAM_B1 * m_ref[...] + (1.0 - ADAM_B1) * g
        vn = ADAM_B2 * v_ref[...] + (1.0 - ADAM_B2) * (g * g)
        g_ref[...] = g
        mo_ref[...] = mn
        vo_ref[...] = vn
        d_ref[...] = -ADAM_LR * ((mn * c1) / (jnp.sqrt(vn * c2) + ADAM_EPS) + ADAM_WD * w_ref[...])

    def part_spec(ll):
        def block(l, i):
            return jnp.where(l == ll, i, jnp.where(l < ll, 0, nr - 1))
        if transposed:
            return pl.BlockSpec((N_DEV, C, tr), lambda l, i: (0, 0, block(l, i)))
        return pl.BlockSpec((N_DEV, tr, C), lambda l, i: (0, block(l, i), 0))

    blk = pl.BlockSpec((None, tr, C), lambda l, i: (l, i, 0))
    return pl.pallas_call(
        body, name="adamw",
        grid=(L, nr),
        in_specs=[part_spec(ll) for ll in range(L)] + [blk] * 3,
        out_specs=[blk] * 4,
        out_shape=[jax.ShapeDtypeStruct((L, R, C), F32)] * 4,
        compiler_params=_params(2),
    )(*parts, w, m, v)


def _my_id():
    return lax.axis_index("x") * 4 + lax.axis_index("y") * 2 + lax.axis_index("c")


def _peer(k):
    x, y, c = lax.axis_index("x"), lax.axis_index("y"), lax.axis_index("c")
    px = 1 - x if k & 4 else x
    py = 1 - y if k & 2 else y
    pc = 1 - c if k & 1 else c
    return (px, py, pc), px * 4 + py * 2 + pc


GATHER = "gather"
EXCHANGE = "exchange"


def _copies(kind, ins, outs, send_sems, recv_sems, local_sems, receive_side):
    me = _my_id()
    local, sends, recvs = [], [], []
    for t in range(len(ins)):
        src = ins[t] if kind == GATHER else ins[t].at[me]
        local.append(pltpu.make_async_copy(src, outs[t].at[me], local_sems.at[t]))
    for k in range(1, N_DEV):
        dev, pid = _peer(k)
        for t in range(len(ins)):
            sems = dict(send_sem=send_sems.at[t, k - 1], recv_sem=recv_sems.at[t, k - 1],
                        device_id=dev, device_id_type=pl.DeviceIdType.MESH)
            src = ins[t] if kind == GATHER else ins[t].at[pid]
            sends.append(pltpu.make_async_remote_copy(src_ref=src, dst_ref=outs[t].at[me], **sems))
            if receive_side:
                recvs.append(pltpu.make_async_remote_copy(src_ref=src, dst_ref=outs[t].at[pid], **sems))
    return local, sends, recvs


def _comm_start(kind, ins, outs, sems):
    local, sends, _ = _copies(kind, ins, outs, *sems, receive_side=False)
    for cp in local + sends:
        cp.start()


def _comm_wait(kind, ins, outs, sems):
    local, sends, recvs = _copies(kind, ins, outs, *sems, receive_side=True)
    for cp in recvs:
        cp.wait_recv()
    for cp in sends:
        cp.wait_send()
    for cp in local:
        cp.wait()


def _comm_shapes(kind, arrays):
    n = len(arrays)
    out_shape = [jax.ShapeDtypeStruct(((N_DEV,) + a.shape) if kind == GATHER else a.shape, a.dtype) for a in arrays]
    sems = [pltpu.SemaphoreType.DMA((n, N_DEV - 1)), pltpu.SemaphoreType.DMA((n, N_DEV - 1)),
            pltpu.SemaphoreType.DMA((n,))]
    return out_shape, sems


def communicate(kind, arrays):
    n = len(arrays)

    def body(*refs):
        ins, outs, sems = refs[:n], refs[n:2 * n], refs[2 * n:]
        _comm_start(kind, ins, outs, sems)
        _comm_wait(kind, ins, outs, sems)

    out_shape, sems = _comm_shapes(kind, arrays)
    any_spec = pl.BlockSpec(memory_space=pl.ANY)
    return pl.pallas_call(
        body, name=kind, in_specs=[any_spec] * n, out_specs=[any_spec] * n, out_shape=out_shape, scratch_shapes=sems,
    )(*arrays)


def gather_two_level(arrays):
    n = len(arrays)

    def body(*refs):
        ins, outs = refs[:n], refs[n:2 * n]
        send_sems, recv_sems, local_sems = refs[2 * n:]
        x, y, c = lax.axis_index("x"), lax.axis_index("y"), lax.axis_index("c")
        me, sibling = (x, y, c), (x, y, 1 - c)
        chips = [(1 - x, y), (x, 1 - y), (1 - x, 1 - y)]

        def slot(px, py, pc):
            return px * 4 + py * 2 + pc

        def copy(t, k, src, owner, to):
            return pltpu.make_async_remote_copy(
                src_ref=src, dst_ref=outs[t].at[slot(*owner)], send_sem=send_sems.at[t, k], recv_sem=recv_sems.at[t, k],
                device_id=to, device_id_type=pl.DeviceIdType.MESH)

        local = [pltpu.make_async_copy(ins[t], outs[t].at[slot(*me)], local_sems.at[t]) for t in range(n)]
        first = [copy(t, 0, ins[t], me, sibling) for t in range(n)]
        first += [copy(t, 1 + j, ins[t], me, (*chip, c)) for j, chip in enumerate(chips) for t in range(n)]
        for cp in local + first:
            cp.start()
        passed = []
        for j, chip in enumerate(chips):
            for t in range(n):
                copy(t, 1 + j, ins[t], (*chip, c), me).wait_recv()
                cp = copy(t, 4 + j, outs[t].at[slot(*chip, c)], (*chip, c), sibling)
                cp.start()
                passed.append(cp)
        for t in range(n):
            copy(t, 0, ins[t], sibling, me).wait_recv()
            for j, chip in enumerate(chips):
                copy(t, 4 + j, ins[t], (*chip, 1 - c), me).wait_recv()
        for cp in first + passed:
            cp.wait_send()
        for cp in local:
            cp.wait()

    out_shape, sems = _comm_shapes(GATHER, arrays)
    any_spec = pl.BlockSpec(memory_space=pl.ANY)
    return pl.pallas_call(
        body, name="gather_two_level", in_specs=[any_spec] * n, out_specs=[any_spec] * n, out_shape=out_shape,
        scratch_shapes=sems,
    )(*arrays)


def _call(body, operands, comm, *, name, grid, in_specs, out_specs, out_shape, scratch_shapes):
    if comm is None:
        outs = pl.pallas_call(body, name=name, grid=grid, in_specs=in_specs, out_specs=out_specs, out_shape=out_shape,
                              scratch_shapes=scratch_shapes, compiler_params=_params(len(grid)))(*operands)
        return outs, []
    kind, arrays = comm
    n, n_in, n_out, n_sc = len(arrays), len(in_specs), len(out_specs), len(scratch_shapes)

    def carrier(*refs):
        ins, cins = refs[:n_in], refs[n_in:n_in + n]
        refs = refs[n_in + n:]
        outs, couts = refs[:n_out], refs[n_out:n_out + n]
        scratch, sems = refs[n_out + n:n_out + n + n_sc], refs[n_out + n + n_sc:]
        steps = [pl.program_id(a) for a in range(len(grid))]
        first = functools.reduce(jnp.logical_and, [s == 0 for s in steps])
        last = functools.reduce(jnp.logical_and, [s == g - 1 for s, g in zip(steps, grid)])

        @pl.when(first)
        def _():
            _comm_start(kind, cins, couts, sems)

        body(*ins, *outs, *scratch)

        @pl.when(last)
        def _():
            _comm_wait(kind, cins, couts, sems)

    comm_shape, sems = _comm_shapes(kind, arrays)
    any_spec = pl.BlockSpec(memory_space=pl.ANY)
    outs = pl.pallas_call(
        carrier, name=f"{name}_{kind}", grid=grid,
        in_specs=list(in_specs) + [any_spec] * n,
        out_specs=list(out_specs) + [any_spec] * n,
        out_shape=list(out_shape) + comm_shape,
        scratch_shapes=list(scratch_shapes) + sems,
        compiler_params=_params(len(grid)),
    )(*operands, *arrays)
    return outs[:n_out], outs[n_out:]


def _row(v):
    return v.reshape(1, -1)


def _pad_taps(cw):
    return jnp.concatenate([cw, jnp.zeros((CONV_HALO - CONV_WIDTH, D_CONV), F32)], axis=0)


COL_SHARDED = ("ffn1_w_in", "mix_w_in", "ffn2_w_in")
ROW_SHARDED = ("ffn1_w_out", "mix_w_out", "ffn2_w_out")
SMALL = ("ffn1_norm", "mix_norm", "conv_b", "conv_ln_g", "conv_ln_b", "ret_norm_g", "ffn2_norm", "final_norm")
WEIGHTS = ("ffn1_norm", "ffn1_w_in", "ffn1_w_out", "mix_norm", "mix_w_in", "conv_w", "conv_b", "conv_ln_g",
           "conv_ln_b", "ret_norm_g", "mix_w_out", "ffn2_norm", "ffn2_w_in", "ffn2_w_out", "final_norm")
SMALL_ROWS = 32

FFN1 = ("ffn1_w_in", "ffn1_w_out")
MIX = ("mix_w_in", "mix_w_out")
FFN2 = ("ffn2_w_in", "ffn2_w_out")
STAGE_A = [(n, 0) for n in FFN1]
STAGE_B = [(n, 0) for n in MIX] + [("conv_w", None)]
STAGE_C = [(n, 0) for n in FFN2] + [(n, 1) for n in FFN1 + MIX + FFN2]
STAGE_D = [(n, 1) for n in FFN2]
STAGE_E = [(n, 1) for n in MIX + FFN1] + [(n, 0) for n in FFN2]
STAGE_F = [(n, 0) for n in MIX]
STAGE_G = [("ffn1_w_in", 0)]
STAGE_H = [("ffn1_w_out", 0)]


def _natural(name, got):
    if name == "conv_w":
        return got.transpose(1, 2, 0, 3).reshape(DEPTH, CONV_WIDTH, D_CONV)
    return got.reshape(-1, D_MODEL)


def _by_device(grad):
    return grad.reshape(N_DEV, -1, D_MODEL)


def _pack_small(g):
    flat = jnp.concatenate([g[n].reshape(-1) for n in SMALL] + [g["conv_w"].reshape(-1)])
    flat = jnp.concatenate([flat, jnp.zeros((SMALL_ROWS * D_MODEL - flat.shape[0],), F32)])
    return flat.reshape(SMALL_ROWS, D_MODEL)


def _unpack_small(buf, like):
    flat = buf.reshape(-1)
    out, off = {}, 0
    for n in SMALL:
        size = int(np.prod(like[n].shape))
        out[n] = flat[off:off + size].reshape(like[n].shape)
        off += size
    size = DEPTH * CONV_WIDTH * D_CONV
    out["conv_w"] = flat[off:off + size].reshape(DEPTH, CONV_WIDTH, D_CONV)
    return out


def kernel(x, ffn1_norm, ffn1_w_in, ffn1_w_out, mix_norm, mix_w_in, conv_w, conv_b, conv_ln_g, conv_ln_b, ret_norm_g, mix_w_out, ffn2_norm, ffn2_w_in, ffn2_w_out, final_norm, loss_target, m_ffn1_norm, m_ffn1_w_in, m_ffn1_w_out, m_mix_norm, m_mix_w_in, m_conv_w, m_conv_b, m_conv_ln_g, m_conv_ln_b, m_ret_norm_g, m_mix_w_out, m_ffn2_norm, m_ffn2_w_in, m_ffn2_w_out, m_final_norm, v_ffn1_norm, v_ffn1_w_in, v_ffn1_w_out, v_mix_norm, v_mix_w_in, v_conv_w, v_conv_b, v_conv_ln_g, v_conv_ln_b, v_ret_norm_g, v_mix_w_out, v_ffn2_norm, v_ffn2_w_in, v_ffn2_w_out, v_final_norm):
    args = locals()
    w = {n: args[n] for n in WEIGHTS}
    m = {n: args["m_" + n] for n in WEIGHTS}
    v = {n: args["v_" + n] for n in WEIGHTS}
    me = _my_id()
    x = x[0]
    target = loss_target[0]
    S = x.shape[0]
    cos, sin = _rope_tables(S)
    tables = _ret_tables()

    full = {}

    def shard(n, l):
        if n == "conv_w":
            return w[n]
        return (w[n][l].T if n in COL_SHARDED else w[n][l]).astype(BF16)

    def gather(keys):
        return GATHER, [shard(n, l) for n, l in keys]

    def gathered(keys, got):
        for (n, l), g in zip(keys, got):
            full[(n, l)] = _natural(n, g)

    gathered(STAGE_A, gather_two_level(gather(STAGE_A)[1]))

    saved = []
    for l in range(DEPTH):
        sv = {"x0": x}
        (x, sv["gate1"], sv["up1"]), got = ffn_fwd(x, _row(w["ffn1_norm"][l]), full[("ffn1_w_in", l)],
                                                   full[("ffn1_w_out", l)], gather(STAGE_B) if l == 0 else None)
        gathered(STAGE_B if l == 0 else [], got)
        sv["x1"] = x
        (sv["u"], sv["q_sb"], sv["k_sb"], sv["v_sb"], sv["qt_sb"], sv["q_r"], sv["k_r"], sv["v_r"],
         sv["g_r"]) = mix_in_fwd(x, _row(w["mix_norm"][l]), full[("mix_w_in", l)], cos, sin)
        cw = _pad_taps(full[("conv_w", None)][l])
        y_conv, sv["ypre"] = conv_fwd(sv["u"], cw, _row(w["conv_b"][l]), _row(w["conv_ln_g"][l]), _row(w["conv_ln_b"][l]))
        (o_sb, sv["tot"]), got = sb_fwd(sv["q_sb"], sv["k_sb"], sv["v_sb"], gather(STAGE_C) if l == 0 else None)
        gathered(STAGE_C if l == 0 else [], got)
        ng = w["ret_norm_g"][l].reshape(N_RET_HEADS, 1, HEAD_DIM)
        o_r, sv["y_r"], sv["states"] = ret_fwd(sv["q_r"], sv["k_r"], sv["v_r"], sv["g_r"], ng, tables)
        x, sv["ycat"] = mix_out_fwd(y_conv, o_sb, o_r, full[("mix_w_out", l)], x)
        sv["x2"] = x
        (x, sv["gate2"], sv["up2"]), _ = ffn_fwd(x, _row(w["ffn2_norm"][l]), full[("ffn2_w_in", l)],
                                                 full[("ffn2_w_out", l)])
        saved.append(sv)

    loss_acc, dx, dg_final = loss_head(x, _row(w["final_norm"]), target)
    loss = lax.psum(loss_acc[0, 0], ("x", "y", "c"))

    g = {"final_norm": dg_final.reshape(D_MODEL)}
    received = {}

    def exchange(keys, extra=(), dtype=F32):
        return EXCHANGE, [_by_device(g[(n, l)]).astype(dtype) for n, l in keys] + list(extra)

    def exchanged(keys, got):
        for key, p in zip(keys, got):
            received[key] = p

    def ffn_back(dx, x_in, gate, up, norm, names, l, comm=None):
        (dx, h, dyh, dgate, dup, hid, dg), got = ffn_bwd(dx, x_in, _row(norm), gate, up, full[(names[0], l)],
                                                         full[(names[1], l)], comm)
        g[(names[0], l)] = matmul_tn([dgate, dup], h, FF_TILE, D_MODEL, name="ffn_dw_in")
        if [(names[0], l)] == STAGE_G:
            g[(names[1], l)], got_g = matmul_tn([hid], dyh, FF_TILE, D_MODEL, name="ffn_dw_out",
                                                comm=exchange(STAGE_G, dtype=BF16))
            exchanged(STAGE_G, got_g)
        else:
            g[(names[1], l)] = matmul_tn([hid], dyh, FF_TILE, D_MODEL, name="ffn_dw_out")
        return dx, dg.reshape(D_MODEL), got

    for l in reversed(range(DEPTH)):
        sv = saved[l]
        dx, g[("ffn2_norm", l)], _ = ffn_back(dx, sv["x2"], sv["gate2"], sv["up2"], w["ffn2_norm"][l], FFN2, l)
        dxb, dy_conv, do_sb, dot_sb, do_r = mix_out_bwd(dx, full[("mix_w_out", l)])
        g[("mix_w_out", l)] = matmul_tn([sv["ycat"]], dxb, D_MODEL, D_MODEL, name="mix_dw_out")
        cw = _pad_taps(full[("conv_w", None)][l])
        du_conv, dcw, dsm = conv_bwd(dy_conv, sv["ypre"], sv["u"], cw, _row(w["conv_ln_g"][l]), _row(w["conv_ln_b"][l]))
        g[("conv_w", l)] = dcw[:CONV_WIDTH]
        g[("conv_b", l)], g[("conv_ln_g", l)], g[("conv_ln_b", l)] = dsm[0], dsm[1], dsm[2]
        stage = STAGE_D if l == DEPTH - 1 else STAGE_E
        (dq_sb, dk_t, dv_t), got = sb_bwd(sv["q_sb"], sv["k_sb"], sv["v_sb"], do_sb, sv["qt_sb"], dot_sb, sv["tot"],
                                          exchange(stage))
        exchanged(stage, got)
        ng = w["ret_norm_g"][l].reshape(N_RET_HEADS, 1, HEAD_DIM)
        dq_r, dk_r, dv_r, dg_r, dng = ret_bwd(do_r, sv["q_r"], sv["k_r"], sv["v_r"], sv["g_r"], ng, sv["y_r"],
                                              sv["states"], tables)
        g[("ret_norm_g", l)] = dng.reshape(D_RET)
        dx, h, dproj, dg = mix_in_bwd(du_conv, dq_sb, dk_t, dv_t, dq_r, dk_r, dv_r, dg_r, cos, sin,
                                      full[("mix_w_in", l)], sv["x1"], _row(w["mix_norm"][l]), dx)
        g[("mix_norm", l)] = dg.reshape(D_MODEL)
        g[("mix_w_in", l)] = matmul_tn([dproj], h, D_MODEL, D_MODEL, name="mix_dw_in")
        dx, g[("ffn1_norm", l)], got = ffn_back(dx, sv["x0"], sv["gate1"], sv["up1"], w["ffn1_norm"][l], FFN1, l,
                                                exchange(STAGE_F) if l == 0 else None)
        exchanged(STAGE_F if l == 0 else [], got)
    grad_x = dx

    small_names = [n for n in SMALL if n != "final_norm"] + ["conv_w"]
    gs = {n: jnp.stack([g[(n, l)] for l in range(DEPTH)], axis=0) for n in small_names}
    gs["final_norm"] = g["final_norm"]
    small = _pack_small(gs)
    got = communicate(*exchange(STAGE_H, [jnp.broadcast_to(small[None], (N_DEV, SMALL_ROWS, D_MODEL))], dtype=BF16))
    exchanged(STAGE_H, got[:-1])

    grad, delta, new_m, new_v = {}, {}, {}, {}
    for n in COL_SHARDED + ROW_SHARDED:
        rows = w[n].shape[1]
        col = n in COL_SHARDED
        grad[n], delta[n], new_m[n], new_v[n] = adamw([received[(n, l)] for l in range(DEPTH)], w[n], m[n], v[n],
                                                      tr=128 if col else rows // 2, transposed=col)

    def small_pack(d):
        mine = dict(d)
        cwf = jnp.zeros((DEPTH, CONV_WIDTH, D_CONV), F32)
        mine["conv_w"] = lax.dynamic_update_slice(cwf, d["conv_w"], (0, 0, me * (D_CONV // N_DEV)))
        return _pack_small(mine)

    outs = adamw([got[-1]], small_pack(w)[None], small_pack(m)[None], small_pack(v)[None], tr=SMALL_ROWS)
    for dst, o in zip((grad, delta, new_m, new_v), outs):
        un = _unpack_small(o[0], w)
        un["conv_w"] = lax.dynamic_slice(un["conv_w"], (0, 0, me * (D_CONV // N_DEV)),
                                         (DEPTH, CONV_WIDTH, D_CONV // N_DEV))
        dst.update(un)

    return (loss, grad_x[None], *[grad[n] for n in WEIGHTS], *[delta[n] for n in WEIGHTS],
            *[new_m[n] for n in WEIGHTS], *[new_v[n] for n in WEIGHTS])
```

```python
import functools

import numpy as np
import jax
import jax.numpy as jnp
from jax import lax
from jax.experimental import pallas as pl
from jax.experimental.pallas import tpu as pltpu

F32 = jnp.float32
BF16 = jnp.bfloat16

D_MODEL = 1024
DEPTH = 2
D_FF = 2816
D_CONV = 256
CONV_WIDTH = 31
CONV_HALO = 32
D_SB = 512
N_SB_HEADS = 8
D_RET = 256
N_RET_HEADS = 4
HEAD_DIM = 64
D_IN_PROJ = 3072
ROPE_BASE = 10000.0
EPS = 1e-6
N_DEV = 8

ADAM_LR = 0.001
ADAM_B1 = 0.9
ADAM_B2 = 0.999
ADAM_EPS = 1e-08
ADAM_WD = 0.01
ADAM_STEP = 10

VMEM_LIMIT = 56 * 1024 * 1024
ROW_TILE = 512
FF_TILE = 1408
SB_TILE = 256
SB_ROWS = 512
RET_TILE = 512
CONV_TILE = 256

NT_DIMS = (((1,), (1,)), ((), ()))
TN_DIMS = (((0,), (0,)), ((), ()))


def _params(n_axes, vmem=VMEM_LIMIT):
    return pltpu.CompilerParams(dimension_semantics=("arbitrary",) * n_axes, vmem_limit_bytes=vmem)


def _dot(a, b):
    return jnp.dot(a, b, preferred_element_type=F32)


def _dot_nt(a, b):
    return lax.dot_general(a, b, NT_DIMS, preferred_element_type=F32)


def _dot_tn(a, b):
    return lax.dot_general(a, b, TN_DIMS, preferred_element_type=F32)


def _sigmoid(z):
    return 1.0 / (1.0 + jnp.exp(-z))


def _rms_stats(xv):
    r = lax.rsqrt(jnp.mean(xv * xv, axis=-1, keepdims=True) + EPS)
    return r, xv * r


def _rms_bwd(xv, g, dh):
    r, xhat = _rms_stats(xv)
    dxhat = dh * g
    dx = r * (dxhat - xhat * jnp.mean(dxhat * xhat, axis=-1, keepdims=True))
    dg = jnp.sum(dh * xhat, axis=0, keepdims=True)
    return dx, (xhat * g).astype(BF16), dg


def ffn_fwd(x, g, w_in, w_out, comm=None, tm=ROW_TILE):
    S = x.shape[0]
    nj = D_FF // FF_TILE

    def body(x_ref, g_ref, wg_ref, wu_ref, wo_ref, y_ref, gate_ref, up_ref, h_sc, acc_sc):
        j = pl.program_id(1)

        @pl.when(j == 0)
        def _():
            _, xhat = _rms_stats(x_ref[...])
            h_sc[...] = (xhat * g_ref[...]).astype(BF16)
            acc_sc[...] = jnp.zeros_like(acc_sc)

        h = h_sc[...]
        gt = _dot_nt(h, wg_ref[...])
        up = _dot_nt(h, wu_ref[...])
        gate_ref[...] = gt.astype(BF16)
        up_ref[...] = up.astype(BF16)
        hid = (gt * _sigmoid(gt) * up).astype(BF16)
        acc_sc[...] += _dot(hid, wo_ref[...])

        @pl.when(j == nj - 1)
        def _():
            y_ref[...] = x_ref[...] + 0.5 * acc_sc[...]

    return _call(
        body, (x, g, w_in, w_in, w_out), comm, name="ffn_fwd",
        grid=(S // tm, nj),
        in_specs=[
            pl.BlockSpec((tm, D_MODEL), lambda i, j: (i, 0)),
            pl.BlockSpec((1, D_MODEL), lambda i, j: (0, 0)),
            pl.BlockSpec((FF_TILE, D_MODEL), lambda i, j: (j, 0)),
            pl.BlockSpec((FF_TILE, D_MODEL), lambda i, j: (j + nj, 0)),
            pl.BlockSpec((FF_TILE, D_MODEL), lambda i, j: (j, 0)),
        ],
        out_specs=[
            pl.BlockSpec((tm, D_MODEL), lambda i, j: (i, 0)),
            pl.BlockSpec((tm, FF_TILE), lambda i, j: (i, j)),
            pl.BlockSpec((tm, FF_TILE), lambda i, j: (i, j)),
        ],
        out_shape=[
            jax.ShapeDtypeStruct((S, D_MODEL), F32),
            jax.ShapeDtypeStruct((S, D_FF), BF16),
            jax.ShapeDtypeStruct((S, D_FF), BF16),
        ],
        scratch_shapes=[pltpu.VMEM((tm, D_MODEL), BF16), pltpu.VMEM((tm, D_MODEL), F32)],
    )


def ffn_bwd(dy, x, g, gate, up, w_in, w_out, comm=None, tm=ROW_TILE // 2):
    S = x.shape[0]
    nj = D_FF // FF_TILE

    def body(dy_ref, x_ref, g_ref, gate_ref, up_ref, w_ref, wo_ref,
             dx_ref, h_ref, dyh_ref, dgate_ref, dup_ref, hid_ref, dg_ref):
        i = pl.program_id(0)
        d2 = (0.5 * dy_ref[...]).astype(BF16)
        dyh_ref[...] = d2
        dh = None
        for j in range(nj):
            cols = pl.ds(j * FF_TILE, FF_TILE)
            dhid = _dot_nt(d2, wo_ref[cols, :])
            gt = gate_ref[:, cols].astype(F32)
            u = up_ref[:, cols].astype(F32)
            sig = _sigmoid(gt)
            sl = gt * sig
            dgate = (dhid * u * (sig * (1.0 + gt * (1.0 - sig)))).astype(BF16)
            dup = (dhid * sl).astype(BF16)
            dgate_ref[:, cols] = dgate
            dup_ref[:, cols] = dup
            hid_ref[:, cols] = (sl * u).astype(BF16)
            part = _dot(dgate, w_ref[cols, :]) + _dot(dup, w_ref[pl.ds(D_FF + j * FF_TILE, FF_TILE), :])
            dh = part if dh is None else dh + part
        dx, h, dg = _rms_bwd(x_ref[...], g_ref[...], dh)
        dx_ref[...] = dy_ref[...] + dx
        h_ref[...] = h

        @pl.when(i == 0)
        def _():
            dg_ref[...] = dg

        @pl.when(i > 0)
        def _():
            dg_ref[...] += dg

    row = lambda i: (i, 0)
    one = lambda i: (0, 0)
    resident = pl.Buffered(1)
    return _call(
        body, (dy, x, g, gate, up, w_in, w_out), comm, name="ffn_bwd",
        grid=(S // tm,),
        in_specs=[
            pl.BlockSpec((tm, D_MODEL), row),
            pl.BlockSpec((tm, D_MODEL), row),
            pl.BlockSpec((1, D_MODEL), one),
            pl.BlockSpec((tm, D_FF), row),
            pl.BlockSpec((tm, D_FF), row),
            pl.BlockSpec((2 * D_FF, D_MODEL), one, pipeline_mode=resident),
            pl.BlockSpec((D_FF, D_MODEL), one, pipeline_mode=resident),
        ],
        out_specs=[
            pl.BlockSpec((tm, D_MODEL), row),
            pl.BlockSpec((tm, D_MODEL), row),
            pl.BlockSpec((tm, D_MODEL), row),
            pl.BlockSpec((tm, D_FF), row),
            pl.BlockSpec((tm, D_FF), row),
            pl.BlockSpec((tm, D_FF), row),
            pl.BlockSpec((1, D_MODEL), one),
        ],
        out_shape=[
            jax.ShapeDtypeStruct((S, D_MODEL), F32),
            jax.ShapeDtypeStruct((S, D_MODEL), BF16),
            jax.ShapeDtypeStruct((S, D_MODEL), BF16),
            jax.ShapeDtypeStruct((S, D_FF), BF16),
            jax.ShapeDtypeStruct((S, D_FF), BF16),
            jax.ShapeDtypeStruct((S, D_FF), BF16),
            jax.ShapeDtypeStruct((1, D_MODEL), F32),
        ],
        scratch_shapes=[],
    )


def matmul_tn(a_list, b, ta, tn, tk=2 * ROW_TILE, name="matmul_tn", comm=None):
    S, ka = a_list[0].shape
    nb = b.shape[1]
    per = ka // ta

    def body(*refs):
        a_refs, b_ref, o_ref = refs[:-2], refs[-2], refs[-1]
        i = pl.program_id(0)
        k = pl.program_id(2)

        @pl.when(k == 0)
        def _():
            o_ref[...] = jnp.zeros_like(o_ref)

        for t, a_ref in enumerate(a_refs):
            @pl.when(lax.div(i, per) == t)
            def _(a_ref=a_ref):
                o_ref[...] += _dot_tn(a_ref[...], b_ref[...])

    def a_spec(t):
        def index(i, j, k):
            mine = lax.div(i, per) == t
            return jnp.where(mine, k, 0), jnp.where(mine, i - t * per, 0)
        return pl.BlockSpec((tk, ta), index)

    (out,), got = _call(
        body, (*a_list, b), comm, name=name,
        grid=(per * len(a_list), nb // tn, S // tk),
        in_specs=[a_spec(t) for t in range(len(a_list))] + [pl.BlockSpec((tk, tn), lambda i, j, k: (k, j))],
        out_specs=[pl.BlockSpec((ta, tn), lambda i, j, k: (i, j))],
        out_shape=[jax.ShapeDtypeStruct((ka * len(a_list), nb), F32)],
        scratch_shapes=[],
    )
    return (out, got) if comm is not None else out


SB_COLS = (2 * D_CONV, 2 * D_CONV + D_SB, 2 * D_CONV + 2 * D_SB)
RET_COLS = tuple(2 * D_CONV + 3 * D_SB + j * D_RET for j in range(4))


def _swap_halves(x):
    n = x.shape[1]
    lane = lax.broadcasted_iota(jnp.int32, x.shape, 1)
    first = (lane % HEAD_DIM) < (HEAD_DIM // 2)
    return jnp.where(first, pltpu.roll(x, n - HEAD_DIM // 2, 1), pltpu.roll(x, HEAD_DIM // 2, 1))


def _head(x, h):
    return x[:, h * HEAD_DIM:(h + 1) * HEAD_DIM]


def _heads_spec(n_heads, tm):
    return pl.BlockSpec((n_heads, tm, HEAD_DIM), lambda i: (0, i, 0))


def mix_in_fwd(x, g, w, cos, sin, tm=ROW_TILE):
    S = x.shape[0]

    def body(x_ref, g_ref, w_ref, c_ref, s_ref, u_ref, q_ref, k_ref, v_ref, qt_ref, qr_ref, kr_ref, vr_ref, gr_ref):
        _, xhat = _rms_stats(x_ref[...])
        proj = _dot_nt((xhat * g_ref[...]).astype(BF16), w_ref[...])
        u_ref[...] = proj[:, :2 * D_CONV]
        for h in range(N_SB_HEADS):
            q = (_head(proj[:, SB_COLS[0]:SB_COLS[1]], h) * 0.125).astype(BF16)
            q_ref[h] = q
            qt_ref[h] = q.T
            k_ref[h] = _head(proj[:, SB_COLS[1]:SB_COLS[2]], h).astype(BF16)
            v_ref[h] = _head(proj[:, SB_COLS[2]:RET_COLS[0]], h).astype(BF16)
        c = c_ref[...]
        s = s_ref[...]
        qv = proj[:, RET_COLS[0]:RET_COLS[1]]
        kv = proj[:, RET_COLS[1]:RET_COLS[2]]
        q_rot = ((qv * c + _swap_halves(qv) * s) * 0.125).astype(BF16)
        k_rot = (kv * c + _swap_halves(kv) * s).astype(BF16)
        for h in range(N_RET_HEADS):
            qr_ref[h] = _head(q_rot, h)
            kr_ref[h] = _head(k_rot, h)
            vr_ref[h] = _head(proj[:, RET_COLS[2]:RET_COLS[3]], h).astype(BF16)
            gr_ref[h] = _head(proj[:, RET_COLS[3]:], h)

    row = lambda i: (i, 0)
    one = lambda i: (0, 0)
    sb = jax.ShapeDtypeStruct((N_SB_HEADS, S, HEAD_DIM), BF16)
    ret = jax.ShapeDtypeStruct((N_RET_HEADS, S, HEAD_DIM), BF16)
    return pl.pallas_call(
        body, name="mix_in_fwd",
        grid=(S // tm,),
        in_specs=[
            pl.BlockSpec((tm, D_MODEL), row),
            pl.BlockSpec((1, D_MODEL), one),
            pl.BlockSpec((D_IN_PROJ, D_MODEL), one, pipeline_mode=pl.Buffered(1)),
            pl.BlockSpec((tm, D_RET), row),
            pl.BlockSpec((tm, D_RET), row),
        ],
        out_specs=[
            pl.BlockSpec((tm, 2 * D_CONV), row),
            _heads_spec(N_SB_HEADS, tm), _heads_spec(N_SB_HEADS, tm), _heads_spec(N_SB_HEADS, tm),
            pl.BlockSpec((N_SB_HEADS, HEAD_DIM, tm), lambda i: (0, 0, i)),
            _heads_spec(N_RET_HEADS, tm), _heads_spec(N_RET_HEADS, tm), _heads_spec(N_RET_HEADS, tm),
            _heads_spec(N_RET_HEADS, tm),
        ],
        out_shape=[
            jax.ShapeDtypeStruct((S, 2 * D_CONV), F32), sb, sb, sb,
            jax.ShapeDtypeStruct((N_SB_HEADS, HEAD_DIM, S), BF16),
            ret, ret, ret, jax.ShapeDtypeStruct((N_RET_HEADS, S, HEAD_DIM), F32),
        ],
        compiler_params=_params(1),
    )(x, g, w, cos, sin)


def mix_in_bwd(du, dq, dkt, dvt, dqr, dkr, dvr, dgr, cos, sin, w, x, g, dy, tm=SB_TILE):
    S = x.shape[0]
    assert dkt.shape[-1] == tm

    def body(du_ref, dq_ref, dkt_ref, dvt_ref, dqr_ref, dkr_ref, dvr_ref, dgr_ref, c_ref, s_ref, w_ref, x_ref, g_ref,
             dy_ref, dx_ref, h_ref, dp_ref, dg_ref):
        i = pl.program_id(0)
        sb_heads = range(N_SB_HEADS)
        ret_heads = range(N_RET_HEADS)
        c = c_ref[...]
        s = s_ref[...]
        dq_rot = jnp.concatenate([dqr_ref[h] for h in ret_heads], axis=1) * 0.125
        dk_rot = jnp.concatenate([dkr_ref[h] for h in ret_heads], axis=1)
        dproj = jnp.concatenate([
            du_ref[...].astype(BF16),
            jnp.concatenate([dq_ref[h] * 0.125 for h in sb_heads], axis=1).astype(BF16),
            jnp.concatenate([dkt_ref[h, 0].T for h in sb_heads], axis=1).astype(BF16),
            jnp.concatenate([dvt_ref[h, 0].T for h in sb_heads], axis=1).astype(BF16),
            (dq_rot * c - _swap_halves(dq_rot) * s).astype(BF16),
            (dk_rot * c - _swap_halves(dk_rot) * s).astype(BF16),
            jnp.concatenate([dvr_ref[h] for h in ret_heads], axis=1).astype(BF16),
            jnp.concatenate([dgr_ref[h] for h in ret_heads], axis=1).astype(BF16)], axis=1)
        dp_ref[...] = dproj
        dh = _dot(dproj, w_ref[...])
        dx, h, dg = _rms_bwd(x_ref[...], g_ref[...], dh)
        dx_ref[...] = dy_ref[...] + dx
        h_ref[...] = h

        @pl.when(i == 0)
        def _():
            dg_ref[...] = dg

        @pl.when(i > 0)
        def _():
            dg_ref[...] += dg

    row = lambda i: (i, 0)
    one = lambda i: (0, 0)
    tiles = pl.BlockSpec((N_SB_HEADS, 1, HEAD_DIM, tm), lambda i: (0, i, 0, 0))
    return pl.pallas_call(
        body, name="mix_in_bwd",
        grid=(S // tm,),
        in_specs=[
            pl.BlockSpec((tm, 2 * D_CONV), row),
            _heads_spec(N_SB_HEADS, tm), tiles, tiles,
            _heads_spec(N_RET_HEADS, tm), _heads_spec(N_RET_HEADS, tm), _heads_spec(N_RET_HEADS, tm),
            _heads_spec(N_RET_HEADS, tm),
            pl.BlockSpec((tm, D_RET), row),
            pl.BlockSpec((tm, D_RET), row),
            pl.BlockSpec((D_IN_PROJ, D_MODEL), one, pipeline_mode=pl.Buffered(1)),
            pl.BlockSpec((tm, D_MODEL), row),
            pl.BlockSpec((1, D_MODEL), one),
            pl.BlockSpec((tm, D_MODEL), row),
        ],
        out_specs=[
            pl.BlockSpec((tm, D_MODEL), row),
            pl.BlockSpec((tm, D_MODEL), row),
            pl.BlockSpec((tm, D_IN_PROJ), row),
            pl.BlockSpec((1, D_MODEL), one),
        ],
        out_shape=[
            jax.ShapeDtypeStruct((S, D_MODEL), F32),
            jax.ShapeDtypeStruct((S, D_MODEL), BF16),
            jax.ShapeDtypeStruct((S, D_IN_PROJ), BF16),
            jax.ShapeDtypeStruct((1, D_MODEL), F32),
        ],
        compiler_params=_params(1),
    )(du, dq, dkt, dvt, dqr, dkr, dvr, dgr, cos, sin, w, x, g, dy)


def mix_out_fwd(y_conv, o_sb, o_ret, w, x, tm=ROW_TILE):
    S = x.shape[0]

    def body(yc_ref, sb_ref, rt_ref, w_ref, x_ref, o_ref, ycat_ref):
        ycat = jnp.concatenate(
            [yc_ref[...]] + [sb_ref[h].astype(BF16) for h in range(N_SB_HEADS)]
            + [rt_ref[h].astype(BF16) for h in range(N_RET_HEADS)], axis=1)
        ycat_ref[...] = ycat
        o_ref[...] = x_ref[...] + _dot(ycat, w_ref[...])

    row = lambda i: (i, 0)
    return pl.pallas_call(
        body, name="mix_out_fwd",
        grid=(S // tm,),
        in_specs=[
            pl.BlockSpec((tm, D_CONV), row),
            _heads_spec(N_SB_HEADS, tm),
            _heads_spec(N_RET_HEADS, tm),
            pl.BlockSpec((D_MODEL, D_MODEL), lambda i: (0, 0)),
            pl.BlockSpec((tm, D_MODEL), row),
        ],
        out_specs=[pl.BlockSpec((tm, D_MODEL), row), pl.BlockSpec((tm, D_MODEL), row)],
        out_shape=[jax.ShapeDtypeStruct((S, D_MODEL), F32), jax.ShapeDtypeStruct((S, D_MODEL), BF16)],
        compiler_params=_params(1),
    )(y_conv, o_sb, o_ret, w, x)


def mix_out_bwd(dy, w, tm=ROW_TILE):
    S = dy.shape[0]

    def body(dy_ref, w_ref, dyb_ref, dc_ref, do_ref, dot_ref, dr_ref):
        d = dy_ref[...].astype(BF16)
        dyb_ref[...] = d
        dycat = _dot_nt(d, w_ref[...])
        dc_ref[...] = dycat[:, :D_CONV]
        for h in range(N_SB_HEADS):
            do = _head(dycat[:, D_CONV:D_CONV + D_SB], h).astype(BF16)
            do_ref[h] = do
            dot_ref[h] = do.T
        for h in range(N_RET_HEADS):
            dr_ref[h] = _head(dycat[:, D_CONV + D_SB:], h)

    row = lambda i: (i, 0)
    return pl.pallas_call(
        body, name="mix_out_bwd",
        grid=(S // tm,),
        in_specs=[
            pl.BlockSpec((tm, D_MODEL), row),
            pl.BlockSpec((D_MODEL, D_MODEL), lambda i: (0, 0)),
        ],
        out_specs=[
            pl.BlockSpec((tm, D_MODEL), row),
            pl.BlockSpec((tm, D_CONV), row),
            _heads_spec(N_SB_HEADS, tm),
            pl.BlockSpec((N_SB_HEADS, HEAD_DIM, tm), lambda i: (0, 0, i)),
            _heads_spec(N_RET_HEADS, tm),
        ],
        out_shape=[
            jax.ShapeDtypeStruct((S, D_MODEL), BF16),
            jax.ShapeDtypeStruct((S, D_CONV), F32),
            jax.ShapeDtypeStruct((N_SB_HEADS, S, HEAD_DIM), BF16),
            jax.ShapeDtypeStruct((N_SB_HEADS, HEAD_DIM, S), BF16),
            jax.ShapeDtypeStruct((N_RET_HEADS, S, HEAD_DIM), F32),
        ],
        compiler_params=_params(1),
    )(dy, w)


def _rows_from(x, start, n):
    return pltpu.roll(x, (x.shape[0] - start) % x.shape[0], 0)[:n]


def _conv_ln(ypre, ln_g, ln_b):
    mu = jnp.mean(ypre, axis=-1, keepdims=True)
    yc = ypre - mu
    rstd = lax.rsqrt(jnp.mean(yc * yc, axis=-1, keepdims=True) + EPS)
    yn = yc * rstd
    return yn, rstd, yn * ln_g + ln_b


def conv_fwd(proj, cw, cb, ln_g, ln_b, tm=CONV_TILE):
    S = proj.shape[0]
    hb = tm // CONV_HALO

    def body(a_ref, b_ref, ap_ref, bp_ref, cw_ref, cb_ref, g_ref, bb_ref, y_ref, ypre_ref, v_sc):
        i = pl.program_id(0)
        prev = ap_ref[...] * _sigmoid(bp_ref[...])
        v_sc[pl.ds(0, CONV_HALO), :] = jnp.where(i > 0, prev, 0.0)
        v_sc[pl.ds(CONV_HALO, tm), :] = a_ref[...] * _sigmoid(b_ref[...])
        vext = v_sc[...]
        acc = jnp.zeros((tm, D_CONV), F32)
        for j in range(CONV_WIDTH):
            acc = acc + cw_ref[pl.ds(j, 1), :] * _rows_from(vext, CONV_HALO - (CONV_WIDTH - 1) + j, tm)
        ypre = acc + cb_ref[...]
        ypre_ref[...] = ypre
        _, _, z = _conv_ln(ypre, g_ref[...], bb_ref[...])
        y_ref[...] = (z * _sigmoid(z)).astype(BF16)

    one = lambda i: (0, 0)
    return pl.pallas_call(
        body, name="conv_fwd",
        grid=(S // tm,),
        in_specs=[
            pl.BlockSpec((tm, D_CONV), lambda i: (i, 0)),
            pl.BlockSpec((tm, D_CONV), lambda i: (i, 1)),
            pl.BlockSpec((CONV_HALO, D_CONV), lambda i: (jnp.maximum(i * hb - 1, 0), 0)),
            pl.BlockSpec((CONV_HALO, D_CONV), lambda i: (jnp.maximum(i * hb - 1, 0), 1)),
            pl.BlockSpec((CONV_HALO, D_CONV), one),
            pl.BlockSpec((1, D_CONV), one),
            pl.BlockSpec((1, D_CONV), one),
            pl.BlockSpec((1, D_CONV), one),
        ],
        out_specs=[pl.BlockSpec((tm, D_CONV), lambda i: (i, 0)), pl.BlockSpec((tm, D_CONV), lambda i: (i, 0))],
        out_shape=[jax.ShapeDtypeStruct((S, D_CONV), BF16), jax.ShapeDtypeStruct((S, D_CONV), F32)],
        scratch_shapes=[pltpu.VMEM((tm + CONV_HALO, D_CONV), F32)],
        compiler_params=_params(1),
    )(proj, proj, proj, proj, cw, cb, ln_g, ln_b)


def conv_bwd(dyc, ypre, proj, cw, ln_g, ln_b, tm=CONV_TILE):
    S = ypre.shape[0]
    hb = tm // CONV_HALO
    nblk = S // tm
    last_halo = S // CONV_HALO - 1

    def dpre(dy, yp, g, bb):
        yn, rstd, z = _conv_ln(yp, g, bb)
        sg = _sigmoid(z)
        dz = dy * (sg * (1.0 + z * (1.0 - sg)))
        dyn = dz * g
        d = rstd * (dyn - jnp.mean(dyn, axis=-1, keepdims=True) - yn * jnp.mean(dyn * yn, axis=-1, keepdims=True))
        return d, dz * yn, dz

    def body(dy_ref, yp_ref, dyn_ref, ypn_ref, a_ref, b_ref, ap_ref, bp_ref, cw_ref, g_ref, bb_ref,
             du_ref, dcw_ref, dsm_ref, d_sc, v_sc):
        i = pl.program_id(0)
        g = g_ref[...]
        bb = bb_ref[...]
        d_main, dgn, dz = dpre(dy_ref[...], yp_ref[...], g, bb)
        d_next, _, _ = dpre(dyn_ref[...], ypn_ref[...], g, bb)
        d_sc[pl.ds(0, tm), :] = d_main
        d_sc[pl.ds(tm, CONV_HALO), :] = jnp.where(i < nblk - 1, d_next, 0.0)
        a = a_ref[...]
        sb = _sigmoid(b_ref[...])
        prev = ap_ref[...] * _sigmoid(bp_ref[...])
        v_sc[pl.ds(0, CONV_HALO), :] = jnp.where(i > 0, prev, 0.0)
        v_sc[pl.ds(CONV_HALO, tm), :] = a * sb

        @pl.when(i == 0)
        def _():
            dcw_ref[...] = jnp.zeros_like(dcw_ref)
            dsm_ref[...] = jnp.zeros_like(dsm_ref)

        dext = d_sc[...]
        vext = v_sc[...]
        dv = jnp.zeros((tm, D_CONV), F32)
        for j in range(CONV_WIDTH):
            dv = dv + cw_ref[pl.ds(j, 1), :] * _rows_from(dext, CONV_WIDTH - 1 - j, tm)
            shifted = _rows_from(vext, CONV_HALO - (CONV_WIDTH - 1) + j, tm)
            dcw_ref[pl.ds(j, 1), :] += jnp.sum(d_main * shifted, axis=0, keepdims=True)
        du_ref[:, pl.ds(0, D_CONV)] = dv * sb
        du_ref[:, pl.ds(D_CONV, D_CONV)] = dv * a * sb * (1.0 - sb)
        dsm_ref[pl.ds(0, 1), :] += jnp.sum(d_main, axis=0, keepdims=True)
        dsm_ref[pl.ds(1, 1), :] += jnp.sum(dgn, axis=0, keepdims=True)
        dsm_ref[pl.ds(2, 1), :] += jnp.sum(dz, axis=0, keepdims=True)

    one = lambda i: (0, 0)
    prev_map = lambda c: (lambda i: (jnp.maximum(i * hb - 1, 0), c))
    next_map = lambda i: (jnp.minimum((i + 1) * hb, last_halo), 0)
    return pl.pallas_call(
        body, name="conv_bwd",
        grid=(nblk,),
        in_specs=[
            pl.BlockSpec((tm, D_CONV), lambda i: (i, 0)),
            pl.BlockSpec((tm, D_CONV), lambda i: (i, 0)),
            pl.BlockSpec((CONV_HALO, D_CONV), next_map),
            pl.BlockSpec((CONV_HALO, D_CONV), next_map),
            pl.BlockSpec((tm, D_CONV), lambda i: (i, 0)),
            pl.BlockSpec((tm, D_CONV), lambda i: (i, 1)),
            pl.BlockSpec((CONV_HALO, D_CONV), prev_map(0)),
            pl.BlockSpec((CONV_HALO, D_CONV), prev_map(1)),
            pl.BlockSpec((CONV_HALO, D_CONV), one),
            pl.BlockSpec((1, D_CONV), one),
            pl.BlockSpec((1, D_CONV), one),
        ],
        out_specs=[
            pl.BlockSpec((tm, 2 * D_CONV), lambda i: (i, 0)),
            pl.BlockSpec((CONV_HALO, D_CONV), one),
            pl.BlockSpec((8, D_CONV), one),
        ],
        out_shape=[
            jax.ShapeDtypeStruct((S, 2 * D_CONV), F32),
            jax.ShapeDtypeStruct((CONV_HALO, D_CONV), F32),
            jax.ShapeDtypeStruct((8, D_CONV), F32),
        ],
        scratch_shapes=[pltpu.VMEM((tm + CONV_HALO, D_CONV), F32), pltpu.VMEM((tm + CONV_HALO, D_CONV), F32)],
        compiler_params=_params(1),
    )(dyc, ypre, dyc, ypre, proj, proj, proj, proj, cw, ln_g, ln_b)


SB_GROUP = 8


def _softplus(z):
    neg_abs = lax.bitcast_convert_type(lax.bitcast_convert_type(z, jnp.uint32) | jnp.uint32(0x80000000), F32)
    return jnp.maximum(z, 0.0) + jnp.log(1.0 + jnp.exp(neg_abs))


def _full_groups(n, body):
    def step(t, c):
        body(t * SB_GROUP)
        return c

    lax.fori_loop(0, lax.div(n, SB_GROUP), step, 0)


def _last_group(n, step, body):
    r = lax.rem(n, SB_GROUP)
    for k in range(0, SB_GROUP, step):
        @pl.when(r == k)
        def _(k=k):
            body(k)


def _rows(xs):
    return xs[0] if len(xs) == 1 else jnp.concatenate(xs, axis=0)


def sb_fwd(q, k, v, comm=None, T=SB_TILE, Q=SB_ROWS):
    H, S, dh = q.shape
    M = Q // T

    def body(q_ref, k_ref, v_ref, o_ref, tot_ref, acc_sc, car_sc):
        qb = pl.program_id(1)
        qv = q_ref[...]
        row = lax.broadcasted_iota(jnp.int32, (T, T), 0)
        col = lax.broadcasted_iota(jnp.int32, (T, T), 1)
        tri = jnp.where(row >= col, 1.0, 0.0).astype(BF16)
        qrow = lax.broadcasted_iota(jnp.int32, (Q, T), 0)
        kcol = lax.broadcasted_iota(jnp.int32, (Q, T), 1)
        causal = {d + 1: kcol + d * T < qrow for d in range(M)}
        acc_sc[...] = jnp.zeros_like(acc_sc)
        car_sc[...] = jnp.zeros_like(car_sc)

        def logits(kb, masked):
            ks = k_ref[pl.ds(pl.multiple_of(kb * T, T), T), :]
            z = _dot_nt(qv, ks)
            nb = _softplus(z)
            if masked:
                nb = jnp.where(causal[masked], nb, 0.0)
            return z, nb.astype(BF16)

        def group(kbs, diag):
            parts = [logits(kb, d) for kb, d in zip(kbs, diag)]
            pall = _dot(_rows([nb for _, nb in parts]), tri)
            carry = car_sc[...]
            out = None
            for j, kb in enumerate(kbs):
                p = pall[j * Q:(j + 1) * Q]
                vs = v_ref[pl.ds(pl.multiple_of(kb * T, T), T), :]
                w = jnp.exp((parts[j][0] - carry) - p)
                if diag[j]:
                    w = jnp.where(causal[diag[j]], w, 0.0)
                o = _dot(w.astype(BF16), vs)
                out = o if out is None else out + o
                carry = carry + p[:, 0:1]
            acc_sc[...] += out
            car_sc[...] = carry

        full = M * qb
        _last_group(full, M, lambda r: group([full + d for d in reversed(range(M))] + [full - 1 - o for o in range(r)],
                                             [d + 1 for d in reversed(range(M))] + [0] * r))
        rest = full - lax.rem(full, SB_GROUP)
        _full_groups(rest, lambda o: group([rest - 1 - o - j for j in range(SB_GROUP)], [0] * SB_GROUP))
        o_ref[...] = acc_sc[...]
        tot_ref[...] = car_sc[...]

    return _call(
        body, (q, k, v), comm, name="sb_fwd",
        grid=(H, S // Q),
        in_specs=[
            pl.BlockSpec((None, Q, dh), lambda h, i: (h, i, 0)),
            pl.BlockSpec((None, S, dh), lambda h, i: (h, 0, 0)),
            pl.BlockSpec((None, S, dh), lambda h, i: (h, 0, 0)),
        ],
        out_specs=[
            pl.BlockSpec((None, Q, dh), lambda h, i: (h, i, 0)),
            pl.BlockSpec((None, Q, 1), lambda h, i: (h, i, 0)),
        ],
        out_shape=[jax.ShapeDtypeStruct((H, S, dh), F32), jax.ShapeDtypeStruct((H, S, 1), F32)],
        scratch_shapes=[pltpu.VMEM((Q, dh), F32), pltpu.VMEM((Q, 1), F32)],
    )


def sb_bwd(q, k, v, do, qt, dot, tot, comm=None, T=SB_TILE, Q=SB_ROWS):
    H, S, dh = q.shape
    nt = S // T
    M = Q // T

    def body(q_ref, k_ref, v_ref, do_ref, qt_ref, dot_ref, tot_ref, dq_ref, dk_ref, dv_ref, acc_sc, rc_sc, gc_sc):
        qb = pl.program_id(1)
        qv = q_ref[...]
        dov = do_ref[...]
        qtv = qt_ref[...]
        dotv = dot_ref[...]
        row = lax.broadcasted_iota(jnp.int32, (T, T), 0)
        col = lax.broadcasted_iota(jnp.int32, (T, T), 1)
        before = jnp.where(row < col, 1.0, 0.0).astype(BF16)
        qrow = lax.broadcasted_iota(jnp.int32, (Q, T), 0)
        kcol = lax.broadcasted_iota(jnp.int32, (Q, T), 1)
        causal = {d + 1: kcol + d * T < qrow for d in range(M)}
        acc_sc[...] = jnp.zeros_like(acc_sc)
        rc_sc[...] = tot_ref[...]
        gc_sc[...] = jnp.zeros_like(gc_sc)

        @pl.when(qb == 0)
        def _():
            dk_ref[...] = jnp.zeros_like(dk_ref)
            dv_ref[...] = jnp.zeros_like(dv_ref)

        def first(kb, masked):
            start = pl.multiple_of(kb * T, T)
            z = _dot_nt(qv, k_ref[pl.ds(start, T), :])
            nb = _softplus(z)
            sig = jnp.exp(z - nb)
            if masked:
                nb = jnp.where(causal[masked], nb, 0.0)
            dw = _dot_nt(dov, v_ref[pl.ds(start, T), :])
            return z, sig, nb.astype(BF16), dw

        def group(kbs, diag):
            parts = [first(kb, d) for kb, d in zip(kbs, diag)]
            pall = _dot(_rows([p[2] for p in parts]), before)
            rc = rc_sc[...]
            ws, gs, ghs = [], [], []
            for j in range(len(kbs)):
                z, _, nbh, dw = parts[j]
                p = pall[j * Q:(j + 1) * Q]
                w = jnp.exp((z - rc) + p)
                rc = rc - (p[:, T - 1:T] + nbh[:, T - 1:T].astype(F32))
                if diag[j]:
                    w = jnp.where(causal[diag[j]], w, 0.0)
                g = dw * w
                ws.append(w.astype(BF16))
                gs.append(g)
                ghs.append(g.astype(BF16))
            glall = _dot(_rows(ghs), before)
            gc = gc_sc[...]
            dq = None
            for j, kb in enumerate(kbs):
                ks = k_ref[pl.ds(pl.multiple_of(kb * T, T), T), :]
                gl = glall[j * Q:(j + 1) * Q]
                dz = gs[j] - parts[j][1] * (gs[j] + (gl + gc))
                gc = gc + gl[:, T - 1:T] + ghs[j][:, T - 1:T].astype(F32)
                if diag[j]:
                    dz = jnp.where(causal[diag[j]], dz, 0.0)
                dzb = dz.astype(BF16)
                d = _dot(dzb, ks)
                dq = d if dq is None else dq + d
                dk_ref[kb] += _dot(qtv, dzb)
                dv_ref[kb] += _dot(dotv, ws[j])
            acc_sc[...] += dq
            rc_sc[...] = rc
            gc_sc[...] = gc

        full = M * qb
        _full_groups(full, lambda o: group([o + j for j in range(SB_GROUP)], [0] * SB_GROUP))
        rest = full - lax.rem(full, SB_GROUP)
        _last_group(full, M, lambda r: group([rest + j for j in range(r)] + [full + d for d in range(M)],
                                             [0] * r + [d + 1 for d in range(M)]))
        dq_ref[...] = acc_sc[...]

    return _call(
        body, (q, k, v, do, qt, dot, tot), comm, name="sb_bwd",
        grid=(H, S // Q),
        in_specs=[
            pl.BlockSpec((None, Q, dh), lambda h, i: (h, i, 0)),
            pl.BlockSpec((None, S, dh), lambda h, i: (h, 0, 0)),
            pl.BlockSpec((None, S, dh), lambda h, i: (h, 0, 0)),
            pl.BlockSpec((None, Q, dh), lambda h, i: (h, i, 0)),
            pl.BlockSpec((None, dh, Q), lambda h, i: (h, 0, i)),
            pl.BlockSpec((None, dh, Q), lambda h, i: (h, 0, i)),
            pl.BlockSpec((None, Q, 1), lambda h, i: (h, i, 0)),
        ],
        out_specs=[
            pl.BlockSpec((None, Q, dh), lambda h, i: (h, i, 0)),
            pl.BlockSpec((None, nt, dh, T), lambda h, i: (h, 0, 0, 0)),
            pl.BlockSpec((None, nt, dh, T), lambda h, i: (h, 0, 0, 0)),
        ],
        out_shape=[jax.ShapeDtypeStruct((H, S, dh), F32), jax.ShapeDtypeStruct((H, nt, dh, T), F32),
                   jax.ShapeDtypeStruct((H, nt, dh, T), F32)],
        scratch_shapes=[pltpu.VMEM((Q, dh), F32), pltpu.VMEM((Q, 1), F32), pltpu.VMEM((Q, 1), F32)],
    )


def _ret_tables(T=RET_TILE):
    hh = jnp.arange(N_RET_HEADS, dtype=F32)
    log_gamma = jnp.log1p(-jnp.exp2(-5.0 - hh))
    idx = jnp.arange(T, dtype=F32)
    diff = idx[:, None] - idx[None, :]
    ci = (jnp.arange(T) // 64)
    same = ci[:, None] == ci[None, :]
    earlier = ci[None, :] < ci[:, None]
    dist = jnp.where(same, jnp.abs(diff), diff)
    dmat = jnp.where(same | earlier, jnp.exp(log_gamma[:, None, None] * dist[None]), 0.0)
    ones = jnp.ones((1, 1, HEAD_DIM), F32)
    qdec = jnp.exp(log_gamma[:, None] * (idx + 1.0)[None, :])[:, :, None] * ones
    kdec = jnp.exp(log_gamma[:, None] * (T - 1.0 - idx)[None, :])[:, :, None] * ones
    bdec = jnp.exp(log_gamma * T)[:, None, None] * jnp.ones((1, HEAD_DIM, HEAD_DIM), F32)
    return dmat, qdec, kdec, bdec


def _rope_tables(S):
    half = HEAD_DIM // 2
    inv = 1.0 / (ROPE_BASE ** (jnp.arange(half, dtype=F32) / half))
    ang = jnp.arange(S).astype(F32)[:, None] * inv[None, :]
    c = jnp.cos(ang)
    s = jnp.sin(ang)
    cos = jnp.tile(jnp.concatenate([c, c], axis=1), (1, N_RET_HEADS))
    sin = jnp.tile(jnp.concatenate([-s, s], axis=1), (1, N_RET_HEADS))
    return cos, sin


def ret_fwd(q, k, v, gate, ng, tables, T=RET_TILE):
    H, S, dh = q.shape
    dmat, qdec, kdec, bdec = tables

    def body(q_ref, k_ref, v_ref, gt_ref, ng_ref, dm_ref, qd_ref, kd_ref, bd_ref, o_ref, y_ref, st_ref, s_sc):
        n = pl.program_id(1)

        @pl.when(n == 0)
        def _():
            s_sc[...] = jnp.zeros_like(s_sc)

        qv = q_ref[...]
        kv = k_ref[...]
        vv = v_ref[...]
        state = s_sc[...]
        st_ref[...] = state
        sc = (_dot_nt(qv, kv) * dm_ref[...]).astype(BF16)
        qd = (qv.astype(F32) * qd_ref[...]).astype(BF16)
        y = _dot(sc, vv) + _dot(qd, state.astype(BF16))
        y_ref[...] = y
        kd = (kv.astype(F32) * kd_ref[...]).astype(BF16)
        s_sc[...] = bd_ref[...] * state + _dot_tn(kd, vv)
        mu = jnp.mean(y, axis=-1, keepdims=True)
        yc = y - mu
        yn = yc * lax.rsqrt(jnp.mean(yc * yc, axis=-1, keepdims=True) + EPS)
        gt = gt_ref[...]
        o_ref[...] = gt * _sigmoid(gt) * (yn * ng_ref[...])

    blk = lambda h, n: (h, n, 0)
    head = lambda h, n: (h, 0, 0)
    return pl.pallas_call(
        body, name="ret_fwd",
        grid=(H, S // T),
        in_specs=[
            pl.BlockSpec((None, T, dh), blk),
            pl.BlockSpec((None, T, dh), blk),
            pl.BlockSpec((None, T, dh), blk),
            pl.BlockSpec((None, T, dh), blk),
            pl.BlockSpec((None, 1, dh), head),
            pl.BlockSpec((None, T, T), head),
            pl.BlockSpec((None, T, dh), head),
            pl.BlockSpec((None, T, dh), head),
            pl.BlockSpec((None, dh, dh), head),
        ],
        out_specs=[
            pl.BlockSpec((None, T, dh), blk),
            pl.BlockSpec((None, T, dh), blk),
            pl.BlockSpec((None, None, dh, dh), lambda h, n: (h, n, 0, 0)),
        ],
        out_shape=[
            jax.ShapeDtypeStruct((H, S, dh), F32),
            jax.ShapeDtypeStruct((H, S, dh), F32),
            jax.ShapeDtypeStruct((H, S // T, dh, dh), F32),
        ],
        scratch_shapes=[pltpu.VMEM((dh, dh), F32)],
        compiler_params=_params(2),
    )(q, k, v, gate, ng, dmat, qdec, kdec, bdec)


def ret_bwd(do, q, k, v, gate, ng, y, states, tables, T=RET_TILE):
    H, S, dh = q.shape
    nb = S // T
    dmat, qdec, kdec, bdec = tables

    def body(do_ref, q_ref, k_ref, v_ref, gt_ref, ng_ref, y_ref, st_ref, dm_ref, qd_ref, kd_ref, bd_ref,
             dq_ref, dk_ref, dv_ref, dgt_ref, dng_ref, u_sc):
        n = pl.program_id(1)

        @pl.when(n == 0)
        def _():
            u_sc[...] = jnp.zeros_like(u_sc)
            dng_ref[...] = jnp.zeros_like(dng_ref)

        yv = y_ref[...]
        mu = jnp.mean(yv, axis=-1, keepdims=True)
        yc = yv - mu
        rstd = lax.rsqrt(jnp.mean(yc * yc, axis=-1, keepdims=True) + EPS)
        yn = yc * rstd
        gt = gt_ref[...]
        sg = _sigmoid(gt)
        ngv = ng_ref[...]
        dout = do_ref[...]
        dgt_ref[...] = dout * (yn * ngv) * (sg * (1.0 + gt * (1.0 - sg)))
        dn = dout * (gt * sg)
        dng_ref[...] += jnp.sum(dn * yn, axis=0, keepdims=True)
        dyn = dn * ngv
        dy = rstd * (dyn - jnp.mean(dyn, axis=-1, keepdims=True) - yn * jnp.mean(dyn * yn, axis=-1, keepdims=True))
        dyb = dy.astype(BF16)

        qv = q_ref[...]
        kv = k_ref[...]
        vv = v_ref[...]
        dm = dm_ref[...]
        qdt = qd_ref[...]
        kdt = kd_ref[...]
        sb = st_ref[...].astype(BF16)
        u = u_sc[...]
        ub = u.astype(BF16)
        dqk = (_dot_nt(dyb, vv) * dm).astype(BF16)
        sc = (_dot_nt(qv, kv) * dm).astype(BF16)
        qd = (qv.astype(F32) * qdt).astype(BF16)
        kd = (kv.astype(F32) * kdt).astype(BF16)
        dq_ref[...] = _dot(dqk, kv) + qdt * _dot_nt(dyb, sb)
        dk_ref[...] = _dot_tn(dqk, qv) + kdt * _dot_nt(vv, ub)
        dv_ref[...] = _dot_tn(sc, dyb) + _dot(kd, ub)
        u_sc[...] = bd_ref[...] * u + _dot_tn(qd, dyb)

    blk = lambda h, n: (h, nb - 1 - n, 0)
    head = lambda h, n: (h, 0, 0)
    return pl.pallas_call(
        body, name="ret_bwd",
        grid=(H, nb),
        in_specs=[
            pl.BlockSpec((None, T, dh), blk),
            pl.BlockSpec((None, T, dh), blk),
            pl.BlockSpec((None, T, dh), blk),
            pl.BlockSpec((None, T, dh), blk),
            pl.BlockSpec((None, T, dh), blk),
            pl.BlockSpec((None, 1, dh), head),
            pl.BlockSpec((None, T, dh), blk),
            pl.BlockSpec((None, None, dh, dh), lambda h, n: (h, nb - 1 - n, 0, 0)),
            pl.BlockSpec((None, T, T), head),
            pl.BlockSpec((None, T, dh), head),
            pl.BlockSpec((None, T, dh), head),
            pl.BlockSpec((None, dh, dh), head),
        ],
        out_specs=[
            pl.BlockSpec((None, T, dh), blk),
            pl.BlockSpec((None, T, dh), blk),
            pl.BlockSpec((None, T, dh), blk),
            pl.BlockSpec((None, T, dh), blk),
            pl.BlockSpec((None, 1, dh), head),
        ],
        out_shape=[jax.ShapeDtypeStruct((H, S, dh), F32)] * 4 + [jax.ShapeDtypeStruct((H, 1, dh), F32)],
        scratch_shapes=[pltpu.VMEM((dh, dh), F32)],
        compiler_params=_params(2),
    )(do, q, k, v, gate, ng, y, states, dmat, qdec, kdec, bdec)


def loss_head(x, g, target, tm=ROW_TILE):
    S = x.shape[0]

    def body(x_ref, g_ref, t_ref, loss_ref, dx_ref, dg_ref):
        i = pl.program_id(0)
        xv = x_ref[...]
        gv = g_ref[...]
        _, xhat = _rms_stats(xv)
        err = xhat * gv - t_ref[...]
        part = 0.5 * jnp.sum(jnp.mean(err * err, axis=-1, keepdims=True), axis=0, keepdims=True)
        dx, _, dg = _rms_bwd(xv, gv, err * (1.0 / D_MODEL))
        dx_ref[...] = dx
        part = jnp.broadcast_to(part, (1, 128))

        @pl.when(i == 0)
        def _():
            loss_ref[...] = part
            dg_ref[...] = dg

        @pl.when(i > 0)
        def _():
            loss_ref[...] += part
            dg_ref[...] += dg

    row = lambda i: (i, 0)
    one = lambda i: (0, 0)
    return pl.pallas_call(
        body, name="loss_head",
        grid=(S // tm,),
        in_specs=[pl.BlockSpec((tm, D_MODEL), row), pl.BlockSpec((1, D_MODEL), one), pl.BlockSpec((tm, D_MODEL), row)],
        out_specs=[pl.BlockSpec((1, 128), one), pl.BlockSpec((tm, D_MODEL), row), pl.BlockSpec((1, D_MODEL), one)],
        out_shape=[
            jax.ShapeDtypeStruct((1, 128), F32),
            jax.ShapeDtypeStruct((S, D_MODEL), F32),
            jax.ShapeDtypeStruct((1, D_MODEL), F32),
        ],
        compiler_params=_params(1),
    )(x, g, target)


def adamw(parts, w, m, v, tr, transposed=False):
    L, R, C = w.shape
    nr = R // tr
    c1 = 1.0 / (1.0 - ADAM_B1 ** ADAM_STEP)
    c2 = 1.0 / (1.0 - ADAM_B2 ** ADAM_STEP)

    def body(*refs):
        p_refs = refs[:L]
        w_ref, m_ref, v_ref, g_ref, d_ref, mo_ref, vo_ref = refs[L:]
        l = pl.program_id(0)
        g = None
        for d in range(N_DEV):
            pd = p_refs[0][d].astype(F32)
            for ll in range(1, L):
                pd = jnp.where(l == ll, p_refs[ll][d].astype(F32), pd)
            g = pd if g is None else g + pd
        if transposed:
            g = g.T
        mn = ADAM_B1 * m_ref[...] + (1.0 - ADAM_B1) * g
        vn = ADAM_B2 * v_ref[...] + (1.0 - ADAM_B2) * (g * g)
        g_ref[...] = g
        mo_ref[...] = mn
        vo_ref[...] = vn
        d_ref[...] = -ADAM_LR * ((mn * c1) / (jnp.sqrt(vn * c2) + ADAM_EPS) + ADAM_WD * w_ref[...])

    def part_spec(ll):
        def block(l, i):
            return jnp.where(l == ll, i, jnp.where(l < ll, 0, nr - 1))
        if transposed:
            return pl.BlockSpec((N_DEV, C, tr), lambda l, i: (0, 0, block(l, i)))
        return pl.BlockSpec((N_DEV, tr, C), lambda l, i: (0, block(l, i), 0))

    blk = pl.BlockSpec((None, tr, C), lambda l, i: (l, i, 0))
    return pl.pallas_call(
        body, name="adamw",
        grid=(L, nr),
        in_specs=[part_spec(ll) for ll in range(L)] + [blk] * 3,
        out_specs=[blk] * 4,
        out_shape=[jax.ShapeDtypeStruct((L, R, C), F32)] * 4,
        compiler_params=_params(2),
    )(*parts, w, m, v)


def _my_id():
    return lax.axis_index("x") * 4 + lax.axis_index("y") * 2 + lax.axis_index("c")


def _peer(k):
    x, y, c = lax.axis_index("x"), lax.axis_index("y"), lax.axis_index("c")
    px = 1 - x if k & 4 else x
    py = 1 - y if k & 2 else y
    pc = 1 - c if k & 1 else c
    return (px, py, pc), px * 4 + py * 2 + pc


GATHER = "gather"
EXCHANGE = "exchange"


def _copies(kind, ins, outs, send_sems, recv_sems, local_sems, receive_side):
    me = _my_id()
    local, sends, recvs = [], [], []
    for t in range(len(ins)):
        src = ins[t] if kind == GATHER else ins[t].at[me]
        local.append(pltpu.make_async_copy(src, outs[t].at[me], local_sems.at[t]))
    for k in range(1, N_DEV):
        dev, pid = _peer(k)
        for t in range(len(ins)):
            sems = dict(send_sem=send_sems.at[t, k - 1], recv_sem=recv_sems.at[t, k - 1],
                        device_id=dev, device_id_type=pl.DeviceIdType.MESH)
            src = ins[t] if kind == GATHER else ins[t].at[pid]
            sends.append(pltpu.make_async_remote_copy(src_ref=src, dst_ref=outs[t].at[me], **sems))
            if receive_side:
                recvs.append(pltpu.make_async_remote_copy(src_ref=src, dst_ref=outs[t].at[pid], **sems))
    return local, sends, recvs


def _comm_start(kind, ins, outs, sems):
    local, sends, _ = _copies(kind, ins, outs, *sems, receive_side=False)
    for cp in local + sends:
        cp.start()


def _comm_wait(kind, ins, outs, sems):
    local, sends, recvs = _copies(kind, ins, outs, *sems, receive_side=True)
    for cp in recvs:
        cp.wait_recv()
    for cp in sends:
        cp.wait_send()
    for cp in local:
        cp.wait()


def _comm_shapes(kind, arrays):
    n = len(arrays)
    out_shape = [jax.ShapeDtypeStruct(((N_DEV,) + a.shape) if kind == GATHER else a.shape, a.dtype) for a in arrays]
    sems = [pltpu.SemaphoreType.DMA((n, N_DEV - 1)), pltpu.SemaphoreType.DMA((n, N_DEV - 1)),
            pltpu.SemaphoreType.DMA((n,))]
    return out_shape, sems


def communicate(kind, arrays):
    n = len(arrays)

    def body(*refs):
        ins, outs, sems = refs[:n], refs[n:2 * n], refs[2 * n:]
        _comm_start(kind, ins, outs, sems)
        _comm_wait(kind, ins, outs, sems)

    out_shape, sems = _comm_shapes(kind, arrays)
    any_spec = pl.BlockSpec(memory_space=pl.ANY)
    return pl.pallas_call(
        body, name=kind, in_specs=[any_spec] * n, out_specs=[any_spec] * n, out_shape=out_shape, scratch_shapes=sems,
    )(*arrays)


def gather_two_level(arrays):
    n = len(arrays)

    def body(*refs):
        ins, outs = refs[:n], refs[n:2 * n]
        send_sems, recv_sems, local_sems = refs[2 * n:]
        x, y, c = lax.axis_index("x"), lax.axis_index("y"), lax.axis_index("c")
        me, sibling = (x, y, c), (x, y, 1 - c)
        chips = [(1 - x, y), (x, 1 - y), (1 - x, 1 - y)]

        def slot(px, py, pc):
            return px * 4 + py * 2 + pc

        def copy(t, k, src, owner, to):
            return pltpu.make_async_remote_copy(
                src_ref=src, dst_ref=outs[t].at[slot(*owner)], send_sem=send_sems.at[t, k], recv_sem=recv_sems.at[t, k],
                device_id=to, device_id_type=pl.DeviceIdType.MESH)

        local = [pltpu.make_async_copy(ins[t], outs[t].at[slot(*me)], local_sems.at[t]) for t in range(n)]
        first = [copy(t, 0, ins[t], me, sibling) for t in range(n)]
        first += [copy(t, 1 + j, ins[t], me, (*chip, c)) for j, chip in enumerate(chips) for t in range(n)]
        for cp in local + first:
            cp.start()
        passed = []
        for j, chip in enumerate(chips):
            for t in range(n):
                copy(t, 1 + j, ins[t], (*chip, c), me).wait_recv()
                cp = copy(t, 4 + j, outs[t].at[slot(*chip, c)], (*chip, c), sibling)
                cp.start()
                passed.append(cp)
        for t in range(n):
            copy(t, 0, ins[t], sibling, me).wait_recv()
            for j, chip in enumerate(chips):
                copy(t, 4 + j, ins[t], (*chip, 1 - c), me).wait_recv()
        for cp in first + passed:
            cp.wait_send()
        for cp in local:
            cp.wait()

    out_shape, sems = _comm_shapes(GATHER, arrays)
    any_spec = pl.BlockSpec(memory_space=pl.ANY)
    return pl.pallas_call(
        body, name="gather_two_level", in_specs=[any_spec] * n, out_specs=[any_spec] * n, out_shape=out_shape,
        scratch_shapes=sems,
    )(*arrays)


def _call(body, operands, comm, *, name, grid, in_specs, out_specs, out_shape, scratch_shapes):
    if comm is None:
        outs = pl.pallas_call(body, name=name, grid=grid, in_specs=in_specs, out_specs=out_specs, out_shape=out_shape,
                              scratch_shapes=scratch_shapes, compiler_params=_params(len(grid)))(*operands)
        return outs, []
    kind, arrays = comm
    n, n_in, n_out, n_sc = len(arrays), len(in_specs), len(out_specs), len(scratch_shapes)

    def carrier(*refs):
        ins, cins = refs[:n_in], refs[n_in:n_in + n]
        refs = refs[n_in + n:]
        outs, couts = refs[:n_out], refs[n_out:n_out + n]
        scratch, sems = refs[n_out + n:n_out + n + n_sc], refs[n_out + n + n_sc:]
        steps = [pl.program_id(a) for a in range(len(grid))]
        first = functools.reduce(jnp.logical_and, [s == 0 for s in steps])
        last = functools.reduce(jnp.logical_and, [s == g - 1 for s, g in zip(steps, grid)])

        @pl.when(first)
        def _():
            _comm_start(kind, cins, couts, sems)

        body(*ins, *outs, *scratch)

        @pl.when(last)
        def _():
            _comm_wait(kind, cins, couts, sems)

    comm_shape, sems = _comm_shapes(kind, arrays)
    any_spec = pl.BlockSpec(memory_space=pl.ANY)
    outs = pl.pallas_call(
        carrier, name=f"{name}_{kind}", grid=grid,
        in_specs=list(in_specs) + [any_spec] * n,
        out_specs=list(out_specs) + [any_spec] * n,
        out_shape=list(out_shape) + comm_shape,
        scratch_shapes=list(scratch_shapes) + sems,
        compiler_params=_params(len(grid)),
    )(*operands, *arrays)
    return outs[:n_out], outs[n_out:]


def _row(v):
    return v.reshape(1, -1)


def _pad_taps(cw):
    return jnp.concatenate([cw, jnp.zeros((CONV_HALO - CONV_WIDTH, D_CONV), F32)], axis=0)


COL_SHARDED = ("ffn1_w_in", "mix_w_in", "ffn2_w_in")
ROW_SHARDED = ("ffn1_w_out", "mix_w_out", "ffn2_w_out")
SMALL = ("ffn1_norm", "mix_norm", "conv_b", "conv_ln_g", "conv_ln_b", "ret_norm_g", "ffn2_norm", "final_norm")
WEIGHTS = ("ffn1_norm", "ffn1_w_in", "ffn1_w_out", "mix_norm", "mix_w_in", "conv_w", "conv_b", "conv_ln_g",
           "conv_ln_b", "ret_norm_g", "mix_w_out", "ffn2_norm", "ffn2_w_in", "ffn2_w_out", "final_norm")
SMALL_ROWS = 32

FFN1 = ("ffn1_w_in", "ffn1_w_out")
MIX = ("mix_w_in", "mix_w_out")
FFN2 = ("ffn2_w_in", "ffn2_w_out")
STAGE_A = [(n, 0) for n in FFN1]
STAGE_B = [(n, 0) for n in MIX] + [("conv_w", None)]
STAGE_C = [(n, 0) for n in FFN2] + [(n, 1) for n in FFN1 + MIX + FFN2]
STAGE_D = [(n, 1) for n in FFN2]
STAGE_E = [(n, 1) for n in MIX + FFN1] + [(n, 0) for n in FFN2]
STAGE_F = [(n, 0) for n in MIX]
STAGE_G = [("ffn1_w_in", 0)]
STAGE_H = [("ffn1_w_out", 0)]


def _natural(name, got):
    if name == "conv_w":
        return got.transpose(1, 2, 0, 3).reshape(DEPTH, CONV_WIDTH, D_CONV)
    return got.reshape(-1, D_MODEL)


def _by_device(grad):
    return grad.reshape(N_DEV, -1, D_MODEL)


def _pack_small(g):
    flat = jnp.concatenate([g[n].reshape(-1) for n in SMALL] + [g["conv_w"].reshape(-1)])
    flat = jnp.concatenate([flat, jnp.zeros((SMALL_ROWS * D_MODEL - flat.shape[0],), F32)])
    return flat.reshape(SMALL_ROWS, D_MODEL)


def _unpack_small(buf, like):
    flat = buf.reshape(-1)
    out, off = {}, 0
    for n in SMALL:
        size = int(np.prod(like[n].shape))
        out[n] = flat[off:off + size].reshape(like[n].shape)
        off += size
    size = DEPTH * CONV_WIDTH * D_CONV
    out["conv_w"] = flat[off:off + size].reshape(DEPTH, CONV_WIDTH, D_CONV)
    return out


def kernel(x, ffn1_norm, ffn1_w_in, ffn1_w_out, mix_norm, mix_w_in, conv_w, conv_b, conv_ln_g, conv_ln_b, ret_norm_g, mix_w_out, ffn2_norm, ffn2_w_in, ffn2_w_out, final_norm, loss_target, m_ffn1_norm, m_ffn1_w_in, m_ffn1_w_out, m_mix_norm, m_mix_w_in, m_conv_w, m_conv_b, m_conv_ln_g, m_conv_ln_b, m_ret_norm_g, m_mix_w_out, m_ffn2_norm, m_ffn2_w_in, m_ffn2_w_out, m_final_norm, v_ffn1_norm, v_ffn1_w_in, v_ffn1_w_out, v_mix_norm, v_mix_w_in, v_conv_w, v_conv_b, v_conv_ln_g, v_conv_ln_b, v_ret_norm_g, v_mix_w_out, v_ffn2_norm, v_ffn2_w_in, v_ffn2_w_out, v_final_norm):
    args = locals()
    w = {n: args[n] for n in WEIGHTS}
    m = {n: args["m_" + n] for n in WEIGHTS}
    v = {n: args["v_" + n] for n in WEIGHTS}
    me = _my_id()
    x = x[0]
    target = loss_target[0]
    S = x.shape[0]
    cos, sin = _rope_tables(S)
    tables = _ret_tables()

    full = {}

    def shard(n, l):
        if n == "conv_w":
            return w[n]
        return (w[n][l].T if n in COL_SHARDED else w[n][l]).astype(BF16)

    def gather(keys):
        return GATHER, [shard(n, l) for n, l in keys]

    def gathered(keys, got):
        for (n, l), g in zip(keys, got):
            full[(n, l)] = _natural(n, g)

    gathered(STAGE_A, gather_two_level(gather(STAGE_A)[1]))

    saved = []
    for l in range(DEPTH):
        sv = {"x0": x}
        (x, sv["gate1"], sv["up1"]), got = ffn_fwd(x, _row(w["ffn1_norm"][l]), full[("ffn1_w_in", l)],
                                                   full[("ffn1_w_out", l)], gather(STAGE_B) if l == 0 else None)
        gathered(STAGE_B if l == 0 else [], got)
        sv["x1"] = x
        (sv["u"], sv["q_sb"], sv["k_sb"], sv["v_sb"], sv["qt_sb"], sv["q_r"], sv["k_r"], sv["v_r"],
         sv["g_r"]) = mix_in_fwd(x, _row(w["mix_norm"][l]), full[("mix_w_in", l)], cos, sin)
        cw = _pad_taps(full[("conv_w", None)][l])
        y_conv, sv["ypre"] = conv_fwd(sv["u"], cw, _row(w["conv_b"][l]), _row(w["conv_ln_g"][l]), _row(w["conv_ln_b"][l]))
        (o_sb, sv["tot"]), got = sb_fwd(sv["q_sb"], sv["k_sb"], sv["v_sb"], gather(STAGE_C) if l == 0 else None)
        gathered(STAGE_C if l == 0 else [], got)
        ng = w["ret_norm_g"][l].reshape(N_RET_HEADS, 1, HEAD_DIM)
        o_r, sv["y_r"], sv["states"] = ret_fwd(sv["q_r"], sv["k_r"], sv["v_r"], sv["g_r"], ng, tables)
        x, sv["ycat"] = mix_out_fwd(y_conv, o_sb, o_r, full[("mix_w_out", l)], x)
        sv["x2"] = x
        (x, sv["gate2"], sv["up2"]), _ = ffn_fwd(x, _row(w["ffn2_norm"][l]), full[("ffn2_w_in", l)],
                                                 full[("ffn2_w_out", l)])
        saved.append(sv)

    loss_acc, dx, dg_final = loss_head(x, _row(w["final_norm"]), target)
    loss = lax.psum(loss_acc[0, 0], ("x", "y", "c"))

    g = {"final_norm": dg_final.reshape(D_MODEL)}
    received = {}

    def exchange(keys, extra=(), dtype=F32):
        return EXCHANGE, [_by_device(g[(n, l)]).astype(dtype) for n, l in keys] + list(extra)

    def exchanged(keys, got):
        for key, p in zip(keys, got):
            received[key] = p

    def ffn_back(dx, x_in, gate, up, norm, names, l, comm=None):
        (dx, h, dyh, dgate, dup, hid, dg), got = ffn_bwd(dx, x_in, _row(norm), gate, up, full[(names[0], l)],
                                                         full[(names[1], l)], comm)
        g[(names[0], l)] = matmul_tn([dgate, dup], h, FF_TILE, D_MODEL, name="ffn_dw_in")
        if [(names[0], l)] == STAGE_G:
            g[(names[1], l)], got_g = matmul_tn([hid], dyh, FF_TILE, D_MODEL, name="ffn_dw_out",
                                                comm=exchange(STAGE_G, dtype=BF16))
            exchanged(STAGE_G, got_g)
        else:
            g[(names[1], l)] = matmul_tn([hid], dyh, FF_TILE, D_MODEL, name="ffn_dw_out")
        return dx, dg.reshape(D_MODEL), got

    for l in reversed(range(DEPTH)):
        sv = saved[l]
        dx, g[("ffn2_norm", l)], _ = ffn_back(dx, sv["x2"], sv["gate2"], sv["up2"], w["ffn2_norm"][l], FFN2, l)
        dxb, dy_conv, do_sb, dot_sb, do_r = mix_out_bwd(dx, full[("mix_w_out", l)])
        g[("mix_w_out", l)] = matmul_tn([sv["ycat"]], dxb, D_MODEL, D_MODEL, name="mix_dw_out")
        cw = _pad_taps(full[("conv_w", None)][l])
        du_conv, dcw, dsm = conv_bwd(dy_conv, sv["ypre"], sv["u"], cw, _row(w["conv_ln_g"][l]), _row(w["conv_ln_b"][l]))
        g[("conv_w", l)] = dcw[:CONV_WIDTH]
        g[("conv_b", l)], g[("conv_ln_g", l)], g[("conv_ln_b", l)] = dsm[0], dsm[1], dsm[2]
        stage = STAGE_D if l == DEPTH - 1 else STAGE_E
        (dq_sb, dk_t, dv_t), got = sb_bwd(sv["q_sb"], sv["k_sb"], sv["v_sb"], do_sb, sv["qt_sb"], dot_sb, sv["tot"],
                                          exchange(stage))
        exchanged(stage, got)
        ng = w["ret_norm_g"][l].reshape(N_RET_HEADS, 1, HEAD_DIM)
        dq_r, dk_r, dv_r, dg_r, dng = ret_bwd(do_r, sv["q_r"], sv["k_r"], sv["v_r"], sv["g_r"], ng, sv["y_r"],
                                              sv["states"], tables)
        g[("ret_norm_g", l)] = dng.reshape(D_RET)
        dx, h, dproj, dg = mix_in_bwd(du_conv, dq_sb, dk_t, dv_t, dq_r, dk_r, dv_r, dg_r, cos, sin,
                                      full[("mix_w_in", l)], sv["x1"], _row(w["mix_norm"][l]), dx)
        g[("mix_norm", l)] = dg.reshape(D_MODEL)
        g[("mix_w_in", l)] = matmul_tn([dproj], h, D_MODEL, D_MODEL, name="mix_dw_in")
        dx, g[("ffn1_norm", l)], got = ffn_back(dx, sv["x0"], sv["gate1"], sv["up1"], w["ffn1_norm"][l], FFN1, l,
                                                exchange(STAGE_F) if l == 0 else None)
        exchanged(STAGE_F if l == 0 else [], got)
    grad_x = dx

    small_names = [n for n in SMALL if n != "final_norm"] + ["conv_w"]
    gs = {n: jnp.stack([g[(n, l)] for l in range(DEPTH)], axis=0) for n in small_names}
    gs["final_norm"] = g["final_norm"]
    small = _pack_small(gs)
    got = communicate(*exchange(STAGE_H, [jnp.broadcast_to(small[None], (N_DEV, SMALL_ROWS, D_MODEL))], dtype=BF16))
    exchanged(STAGE_H, got[:-1])

    grad, delta, new_m, new_v = {}, {}, {}, {}
    for n in COL_SHARDED + ROW_SHARDED:
        rows = w[n].shape[1]
        col = n in COL_SHARDED
        grad[n], delta[n], new_m[n], new_v[n] = adamw([received[(n, l)] for l in range(DEPTH)], w[n], m[n], v[n],
                                                      tr=128 if col else rows // 2, transposed=col)

    def small_pack(d):
        mine = dict(d)
        cwf = jnp.zeros((DEPTH, CONV_WIDTH, D_CONV), F32)
        mine["conv_w"] = lax.dynamic_update_slice(cwf, d["conv_w"], (0, 0, me * (D_CONV // N_DEV)))
        return _pack_small(mine)

    outs = adamw([got[-1]], small_pack(w)[None], small_pack(m)[None], small_pack(v)[None], tr=SMALL_ROWS)
    for dst, o in zip((grad, delta, new_m, new_v), outs):
        un = _unpack_small(o[0], w)
        un["conv_w"] = lax.dynamic_slice(un["conv_w"], (0, 0, me * (D_CONV // N_DEV)),
                                         (DEPTH, CONV_WIDTH, D_CONV // N_DEV))
        dst.update(un)

    return (loss, grad_x[None], *[grad[n] for n in WEIGHTS], *[delta[n] for n in WEIGHTS],
            *[new_m[n] for n in WEIGHTS], *[new_v[n] for n in WEIGHTS])
```

```python
import functools

import numpy as np
import jax
import jax.numpy as jnp
from jax import lax
from jax.experimental import pallas as pl
from jax.experimental.pallas import tpu as pltpu

F32 = jnp.float32
BF16 = jnp.bfloat16

D_MODEL = 1024
DEPTH = 2
D_FF = 2816
D_CONV = 256
CONV_WIDTH = 31
CONV_HALO = 32
D_SB = 512
N_SB_HEADS = 8
D_RET = 256
N_RET_HEADS = 4
HEAD_DIM = 64
D_IN_PROJ = 3072
ROPE_BASE = 10000.0
EPS = 1e-6
N_DEV = 8

ADAM_LR = 0.001
ADAM_B1 = 0.9
ADAM_B2 = 0.999
ADAM_EPS = 1e-08
ADAM_WD = 0.01
ADAM_STEP = 10

VMEM_LIMIT = 56 * 1024 * 1024
ROW_TILE = 512
FF_TILE = 1408
FF_FWD_CHUNK = 256
FF_BWD_CHUNK = 2816
SB_TILE = 256
SB_ROWS = 512
RET_TILE = 512
CONV_TILE = 256

NT_DIMS = (((1,), (1,)), ((), ()))
TN_DIMS = (((0,), (0,)), ((), ()))


def _params(n_axes, vmem=VMEM_LIMIT):
    return pltpu.CompilerParams(dimension_semantics=("arbitrary",) * n_axes, vmem_limit_bytes=vmem)


def _dot(a, b):
    return jnp.dot(a, b, preferred_element_type=F32)


def _dot_nt(a, b):
    return lax.dot_general(a, b, NT_DIMS, preferred_element_type=F32)


def _dot_tn(a, b):
    return lax.dot_general(a, b, TN_DIMS, preferred_element_type=F32)


def _sigmoid(z):
    return 1.0 / (1.0 + jnp.exp(-z))


def _rms_stats(xv):
    r = lax.rsqrt(jnp.mean(xv * xv, axis=-1, keepdims=True) + EPS)
    return r, xv * r


def _rms_bwd(xv, g, dh):
    r, xhat = _rms_stats(xv)
    dxhat = dh * g
    dx = r * (dxhat - xhat * jnp.mean(dxhat * xhat, axis=-1, keepdims=True))
    dg = jnp.sum(dh * xhat, axis=0, keepdims=True)
    return dx, (xhat * g).astype(BF16), dg


def ffn_fwd(x, g, w_in, w_out, comm=None, tm=ROW_TILE):
    S = x.shape[0]
    chunk = FF_FWD_CHUNK

    def body(x_ref, g_ref, w_ref, wo_ref, y_ref, gate_ref, up_ref):
        xv = x_ref[...]
        _, xhat = _rms_stats(xv)
        h = (xhat * g_ref[...]).astype(BF16)
        acc = None
        for j in range(D_FF // chunk):
            cols = pl.ds(j * chunk, chunk)
            gt = _dot_nt(h, w_ref[cols, :])
            up = _dot_nt(h, w_ref[pl.ds(D_FF + j * chunk, chunk), :])
            gate_ref[:, cols] = gt.astype(BF16)
            up_ref[:, cols] = up.astype(BF16)
            part = _dot((gt * _sigmoid(gt) * up).astype(BF16), wo_ref[cols, :])
            acc = part if acc is None else acc + part
        y_ref[...] = xv + 0.5 * acc

    row = lambda i: (i, 0)
    one = lambda i: (0, 0)
    resident = pl.Buffered(1)
    return _call(
        body, (x, g, w_in, w_out), comm, name="ffn_fwd",
        grid=(S // tm,),
        in_specs=[
            pl.BlockSpec((tm, D_MODEL), row),
            pl.BlockSpec((1, D_MODEL), one),
            pl.BlockSpec((2 * D_FF, D_MODEL), one, pipeline_mode=resident),
            pl.BlockSpec((D_FF, D_MODEL), one, pipeline_mode=resident),
        ],
        out_specs=[
            pl.BlockSpec((tm, D_MODEL), row),
            pl.BlockSpec((tm, D_FF), row),
            pl.BlockSpec((tm, D_FF), row),
        ],
        out_shape=[
            jax.ShapeDtypeStruct((S, D_MODEL), F32),
            jax.ShapeDtypeStruct((S, D_FF), BF16),
            jax.ShapeDtypeStruct((S, D_FF), BF16),
        ],
        scratch_shapes=[],
    )


def ffn_bwd(dy, x, g, gate, up, w_in, w_out, comm=None, tm=ROW_TILE // 2):
    S = x.shape[0]
    chunk = FF_BWD_CHUNK

    def body(dy_ref, x_ref, g_ref, gate_ref, up_ref, w_ref, wo_ref,
             dx_ref, h_ref, dyh_ref, dgate_ref, dup_ref, hid_ref, dg_ref):
        i = pl.program_id(0)
        d2 = (0.5 * dy_ref[...]).astype(BF16)
        dyh_ref[...] = d2
        dh = None
        for j in range(D_FF // chunk):
            cols = pl.ds(j * chunk, chunk)
            dhid = _dot_nt(d2, wo_ref[cols, :])
            gt = gate_ref[:, cols].astype(F32)
            u = up_ref[:, cols].astype(F32)
            sig = _sigmoid(gt)
            sl = gt * sig
            dgate = (dhid * u * (sig * (1.0 + gt * (1.0 - sig)))).astype(BF16)
            dup = (dhid * sl).astype(BF16)
            dgate_ref[:, cols] = dgate
            dup_ref[:, cols] = dup
            hid_ref[:, cols] = (sl * u).astype(BF16)
            part = _dot(dgate, w_ref[cols, :]) + _dot(dup, w_ref[pl.ds(D_FF + j * chunk, chunk), :])
            dh = part if dh is None else dh + part
        dx, h, dg = _rms_bwd(x_ref[...], g_ref[...], dh)
        dx_ref[...] = dy_ref[...] + dx
        h_ref[...] = h

        @pl.when(i == 0)
        def _():
            dg_ref[...] = dg

        @pl.when(i > 0)
        def _():
            dg_ref[...] += dg

    row = lambda i: (i, 0)
    one = lambda i: (0, 0)
    resident = pl.Buffered(1)
    return _call(
        body, (dy, x, g, gate, up, w_in, w_out), comm, name="ffn_bwd",
        grid=(S // tm,),
        in_specs=[
            pl.BlockSpec((tm, D_MODEL), row),
            pl.BlockSpec((tm, D_MODEL), row),
            pl.BlockSpec((1, D_MODEL), one),
            pl.BlockSpec((tm, D_FF), row),
            pl.BlockSpec((tm, D_FF), row),
            pl.BlockSpec((2 * D_FF, D_MODEL), one, pipeline_mode=resident),
            pl.BlockSpec((D_FF, D_MODEL), one, pipeline_mode=resident),
        ],
        out_specs=[
            pl.BlockSpec((tm, D_MODEL), row),
            pl.BlockSpec((tm, D_MODEL), row),
            pl.BlockSpec((tm, D_MODEL), row),
            pl.BlockSpec((tm, D_FF), row),
            pl.BlockSpec((tm, D_FF), row),
            pl.BlockSpec((tm, D_FF), row),
            pl.BlockSpec((1, D_MODEL), one),
        ],
        out_shape=[
            jax.ShapeDtypeStruct((S, D_MODEL), F32),
            jax.ShapeDtypeStruct((S, D_MODEL), BF16),
            jax.ShapeDtypeStruct((S, D_MODEL), BF16),
            jax.ShapeDtypeStruct((S, D_FF), BF16),
            jax.ShapeDtypeStruct((S, D_FF), BF16),
            jax.ShapeDtypeStruct((S, D_FF), BF16),
            jax.ShapeDtypeStruct((1, D_MODEL), F32),
        ],
        scratch_shapes=[],
    )


def matmul_tn(a_list, b, ta, tn, tk=2 * ROW_TILE, name="matmul_tn", comm=None):
    S, ka = a_list[0].shape
    nb = b.shape[1]
    assert S % tk == 0 and ka % ta == 0 and nb % tn == 0
    per = ka // ta

    def body(*refs):
        a_refs, b_ref, o_ref = refs[:-2], refs[-2], refs[-1]
        i = pl.program_id(0)
        k = pl.program_id(2)

        @pl.when(k == 0)
        def _():
            o_ref[...] = jnp.zeros_like(o_ref)

        for t, a_ref in enumerate(a_refs):
            @pl.when(lax.div(i, per) == t)
            def _(a_ref=a_ref):
                o_ref[...] += _dot_tn(a_ref[...], b_ref[...])

    def a_spec(t):
        def index(i, j, k):
            mine = lax.div(i, per) == t
            return jnp.where(mine, k, 0), jnp.where(mine, i - t * per, 0)
        return pl.BlockSpec((tk, ta), index)

    (out,), got = _call(
        body, (*a_list, b), comm, name=name,
        grid=(per * len(a_list), nb // tn, S // tk),
        in_specs=[a_spec(t) for t in range(len(a_list))] + [pl.BlockSpec((tk, tn), lambda i, j, k: (k, j))],
        out_specs=[pl.BlockSpec((ta, tn), lambda i, j, k: (i, j))],
        out_shape=[jax.ShapeDtypeStruct((ka * len(a_list), nb), F32)],
        scratch_shapes=[],
    )
    return (out, got) if comm is not None else out


SB_COLS = (2 * D_CONV, 2 * D_CONV + D_SB, 2 * D_CONV + 2 * D_SB)
RET_COLS = tuple(2 * D_CONV + 3 * D_SB + j * D_RET for j in range(4))


def _swap_halves(x):
    n = x.shape[1]
    lane = lax.broadcasted_iota(jnp.int32, x.shape, 1)
    first = (lane % HEAD_DIM) < (HEAD_DIM // 2)
    return jnp.where(first, pltpu.roll(x, n - HEAD_DIM // 2, 1), pltpu.roll(x, HEAD_DIM // 2, 1))


def _head(x, h):
    return x[:, h * HEAD_DIM:(h + 1) * HEAD_DIM]


def _heads_spec(n_heads, tm):
    return pl.BlockSpec((n_heads, tm, HEAD_DIM), lambda i: (0, i, 0))


def mix_in_fwd(x, g, w, cos, sin, tm=ROW_TILE):
    S = x.shape[0]

    def body(x_ref, g_ref, w_ref, c_ref, s_ref, u_ref, q_ref, k_ref, v_ref, qt_ref, qr_ref, kr_ref, vr_ref, gr_ref):
        _, xhat = _rms_stats(x_ref[...])
        proj = _dot_nt((xhat * g_ref[...]).astype(BF16), w_ref[...])
        u_ref[...] = proj[:, :2 * D_CONV]
        for h in range(N_SB_HEADS):
            q = (_head(proj[:, SB_COLS[0]:SB_COLS[1]], h) * 0.125).astype(BF16)
            q_ref[h] = q
            qt_ref[h] = q.T
            k_ref[h] = _head(proj[:, SB_COLS[1]:SB_COLS[2]], h).astype(BF16)
            v_ref[h] = _head(proj[:, SB_COLS[2]:RET_COLS[0]], h).astype(BF16)
        c = c_ref[...]
        s = s_ref[...]
        qv = proj[:, RET_COLS[0]:RET_COLS[1]]
        kv = proj[:, RET_COLS[1]:RET_COLS[2]]
        q_rot = ((qv * c + _swap_halves(qv) * s) * 0.125).astype(BF16)
        k_rot = (kv * c + _swap_halves(kv) * s).astype(BF16)
        for h in range(N_RET_HEADS):
            qr_ref[h] = _head(q_rot, h)
            kr_ref[h] = _head(k_rot, h)
            vr_ref[h] = _head(proj[:, RET_COLS[2]:RET_COLS[3]], h).astype(BF16)
            gr_ref[h] = _head(proj[:, RET_COLS[3]:], h)

    row = lambda i: (i, 0)
    one = lambda i: (0, 0)
    sb = jax.ShapeDtypeStruct((N_SB_HEADS, S, HEAD_DIM), BF16)
    ret = jax.ShapeDtypeStruct((N_RET_HEADS, S, HEAD_DIM), BF16)
    return pl.pallas_call(
        body, name="mix_in_fwd",
        grid=(S // tm,),
        in_specs=[
            pl.BlockSpec((tm, D_MODEL), row),
            pl.BlockSpec((1, D_MODEL), one),
            pl.BlockSpec((D_IN_PROJ, D_MODEL), one, pipeline_mode=pl.Buffered(1)),
            pl.BlockSpec((tm, D_RET), row),
            pl.BlockSpec((tm, D_RET), row),
        ],
        out_specs=[
            pl.BlockSpec((tm, 2 * D_CONV), row),
            _heads_spec(N_SB_HEADS, tm), _heads_spec(N_SB_HEADS, tm), _heads_spec(N_SB_HEADS, tm),
            pl.BlockSpec((N_SB_HEADS, HEAD_DIM, tm), lambda i: (0, 0, i)),
            _heads_spec(N_RET_HEADS, tm), _heads_spec(N_RET_HEADS, tm), _heads_spec(N_RET_HEADS, tm),
            _heads_spec(N_RET_HEADS, tm),
        ],
        out_shape=[
            jax.ShapeDtypeStruct((S, 2 * D_CONV), F32), sb, sb, sb,
            jax.ShapeDtypeStruct((N_SB_HEADS, HEAD_DIM, S), BF16),
            ret, ret, ret, jax.ShapeDtypeStruct((N_RET_HEADS, S, HEAD_DIM), F32),
        ],
        compiler_params=_params(1),
    )(x, g, w, cos, sin)


def mix_in_bwd(du, dq, dkt, dvt, dqr, dkr, dvr, dgr, cos, sin, w, x, g, dy, tm=SB_TILE):
    S = x.shape[0]
    assert dkt.shape[-1] == tm

    def body(du_ref, dq_ref, dkt_ref, dvt_ref, dqr_ref, dkr_ref, dvr_ref, dgr_ref, c_ref, s_ref, w_ref, x_ref, g_ref,
             dy_ref, dx_ref, h_ref, dp_ref, dg_ref):
        i = pl.program_id(0)
        sb_heads = range(N_SB_HEADS)
        ret_heads = range(N_RET_HEADS)
        c = c_ref[...]
        s = s_ref[...]
        dq_rot = jnp.concatenate([dqr_ref[h] for h in ret_heads], axis=1) * 0.125
        dk_rot = jnp.concatenate([dkr_ref[h] for h in ret_heads], axis=1)
        dproj = jnp.concatenate([
            du_ref[...].astype(BF16),
            jnp.concatenate([dq_ref[h] * 0.125 for h in sb_heads], axis=1).astype(BF16),
            jnp.concatenate([dkt_ref[h, 0].T for h in sb_heads], axis=1).astype(BF16),
            jnp.concatenate([dvt_ref[h, 0].T for h in sb_heads], axis=1).astype(BF16),
            (dq_rot * c - _swap_halves(dq_rot) * s).astype(BF16),
            (dk_rot * c - _swap_halves(dk_rot) * s).astype(BF16),
            jnp.concatenate([dvr_ref[h] for h in ret_heads], axis=1).astype(BF16),
            jnp.concatenate([dgr_ref[h] for h in ret_heads], axis=1).astype(BF16)], axis=1)
        dp_ref[...] = dproj
        dh = _dot(dproj, w_ref[...])
        dx, h, dg = _rms_bwd(x_ref[...], g_ref[...], dh)
        dx_ref[...] = dy_ref[...] + dx
        h_ref[...] = h

        @pl.when(i == 0)
        def _():
            dg_ref[...] = dg

        @pl.when(i > 0)
        def _():
            dg_ref[...] += dg

    row = lambda i: (i, 0)
    one = lambda i: (0, 0)
    tiles = pl.BlockSpec((N_SB_HEADS, 1, HEAD_DIM, tm), lambda i: (0, i, 0, 0))
    return pl.pallas_call(
        body, name="mix_in_bwd",
        grid=(S // tm,),
        in_specs=[
            pl.BlockSpec((tm, 2 * D_CONV), row),
            _heads_spec(N_SB_HEADS, tm), tiles, tiles,
            _heads_spec(N_RET_HEADS, tm), _heads_spec(N_RET_HEADS, tm), _heads_spec(N_RET_HEADS, tm),
            _heads_spec(N_RET_HEADS, tm),
            pl.BlockSpec((tm, D_RET), row),
            pl.BlockSpec((tm, D_RET), row),
            pl.BlockSpec((D_IN_PROJ, D_MODEL), one, pipeline_mode=pl.Buffered(1)),
            pl.BlockSpec((tm, D_MODEL), row),
            pl.BlockSpec((1, D_MODEL), one),
            pl.BlockSpec((tm, D_MODEL), row),
        ],
        out_specs=[
            pl.BlockSpec((tm, D_MODEL), row),
            pl.BlockSpec((tm, D_MODEL), row),
            pl.BlockSpec((tm, D_IN_PROJ), row),
            pl.BlockSpec((1, D_MODEL), one),
        ],
        out_shape=[
            jax.ShapeDtypeStruct((S, D_MODEL), F32),
            jax.ShapeDtypeStruct((S, D_MODEL), BF16),
            jax.ShapeDtypeStruct((S, D_IN_PROJ), BF16),
            jax.ShapeDtypeStruct((1, D_MODEL), F32),
        ],
        compiler_params=_params(1),
    )(du, dq, dkt, dvt, dqr, dkr, dvr, dgr, cos, sin, w, x, g, dy)


def mix_out_fwd(y_conv, o_sb, o_ret, w, x, tm=ROW_TILE):
    S = x.shape[0]

    def body(yc_ref, sb_ref, rt_ref, w_ref, x_ref, o_ref, ycat_ref):
        ycat = jnp.concatenate(
            [yc_ref[...]] + [sb_ref[h].astype(BF16) for h in range(N_SB_HEADS)]
            + [rt_ref[h].astype(BF16) for h in range(N_RET_HEADS)], axis=1)
        ycat_ref[...] = ycat
        o_ref[...] = x_ref[...] + _dot(ycat, w_ref[...])

    row = lambda i: (i, 0)
    return pl.pallas_call(
        body, name="mix_out_fwd",
        grid=(S // tm,),
        in_specs=[
            pl.BlockSpec((tm, D_CONV), row),
            _heads_spec(N_SB_HEADS, tm),
            _heads_spec(N_RET_HEADS, tm),
            pl.BlockSpec((D_MODEL, D_MODEL), lambda i: (0, 0)),
            pl.BlockSpec((tm, D_MODEL), row),
        ],
        out_specs=[pl.BlockSpec((tm, D_MODEL), row), pl.BlockSpec((tm, D_MODEL), row)],
        out_shape=[jax.ShapeDtypeStruct((S, D_MODEL), F32), jax.ShapeDtypeStruct((S, D_MODEL), BF16)],
        compiler_params=_params(1),
    )(y_conv, o_sb, o_ret, w, x)


def mix_out_bwd(dy, w, tm=ROW_TILE):
    S = dy.shape[0]

    def body(dy_ref, w_ref, dyb_ref, dc_ref, do_ref, dot_ref, dr_ref):
        d = dy_ref[...].astype(BF16)
        dyb_ref[...] = d
        dycat = _dot_nt(d, w_ref[...])
        dc_ref[...] = dycat[:, :D_CONV]
        for h in range(N_SB_HEADS):
            do = _head(dycat[:, D_CONV:D_CONV + D_SB], h).astype(BF16)
            do_ref[h] = do
            dot_ref[h] = do.T
        for h in range(N_RET_HEADS):
            dr_ref[h] = _head(dycat[:, D_CONV + D_SB:], h)

    row = lambda i: (i, 0)
    return pl.pallas_call(
        body, name="mix_out_bwd",
        grid=(S // tm,),
        in_specs=[
            pl.BlockSpec((tm, D_MODEL), row),
            pl.BlockSpec((D_MODEL, D_MODEL), lambda i: (0, 0)),
        ],
        out_specs=[
            pl.BlockSpec((tm, D_MODEL), row),
            pl.BlockSpec((tm, D_CONV), row),
            _heads_spec(N_SB_HEADS, tm),
            pl.BlockSpec((N_SB_HEADS, HEAD_DIM, tm), lambda i: (0, 0, i)),
            _heads_spec(N_RET_HEADS, tm),
        ],
        out_shape=[
            jax.ShapeDtypeStruct((S, D_MODEL), BF16),
            jax.ShapeDtypeStruct((S, D_CONV), F32),
            jax.ShapeDtypeStruct((N_SB_HEADS, S, HEAD_DIM), BF16),
            jax.ShapeDtypeStruct((N_SB_HEADS, HEAD_DIM, S), BF16),
            jax.ShapeDtypeStruct((N_RET_HEADS, S, HEAD_DIM), F32),
        ],
        compiler_params=_params(1),
    )(dy, w)


def _rows_from(x, start, n):
    return pltpu.roll(x, (x.shape[0] - start) % x.shape[0], 0)[:n]


def _conv_ln(ypre, ln_g, ln_b):
    mu = jnp.mean(ypre, axis=-1, keepdims=True)
    yc = ypre - mu
    rstd = lax.rsqrt(jnp.mean(yc * yc, axis=-1, keepdims=True) + EPS)
    yn = yc * rstd
    return yn, rstd, yn * ln_g + ln_b


def conv_fwd(proj, cw, cb, ln_g, ln_b, tm=CONV_TILE):
    S = proj.shape[0]
    hb = tm // CONV_HALO

    def body(a_ref, b_ref, ap_ref, bp_ref, cw_ref, cb_ref, g_ref, bb_ref, y_ref, ypre_ref, v_sc):
        i = pl.program_id(0)
        prev = ap_ref[...] * _sigmoid(bp_ref[...])
        v_sc[pl.ds(0, CONV_HALO), :] = jnp.where(i > 0, prev, 0.0)
        v_sc[pl.ds(CONV_HALO, tm), :] = a_ref[...] * _sigmoid(b_ref[...])
        vext = v_sc[...]
        acc = jnp.zeros((tm, D_CONV), F32)
        for j in range(CONV_WIDTH):
            acc = acc + cw_ref[pl.ds(j, 1), :] * _rows_from(vext, CONV_HALO - (CONV_WIDTH - 1) + j, tm)
        ypre = acc + cb_ref[...]
        ypre_ref[...] = ypre
        _, _, z = _conv_ln(ypre, g_ref[...], bb_ref[...])
        y_ref[...] = (z * _sigmoid(z)).astype(BF16)

    one = lambda i: (0, 0)
    return pl.pallas_call(
        body, name="conv_fwd",
        grid=(S // tm,),
        in_specs=[
            pl.BlockSpec((tm, D_CONV), lambda i: (i, 0)),
            pl.BlockSpec((tm, D_CONV), lambda i: (i, 1)),
            pl.BlockSpec((CONV_HALO, D_CONV), lambda i: (jnp.maximum(i * hb - 1, 0), 0)),
            pl.BlockSpec((CONV_HALO, D_CONV), lambda i: (jnp.maximum(i * hb - 1, 0), 1)),
            pl.BlockSpec((CONV_HALO, D_CONV), one),
            pl.BlockSpec((1, D_CONV), one),
            pl.BlockSpec((1, D_CONV), one),
            pl.BlockSpec((1, D_CONV), one),
        ],
        out_specs=[pl.BlockSpec((tm, D_CONV), lambda i: (i, 0)), pl.BlockSpec((tm, D_CONV), lambda i: (i, 0))],
        out_shape=[jax.ShapeDtypeStruct((S, D_CONV), BF16), jax.ShapeDtypeStruct((S, D_CONV), F32)],
        scratch_shapes=[pltpu.VMEM((tm + CONV_HALO, D_CONV), F32)],
        compiler_params=_params(1),
    )(proj, proj, proj, proj, cw, cb, ln_g, ln_b)


def conv_bwd(dyc, ypre, proj, cw, ln_g, ln_b, tm=CONV_TILE):
    S = ypre.shape[0]
    hb = tm // CONV_HALO
    nblk = S // tm
    last_halo = S // CONV_HALO - 1

    def dpre(dy, yp, g, bb):
        yn, rstd, z = _conv_ln(yp, g, bb)
        sg = _sigmoid(z)
        dz = dy * (sg * (1.0 + z * (1.0 - sg)))
        dyn = dz * g
        d = rstd * (dyn - jnp.mean(dyn, axis=-1, keepdims=True) - yn * jnp.mean(dyn * yn, axis=-1, keepdims=True))
        return d, dz * yn, dz

    def body(dy_ref, yp_ref, dyn_ref, ypn_ref, a_ref, b_ref, ap_ref, bp_ref, cw_ref, g_ref, bb_ref,
             du_ref, dcw_ref, dsm_ref, d_sc, v_sc):
        i = pl.program_id(0)
        g = g_ref[...]
        bb = bb_ref[...]
        d_main, dgn, dz = dpre(dy_ref[...], yp_ref[...], g, bb)
        d_next, _, _ = dpre(dyn_ref[...], ypn_ref[...], g, bb)
        d_sc[pl.ds(0, tm), :] = d_main
        d_sc[pl.ds(tm, CONV_HALO), :] = jnp.where(i < nblk - 1, d_next, 0.0)
        a = a_ref[...]
        sb = _sigmoid(b_ref[...])
        prev = ap_ref[...] * _sigmoid(bp_ref[...])
        v_sc[pl.ds(0, CONV_HALO), :] = jnp.where(i > 0, prev, 0.0)
        v_sc[pl.ds(CONV_HALO, tm), :] = a * sb

        @pl.when(i == 0)
        def _():
            dcw_ref[...] = jnp.zeros_like(dcw_ref)
            dsm_ref[...] = jnp.zeros_like(dsm_ref)

        dext = d_sc[...]
        vext = v_sc[...]
        dv = jnp.zeros((tm, D_CONV), F32)
        for j in range(CONV_WIDTH):
            dv = dv + cw_ref[pl.ds(j, 1), :] * _rows_from(dext, CONV_WIDTH - 1 - j, tm)
            shifted = _rows_from(vext, CONV_HALO - (CONV_WIDTH - 1) + j, tm)
            dcw_ref[pl.ds(j, 1), :] += jnp.sum(d_main * shifted, axis=0, keepdims=True)
        du_ref[:, pl.ds(0, D_CONV)] = dv * sb
        du_ref[:, pl.ds(D_CONV, D_CONV)] = dv * a * sb * (1.0 - sb)
        dsm_ref[pl.ds(0, 1), :] += jnp.sum(d_main, axis=0, keepdims=True)
        dsm_ref[pl.ds(1, 1), :] += jnp.sum(dgn, axis=0, keepdims=True)
        dsm_ref[pl.ds(2, 1), :] += jnp.sum(dz, axis=0, keepdims=True)

    one = lambda i: (0, 0)
    prev_map = lambda c: (lambda i: (jnp.maximum(i * hb - 1, 0), c))
    next_map = lambda i: (jnp.minimum((i + 1) * hb, last_halo), 0)
    return pl.pallas_call(
        body, name="conv_bwd",
        grid=(nblk,),
        in_specs=[
            pl.BlockSpec((tm, D_CONV), lambda i: (i, 0)),
            pl.BlockSpec((tm, D_CONV), lambda i: (i, 0)),
            pl.BlockSpec((CONV_HALO, D_CONV), next_map),
            pl.BlockSpec((CONV_HALO, D_CONV), next_map),
            pl.BlockSpec((tm, D_CONV), lambda i: (i, 0)),
            pl.BlockSpec((tm, D_CONV), lambda i: (i, 1)),
            pl.BlockSpec((CONV_HALO, D_CONV), prev_map(0)),
            pl.BlockSpec((CONV_HALO, D_CONV), prev_map(1)),
            pl.BlockSpec((CONV_HALO, D_CONV), one),
            pl.BlockSpec((1, D_CONV), one),
            pl.BlockSpec((1, D_CONV), one),
        ],
        out_specs=[
            pl.BlockSpec((tm, 2 * D_CONV), lambda i: (i, 0)),
            pl.BlockSpec((CONV_HALO, D_CONV), one),
            pl.BlockSpec((8, D_CONV), one),
        ],
        out_shape=[
            jax.ShapeDtypeStruct((S, 2 * D_CONV), F32),
            jax.ShapeDtypeStruct((CONV_HALO, D_CONV), F32),
            jax.ShapeDtypeStruct((8, D_CONV), F32),
        ],
        scratch_shapes=[pltpu.VMEM((tm + CONV_HALO, D_CONV), F32), pltpu.VMEM((tm + CONV_HALO, D_CONV), F32)],
        compiler_params=_params(1),
    )(dyc, ypre, dyc, ypre, proj, proj, proj, proj, cw, ln_g, ln_b)


SB_GROUP = 8


def _softplus(z):
    neg_abs = lax.bitcast_convert_type(lax.bitcast_convert_type(z, jnp.uint32) | jnp.uint32(0x80000000), F32)
    return jnp.maximum(z, 0.0) + jnp.log(1.0 + jnp.exp(neg_abs))


def _full_groups(n, body):
    def step(t, c):
        body(t * SB_GROUP)
        return c

    lax.fori_loop(0, lax.div(n, SB_GROUP), step, 0)


def _last_group(n, step, body):
    r = lax.rem(n, SB_GROUP)
    for k in range(0, SB_GROUP, step):
        @pl.when(r == k)
        def _(k=k):
            body(k)


def _rows(xs):
    return xs[0] if len(xs) == 1 else jnp.concatenate(xs, axis=0)


def sb_fwd(q, k, v, comm=None, T=SB_TILE, Q=SB_ROWS):
    H, S, dh = q.shape
    M = Q // T

    def body(q_ref, k_ref, v_ref, o_ref, tot_ref, acc_sc, car_sc):
        qb = pl.program_id(1)
        qv = q_ref[...]
        row = lax.broadcasted_iota(jnp.int32, (T, T), 0)
        col = lax.broadcasted_iota(jnp.int32, (T, T), 1)
        tri = jnp.where(row >= col, 1.0, 0.0).astype(BF16)
        qrow = lax.broadcasted_iota(jnp.int32, (Q, T), 0)
        kcol = lax.broadcasted_iota(jnp.int32, (Q, T), 1)
        causal = {d + 1: kcol + d * T < qrow for d in range(M)}
        acc_sc[...] = jnp.zeros_like(acc_sc)
        car_sc[...] = jnp.zeros_like(car_sc)

        def logits(kb, masked):
            ks = k_ref[pl.ds(pl.multiple_of(kb * T, T), T), :]
            z = _dot_nt(qv, ks)
            nb = _softplus(z)
            if masked:
                nb = jnp.where(causal[masked], nb, 0.0)
            return z, nb.astype(BF16)

        def group(kbs, diag):
            parts = [logits(kb, d) for kb, d in zip(kbs, diag)]
            pall = _dot(_rows([nb for _, nb in parts]), tri)
            carry = car_sc[...]
            out = None
            for j, kb in enumerate(kbs):
                p = pall[j * Q:(j + 1) * Q]
                vs = v_ref[pl.ds(pl.multiple_of(kb * T, T), T), :]
                w = jnp.exp((parts[j][0] - carry) - p)
                if diag[j]:
                    w = jnp.where(causal[diag[j]], w, 0.0)
                o = _dot(w.astype(BF16), vs)
                out = o if out is None else out + o
                carry = carry + p[:, 0:1]
            acc_sc[...] += out
            car_sc[...] = carry

        full = M * qb
        _last_group(full, M, lambda r: group([full + d for d in reversed(range(M))] + [full - 1 - o for o in range(r)],
                                             [d + 1 for d in reversed(range(M))] + [0] * r))
        rest = full - lax.rem(full, SB_GROUP)
        _full_groups(rest, lambda o: group([rest - 1 - o - j for j in range(SB_GROUP)], [0] * SB_GROUP))
        o_ref[...] = acc_sc[...]
        tot_ref[...] = car_sc[...]

    return _call(
        body, (q, k, v), comm, name="sb_fwd",
        grid=(H, S // Q),
        in_specs=[
            pl.BlockSpec((None, Q, dh), lambda h, i: (h, i, 0)),
            pl.BlockSpec((None, S, dh), lambda h, i: (h, 0, 0)),
            pl.BlockSpec((None, S, dh), lambda h, i: (h, 0, 0)),
        ],
        out_specs=[
            pl.BlockSpec((None, Q, dh), lambda h, i: (h, i, 0)),
            pl.BlockSpec((None, Q, 1), lambda h, i: (h, i, 0)),
        ],
        out_shape=[jax.ShapeDtypeStruct((H, S, dh), F32), jax.ShapeDtypeStruct((H, S, 1), F32)],
        scratch_shapes=[pltpu.VMEM((Q, dh), F32), pltpu.VMEM((Q, 1), F32)],
    )


def sb_bwd(q, k, v, do, qt, dot, tot, comm=None, T=SB_TILE, Q=SB_ROWS):
    H, S, dh = q.shape
    nt = S // T
    M = Q // T

    def body(q_ref, k_ref, v_ref, do_ref, qt_ref, dot_ref, tot_ref, dq_ref, dk_ref, dv_ref, acc_sc, rc_sc, gc_sc):
        qb = pl.program_id(1)
        qv = q_ref[...]
        dov = do_ref[...]
        qtv = qt_ref[...]
        dotv = dot_ref[...]
        row = lax.broadcasted_iota(jnp.int32, (T, T), 0)
        col = lax.broadcasted_iota(jnp.int32, (T, T), 1)
        before = jnp.where(row < col, 1.0, 0.0).astype(BF16)
        qrow = lax.broadcasted_iota(jnp.int32, (Q, T), 0)
        kcol = lax.broadcasted_iota(jnp.int32, (Q, T), 1)
        causal = {d + 1: kcol + d * T < qrow for d in range(M)}
        acc_sc[...] = jnp.zeros_like(acc_sc)
        rc_sc[...] = tot_ref[...]
        gc_sc[...] = jnp.zeros_like(gc_sc)

        @pl.when(qb == 0)
        def _():
            dk_ref[...] = jnp.zeros_like(dk_ref)
            dv_ref[...] = jnp.zeros_like(dv_ref)

        def first(kb, masked):
            start = pl.multiple_of(kb * T, T)
            z = _dot_nt(qv, k_ref[pl.ds(start, T), :])
            nb = _softplus(z)
            sig = jnp.exp(z - nb)
            if masked:
                nb = jnp.where(causal[masked], nb, 0.0)
            dw = _dot_nt(dov, v_ref[pl.ds(start, T), :])
            return z, sig, nb.astype(BF16), dw

        def group(kbs, diag):
            parts = [first(kb, d) for kb, d in zip(kbs, diag)]
            pall = _dot(_rows([p[2] for p in parts]), before)
            rc = rc_sc[...]
            ws, gs, ghs = [], [], []
            for j in range(len(kbs)):
                z, _, nbh, dw = parts[j]
                p = pall[j * Q:(j + 1) * Q]
                w = jnp.exp((z - rc) + p)
                rc = rc - (p[:, T - 1:T] + nbh[:, T - 1:T].astype(F32))
                if diag[j]:
                    w = jnp.where(causal[diag[j]], w, 0.0)
                g = dw * w
                ws.append(w.astype(BF16))
                gs.append(g)
                ghs.append(g.astype(BF16))
            glall = _dot(_rows(ghs), before)
            gc = gc_sc[...]
            dq = None
            for j, kb in enumerate(kbs):
                ks = k_ref[pl.ds(pl.multiple_of(kb * T, T), T), :]
                gl = glall[j * Q:(j + 1) * Q]
                dz = gs[j] - parts[j][1] * (gs[j] + (gl + gc))
                gc = gc + gl[:, T - 1:T] + ghs[j][:, T - 1:T].astype(F32)
                if diag[j]:
                    dz = jnp.where(causal[diag[j]], dz, 0.0)
                dzb = dz.astype(BF16)
                d = _dot(dzb, ks)
                dq = d if dq is None else dq + d
                dk_ref[kb] += _dot(qtv, dzb)
                dv_ref[kb] += _dot(dotv, ws[j])
            acc_sc[...] += dq
            rc_sc[...] = rc
            gc_sc[...] = gc

        full = M * qb
        _full_groups(full, lambda o: group([o + j for j in range(SB_GROUP)], [0] * SB_GROUP))
        rest = full - lax.rem(full, SB_GROUP)
        _last_group(full, M, lambda r: group([rest + j for j in range(r)] + [full + d for d in range(M)],
                                             [0] * r + [d + 1 for d in range(M)]))
        dq_ref[...] = acc_sc[...]

    return _call(
        body, (q, k, v, do, qt, dot, tot), comm, name="sb_bwd",
        grid=(H, S // Q),
        in_specs=[
            pl.BlockSpec((None, Q, dh), lambda h, i: (h, i, 0)),
            pl.BlockSpec((None, S, dh), lambda h, i: (h, 0, 0)),
            pl.BlockSpec((None, S, dh), lambda h, i: (h, 0, 0)),
            pl.BlockSpec((None, Q, dh), lambda h, i: (h, i, 0)),
            pl.BlockSpec((None, dh, Q), lambda h, i: (h, 0, i)),
            pl.BlockSpec((None, dh, Q), lambda h, i: (h, 0, i)),
            pl.BlockSpec((None, Q, 1), lambda h, i: (h, i, 0)),
        ],
        out_specs=[
            pl.BlockSpec((None, Q, dh), lambda h, i: (h, i, 0)),
            pl.BlockSpec((None, nt, dh, T), lambda h, i: (h, 0, 0, 0)),
            pl.BlockSpec((None, nt, dh, T), lambda h, i: (h, 0, 0, 0)),
        ],
        out_shape=[jax.ShapeDtypeStruct((H, S, dh), F32), jax.ShapeDtypeStruct((H, nt, dh, T), F32),
                   jax.ShapeDtypeStruct((H, nt, dh, T), F32)],
        scratch_shapes=[pltpu.VMEM((Q, dh), F32), pltpu.VMEM((Q, 1), F32), pltpu.VMEM((Q, 1), F32)],
    )


def _ret_tables(T=RET_TILE):
    hh = jnp.arange(N_RET_HEADS, dtype=F32)
    log_gamma = jnp.log1p(-jnp.exp2(-5.0 - hh))
    idx = jnp.arange(T, dtype=F32)
    diff = idx[:, None] - idx[None, :]
    ci = (jnp.arange(T) // 64)
    same = ci[:, None] == ci[None, :]
    earlier = ci[None, :] < ci[:, None]
    dist = jnp.where(same, jnp.abs(diff), diff)
    dmat = jnp.where(same | earlier, jnp.exp(log_gamma[:, None, None] * dist[None]), 0.0)
    ones = jnp.ones((1, 1, HEAD_DIM), F32)
    qdec = jnp.exp(log_gamma[:, None] * (idx + 1.0)[None, :])[:, :, None] * ones
    kdec = jnp.exp(log_gamma[:, None] * (T - 1.0 - idx)[None, :])[:, :, None] * ones
    bdec = jnp.exp(log_gamma * T)[:, None, None] * jnp.ones((1, HEAD_DIM, HEAD_DIM), F32)
    return dmat, qdec, kdec, bdec


def _rope_tables(S):
    half = HEAD_DIM // 2
    inv = 1.0 / (ROPE_BASE ** (jnp.arange(half, dtype=F32) / half))
    ang = jnp.arange(S).astype(F32)[:, None] * inv[None, :]
    c = jnp.cos(ang)
    s = jnp.sin(ang)
    cos = jnp.tile(jnp.concatenate([c, c], axis=1), (1, N_RET_HEADS))
    sin = jnp.tile(jnp.concatenate([-s, s], axis=1), (1, N_RET_HEADS))
    return cos, sin


def ret_fwd(q, k, v, gate, ng, tables, T=RET_TILE):
    H, S, dh = q.shape
    dmat, qdec, kdec, bdec = tables

    def body(q_ref, k_ref, v_ref, gt_ref, ng_ref, dm_ref, qd_ref, kd_ref, bd_ref, o_ref, y_ref, st_ref, s_sc):
        n = pl.program_id(1)

        @pl.when(n == 0)
        def _():
            s_sc[...] = jnp.zeros_like(s_sc)

        qv = q_ref[...]
        kv = k_ref[...]
        vv = v_ref[...]
        state = s_sc[...]
        st_ref[...] = state
        sc = (_dot_nt(qv, kv) * dm_ref[...]).astype(BF16)
        qd = (qv.astype(F32) * qd_ref[...]).astype(BF16)
        y = _dot(sc, vv) + _dot(qd, state.astype(BF16))
        y_ref[...] = y
        kd = (kv.astype(F32) * kd_ref[...]).astype(BF16)
        s_sc[...] = bd_ref[...] * state + _dot_tn(kd, vv)
        mu = jnp.mean(y, axis=-1, keepdims=True)
        yc = y - mu
        yn = yc * lax.rsqrt(jnp.mean(yc * yc, axis=-1, keepdims=True) + EPS)
        gt = gt_ref[...]
        o_ref[...] = gt * _sigmoid(gt) * (yn * ng_ref[...])

    blk = lambda h, n: (h, n, 0)
    head = lambda h, n: (h, 0, 0)
    return pl.pallas_call(
        body, name="ret_fwd",
        grid=(H, S // T),
        in_specs=[
            pl.BlockSpec((None, T, dh), blk),
            pl.BlockSpec((None, T, dh), blk),
            pl.BlockSpec((None, T, dh), blk),
            pl.BlockSpec((None, T, dh), blk),
            pl.BlockSpec((None, 1, dh), head),
            pl.BlockSpec((None, T, T), head),
            pl.BlockSpec((None, T, dh), head),
            pl.BlockSpec((None, T, dh), head),
            pl.BlockSpec((None, dh, dh), head),
        ],
        out_specs=[
            pl.BlockSpec((None, T, dh), blk),
            pl.BlockSpec((None, T, dh), blk),
            pl.BlockSpec((None, None, dh, dh), lambda h, n: (h, n, 0, 0)),
        ],
        out_shape=[
            jax.ShapeDtypeStruct((H, S, dh), F32),
            jax.ShapeDtypeStruct((H, S, dh), F32),
            jax.ShapeDtypeStruct((H, S // T, dh, dh), F32),
        ],
        scratch_shapes=[pltpu.VMEM((dh, dh), F32)],
        compiler_params=_params(2),
    )(q, k, v, gate, ng, dmat, qdec, kdec, bdec)


def ret_bwd(do, q, k, v, gate, ng, y, states, tables, T=RET_TILE):
    H, S, dh = q.shape
    nb = S // T
    dmat, qdec, kdec, bdec = tables

    def body(do_ref, q_ref, k_ref, v_ref, gt_ref, ng_ref, y_ref, st_ref, dm_ref, qd_ref, kd_ref, bd_ref,
             dq_ref, dk_ref, dv_ref, dgt_ref, dng_ref, u_sc):
        n = pl.program_id(1)

        @pl.when(n == 0)
        def _():
            u_sc[...] = jnp.zeros_like(u_sc)
            dng_ref[...] = jnp.zeros_like(dng_ref)

        yv = y_ref[...]
        mu = jnp.mean(yv, axis=-1, keepdims=True)
        yc = yv - mu
        rstd = lax.rsqrt(jnp.mean(yc * yc, axis=-1, keepdims=True) + EPS)
        yn = yc * rstd
        gt = gt_ref[...]
        sg = _sigmoid(gt)
        ngv = ng_ref[...]
        dout = do_ref[...]
        dgt_ref[...] = dout * (yn * ngv) * (sg * (1.0 + gt * (1.0 - sg)))
        dn = dout * (gt * sg)
        dng_ref[...] += jnp.sum(dn * yn, axis=0, keepdims=True)
        dyn = dn * ngv
        dy = rstd * (dyn - jnp.mean(dyn, axis=-1, keepdims=True) - yn * jnp.mean(dyn * yn, axis=-1, keepdims=True))
        dyb = dy.astype(BF16)

        qv = q_ref[...]
        kv = k_ref[...]
        vv = v_ref[...]
        dm = dm_ref[...]
        qdt = qd_ref[...]
        kdt = kd_ref[...]
        sb = st_ref[...].astype(BF16)
        u = u_sc[...]
        ub = u.astype(BF16)
        dqk = (_dot_nt(dyb, vv) * dm).astype(BF16)
        sc = (_dot_nt(qv, kv) * dm).astype(BF16)
        qd = (qv.astype(F32) * qdt).astype(BF16)
        kd = (kv.astype(F32) * kdt).astype(BF16)
        dq_ref[...] = _dot(dqk, kv) + qdt * _dot_nt(dyb, sb)
        dk_ref[...] = _dot_tn(dqk, qv) + kdt * _dot_nt(vv, ub)
        dv_ref[...] = _dot_tn(sc, dyb) + _dot(kd, ub)
        u_sc[...] = bd_ref[...] * u + _dot_tn(qd, dyb)

    blk = lambda h, n: (h, nb - 1 - n, 0)
    head = lambda h, n: (h, 0, 0)
    return pl.pallas_call(
        body, name="ret_bwd",
        grid=(H, nb),
        in_specs=[
            pl.BlockSpec((None, T, dh), blk),
            pl.BlockSpec((None, T, dh), blk),
            pl.BlockSpec((None, T, dh), blk),
            pl.BlockSpec((None, T, dh), blk),
            pl.BlockSpec((None, T, dh), blk),
            pl.BlockSpec((None, 1, dh), head),
            pl.BlockSpec((None, T, dh), blk),
            pl.BlockSpec((None, None, dh, dh), lambda h, n: (h, nb - 1 - n, 0, 0)),
            pl.BlockSpec((None, T, T), head),
            pl.BlockSpec((None, T, dh), head),
            pl.BlockSpec((None, T, dh), head),
            pl.BlockSpec((None, dh, dh), head),
        ],
        out_specs=[
            pl.BlockSpec((None, T, dh), blk),
            pl.BlockSpec((None, T, dh), blk),
            pl.BlockSpec((None, T, dh), blk),
            pl.BlockSpec((None, T, dh), blk),
            pl.BlockSpec((None, 1, dh), head),
        ],
        out_shape=[jax.ShapeDtypeStruct((H, S, dh), F32)] * 4 + [jax.ShapeDtypeStruct((H, 1, dh), F32)],
        scratch_shapes=[pltpu.VMEM((dh, dh), F32)],
        compiler_params=_params(2),
    )(do, q, k, v, gate, ng, y, states, dmat, qdec, kdec, bdec)


def loss_head(x, g, target, tm=ROW_TILE):
    S = x.shape[0]

    def body(x_ref, g_ref, t_ref, loss_ref, dx_ref, dg_ref):
        i = pl.program_id(0)
        xv = x_ref[...]
        gv = g_ref[...]
        _, xhat = _rms_stats(xv)
        err = xhat * gv - t_ref[...]
        part = 0.5 * jnp.sum(jnp.mean(err * err, axis=-1, keepdims=True), axis=0, keepdims=True)
        dx, _, dg = _rms_bwd(xv, gv, err * (1.0 / D_MODEL))
        dx_ref[...] = dx
        part = jnp.broadcast_to(part, (1, 128))

        @pl.when(i == 0)
        def _():
            loss_ref[...] = part
            dg_ref[...] = dg

        @pl.when(i > 0)
        def _():
            loss_ref[...] += part
            dg_ref[...] += dg

    row = lambda i: (i, 0)
    one = lambda i: (0, 0)
    return pl.pallas_call(
        body, name="loss_head",
        grid=(S // tm,),
        in_specs=[pl.BlockSpec((tm, D_MODEL), row), pl.BlockSpec((1, D_MODEL), one), pl.BlockSpec((tm, D_MODEL), row)],
        out_specs=[pl.BlockSpec((1, 128), one), pl.BlockSpec((tm, D_MODEL), row), pl.BlockSpec((1, D_MODEL), one)],
        out_shape=[
            jax.ShapeDtypeStruct((1, 128), F32),
            jax.ShapeDtypeStruct((S, D_MODEL), F32),
            jax.ShapeDtypeStruct((1, D_MODEL), F32),
        ],
        compiler_params=_params(1),
    )(x, g, target)


def adamw(parts, w, m, v, tr, transposed=False):
    L, R, C = w.shape
    nr = R // tr
    c1 = 1.0 / (1.0 - ADAM_B1 ** ADAM_STEP)
    c2 = 1.0 / (1.0 - ADAM_B2 ** ADAM_STEP)

    def body(*refs):
        p_refs = refs[:L]
        w_ref, m_ref, v_ref, g_ref, d_ref, mo_ref, vo_ref = refs[L:]
        l = pl.program_id(0)
        g = None
        for d in range(N_DEV):
            pd = p_refs[0][d].astype(F32)
            for ll in range(1, L):
                pd = jnp.where(l == ll, p_refs[ll][d].astype(F32), pd)
            g = pd if g is None else g + pd
        if transposed:
            g = g.T
        mn = ADAM_B1 * m_ref[...] + (1.0 - ADAM_B1) * g
        vn = ADAM_B2 * v_ref[...] + (1.0 - ADAM_B2) * (g * g)
        g_ref[...] = g
        mo_ref[...] = mn
        vo_ref[...] = vn
        d_ref[...] = -ADAM_LR * ((mn * c1) / (jnp.sqrt(vn * c2) + ADAM_EPS) + ADAM_WD * w_ref[...])

    def part_spec(ll):
        def block(l, i):
            return jnp.where(l == ll, i, jnp.where(l < ll, 0, nr - 1))
        if transposed:
            return pl.BlockSpec((N_DEV, C, tr), lambda l, i: (0, 0, block(l, i)))
        return pl.BlockSpec((N_DEV, tr, C), lambda l, i: (0, block(l, i), 0))

    blk = pl.BlockSpec((None, tr, C), lambda l, i: (l, i, 0))
    return pl.pallas_call(
        body, name="adamw",
        grid=(L, nr),
        in_specs=[part_spec(ll) for ll in range(L)] + [blk] * 3,
        out_specs=[blk] * 4,
        out_shape=[jax.ShapeDtypeStruct((L, R, C), F32)] * 4,
        compiler_params=_params(2),
    )(*parts, w, m, v)


def _my_id():
    return lax.axis_index("x") * 4 + lax.axis_index("y") * 2 + lax.axis_index("c")


def _peer(k):
    x, y, c = lax.axis_index("x"), lax.axis_index("y"), lax.axis_index("c")
    px = 1 - x if k & 4 else x
    py = 1 - y if k & 2 else y
    pc = 1 - c if k & 1 else c
    return (px, py, pc), px * 4 + py * 2 + pc


GATHER = "gather"
EXCHANGE = "exchange"


def _copies(kind, ins, outs, send_sems, recv_sems, local_sems, receive_side):
    me = _my_id()
    local, sends, recvs = [], [], []
    for t in range(len(ins)):
        src = ins[t] if kind == GATHER else ins[t].at[me]
        local.append(pltpu.make_async_copy(src, outs[t].at[me], local_sems.at[t]))
    for k in range(1, N_DEV):
        dev, pid = _peer(k)
        for t in range(len(ins)):
            sems = dict(send_sem=send_sems.at[t, k - 1], recv_sem=recv_sems.at[t, k - 1],
                        device_id=dev, device_id_type=pl.DeviceIdType.MESH)
            src = ins[t] if kind == GATHER else ins[t].at[pid]
            sends.append(pltpu.make_async_remote_copy(src_ref=src, dst_ref=outs[t].at[me], **sems))
            if receive_side:
                recvs.append(pltpu.make_async_remote_copy(src_ref=src, dst_ref=outs[t].at[pid], **sems))
    return local, sends, recvs


def _comm_start(kind, ins, outs, sems):
    local, sends, _ = _copies(kind, ins, outs, *sems, receive_side=False)
    for cp in local + sends:
        cp.start()


def _comm_wait(kind, ins, outs, sems):
    local, sends, recvs = _copies(kind, ins, outs, *sems, receive_side=True)
    for cp in recvs:
        cp.wait_recv()
    for cp in sends:
        cp.wait_send()
    for cp in local:
        cp.wait()


def _comm_shapes(kind, arrays):
    n = len(arrays)
    out_shape = [jax.ShapeDtypeStruct(((N_DEV,) + a.shape) if kind == GATHER else a.shape, a.dtype) for a in arrays]
    sems = [pltpu.SemaphoreType.DMA((n, N_DEV - 1)), pltpu.SemaphoreType.DMA((n, N_DEV - 1)),
            pltpu.SemaphoreType.DMA((n,))]
    return out_shape, sems


def communicate(kind, arrays):
    n = len(arrays)

    def body(*refs):
        ins, outs, sems = refs[:n], refs[n:2 * n], refs[2 * n:]
        _comm_start(kind, ins, outs, sems)
        _comm_wait(kind, ins, outs, sems)

    out_shape, sems = _comm_shapes(kind, arrays)
    any_spec = pl.BlockSpec(memory_space=pl.ANY)
    return pl.pallas_call(
        body, name=kind, in_specs=[any_spec] * n, out_specs=[any_spec] * n, out_shape=out_shape, scratch_shapes=sems,
    )(*arrays)


def gather_two_level(arrays):
    n = len(arrays)

    def body(*refs):
        ins, outs = refs[:n], refs[n:2 * n]
        send_sems, recv_sems, local_sems = refs[2 * n:]
        x, y, c = lax.axis_index("x"), lax.axis_index("y"), lax.axis_index("c")
        me, sibling = (x, y, c), (x, y, 1 - c)
        chips = [(1 - x, y), (x, 1 - y), (1 - x, 1 - y)]

        def slot(px, py, pc):
            return px * 4 + py * 2 + pc

        def copy(t, k, src, owner, to):
            return pltpu.make_async_remote_copy(
                src_ref=src, dst_ref=outs[t].at[slot(*owner)], send_sem=send_sems.at[t, k], recv_sem=recv_sems.at[t, k],
                device_id=to, device_id_type=pl.DeviceIdType.MESH)

        local = [pltpu.make_async_copy(ins[t], outs[t].at[slot(*me)], local_sems.at[t]) for t in range(n)]
        first = [copy(t, 0, ins[t], me, sibling) for t in range(n)]
        first += [copy(t, 1 + j, ins[t], me, (*chip, c)) for j, chip in enumerate(chips) for t in range(n)]
        for cp in local + first:
            cp.start()
        passed = []
        for j, chip in enumerate(chips):
            for t in range(n):
                copy(t, 1 + j, ins[t], (*chip, c), me).wait_recv()
                cp = copy(t, 4 + j, outs[t].at[slot(*chip, c)], (*chip, c), sibling)
                cp.start()
                passed.append(cp)
        for t in range(n):
            copy(t, 0, ins[t], sibling, me).wait_recv()
            for j, chip in enumerate(chips):
                copy(t, 4 + j, ins[t], (*chip, 1 - c), me).wait_recv()
        for cp in first + passed:
            cp.wait_send()
        for cp in local:
            cp.wait()

    out_shape, sems = _comm_shapes(GATHER, arrays)
    any_spec = pl.BlockSpec(memory_space=pl.ANY)
    return pl.pallas_call(
        body, name="gather_two_level", in_specs=[any_spec] * n, out_specs=[any_spec] * n, out_shape=out_shape,
        scratch_shapes=sems,
    )(*arrays)


def _call(body, operands, comm, *, name, grid, in_specs, out_specs, out_shape, scratch_shapes):
    if comm is None:
        outs = pl.pallas_call(body, name=name, grid=grid, in_specs=in_specs, out_specs=out_specs, out_shape=out_shape,
                              scratch_shapes=scratch_shapes, compiler_params=_params(len(grid)))(*operands)
        return outs, []
    kind, arrays = comm
    n, n_in, n_out, n_sc = len(arrays), len(in_specs), len(out_specs), len(scratch_shapes)

    def carrier(*refs):
        ins, cins = refs[:n_in], refs[n_in:n_in + n]
        refs = refs[n_in + n:]
        outs, couts = refs[:n_out], refs[n_out:n_out + n]
        scratch, sems = refs[n_out + n:n_out + n + n_sc], refs[n_out + n + n_sc:]
        steps = [pl.program_id(a) for a in range(len(grid))]
        first = functools.reduce(jnp.logical_and, [s == 0 for s in steps])
        last = functools.reduce(jnp.logical_and, [s == g - 1 for s, g in zip(steps, grid)])

        @pl.when(first)
        def _():
            _comm_start(kind, cins, couts, sems)

        body(*ins, *outs, *scratch)

        @pl.when(last)
        def _():
            _comm_wait(kind, cins, couts, sems)

    comm_shape, sems = _comm_shapes(kind, arrays)
    any_spec = pl.BlockSpec(memory_space=pl.ANY)
    outs = pl.pallas_call(
        carrier, name=f"{name}_{kind}", grid=grid,
        in_specs=list(in_specs) + [any_spec] * n,
        out_specs=list(out_specs) + [any_spec] * n,
        out_shape=list(out_shape) + comm_shape,
        scratch_shapes=list(scratch_shapes) + sems,
        compiler_params=_params(len(grid)),
    )(*operands, *arrays)
    return outs[:n_out], outs[n_out:]


def _row(v):
    return v.reshape(1, -1)


def _pad_taps(cw):
    return jnp.concatenate([cw, jnp.zeros((CONV_HALO - CONV_WIDTH, D_CONV), F32)], axis=0)


COL_SHARDED = ("ffn1_w_in", "mix_w_in", "ffn2_w_in")
ROW_SHARDED = ("ffn1_w_out", "mix_w_out", "ffn2_w_out")
SMALL = ("ffn1_norm", "mix_norm", "conv_b", "conv_ln_g", "conv_ln_b", "ret_norm_g", "ffn2_norm", "final_norm")
WEIGHTS = ("ffn1_norm", "ffn1_w_in", "ffn1_w_out", "mix_norm", "mix_w_in", "conv_w", "conv_b", "conv_ln_g",
           "conv_ln_b", "ret_norm_g", "mix_w_out", "ffn2_norm", "ffn2_w_in", "ffn2_w_out", "final_norm")
SMALL_ROWS = 32

FFN1 = ("ffn1_w_in", "ffn1_w_out")
MIX = ("mix_w_in", "mix_w_out")
FFN2 = ("ffn2_w_in", "ffn2_w_out")
STAGE_A = [(n, 0) for n in FFN1]
STAGE_B = [(n, 0) for n in MIX] + [("conv_w", None)]
STAGE_C = [(n, 0) for n in FFN2] + [(n, 1) for n in FFN1 + MIX + FFN2]
STAGE_D = [(n, 1) for n in FFN2]
STAGE_E = [(n, 1) for n in MIX + FFN1] + [(n, 0) for n in FFN2]
STAGE_F = [(n, 0) for n in MIX]
STAGE_G = [("ffn1_w_in", 0)]
STAGE_H = [("ffn1_w_out", 0)]


def _natural(name, got):
    if name == "conv_w":
        return got.transpose(1, 2, 0, 3).reshape(DEPTH, CONV_WIDTH, D_CONV)
    return got.reshape(-1, D_MODEL)


def _by_device(grad):
    return grad.reshape(N_DEV, -1, D_MODEL)


def _pack_small(g):
    flat = jnp.concatenate([g[n].reshape(-1) for n in SMALL] + [g["conv_w"].reshape(-1)])
    flat = jnp.concatenate([flat, jnp.zeros((SMALL_ROWS * D_MODEL - flat.shape[0],), F32)])
    return flat.reshape(SMALL_ROWS, D_MODEL)


def _unpack_small(buf, like):
    flat = buf.reshape(-1)
    out, off = {}, 0
    for n in SMALL:
        size = int(np.prod(like[n].shape))
        out[n] = flat[off:off + size].reshape(like[n].shape)
        off += size
    size = DEPTH * CONV_WIDTH * D_CONV
    out["conv_w"] = flat[off:off + size].reshape(DEPTH, CONV_WIDTH, D_CONV)
    return out


def kernel(x, ffn1_norm, ffn1_w_in, ffn1_w_out, mix_norm, mix_w_in, conv_w, conv_b, conv_ln_g, conv_ln_b, ret_norm_g, mix_w_out, ffn2_norm, ffn2_w_in, ffn2_w_out, final_norm, loss_target, m_ffn1_norm, m_ffn1_w_in, m_ffn1_w_out, m_mix_norm, m_mix_w_in, m_conv_w, m_conv_b, m_conv_ln_g, m_conv_ln_b, m_ret_norm_g, m_mix_w_out, m_ffn2_norm, m_ffn2_w_in, m_ffn2_w_out, m_final_norm, v_ffn1_norm, v_ffn1_w_in, v_ffn1_w_out, v_mix_norm, v_mix_w_in, v_conv_w, v_conv_b, v_conv_ln_g, v_conv_ln_b, v_ret_norm_g, v_mix_w_out, v_ffn2_norm, v_ffn2_w_in, v_ffn2_w_out, v_final_norm):
    args = locals()
    w = {n: args[n] for n in WEIGHTS}
    m = {n: args["m_" + n] for n in WEIGHTS}
    v = {n: args["v_" + n] for n in WEIGHTS}
    me = _my_id()
    x = x[0]
    target = loss_target[0]
    S = x.shape[0]
    cos, sin = _rope_tables(S)
    tables = _ret_tables()

    full = {}

    def shard(n, l):
        if n == "conv_w":
            return w[n]
        return (w[n][l].T if n in COL_SHARDED else w[n][l]).astype(BF16)

    def gather(keys):
        return GATHER, [shard(n, l) for n, l in keys]

    def gathered(keys, got):
        for (n, l), g in zip(keys, got):
            full[(n, l)] = _natural(n, g)

    gathered(STAGE_A, gather_two_level(gather(STAGE_A)[1]))

    saved = []
    for l in range(DEPTH):
        sv = {"x0": x}
        (x, sv["gate1"], sv["up1"]), got = ffn_fwd(x, _row(w["ffn1_norm"][l]), full[("ffn1_w_in", l)],
                                                   full[("ffn1_w_out", l)], gather(STAGE_B) if l == 0 else None)
        gathered(STAGE_B if l == 0 else [], got)
        sv["x1"] = x
        (sv["u"], sv["q_sb"], sv["k_sb"], sv["v_sb"], sv["qt_sb"], sv["q_r"], sv["k_r"], sv["v_r"],
         sv["g_r"]) = mix_in_fwd(x, _row(w["mix_norm"][l]), full[("mix_w_in", l)], cos, sin)
        cw = _pad_taps(full[("conv_w", None)][l])
        y_conv, sv["ypre"] = conv_fwd(sv["u"], cw, _row(w["conv_b"][l]), _row(w["conv_ln_g"][l]), _row(w["conv_ln_b"][l]))
        (o_sb, sv["tot"]), got = sb_fwd(sv["q_sb"], sv["k_sb"], sv["v_sb"], gather(STAGE_C) if l == 0 else None)
        gathered(STAGE_C if l == 0 else [], got)
        ng = w["ret_norm_g"][l].reshape(N_RET_HEADS, 1, HEAD_DIM)
        o_r, sv["y_r"], sv["states"] = ret_fwd(sv["q_r"], sv["k_r"], sv["v_r"], sv["g_r"], ng, tables)
        x, sv["ycat"] = mix_out_fwd(y_conv, o_sb, o_r, full[("mix_w_out", l)], x)
        sv["x2"] = x
        (x, sv["gate2"], sv["up2"]), _ = ffn_fwd(x, _row(w["ffn2_norm"][l]), full[("ffn2_w_in", l)],
                                                 full[("ffn2_w_out", l)])
        saved.append(sv)

    loss_acc, dx, dg_final = loss_head(x, _row(w["final_norm"]), target)
    loss = lax.psum(loss_acc[0, 0], ("x", "y", "c"))

    g = {"final_norm": dg_final.reshape(D_MODEL)}
    received = {}

    def exchange(keys, extra=(), dtype=F32):
        return EXCHANGE, [_by_device(g[(n, l)]).astype(dtype) for n, l in keys] + list(extra)

    def exchanged(keys, got):
        for key, p in zip(keys, got):
            received[key] = p

    def ffn_back(dx, x_in, gate, up, norm, names, l, comm=None):
        (dx, h, dyh, dgate, dup, hid, dg), got = ffn_bwd(dx, x_in, _row(norm), gate, up, full[(names[0], l)],
                                                         full[(names[1], l)], comm)
        g[(names[0], l)] = matmul_tn([dgate, dup], h, FF_TILE, D_MODEL, name="ffn_dw_in")
        if [(names[0], l)] == STAGE_G:
            g[(names[1], l)], got_g = matmul_tn([hid], dyh, FF_TILE, D_MODEL, name="ffn_dw_out",
                                                comm=exchange(STAGE_G, dtype=BF16))
            exchanged(STAGE_G, got_g)
        else:
            g[(names[1], l)] = matmul_tn([hid], dyh, FF_TILE, D_MODEL, name="ffn_dw_out")
        return dx, dg.reshape(D_MODEL), got

    for l in reversed(range(DEPTH)):
        sv = saved[l]
        dx, g[("ffn2_norm", l)], _ = ffn_back(dx, sv["x2"], sv["gate2"], sv["up2"], w["ffn2_norm"][l], FFN2, l)
        dxb, dy_conv, do_sb, dot_sb, do_r = mix_out_bwd(dx, full[("mix_w_out", l)])
        g[("mix_w_out", l)] = matmul_tn([sv["ycat"]], dxb, D_MODEL, D_MODEL, name="mix_dw_out")
        cw = _pad_taps(full[("conv_w", None)][l])
        du_conv, dcw, dsm = conv_bwd(dy_conv, sv["ypre"], sv["u"], cw, _row(w["conv_ln_g"][l]), _row(w["conv_ln_b"][l]))
        g[("conv_w", l)] = dcw[:CONV_WIDTH]
        g[("conv_b", l)], g[("conv_ln_g", l)], g[("conv_ln_b", l)] = dsm[0], dsm[1], dsm[2]
        stage = STAGE_D if l == DEPTH - 1 else STAGE_E
        (dq_sb, dk_t, dv_t), got = sb_bwd(sv["q_sb"], sv["k_sb"], sv["v_sb"], do_sb, sv["qt_sb"], dot_sb, sv["tot"],
                                          exchange(stage))
        exchanged(stage, got)
        ng = w["ret_norm_g"][l].reshape(N_RET_HEADS, 1, HEAD_DIM)
        dq_r, dk_r, dv_r, dg_r, dng = ret_bwd(do_r, sv["q_r"], sv["k_r"], sv["v_r"], sv["g_r"], ng, sv["y_r"],
                                              sv["states"], tables)
        g[("ret_norm_g", l)] = dng.reshape(D_RET)
        dx, h, dproj, dg = mix_in_bwd(du_conv, dq_sb, dk_t, dv_t, dq_r, dk_r, dv_r, dg_r, cos, sin,
                                      full[("mix_w_in", l)], sv["x1"], _row(w["mix_norm"][l]), dx)
        g[("mix_norm", l)] = dg.reshape(D_MODEL)
        g[("mix_w_in", l)] = matmul_tn([dproj], h, D_MODEL, D_MODEL, name="mix_dw_in")
        dx, g[("ffn1_norm", l)], got = ffn_back(dx, sv["x0"], sv["gate1"], sv["up1"], w["ffn1_norm"][l], FFN1, l,
                                                exchange(STAGE_F) if l == 0 else None)
        exchanged(STAGE_F if l == 0 else [], got)
    grad_x = dx

    small_names = [n for n in SMALL if n != "final_norm"] + ["conv_w"]
    gs = {n: jnp.stack([g[(n, l)] for l in range(DEPTH)], axis=0) for n in small_names}
    gs["final_norm"] = g["final_norm"]
    small = _pack_small(gs)
    got = communicate(*exchange(STAGE_H, [jnp.broadcast_to(small[None], (N_DEV, SMALL_ROWS, D_MODEL))], dtype=BF16))
    exchanged(STAGE_H, got[:-1])

    grad, delta, new_m, new_v = {}, {}, {}, {}
    for n in COL_SHARDED + ROW_SHARDED:
        rows = w[n].shape[1]
        col = n in COL_SHARDED
        grad[n], delta[n], new_m[n], new_v[n] = adamw([received[(n, l)] for l in range(DEPTH)], w[n], m[n], v[n],
                                                      tr=128 if col else rows // 2, transposed=col)

    def small_pack(d):
        mine = dict(d)
        cwf = jnp.zeros((DEPTH, CONV_WIDTH, D_CONV), F32)
        mine["conv_w"] = lax.dynamic_update_slice(cwf, d["conv_w"], (0, 0, me * (D_CONV // N_DEV)))
        return _pack_small(mine)

    outs = adamw([got[-1]], small_pack(w)[None], small_pack(m)[None], small_pack(v)[None], tr=SMALL_ROWS)
    for dst, o in zip((grad, delta, new_m, new_v), outs):
        un = _unpack_small(o[0], w)
        un["conv_w"] = lax.dynamic_slice(un["conv_w"], (0, 0, me * (D_CONV // N_DEV)),
                                         (DEPTH, CONV_WIDTH, D_CONV // N_DEV))
        dst.update(un)

    return (loss, grad_x[None], *[grad[n] for n in WEIGHTS], *[delta[n] for n in WEIGHTS],
            *[new_m[n] for n in WEIGHTS], *[new_v[n] for n in WEIGHTS])
```

```python
import functools

import numpy as np
import jax
import jax.numpy as jnp
from jax import lax
from jax.experimental import pallas as pl
from jax.experimental.pallas import tpu as pltpu

F32 = jnp.float32
BF16 = jnp.bfloat16

D_MODEL = 1024
DEPTH = 2
D_FF = 2816
D_CONV = 256
CONV_WIDTH = 31
CONV_HALO = 32
D_SB = 512
N_SB_HEADS = 8
D_RET = 256
N_RET_HEADS = 4
HEAD_DIM = 64
D_IN_PROJ = 3072
ROPE_BASE = 10000.0
EPS = 1e-6
N_DEV = 8

ADAM_LR = 0.001
ADAM_B1 = 0.9
ADAM_B2 = 0.999
ADAM_EPS = 1e-08
ADAM_WD = 0.01
ADAM_STEP = 10

VMEM_LIMIT = 56 * 1024 * 1024
ROW_TILE = 512
FF_TILE = 1408
FF_FWD_CHUNK = 256
FF_BWD_CHUNK = 2816
SB_TILE = 256
SB_ROWS = 512
SB_FWD_ROWS = 1024
RET_TILE = 512
CONV_TILE = 256

NT_DIMS = (((1,), (1,)), ((), ()))
TN_DIMS = (((0,), (0,)), ((), ()))


def _params(n_axes, vmem=VMEM_LIMIT):
    return pltpu.CompilerParams(dimension_semantics=("arbitrary",) * n_axes, vmem_limit_bytes=vmem)


def _dot(a, b):
    return jnp.dot(a, b, preferred_element_type=F32)


def _dot_nt(a, b):
    return lax.dot_general(a, b, NT_DIMS, preferred_element_type=F32)


def _dot_tn(a, b):
    return lax.dot_general(a, b, TN_DIMS, preferred_element_type=F32)


def _sigmoid(z):
    return 1.0 / (1.0 + jnp.exp(-z))


def _rms_stats(xv):
    r = lax.rsqrt(jnp.mean(xv * xv, axis=-1, keepdims=True) + EPS)
    return r, xv * r


def _rms_bwd(xv, g, dh):
    r, xhat = _rms_stats(xv)
    dxhat = dh * g
    dx = r * (dxhat - xhat * jnp.mean(dxhat * xhat, axis=-1, keepdims=True))
    dg = jnp.sum(dh * xhat, axis=0, keepdims=True)
    return dx, (xhat * g).astype(BF16), dg


def ffn_fwd(x, g, w_in, w_out, comm=None, tm=ROW_TILE):
    S = x.shape[0]
    chunk = FF_FWD_CHUNK

    def body(x_ref, g_ref, w_ref, wo_ref, y_ref, gate_ref, up_ref):
        xv = x_ref[...]
        _, xhat = _rms_stats(xv)
        h = (xhat * g_ref[...]).astype(BF16)
        acc = None
        for j in range(D_FF // chunk):
            cols = pl.ds(j * chunk, chunk)
            gt = _dot_nt(h, w_ref[cols, :])
            up = _dot_nt(h, w_ref[pl.ds(D_FF + j * chunk, chunk), :])
            gate_ref[:, cols] = gt.astype(BF16)
            up_ref[:, cols] = up.astype(BF16)
            part = _dot((gt * _sigmoid(gt) * up).astype(BF16), wo_ref[cols, :])
            acc = part if acc is None else acc + part
        y_ref[...] = xv + 0.5 * acc

    row = lambda i: (i, 0)
    one = lambda i: (0, 0)
    resident = pl.Buffered(1)
    return _call(
        body, (x, g, w_in, w_out), comm, name="ffn_fwd",
        grid=(S // tm,),
        in_specs=[
            pl.BlockSpec((tm, D_MODEL), row),
            pl.BlockSpec((1, D_MODEL), one),
            pl.BlockSpec((2 * D_FF, D_MODEL), one, pipeline_mode=resident),
            pl.BlockSpec((D_FF, D_MODEL), one, pipeline_mode=resident),
        ],
        out_specs=[
            pl.BlockSpec((tm, D_MODEL), row),
            pl.BlockSpec((tm, D_FF), row),
            pl.BlockSpec((tm, D_FF), row),
        ],
        out_shape=[
            jax.ShapeDtypeStruct((S, D_MODEL), F32),
            jax.ShapeDtypeStruct((S, D_FF), BF16),
            jax.ShapeDtypeStruct((S, D_FF), BF16),
        ],
        scratch_shapes=[],
    )


def ffn_bwd(dy, x, g, gate, up, w_in, w_out, comm=None, tm=ROW_TILE // 2):
    S = x.shape[0]
    chunk = FF_BWD_CHUNK

    def body(dy_ref, x_ref, g_ref, gate_ref, up_ref, w_ref, wo_ref,
             dx_ref, h_ref, dyh_ref, dgate_ref, dup_ref, hid_ref, dg_ref):
        i = pl.program_id(0)
        d2 = (0.5 * dy_ref[...]).astype(BF16)
        dyh_ref[...] = d2
        dh = None
        for j in range(D_FF // chunk):
            cols = pl.ds(j * chunk, chunk)
            dhid = _dot_nt(d2, wo_ref[cols, :])
            gt = gate_ref[:, cols].astype(F32)
            u = up_ref[:, cols].astype(F32)
            sig = _sigmoid(gt)
            sl = gt * sig
            dgate = (dhid * u * (sig * (1.0 + gt * (1.0 - sig)))).astype(BF16)
            dup = (dhid * sl).astype(BF16)
            dgate_ref[:, cols] = dgate
            dup_ref[:, cols] = dup
            hid_ref[:, cols] = (sl * u).astype(BF16)
            part = _dot(dgate, w_ref[cols, :]) + _dot(dup, w_ref[pl.ds(D_FF + j * chunk, chunk), :])
            dh = part if dh is None else dh + part
        dx, h, dg = _rms_bwd(x_ref[...], g_ref[...], dh)
        dx_ref[...] = dy_ref[...] + dx
        h_ref[...] = h

        @pl.when(i == 0)
        def _():
            dg_ref[...] = dg

        @pl.when(i > 0)
        def _():
            dg_ref[...] += dg

    row = lambda i: (i, 0)
    one = lambda i: (0, 0)
    resident = pl.Buffered(1)
    return _call(
        body, (dy, x, g, gate, up, w_in, w_out), comm, name="ffn_bwd",
        grid=(S // tm,),
        in_specs=[
            pl.BlockSpec((tm, D_MODEL), row),
            pl.BlockSpec((tm, D_MODEL), row),
            pl.BlockSpec((1, D_MODEL), one),
            pl.BlockSpec((tm, D_FF), row),
            pl.BlockSpec((tm, D_FF), row),
            pl.BlockSpec((2 * D_FF, D_MODEL), one, pipeline_mode=resident),
            pl.BlockSpec((D_FF, D_MODEL), one, pipeline_mode=resident),
        ],
        out_specs=[
            pl.BlockSpec((tm, D_MODEL), row),
            pl.BlockSpec((tm, D_MODEL), row),
            pl.BlockSpec((tm, D_MODEL), row),
            pl.BlockSpec((tm, D_FF), row),
            pl.BlockSpec((tm, D_FF), row),
            pl.BlockSpec((tm, D_FF), row),
            pl.BlockSpec((1, D_MODEL), one),
        ],
        out_shape=[
            jax.ShapeDtypeStruct((S, D_MODEL), F32),
            jax.ShapeDtypeStruct((S, D_MODEL), BF16),
            jax.ShapeDtypeStruct((S, D_MODEL), BF16),
            jax.ShapeDtypeStruct((S, D_FF), BF16),
            jax.ShapeDtypeStruct((S, D_FF), BF16),
            jax.ShapeDtypeStruct((S, D_FF), BF16),
            jax.ShapeDtypeStruct((1, D_MODEL), F32),
        ],
        scratch_shapes=[],
    )


def matmul_tn(a_list, b, ta, tn, tk=4 * ROW_TILE, name="matmul_tn", comm=None):
    S, ka = a_list[0].shape
    nb = b.shape[1]
    assert S % tk == 0 and ka % ta == 0 and nb % tn == 0
    per = ka // ta

    def body(*refs):
        a_refs, b_ref, o_ref = refs[:-2], refs[-2], refs[-1]
        i = pl.program_id(0)
        k = pl.program_id(2)

        @pl.when(k == 0)
        def _():
            o_ref[...] = jnp.zeros_like(o_ref)

        for t, a_ref in enumerate(a_refs):
            @pl.when(lax.div(i, per) == t)
            def _(a_ref=a_ref):
                o_ref[...] += _dot_tn(a_ref[...], b_ref[...])

    def a_spec(t):
        def index(i, j, k):
            mine = lax.div(i, per) == t
            return jnp.where(mine, k, 0), jnp.where(mine, i - t * per, 0)
        return pl.BlockSpec((tk, ta), index)

    (out,), got = _call(
        body, (*a_list, b), comm, name=name,
        grid=(per * len(a_list), nb // tn, S // tk),
        in_specs=[a_spec(t) for t in range(len(a_list))] + [pl.BlockSpec((tk, tn), lambda i, j, k: (k, j))],
        out_specs=[pl.BlockSpec((ta, tn), lambda i, j, k: (i, j))],
        out_shape=[jax.ShapeDtypeStruct((ka * len(a_list), nb), F32)],
        scratch_shapes=[],
    )
    return (out, got) if comm is not None else out


SB_COLS = (2 * D_CONV, 2 * D_CONV + D_SB, 2 * D_CONV + 2 * D_SB)
RET_COLS = tuple(2 * D_CONV + 3 * D_SB + j * D_RET for j in range(4))


def _swap_halves(x):
    n = x.shape[1]
    lane = lax.broadcasted_iota(jnp.int32, x.shape, 1)
    first = (lane % HEAD_DIM) < (HEAD_DIM // 2)
    return jnp.where(first, pltpu.roll(x, n - HEAD_DIM // 2, 1), pltpu.roll(x, HEAD_DIM // 2, 1))


def _head(x, h):
    return x[:, h * HEAD_DIM:(h + 1) * HEAD_DIM]


def _heads_spec(n_heads, tm):
    return pl.BlockSpec((n_heads, tm, HEAD_DIM), lambda i: (0, i, 0))


def mix_in_fwd(x, g, w, cos, sin, tm=ROW_TILE):
    S = x.shape[0]

    def body(x_ref, g_ref, w_ref, c_ref, s_ref, u_ref, q_ref, k_ref, v_ref, qt_ref, qr_ref, kr_ref, vr_ref, gr_ref):
        _, xhat = _rms_stats(x_ref[...])
        proj = _dot_nt((xhat * g_ref[...]).astype(BF16), w_ref[...])
        u_ref[...] = proj[:, :2 * D_CONV]
        for h in range(N_SB_HEADS):
            q = (_head(proj[:, SB_COLS[0]:SB_COLS[1]], h) * 0.125).astype(BF16)
            q_ref[h] = q
            qt_ref[h] = q.T
            k_ref[h] = _head(proj[:, SB_COLS[1]:SB_COLS[2]], h).astype(BF16)
            v_ref[h] = _head(proj[:, SB_COLS[2]:RET_COLS[0]], h).astype(BF16)
        c = c_ref[...]
        s = s_ref[...]
        qv = proj[:, RET_COLS[0]:RET_COLS[1]]
        kv = proj[:, RET_COLS[1]:RET_COLS[2]]
        q_rot = ((qv * c + _swap_halves(qv) * s) * 0.125).astype(BF16)
        k_rot = (kv * c + _swap_halves(kv) * s).astype(BF16)
        for h in range(N_RET_HEADS):
            qr_ref[h] = _head(q_rot, h)
            kr_ref[h] = _head(k_rot, h)
            vr_ref[h] = _head(proj[:, RET_COLS[2]:RET_COLS[3]], h).astype(BF16)
            gr_ref[h] = _head(proj[:, RET_COLS[3]:], h)

    row = lambda i: (i, 0)
    one = lambda i: (0, 0)
    sb = jax.ShapeDtypeStruct((N_SB_HEADS, S, HEAD_DIM), BF16)
    ret = jax.ShapeDtypeStruct((N_RET_HEADS, S, HEAD_DIM), BF16)
    return pl.pallas_call(
        body, name="mix_in_fwd",
        grid=(S // tm,),
        in_specs=[
            pl.BlockSpec((tm, D_MODEL), row),
            pl.BlockSpec((1, D_MODEL), one),
            pl.BlockSpec((D_IN_PROJ, D_MODEL), one, pipeline_mode=pl.Buffered(1)),
            pl.BlockSpec((tm, D_RET), row),
            pl.BlockSpec((tm, D_RET), row),
        ],
        out_specs=[
            pl.BlockSpec((tm, 2 * D_CONV), row),
            _heads_spec(N_SB_HEADS, tm), _heads_spec(N_SB_HEADS, tm), _heads_spec(N_SB_HEADS, tm),
            pl.BlockSpec((N_SB_HEADS, HEAD_DIM, tm), lambda i: (0, 0, i)),
            _heads_spec(N_RET_HEADS, tm), _heads_spec(N_RET_HEADS, tm), _heads_spec(N_RET_HEADS, tm),
            _heads_spec(N_RET_HEADS, tm),
        ],
        out_shape=[
            jax.ShapeDtypeStruct((S, 2 * D_CONV), F32), sb, sb, sb,
            jax.ShapeDtypeStruct((N_SB_HEADS, HEAD_DIM, S), BF16),
            ret, ret, ret, jax.ShapeDtypeStruct((N_RET_HEADS, S, HEAD_DIM), F32),
        ],
        compiler_params=_params(1),
    )(x, g, w, cos, sin)


def mix_in_bwd(du, dq, dkt, dvt, dqr, dkr, dvr, dgr, cos, sin, w, x, g, dy, tm=SB_TILE):
    S = x.shape[0]
    assert dkt.shape[-1] == tm

    def body(du_ref, dq_ref, dkt_ref, dvt_ref, dqr_ref, dkr_ref, dvr_ref, dgr_ref, c_ref, s_ref, w_ref, x_ref, g_ref,
             dy_ref, dx_ref, h_ref, dp_ref, dg_ref):
        i = pl.program_id(0)
        sb_heads = range(N_SB_HEADS)
        ret_heads = range(N_RET_HEADS)
        c = c_ref[...]
        s = s_ref[...]
        dq_rot = jnp.concatenate([dqr_ref[h] for h in ret_heads], axis=1) * 0.125
        dk_rot = jnp.concatenate([dkr_ref[h] for h in ret_heads], axis=1)
        dproj = jnp.concatenate([
            du_ref[...].astype(BF16),
            jnp.concatenate([dq_ref[h] * 0.125 for h in sb_heads], axis=1).astype(BF16),
            jnp.concatenate([dkt_ref[h, 0].T for h in sb_heads], axis=1).astype(BF16),
            jnp.concatenate([dvt_ref[h, 0].T for h in sb_heads], axis=1).astype(BF16),
            (dq_rot * c - _swap_halves(dq_rot) * s).astype(BF16),
            (dk_rot * c - _swap_halves(dk_rot) * s).astype(BF16),
            jnp.concatenate([dvr_ref[h] for h in ret_heads], axis=1).astype(BF16),
            jnp.concatenate([dgr_ref[h] for h in ret_heads], axis=1).astype(BF16)], axis=1)
        dp_ref[...] = dproj
        dh = _dot(dproj, w_ref[...])
        dx, h, dg = _rms_bwd(x_ref[...], g_ref[...], dh)
        dx_ref[...] = dy_ref[...] + dx
        h_ref[...] = h

        @pl.when(i == 0)
        def _():
            dg_ref[...] = dg

        @pl.when(i > 0)
        def _():
            dg_ref[...] += dg

    row = lambda i: (i, 0)
    one = lambda i: (0, 0)
    tiles = pl.BlockSpec((N_SB_HEADS, 1, HEAD_DIM, tm), lambda i: (0, i, 0, 0))
    return pl.pallas_call(
        body, name="mix_in_bwd",
        grid=(S // tm,),
        in_specs=[
            pl.BlockSpec((tm, 2 * D_CONV), row),
            _heads_spec(N_SB_HEADS, tm), tiles, tiles,
            _heads_spec(N_RET_HEADS, tm), _heads_spec(N_RET_HEADS, tm), _heads_spec(N_RET_HEADS, tm),
            _heads_spec(N_RET_HEADS, tm),
            pl.BlockSpec((tm, D_RET), row),
            pl.BlockSpec((tm, D_RET), row),
            pl.BlockSpec((D_IN_PROJ, D_MODEL), one, pipeline_mode=pl.Buffered(1)),
            pl.BlockSpec((tm, D_MODEL), row),
            pl.BlockSpec((1, D_MODEL), one),
            pl.BlockSpec((tm, D_MODEL), row),
        ],
        out_specs=[
            pl.BlockSpec((tm, D_MODEL), row),
            pl.BlockSpec((tm, D_MODEL), row),
            pl.BlockSpec((tm, D_IN_PROJ), row),
            pl.BlockSpec((1, D_MODEL), one),
        ],
        out_shape=[
            jax.ShapeDtypeStruct((S, D_MODEL), F32),
            jax.ShapeDtypeStruct((S, D_MODEL), BF16),
            jax.ShapeDtypeStruct((S, D_IN_PROJ), BF16),
            jax.ShapeDtypeStruct((1, D_MODEL), F32),
        ],
        compiler_params=_params(1),
    )(du, dq, dkt, dvt, dqr, dkr, dvr, dgr, cos, sin, w, x, g, dy)


def mix_out_fwd(y_conv, o_sb, o_ret, w, x, tm=ROW_TILE):
    S = x.shape[0]

    def body(yc_ref, sb_ref, rt_ref, w_ref, x_ref, o_ref, ycat_ref):
        ycat = jnp.concatenate(
            [yc_ref[...]] + [sb_ref[h].astype(BF16) for h in range(N_SB_HEADS)]
            + [rt_ref[h].astype(BF16) for h in range(N_RET_HEADS)], axis=1)
        ycat_ref[...] = ycat
        o_ref[...] = x_ref[...] + _dot(ycat, w_ref[...])

    row = lambda i: (i, 0)
    return pl.pallas_call(
        body, name="mix_out_fwd",
        grid=(S // tm,),
        in_specs=[
            pl.BlockSpec((tm, D_CONV), row),
            _heads_spec(N_SB_HEADS, tm),
            _heads_spec(N_RET_HEADS, tm),
            pl.BlockSpec((D_MODEL, D_MODEL), lambda i: (0, 0)),
            pl.BlockSpec((tm, D_MODEL), row),
        ],
        out_specs=[pl.BlockSpec((tm, D_MODEL), row), pl.BlockSpec((tm, D_MODEL), row)],
        out_shape=[jax.ShapeDtypeStruct((S, D_MODEL), F32), jax.ShapeDtypeStruct((S, D_MODEL), BF16)],
        compiler_params=_params(1),
    )(y_conv, o_sb, o_ret, w, x)


def mix_out_bwd(dy, w, tm=ROW_TILE):
    S = dy.shape[0]

    def body(dy_ref, w_ref, dyb_ref, dc_ref, do_ref, dot_ref, dr_ref):
        d = dy_ref[...].astype(BF16)
        dyb_ref[...] = d
        dycat = _dot_nt(d, w_ref[...])
        dc_ref[...] = dycat[:, :D_CONV]
        for h in range(N_SB_HEADS):
            do = _head(dycat[:, D_CONV:D_CONV + D_SB], h).astype(BF16)
            do_ref[h] = do
            dot_ref[h] = do.T
        for h in range(N_RET_HEADS):
            dr_ref[h] = _head(dycat[:, D_CONV + D_SB:], h)

    row = lambda i: (i, 0)
    return pl.pallas_call(
        body, name="mix_out_bwd",
        grid=(S // tm,),
        in_specs=[
            pl.BlockSpec((tm, D_MODEL), row),
            pl.BlockSpec((D_MODEL, D_MODEL), lambda i: (0, 0)),
        ],
        out_specs=[
            pl.BlockSpec((tm, D_MODEL), row),
            pl.BlockSpec((tm, D_CONV), row),
            _heads_spec(N_SB_HEADS, tm),
            pl.BlockSpec((N_SB_HEADS, HEAD_DIM, tm), lambda i: (0, 0, i)),
            _heads_spec(N_RET_HEADS, tm),
        ],
        out_shape=[
            jax.ShapeDtypeStruct((S, D_MODEL), BF16),
            jax.ShapeDtypeStruct((S, D_CONV), F32),
            jax.ShapeDtypeStruct((N_SB_HEADS, S, HEAD_DIM), BF16),
            jax.ShapeDtypeStruct((N_SB_HEADS, HEAD_DIM, S), BF16),
            jax.ShapeDtypeStruct((N_RET_HEADS, S, HEAD_DIM), F32),
        ],
        compiler_params=_params(1),
    )(dy, w)


def _rows_from(x, start, n):
    return pltpu.roll(x, (x.shape[0] - start) % x.shape[0], 0)[:n]


def _conv_ln(ypre, ln_g, ln_b):
    mu = jnp.mean(ypre, axis=-1, keepdims=True)
    yc = ypre - mu
    rstd = lax.rsqrt(jnp.mean(yc * yc, axis=-1, keepdims=True) + EPS)
    yn = yc * rstd
    return yn, rstd, yn * ln_g + ln_b


def conv_fwd(proj, cw, cb, ln_g, ln_b, tm=CONV_TILE):
    S = proj.shape[0]
    hb = tm // CONV_HALO

    def body(a_ref, b_ref, ap_ref, bp_ref, cw_ref, cb_ref, g_ref, bb_ref, y_ref, ypre_ref, v_sc):
        i = pl.program_id(0)
        prev = ap_ref[...] * _sigmoid(bp_ref[...])
        v_sc[pl.ds(0, CONV_HALO), :] = jnp.where(i > 0, prev, 0.0)
        v_sc[pl.ds(CONV_HALO, tm), :] = a_ref[...] * _sigmoid(b_ref[...])
        vext = v_sc[...]
        acc = jnp.zeros((tm, D_CONV), F32)
        for j in range(CONV_WIDTH):
            acc = acc + cw_ref[pl.ds(j, 1), :] * _rows_from(vext, CONV_HALO - (CONV_WIDTH - 1) + j, tm)
        ypre = acc + cb_ref[...]
        ypre_ref[...] = ypre
        _, _, z = _conv_ln(ypre, g_ref[...], bb_ref[...])
        y_ref[...] = (z * _sigmoid(z)).astype(BF16)

    one = lambda i: (0, 0)
    return pl.pallas_call(
        body, name="conv_fwd",
        grid=(S // tm,),
        in_specs=[
            pl.BlockSpec((tm, D_CONV), lambda i: (i, 0)),
            pl.BlockSpec((tm, D_CONV), lambda i: (i, 1)),
            pl.BlockSpec((CONV_HALO, D_CONV), lambda i: (jnp.maximum(i * hb - 1, 0), 0)),
            pl.BlockSpec((CONV_HALO, D_CONV), lambda i: (jnp.maximum(i * hb - 1, 0), 1)),
            pl.BlockSpec((CONV_HALO, D_CONV), one),
            pl.BlockSpec((1, D_CONV), one),
            pl.BlockSpec((1, D_CONV), one),
            pl.BlockSpec((1, D_CONV), one),
        ],
        out_specs=[pl.BlockSpec((tm, D_CONV), lambda i: (i, 0)), pl.BlockSpec((tm, D_CONV), lambda i: (i, 0))],
        out_shape=[jax.ShapeDtypeStruct((S, D_CONV), BF16), jax.ShapeDtypeStruct((S, D_CONV), F32)],
        scratch_shapes=[pltpu.VMEM((tm + CONV_HALO, D_CONV), F32)],
        compiler_params=_params(1),
    )(proj, proj, proj, proj, cw, cb, ln_g, ln_b)


def conv_bwd(dyc, ypre, proj, cw, ln_g, ln_b, tm=CONV_TILE):
    S = ypre.shape[0]
    hb = tm // CONV_HALO
    nblk = S // tm
    last_halo = S // CONV_HALO - 1

    def dpre(dy, yp, g, bb):
        yn, rstd, z = _conv_ln(yp, g, bb)
        sg = _sigmoid(z)
        dz = dy * (sg * (1.0 + z * (1.0 - sg)))
        dyn = dz * g
        d = rstd * (dyn - jnp.mean(dyn, axis=-1, keepdims=True) - yn * jnp.mean(dyn * yn, axis=-1, keepdims=True))
        return d, dz * yn, dz

    def body(dy_ref, yp_ref, dyn_ref, ypn_ref, a_ref, b_ref, ap_ref, bp_ref, cw_ref, g_ref, bb_ref,
             du_ref, dcw_ref, dsm_ref, d_sc, v_sc):
        i = pl.program_id(0)
        g = g_ref[...]
        bb = bb_ref[...]
        d_main, dgn, dz = dpre(dy_ref[...], yp_ref[...], g, bb)
        d_next, _, _ = dpre(dyn_ref[...], ypn_ref[...], g, bb)
        d_sc[pl.ds(0, tm), :] = d_main
        d_sc[pl.ds(tm, CONV_HALO), :] = jnp.where(i < nblk - 1, d_next, 0.0)
        a = a_ref[...]
        sb = _sigmoid(b_ref[...])
        prev = ap_ref[...] * _sigmoid(bp_ref[...])
        v_sc[pl.ds(0, CONV_HALO), :] = jnp.where(i > 0, prev, 0.0)
        v_sc[pl.ds(CONV_HALO, tm), :] = a * sb

        @pl.when(i == 0)
        def _():
            dcw_ref[...] = jnp.zeros_like(dcw_ref)
            dsm_ref[...] = jnp.zeros_like(dsm_ref)

        dext = d_sc[...]
        vext = v_sc[...]
        dv = jnp.zeros((tm, D_CONV), F32)
        for j in range(CONV_WIDTH):
            dv = dv + cw_ref[pl.ds(j, 1), :] * _rows_from(dext, CONV_WIDTH - 1 - j, tm)
            shifted = _rows_from(vext, CONV_HALO - (CONV_WIDTH - 1) + j, tm)
            dcw_ref[pl.ds(j, 1), :] += jnp.sum(d_main * shifted, axis=0, keepdims=True)
        du_ref[:, pl.ds(0, D_CONV)] = dv * sb
        du_ref[:, pl.ds(D_CONV, D_CONV)] = dv * a * sb * (1.0 - sb)
        dsm_ref[pl.ds(0, 1), :] += jnp.sum(d_main, axis=0, keepdims=True)
        dsm_ref[pl.ds(1, 1), :] += jnp.sum(dgn, axis=0, keepdims=True)
        dsm_ref[pl.ds(2, 1), :] += jnp.sum(dz, axis=0, keepdims=True)

    one = lambda i: (0, 0)
    prev_map = lambda c: (lambda i: (jnp.maximum(i * hb - 1, 0), c))
    next_map = lambda i: (jnp.minimum((i + 1) * hb, last_halo), 0)
    return pl.pallas_call(
        body, name="conv_bwd",
        grid=(nblk,),
        in_specs=[
            pl.BlockSpec((tm, D_CONV), lambda i: (i, 0)),
            pl.BlockSpec((tm, D_CONV), lambda i: (i, 0)),
            pl.BlockSpec((CONV_HALO, D_CONV), next_map),
            pl.BlockSpec((CONV_HALO, D_CONV), next_map),
            pl.BlockSpec((tm, D_CONV), lambda i: (i, 0)),
            pl.BlockSpec((tm, D_CONV), lambda i: (i, 1)),
            pl.BlockSpec((CONV_HALO, D_CONV), prev_map(0)),
            pl.BlockSpec((CONV_HALO, D_CONV), prev_map(1)),
            pl.BlockSpec((CONV_HALO, D_CONV), one),
            pl.BlockSpec((1, D_CONV), one),
            pl.BlockSpec((1, D_CONV), one),
        ],
        out_specs=[
            pl.BlockSpec((tm, 2 * D_CONV), lambda i: (i, 0)),
            pl.BlockSpec((CONV_HALO, D_CONV), one),
            pl.BlockSpec((8, D_CONV), one),
        ],
        out_shape=[
            jax.ShapeDtypeStruct((S, 2 * D_CONV), F32),
            jax.ShapeDtypeStruct((CONV_HALO, D_CONV), F32),
            jax.ShapeDtypeStruct((8, D_CONV), F32),
        ],
        scratch_shapes=[pltpu.VMEM((tm + CONV_HALO, D_CONV), F32), pltpu.VMEM((tm + CONV_HALO, D_CONV), F32)],
        compiler_params=_params(1),
    )(dyc, ypre, dyc, ypre, proj, proj, proj, proj, cw, ln_g, ln_b)


SB_GROUP = 8


def _softplus(z):
    neg_abs = lax.bitcast_convert_type(lax.bitcast_convert_type(z, jnp.uint32) | jnp.uint32(0x80000000), F32)
    return jnp.maximum(z, 0.0) + jnp.log(1.0 + jnp.exp(neg_abs))


def _full_groups(n, body):
    def step(t, c):
        body(t * SB_GROUP)
        return c

    lax.fori_loop(0, lax.div(n, SB_GROUP), step, 0)


def _last_group(n, step, body):
    r = lax.rem(n, SB_GROUP)
    for k in range(0, SB_GROUP, step):
        @pl.when(r == k)
        def _(k=k):
            body(k)


def _rows(xs):
    return xs[0] if len(xs) == 1 else jnp.concatenate(xs, axis=0)


def sb_fwd(q, k, v, comm=None, T=SB_TILE, Q=SB_FWD_ROWS):
    H, S, dh = q.shape
    M = Q // T

    def body(q_ref, k_ref, v_ref, o_ref, tot_ref, acc_sc, car_sc):
        qb = pl.program_id(1)
        qv = q_ref[...]
        row = lax.broadcasted_iota(jnp.int32, (T, T), 0)
        col = lax.broadcasted_iota(jnp.int32, (T, T), 1)
        tri = jnp.where(row >= col, 1.0, 0.0).astype(BF16)
        qrow = lax.broadcasted_iota(jnp.int32, (Q, T), 0)
        kcol = lax.broadcasted_iota(jnp.int32, (Q, T), 1)
        causal = {d + 1: kcol + d * T < qrow for d in range(M)}
        acc_sc[...] = jnp.zeros_like(acc_sc)
        car_sc[...] = jnp.zeros_like(car_sc)

        def logits(kb, masked):
            ks = k_ref[pl.ds(pl.multiple_of(kb * T, T), T), :]
            z = _dot_nt(qv, ks)
            nb = _softplus(z)
            if masked:
                nb = jnp.where(causal[masked], nb, 0.0)
            return z, nb.astype(BF16)

        def group(kbs, diag):
            parts = [logits(kb, d) for kb, d in zip(kbs, diag)]
            pall = _dot(_rows([nb for _, nb in parts]), tri)
            carry = car_sc[...]
            out = None
            for j, kb in enumerate(kbs):
                p = pall[j * Q:(j + 1) * Q]
                vs = v_ref[pl.ds(pl.multiple_of(kb * T, T), T), :]
                w = jnp.exp((parts[j][0] - carry) - p)
                if diag[j]:
                    w = jnp.where(causal[diag[j]], w, 0.0)
                o = _dot(w.astype(BF16), vs)
                out = o if out is None else out + o
                carry = carry + p[:, 0:1]
            acc_sc[...] += out
            car_sc[...] = carry

        full = M * qb
        _last_group(full, M, lambda r: group([full + d for d in reversed(range(M))] + [full - 1 - o for o in range(r)],
                                             [d + 1 for d in reversed(range(M))] + [0] * r))
        rest = full - lax.rem(full, SB_GROUP)
        _full_groups(rest, lambda o: group([rest - 1 - o - j for j in range(SB_GROUP)], [0] * SB_GROUP))
        o_ref[...] = acc_sc[...]
        tot_ref[...] = car_sc[...]

    return _call(
        body, (q, k, v), comm, name="sb_fwd",
        grid=(H, S // Q),
        in_specs=[
            pl.BlockSpec((None, Q, dh), lambda h, i: (h, i, 0)),
            pl.BlockSpec((None, S, dh), lambda h, i: (h, 0, 0)),
            pl.BlockSpec((None, S, dh), lambda h, i: (h, 0, 0)),
        ],
        out_specs=[
            pl.BlockSpec((None, Q, dh), lambda h, i: (h, i, 0)),
            pl.BlockSpec((None, Q, 1), lambda h, i: (h, i, 0)),
        ],
        out_shape=[jax.ShapeDtypeStruct((H, S, dh), F32), jax.ShapeDtypeStruct((H, S, 1), F32)],
        scratch_shapes=[pltpu.VMEM((Q, dh), F32), pltpu.VMEM((Q, 1), F32)],
    )


def sb_bwd(q, k, v, do, qt, dot, tot, comm=None, T=SB_TILE, Q=SB_ROWS):
    H, S, dh = q.shape
    nt = S // T
    M = Q // T

    def body(q_ref, k_ref, v_ref, do_ref, qt_ref, dot_ref, tot_ref, dq_ref, dk_ref, dv_ref, acc_sc, rc_sc, gc_sc):
        qb = pl.program_id(1)
        qv = q_ref[...]
        dov = do_ref[...]
        qtv = qt_ref[...]
        dotv = dot_ref[...]
        row = lax.broadcasted_iota(jnp.int32, (T, T), 0)
        col = lax.broadcasted_iota(jnp.int32, (T, T), 1)
        before = jnp.where(row < col, 1.0, 0.0).astype(BF16)
        qrow = lax.broadcasted_iota(jnp.int32, (Q, T), 0)
        kcol = lax.broadcasted_iota(jnp.int32, (Q, T), 1)
        causal = {d + 1: kcol + d * T < qrow for d in range(M)}
        acc_sc[...] = jnp.zeros_like(acc_sc)
        rc_sc[...] = tot_ref[...]
        gc_sc[...] = jnp.zeros_like(gc_sc)

        @pl.when(qb == 0)
        def _():
            dk_ref[...] = jnp.zeros_like(dk_ref)
            dv_ref[...] = jnp.zeros_like(dv_ref)

        def first(kb, masked):
            start = pl.multiple_of(kb * T, T)
            z = _dot_nt(qv, k_ref[pl.ds(start, T), :])
            nb = _softplus(z)
            sig = jnp.exp(z - nb)
            if masked:
                nb = jnp.where(causal[masked], nb, 0.0)
            dw = _dot_nt(dov, v_ref[pl.ds(start, T), :])
            return z, sig, nb.astype(BF16), dw

        def group(kbs, diag):
            parts = [first(kb, d) for kb, d in zip(kbs, diag)]
            pall = _dot(_rows([p[2] for p in parts]), before)
            rc = rc_sc[...]
            ws, gs, ghs = [], [], []
            for j in range(len(kbs)):
                z, _, nbh, dw = parts[j]
                p = pall[j * Q:(j + 1) * Q]
                w = jnp.exp((z - rc) + p)
                rc = rc - (p[:, T - 1:T] + nbh[:, T - 1:T].astype(F32))
                if diag[j]:
                    w = jnp.where(causal[diag[j]], w, 0.0)
                g = dw * w
                ws.append(w.astype(BF16))
                gs.append(g)
                ghs.append(g.astype(BF16))
            glall = _dot(_rows(ghs), before)
            gc = gc_sc[...]
            dq = None
            for j, kb in enumerate(kbs):
                ks = k_ref[pl.ds(pl.multiple_of(kb * T, T), T), :]
                gl = glall[j * Q:(j + 1) * Q]
                dz = gs[j] - parts[j][1] * (gs[j] + (gl + gc))
                gc = gc + gl[:, T - 1:T] + ghs[j][:, T - 1:T].astype(F32)
                if diag[j]:
                    dz = jnp.where(causal[diag[j]], dz, 0.0)
                dzb = dz.astype(BF16)
                d = _dot(dzb, ks)
                dq = d if dq is None else dq + d
                dk_ref[kb] += _dot(qtv, dzb)
                dv_ref[kb] += _dot(dotv, ws[j])
            acc_sc[...] += dq
            rc_sc[...] = rc
            gc_sc[...] = gc

        full = M * qb
        _full_groups(full, lambda o: group([o + j for j in range(SB_GROUP)], [0] * SB_GROUP))
        rest = full - lax.rem(full, SB_GROUP)
        _last_group(full, M, lambda r: group([rest + j for j in range(r)] + [full + d for d in range(M)],
                                             [0] * r + [d + 1 for d in range(M)]))
        dq_ref[...] = acc_sc[...]

    return _call(
        body, (q, k, v, do, qt, dot, tot), comm, name="sb_bwd",
        grid=(H, S // Q),
        in_specs=[
            pl.BlockSpec((None, Q, dh), lambda h, i: (h, i, 0)),
            pl.BlockSpec((None, S, dh), lambda h, i: (h, 0, 0)),
            pl.BlockSpec((None, S, dh), lambda h, i: (h, 0, 0)),
            pl.BlockSpec((None, Q, dh), lambda h, i: (h, i, 0)),
            pl.BlockSpec((None, dh, Q), lambda h, i: (h, 0, i)),
            pl.BlockSpec((None, dh, Q), lambda h, i: (h, 0, i)),
            pl.BlockSpec((None, Q, 1), lambda h, i: (h, i, 0)),
        ],
        out_specs=[
            pl.BlockSpec((None, Q, dh), lambda h, i: (h, i, 0)),
            pl.BlockSpec((None, nt, dh, T), lambda h, i: (h, 0, 0, 0)),
            pl.BlockSpec((None, nt, dh, T), lambda h, i: (h, 0, 0, 0)),
        ],
        out_shape=[jax.ShapeDtypeStruct((H, S, dh), F32), jax.ShapeDtypeStruct((H, nt, dh, T), F32),
                   jax.ShapeDtypeStruct((H, nt, dh, T), F32)],
        scratch_shapes=[pltpu.VMEM((Q, dh), F32), pltpu.VMEM((Q, 1), F32), pltpu.VMEM((Q, 1), F32)],
    )


def _ret_tables(T=RET_TILE):
    hh = jnp.arange(N_RET_HEADS, dtype=F32)
    log_gamma = jnp.log1p(-jnp.exp2(-5.0 - hh))
    idx = jnp.arange(T, dtype=F32)
    diff = idx[:, None] - idx[None, :]
    ci = (jnp.arange(T) // 64)
    same = ci[:, None] == ci[None, :]
    earlier = ci[None, :] < ci[:, None]
    dist = jnp.where(same, jnp.abs(diff), diff)
    dmat = jnp.where(same | earlier, jnp.exp(log_gamma[:, None, None] * dist[None]), 0.0)
    ones = jnp.ones((1, 1, HEAD_DIM), F32)
    qdec = jnp.exp(log_gamma[:, None] * (idx + 1.0)[None, :])[:, :, None] * ones
    kdec = jnp.exp(log_gamma[:, None] * (T - 1.0 - idx)[None, :])[:, :, None] * ones
    bdec = jnp.exp(log_gamma * T)[:, None, None] * jnp.ones((1, HEAD_DIM, HEAD_DIM), F32)
    return dmat, qdec, kdec, bdec


def _rope_tables(S):
    half = HEAD_DIM // 2
    inv = 1.0 / (ROPE_BASE ** (jnp.arange(half, dtype=F32) / half))
    ang = jnp.arange(S).astype(F32)[:, None] * inv[None, :]
    c = jnp.cos(ang)
    s = jnp.sin(ang)
    cos = jnp.tile(jnp.concatenate([c, c], axis=1), (1, N_RET_HEADS))
    sin = jnp.tile(jnp.concatenate([-s, s], axis=1), (1, N_RET_HEADS))
    return cos, sin


def ret_fwd(q, k, v, gate, ng, tables, T=RET_TILE):
    H, S, dh = q.shape
    dmat, qdec, kdec, bdec = tables

    def body(q_ref, k_ref, v_ref, gt_ref, ng_ref, dm_ref, qd_ref, kd_ref, bd_ref, o_ref, y_ref, st_ref, s_sc):
        n = pl.program_id(1)

        @pl.when(n == 0)
        def _():
            s_sc[...] = jnp.zeros_like(s_sc)

        qv = q_ref[...]
        kv = k_ref[...]
        vv = v_ref[...]
        state = s_sc[...]
        st_ref[...] = state
        sc = (_dot_nt(qv, kv) * dm_ref[...]).astype(BF16)
        qd = (qv.astype(F32) * qd_ref[...]).astype(BF16)
        y = _dot(sc, vv) + _dot(qd, state.astype(BF16))
        y_ref[...] = y
        kd = (kv.astype(F32) * kd_ref[...]).astype(BF16)
        s_sc[...] = bd_ref[...] * state + _dot_tn(kd, vv)
        mu = jnp.mean(y, axis=-1, keepdims=True)
        yc = y - mu
        yn = yc * lax.rsqrt(jnp.mean(yc * yc, axis=-1, keepdims=True) + EPS)
        gt = gt_ref[...]
        o_ref[...] = gt * _sigmoid(gt) * (yn * ng_ref[...])

    blk = lambda h, n: (h, n, 0)
    head = lambda h, n: (h, 0, 0)
    return pl.pallas_call(
        body, name="ret_fwd",
        grid=(H, S // T),
        in_specs=[
            pl.BlockSpec((None, T, dh), blk),
            pl.BlockSpec((None, T, dh), blk),
            pl.BlockSpec((None, T, dh), blk),
            pl.BlockSpec((None, T, dh), blk),
            pl.BlockSpec((None, 1, dh), head),
            pl.BlockSpec((None, T, T), head),
            pl.BlockSpec((None, T, dh), head),
            pl.BlockSpec((None, T, dh), head),
            pl.BlockSpec((None, dh, dh), head),
        ],
        out_specs=[
            pl.BlockSpec((None, T, dh), blk),
            pl.BlockSpec((None, T, dh), blk),
            pl.BlockSpec((None, None, dh, dh), lambda h, n: (h, n, 0, 0)),
        ],
        out_shape=[
            jax.ShapeDtypeStruct((H, S, dh), F32),
            jax.ShapeDtypeStruct((H, S, dh), F32),
            jax.ShapeDtypeStruct((H, S // T, dh, dh), F32),
        ],
        scratch_shapes=[pltpu.VMEM((dh, dh), F32)],
        compiler_params=_params(2),
    )(q, k, v, gate, ng, dmat, qdec, kdec, bdec)


def ret_bwd(do, q, k, v, gate, ng, y, states, tables, T=RET_TILE):
    H, S, dh = q.shape
    nb = S // T
    dmat, qdec, kdec, bdec = tables

    def body(do_ref, q_ref, k_ref, v_ref, gt_ref, ng_ref, y_ref, st_ref, dm_ref, qd_ref, kd_ref, bd_ref,
             dq_ref, dk_ref, dv_ref, dgt_ref, dng_ref, u_sc):
        n = pl.program_id(1)

        @pl.when(n == 0)
        def _():
            u_sc[...] = jnp.zeros_like(u_sc)
            dng_ref[...] = jnp.zeros_like(dng_ref)

        yv = y_ref[...]
        mu = jnp.mean(yv, axis=-1, keepdims=True)
        yc = yv - mu
        rstd = lax.rsqrt(jnp.mean(yc * yc, axis=-1, keepdims=True) + EPS)
        yn = yc * rstd
        gt = gt_ref[...]
        sg = _sigmoid(gt)
        ngv = ng_ref[...]
        dout = do_ref[...]
        dgt_ref[...] = dout * (yn * ngv) * (sg * (1.0 + gt * (1.0 - sg)))
        dn = dout * (gt * sg)
        dng_ref[...] += jnp.sum(dn * yn, axis=0, keepdims=True)
        dyn = dn * ngv
        dy = rstd * (dyn - jnp.mean(dyn, axis=-1, keepdims=True) - yn * jnp.mean(dyn * yn, axis=-1, keepdims=True))
        dyb = dy.astype(BF16)

        qv = q_ref[...]
        kv = k_ref[...]
        vv = v_ref[...]
        dm = dm_ref[...]
        qdt = qd_ref[...]
        kdt = kd_ref[...]
        sb = st_ref[...].astype(BF16)
        u = u_sc[...]
        ub = u.astype(BF16)
        dqk = (_dot_nt(dyb, vv) * dm).astype(BF16)
        sc = (_dot_nt(qv, kv) * dm).astype(BF16)
        qd = (qv.astype(F32) * qdt).astype(BF16)
        kd = (kv.astype(F32) * kdt).astype(BF16)
        dq_ref[...] = _dot(dqk, kv) + qdt * _dot_nt(dyb, sb)
        dk_ref[...] = _dot_tn(dqk, qv) + kdt * _dot_nt(vv, ub)
        dv_ref[...] = _dot_tn(sc, dyb) + _dot(kd, ub)
        u_sc[...] = bd_ref[...] * u + _dot_tn(qd, dyb)

    blk = lambda h, n: (h, nb - 1 - n, 0)
    head = lambda h, n: (h, 0, 0)
    return pl.pallas_call(
        body, name="ret_bwd",
        grid=(H, nb),
        in_specs=[
            pl.BlockSpec((None, T, dh), blk),
            pl.BlockSpec((None, T, dh), blk),
            pl.BlockSpec((None, T, dh), blk),
            pl.BlockSpec((None, T, dh), blk),
            pl.BlockSpec((None, T, dh), blk),
            pl.BlockSpec((None, 1, dh), head),
            pl.BlockSpec((None, T, dh), blk),
            pl.BlockSpec((None, None, dh, dh), lambda h, n: (h, nb - 1 - n, 0, 0)),
            pl.BlockSpec((None, T, T), head),
            pl.BlockSpec((None, T, dh), head),
            pl.BlockSpec((None, T, dh), head),
            pl.BlockSpec((None, dh, dh), head),
        ],
        out_specs=[
            pl.BlockSpec((None, T, dh), blk),
            pl.BlockSpec((None, T, dh), blk),
            pl.BlockSpec((None, T, dh), blk),
            pl.BlockSpec((None, T, dh), blk),
            pl.BlockSpec((None, 1, dh), head),
        ],
        out_shape=[jax.ShapeDtypeStruct((H, S, dh), F32)] * 4 + [jax.ShapeDtypeStruct((H, 1, dh), F32)],
        scratch_shapes=[pltpu.VMEM((dh, dh), F32)],
        compiler_params=_params(2),
    )(do, q, k, v, gate, ng, y, states, dmat, qdec, kdec, bdec)


def loss_head(x, g, target, tm=ROW_TILE):
    S = x.shape[0]

    def body(x_ref, g_ref, t_ref, loss_ref, dx_ref, dg_ref):
        i = pl.program_id(0)
        xv = x_ref[...]
        gv = g_ref[...]
        _, xhat = _rms_stats(xv)
        err = xhat * gv - t_ref[...]
        part = 0.5 * jnp.sum(jnp.mean(err * err, axis=-1, keepdims=True), axis=0, keepdims=True)
        dx, _, dg = _rms_bwd(xv, gv, err * (1.0 / D_MODEL))
        dx_ref[...] = dx
        part = jnp.broadcast_to(part, (1, 128))

        @pl.when(i == 0)
        def _():
            loss_ref[...] = part
            dg_ref[...] = dg

        @pl.when(i > 0)
        def _():
            loss_ref[...] += part
            dg_ref[...] += dg

    row = lambda i: (i, 0)
    one = lambda i: (0, 0)
    return pl.pallas_call(
        body, name="loss_head",
        grid=(S // tm,),
        in_specs=[pl.BlockSpec((tm, D_MODEL), row), pl.BlockSpec((1, D_MODEL), one), pl.BlockSpec((tm, D_MODEL), row)],
        out_specs=[pl.BlockSpec((1, 128), one), pl.BlockSpec((tm, D_MODEL), row), pl.BlockSpec((1, D_MODEL), one)],
        out_shape=[
            jax.ShapeDtypeStruct((1, 128), F32),
            jax.ShapeDtypeStruct((S, D_MODEL), F32),
            jax.ShapeDtypeStruct((1, D_MODEL), F32),
        ],
        compiler_params=_params(1),
    )(x, g, target)


def adamw(parts, w, m, v, tr, transposed=False):
    L, R, C = w.shape
    nr = R // tr
    c1 = 1.0 / (1.0 - ADAM_B1 ** ADAM_STEP)
    c2 = 1.0 / (1.0 - ADAM_B2 ** ADAM_STEP)

    def body(*refs):
        p_refs = refs[:L]
        w_ref, m_ref, v_ref, g_ref, d_ref, mo_ref, vo_ref = refs[L:]
        l = pl.program_id(0)
        g = None
        for d in range(N_DEV):
            pd = p_refs[0][d].astype(F32)
            for ll in range(1, L):
                pd = jnp.where(l == ll, p_refs[ll][d].astype(F32), pd)
            g = pd if g is None else g + pd
        if transposed:
            g = g.T
        mn = ADAM_B1 * m_ref[...] + (1.0 - ADAM_B1) * g
        vn = ADAM_B2 * v_ref[...] + (1.0 - ADAM_B2) * (g * g)
        g_ref[...] = g
        mo_ref[...] = mn
        vo_ref[...] = vn
        d_ref[...] = -ADAM_LR * ((mn * c1) / (jnp.sqrt(vn * c2) + ADAM_EPS) + ADAM_WD * w_ref[...])

    def part_spec(ll):
        def block(l, i):
            return jnp.where(l == ll, i, jnp.where(l < ll, 0, nr - 1))
        if transposed:
            return pl.BlockSpec((N_DEV, C, tr), lambda l, i: (0, 0, block(l, i)))
        return pl.BlockSpec((N_DEV, tr, C), lambda l, i: (0, block(l, i), 0))

    blk = pl.BlockSpec((None, tr, C), lambda l, i: (l, i, 0))
    return pl.pallas_call(
        body, name="adamw",
        grid=(L, nr),
        in_specs=[part_spec(ll) for ll in range(L)] + [blk] * 3,
        out_specs=[blk] * 4,
        out_shape=[jax.ShapeDtypeStruct((L, R, C), F32)] * 4,
        compiler_params=_params(2),
    )(*parts, w, m, v)


def _my_id():
    return lax.axis_index("x") * 4 + lax.axis_index("y") * 2 + lax.axis_index("c")


def _peer(k):
    x, y, c = lax.axis_index("x"), lax.axis_index("y"), lax.axis_index("c")
    px = 1 - x if k & 4 else x
    py = 1 - y if k & 2 else y
    pc = 1 - c if k & 1 else c
    return (px, py, pc), px * 4 + py * 2 + pc


GATHER = "gather"
EXCHANGE = "exchange"


def _copies(kind, ins, outs, send_sems, recv_sems, local_sems, receive_side):
    me = _my_id()
    local, sends, recvs = [], [], []
    for t in range(len(ins)):
        src = ins[t] if kind == GATHER else ins[t].at[me]
        local.append(pltpu.make_async_copy(src, outs[t].at[me], local_sems.at[t]))
    for k in range(1, N_DEV):
        dev, pid = _peer(k)
        for t in range(len(ins)):
            sems = dict(send_sem=send_sems.at[t, k - 1], recv_sem=recv_sems.at[t, k - 1],
                        device_id=dev, device_id_type=pl.DeviceIdType.MESH)
            src = ins[t] if kind == GATHER else ins[t].at[pid]
            sends.append(pltpu.make_async_remote_copy(src_ref=src, dst_ref=outs[t].at[me], **sems))
            if receive_side:
                recvs.append(pltpu.make_async_remote_copy(src_ref=src, dst_ref=outs[t].at[pid], **sems))
    return local, sends, recvs


def _comm_start(kind, ins, outs, sems):
    local, sends, _ = _copies(kind, ins, outs, *sems, receive_side=False)
    for cp in local + sends:
        cp.start()


def _comm_wait(kind, ins, outs, sems):
    local, sends, recvs = _copies(kind, ins, outs, *sems, receive_side=True)
    for cp in recvs:
        cp.wait_recv()
    for cp in sends:
        cp.wait_send()
    for cp in local:
        cp.wait()


def _comm_shapes(kind, arrays):
    n = len(arrays)
    out_shape = [jax.ShapeDtypeStruct(((N_DEV,) + a.shape) if kind == GATHER else a.shape, a.dtype) for a in arrays]
    sems = [pltpu.SemaphoreType.DMA((n, N_DEV - 1)), pltpu.SemaphoreType.DMA((n, N_DEV - 1)),
            pltpu.SemaphoreType.DMA((n,))]
    return out_shape, sems


def communicate(kind, arrays):
    n = len(arrays)

    def body(*refs):
        ins, outs, sems = refs[:n], refs[n:2 * n], refs[2 * n:]
        _comm_start(kind, ins, outs, sems)
        _comm_wait(kind, ins, outs, sems)

    out_shape, sems = _comm_shapes(kind, arrays)
    any_spec = pl.BlockSpec(memory_space=pl.ANY)
    return pl.pallas_call(
        body, name=kind, in_specs=[any_spec] * n, out_specs=[any_spec] * n, out_shape=out_shape, scratch_shapes=sems,
    )(*arrays)


def gather_two_level(arrays):
    n = len(arrays)

    def body(*refs):
        ins, outs = refs[:n], refs[n:2 * n]
        send_sems, recv_sems, local_sems = refs[2 * n:]
        x, y, c = lax.axis_index("x"), lax.axis_index("y"), lax.axis_index("c")
        me, sibling = (x, y, c), (x, y, 1 - c)
        chips = [(1 - x, y), (x, 1 - y), (1 - x, 1 - y)]

        def slot(px, py, pc):
            return px * 4 + py * 2 + pc

        def copy(t, k, src, owner, to):
            return pltpu.make_async_remote_copy(
                src_ref=src, dst_ref=outs[t].at[slot(*owner)], send_sem=send_sems.at[t, k], recv_sem=recv_sems.at[t, k],
                device_id=to, device_id_type=pl.DeviceIdType.MESH)

        local = [pltpu.make_async_copy(ins[t], outs[t].at[slot(*me)], local_sems.at[t]) for t in range(n)]
        first = [copy(t, 0, ins[t], me, sibling) for t in range(n)]
        first += [copy(t, 1 + j, ins[t], me, (*chip, c)) for j, chip in enumerate(chips) for t in range(n)]
        for cp in local + first:
            cp.start()
        passed = []
        for j, chip in enumerate(chips):
            for t in range(n):
                copy(t, 1 + j, ins[t], (*chip, c), me).wait_recv()
                cp = copy(t, 4 + j, outs[t].at[slot(*chip, c)], (*chip, c), sibling)
                cp.start()
                passed.append(cp)
        for t in range(n):
            copy(t, 0, ins[t], sibling, me).wait_recv()
            for j, chip in enumerate(chips):
                copy(t, 4 + j, ins[t], (*chip, 1 - c), me).wait_recv()
        for cp in first + passed:
            cp.wait_send()
        for cp in local:
            cp.wait()

    out_shape, sems = _comm_shapes(GATHER, arrays)
    any_spec = pl.BlockSpec(memory_space=pl.ANY)
    return pl.pallas_call(
        body, name="gather_two_level", in_specs=[any_spec] * n, out_specs=[any_spec] * n, out_shape=out_shape,
        scratch_shapes=sems,
    )(*arrays)


def _call(body, operands, comm, *, name, grid, in_specs, out_specs, out_shape, scratch_shapes):
    if comm is None:
        outs = pl.pallas_call(body, name=name, grid=grid, in_specs=in_specs, out_specs=out_specs, out_shape=out_shape,
                              scratch_shapes=scratch_shapes, compiler_params=_params(len(grid)))(*operands)
        return outs, []
    kind, arrays = comm
    n, n_in, n_out, n_sc = len(arrays), len(in_specs), len(out_specs), len(scratch_shapes)

    def carrier(*refs):
        ins, cins = refs[:n_in], refs[n_in:n_in + n]
        refs = refs[n_in + n:]
        outs, couts = refs[:n_out], refs[n_out:n_out + n]
        scratch, sems = refs[n_out + n:n_out + n + n_sc], refs[n_out + n + n_sc:]
        steps = [pl.program_id(a) for a in range(len(grid))]
        first = functools.reduce(jnp.logical_and, [s == 0 for s in steps])
        last = functools.reduce(jnp.logical_and, [s == g - 1 for s, g in zip(steps, grid)])

        @pl.when(first)
        def _():
            _comm_start(kind, cins, couts, sems)

        body(*ins, *outs, *scratch)

        @pl.when(last)
        def _():
            _comm_wait(kind, cins, couts, sems)

    comm_shape, sems = _comm_shapes(kind, arrays)
    any_spec = pl.BlockSpec(memory_space=pl.ANY)
    outs = pl.pallas_call(
        carrier, name=f"{name}_{kind}", grid=grid,
        in_specs=list(in_specs) + [any_spec] * n,
        out_specs=list(out_specs) + [any_spec] * n,
        out_shape=list(out_shape) + comm_shape,
        scratch_shapes=list(scratch_shapes) + sems,
        compiler_params=_params(len(grid)),
    )(*operands, *arrays)
    return outs[:n_out], outs[n_out:]


def _row(v):
    return v.reshape(1, -1)


def _pad_taps(cw):
    return jnp.concatenate([cw, jnp.zeros((CONV_HALO - CONV_WIDTH, D_CONV), F32)], axis=0)


COL_SHARDED = ("ffn1_w_in", "mix_w_in", "ffn2_w_in")
ROW_SHARDED = ("ffn1_w_out", "mix_w_out", "ffn2_w_out")
SMALL = ("ffn1_norm", "mix_norm", "conv_b", "conv_ln_g", "conv_ln_b", "ret_norm_g", "ffn2_norm", "final_norm")
WEIGHTS = ("ffn1_norm", "ffn1_w_in", "ffn1_w_out", "mix_norm", "mix_w_in", "conv_w", "conv_b", "conv_ln_g",
           "conv_ln_b", "ret_norm_g", "mix_w_out", "ffn2_norm", "ffn2_w_in", "ffn2_w_out", "final_norm")
SMALL_ROWS = 32

FFN1 = ("ffn1_w_in", "ffn1_w_out")
MIX = ("mix_w_in", "mix_w_out")
FFN2 = ("ffn2_w_in", "ffn2_w_out")
STAGE_A = [(n, 0) for n in FFN1]
STAGE_B = [(n, 0) for n in MIX] + [("conv_w", None)]
STAGE_C = [(n, 0) for n in FFN2] + [(n, 1) for n in FFN1 + MIX + FFN2]
STAGE_D = [(n, 1) for n in FFN2]
STAGE_E = [(n, 1) for n in MIX + FFN1] + [(n, 0) for n in FFN2]
STAGE_F = [(n, 0) for n in MIX]
STAGE_G = [("ffn1_w_in", 0)]
STAGE_H = [("ffn1_w_out", 0)]


def _natural(name, got):
    if name == "conv_w":
        return got.transpose(1, 2, 0, 3).reshape(DEPTH, CONV_WIDTH, D_CONV)
    return got.reshape(-1, D_MODEL)


def _by_device(grad):
    return grad.reshape(N_DEV, -1, D_MODEL)


def _pack_small(g):
    flat = jnp.concatenate([g[n].reshape(-1) for n in SMALL] + [g["conv_w"].reshape(-1)])
    flat = jnp.concatenate([flat, jnp.zeros((SMALL_ROWS * D_MODEL - flat.shape[0],), F32)])
    return flat.reshape(SMALL_ROWS, D_MODEL)


def _unpack_small(buf, like):
    flat = buf.reshape(-1)
    out, off = {}, 0
    for n in SMALL:
        size = int(np.prod(like[n].shape))
        out[n] = flat[off:off + size].reshape(like[n].shape)
        off += size
    size = DEPTH * CONV_WIDTH * D_CONV
    out["conv_w"] = flat[off:off + size].reshape(DEPTH, CONV_WIDTH, D_CONV)
    return out


def kernel(x, ffn1_norm, ffn1_w_in, ffn1_w_out, mix_norm, mix_w_in, conv_w, conv_b, conv_ln_g, conv_ln_b, ret_norm_g, mix_w_out, ffn2_norm, ffn2_w_in, ffn2_w_out, final_norm, loss_target, m_ffn1_norm, m_ffn1_w_in, m_ffn1_w_out, m_mix_norm, m_mix_w_in, m_conv_w, m_conv_b, m_conv_ln_g, m_conv_ln_b, m_ret_norm_g, m_mix_w_out, m_ffn2_norm, m_ffn2_w_in, m_ffn2_w_out, m_final_norm, v_ffn1_norm, v_ffn1_w_in, v_ffn1_w_out, v_mix_norm, v_mix_w_in, v_conv_w, v_conv_b, v_conv_ln_g, v_conv_ln_b, v_ret_norm_g, v_mix_w_out, v_ffn2_norm, v_ffn2_w_in, v_ffn2_w_out, v_final_norm):
    args = locals()
    w = {n: args[n] for n in WEIGHTS}
    m = {n: args["m_" + n] for n in WEIGHTS}
    v = {n: args["v_" + n] for n in WEIGHTS}
    me = _my_id()
    x = x[0]
    target = loss_target[0]
    S = x.shape[0]
    cos, sin = _rope_tables(S)
    tables = _ret_tables()

    full = {}

    def shard(n, l):
        if n == "conv_w":
            return w[n]
        return (w[n][l].T if n in COL_SHARDED else w[n][l]).astype(BF16)

    def gather(keys):
        return GATHER, [shard(n, l) for n, l in keys]

    def gathered(keys, got):
        for (n, l), g in zip(keys, got):
            full[(n, l)] = _natural(n, g)

    gathered(STAGE_A, gather_two_level(gather(STAGE_A)[1]))

    saved = []
    for l in range(DEPTH):
        sv = {"x0": x}
        (x, sv["gate1"], sv["up1"]), got = ffn_fwd(x, _row(w["ffn1_norm"][l]), full[("ffn1_w_in", l)],
                                                   full[("ffn1_w_out", l)], gather(STAGE_B) if l == 0 else None)
        gathered(STAGE_B if l == 0 else [], got)
        sv["x1"] = x
        (sv["u"], sv["q_sb"], sv["k_sb"], sv["v_sb"], sv["qt_sb"], sv["q_r"], sv["k_r"], sv["v_r"],
         sv["g_r"]) = mix_in_fwd(x, _row(w["mix_norm"][l]), full[("mix_w_in", l)], cos, sin)
        cw = _pad_taps(full[("conv_w", None)][l])
        y_conv, sv["ypre"] = conv_fwd(sv["u"], cw, _row(w["conv_b"][l]), _row(w["conv_ln_g"][l]), _row(w["conv_ln_b"][l]))
        (o_sb, sv["tot"]), got = sb_fwd(sv["q_sb"], sv["k_sb"], sv["v_sb"], gather(STAGE_C) if l == 0 else None)
        gathered(STAGE_C if l == 0 else [], got)
        ng = w["ret_norm_g"][l].reshape(N_RET_HEADS, 1, HEAD_DIM)
        o_r, sv["y_r"], sv["states"] = ret_fwd(sv["q_r"], sv["k_r"], sv["v_r"], sv["g_r"], ng, tables)
        x, sv["ycat"] = mix_out_fwd(y_conv, o_sb, o_r, full[("mix_w_out", l)], x)
        sv["x2"] = x
        (x, sv["gate2"], sv["up2"]), _ = ffn_fwd(x, _row(w["ffn2_norm"][l]), full[("ffn2_w_in", l)],
                                                 full[("ffn2_w_out", l)])
        saved.append(sv)

    loss_acc, dx, dg_final = loss_head(x, _row(w["final_norm"]), target)
    loss = lax.psum(loss_acc[0, 0], ("x", "y", "c"))

    g = {"final_norm": dg_final.reshape(D_MODEL)}
    received = {}

    def exchange(keys, extra=(), dtype=F32):
        return EXCHANGE, [_by_device(g[(n, l)]).astype(dtype) for n, l in keys] + list(extra)

    def exchanged(keys, got):
        for key, p in zip(keys, got):
            received[key] = p

    def ffn_back(dx, x_in, gate, up, norm, names, l, comm=None):
        (dx, h, dyh, dgate, dup, hid, dg), got = ffn_bwd(dx, x_in, _row(norm), gate, up, full[(names[0], l)],
                                                         full[(names[1], l)], comm)
        g[(names[0], l)] = matmul_tn([dgate, dup], h, FF_TILE, D_MODEL, name="ffn_dw_in")
        if [(names[0], l)] == STAGE_G:
            g[(names[1], l)], got_g = matmul_tn([hid], dyh, FF_TILE, D_MODEL, name="ffn_dw_out",
                                                comm=exchange(STAGE_G, dtype=BF16))
            exchanged(STAGE_G, got_g)
        else:
            g[(names[1], l)] = matmul_tn([hid], dyh, FF_TILE, D_MODEL, name="ffn_dw_out")
        return dx, dg.reshape(D_MODEL), got

    for l in reversed(range(DEPTH)):
        sv = saved[l]
        dx, g[("ffn2_norm", l)], _ = ffn_back(dx, sv["x2"], sv["gate2"], sv["up2"], w["ffn2_norm"][l], FFN2, l)
        dxb, dy_conv, do_sb, dot_sb, do_r = mix_out_bwd(dx, full[("mix_w_out", l)])
        g[("mix_w_out", l)] = matmul_tn([sv["ycat"]], dxb, D_MODEL, D_MODEL, name="mix_dw_out")
        cw = _pad_taps(full[("conv_w", None)][l])
        du_conv, dcw, dsm = conv_bwd(dy_conv, sv["ypre"], sv["u"], cw, _row(w["conv_ln_g"][l]), _row(w["conv_ln_b"][l]))
        g[("conv_w", l)] = dcw[:CONV_WIDTH]
        g[("conv_b", l)], g[("conv_ln_g", l)], g[("conv_ln_b", l)] = dsm[0], dsm[1], dsm[2]
        stage = STAGE_D if l == DEPTH - 1 else STAGE_E
        (dq_sb, dk_t, dv_t), got = sb_bwd(sv["q_sb"], sv["k_sb"], sv["v_sb"], do_sb, sv["qt_sb"], dot_sb, sv["tot"],
                                          exchange(stage))
        exchanged(stage, got)
        ng = w["ret_norm_g"][l].reshape(N_RET_HEADS, 1, HEAD_DIM)
        dq_r, dk_r, dv_r, dg_r, dng = ret_bwd(do_r, sv["q_r"], sv["k_r"], sv["v_r"], sv["g_r"], ng, sv["y_r"],
                                              sv["states"], tables)
        g[("ret_norm_g", l)] = dng.reshape(D_RET)
        dx, h, dproj, dg = mix_in_bwd(du_conv, dq_sb, dk_t, dv_t, dq_r, dk_r, dv_r, dg_r, cos, sin,
                                      full[("mix_w_in", l)], sv["x1"], _row(w["mix_norm"][l]), dx)
        g[("mix_norm", l)] = dg.reshape(D_MODEL)
        g[("mix_w_in", l)] = matmul_tn([dproj], h, D_MODEL, D_MODEL, name="mix_dw_in")
        dx, g[("ffn1_norm", l)], got = ffn_back(dx, sv["x0"], sv["gate1"], sv["up1"], w["ffn1_norm"][l], FFN1, l,
                                                exchange(STAGE_F) if l == 0 else None)
        exchanged(STAGE_F if l == 0 else [], got)
    grad_x = dx

    small_names = [n for n in SMALL if n != "final_norm"] + ["conv_w"]
    gs = {n: jnp.stack([g[(n, l)] for l in range(DEPTH)], axis=0) for n in small_names}
    gs["final_norm"] = g["final_norm"]
    small = _pack_small(gs)
    got = communicate(*exchange(STAGE_H, [jnp.broadcast_to(small[None], (N_DEV, SMALL_ROWS, D_MODEL))], dtype=BF16))
    exchanged(STAGE_H, got[:-1])

    grad, delta, new_m, new_v = {}, {}, {}, {}
    for n in COL_SHARDED + ROW_SHARDED:
        rows = w[n].shape[1]
        col = n in COL_SHARDED
        grad[n], delta[n], new_m[n], new_v[n] = adamw([received[(n, l)] for l in range(DEPTH)], w[n], m[n], v[n],
                                                      tr=128 if col else rows // 2, transposed=col)

    def small_pack(d):
        mine = dict(d)
        cwf = jnp.zeros((DEPTH, CONV_WIDTH, D_CONV), F32)
        mine["conv_w"] = lax.dynamic_update_slice(cwf, d["conv_w"], (0, 0, me * (D_CONV // N_DEV)))
        return _pack_small(mine)

    outs = adamw([got[-1]], small_pack(w)[None], small_pack(m)[None], small_pack(v)[None], tr=SMALL_ROWS)
    for dst, o in zip((grad, delta, new_m, new_v), outs):
        un = _unpack_small(o[0], w)
        un["conv_w"] = lax.dynamic_slice(un["conv_w"], (0, 0, me * (D_CONV // N_DEV)),
                                         (DEPTH, CONV_WIDTH, D_CONV // N_DEV))
        dst.update(un)

    return (loss, grad_x[None], *[grad[n] for n in WEIGHTS], *[delta[n] for n in WEIGHTS],
            *[new_m[n] for n in WEIGHTS], *[new_v[n] for n in WEIGHTS])
```

```python
import functools

import numpy as np
import jax
import jax.numpy as jnp
from jax import lax
from jax.experimental import pallas as pl
from jax.experimental.pallas import tpu as pltpu

F32 = jnp.float32
BF16 = jnp.bfloat16

D_MODEL = 1024
DEPTH = 2
D_FF = 2816
D_CONV = 256
CONV_WIDTH = 31
CONV_HALO = 32
D_SB = 512
N_SB_HEADS = 8
D_RET = 256
N_RET_HEADS = 4
HEAD_DIM = 64
D_IN_PROJ = 3072
ROPE_BASE = 10000.0
EPS = 1e-6
N_DEV = 8

ADAM_LR = 0.001
ADAM_B1 = 0.9
ADAM_B2 = 0.999
ADAM_EPS = 1e-08
ADAM_WD = 0.01
ADAM_STEP = 10

VMEM_LIMIT = 56 * 1024 * 1024
ROW_TILE = 512
FF_TILE = 1408
FF_FWD_CHUNK = 256
FF_BWD_CHUNK = 2816
SB_TILE = 256
SB_ROWS = 512
SB_FWD_ROWS = 1024
RET_TILE = 512
CONV_TILE = 256

NT_DIMS = (((1,), (1,)), ((), ()))
TN_DIMS = (((0,), (0,)), ((), ()))


def _params(n_axes, vmem=VMEM_LIMIT):
    return pltpu.CompilerParams(dimension_semantics=("arbitrary",) * n_axes, vmem_limit_bytes=vmem)


def _dot(a, b):
    return jnp.dot(a, b, preferred_element_type=F32)


def _dot_nt(a, b):
    return lax.dot_general(a, b, NT_DIMS, preferred_element_type=F32)


def _dot_tn(a, b):
    return lax.dot_general(a, b, TN_DIMS, preferred_element_type=F32)


def _sigmoid(z):
    return 1.0 / (1.0 + jnp.exp(-z))


def _rms_stats(xv):
    r = lax.rsqrt(jnp.mean(xv * xv, axis=-1, keepdims=True) + EPS)
    return r, xv * r


def _rms_bwd(xv, g, dh):
    r, xhat = _rms_stats(xv)
    dxhat = dh * g
    dx = r * (dxhat - xhat * jnp.mean(dxhat * xhat, axis=-1, keepdims=True))
    dg = jnp.sum(dh * xhat, axis=0, keepdims=True)
    return dx, (xhat * g).astype(BF16), dg


def ffn_fwd(x, g, w_in, w_out, comm=None, tm=ROW_TILE):
    S = x.shape[0]
    chunk = FF_FWD_CHUNK

    def body(x_ref, g_ref, w_ref, wo_ref, y_ref, gate_ref, up_ref):
        xv = x_ref[...]
        _, xhat = _rms_stats(xv)
        h = (xhat * g_ref[...]).astype(BF16)
        acc = None
        for j in range(D_FF // chunk):
            cols = pl.ds(j * chunk, chunk)
            gt = _dot_nt(h, w_ref[cols, :])
            up = _dot_nt(h, w_ref[pl.ds(D_FF + j * chunk, chunk), :])
            gate_ref[:, cols] = gt.astype(BF16)
            up_ref[:, cols] = up.astype(BF16)
            part = _dot((gt * _sigmoid(gt) * up).astype(BF16), wo_ref[cols, :])
            acc = part if acc is None else acc + part
        y_ref[...] = xv + 0.5 * acc

    row = lambda i: (i, 0)
    one = lambda i: (0, 0)
    resident = pl.Buffered(1)
    return _call(
        body, (x, g, w_in, w_out), comm, name="ffn_fwd",
        grid=(S // tm,),
        in_specs=[
            pl.BlockSpec((tm, D_MODEL), row),
            pl.BlockSpec((1, D_MODEL), one),
            pl.BlockSpec((2 * D_FF, D_MODEL), one, pipeline_mode=resident),
            pl.BlockSpec((D_FF, D_MODEL), one, pipeline_mode=resident),
        ],
        out_specs=[
            pl.BlockSpec((tm, D_MODEL), row),
            pl.BlockSpec((tm, D_FF), row),
            pl.BlockSpec((tm, D_FF), row),
        ],
        out_shape=[
            jax.ShapeDtypeStruct((S, D_MODEL), F32),
            jax.ShapeDtypeStruct((S, D_FF), BF16),
            jax.ShapeDtypeStruct((S, D_FF), BF16),
        ],
        scratch_shapes=[],
    )


def ffn_bwd(dy, x, g, gate, up, w_in, w_out, comm=None, tm=ROW_TILE // 2):
    S = x.shape[0]
    chunk = FF_BWD_CHUNK

    def body(dy_ref, x_ref, g_ref, gate_ref, up_ref, w_ref, wo_ref,
             dx_ref, h_ref, dyh_ref, dgate_ref, dup_ref, hid_ref, dg_ref):
        i = pl.program_id(0)
        d2 = (0.5 * dy_ref[...]).astype(BF16)
        dyh_ref[...] = d2
        dh = None
        for j in range(D_FF // chunk):
            cols = pl.ds(j * chunk, chunk)
            dhid = _dot_nt(d2, wo_ref[cols, :])
            gt = gate_ref[:, cols].astype(F32)
            u = up_ref[:, cols].astype(F32)
            sig = _sigmoid(gt)
            sl = gt * sig
            dgate = (dhid * u * (sig * (1.0 + gt * (1.0 - sig)))).astype(BF16)
            dup = (dhid * sl).astype(BF16)
            dgate_ref[:, cols] = dgate
            dup_ref[:, cols] = dup
            hid_ref[:, cols] = (sl * u).astype(BF16)
            part = _dot(dgate, w_ref[cols, :]) + _dot(dup, w_ref[pl.ds(D_FF + j * chunk, chunk), :])
            dh = part if dh is None else dh + part
        dx, h, dg = _rms_bwd(x_ref[...], g_ref[...], dh)
        dx_ref[...] = dy_ref[...] + dx
        h_ref[...] = h

        @pl.when(i == 0)
        def _():
            dg_ref[...] = dg

        @pl.when(i > 0)
        def _():
            dg_ref[...] += dg

    row = lambda i: (i, 0)
    one = lambda i: (0, 0)
    resident = pl.Buffered(1)
    return _call(
        body, (dy, x, g, gate, up, w_in, w_out), comm, name="ffn_bwd",
        grid=(S // tm,),
        in_specs=[
            pl.BlockSpec((tm, D_MODEL), row),
            pl.BlockSpec((tm, D_MODEL), row),
            pl.BlockSpec((1, D_MODEL), one),
            pl.BlockSpec((tm, D_FF), row),
            pl.BlockSpec((tm, D_FF), row),
            pl.BlockSpec((2 * D_FF, D_MODEL), one, pipeline_mode=resident),
            pl.BlockSpec((D_FF, D_MODEL), one, pipeline_mode=resident),
        ],
        out_specs=[
            pl.BlockSpec((tm, D_MODEL), row),
            pl.BlockSpec((tm, D_MODEL), row),
            pl.BlockSpec((tm, D_MODEL), row),
            pl.BlockSpec((tm, D_FF), row),
            pl.BlockSpec((tm, D_FF), row),
            pl.BlockSpec((tm, D_FF), row),
            pl.BlockSpec((1, D_MODEL), one),
        ],
        out_shape=[
            jax.ShapeDtypeStruct((S, D_MODEL), F32),
            jax.ShapeDtypeStruct((S, D_MODEL), BF16),
            jax.ShapeDtypeStruct((S, D_MODEL), BF16),
            jax.ShapeDtypeStruct((S, D_FF), BF16),
            jax.ShapeDtypeStruct((S, D_FF), BF16),
            jax.ShapeDtypeStruct((S, D_FF), BF16),
            jax.ShapeDtypeStruct((1, D_MODEL), F32),
        ],
        scratch_shapes=[],
    )


def matmul_tn(a_list, b, ta, tn, tk=4 * ROW_TILE, name="matmul_tn", comm=None):
    S, ka = a_list[0].shape
    nb = b.shape[1]
    assert S % tk == 0 and ka % ta == 0 and nb % tn == 0
    per = ka // ta

    def body(*refs):
        a_refs, b_ref, o_ref = refs[:-2], refs[-2], refs[-1]
        i = pl.program_id(0)
        k = pl.program_id(2)

        @pl.when(k == 0)
        def _():
            o_ref[...] = jnp.zeros_like(o_ref)

        for t, a_ref in enumerate(a_refs):
            @pl.when(lax.div(i, per) == t)
            def _(a_ref=a_ref):
                o_ref[...] += _dot_tn(a_ref[...], b_ref[...])

    def a_spec(t):
        def index(i, j, k):
            mine = lax.div(i, per) == t
            return jnp.where(mine, k, 0), jnp.where(mine, i - t * per, 0)
        return pl.BlockSpec((tk, ta), index)

    (out,), got = _call(
        body, (*a_list, b), comm, name=name,
        grid=(per * len(a_list), nb // tn, S // tk),
        in_specs=[a_spec(t) for t in range(len(a_list))] + [pl.BlockSpec((tk, tn), lambda i, j, k: (k, j))],
        out_specs=[pl.BlockSpec((ta, tn), lambda i, j, k: (i, j))],
        out_shape=[jax.ShapeDtypeStruct((ka * len(a_list), nb), F32)],
        scratch_shapes=[],
    )
    return (out, got) if comm is not None else out


SB_COLS = (2 * D_CONV, 2 * D_CONV + D_SB, 2 * D_CONV + 2 * D_SB)
RET_COLS = tuple(2 * D_CONV + 3 * D_SB + j * D_RET for j in range(4))


def _swap_halves(x):
    n = x.shape[1]
    lane = lax.broadcasted_iota(jnp.int32, x.shape, 1)
    first = (lane % HEAD_DIM) < (HEAD_DIM // 2)
    return jnp.where(first, pltpu.roll(x, n - HEAD_DIM // 2, 1), pltpu.roll(x, HEAD_DIM // 2, 1))


def _head(x, h):
    return x[:, h * HEAD_DIM:(h + 1) * HEAD_DIM]


def _heads_spec(n_heads, tm):
    return pl.BlockSpec((n_heads, tm, HEAD_DIM), lambda i: (0, i, 0))


def mix_in_fwd(x, g, w, cos, sin, tm=ROW_TILE):
    S = x.shape[0]

    def body(x_ref, g_ref, w_ref, c_ref, s_ref, u_ref, q_ref, k_ref, v_ref, qt_ref, qr_ref, kr_ref, vr_ref, gr_ref):
        _, xhat = _rms_stats(x_ref[...])
        proj = _dot_nt((xhat * g_ref[...]).astype(BF16), w_ref[...])
        u_ref[...] = proj[:, :2 * D_CONV]
        for h in range(N_SB_HEADS):
            q = (_head(proj[:, SB_COLS[0]:SB_COLS[1]], h) * 0.125).astype(BF16)
            q_ref[h] = q
            qt_ref[h] = q.T
            k_ref[h] = _head(proj[:, SB_COLS[1]:SB_COLS[2]], h).astype(BF16)
            v_ref[h] = _head(proj[:, SB_COLS[2]:RET_COLS[0]], h).astype(BF16)
        c = c_ref[...]
        s = s_ref[...]
        qv = proj[:, RET_COLS[0]:RET_COLS[1]]
        kv = proj[:, RET_COLS[1]:RET_COLS[2]]
        q_rot = ((qv * c + _swap_halves(qv) * s) * 0.125).astype(BF16)
        k_rot = (kv * c + _swap_halves(kv) * s).astype(BF16)
        for h in range(N_RET_HEADS):
            qr_ref[h] = _head(q_rot, h)
            kr_ref[h] = _head(k_rot, h)
            vr_ref[h] = _head(proj[:, RET_COLS[2]:RET_COLS[3]], h).astype(BF16)
            gr_ref[h] = _head(proj[:, RET_COLS[3]:], h)

    row = lambda i: (i, 0)
    one = lambda i: (0, 0)
    sb = jax.ShapeDtypeStruct((N_SB_HEADS, S, HEAD_DIM), BF16)
    ret = jax.ShapeDtypeStruct((N_RET_HEADS, S, HEAD_DIM), BF16)
    return pl.pallas_call(
        body, name="mix_in_fwd",
        grid=(S // tm,),
        in_specs=[
            pl.BlockSpec((tm, D_MODEL), row),
            pl.BlockSpec((1, D_MODEL), one),
            pl.BlockSpec((D_IN_PROJ, D_MODEL), one, pipeline_mode=pl.Buffered(1)),
            pl.BlockSpec((tm, D_RET), row),
            pl.BlockSpec((tm, D_RET), row),
        ],
        out_specs=[
            pl.BlockSpec((tm, 2 * D_CONV), row),
            _heads_spec(N_SB_HEADS, tm), _heads_spec(N_SB_HEADS, tm), _heads_spec(N_SB_HEADS, tm),
            pl.BlockSpec((N_SB_HEADS, HEAD_DIM, tm), lambda i: (0, 0, i)),
            _heads_spec(N_RET_HEADS, tm), _heads_spec(N_RET_HEADS, tm), _heads_spec(N_RET_HEADS, tm),
            _heads_spec(N_RET_HEADS, tm),
        ],
        out_shape=[
            jax.ShapeDtypeStruct((S, 2 * D_CONV), F32), sb, sb, sb,
            jax.ShapeDtypeStruct((N_SB_HEADS, HEAD_DIM, S), BF16),
            ret, ret, ret, jax.ShapeDtypeStruct((N_RET_HEADS, S, HEAD_DIM), F32),
        ],
        compiler_params=_params(1),
    )(x, g, w, cos, sin)


def mix_in_bwd(du, dq, dkt, dvt, dqr, dkr, dvr, dgr, cos, sin, w, x, g, dy, tm=SB_TILE):
    S = x.shape[0]
    assert dkt.shape[-1] == tm

    def body(du_ref, dq_ref, dkt_ref, dvt_ref, dqr_ref, dkr_ref, dvr_ref, dgr_ref, c_ref, s_ref, w_ref, x_ref, g_ref,
             dy_ref, dx_ref, h_ref, dp_ref, dg_ref):
        i = pl.program_id(0)
        sb_heads = range(N_SB_HEADS)
        ret_heads = range(N_RET_HEADS)
        c = c_ref[...]
        s = s_ref[...]
        dq_rot = jnp.concatenate([dqr_ref[h] for h in ret_heads], axis=1) * 0.125
        dk_rot = jnp.concatenate([dkr_ref[h] for h in ret_heads], axis=1)
        dproj = jnp.concatenate([
            du_ref[...].astype(BF16),
            jnp.concatenate([dq_ref[h] * 0.125 for h in sb_heads], axis=1).astype(BF16),
            jnp.concatenate([dkt_ref[h, 0].T for h in sb_heads], axis=1).astype(BF16),
            jnp.concatenate([dvt_ref[h, 0].T for h in sb_heads], axis=1).astype(BF16),
            (dq_rot * c - _swap_halves(dq_rot) * s).astype(BF16),
            (dk_rot * c - _swap_halves(dk_rot) * s).astype(BF16),
            jnp.concatenate([dvr_ref[h] for h in ret_heads], axis=1).astype(BF16),
            jnp.concatenate([dgr_ref[h] for h in ret_heads], axis=1).astype(BF16)], axis=1)
        dp_ref[...] = dproj
        dh = _dot(dproj, w_ref[...])
        dx, h, dg = _rms_bwd(x_ref[...], g_ref[...], dh)
        dx_ref[...] = dy_ref[...] + dx
        h_ref[...] = h

        @pl.when(i == 0)
        def _():
            dg_ref[...] = dg

        @pl.when(i > 0)
        def _():
            dg_ref[...] += dg

    row = lambda i: (i, 0)
    one = lambda i: (0, 0)
    tiles = pl.BlockSpec((N_SB_HEADS, 1, HEAD_DIM, tm), lambda i: (0, i, 0, 0))
    return pl.pallas_call(
        body, name="mix_in_bwd",
        grid=(S // tm,),
        in_specs=[
            pl.BlockSpec((tm, 2 * D_CONV), row),
            _heads_spec(N_SB_HEADS, tm), tiles, tiles,
            _heads_spec(N_RET_HEADS, tm), _heads_spec(N_RET_HEADS, tm), _heads_spec(N_RET_HEADS, tm),
            _heads_spec(N_RET_HEADS, tm),
            pl.BlockSpec((tm, D_RET), row),
            pl.BlockSpec((tm, D_RET), row),
            pl.BlockSpec((D_IN_PROJ, D_MODEL), one, pipeline_mode=pl.Buffered(1)),
            pl.BlockSpec((tm, D_MODEL), row),
            pl.BlockSpec((1, D_MODEL), one),
            pl.BlockSpec((tm, D_MODEL), row),
        ],
        out_specs=[
            pl.BlockSpec((tm, D_MODEL), row),
            pl.BlockSpec((tm, D_MODEL), row),
            pl.BlockSpec((tm, D_IN_PROJ), row),
            pl.BlockSpec((1, D_MODEL), one),
        ],
        out_shape=[
            jax.ShapeDtypeStruct((S, D_MODEL), F32),
            jax.ShapeDtypeStruct((S, D_MODEL), BF16),
            jax.ShapeDtypeStruct((S, D_IN_PROJ), BF16),
            jax.ShapeDtypeStruct((1, D_MODEL), F32),
        ],
        compiler_params=_params(1),
    )(du, dq, dkt, dvt, dqr, dkr, dvr, dgr, cos, sin, w, x, g, dy)


def mix_out_fwd(y_conv, o_sb, o_ret, w, x, tm=ROW_TILE):
    S = x.shape[0]

    def body(yc_ref, sb_ref, rt_ref, w_ref, x_ref, o_ref, ycat_ref):
        ycat = jnp.concatenate(
            [yc_ref[...]] + [sb_ref[h].astype(BF16) for h in range(N_SB_HEADS)]
            + [rt_ref[h].astype(BF16) for h in range(N_RET_HEADS)], axis=1)
        ycat_ref[...] = ycat
        o_ref[...] = x_ref[...] + _dot(ycat, w_ref[...])

    row = lambda i: (i, 0)
    return pl.pallas_call(
        body, name="mix_out_fwd",
        grid=(S // tm,),
        in_specs=[
            pl.BlockSpec((tm, D_CONV), row),
            _heads_spec(N_SB_HEADS, tm),
            _heads_spec(N_RET_HEADS, tm),
            pl.BlockSpec((D_MODEL, D_MODEL), lambda i: (0, 0)),
            pl.BlockSpec((tm, D_MODEL), row),
        ],
        out_specs=[pl.BlockSpec((tm, D_MODEL), row), pl.BlockSpec((tm, D_MODEL), row)],
        out_shape=[jax.ShapeDtypeStruct((S, D_MODEL), F32), jax.ShapeDtypeStruct((S, D_MODEL), BF16)],
        compiler_params=_params(1),
    )(y_conv, o_sb, o_ret, w, x)


def mix_out_bwd(dy, w, tm=ROW_TILE):
    S = dy.shape[0]

    def body(dy_ref, w_ref, dyb_ref, dc_ref, do_ref, dot_ref, dr_ref):
        d = dy_ref[...].astype(BF16)
        dyb_ref[...] = d
        dycat = _dot_nt(d, w_ref[...])
        dc_ref[...] = dycat[:, :D_CONV]
        for h in range(N_SB_HEADS):
            do = _head(dycat[:, D_CONV:D_CONV + D_SB], h).astype(BF16)
            do_ref[h] = do
            dot_ref[h] = do.T
        for h in range(N_RET_HEADS):
            dr_ref[h] = _head(dycat[:, D_CONV + D_SB:], h)

    row = lambda i: (i, 0)
    return pl.pallas_call(
        body, name="mix_out_bwd",
        grid=(S // tm,),
        in_specs=[
            pl.BlockSpec((tm, D_MODEL), row),
            pl.BlockSpec((D_MODEL, D_MODEL), lambda i: (0, 0)),
        ],
        out_specs=[
            pl.BlockSpec((tm, D_MODEL), row),
            pl.BlockSpec((tm, D_CONV), row),
            _heads_spec(N_SB_HEADS, tm),
            pl.BlockSpec((N_SB_HEADS, HEAD_DIM, tm), lambda i: (0, 0, i)),
            _heads_spec(N_RET_HEADS, tm),
        ],
        out_shape=[
            jax.ShapeDtypeStruct((S, D_MODEL), BF16),
            jax.ShapeDtypeStruct((S, D_CONV), F32),
            jax.ShapeDtypeStruct((N_SB_HEADS, S, HEAD_DIM), BF16),
            jax.ShapeDtypeStruct((N_SB_HEADS, HEAD_DIM, S), BF16),
            jax.ShapeDtypeStruct((N_RET_HEADS, S, HEAD_DIM), F32),
        ],
        compiler_params=_params(1),
    )(dy, w)


def _rows_from(x, start, n):
    return pltpu.roll(x, (x.shape[0] - start) % x.shape[0], 0)[:n]


def _conv_ln(ypre, ln_g, ln_b):
    mu = jnp.mean(ypre, axis=-1, keepdims=True)
    yc = ypre - mu
    rstd = lax.rsqrt(jnp.mean(yc * yc, axis=-1, keepdims=True) + EPS)
    yn = yc * rstd
    return yn, rstd, yn * ln_g + ln_b


def conv_fwd(proj, cw, cb, ln_g, ln_b, tm=CONV_TILE):
    S = proj.shape[0]
    hb = tm // CONV_HALO

    def body(a_ref, b_ref, ap_ref, bp_ref, cw_ref, cb_ref, g_ref, bb_ref, y_ref, ypre_ref, v_sc):
        i = pl.program_id(0)
        prev = ap_ref[...] * _sigmoid(bp_ref[...])
        v_sc[pl.ds(0, CONV_HALO), :] = jnp.where(i > 0, prev, 0.0)
        v_sc[pl.ds(CONV_HALO, tm), :] = a_ref[...] * _sigmoid(b_ref[...])
        vext = v_sc[...]
        acc = jnp.zeros((tm, D_CONV), F32)
        for j in range(CONV_WIDTH):
            acc = acc + cw_ref[pl.ds(j, 1), :] * _rows_from(vext, CONV_HALO - (CONV_WIDTH - 1) + j, tm)
        ypre = acc + cb_ref[...]
        ypre_ref[...] = ypre
        _, _, z = _conv_ln(ypre, g_ref[...], bb_ref[...])
        y_ref[...] = (z * _sigmoid(z)).astype(BF16)

    one = lambda i: (0, 0)
    return pl.pallas_call(
        body, name="conv_fwd",
        grid=(S // tm,),
        in_specs=[
            pl.BlockSpec((tm, D_CONV), lambda i: (i, 0)),
            pl.BlockSpec((tm, D_CONV), lambda i: (i, 1)),
            pl.BlockSpec((CONV_HALO, D_CONV), lambda i: (jnp.maximum(i * hb - 1, 0), 0)),
            pl.BlockSpec((CONV_HALO, D_CONV), lambda i: (jnp.maximum(i * hb - 1, 0), 1)),
            pl.BlockSpec((CONV_HALO, D_CONV), one),
            pl.BlockSpec((1, D_CONV), one),
            pl.BlockSpec((1, D_CONV), one),
            pl.BlockSpec((1, D_CONV), one),
        ],
        out_specs=[pl.BlockSpec((tm, D_CONV), lambda i: (i, 0)), pl.BlockSpec((tm, D_CONV), lambda i: (i, 0))],
        out_shape=[jax.ShapeDtypeStruct((S, D_CONV), BF16), jax.ShapeDtypeStruct((S, D_CONV), F32)],
        scratch_shapes=[pltpu.VMEM((tm + CONV_HALO, D_CONV), F32)],
        compiler_params=_params(1),
    )(proj, proj, proj, proj, cw, cb, ln_g, ln_b)


def conv_bwd(dyc, ypre, proj, cw, ln_g, ln_b, tm=CONV_TILE):
    S = ypre.shape[0]
    hb = tm // CONV_HALO
    nblk = S // tm
    last_halo = S // CONV_HALO - 1

    def dpre(dy, yp, g, bb):
        yn, rstd, z = _conv_ln(yp, g, bb)
        sg = _sigmoid(z)
        dz = dy * (sg * (1.0 + z * (1.0 - sg)))
        dyn = dz * g
        d = rstd * (dyn - jnp.mean(dyn, axis=-1, keepdims=True) - yn * jnp.mean(dyn * yn, axis=-1, keepdims=True))
        return d, dz * yn, dz

    def body(dy_ref, yp_ref, dyn_ref, ypn_ref, a_ref, b_ref, ap_ref, bp_ref, cw_ref, g_ref, bb_ref,
             du_ref, dcw_ref, dsm_ref, d_sc, v_sc):
        i = pl.program_id(0)
        g = g_ref[...]
        bb = bb_ref[...]
        d_main, dgn, dz = dpre(dy_ref[...], yp_ref[...], g, bb)
        d_next, _, _ = dpre(dyn_ref[...], ypn_ref[...], g, bb)
        d_sc[pl.ds(0, tm), :] = d_main
        d_sc[pl.ds(tm, CONV_HALO), :] = jnp.where(i < nblk - 1, d_next, 0.0)
        a = a_ref[...]
        sb = _sigmoid(b_ref[...])
        prev = ap_ref[...] * _sigmoid(bp_ref[...])
        v_sc[pl.ds(0, CONV_HALO), :] = jnp.where(i > 0, prev, 0.0)
        v_sc[pl.ds(CONV_HALO, tm), :] = a * sb

        @pl.when(i == 0)
        def _():
            dcw_ref[...] = jnp.zeros_like(dcw_ref)
            dsm_ref[...] = jnp.zeros_like(dsm_ref)

        dext = d_sc[...]
        vext = v_sc[...]
        dv = jnp.zeros((tm, D_CONV), F32)
        for j in range(CONV_WIDTH):
            dv = dv + cw_ref[pl.ds(j, 1), :] * _rows_from(dext, CONV_WIDTH - 1 - j, tm)
            shifted = _rows_from(vext, CONV_HALO - (CONV_WIDTH - 1) + j, tm)
            dcw_ref[pl.ds(j, 1), :] += jnp.sum(d_main * shifted, axis=0, keepdims=True)
        du_ref[:, pl.ds(0, D_CONV)] = dv * sb
        du_ref[:, pl.ds(D_CONV, D_CONV)] = dv * a * sb * (1.0 - sb)
        dsm_ref[pl.ds(0, 1), :] += jnp.sum(d_main, axis=0, keepdims=True)
        dsm_ref[pl.ds(1, 1), :] += jnp.sum(dgn, axis=0, keepdims=True)
        dsm_ref[pl.ds(2, 1), :] += jnp.sum(dz, axis=0, keepdims=True)

    one = lambda i: (0, 0)
    prev_map = lambda c: (lambda i: (jnp.maximum(i * hb - 1, 0), c))
    next_map = lambda i: (jnp.minimum((i + 1) * hb, last_halo), 0)
    return pl.pallas_call(
        body, name="conv_bwd",
        grid=(nblk,),
        in_specs=[
            pl.BlockSpec((tm, D_CONV), lambda i: (i, 0)),
            pl.BlockSpec((tm, D_CONV), lambda i: (i, 0)),
            pl.BlockSpec((CONV_HALO, D_CONV), next_map),
            pl.BlockSpec((CONV_HALO, D_CONV), next_map),
            pl.BlockSpec((tm, D_CONV), lambda i: (i, 0)),
            pl.BlockSpec((tm, D_CONV), lambda i: (i, 1)),
            pl.BlockSpec((CONV_HALO, D_CONV), prev_map(0)),
            pl.BlockSpec((CONV_HALO, D_CONV), prev_map(1)),
            pl.BlockSpec((CONV_HALO, D_CONV), one),
            pl.BlockSpec((1, D_CONV), one),
            pl.BlockSpec((1, D_CONV), one),
        ],
        out_specs=[
            pl.BlockSpec((tm, 2 * D_CONV), lambda i: (i, 0)),
            pl.BlockSpec((CONV_HALO, D_CONV), one),
            pl.BlockSpec((8, D_CONV), one),
        ],
        out_shape=[
            jax.ShapeDtypeStruct((S, 2 * D_CONV), F32),
            jax.ShapeDtypeStruct((CONV_HALO, D_CONV), F32),
            jax.ShapeDtypeStruct((8, D_CONV), F32),
        ],
        scratch_shapes=[pltpu.VMEM((tm + CONV_HALO, D_CONV), F32), pltpu.VMEM((tm + CONV_HALO, D_CONV), F32)],
        compiler_params=_params(1),
    )(dyc, ypre, dyc, ypre, proj, proj, proj, proj, cw, ln_g, ln_b)


SB_GROUP = 8


def _softplus(z):
    neg_abs = lax.bitcast_convert_type(lax.bitcast_convert_type(z, jnp.uint32) | jnp.uint32(0x80000000), F32)
    return jnp.maximum(z, 0.0) + jnp.log(1.0 + jnp.exp(neg_abs))


def _full_groups(n, body):
    def step(t, c):
        body(t * SB_GROUP)
        return c

    lax.fori_loop(0, lax.div(n, SB_GROUP), step, 0)


def _last_group(n, step, body):
    r = lax.rem(n, SB_GROUP)
    for k in range(0, SB_GROUP, step):
        @pl.when(r == k)
        def _(k=k):
            body(k)


def _rows(xs):
    return xs[0] if len(xs) == 1 else jnp.concatenate(xs, axis=0)


def sb_fwd(q, k, v, comm=None, T=SB_TILE, Q=SB_FWD_ROWS):
    H, S, dh = q.shape
    M = Q // T

    def body(q_ref, k_ref, v_ref, o_ref, tot_ref, acc_sc, car_sc):
        qb = pl.program_id(1)
        qv = q_ref[...]
        row = lax.broadcasted_iota(jnp.int32, (T, T), 0)
        col = lax.broadcasted_iota(jnp.int32, (T, T), 1)
        tri = jnp.where(row >= col, 1.0, 0.0).astype(BF16)
        qrow = lax.broadcasted_iota(jnp.int32, (Q, T), 0)
        kcol = lax.broadcasted_iota(jnp.int32, (Q, T), 1)
        causal = {d + 1: kcol + d * T < qrow for d in range(M)}
        acc_sc[...] = jnp.zeros_like(acc_sc)
        car_sc[...] = jnp.zeros_like(car_sc)

        def logits(kb, masked):
            ks = k_ref[pl.ds(pl.multiple_of(kb * T, T), T), :]
            z = _dot_nt(qv, ks)
            nb = _softplus(z)
            if masked:
                nb = jnp.where(causal[masked], nb, 0.0)
            return z, nb.astype(BF16)

        def group(kbs, diag):
            parts = [logits(kb, d) for kb, d in zip(kbs, diag)]
            pall = _dot(_rows([nb for _, nb in parts]), tri)
            carry = car_sc[...]
            out = None
            for j, kb in enumerate(kbs):
                p = pall[j * Q:(j + 1) * Q]
                vs = v_ref[pl.ds(pl.multiple_of(kb * T, T), T), :]
                w = jnp.exp((parts[j][0] - carry) - p)
                if diag[j]:
                    w = jnp.where(causal[diag[j]], w, 0.0)
                o = _dot(w.astype(BF16), vs)
                out = o if out is None else out + o
                carry = carry + p[:, 0:1]
            acc_sc[...] += out
            car_sc[...] = carry

        full = M * qb
        _last_group(full, M, lambda r: group([full + d for d in reversed(range(M))] + [full - 1 - o for o in range(r)],
                                             [d + 1 for d in reversed(range(M))] + [0] * r))
        rest = full - lax.rem(full, SB_GROUP)
        _full_groups(rest, lambda o: group([rest - 1 - o - j for j in range(SB_GROUP)], [0] * SB_GROUP))
        o_ref[...] = acc_sc[...]
        tot_ref[...] = car_sc[...]

    return _call(
        body, (q, k, v), comm, name="sb_fwd",
        grid=(H, S // Q),
        in_specs=[
            pl.BlockSpec((None, Q, dh), lambda h, i: (h, i, 0)),
            pl.BlockSpec((None, S, dh), lambda h, i: (h, 0, 0)),
            pl.BlockSpec((None, S, dh), lambda h, i: (h, 0, 0)),
        ],
        out_specs=[
            pl.BlockSpec((None, Q, dh), lambda h, i: (h, i, 0)),
            pl.BlockSpec((None, Q, 1), lambda h, i: (h, i, 0)),
        ],
        out_shape=[jax.ShapeDtypeStruct((H, S, dh), F32), jax.ShapeDtypeStruct((H, S, 1), F32)],
        scratch_shapes=[pltpu.VMEM((Q, dh), F32), pltpu.VMEM((Q, 1), F32)],
    )


def sb_bwd(q, k, v, do, qt, dot, tot, comm=None, T=SB_TILE, Q=SB_ROWS):
    H, S, dh = q.shape
    nt = S // T
    M = Q // T

    def body(q_ref, k_ref, v_ref, do_ref, qt_ref, dot_ref, tot_ref, dq_ref, dk_ref, dv_ref, acc_sc, rc_sc, gc_sc):
        qb = pl.program_id(1)
        qv = q_ref[...]
        dov = do_ref[...]
        qtv = qt_ref[...]
        dotv = dot_ref[...]
        row = lax.broadcasted_iota(jnp.int32, (T, T), 0)
        col = lax.broadcasted_iota(jnp.int32, (T, T), 1)
        before = jnp.where(row < col, 1.0, 0.0).astype(BF16)
        qrow = lax.broadcasted_iota(jnp.int32, (Q, T), 0)
        kcol = lax.broadcasted_iota(jnp.int32, (Q, T), 1)
        causal = {d + 1: kcol + d * T < qrow for d in range(M)}
        acc_sc[...] = jnp.zeros_like(acc_sc)
        rc_sc[...] = tot_ref[...]
        gc_sc[...] = jnp.zeros_like(gc_sc)

        @pl.when(qb == 0)
        def _():
            dk_ref[...] = jnp.zeros_like(dk_ref)
            dv_ref[...] = jnp.zeros_like(dv_ref)

        def first(kb, masked):
            start = pl.multiple_of(kb * T, T)
            z = _dot_nt(qv, k_ref[pl.ds(start, T), :])
            nb = _softplus(z)
            sig = jnp.exp(z - nb)
            if masked:
                nb = jnp.where(causal[masked], nb, 0.0)
            dw = _dot_nt(dov, v_ref[pl.ds(start, T), :])
            return z, sig, nb.astype(BF16), dw

        def group(kbs, diag):
            parts = [first(kb, d) for kb, d in zip(kbs, diag)]
            pall = _dot(_rows([p[2] for p in parts]), before)
            rc = rc_sc[...]
            ws, gs, ghs = [], [], []
            for j in range(len(kbs)):
                z, _, nbh, dw = parts[j]
                p = pall[j * Q:(j + 1) * Q]
                w = jnp.exp((z - rc) + p)
                rc = rc - (p[:, T - 1:T] + nbh[:, T - 1:T].astype(F32))
                if diag[j]:
                    w = jnp.where(causal[diag[j]], w, 0.0)
                g = dw * w
                ws.append(w.astype(BF16))
                gs.append(g)
                ghs.append(g.astype(BF16))
            glall = _dot(_rows(ghs), before)
            gc = gc_sc[...]
            dq = None
            for j, kb in enumerate(kbs):
                ks = k_ref[pl.ds(pl.multiple_of(kb * T, T), T), :]
                gl = glall[j * Q:(j + 1) * Q]
                dz = gs[j] - parts[j][1] * (gs[j] + (gl + gc))
                gc = gc + gl[:, T - 1:T] + ghs[j][:, T - 1:T].astype(F32)
                if diag[j]:
                    dz = jnp.where(causal[diag[j]], dz, 0.0)
                dzb = dz.astype(BF16)
                d = _dot(dzb, ks)
                dq = d if dq is None else dq + d
                dk_ref[kb] += _dot(qtv, dzb)
                dv_ref[kb] += _dot(dotv, ws[j])
            acc_sc[...] += dq
            rc_sc[...] = rc
            gc_sc[...] = gc

        full = M * qb
        _full_groups(full, lambda o: group([o + j for j in range(SB_GROUP)], [0] * SB_GROUP))
        rest = full - lax.rem(full, SB_GROUP)
        _last_group(full, M, lambda r: group([rest + j for j in range(r)] + [full + d for d in range(M)],
                                             [0] * r + [d + 1 for d in range(M)]))
        dq_ref[...] = acc_sc[...]

    return _call(
        body, (q, k, v, do, qt, dot, tot), comm, name="sb_bwd",
        grid=(H, S // Q),
        in_specs=[
            pl.BlockSpec((None, Q, dh), lambda h, i: (h, i, 0)),
            pl.BlockSpec((None, S, dh), lambda h, i: (h, 0, 0)),
            pl.BlockSpec((None, S, dh), lambda h, i: (h, 0, 0)),
            pl.BlockSpec((None, Q, dh), lambda h, i: (h, i, 0)),
            pl.BlockSpec((None, dh, Q), lambda h, i: (h, 0, i)),
            pl.BlockSpec((None, dh, Q), lambda h, i: (h, 0, i)),
            pl.BlockSpec((None, Q, 1), lambda h, i: (h, i, 0)),
        ],
        out_specs=[
            pl.BlockSpec((None, Q, dh), lambda h, i: (h, i, 0)),
            pl.BlockSpec((None, nt, dh, T), lambda h, i: (h, 0, 0, 0)),
            pl.BlockSpec((None, nt, dh, T), lambda h, i: (h, 0, 0, 0)),
        ],
        out_shape=[jax.ShapeDtypeStruct((H, S, dh), F32), jax.ShapeDtypeStruct((H, nt, dh, T), F32),
                   jax.ShapeDtypeStruct((H, nt, dh, T), F32)],
        scratch_shapes=[pltpu.VMEM((Q, dh), F32), pltpu.VMEM((Q, 1), F32), pltpu.VMEM((Q, 1), F32)],
    )


def _ret_tables(T=RET_TILE):
    hh = jnp.arange(N_RET_HEADS, dtype=F32)
    log_gamma = jnp.log1p(-jnp.exp2(-5.0 - hh))
    idx = jnp.arange(T, dtype=F32)
    diff = idx[:, None] - idx[None, :]
    ci = (jnp.arange(T) // 64)
    same = ci[:, None] == ci[None, :]
    earlier = ci[None, :] < ci[:, None]
    dist = jnp.where(same, jnp.abs(diff), diff)
    dmat = jnp.where(same | earlier, jnp.exp(log_gamma[:, None, None] * dist[None]), 0.0)
    ones = jnp.ones((1, 1, HEAD_DIM), F32)
    qdec = jnp.exp(log_gamma[:, None] * (idx + 1.0)[None, :])[:, :, None] * ones
    kdec = jnp.exp(log_gamma[:, None] * (T - 1.0 - idx)[None, :])[:, :, None] * ones
    bdec = jnp.exp(log_gamma * T)[:, None, None] * jnp.ones((1, HEAD_DIM, HEAD_DIM), F32)
    return dmat, qdec, kdec, bdec


def _rope_tables(S):
    half = HEAD_DIM // 2
    inv = 1.0 / (ROPE_BASE ** (jnp.arange(half, dtype=F32) / half))
    ang = jnp.arange(S).astype(F32)[:, None] * inv[None, :]
    c = jnp.cos(ang)
    s = jnp.sin(ang)
    cos = jnp.tile(jnp.concatenate([c, c], axis=1), (1, N_RET_HEADS))
    sin = jnp.tile(jnp.concatenate([-s, s], axis=1), (1, N_RET_HEADS))
    return cos, sin


def ret_fwd(q, k, v, gate, ng, tables, T=RET_TILE):
    H, S, dh = q.shape
    dmat, qdec, kdec, bdec = tables

    def body(q_ref, k_ref, v_ref, gt_ref, ng_ref, dm_ref, qd_ref, kd_ref, bd_ref, o_ref, y_ref, st_ref, s_sc):
        n = pl.program_id(1)

        @pl.when(n == 0)
        def _():
            s_sc[...] = jnp.zeros_like(s_sc)

        qv = q_ref[...]
        kv = k_ref[...]
        vv = v_ref[...]
        state = s_sc[...]
        st_ref[...] = state
        sc = (_dot_nt(qv, kv) * dm_ref[...]).astype(BF16)
        qd = (qv.astype(F32) * qd_ref[...]).astype(BF16)
        y = _dot(sc, vv) + _dot(qd, state.astype(BF16))
        y_ref[...] = y
        kd = (kv.astype(F32) * kd_ref[...]).astype(BF16)
        s_sc[...] = bd_ref[...] * state + _dot_tn(kd, vv)
        mu = jnp.mean(y, axis=-1, keepdims=True)
        yc = y - mu
        yn = yc * lax.rsqrt(jnp.mean(yc * yc, axis=-1, keepdims=True) + EPS)
        gt = gt_ref[...]
        o_ref[...] = gt * _sigmoid(gt) * (yn * ng_ref[...])

    blk = lambda h, n: (h, n, 0)
    head = lambda h, n: (h, 0, 0)
    return pl.pallas_call(
        body, name="ret_fwd",
        grid=(H, S // T),
        in_specs=[
            pl.BlockSpec((None, T, dh), blk),
            pl.BlockSpec((None, T, dh), blk),
            pl.BlockSpec((None, T, dh), blk),
            pl.BlockSpec((None, T, dh), blk),
            pl.BlockSpec((None, 1, dh), head),
            pl.BlockSpec((None, T, T), head),
            pl.BlockSpec((None, T, dh), head),
            pl.BlockSpec((None, T, dh), head),
            pl.BlockSpec((None, dh, dh), head),
        ],
        out_specs=[
            pl.BlockSpec((None, T, dh), blk),
            pl.BlockSpec((None, T, dh), blk),
            pl.BlockSpec((None, None, dh, dh), lambda h, n: (h, n, 0, 0)),
        ],
        out_shape=[
            jax.ShapeDtypeStruct((H, S, dh), F32),
            jax.ShapeDtypeStruct((H, S, dh), F32),
            jax.ShapeDtypeStruct((H, S // T, dh, dh), F32),
        ],
        scratch_shapes=[pltpu.VMEM((dh, dh), F32)],
        compiler_params=_params(2),
    )(q, k, v, gate, ng, dmat, qdec, kdec, bdec)


def ret_bwd(do, q, k, v, gate, ng, y, states, tables, T=RET_TILE):
    H, S, dh = q.shape
    nb = S // T
    dmat, qdec, kdec, bdec = tables

    def body(do_ref, q_ref, k_ref, v_ref, gt_ref, ng_ref, y_ref, st_ref, dm_ref, qd_ref, kd_ref, bd_ref,
             dq_ref, dk_ref, dv_ref, dgt_ref, dng_ref, u_sc):
        n = pl.program_id(1)

        @pl.when(n == 0)
        def _():
            u_sc[...] = jnp.zeros_like(u_sc)
            dng_ref[...] = jnp.zeros_like(dng_ref)

        yv = y_ref[...]
        mu = jnp.mean(yv, axis=-1, keepdims=True)
        yc = yv - mu
        rstd = lax.rsqrt(jnp.mean(yc * yc, axis=-1, keepdims=True) + EPS)
        yn = yc * rstd
        gt = gt_ref[...]
        sg = _sigmoid(gt)
        ngv = ng_ref[...]
        dout = do_ref[...]
        dgt_ref[...] = dout * (yn * ngv) * (sg * (1.0 + gt * (1.0 - sg)))
        dn = dout * (gt * sg)
        dng_ref[...] += jnp.sum(dn * yn, axis=0, keepdims=True)
        dyn = dn * ngv
        dy = rstd * (dyn - jnp.mean(dyn, axis=-1, keepdims=True) - yn * jnp.mean(dyn * yn, axis=-1, keepdims=True))
        dyb = dy.astype(BF16)

        qv = q_ref[...]
        kv = k_ref[...]
        vv = v_ref[...]
        dm = dm_ref[...]
        qdt = qd_ref[...]
        kdt = kd_ref[...]
        sb = st_ref[...].astype(BF16)
        u = u_sc[...]
        ub = u.astype(BF16)
        dqk = (_dot_nt(dyb, vv) * dm).astype(BF16)
        sc = (_dot_nt(qv, kv) * dm).astype(BF16)
        qd = (qv.astype(F32) * qdt).astype(BF16)
        kd = (kv.astype(F32) * kdt).astype(BF16)
        dq_ref[...] = _dot(dqk, kv) + qdt * _dot_nt(dyb, sb)
        dk_ref[...] = _dot_tn(dqk, qv) + kdt * _dot_nt(vv, ub)
        dv_ref[...] = _dot_tn(sc, dyb) + _dot(kd, ub)
        u_sc[...] = bd_ref[...] * u + _dot_tn(qd, dyb)

    blk = lambda h, n: (h, nb - 1 - n, 0)
    head = lambda h, n: (h, 0, 0)
    return pl.pallas_call(
        body, name="ret_bwd",
        grid=(H, nb),
        in_specs=[
            pl.BlockSpec((None, T, dh), blk),
            pl.BlockSpec((None, T, dh), blk),
            pl.BlockSpec((None, T, dh), blk),
            pl.BlockSpec((None, T, dh), blk),
            pl.BlockSpec((None, T, dh), blk),
            pl.BlockSpec((None, 1, dh), head),
            pl.BlockSpec((None, T, dh), blk),
            pl.BlockSpec((None, None, dh, dh), lambda h, n: (h, nb - 1 - n, 0, 0)),
            pl.BlockSpec((None, T, T), head),
            pl.BlockSpec((None, T, dh), head),
            pl.BlockSpec((None, T, dh), head),
            pl.BlockSpec((None, dh, dh), head),
        ],
        out_specs=[
            pl.BlockSpec((None, T, dh), blk),
            pl.BlockSpec((None, T, dh), blk),
            pl.BlockSpec((None, T, dh), blk),
            pl.BlockSpec((None, T, dh), blk),
            pl.BlockSpec((None, 1, dh), head),
        ],
        out_shape=[jax.ShapeDtypeStruct((H, S, dh), F32)] * 4 + [jax.ShapeDtypeStruct((H, 1, dh), F32)],
        scratch_shapes=[pltpu.VMEM((dh, dh), F32)],
        compiler_params=_params(2),
    )(do, q, k, v, gate, ng, y, states, dmat, qdec, kdec, bdec)


def loss_head(x, g, target, tm=ROW_TILE):
    S = x.shape[0]

    def body(x_ref, g_ref, t_ref, loss_ref, dx_ref, dg_ref):
        i = pl.program_id(0)
        xv = x_ref[...]
        gv = g_ref[...]
        _, xhat = _rms_stats(xv)
        err = xhat * gv - t_ref[...]
        part = 0.5 * jnp.sum(jnp.mean(err * err, axis=-1, keepdims=True), axis=0, keepdims=True)
        dx, _, dg = _rms_bwd(xv, gv, err * (1.0 / D_MODEL))
        dx_ref[...] = dx
        part = jnp.broadcast_to(part, (1, 128))

        @pl.when(i == 0)
        def _():
            loss_ref[...] = part
            dg_ref[...] = dg

        @pl.when(i > 0)
        def _():
            loss_ref[...] += part
            dg_ref[...] += dg

    row = lambda i: (i, 0)
    one = lambda i: (0, 0)
    return pl.pallas_call(
        body, name="loss_head",
        grid=(S // tm,),
        in_specs=[pl.BlockSpec((tm, D_MODEL), row), pl.BlockSpec((1, D_MODEL), one), pl.BlockSpec((tm, D_MODEL), row)],
        out_specs=[pl.BlockSpec((1, 128), one), pl.BlockSpec((tm, D_MODEL), row), pl.BlockSpec((1, D_MODEL), one)],
        out_shape=[
            jax.ShapeDtypeStruct((1, 128), F32),
            jax.ShapeDtypeStruct((S, D_MODEL), F32),
            jax.ShapeDtypeStruct((1, D_MODEL), F32),
        ],
        compiler_params=_params(1),
    )(x, g, target)


def adamw(parts, w, m, v, tr, transposed=False):
    L, R, C = w.shape
    nr = R // tr
    c1 = 1.0 / (1.0 - ADAM_B1 ** ADAM_STEP)
    c2 = 1.0 / (1.0 - ADAM_B2 ** ADAM_STEP)

    def body(*refs):
        p_refs = refs[:L]
        w_ref, m_ref, v_ref, g_ref, d_ref, mo_ref, vo_ref = refs[L:]
        l = pl.program_id(0)
        g = None
        for d in range(N_DEV):
            pd = p_refs[0][d].astype(F32)
            for ll in range(1, L):
                pd = jnp.where(l == ll, p_refs[ll][d].astype(F32), pd)
            g = pd if g is None else g + pd
        if transposed:
            g = g.T
        mn = ADAM_B1 * m_ref[...] + (1.0 - ADAM_B1) * g
        vn = ADAM_B2 * v_ref[...] + (1.0 - ADAM_B2) * (g * g)
        g_ref[...] = g
        mo_ref[...] = mn
        vo_ref[...] = vn
        d_ref[...] = -ADAM_LR * ((mn * c1) / (jnp.sqrt(vn * c2) + ADAM_EPS) + ADAM_WD * w_ref[...])

    def part_spec(ll):
        def block(l, i):
            return jnp.where(l == ll, i, jnp.where(l < ll, 0, nr - 1))
        if transposed:
            return pl.BlockSpec((N_DEV, C, tr), lambda l, i: (0, 0, block(l, i)))
        return pl.BlockSpec((N_DEV, tr, C), lambda l, i: (0, block(l, i), 0))

    blk = pl.BlockSpec((None, tr, C), lambda l, i: (l, i, 0))
    return pl.pallas_call(
        body, name="adamw",
        grid=(L, nr),
        in_specs=[part_spec(ll) for ll in range(L)] + [blk] * 3,
        out_specs=[blk] * 4,
        out_shape=[jax.ShapeDtypeStruct((L, R, C), F32)] * 4,
        compiler_params=_params(2),
    )(*parts, w, m, v)


def _my_id():
    return lax.axis_index("x") * 4 + lax.axis_index("y") * 2 + lax.axis_index("c")


def _peer(k):
    x, y, c = lax.axis_index("x"), lax.axis_index("y"), lax.axis_index("c")
    px = 1 - x if k & 4 else x
    py = 1 - y if k & 2 else y
    pc = 1 - c if k & 1 else c
    return (px, py, pc), px * 4 + py * 2 + pc


GATHER = "gather"
EXCHANGE = "exchange"


def _copies(kind, ins, outs, send_sems, recv_sems, local_sems, receive_side):
    me = _my_id()
    local, sends, recvs = [], [], []
    for t in range(len(ins)):
        src = ins[t] if kind == GATHER else ins[t].at[me]
        local.append(pltpu.make_async_copy(src, outs[t].at[me], local_sems.at[t]))
    for k in range(1, N_DEV):
        dev, pid = _peer(k)
        for t in range(len(ins)):
            sems = dict(send_sem=send_sems.at[t, k - 1], recv_sem=recv_sems.at[t, k - 1],
                        device_id=dev, device_id_type=pl.DeviceIdType.MESH)
            src = ins[t] if kind == GATHER else ins[t].at[pid]
            sends.append(pltpu.make_async_remote_copy(src_ref=src, dst_ref=outs[t].at[me], **sems))
            if receive_side:
                recvs.append(pltpu.make_async_remote_copy(src_ref=src, dst_ref=outs[t].at[pid], **sems))
    return local, sends, recvs


def _comm_start(kind, ins, outs, sems):
    local, sends, _ = _copies(kind, ins, outs, *sems, receive_side=False)
    for cp in local + sends:
        cp.start()


def _comm_wait(kind, ins, outs, sems):
    local, sends, recvs = _copies(kind, ins, outs, *sems, receive_side=True)
    for cp in recvs:
        cp.wait_recv()
    for cp in sends:
        cp.wait_send()
    for cp in local:
        cp.wait()


def _entries(arrays):
    ops = [a[0] if isinstance(a, tuple) else a for a in arrays]
    idx = [a[1] if isinstance(a, tuple) else None for a in arrays]
    return ops, idx


def _views(refs, idx):
    return [r if i is None else r.at[i] for r, i in zip(refs, idx)]


def _comm_shapes(kind, arrays):
    n = len(arrays)
    ops, idx = _entries(arrays)
    shapes = [a.shape if i is None else a.shape[1:] for a, i in zip(ops, idx)]
    out_shape = [jax.ShapeDtypeStruct(((N_DEV,) + s) if kind == GATHER else s, a.dtype) for a, s in zip(ops, shapes)]
    sems = [pltpu.SemaphoreType.DMA((n, N_DEV - 1)), pltpu.SemaphoreType.DMA((n, N_DEV - 1)),
            pltpu.SemaphoreType.DMA((n,))]
    return out_shape, sems


def communicate(kind, arrays):
    n = len(arrays)
    ops, idx = _entries(arrays)

    def body(*refs):
        ins, outs, sems = _views(refs[:n], idx), refs[n:2 * n], refs[2 * n:]
        _comm_start(kind, ins, outs, sems)
        _comm_wait(kind, ins, outs, sems)

    out_shape, sems = _comm_shapes(kind, arrays)
    any_spec = pl.BlockSpec(memory_space=pl.ANY)
    return pl.pallas_call(
        body, name=kind, in_specs=[any_spec] * n, out_specs=[any_spec] * n, out_shape=out_shape, scratch_shapes=sems,
    )(*ops)


def gather_two_level(arrays):
    n = len(arrays)
    ops, idx = _entries(arrays)

    def body(*refs):
        ins, outs = _views(refs[:n], idx), refs[n:2 * n]
        send_sems, recv_sems, local_sems = refs[2 * n:]
        x, y, c = lax.axis_index("x"), lax.axis_index("y"), lax.axis_index("c")
        me, sibling = (x, y, c), (x, y, 1 - c)
        chips = [(1 - x, y), (x, 1 - y), (1 - x, 1 - y)]

        def slot(px, py, pc):
            return px * 4 + py * 2 + pc

        def copy(t, k, src, owner, to):
            return pltpu.make_async_remote_copy(
                src_ref=src, dst_ref=outs[t].at[slot(*owner)], send_sem=send_sems.at[t, k], recv_sem=recv_sems.at[t, k],
                device_id=to, device_id_type=pl.DeviceIdType.MESH)

        local = [pltpu.make_async_copy(ins[t], outs[t].at[slot(*me)], local_sems.at[t]) for t in range(n)]
        first = [copy(t, 0, ins[t], me, sibling) for t in range(n)]
        first += [copy(t, 1 + j, ins[t], me, (*chip, c)) for j, chip in enumerate(chips) for t in range(n)]
        for cp in local + first:
            cp.start()
        passed = []
        for j, chip in enumerate(chips):
            for t in range(n):
                copy(t, 1 + j, ins[t], (*chip, c), me).wait_recv()
                cp = copy(t, 4 + j, outs[t].at[slot(*chip, c)], (*chip, c), sibling)
                cp.start()
                passed.append(cp)
        for t in range(n):
            copy(t, 0, ins[t], sibling, me).wait_recv()
            for j, chip in enumerate(chips):
                copy(t, 4 + j, ins[t], (*chip, 1 - c), me).wait_recv()
        for cp in first + passed:
            cp.wait_send()
        for cp in local:
            cp.wait()

    out_shape, sems = _comm_shapes(GATHER, arrays)
    any_spec = pl.BlockSpec(memory_space=pl.ANY)
    return pl.pallas_call(
        body, name="gather_two_level", in_specs=[any_spec] * n, out_specs=[any_spec] * n, out_shape=out_shape,
        scratch_shapes=sems,
    )(*ops)


def _call(body, operands, comm, *, name, grid, in_specs, out_specs, out_shape, scratch_shapes):
    if comm is None:
        outs = pl.pallas_call(body, name=name, grid=grid, in_specs=in_specs, out_specs=out_specs, out_shape=out_shape,
                              scratch_shapes=scratch_shapes, compiler_params=_params(len(grid)))(*operands)
        return outs, []
    kind, arrays = comm
    comm_ops, comm_idx = _entries(arrays)
    n, n_in, n_out, n_sc = len(arrays), len(in_specs), len(out_specs), len(scratch_shapes)

    def carrier(*refs):
        ins, cins = refs[:n_in], _views(refs[n_in:n_in + n], comm_idx)
        refs = refs[n_in + n:]
        outs, couts = refs[:n_out], refs[n_out:n_out + n]
        scratch, sems = refs[n_out + n:n_out + n + n_sc], refs[n_out + n + n_sc:]
        steps = [pl.program_id(a) for a in range(len(grid))]
        first = functools.reduce(jnp.logical_and, [s == 0 for s in steps])
        last = functools.reduce(jnp.logical_and, [s == g - 1 for s, g in zip(steps, grid)])

        @pl.when(first)
        def _():
            _comm_start(kind, cins, couts, sems)

        body(*ins, *outs, *scratch)

        @pl.when(last)
        def _():
            _comm_wait(kind, cins, couts, sems)

    comm_shape, sems = _comm_shapes(kind, arrays)
    any_spec = pl.BlockSpec(memory_space=pl.ANY)
    outs = pl.pallas_call(
        carrier, name=f"{name}_{kind}", grid=grid,
        in_specs=list(in_specs) + [any_spec] * n,
        out_specs=list(out_specs) + [any_spec] * n,
        out_shape=list(out_shape) + comm_shape,
        scratch_shapes=list(scratch_shapes) + sems,
        compiler_params=_params(len(grid)),
    )(*operands, *comm_ops)
    return outs[:n_out], outs[n_out:]


def _row(v):
    return v.reshape(1, -1)


def _pad_taps(cw):
    return jnp.concatenate([cw, jnp.zeros((CONV_HALO - CONV_WIDTH, D_CONV), F32)], axis=0)


COL_SHARDED = ("ffn1_w_in", "mix_w_in", "ffn2_w_in")
ROW_SHARDED = ("ffn1_w_out", "mix_w_out", "ffn2_w_out")
SMALL = ("ffn1_norm", "mix_norm", "conv_b", "conv_ln_g", "conv_ln_b", "ret_norm_g", "ffn2_norm", "final_norm")
WEIGHTS = ("ffn1_norm", "ffn1_w_in", "ffn1_w_out", "mix_norm", "mix_w_in", "conv_w", "conv_b", "conv_ln_g",
           "conv_ln_b", "ret_norm_g", "mix_w_out", "ffn2_norm", "ffn2_w_in", "ffn2_w_out", "final_norm")
SMALL_ROWS = 32

FFN1 = ("ffn1_w_in", "ffn1_w_out")
MIX = ("mix_w_in", "mix_w_out")
FFN2 = ("ffn2_w_in", "ffn2_w_out")
STAGE_A = [(n, 0) for n in FFN1]
STAGE_B = [(n, 0) for n in MIX] + [("conv_w", None)]
STAGE_C = [(n, 0) for n in FFN2] + [(n, 1) for n in FFN1 + MIX + FFN2]
STAGE_D = [(n, 1) for n in FFN2]
STAGE_E = [(n, 1) for n in MIX + FFN1] + [(n, 0) for n in FFN2]
STAGE_F = [(n, 0) for n in MIX]
STAGE_G = [("ffn1_w_in", 0)]
STAGE_H = [("ffn1_w_out", 0)]


def _natural(name, got):
    if name == "conv_w":
        return got.transpose(1, 2, 0, 3).reshape(DEPTH, CONV_WIDTH, D_CONV)
    return got.reshape(-1, D_MODEL)


def _by_device(grad):
    return grad.reshape(N_DEV, -1, D_MODEL)


def _pack_small(g):
    flat = jnp.concatenate([g[n].reshape(-1) for n in SMALL] + [g["conv_w"].reshape(-1)])
    flat = jnp.concatenate([flat, jnp.zeros((SMALL_ROWS * D_MODEL - flat.shape[0],), F32)])
    return flat.reshape(SMALL_ROWS, D_MODEL)


def _unpack_small(buf, like):
    flat = buf.reshape(-1)
    out, off = {}, 0
    for n in SMALL:
        size = int(np.prod(like[n].shape))
        out[n] = flat[off:off + size].reshape(like[n].shape)
        off += size
    size = DEPTH * CONV_WIDTH * D_CONV
    out["conv_w"] = flat[off:off + size].reshape(DEPTH, CONV_WIDTH, D_CONV)
    return out


def kernel(x, ffn1_norm, ffn1_w_in, ffn1_w_out, mix_norm, mix_w_in, conv_w, conv_b, conv_ln_g, conv_ln_b, ret_norm_g, mix_w_out, ffn2_norm, ffn2_w_in, ffn2_w_out, final_norm, loss_target, m_ffn1_norm, m_ffn1_w_in, m_ffn1_w_out, m_mix_norm, m_mix_w_in, m_conv_w, m_conv_b, m_conv_ln_g, m_conv_ln_b, m_ret_norm_g, m_mix_w_out, m_ffn2_norm, m_ffn2_w_in, m_ffn2_w_out, m_final_norm, v_ffn1_norm, v_ffn1_w_in, v_ffn1_w_out, v_mix_norm, v_mix_w_in, v_conv_w, v_conv_b, v_conv_ln_g, v_conv_ln_b, v_ret_norm_g, v_mix_w_out, v_ffn2_norm, v_ffn2_w_in, v_ffn2_w_out, v_final_norm):
    args = locals()
    w = {n: args[n] for n in WEIGHTS}
    m = {n: args["m_" + n] for n in WEIGHTS}
    v = {n: args["v_" + n] for n in WEIGHTS}
    me = _my_id()
    x = x[0]
    target = loss_target[0]
    S = x.shape[0]
    cos, sin = _rope_tables(S)
    tables = _ret_tables()

    full = {}

    wb = {n: (w[n].transpose(0, 2, 1) if n in COL_SHARDED else w[n]).astype(BF16) for n in COL_SHARDED + ROW_SHARDED}

    def gather(keys):
        return GATHER, [w[n] if n == "conv_w" else (wb[n], l) for n, l in keys]

    def gathered(keys, got):
        for (n, l), g in zip(keys, got):
            full[(n, l)] = _natural(n, g)

    gathered(STAGE_A, gather_two_level(gather(STAGE_A)[1]))

    saved = []
    for l in range(DEPTH):
        sv = {"x0": x}
        (x, sv["gate1"], sv["up1"]), got = ffn_fwd(x, _row(w["ffn1_norm"][l]), full[("ffn1_w_in", l)],
                                                   full[("ffn1_w_out", l)], gather(STAGE_B) if l == 0 else None)
        gathered(STAGE_B if l == 0 else [], got)
        sv["x1"] = x
        (sv["u"], sv["q_sb"], sv["k_sb"], sv["v_sb"], sv["qt_sb"], sv["q_r"], sv["k_r"], sv["v_r"],
         sv["g_r"]) = mix_in_fwd(x, _row(w["mix_norm"][l]), full[("mix_w_in", l)], cos, sin)
        cw = _pad_taps(full[("conv_w", None)][l])
        y_conv, sv["ypre"] = conv_fwd(sv["u"], cw, _row(w["conv_b"][l]), _row(w["conv_ln_g"][l]), _row(w["conv_ln_b"][l]))
        (o_sb, sv["tot"]), got = sb_fwd(sv["q_sb"], sv["k_sb"], sv["v_sb"], gather(STAGE_C) if l == 0 else None)
        gathered(STAGE_C if l == 0 else [], got)
        ng = w["ret_norm_g"][l].reshape(N_RET_HEADS, 1, HEAD_DIM)
        o_r, sv["y_r"], sv["states"] = ret_fwd(sv["q_r"], sv["k_r"], sv["v_r"], sv["g_r"], ng, tables)
        x, sv["ycat"] = mix_out_fwd(y_conv, o_sb, o_r, full[("mix_w_out", l)], x)
        sv["x2"] = x
        (x, sv["gate2"], sv["up2"]), _ = ffn_fwd(x, _row(w["ffn2_norm"][l]), full[("ffn2_w_in", l)],
                                                 full[("ffn2_w_out", l)])
        saved.append(sv)

    loss_acc, dx, dg_final = loss_head(x, _row(w["final_norm"]), target)
    loss = lax.psum(loss_acc[0, 0], ("x", "y", "c"))

    g = {"final_norm": dg_final.reshape(D_MODEL)}
    received = {}

    def exchange(keys, extra=(), dtype=F32):
        return EXCHANGE, [_by_device(g[(n, l)]).astype(dtype) for n, l in keys] + list(extra)

    def exchanged(keys, got):
        for key, p in zip(keys, got):
            received[key] = p

    def ffn_back(dx, x_in, gate, up, norm, names, l, comm=None):
        (dx, h, dyh, dgate, dup, hid, dg), got = ffn_bwd(dx, x_in, _row(norm), gate, up, full[(names[0], l)],
                                                         full[(names[1], l)], comm)
        g[(names[0], l)] = matmul_tn([dgate, dup], h, FF_TILE, D_MODEL, name="ffn_dw_in")
        if [(names[0], l)] == STAGE_G:
            g[(names[1], l)], got_g = matmul_tn([hid], dyh, FF_TILE, D_MODEL, name="ffn_dw_out",
                                                comm=exchange(STAGE_G, dtype=BF16))
            exchanged(STAGE_G, got_g)
        else:
            g[(names[1], l)] = matmul_tn([hid], dyh, FF_TILE, D_MODEL, name="ffn_dw_out")
        return dx, dg.reshape(D_MODEL), got

    for l in reversed(range(DEPTH)):
        sv = saved[l]
        dx, g[("ffn2_norm", l)], _ = ffn_back(dx, sv["x2"], sv["gate2"], sv["up2"], w["ffn2_norm"][l], FFN2, l)
        dxb, dy_conv, do_sb, dot_sb, do_r = mix_out_bwd(dx, full[("mix_w_out", l)])
        g[("mix_w_out", l)] = matmul_tn([sv["ycat"]], dxb, D_MODEL, D_MODEL, name="mix_dw_out")
        cw = _pad_taps(full[("conv_w", None)][l])
        du_conv, dcw, dsm = conv_bwd(dy_conv, sv["ypre"], sv["u"], cw, _row(w["conv_ln_g"][l]), _row(w["conv_ln_b"][l]))
        g[("conv_w", l)] = dcw[:CONV_WIDTH]
        g[("conv_b", l)], g[("conv_ln_g", l)], g[("conv_ln_b", l)] = dsm[0], dsm[1], dsm[2]
        stage = STAGE_D if l == DEPTH - 1 else STAGE_E
        (dq_sb, dk_t, dv_t), got = sb_bwd(sv["q_sb"], sv["k_sb"], sv["v_sb"], do_sb, sv["qt_sb"], dot_sb, sv["tot"],
                                          exchange(stage))
        exchanged(stage, got)
        ng = w["ret_norm_g"][l].reshape(N_RET_HEADS, 1, HEAD_DIM)
        dq_r, dk_r, dv_r, dg_r, dng = ret_bwd(do_r, sv["q_r"], sv["k_r"], sv["v_r"], sv["g_r"], ng, sv["y_r"],
                                              sv["states"], tables)
        g[("ret_norm_g", l)] = dng.reshape(D_RET)
        dx, h, dproj, dg = mix_in_bwd(du_conv, dq_sb, dk_t, dv_t, dq_r, dk_r, dv_r, dg_r, cos, sin,
                                      full[("mix_w_in", l)], sv["x1"], _row(w["mix_norm"][l]), dx)
        g[("mix_norm", l)] = dg.reshape(D_MODEL)
        g[("mix_w_in", l)] = matmul_tn([dproj], h, D_MODEL, D_MODEL, name="mix_dw_in")
        dx, g[("ffn1_norm", l)], got = ffn_back(dx, sv["x0"], sv["gate1"], sv["up1"], w["ffn1_norm"][l], FFN1, l,
                                                exchange(STAGE_F) if l == 0 else None)
        exchanged(STAGE_F if l == 0 else [], got)
    grad_x = dx

    small_names = [n for n in SMALL if n != "final_norm"] + ["conv_w"]
    gs = {n: jnp.stack([g[(n, l)] for l in range(DEPTH)], axis=0) for n in small_names}
    gs["final_norm"] = g["final_norm"]
    small = _pack_small(gs)
    got = communicate(*exchange(STAGE_H, [jnp.broadcast_to(small[None], (N_DEV, SMALL_ROWS, D_MODEL))], dtype=BF16))
    exchanged(STAGE_H, got[:-1])

    grad, delta, new_m, new_v = {}, {}, {}, {}
    for n in COL_SHARDED + ROW_SHARDED:
        rows = w[n].shape[1]
        col = n in COL_SHARDED
        grad[n], delta[n], new_m[n], new_v[n] = adamw([received[(n, l)] for l in range(DEPTH)], w[n], m[n], v[n],
                                                      tr=128 if col else rows // 2, transposed=col)

    def small_pack(d):
        mine = dict(d)
        cwf = jnp.zeros((DEPTH, CONV_WIDTH, D_CONV), F32)
        mine["conv_w"] = lax.dynamic_update_slice(cwf, d["conv_w"], (0, 0, me * (D_CONV // N_DEV)))
        return _pack_small(mine)

    outs = adamw([got[-1]], small_pack(w)[None], small_pack(m)[None], small_pack(v)[None], tr=SMALL_ROWS)
    for dst, o in zip((grad, delta, new_m, new_v), outs):
        un = _unpack_small(o[0], w)
        un["conv_w"] = lax.dynamic_slice(un["conv_w"], (0, 0, me * (D_CONV // N_DEV)),
                                         (DEPTH, CONV_WIDTH, D_CONV // N_DEV))
        dst.update(un)

    return (loss, grad_x[None], *[grad[n] for n in WEIGHTS], *[delta[n] for n in WEIGHTS],
            *[new_m[n] for n in WEIGHTS], *[new_v[n] for n in WEIGHTS])
```

```python
import functools

import numpy as np
import jax
import jax.numpy as jnp
from jax import lax
from jax.experimental import pallas as pl
from jax.experimental.pallas import tpu as pltpu

F32 = jnp.float32
BF16 = jnp.bfloat16

D_MODEL = 1024
DEPTH = 2
D_FF = 2816
D_CONV = 256
CONV_WIDTH = 31
CONV_HALO = 32
D_SB = 512
N_SB_HEADS = 8
D_RET = 256
N_RET_HEADS = 4
HEAD_DIM = 64
D_IN_PROJ = 3072
ROPE_BASE = 10000.0
EPS = 1e-6
N_DEV = 8

ADAM_LR = 0.001
ADAM_B1 = 0.9
ADAM_B2 = 0.999
ADAM_EPS = 1e-08
ADAM_WD = 0.01
ADAM_STEP = 10

VMEM_LIMIT = 56 * 1024 * 1024
ROW_TILE = 512
FF_TILE = 1408
FF_FWD_CHUNK = 256
FF_BWD_CHUNK = 2816
SB_TILE = 256
SB_ROWS = 512
SB_FWD_ROWS = 1024
RET_TILE = 512
CONV_TILE = 256

NT_DIMS = (((1,), (1,)), ((), ()))
TN_DIMS = (((0,), (0,)), ((), ()))


def _params(n_axes, vmem=VMEM_LIMIT):
    return pltpu.CompilerParams(dimension_semantics=("arbitrary",) * n_axes, vmem_limit_bytes=vmem)


def _dot(a, b):
    return jnp.dot(a, b, preferred_element_type=F32)


def _dot_nt(a, b):
    return lax.dot_general(a, b, NT_DIMS, preferred_element_type=F32)


def _dot_tn(a, b):
    return lax.dot_general(a, b, TN_DIMS, preferred_element_type=F32)


def _sigmoid(z):
    return 1.0 / (1.0 + jnp.exp(-z))


def _rms_stats(xv):
    r = lax.rsqrt(jnp.mean(xv * xv, axis=-1, keepdims=True) + EPS)
    return r, xv * r


def _rms_bwd(xv, g, dh):
    r, xhat = _rms_stats(xv)
    dxhat = dh * g
    dx = r * (dxhat - xhat * jnp.mean(dxhat * xhat, axis=-1, keepdims=True))
    dg = jnp.sum(dh * xhat, axis=0, keepdims=True)
    return dx, (xhat * g).astype(BF16), dg


def ffn_fwd(x, g, w_in, w_out, comm=None, tm=ROW_TILE):
    S = x.shape[0]
    chunk = FF_FWD_CHUNK

    def body(x_ref, g_ref, w_ref, wo_ref, y_ref, gate_ref, up_ref):
        xv = x_ref[...]
        _, xhat = _rms_stats(xv)
        h = (xhat * g_ref[...]).astype(BF16)
        acc = None
        for j in range(D_FF // chunk):
            cols = pl.ds(j * chunk, chunk)
            gt = _dot_nt(h, w_ref[cols, :])
            up = _dot_nt(h, w_ref[pl.ds(D_FF + j * chunk, chunk), :])
            gate_ref[:, cols] = gt.astype(BF16)
            up_ref[:, cols] = up.astype(BF16)
            part = _dot((gt * _sigmoid(gt) * up).astype(BF16), wo_ref[cols, :])
            acc = part if acc is None else acc + part
        y_ref[...] = xv + 0.5 * acc

    row = lambda i: (i, 0)
    one = lambda i: (0, 0)
    resident = pl.Buffered(1)
    return _call(
        body, (x, g, w_in, w_out), comm, name="ffn_fwd",
        grid=(S // tm,),
        in_specs=[
            pl.BlockSpec((tm, D_MODEL), row),
            pl.BlockSpec((1, D_MODEL), one),
            pl.BlockSpec((2 * D_FF, D_MODEL), one, pipeline_mode=resident),
            pl.BlockSpec((D_FF, D_MODEL), one, pipeline_mode=resident),
        ],
        out_specs=[
            pl.BlockSpec((tm, D_MODEL), row),
            pl.BlockSpec((tm, D_FF), row),
            pl.BlockSpec((tm, D_FF), row),
        ],
        out_shape=[
            jax.ShapeDtypeStruct((S, D_MODEL), F32),
            jax.ShapeDtypeStruct((S, D_FF), BF16),
            jax.ShapeDtypeStruct((S, D_FF), BF16),
        ],
        scratch_shapes=[],
    )


def ffn_bwd(dy, x, g, gate, up, w_in, w_out, comm=None, tm=ROW_TILE // 2):
    S = x.shape[0]
    chunk = FF_BWD_CHUNK

    def body(dy_ref, x_ref, g_ref, gate_ref, up_ref, w_ref, wo_ref,
             dx_ref, h_ref, dyh_ref, dgate_ref, dup_ref, hid_ref, dg_ref):
        i = pl.program_id(0)
        d2 = (0.5 * dy_ref[...]).astype(BF16)
        dyh_ref[...] = d2
        dh = None
        for j in range(D_FF // chunk):
            cols = pl.ds(j * chunk, chunk)
            dhid = _dot_nt(d2, wo_ref[cols, :])
            gt = gate_ref[:, cols].astype(F32)
            u = up_ref[:, cols].astype(F32)
            sig = _sigmoid(gt)
            sl = gt * sig
            dgate = (dhid * u * (sig * (1.0 + gt * (1.0 - sig)))).astype(BF16)
            dup = (dhid * sl).astype(BF16)
            dgate_ref[:, cols] = dgate
            dup_ref[:, cols] = dup
            hid_ref[:, cols] = (sl * u).astype(BF16)
            part = _dot(dgate, w_ref[cols, :]) + _dot(dup, w_ref[pl.ds(D_FF + j * chunk, chunk), :])
            dh = part if dh is None else dh + part
        dx, h, dg = _rms_bwd(x_ref[...], g_ref[...], dh)
        dx_ref[...] = dy_ref[...] + dx
        h_ref[...] = h

        @pl.when(i == 0)
        def _():
            dg_ref[...] = dg

        @pl.when(i > 0)
        def _():
            dg_ref[...] += dg

    row = lambda i: (i, 0)
    one = lambda i: (0, 0)
    resident = pl.Buffered(1)
    return _call(
        body, (dy, x, g, gate, up, w_in, w_out), comm, name="ffn_bwd",
        grid=(S // tm,),
        in_specs=[
            pl.BlockSpec((tm, D_MODEL), row),
            pl.BlockSpec((tm, D_MODEL), row),
            pl.BlockSpec((1, D_MODEL), one),
            pl.BlockSpec((tm, D_FF), row),
            pl.BlockSpec((tm, D_FF), row),
            pl.BlockSpec((2 * D_FF, D_MODEL), one, pipeline_mode=resident),
            pl.BlockSpec((D_FF, D_MODEL), one, pipeline_mode=resident),
        ],
        out_specs=[
            pl.BlockSpec((tm, D_MODEL), row),
            pl.BlockSpec((tm, D_MODEL), row),
            pl.BlockSpec((tm, D_MODEL), row),
            pl.BlockSpec((tm, D_FF), row),
            pl.BlockSpec((tm, D_FF), row),
            pl.BlockSpec((tm, D_FF), row),
            pl.BlockSpec((1, D_MODEL), one),
        ],
        out_shape=[
            jax.ShapeDtypeStruct((S, D_MODEL), F32),
            jax.ShapeDtypeStruct((S, D_MODEL), BF16),
            jax.ShapeDtypeStruct((S, D_MODEL), BF16),
            jax.ShapeDtypeStruct((S, D_FF), BF16),
            jax.ShapeDtypeStruct((S, D_FF), BF16),
            jax.ShapeDtypeStruct((S, D_FF), BF16),
            jax.ShapeDtypeStruct((1, D_MODEL), F32),
        ],
        scratch_shapes=[],
    )


def matmul_tn(a_list, b, ta, tn, tk=4 * ROW_TILE, name="matmul_tn", comm=None):
    S, ka = a_list[0].shape
    nb = b.shape[1]
    assert S % tk == 0 and ka % ta == 0 and nb % tn == 0
    per = ka // ta

    def body(*refs):
        a_refs, b_ref, o_ref = refs[:-2], refs[-2], refs[-1]
        i = pl.program_id(0)
        k = pl.program_id(2)

        @pl.when(k == 0)
        def _():
            o_ref[...] = jnp.zeros_like(o_ref)

        for t, a_ref in enumerate(a_refs):
            @pl.when(lax.div(i, per) == t)
            def _(a_ref=a_ref):
                o_ref[...] += _dot_tn(a_ref[...], b_ref[...])

    def a_spec(t):
        def index(i, j, k):
            mine = lax.div(i, per) == t
            return jnp.where(mine, k, 0), jnp.where(mine, i - t * per, 0)
        return pl.BlockSpec((tk, ta), index)

    (out,), got = _call(
        body, (*a_list, b), comm, name=name,
        grid=(per * len(a_list), nb // tn, S // tk),
        in_specs=[a_spec(t) for t in range(len(a_list))] + [pl.BlockSpec((tk, tn), lambda i, j, k: (k, j))],
        out_specs=[pl.BlockSpec((ta, tn), lambda i, j, k: (i, j))],
        out_shape=[jax.ShapeDtypeStruct((ka * len(a_list), nb), F32)],
        scratch_shapes=[],
    )
    return (out, got) if comm is not None else out


SB_COLS = (2 * D_CONV, 2 * D_CONV + D_SB, 2 * D_CONV + 2 * D_SB)
RET_COLS = tuple(2 * D_CONV + 3 * D_SB + j * D_RET for j in range(4))


def _swap_halves(x):
    n = x.shape[1]
    lane = lax.broadcasted_iota(jnp.int32, x.shape, 1)
    first = (lane % HEAD_DIM) < (HEAD_DIM // 2)
    return jnp.where(first, pltpu.roll(x, n - HEAD_DIM // 2, 1), pltpu.roll(x, HEAD_DIM // 2, 1))


def _head(x, h):
    return x[:, h * HEAD_DIM:(h + 1) * HEAD_DIM]


def _heads_spec(n_heads, tm):
    return pl.BlockSpec((n_heads, tm, HEAD_DIM), lambda i: (0, i, 0))


def mix_in_fwd(x, g, w, cos, sin, tm=ROW_TILE):
    S = x.shape[0]

    def body(x_ref, g_ref, w_ref, c_ref, s_ref, u_ref, q_ref, k_ref, v_ref, qt_ref, qr_ref, kr_ref, vr_ref, gr_ref):
        _, xhat = _rms_stats(x_ref[...])
        proj = _dot_nt((xhat * g_ref[...]).astype(BF16), w_ref[...])
        u_ref[...] = proj[:, :2 * D_CONV]
        for h in range(N_SB_HEADS):
            q = (_head(proj[:, SB_COLS[0]:SB_COLS[1]], h) * 0.125).astype(BF16)
            q_ref[h] = q
            qt_ref[h] = q.T
            k_ref[h] = _head(proj[:, SB_COLS[1]:SB_COLS[2]], h).astype(BF16)
            v_ref[h] = _head(proj[:, SB_COLS[2]:RET_COLS[0]], h).astype(BF16)
        c = c_ref[...]
        s = s_ref[...]
        qv = proj[:, RET_COLS[0]:RET_COLS[1]]
        kv = proj[:, RET_COLS[1]:RET_COLS[2]]
        q_rot = ((qv * c + _swap_halves(qv) * s) * 0.125).astype(BF16)
        k_rot = (kv * c + _swap_halves(kv) * s).astype(BF16)
        for h in range(N_RET_HEADS):
            qr_ref[h] = _head(q_rot, h)
            kr_ref[h] = _head(k_rot, h)
            vr_ref[h] = _head(proj[:, RET_COLS[2]:RET_COLS[3]], h).astype(BF16)
            gr_ref[h] = _head(proj[:, RET_COLS[3]:], h)

    row = lambda i: (i, 0)
    one = lambda i: (0, 0)
    sb = jax.ShapeDtypeStruct((N_SB_HEADS, S, HEAD_DIM), BF16)
    ret = jax.ShapeDtypeStruct((N_RET_HEADS, S, HEAD_DIM), BF16)
    return pl.pallas_call(
        body, name="mix_in_fwd",
        grid=(S // tm,),
        in_specs=[
            pl.BlockSpec((tm, D_MODEL), row),
            pl.BlockSpec((1, D_MODEL), one),
            pl.BlockSpec((D_IN_PROJ, D_MODEL), one, pipeline_mode=pl.Buffered(1)),
            pl.BlockSpec((tm, D_RET), row),
            pl.BlockSpec((tm, D_RET), row),
        ],
        out_specs=[
            pl.BlockSpec((tm, 2 * D_CONV), row),
            _heads_spec(N_SB_HEADS, tm), _heads_spec(N_SB_HEADS, tm), _heads_spec(N_SB_HEADS, tm),
            pl.BlockSpec((N_SB_HEADS, HEAD_DIM, tm), lambda i: (0, 0, i)),
            _heads_spec(N_RET_HEADS, tm), _heads_spec(N_RET_HEADS, tm), _heads_spec(N_RET_HEADS, tm),
            _heads_spec(N_RET_HEADS, tm),
        ],
        out_shape=[
            jax.ShapeDtypeStruct((S, 2 * D_CONV), F32), sb, sb, sb,
            jax.ShapeDtypeStruct((N_SB_HEADS, HEAD_DIM, S), BF16),
            ret, ret, ret, jax.ShapeDtypeStruct((N_RET_HEADS, S, HEAD_DIM), F32),
        ],
        compiler_params=_params(1),
    )(x, g, w, cos, sin)


def mix_in_bwd(du, dq, dkt, dvt, dqr, dkr, dvr, dgr, cos, sin, w, x, g, dy, tm=SB_TILE):
    S = x.shape[0]
    assert dkt.shape[-1] == tm

    def body(du_ref, dq_ref, dkt_ref, dvt_ref, dqr_ref, dkr_ref, dvr_ref, dgr_ref, c_ref, s_ref, w_ref, x_ref, g_ref,
             dy_ref, dx_ref, h_ref, dp_ref, dg_ref):
        i = pl.program_id(0)
        sb_heads = range(N_SB_HEADS)
        ret_heads = range(N_RET_HEADS)
        c = c_ref[...]
        s = s_ref[...]
        dq_rot = jnp.concatenate([dqr_ref[h] for h in ret_heads], axis=1) * 0.125
        dk_rot = jnp.concatenate([dkr_ref[h] for h in ret_heads], axis=1)
        dproj = jnp.concatenate([
            du_ref[...].astype(BF16),
            jnp.concatenate([dq_ref[h] * 0.125 for h in sb_heads], axis=1).astype(BF16),
            jnp.concatenate([dkt_ref[h, 0].T for h in sb_heads], axis=1).astype(BF16),
            jnp.concatenate([dvt_ref[h, 0].T for h in sb_heads], axis=1).astype(BF16),
            (dq_rot * c - _swap_halves(dq_rot) * s).astype(BF16),
            (dk_rot * c - _swap_halves(dk_rot) * s).astype(BF16),
            jnp.concatenate([dvr_ref[h] for h in ret_heads], axis=1).astype(BF16),
            jnp.concatenate([dgr_ref[h] for h in ret_heads], axis=1).astype(BF16)], axis=1)
        dp_ref[...] = dproj
        dh = _dot(dproj, w_ref[...])
        dx, h, dg = _rms_bwd(x_ref[...], g_ref[...], dh)
        dx_ref[...] = dy_ref[...] + dx
        h_ref[...] = h

        @pl.when(i == 0)
        def _():
            dg_ref[...] = dg

        @pl.when(i > 0)
        def _():
            dg_ref[...] += dg

    row = lambda i: (i, 0)
    one = lambda i: (0, 0)
    tiles = pl.BlockSpec((N_SB_HEADS, 1, HEAD_DIM, tm), lambda i: (0, i, 0, 0))
    return pl.pallas_call(
        body, name="mix_in_bwd",
        grid=(S // tm,),
        in_specs=[
            pl.BlockSpec((tm, 2 * D_CONV), row),
            _heads_spec(N_SB_HEADS, tm), tiles, tiles,
            _heads_spec(N_RET_HEADS, tm), _heads_spec(N_RET_HEADS, tm), _heads_spec(N_RET_HEADS, tm),
            _heads_spec(N_RET_HEADS, tm),
            pl.BlockSpec((tm, D_RET), row),
            pl.BlockSpec((tm, D_RET), row),
            pl.BlockSpec((D_IN_PROJ, D_MODEL), one, pipeline_mode=pl.Buffered(1)),
            pl.BlockSpec((tm, D_MODEL), row),
            pl.BlockSpec((1, D_MODEL), one),
            pl.BlockSpec((tm, D_MODEL), row),
        ],
        out_specs=[
            pl.BlockSpec((tm, D_MODEL), row),
            pl.BlockSpec((tm, D_MODEL), row),
            pl.BlockSpec((tm, D_IN_PROJ), row),
            pl.BlockSpec((1, D_MODEL), one),
        ],
        out_shape=[
            jax.ShapeDtypeStruct((S, D_MODEL), F32),
            jax.ShapeDtypeStruct((S, D_MODEL), BF16),
            jax.ShapeDtypeStruct((S, D_IN_PROJ), BF16),
            jax.ShapeDtypeStruct((1, D_MODEL), F32),
        ],
        compiler_params=_params(1),
    )(du, dq, dkt, dvt, dqr, dkr, dvr, dgr, cos, sin, w, x, g, dy)


def mix_out_fwd(y_conv, o_sb, o_ret, w, x, tm=ROW_TILE):
    S = x.shape[0]

    def body(yc_ref, sb_ref, rt_ref, w_ref, x_ref, o_ref, ycat_ref):
        ycat = jnp.concatenate(
            [yc_ref[...]] + [sb_ref[h].astype(BF16) for h in range(N_SB_HEADS)]
            + [rt_ref[h].astype(BF16) for h in range(N_RET_HEADS)], axis=1)
        ycat_ref[...] = ycat
        o_ref[...] = x_ref[...] + _dot(ycat, w_ref[...])

    row = lambda i: (i, 0)
    return pl.pallas_call(
        body, name="mix_out_fwd",
        grid=(S // tm,),
        in_specs=[
            pl.BlockSpec((tm, D_CONV), row),
            _heads_spec(N_SB_HEADS, tm),
            _heads_spec(N_RET_HEADS, tm),
            pl.BlockSpec((D_MODEL, D_MODEL), lambda i: (0, 0)),
            pl.BlockSpec((tm, D_MODEL), row),
        ],
        out_specs=[pl.BlockSpec((tm, D_MODEL), row), pl.BlockSpec((tm, D_MODEL), row)],
        out_shape=[jax.ShapeDtypeStruct((S, D_MODEL), F32), jax.ShapeDtypeStruct((S, D_MODEL), BF16)],
        compiler_params=_params(1),
    )(y_conv, o_sb, o_ret, w, x)


def mix_out_bwd(dy, w, tm=ROW_TILE):
    S = dy.shape[0]

    def body(dy_ref, w_ref, dyb_ref, dc_ref, do_ref, dot_ref, dr_ref):
        d = dy_ref[...].astype(BF16)
        dyb_ref[...] = d
        dycat = _dot_nt(d, w_ref[...])
        dc_ref[...] = dycat[:, :D_CONV]
        for h in range(N_SB_HEADS):
            do = _head(dycat[:, D_CONV:D_CONV + D_SB], h).astype(BF16)
            do_ref[h] = do
            dot_ref[h] = do.T
        for h in range(N_RET_HEADS):
            dr_ref[h] = _head(dycat[:, D_CONV + D_SB:], h)

    row = lambda i: (i, 0)
    return pl.pallas_call(
        body, name="mix_out_bwd",
        grid=(S // tm,),
        in_specs=[
            pl.BlockSpec((tm, D_MODEL), row),
            pl.BlockSpec((D_MODEL, D_MODEL), lambda i: (0, 0)),
        ],
        out_specs=[
            pl.BlockSpec((tm, D_MODEL), row),
            pl.BlockSpec((tm, D_CONV), row),
            _heads_spec(N_SB_HEADS, tm),
            pl.BlockSpec((N_SB_HEADS, HEAD_DIM, tm), lambda i: (0, 0, i)),
            _heads_spec(N_RET_HEADS, tm),
        ],
        out_shape=[
            jax.ShapeDtypeStruct((S, D_MODEL), BF16),
            jax.ShapeDtypeStruct((S, D_CONV), F32),
            jax.ShapeDtypeStruct((N_SB_HEADS, S, HEAD_DIM), BF16),
            jax.ShapeDtypeStruct((N_SB_HEADS, HEAD_DIM, S), BF16),
            jax.ShapeDtypeStruct((N_RET_HEADS, S, HEAD_DIM), F32),
        ],
        compiler_params=_params(1),
    )(dy, w)


def _rows_from(x, start, n):
    return pltpu.roll(x, (x.shape[0] - start) % x.shape[0], 0)[:n]


def _conv_ln(ypre, ln_g, ln_b):
    mu = jnp.mean(ypre, axis=-1, keepdims=True)
    yc = ypre - mu
    rstd = lax.rsqrt(jnp.mean(yc * yc, axis=-1, keepdims=True) + EPS)
    yn = yc * rstd
    return yn, rstd, yn * ln_g + ln_b


def conv_fwd(proj, cw, cb, ln_g, ln_b, tm=CONV_TILE):
    S = proj.shape[0]
    hb = tm // CONV_HALO

    def body(a_ref, b_ref, ap_ref, bp_ref, cw_ref, cb_ref, g_ref, bb_ref, y_ref, ypre_ref, v_sc):
        i = pl.program_id(0)
        prev = ap_ref[...] * _sigmoid(bp_ref[...])
        v_sc[pl.ds(0, CONV_HALO), :] = jnp.where(i > 0, prev, 0.0)
        v_sc[pl.ds(CONV_HALO, tm), :] = a_ref[...] * _sigmoid(b_ref[...])
        vext = v_sc[...]
        acc = jnp.zeros((tm, D_CONV), F32)
        for j in range(CONV_WIDTH):
            acc = acc + cw_ref[pl.ds(j, 1), :] * _rows_from(vext, CONV_HALO - (CONV_WIDTH - 1) + j, tm)
        ypre = acc + cb_ref[...]
        ypre_ref[...] = ypre
        _, _, z = _conv_ln(ypre, g_ref[...], bb_ref[...])
        y_ref[...] = (z * _sigmoid(z)).astype(BF16)

    one = lambda i: (0, 0)
    return pl.pallas_call(
        body, name="conv_fwd",
        grid=(S // tm,),
        in_specs=[
            pl.BlockSpec((tm, D_CONV), lambda i: (i, 0)),
            pl.BlockSpec((tm, D_CONV), lambda i: (i, 1)),
            pl.BlockSpec((CONV_HALO, D_CONV), lambda i: (jnp.maximum(i * hb - 1, 0), 0)),
            pl.BlockSpec((CONV_HALO, D_CONV), lambda i: (jnp.maximum(i * hb - 1, 0), 1)),
            pl.BlockSpec((CONV_HALO, D_CONV), one),
            pl.BlockSpec((1, D_CONV), one),
            pl.BlockSpec((1, D_CONV), one),
            pl.BlockSpec((1, D_CONV), one),
        ],
        out_specs=[pl.BlockSpec((tm, D_CONV), lambda i: (i, 0)), pl.BlockSpec((tm, D_CONV), lambda i: (i, 0))],
        out_shape=[jax.ShapeDtypeStruct((S, D_CONV), BF16), jax.ShapeDtypeStruct((S, D_CONV), F32)],
        scratch_shapes=[pltpu.VMEM((tm + CONV_HALO, D_CONV), F32)],
        compiler_params=_params(1),
    )(proj, proj, proj, proj, cw, cb, ln_g, ln_b)


def conv_bwd(dyc, ypre, proj, cw, ln_g, ln_b, tm=CONV_TILE):
    S = ypre.shape[0]
    hb = tm // CONV_HALO
    nblk = S // tm
    last_halo = S // CONV_HALO - 1

    def dpre(dy, yp, g, bb):
        yn, rstd, z = _conv_ln(yp, g, bb)
        sg = _sigmoid(z)
        dz = dy * (sg * (1.0 + z * (1.0 - sg)))
        dyn = dz * g
        d = rstd * (dyn - jnp.mean(dyn, axis=-1, keepdims=True) - yn * jnp.mean(dyn * yn, axis=-1, keepdims=True))
        return d, dz * yn, dz

    def body(dy_ref, yp_ref, dyn_ref, ypn_ref, a_ref, b_ref, ap_ref, bp_ref, cw_ref, g_ref, bb_ref,
             du_ref, dcw_ref, dsm_ref, d_sc, v_sc):
        i = pl.program_id(0)
        g = g_ref[...]
        bb = bb_ref[...]
        d_main, dgn, dz = dpre(dy_ref[...], yp_ref[...], g, bb)
        d_next, _, _ = dpre(dyn_ref[...], ypn_ref[...], g, bb)
        d_sc[pl.ds(0, tm), :] = d_main
        d_sc[pl.ds(tm, CONV_HALO), :] = jnp.where(i < nblk - 1, d_next, 0.0)
        a = a_ref[...]
        sb = _sigmoid(b_ref[...])
        prev = ap_ref[...] * _sigmoid(bp_ref[...])
        v_sc[pl.ds(0, CONV_HALO), :] = jnp.where(i > 0, prev, 0.0)
        v_sc[pl.ds(CONV_HALO, tm), :] = a * sb

        @pl.when(i == 0)
        def _():
            dcw_ref[...] = jnp.zeros_like(dcw_ref)
            dsm_ref[...] = jnp.zeros_like(dsm_ref)

        dext = d_sc[...]
        vext = v_sc[...]
        dv = jnp.zeros((tm, D_CONV), F32)
        for j in range(CONV_WIDTH):
            dv = dv + cw_ref[pl.ds(j, 1), :] * _rows_from(dext, CONV_WIDTH - 1 - j, tm)
            shifted = _rows_from(vext, CONV_HALO - (CONV_WIDTH - 1) + j, tm)
            dcw_ref[pl.ds(j, 1), :] += jnp.sum(d_main * shifted, axis=0, keepdims=True)
        du_ref[:, pl.ds(0, D_CONV)] = dv * sb
        du_ref[:, pl.ds(D_CONV, D_CONV)] = dv * a * sb * (1.0 - sb)
        dsm_ref[pl.ds(0, 1), :] += jnp.sum(d_main, axis=0, keepdims=True)
        dsm_ref[pl.ds(1, 1), :] += jnp.sum(dgn, axis=0, keepdims=True)
        dsm_ref[pl.ds(2, 1), :] += jnp.sum(dz, axis=0, keepdims=True)

    one = lambda i: (0, 0)
    prev_map = lambda c: (lambda i: (jnp.maximum(i * hb - 1, 0), c))
    next_map = lambda i: (jnp.minimum((i + 1) * hb, last_halo), 0)
    return pl.pallas_call(
        body, name="conv_bwd",
        grid=(nblk,),
        in_specs=[
            pl.BlockSpec((tm, D_CONV), lambda i: (i, 0)),
            pl.BlockSpec((tm, D_CONV), lambda i: (i, 0)),
            pl.BlockSpec((CONV_HALO, D_CONV), next_map),
            pl.BlockSpec((CONV_HALO, D_CONV), next_map),
            pl.BlockSpec((tm, D_CONV), lambda i: (i, 0)),
            pl.BlockSpec((tm, D_CONV), lambda i: (i, 1)),
            pl.BlockSpec((CONV_HALO, D_CONV), prev_map(0)),
            pl.BlockSpec((CONV_HALO, D_CONV), prev_map(1)),
            pl.BlockSpec((CONV_HALO, D_CONV), one),
            pl.BlockSpec((1, D_CONV), one),
            pl.BlockSpec((1, D_CONV), one),
        ],
        out_specs=[
            pl.BlockSpec((tm, 2 * D_CONV), lambda i: (i, 0)),
            pl.BlockSpec((CONV_HALO, D_CONV), one),
            pl.BlockSpec((8, D_CONV), one),
        ],
        out_shape=[
            jax.ShapeDtypeStruct((S, 2 * D_CONV), F32),
            jax.ShapeDtypeStruct((CONV_HALO, D_CONV), F32),
            jax.ShapeDtypeStruct((8, D_CONV), F32),
        ],
        scratch_shapes=[pltpu.VMEM((tm + CONV_HALO, D_CONV), F32), pltpu.VMEM((tm + CONV_HALO, D_CONV), F32)],
        compiler_params=_params(1),
    )(dyc, ypre, dyc, ypre, proj, proj, proj, proj, cw, ln_g, ln_b)


SB_GROUP = 8


def _softplus(z):
    neg_abs = lax.bitcast_convert_type(lax.bitcast_convert_type(z, jnp.uint32) | jnp.uint32(0x80000000), F32)
    return jnp.maximum(z, 0.0) + jnp.log(1.0 + jnp.exp(neg_abs))


def _full_groups(n, body):
    def step(t, c):
        body(t * SB_GROUP)
        return c

    lax.fori_loop(0, lax.div(n, SB_GROUP), step, 0)


def _last_group(n, step, body):
    r = lax.rem(n, SB_GROUP)
    for k in range(0, SB_GROUP, step):
        @pl.when(r == k)
        def _(k=k):
            body(k)


def _rows(xs):
    return xs[0] if len(xs) == 1 else jnp.concatenate(xs, axis=0)


def sb_fwd(q, k, v, comm=None, T=SB_TILE, Q=SB_FWD_ROWS):
    H, S, dh = q.shape
    M = Q // T

    def body(q_ref, k_ref, v_ref, o_ref, tot_ref, acc_sc, car_sc):
        qb = pl.program_id(1)
        qv = q_ref[...]
        row = lax.broadcasted_iota(jnp.int32, (T, T), 0)
        col = lax.broadcasted_iota(jnp.int32, (T, T), 1)
        tri = jnp.where(row >= col, 1.0, 0.0).astype(BF16)
        qrow = lax.broadcasted_iota(jnp.int32, (Q, T), 0)
        kcol = lax.broadcasted_iota(jnp.int32, (Q, T), 1)
        causal = {d + 1: kcol + d * T < qrow for d in range(M)}
        acc_sc[...] = jnp.zeros_like(acc_sc)
        car_sc[...] = jnp.zeros_like(car_sc)

        def logits(kb, masked):
            ks = k_ref[pl.ds(pl.multiple_of(kb * T, T), T), :]
            z = _dot_nt(qv, ks)
            nb = _softplus(z)
            if masked:
                nb = jnp.where(causal[masked], nb, 0.0)
            return z, nb.astype(BF16)

        def group(kbs, diag):
            parts = [logits(kb, d) for kb, d in zip(kbs, diag)]
            pall = _dot(_rows([nb for _, nb in parts]), tri)
            carry = car_sc[...]
            out = None
            for j, kb in enumerate(kbs):
                p = pall[j * Q:(j + 1) * Q]
                vs = v_ref[pl.ds(pl.multiple_of(kb * T, T), T), :]
                w = jnp.exp((parts[j][0] - carry) - p)
                if diag[j]:
                    w = jnp.where(causal[diag[j]], w, 0.0)
                o = _dot(w.astype(BF16), vs)
                out = o if out is None else out + o
                carry = carry + p[:, 0:1]
            acc_sc[...] += out
            car_sc[...] = carry

        full = M * qb
        _last_group(full, M, lambda r: group([full + d for d in reversed(range(M))] + [full - 1 - o for o in range(r)],
                                             [d + 1 for d in reversed(range(M))] + [0] * r))
        rest = full - lax.rem(full, SB_GROUP)
        _full_groups(rest, lambda o: group([rest - 1 - o - j for j in range(SB_GROUP)], [0] * SB_GROUP))
        o_ref[...] = acc_sc[...]
        tot_ref[...] = car_sc[...]

    return _call(
        body, (q, k, v), comm, name="sb_fwd",
        grid=(H, S // Q),
        in_specs=[
            pl.BlockSpec((None, Q, dh), lambda h, i: (h, i, 0)),
            pl.BlockSpec((None, S, dh), lambda h, i: (h, 0, 0)),
            pl.BlockSpec((None, S, dh), lambda h, i: (h, 0, 0)),
        ],
        out_specs=[
            pl.BlockSpec((None, Q, dh), lambda h, i: (h, i, 0)),
            pl.BlockSpec((None, Q, 1), lambda h, i: (h, i, 0)),
        ],
        out_shape=[jax.ShapeDtypeStruct((H, S, dh), F32), jax.ShapeDtypeStruct((H, S, 1), F32)],
        scratch_shapes=[pltpu.VMEM((Q, dh), F32), pltpu.VMEM((Q, 1), F32)],
    )


def sb_bwd(q, k, v, do, qt, dot, tot, comm=None, T=SB_TILE, Q=SB_ROWS):
    H, S, dh = q.shape
    nt = S // T
    M = Q // T

    def body(q_ref, k_ref, v_ref, do_ref, qt_ref, dot_ref, tot_ref, dq_ref, dk_ref, dv_ref, acc_sc, rc_sc, gc_sc):
        qb = pl.program_id(1)
        qv = q_ref[...]
        dov = do_ref[...]
        qtv = qt_ref[...]
        dotv = dot_ref[...]
        row = lax.broadcasted_iota(jnp.int32, (T, T), 0)
        col = lax.broadcasted_iota(jnp.int32, (T, T), 1)
        before = jnp.where(row < col, 1.0, 0.0).astype(BF16)
        qrow = lax.broadcasted_iota(jnp.int32, (Q, T), 0)
        kcol = lax.broadcasted_iota(jnp.int32, (Q, T), 1)
        causal = {d + 1: kcol + d * T < qrow for d in range(M)}
        acc_sc[...] = jnp.zeros_like(acc_sc)
        rc_sc[...] = tot_ref[...]
        gc_sc[...] = jnp.zeros_like(gc_sc)

        @pl.when(qb == 0)
        def _():
            dk_ref[...] = jnp.zeros_like(dk_ref)
            dv_ref[...] = jnp.zeros_like(dv_ref)

        def first(kb, masked):
            start = pl.multiple_of(kb * T, T)
            z = _dot_nt(qv, k_ref[pl.ds(start, T), :])
            nb = _softplus(z)
            sig = jnp.exp(z - nb)
            if masked:
                nb = jnp.where(causal[masked], nb, 0.0)
            dw = _dot_nt(dov, v_ref[pl.ds(start, T), :])
            return z, sig, nb.astype(BF16), dw

        def group(kbs, diag):
            parts = [first(kb, d) for kb, d in zip(kbs, diag)]
            pall = _dot(_rows([p[2] for p in parts]), before)
            rc = rc_sc[...]
            ws, gs, ghs = [], [], []
            for j in range(len(kbs)):
                z, _, nbh, dw = parts[j]
                p = pall[j * Q:(j + 1) * Q]
                w = jnp.exp((z - rc) + p)
                rc = rc - (p[:, T - 1:T] + nbh[:, T - 1:T].astype(F32))
                if diag[j]:
                    w = jnp.where(causal[diag[j]], w, 0.0)
                g = dw * w
                ws.append(w.astype(BF16))
                gs.append(g)
                ghs.append(g.astype(BF16))
            glall = _dot(_rows(ghs), before)
            gc = gc_sc[...]
            dq = None
            for j, kb in enumerate(kbs):
                ks = k_ref[pl.ds(pl.multiple_of(kb * T, T), T), :]
                gl = glall[j * Q:(j + 1) * Q]
                dz = gs[j] - parts[j][1] * (gs[j] + (gl + gc))
                gc = gc + gl[:, T - 1:T] + ghs[j][:, T - 1:T].astype(F32)
                if diag[j]:
                    dz = jnp.where(causal[diag[j]], dz, 0.0)
                dzb = dz.astype(BF16)
                d = _dot(dzb, ks)
                dq = d if dq is None else dq + d
                dk_ref[kb] += _dot(qtv, dzb)
                dv_ref[kb] += _dot(dotv, ws[j])
            acc_sc[...] += dq
            rc_sc[...] = rc
            gc_sc[...] = gc

        full = M * qb
        _full_groups(full, lambda o: group([o + j for j in range(SB_GROUP)], [0] * SB_GROUP))
        rest = full - lax.rem(full, SB_GROUP)
        _last_group(full, M, lambda r: group([rest + j for j in range(r)] + [full + d for d in range(M)],
                                             [0] * r + [d + 1 for d in range(M)]))
        dq_ref[...] = acc_sc[...]

    return _call(
        body, (q, k, v, do, qt, dot, tot), comm, name="sb_bwd",
        grid=(H, S // Q),
        in_specs=[
            pl.BlockSpec((None, Q, dh), lambda h, i: (h, i, 0)),
            pl.BlockSpec((None, S, dh), lambda h, i: (h, 0, 0)),
            pl.BlockSpec((None, S, dh), lambda h, i: (h, 0, 0)),
            pl.BlockSpec((None, Q, dh), lambda h, i: (h, i, 0)),
            pl.BlockSpec((None, dh, Q), lambda h, i: (h, 0, i)),
            pl.BlockSpec((None, dh, Q), lambda h, i: (h, 0, i)),
            pl.BlockSpec((None, Q, 1), lambda h, i: (h, i, 0)),
        ],
        out_specs=[
            pl.BlockSpec((None, Q, dh), lambda h, i: (h, i, 0)),
            pl.BlockSpec((None, nt, dh, T), lambda h, i: (h, 0, 0, 0)),
            pl.BlockSpec((None, nt, dh, T), lambda h, i: (h, 0, 0, 0)),
        ],
        out_shape=[jax.ShapeDtypeStruct((H, S, dh), F32), jax.ShapeDtypeStruct((H, nt, dh, T), F32),
                   jax.ShapeDtypeStruct((H, nt, dh, T), F32)],
        scratch_shapes=[pltpu.VMEM((Q, dh), F32), pltpu.VMEM((Q, 1), F32), pltpu.VMEM((Q, 1), F32)],
    )


def _ret_tables(T=RET_TILE):
    hh = jnp.arange(N_RET_HEADS, dtype=F32)
    log_gamma = jnp.log1p(-jnp.exp2(-5.0 - hh))
    idx = jnp.arange(T, dtype=F32)
    diff = idx[:, None] - idx[None, :]
    ci = (jnp.arange(T) // 64)
    same = ci[:, None] == ci[None, :]
    earlier = ci[None, :] < ci[:, None]
    dist = jnp.where(same, jnp.abs(diff), diff)
    dmat = jnp.where(same | earlier, jnp.exp(log_gamma[:, None, None] * dist[None]), 0.0)
    ones = jnp.ones((1, 1, HEAD_DIM), F32)
    qdec = jnp.exp(log_gamma[:, None] * (idx + 1.0)[None, :])[:, :, None] * ones
    kdec = jnp.exp(log_gamma[:, None] * (T - 1.0 - idx)[None, :])[:, :, None] * ones
    bdec = jnp.exp(log_gamma * T)[:, None, None] * jnp.ones((1, HEAD_DIM, HEAD_DIM), F32)
    return dmat, qdec, kdec, bdec


def _rope_tables(S):
    half = HEAD_DIM // 2
    inv = 1.0 / (ROPE_BASE ** (jnp.arange(half, dtype=F32) / half))
    ang = jnp.arange(S).astype(F32)[:, None] * inv[None, :]
    c = jnp.cos(ang)
    s = jnp.sin(ang)
    cos = jnp.tile(jnp.concatenate([c, c], axis=1), (1, N_RET_HEADS))
    sin = jnp.tile(jnp.concatenate([-s, s], axis=1), (1, N_RET_HEADS))
    return cos, sin


def ret_fwd(q, k, v, gate, ng, tables, T=RET_TILE):
    H, S, dh = q.shape
    dmat, qdec, kdec, bdec = tables

    def body(q_ref, k_ref, v_ref, gt_ref, ng_ref, dm_ref, qd_ref, kd_ref, bd_ref, o_ref, y_ref, st_ref, s_sc):
        n = pl.program_id(1)

        @pl.when(n == 0)
        def _():
            s_sc[...] = jnp.zeros_like(s_sc)

        qv = q_ref[...]
        kv = k_ref[...]
        vv = v_ref[...]
        state = s_sc[...]
        st_ref[...] = state
        sc = (_dot_nt(qv, kv) * dm_ref[...]).astype(BF16)
        qd = (qv.astype(F32) * qd_ref[...]).astype(BF16)
        y = _dot(sc, vv) + _dot(qd, state.astype(BF16))
        y_ref[...] = y
        kd = (kv.astype(F32) * kd_ref[...]).astype(BF16)
        s_sc[...] = bd_ref[...] * state + _dot_tn(kd, vv)
        mu = jnp.mean(y, axis=-1, keepdims=True)
        yc = y - mu
        yn = yc * lax.rsqrt(jnp.mean(yc * yc, axis=-1, keepdims=True) + EPS)
        gt = gt_ref[...]
        o_ref[...] = gt * _sigmoid(gt) * (yn * ng_ref[...])

    blk = lambda h, n: (h, n, 0)
    head = lambda h, n: (h, 0, 0)
    return pl.pallas_call(
        body, name="ret_fwd",
        grid=(H, S // T),
        in_specs=[
            pl.BlockSpec((None, T, dh), blk),
            pl.BlockSpec((None, T, dh), blk),
            pl.BlockSpec((None, T, dh), blk),
            pl.BlockSpec((None, T, dh), blk),
            pl.BlockSpec((None, 1, dh), head),
            pl.BlockSpec((None, T, T), head),
            pl.BlockSpec((None, T, dh), head),
            pl.BlockSpec((None, T, dh), head),
            pl.BlockSpec((None, dh, dh), head),
        ],
        out_specs=[
            pl.BlockSpec((None, T, dh), blk),
            pl.BlockSpec((None, T, dh), blk),
            pl.BlockSpec((None, None, dh, dh), lambda h, n: (h, n, 0, 0)),
        ],
        out_shape=[
            jax.ShapeDtypeStruct((H, S, dh), F32),
            jax.ShapeDtypeStruct((H, S, dh), F32),
            jax.ShapeDtypeStruct((H, S // T, dh, dh), F32),
        ],
        scratch_shapes=[pltpu.VMEM((dh, dh), F32)],
        compiler_params=_params(2),
    )(q, k, v, gate, ng, dmat, qdec, kdec, bdec)


def ret_bwd(do, q, k, v, gate, ng, y, states, tables, T=RET_TILE):
    H, S, dh = q.shape
    nb = S // T
    dmat, qdec, kdec, bdec = tables

    def body(do_ref, q_ref, k_ref, v_ref, gt_ref, ng_ref, y_ref, st_ref, dm_ref, qd_ref, kd_ref, bd_ref,
             dq_ref, dk_ref, dv_ref, dgt_ref, dng_ref, u_sc):
        n = pl.program_id(1)

        @pl.when(n == 0)
        def _():
            u_sc[...] = jnp.zeros_like(u_sc)
            dng_ref[...] = jnp.zeros_like(dng_ref)

        yv = y_ref[...]
        mu = jnp.mean(yv, axis=-1, keepdims=True)
        yc = yv - mu
        rstd = lax.rsqrt(jnp.mean(yc * yc, axis=-1, keepdims=True) + EPS)
        yn = yc * rstd
        gt = gt_ref[...]
        sg = _sigmoid(gt)
        ngv = ng_ref[...]
        dout = do_ref[...]
        dgt_ref[...] = dout * (yn * ngv) * (sg * (1.0 + gt * (1.0 - sg)))
        dn = dout * (gt * sg)
        dng_ref[...] += jnp.sum(dn * yn, axis=0, keepdims=True)
        dyn = dn * ngv
        dy = rstd * (dyn - jnp.mean(dyn, axis=-1, keepdims=True) - yn * jnp.mean(dyn * yn, axis=-1, keepdims=True))
        dyb = dy.astype(BF16)

        qv = q_ref[...]
        kv = k_ref[...]
        vv = v_ref[...]
        dm = dm_ref[...]
        qdt = qd_ref[...]
        kdt = kd_ref[...]
        sb = st_ref[...].astype(BF16)
        u = u_sc[...]
        ub = u.astype(BF16)
        dqk = (_dot_nt(dyb, vv) * dm).astype(BF16)
        sc = (_dot_nt(qv, kv) * dm).astype(BF16)
        qd = (qv.astype(F32) * qdt).astype(BF16)
        kd = (kv.astype(F32) * kdt).astype(BF16)
        dq_ref[...] = _dot(dqk, kv) + qdt * _dot_nt(dyb, sb)
        dk_ref[...] = _dot_tn(dqk, qv) + kdt * _dot_nt(vv, ub)
        dv_ref[...] = _dot_tn(sc, dyb) + _dot(kd, ub)
        u_sc[...] = bd_ref[...] * u + _dot_tn(qd, dyb)

    blk = lambda h, n: (h, nb - 1 - n, 0)
    head = lambda h, n: (h, 0, 0)
    return pl.pallas_call(
        body, name="ret_bwd",
        grid=(H, nb),
        in_specs=[
            pl.BlockSpec((None, T, dh), blk),
            pl.BlockSpec((None, T, dh), blk),
            pl.BlockSpec((None, T, dh), blk),
            pl.BlockSpec((None, T, dh), blk),
            pl.BlockSpec((None, T, dh), blk),
            pl.BlockSpec((None, 1, dh), head),
            pl.BlockSpec((None, T, dh), blk),
            pl.BlockSpec((None, None, dh, dh), lambda h, n: (h, nb - 1 - n, 0, 0)),
            pl.BlockSpec((None, T, T), head),
            pl.BlockSpec((None, T, dh), head),
            pl.BlockSpec((None, T, dh), head),
            pl.BlockSpec((None, dh, dh), head),
        ],
        out_specs=[
            pl.BlockSpec((None, T, dh), blk),
            pl.BlockSpec((None, T, dh), blk),
            pl.BlockSpec((None, T, dh), blk),
            pl.BlockSpec((None, T, dh), blk),
            pl.BlockSpec((None, 1, dh), head),
        ],
        out_shape=[jax.ShapeDtypeStruct((H, S, dh), F32)] * 4 + [jax.ShapeDtypeStruct((H, 1, dh), F32)],
        scratch_shapes=[pltpu.VMEM((dh, dh), F32)],
        compiler_params=_params(2),
    )(do, q, k, v, gate, ng, y, states, dmat, qdec, kdec, bdec)


def loss_head(x, g, target, tm=ROW_TILE):
    S = x.shape[0]

    def body(x_ref, g_ref, t_ref, loss_ref, dx_ref, dg_ref):
        i = pl.program_id(0)
        xv = x_ref[...]
        gv = g_ref[...]
        _, xhat = _rms_stats(xv)
        err = xhat * gv - t_ref[...]
        part = 0.5 * jnp.sum(jnp.mean(err * err, axis=-1, keepdims=True), axis=0, keepdims=True)
        dx, _, dg = _rms_bwd(xv, gv, err * (1.0 / D_MODEL))
        dx_ref[...] = dx
        part = jnp.broadcast_to(part, (1, 128))

        @pl.when(i == 0)
        def _():
            loss_ref[...] = part
            dg_ref[...] = dg

        @pl.when(i > 0)
        def _():
            loss_ref[...] += part
            dg_ref[...] += dg

    row = lambda i: (i, 0)
    one = lambda i: (0, 0)
    return pl.pallas_call(
        body, name="loss_head",
        grid=(S // tm,),
        in_specs=[pl.BlockSpec((tm, D_MODEL), row), pl.BlockSpec((1, D_MODEL), one), pl.BlockSpec((tm, D_MODEL), row)],
        out_specs=[pl.BlockSpec((1, 128), one), pl.BlockSpec((tm, D_MODEL), row), pl.BlockSpec((1, D_MODEL), one)],
        out_shape=[
            jax.ShapeDtypeStruct((1, 128), F32),
            jax.ShapeDtypeStruct((S, D_MODEL), F32),
            jax.ShapeDtypeStruct((1, D_MODEL), F32),
        ],
        compiler_params=_params(1),
    )(x, g, target)


def adamw(parts, w, m, v, tr, transposed=False):
    L, R, C = w.shape
    nr = R // tr
    c1 = 1.0 / (1.0 - ADAM_B1 ** ADAM_STEP)
    c2 = 1.0 / (1.0 - ADAM_B2 ** ADAM_STEP)

    def body(*refs):
        p_refs = refs[:L]
        w_ref, m_ref, v_ref, g_ref, d_ref, mo_ref, vo_ref = refs[L:]
        l = pl.program_id(0)
        g = None
        for d in range(N_DEV):
            pd = p_refs[0][d].astype(F32)
            for ll in range(1, L):
                pd = jnp.where(l == ll, p_refs[ll][d].astype(F32), pd)
            g = pd if g is None else g + pd
        if transposed:
            g = g.T
        mn = ADAM_B1 * m_ref[...] + (1.0 - ADAM_B1) * g
        vn = ADAM_B2 * v_ref[...] + (1.0 - ADAM_B2) * (g * g)
        g_ref[...] = g
        mo_ref[...] = mn
        vo_ref[...] = vn
        d_ref[...] = -ADAM_LR * ((mn * c1) / (jnp.sqrt(vn * c2) + ADAM_EPS) + ADAM_WD * w_ref[...])

    def part_spec(ll):
        def block(l, i):
            return jnp.where(l == ll, i, jnp.where(l < ll, 0, nr - 1))
        if transposed:
            return pl.BlockSpec((N_DEV, C, tr), lambda l, i: (0, 0, block(l, i)))
        return pl.BlockSpec((N_DEV, tr, C), lambda l, i: (0, block(l, i), 0))

    blk = pl.BlockSpec((None, tr, C), lambda l, i: (l, i, 0))
    return pl.pallas_call(
        body, name="adamw",
        grid=(L, nr),
        in_specs=[part_spec(ll) for ll in range(L)] + [blk] * 3,
        out_specs=[blk] * 4,
        out_shape=[jax.ShapeDtypeStruct((L, R, C), F32)] * 4,
        compiler_params=_params(2),
    )(*parts, w, m, v)


def _my_id():
    return lax.axis_index("x") * 4 + lax.axis_index("y") * 2 + lax.axis_index("c")


def _peer(k):
    x, y, c = lax.axis_index("x"), lax.axis_index("y"), lax.axis_index("c")
    px = 1 - x if k & 4 else x
    py = 1 - y if k & 2 else y
    pc = 1 - c if k & 1 else c
    return (px, py, pc), px * 4 + py * 2 + pc


GATHER = "gather"
EXCHANGE = "exchange"


def _copies(kind, ins, outs, send_sems, recv_sems, local_sems, receive_side):
    me = _my_id()
    local, sends, recvs = [], [], []
    for t in range(len(ins)):
        src = ins[t] if kind == GATHER else ins[t].at[me]
        local.append(pltpu.make_async_copy(src, outs[t].at[me], local_sems.at[t]))
    for k in range(1, N_DEV):
        dev, pid = _peer(k)
        for t in range(len(ins)):
            sems = dict(send_sem=send_sems.at[t, k - 1], recv_sem=recv_sems.at[t, k - 1],
                        device_id=dev, device_id_type=pl.DeviceIdType.MESH)
            src = ins[t] if kind == GATHER else ins[t].at[pid]
            sends.append(pltpu.make_async_remote_copy(src_ref=src, dst_ref=outs[t].at[me], **sems))
            if receive_side:
                recvs.append(pltpu.make_async_remote_copy(src_ref=src, dst_ref=outs[t].at[pid], **sems))
    return local, sends, recvs


def _comm_start(kind, ins, outs, sems):
    local, sends, _ = _copies(kind, ins, outs, *sems, receive_side=False)
    for cp in local + sends:
        cp.start()


def _comm_wait(kind, ins, outs, sems):
    local, sends, recvs = _copies(kind, ins, outs, *sems, receive_side=True)
    for cp in recvs:
        cp.wait_recv()
    for cp in sends:
        cp.wait_send()
    for cp in local:
        cp.wait()


def _entries(arrays):
    ops = [a[0] if isinstance(a, tuple) else a for a in arrays]
    idx = [a[1] if isinstance(a, tuple) else None for a in arrays]
    return ops, idx


def _views(refs, idx):
    return [r if i is None else r.at[i] for r, i in zip(refs, idx)]


def _comm_shapes(kind, arrays):
    n = len(arrays)
    ops, idx = _entries(arrays)
    shapes = [a.shape if i is None else a.shape[1:] for a, i in zip(ops, idx)]
    out_shape = [jax.ShapeDtypeStruct(((N_DEV,) + s) if kind == GATHER else s, a.dtype) for a, s in zip(ops, shapes)]
    sems = [pltpu.SemaphoreType.DMA((n, N_DEV - 1)), pltpu.SemaphoreType.DMA((n, N_DEV - 1)),
            pltpu.SemaphoreType.DMA((n,))]
    return out_shape, sems


def communicate(kind, arrays):
    n = len(arrays)
    ops, idx = _entries(arrays)

    def body(*refs):
        ins, outs, sems = _views(refs[:n], idx), refs[n:2 * n], refs[2 * n:]
        _comm_start(kind, ins, outs, sems)
        _comm_wait(kind, ins, outs, sems)

    out_shape, sems = _comm_shapes(kind, arrays)
    any_spec = pl.BlockSpec(memory_space=pl.ANY)
    return pl.pallas_call(
        body, name=kind, in_specs=[any_spec] * n, out_specs=[any_spec] * n, out_shape=out_shape, scratch_shapes=sems,
    )(*ops)


def gather_two_level(arrays):
    n = len(arrays)
    ops, idx = _entries(arrays)

    def body(*refs):
        ins, outs = _views(refs[:n], idx), refs[n:2 * n]
        send_sems, recv_sems, local_sems = refs[2 * n:]
        x, y, c = lax.axis_index("x"), lax.axis_index("y"), lax.axis_index("c")
        me, sibling = (x, y, c), (x, y, 1 - c)
        chips = [(1 - x, y), (x, 1 - y), (1 - x, 1 - y)]

        def slot(px, py, pc):
            return px * 4 + py * 2 + pc

        def copy(t, k, src, owner, to):
            return pltpu.make_async_remote_copy(
                src_ref=src, dst_ref=outs[t].at[slot(*owner)], send_sem=send_sems.at[t, k], recv_sem=recv_sems.at[t, k],
                device_id=to, device_id_type=pl.DeviceIdType.MESH)

        local = [pltpu.make_async_copy(ins[t], outs[t].at[slot(*me)], local_sems.at[t]) for t in range(n)]
        first = [copy(t, 0, ins[t], me, sibling) for t in range(n)]
        first += [copy(t, 1 + j, ins[t], me, (*chip, c)) for j, chip in enumerate(chips) for t in range(n)]
        for cp in local + first:
            cp.start()
        passed = []
        for j, chip in enumerate(chips):
            for t in range(n):
                copy(t, 1 + j, ins[t], (*chip, c), me).wait_recv()
                cp = copy(t, 4 + j, outs[t].at[slot(*chip, c)], (*chip, c), sibling)
                cp.start()
                passed.append(cp)
        for t in range(n):
            copy(t, 0, ins[t], sibling, me).wait_recv()
            for j, chip in enumerate(chips):
                copy(t, 4 + j, ins[t], (*chip, 1 - c), me).wait_recv()
        for cp in first + passed:
            cp.wait_send()
        for cp in local:
            cp.wait()

    out_shape, sems = _comm_shapes(GATHER, arrays)
    any_spec = pl.BlockSpec(memory_space=pl.ANY)
    return pl.pallas_call(
        body, name="gather_two_level", in_specs=[any_spec] * n, out_specs=[any_spec] * n, out_shape=out_shape,
        scratch_shapes=sems,
    )(*ops)


def _call(body, operands, comm, *, name, grid, in_specs, out_specs, out_shape, scratch_shapes):
    if comm is None:
        outs = pl.pallas_call(body, name=name, grid=grid, in_specs=in_specs, out_specs=out_specs, out_shape=out_shape,
                              scratch_shapes=scratch_shapes, compiler_params=_params(len(grid)))(*operands)
        return outs, []
    kind, arrays = comm
    comm_ops, comm_idx = _entries(arrays)
    n, n_in, n_out, n_sc = len(arrays), len(in_specs), len(out_specs), len(scratch_shapes)

    def carrier(*refs):
        ins, cins = refs[:n_in], _views(refs[n_in:n_in + n], comm_idx)
        refs = refs[n_in + n:]
        outs, couts = refs[:n_out], refs[n_out:n_out + n]
        scratch, sems = refs[n_out + n:n_out + n + n_sc], refs[n_out + n + n_sc:]
        steps = [pl.program_id(a) for a in range(len(grid))]
        first = functools.reduce(jnp.logical_and, [s == 0 for s in steps])
        last = functools.reduce(jnp.logical_and, [s == g - 1 for s, g in zip(steps, grid)])

        @pl.when(first)
        def _():
            _comm_start(kind, cins, couts, sems)

        body(*ins, *outs, *scratch)

        @pl.when(last)
        def _():
            _comm_wait(kind, cins, couts, sems)

    comm_shape, sems = _comm_shapes(kind, arrays)
    any_spec = pl.BlockSpec(memory_space=pl.ANY)
    outs = pl.pallas_call(
        carrier, name=f"{name}_{kind}", grid=grid,
        in_specs=list(in_specs) + [any_spec] * n,
        out_specs=list(out_specs) + [any_spec] * n,
        out_shape=list(out_shape) + comm_shape,
        scratch_shapes=list(scratch_shapes) + sems,
        compiler_params=_params(len(grid)),
    )(*operands, *comm_ops)
    return outs[:n_out], outs[n_out:]


def _row(v):
    return v.reshape(1, -1)


def _pad_taps(cw):
    return jnp.concatenate([cw, jnp.zeros((CONV_HALO - CONV_WIDTH, D_CONV), F32)], axis=0)


COL_SHARDED = ("ffn1_w_in", "mix_w_in", "ffn2_w_in")
ROW_SHARDED = ("ffn1_w_out", "mix_w_out", "ffn2_w_out")
SMALL = ("ffn1_norm", "mix_norm", "conv_b", "conv_ln_g", "conv_ln_b", "ret_norm_g", "ffn2_norm", "final_norm")
WEIGHTS = ("ffn1_norm", "ffn1_w_in", "ffn1_w_out", "mix_norm", "mix_w_in", "conv_w", "conv_b", "conv_ln_g",
           "conv_ln_b", "ret_norm_g", "mix_w_out", "ffn2_norm", "ffn2_w_in", "ffn2_w_out", "final_norm")
SMALL_ROWS = 32

FFN1 = ("ffn1_w_in", "ffn1_w_out")
MIX = ("mix_w_in", "mix_w_out")
FFN2 = ("ffn2_w_in", "ffn2_w_out")
STAGE_A = [(n, 0) for n in FFN1]
STAGE_B = [(n, 0) for n in MIX] + [("conv_w", None)]
STAGE_C = [(n, 0) for n in FFN2] + [(n, 1) for n in FFN1 + MIX + FFN2]
STAGE_D = [(n, 1) for n in FFN2]
STAGE_E = [(n, 1) for n in MIX + FFN1] + [(n, 0) for n in FFN2]
STAGE_F = [(n, 0) for n in MIX]
STAGE_G = [("ffn1_w_in", 0)]
STAGE_H = [("ffn1_w_out", 0)]


def _natural(name, got):
    if name == "conv_w":
        return got.transpose(1, 2, 0, 3).reshape(DEPTH, CONV_WIDTH, D_CONV)
    return got.reshape(-1, D_MODEL)


def _by_device(grad):
    return grad.reshape(N_DEV, -1, D_MODEL)


def _pack_small(g):
    flat = jnp.concatenate([g[n].reshape(-1) for n in SMALL] + [g["conv_w"].reshape(-1)])
    flat = jnp.concatenate([flat, jnp.zeros((SMALL_ROWS * D_MODEL - flat.shape[0],), F32)])
    return flat.reshape(SMALL_ROWS, D_MODEL)


def _unpack_small(buf, like):
    flat = buf.reshape(-1)
    out, off = {}, 0
    for n in SMALL:
        size = int(np.prod(like[n].shape))
        out[n] = flat[off:off + size].reshape(like[n].shape)
        off += size
    size = DEPTH * CONV_WIDTH * D_CONV
    out["conv_w"] = flat[off:off + size].reshape(DEPTH, CONV_WIDTH, D_CONV)
    return out


def kernel(x, ffn1_norm, ffn1_w_in, ffn1_w_out, mix_norm, mix_w_in, conv_w, conv_b, conv_ln_g, conv_ln_b, ret_norm_g, mix_w_out, ffn2_norm, ffn2_w_in, ffn2_w_out, final_norm, loss_target, m_ffn1_norm, m_ffn1_w_in, m_ffn1_w_out, m_mix_norm, m_mix_w_in, m_conv_w, m_conv_b, m_conv_ln_g, m_conv_ln_b, m_ret_norm_g, m_mix_w_out, m_ffn2_norm, m_ffn2_w_in, m_ffn2_w_out, m_final_norm, v_ffn1_norm, v_ffn1_w_in, v_ffn1_w_out, v_mix_norm, v_mix_w_in, v_conv_w, v_conv_b, v_conv_ln_g, v_conv_ln_b, v_ret_norm_g, v_mix_w_out, v_ffn2_norm, v_ffn2_w_in, v_ffn2_w_out, v_final_norm):
    args = locals()
    w = {n: args[n] for n in WEIGHTS}
    m = {n: args["m_" + n] for n in WEIGHTS}
    v = {n: args["v_" + n] for n in WEIGHTS}
    me = _my_id()
    x = x[0]
    target = loss_target[0]
    S = x.shape[0]
    cos, sin = _rope_tables(S)
    tables = _ret_tables()

    full = {}

    wb = {n: (w[n].transpose(0, 2, 1) if n in COL_SHARDED else w[n]).astype(BF16) for n in COL_SHARDED + ROW_SHARDED}

    def gather(keys):
        return GATHER, [w[n] if n == "conv_w" else (wb[n], l) for n, l in keys]

    def gathered(keys, got):
        for (n, l), g in zip(keys, got):
            full[(n, l)] = _natural(n, g)

    gathered(STAGE_A, gather_two_level(gather(STAGE_A)[1]))

    saved = []
    for l in range(DEPTH):
        sv = {"x0": x}
        (x, sv["gate1"], sv["up1"]), got = ffn_fwd(x, _row(w["ffn1_norm"][l]), full[("ffn1_w_in", l)],
                                                   full[("ffn1_w_out", l)], gather(STAGE_B) if l == 0 else None)
        gathered(STAGE_B if l == 0 else [], got)
        sv["x1"] = x
        (sv["u"], sv["q_sb"], sv["k_sb"], sv["v_sb"], sv["qt_sb"], sv["q_r"], sv["k_r"], sv["v_r"],
         sv["g_r"]) = mix_in_fwd(x, _row(w["mix_norm"][l]), full[("mix_w_in", l)], cos, sin)
        cw = _pad_taps(full[("conv_w", None)][l])
        y_conv, sv["ypre"] = conv_fwd(sv["u"], cw, _row(w["conv_b"][l]), _row(w["conv_ln_g"][l]), _row(w["conv_ln_b"][l]))
        (o_sb, sv["tot"]), got = sb_fwd(sv["q_sb"], sv["k_sb"], sv["v_sb"], gather(STAGE_C) if l == 0 else None)
        gathered(STAGE_C if l == 0 else [], got)
        ng = w["ret_norm_g"][l].reshape(N_RET_HEADS, 1, HEAD_DIM)
        o_r, sv["y_r"], sv["states"] = ret_fwd(sv["q_r"], sv["k_r"], sv["v_r"], sv["g_r"], ng, tables)
        x, sv["ycat"] = mix_out_fwd(y_conv, o_sb, o_r, full[("mix_w_out", l)], x)
        sv["x2"] = x
        (x, sv["gate2"], sv["up2"]), _ = ffn_fwd(x, _row(w["ffn2_norm"][l]), full[("ffn2_w_in", l)],
                                                 full[("ffn2_w_out", l)])
        saved.append(sv)

    loss_acc, dx, dg_final = loss_head(x, _row(w["final_norm"]), target)
    loss = lax.psum(loss_acc[0, 0], ("x", "y", "c"))

    g = {"final_norm": dg_final.reshape(D_MODEL)}
    received = {}

    def exchange(keys, extra=(), dtype=F32):
        return EXCHANGE, [_by_device(g[(n, l)]).astype(dtype) for n, l in keys] + list(extra)

    def exchanged(keys, got):
        for key, p in zip(keys, got):
            received[key] = p

    def ffn_back(dx, x_in, gate, up, norm, names, l, comm=None):
        (dx, h, dyh, dgate, dup, hid, dg), got = ffn_bwd(dx, x_in, _row(norm), gate, up, full[(names[0], l)],
                                                         full[(names[1], l)], comm)
        g[(names[0], l)] = matmul_tn([dgate, dup], h, FF_TILE, D_MODEL, name="ffn_dw_in")
        if [(names[0], l)] == STAGE_G:
            g[(names[1], l)], got_g = matmul_tn([hid], dyh, FF_TILE, D_MODEL, name="ffn_dw_out",
                                                comm=exchange(STAGE_G, dtype=BF16))
            exchanged(STAGE_G, got_g)
        else:
            g[(names[1], l)] = matmul_tn([hid], dyh, FF_TILE, D_MODEL, name="ffn_dw_out")
        return dx, dg.reshape(D_MODEL), got

    for l in reversed(range(DEPTH)):
        sv = saved[l]
        dx, g[("ffn2_norm", l)], _ = ffn_back(dx, sv["x2"], sv["gate2"], sv["up2"], w["ffn2_norm"][l], FFN2, l)
        dxb, dy_conv, do_sb, dot_sb, do_r = mix_out_bwd(dx, full[("mix_w_out", l)])
        g[("mix_w_out", l)] = matmul_tn([sv["ycat"]], dxb, D_MODEL, D_MODEL, name="mix_dw_out")
        cw = _pad_taps(full[("conv_w", None)][l])
        du_conv, dcw, dsm = conv_bwd(dy_conv, sv["ypre"], sv["u"], cw, _row(w["conv_ln_g"][l]), _row(w["conv_ln_b"][l]))
        g[("conv_w", l)] = dcw[:CONV_WIDTH]
        g[("conv_b", l)], g[("conv_ln_g", l)], g[("conv_ln_b", l)] = dsm[0], dsm[1], dsm[2]
        stage = STAGE_D if l == DEPTH - 1 else STAGE_E
        (dq_sb, dk_t, dv_t), got = sb_bwd(sv["q_sb"], sv["k_sb"], sv["v_sb"], do_sb, sv["qt_sb"], dot_sb, sv["tot"],
                                          exchange(stage))
        exchanged(stage, got)
        ng = w["ret_norm_g"][l].reshape(N_RET_HEADS, 1, HEAD_DIM)
        dq_r, dk_r, dv_r, dg_r, dng = ret_bwd(do_r, sv["q_r"], sv["k_r"], sv["v_r"], sv["g_r"], ng, sv["y_r"],
                                              sv["states"], tables)
        g[("ret_norm_g", l)] = dng.reshape(D_RET)
        dx, h, dproj, dg = mix_in_bwd(du_conv, dq_sb, dk_t, dv_t, dq_r, dk_r, dv_r, dg_r, cos, sin,
                                      full[("mix_w_in", l)], sv["x1"], _row(w["mix_norm"][l]), dx)
        g[("mix_norm", l)] = dg.reshape(D_MODEL)
        g[("mix_w_in", l)] = matmul_tn([dproj], h, D_MODEL, D_MODEL, name="mix_dw_in")
        dx, g[("ffn1_norm", l)], got = ffn_back(dx, sv["x0"], sv["gate1"], sv["up1"], w["ffn1_norm"][l], FFN1, l,
                                                exchange(STAGE_F) if l == 0 else None)
        exchanged(STAGE_F if l == 0 else [], got)
    grad_x = dx

    small_names = [n for n in SMALL if n != "final_norm"] + ["conv_w"]
    gs = {n: jnp.stack([g[(n, l)] for l in range(DEPTH)], axis=0) for n in small_names}
    gs["final_norm"] = g["final_norm"]
    small = _pack_small(gs)
    got = communicate(*exchange(STAGE_H, [jnp.broadcast_to(small[None], (N_DEV, SMALL_ROWS, D_MODEL))], dtype=BF16))
    exchanged(STAGE_H, got[:-1])

    grad, delta, new_m, new_v = {}, {}, {}, {}
    for n in COL_SHARDED + ROW_SHARDED:
        rows = w[n].shape[1]
        parts = [received[(n, l)] for l in range(DEPTH)]
        if n in COL_SHARDED and w[n].shape[2] % 128:
            swap = lambda a: a.transpose(0, 2, 1)
            outs = adamw(parts, swap(w[n]), swap(m[n]), swap(v[n]), tr=w[n].shape[2] // 4)
            grad[n], delta[n], new_m[n], new_v[n] = [swap(o) for o in outs]
        else:
            col = n in COL_SHARDED
            grad[n], delta[n], new_m[n], new_v[n] = adamw(parts, w[n], m[n], v[n], tr=128 if col else rows // 2,
                                                          transposed=col)

    def small_pack(d):
        mine = dict(d)
        cwf = jnp.zeros((DEPTH, CONV_WIDTH, D_CONV), F32)
        mine["conv_w"] = lax.dynamic_update_slice(cwf, d["conv_w"], (0, 0, me * (D_CONV // N_DEV)))
        return _pack_small(mine)

    outs = adamw([got[-1]], small_pack(w)[None], small_pack(m)[None], small_pack(v)[None], tr=SMALL_ROWS)
    for dst, o in zip((grad, delta, new_m, new_v), outs):
        un = _unpack_small(o[0], w)
        un["conv_w"] = lax.dynamic_slice(un["conv_w"], (0, 0, me * (D_CONV // N_DEV)),
                                         (DEPTH, CONV_WIDTH, D_CONV // N_DEV))
        dst.update(un)

    return (loss, grad_x[None], *[grad[n] for n in WEIGHTS], *[delta[n] for n in WEIGHTS],
            *[new_m[n] for n in WEIGHTS], *[new_v[n] for n in WEIGHTS])
```

```python
import functools

import numpy as np
import jax
import jax.numpy as jnp
from jax import lax
from jax.experimental import pallas as pl
from jax.experimental.pallas import tpu as pltpu

F32 = jnp.float32
BF16 = jnp.bfloat16

D_MODEL = 1024
DEPTH = 2
D_FF = 2816
D_CONV = 256
CONV_WIDTH = 31
CONV_HALO = 32
D_SB = 512
N_SB_HEADS = 8
D_RET = 256
N_RET_HEADS = 4
HEAD_DIM = 64
D_IN_PROJ = 3072
ROPE_BASE = 10000.0
EPS = 1e-6
N_DEV = 8

ADAM_LR = 0.001
ADAM_B1 = 0.9
ADAM_B2 = 0.999
ADAM_EPS = 1e-08
ADAM_WD = 0.01
ADAM_STEP = 10

VMEM_LIMIT = 56 * 1024 * 1024
ROW_TILE = 512
FF_TILE = 1408
FF_FWD_CHUNK = 256
FF_BWD_CHUNK = 2816
SB_TILE = 256
SB_ROWS = 512
SB_FWD_ROWS = 1024
RET_TILE = 512
CONV_TILE = 256

NT_DIMS = (((1,), (1,)), ((), ()))
TN_DIMS = (((0,), (0,)), ((), ()))


def _params(n_axes, vmem=VMEM_LIMIT):
    return pltpu.CompilerParams(dimension_semantics=("arbitrary",) * n_axes, vmem_limit_bytes=vmem)


def _dot(a, b):
    return jnp.dot(a, b, preferred_element_type=F32)


def _dot_nt(a, b):
    return lax.dot_general(a, b, NT_DIMS, preferred_element_type=F32)


def _dot_tn(a, b):
    return lax.dot_general(a, b, TN_DIMS, preferred_element_type=F32)


def _sigmoid(z):
    return 1.0 / (1.0 + jnp.exp(-z))


def _rms_stats(xv):
    r = lax.rsqrt(jnp.mean(xv * xv, axis=-1, keepdims=True) + EPS)
    return r, xv * r


def _rms_bwd(xv, g, dh):
    r, xhat = _rms_stats(xv)
    dxhat = dh * g
    dx = r * (dxhat - xhat * jnp.mean(dxhat * xhat, axis=-1, keepdims=True))
    dg = jnp.sum(dh * xhat, axis=0, keepdims=True)
    return dx, (xhat * g).astype(BF16), dg


def ffn_fwd(x, g, w_in, w_out, comm=None, tm=ROW_TILE):
    S = x.shape[0]
    chunk = FF_FWD_CHUNK

    def body(x_ref, g_ref, w_ref, wo_ref, y_ref, gate_ref, up_ref):
        xv = x_ref[...]
        _, xhat = _rms_stats(xv)
        h = (xhat * g_ref[...]).astype(BF16)
        acc = None
        for j in range(D_FF // chunk):
            cols = pl.ds(j * chunk, chunk)
            gt = _dot_nt(h, w_ref[cols, :])
            up = _dot_nt(h, w_ref[pl.ds(D_FF + j * chunk, chunk), :])
            gate_ref[:, cols] = gt.astype(BF16)
            up_ref[:, cols] = up.astype(BF16)
            part = _dot((gt * _sigmoid(gt) * up).astype(BF16), wo_ref[cols, :])
            acc = part if acc is None else acc + part
        y_ref[...] = xv + 0.5 * acc

    row = lambda i: (i, 0)
    one = lambda i: (0, 0)
    resident = pl.Buffered(1)
    return _call(
        body, (x, g, w_in, w_out), comm, name="ffn_fwd",
        grid=(S // tm,),
        in_specs=[
            pl.BlockSpec((tm, D_MODEL), row),
            pl.BlockSpec((1, D_MODEL), one),
            pl.BlockSpec((2 * D_FF, D_MODEL), one, pipeline_mode=resident),
            pl.BlockSpec((D_FF, D_MODEL), one, pipeline_mode=resident),
        ],
        out_specs=[
            pl.BlockSpec((tm, D_MODEL), row),
            pl.BlockSpec((tm, D_FF), row),
            pl.BlockSpec((tm, D_FF), row),
        ],
        out_shape=[
            jax.ShapeDtypeStruct((S, D_MODEL), F32),
            jax.ShapeDtypeStruct((S, D_FF), BF16),
            jax.ShapeDtypeStruct((S, D_FF), BF16),
        ],
        scratch_shapes=[],
    )


def ffn_bwd(dy, x, g, gate, up, w_in, w_out, comm=None, tm=ROW_TILE // 2):
    S = x.shape[0]
    chunk = FF_BWD_CHUNK

    def body(dy_ref, x_ref, g_ref, gate_ref, up_ref, w_ref, wo_ref,
             dx_ref, h_ref, dyh_ref, dgate_ref, dup_ref, hid_ref, dg_ref):
        i = pl.program_id(0)
        d2 = (0.5 * dy_ref[...]).astype(BF16)
        dyh_ref[...] = d2
        dh = None
        for j in range(D_FF // chunk):
            cols = pl.ds(j * chunk, chunk)
            dhid = _dot_nt(d2, wo_ref[cols, :])
            gt = gate_ref[:, cols].astype(F32)
            u = up_ref[:, cols].astype(F32)
            sig = _sigmoid(gt)
            sl = gt * sig
            dgate = (dhid * u * (sig * (1.0 + gt * (1.0 - sig)))).astype(BF16)
            dup = (dhid * sl).astype(BF16)
            dgate_ref[:, cols] = dgate
            dup_ref[:, cols] = dup
            hid_ref[:, cols] = (sl * u).astype(BF16)
            part = _dot(dgate, w_ref[cols, :]) + _dot(dup, w_ref[pl.ds(D_FF + j * chunk, chunk), :])
            dh = part if dh is None else dh + part
        dx, h, dg = _rms_bwd(x_ref[...], g_ref[...], dh)
        dx_ref[...] = dy_ref[...] + dx
        h_ref[...] = h

        @pl.when(i == 0)
        def _():
            dg_ref[...] = dg

        @pl.when(i > 0)
        def _():
            dg_ref[...] += dg

    row = lambda i: (i, 0)
    one = lambda i: (0, 0)
    resident = pl.Buffered(1)
    return _call(
        body, (dy, x, g, gate, up, w_in, w_out), comm, name="ffn_bwd",
        grid=(S // tm,),
        in_specs=[
            pl.BlockSpec((tm, D_MODEL), row),
            pl.BlockSpec((tm, D_MODEL), row),
            pl.BlockSpec((1, D_MODEL), one),
            pl.BlockSpec((tm, D_FF), row),
            pl.BlockSpec((tm, D_FF), row),
            pl.BlockSpec((2 * D_FF, D_MODEL), one, pipeline_mode=resident),
            pl.BlockSpec((D_FF, D_MODEL), one, pipeline_mode=resident),
        ],
        out_specs=[
            pl.BlockSpec((tm, D_MODEL), row),
            pl.BlockSpec((tm, D_MODEL), row),
            pl.BlockSpec((tm, D_MODEL), row),
            pl.BlockSpec((tm, D_FF), row),
            pl.BlockSpec((tm, D_FF), row),
            pl.BlockSpec((tm, D_FF), row),
            pl.BlockSpec((1, D_MODEL), one),
        ],
        out_shape=[
            jax.ShapeDtypeStruct((S, D_MODEL), F32),
            jax.ShapeDtypeStruct((S, D_MODEL), BF16),
            jax.ShapeDtypeStruct((S, D_MODEL), BF16),
            jax.ShapeDtypeStruct((S, D_FF), BF16),
            jax.ShapeDtypeStruct((S, D_FF), BF16),
            jax.ShapeDtypeStruct((S, D_FF), BF16),
            jax.ShapeDtypeStruct((1, D_MODEL), F32),
        ],
        scratch_shapes=[],
    )


def matmul_tn(a_list, b, ta, tn, tk=4 * ROW_TILE, name="matmul_tn", comm=None):
    S, ka = a_list[0].shape
    nb = b.shape[1]
    assert S % tk == 0 and ka % ta == 0 and nb % tn == 0
    per = ka // ta

    def body(*refs):
        a_refs, b_ref, o_ref = refs[:-2], refs[-2], refs[-1]
        i = pl.program_id(0)
        k = pl.program_id(2)

        @pl.when(k == 0)
        def _():
            o_ref[...] = jnp.zeros_like(o_ref)

        for t, a_ref in enumerate(a_refs):
            @pl.when(lax.div(i, per) == t)
            def _(a_ref=a_ref):
                o_ref[...] += _dot_tn(a_ref[...], b_ref[...])

    def a_spec(t):
        def index(i, j, k):
            mine = lax.div(i, per) == t
            return jnp.where(mine, k, 0), jnp.where(mine, i - t * per, 0)
        return pl.BlockSpec((tk, ta), index)

    (out,), got = _call(
        body, (*a_list, b), comm, name=name,
        grid=(per * len(a_list), nb // tn, S // tk),
        in_specs=[a_spec(t) for t in range(len(a_list))] + [pl.BlockSpec((tk, tn), lambda i, j, k: (k, j))],
        out_specs=[pl.BlockSpec((ta, tn), lambda i, j, k: (i, j))],
        out_shape=[jax.ShapeDtypeStruct((ka * len(a_list), nb), F32)],
        scratch_shapes=[],
    )
    return (out, got) if comm is not None else out


SB_COLS = (2 * D_CONV, 2 * D_CONV + D_SB, 2 * D_CONV + 2 * D_SB)
RET_COLS = tuple(2 * D_CONV + 3 * D_SB + j * D_RET for j in range(4))


def _swap_halves(x):
    n = x.shape[1]
    lane = lax.broadcasted_iota(jnp.int32, x.shape, 1)
    first = (lane % HEAD_DIM) < (HEAD_DIM // 2)
    return jnp.where(first, pltpu.roll(x, n - HEAD_DIM // 2, 1), pltpu.roll(x, HEAD_DIM // 2, 1))


def _head(x, h):
    return x[:, h * HEAD_DIM:(h + 1) * HEAD_DIM]


def _heads_spec(n_heads, tm):
    return pl.BlockSpec((n_heads, tm, HEAD_DIM), lambda i: (0, i, 0))


def mix_in_fwd(x, g, w, cos, sin, tm=ROW_TILE):
    S = x.shape[0]

    def body(x_ref, g_ref, w_ref, c_ref, s_ref, u_ref, q_ref, k_ref, v_ref, qt_ref, qr_ref, kr_ref, vr_ref, gr_ref):
        _, xhat = _rms_stats(x_ref[...])
        proj = _dot_nt((xhat * g_ref[...]).astype(BF16), w_ref[...])
        u_ref[...] = proj[:, :2 * D_CONV]
        for h in range(N_SB_HEADS):
            q = (_head(proj[:, SB_COLS[0]:SB_COLS[1]], h) * 0.125).astype(BF16)
            q_ref[h] = q
            qt_ref[h] = q.T
            k_ref[h] = _head(proj[:, SB_COLS[1]:SB_COLS[2]], h).astype(BF16)
            v_ref[h] = _head(proj[:, SB_COLS[2]:RET_COLS[0]], h).astype(BF16)
        c = c_ref[...]
        s = s_ref[...]
        qv = proj[:, RET_COLS[0]:RET_COLS[1]]
        kv = proj[:, RET_COLS[1]:RET_COLS[2]]
        q_rot = ((qv * c + _swap_halves(qv) * s) * 0.125).astype(BF16)
        k_rot = (kv * c + _swap_halves(kv) * s).astype(BF16)
        for h in range(N_RET_HEADS):
            qr_ref[h] = _head(q_rot, h)
            kr_ref[h] = _head(k_rot, h)
            vr_ref[h] = _head(proj[:, RET_COLS[2]:RET_COLS[3]], h).astype(BF16)
            gr_ref[h] = _head(proj[:, RET_COLS[3]:], h)

    row = lambda i: (i, 0)
    one = lambda i: (0, 0)
    sb = jax.ShapeDtypeStruct((N_SB_HEADS, S, HEAD_DIM), BF16)
    ret = jax.ShapeDtypeStruct((N_RET_HEADS, S, HEAD_DIM), BF16)
    return pl.pallas_call(
        body, name="mix_in_fwd",
        grid=(S // tm,),
        in_specs=[
            pl.BlockSpec((tm, D_MODEL), row),
            pl.BlockSpec((1, D_MODEL), one),
            pl.BlockSpec((D_IN_PROJ, D_MODEL), one, pipeline_mode=pl.Buffered(1)),
            pl.BlockSpec((tm, D_RET), row),
            pl.BlockSpec((tm, D_RET), row),
        ],
        out_specs=[
            pl.BlockSpec((tm, 2 * D_CONV), row),
            _heads_spec(N_SB_HEADS, tm), _heads_spec(N_SB_HEADS, tm), _heads_spec(N_SB_HEADS, tm),
            pl.BlockSpec((N_SB_HEADS, HEAD_DIM, tm), lambda i: (0, 0, i)),
            _heads_spec(N_RET_HEADS, tm), _heads_spec(N_RET_HEADS, tm), _heads_spec(N_RET_HEADS, tm),
            _heads_spec(N_RET_HEADS, tm),
        ],
        out_shape=[
            jax.ShapeDtypeStruct((S, 2 * D_CONV), F32), sb, sb, sb,
            jax.ShapeDtypeStruct((N_SB_HEADS, HEAD_DIM, S), BF16),
            ret, ret, ret, jax.ShapeDtypeStruct((N_RET_HEADS, S, HEAD_DIM), F32),
        ],
        compiler_params=_params(1),
    )(x, g, w, cos, sin)


def mix_in_bwd(du, dq, dkt, dvt, dqr, dkr, dvr, dgr, cos, sin, w, x, g, dy, tm=SB_TILE):
    S = x.shape[0]
    assert dkt.shape[-1] == tm

    def body(du_ref, dq_ref, dkt_ref, dvt_ref, dqr_ref, dkr_ref, dvr_ref, dgr_ref, c_ref, s_ref, w_ref, x_ref, g_ref,
             dy_ref, dx_ref, h_ref, dp_ref, dg_ref):
        i = pl.program_id(0)
        sb_heads = range(N_SB_HEADS)
        ret_heads = range(N_RET_HEADS)
        c = c_ref[...]
        s = s_ref[...]
        dq_rot = jnp.concatenate([dqr_ref[h] for h in ret_heads], axis=1) * 0.125
        dk_rot = jnp.concatenate([dkr_ref[h] for h in ret_heads], axis=1)
        dproj = jnp.concatenate([
            du_ref[...].astype(BF16),
            jnp.concatenate([dq_ref[h] * 0.125 for h in sb_heads], axis=1).astype(BF16),
            jnp.concatenate([dkt_ref[h, 0].T for h in sb_heads], axis=1).astype(BF16),
            jnp.concatenate([dvt_ref[h, 0].T for h in sb_heads], axis=1).astype(BF16),
            (dq_rot * c - _swap_halves(dq_rot) * s).astype(BF16),
            (dk_rot * c - _swap_halves(dk_rot) * s).astype(BF16),
            jnp.concatenate([dvr_ref[h] for h in ret_heads], axis=1).astype(BF16),
            jnp.concatenate([dgr_ref[h] for h in ret_heads], axis=1).astype(BF16)], axis=1)
        dp_ref[...] = dproj
        dh = _dot(dproj, w_ref[...])
        dx, h, dg = _rms_bwd(x_ref[...], g_ref[...], dh)
        dx_ref[...] = dy_ref[...] + dx
        h_ref[...] = h

        @pl.when(i == 0)
        def _():
            dg_ref[...] = dg

        @pl.when(i > 0)
        def _():
            dg_ref[...] += dg

    row = lambda i: (i, 0)
    one = lambda i: (0, 0)
    tiles = pl.BlockSpec((N_SB_HEADS, 1, HEAD_DIM, tm), lambda i: (0, i, 0, 0))
    return pl.pallas_call(
        body, name="mix_in_bwd",
        grid=(S // tm,),
        in_specs=[
            pl.BlockSpec((tm, 2 * D_CONV), row),
            _heads_spec(N_SB_HEADS, tm), tiles, tiles,
            _heads_spec(N_RET_HEADS, tm), _heads_spec(N_RET_HEADS, tm), _heads_spec(N_RET_HEADS, tm),
            _heads_spec(N_RET_HEADS, tm),
            pl.BlockSpec((tm, D_RET), row),
            pl.BlockSpec((tm, D_RET), row),
            pl.BlockSpec((D_IN_PROJ, D_MODEL), one, pipeline_mode=pl.Buffered(1)),
            pl.BlockSpec((tm, D_MODEL), row),
            pl.BlockSpec((1, D_MODEL), one),
            pl.BlockSpec((tm, D_MODEL), row),
        ],
        out_specs=[
            pl.BlockSpec((tm, D_MODEL), row),
            pl.BlockSpec((tm, D_MODEL), row),
            pl.BlockSpec((tm, D_IN_PROJ), row),
            pl.BlockSpec((1, D_MODEL), one),
        ],
        out_shape=[
            jax.ShapeDtypeStruct((S, D_MODEL), F32),
            jax.ShapeDtypeStruct((S, D_MODEL), BF16),
            jax.ShapeDtypeStruct((S, D_IN_PROJ), BF16),
            jax.ShapeDtypeStruct((1, D_MODEL), F32),
        ],
        compiler_params=_params(1),
    )(du, dq, dkt, dvt, dqr, dkr, dvr, dgr, cos, sin, w, x, g, dy)


def mix_out_fwd(y_conv, o_sb, o_ret, w, x, tm=ROW_TILE):
    S = x.shape[0]

    def body(yc_ref, sb_ref, rt_ref, w_ref, x_ref, o_ref, ycat_ref):
        ycat = jnp.concatenate(
            [yc_ref[...]] + [sb_ref[h].astype(BF16) for h in range(N_SB_HEADS)]
            + [rt_ref[h].astype(BF16) for h in range(N_RET_HEADS)], axis=1)
        ycat_ref[...] = ycat
        o_ref[...] = x_ref[...] + _dot(ycat, w_ref[...])

    row = lambda i: (i, 0)
    return pl.pallas_call(
        body, name="mix_out_fwd",
        grid=(S // tm,),
        in_specs=[
            pl.BlockSpec((tm, D_CONV), row),
            _heads_spec(N_SB_HEADS, tm),
            _heads_spec(N_RET_HEADS, tm),
            pl.BlockSpec((D_MODEL, D_MODEL), lambda i: (0, 0)),
            pl.BlockSpec((tm, D_MODEL), row),
        ],
        out_specs=[pl.BlockSpec((tm, D_MODEL), row), pl.BlockSpec((tm, D_MODEL), row)],
        out_shape=[jax.ShapeDtypeStruct((S, D_MODEL), F32), jax.ShapeDtypeStruct((S, D_MODEL), BF16)],
        compiler_params=_params(1),
    )(y_conv, o_sb, o_ret, w, x)


def mix_out_bwd(dy, w, tm=ROW_TILE):
    S = dy.shape[0]

    def body(dy_ref, w_ref, dyb_ref, dc_ref, do_ref, dot_ref, dr_ref):
        d = dy_ref[...].astype(BF16)
        dyb_ref[...] = d
        dycat = _dot_nt(d, w_ref[...])
        dc_ref[...] = dycat[:, :D_CONV]
        for h in range(N_SB_HEADS):
            do = _head(dycat[:, D_CONV:D_CONV + D_SB], h).astype(BF16)
            do_ref[h] = do
            dot_ref[h] = do.T
        for h in range(N_RET_HEADS):
            dr_ref[h] = _head(dycat[:, D_CONV + D_SB:], h)

    row = lambda i: (i, 0)
    return pl.pallas_call(
        body, name="mix_out_bwd",
        grid=(S // tm,),
        in_specs=[
            pl.BlockSpec((tm, D_MODEL), row),
            pl.BlockSpec((D_MODEL, D_MODEL), lambda i: (0, 0)),
        ],
        out_specs=[
            pl.BlockSpec((tm, D_MODEL), row),
            pl.BlockSpec((tm, D_CONV), row),
            _heads_spec(N_SB_HEADS, tm),
            pl.BlockSpec((N_SB_HEADS, HEAD_DIM, tm), lambda i: (0, 0, i)),
            _heads_spec(N_RET_HEADS, tm),
        ],
        out_shape=[
            jax.ShapeDtypeStruct((S, D_MODEL), BF16),
            jax.ShapeDtypeStruct((S, D_CONV), F32),
            jax.ShapeDtypeStruct((N_SB_HEADS, S, HEAD_DIM), BF16),
            jax.ShapeDtypeStruct((N_SB_HEADS, HEAD_DIM, S), BF16),
            jax.ShapeDtypeStruct((N_RET_HEADS, S, HEAD_DIM), F32),
        ],
        compiler_params=_params(1),
    )(dy, w)


def _rows_from(x, start, n):
    return pltpu.roll(x, (x.shape[0] - start) % x.shape[0], 0)[:n]


def _conv_ln(ypre, ln_g, ln_b):
    mu = jnp.mean(ypre, axis=-1, keepdims=True)
    yc = ypre - mu
    rstd = lax.rsqrt(jnp.mean(yc * yc, axis=-1, keepdims=True) + EPS)
    yn = yc * rstd
    return yn, rstd, yn * ln_g + ln_b


def conv_fwd(proj, cw, cb, ln_g, ln_b, tm=CONV_TILE):
    S = proj.shape[0]
    hb = tm // CONV_HALO

    def body(a_ref, b_ref, ap_ref, bp_ref, cw_ref, cb_ref, g_ref, bb_ref, y_ref, ypre_ref, v_sc):
        i = pl.program_id(0)
        prev = ap_ref[...] * _sigmoid(bp_ref[...])
        v_sc[pl.ds(0, CONV_HALO), :] = jnp.where(i > 0, prev, 0.0)
        v_sc[pl.ds(CONV_HALO, tm), :] = a_ref[...] * _sigmoid(b_ref[...])
        vext = v_sc[...]
        acc = jnp.zeros((tm, D_CONV), F32)
        for j in range(CONV_WIDTH):
            acc = acc + cw_ref[pl.ds(j, 1), :] * _rows_from(vext, CONV_HALO - (CONV_WIDTH - 1) + j, tm)
        ypre = acc + cb_ref[...]
        ypre_ref[...] = ypre
        _, _, z = _conv_ln(ypre, g_ref[...], bb_ref[...])
        y_ref[...] = (z * _sigmoid(z)).astype(BF16)

    one = lambda i: (0, 0)
    return pl.pallas_call(
        body, name="conv_fwd",
        grid=(S // tm,),
        in_specs=[
            pl.BlockSpec((tm, D_CONV), lambda i: (i, 0)),
            pl.BlockSpec((tm, D_CONV), lambda i: (i, 1)),
            pl.BlockSpec((CONV_HALO, D_CONV), lambda i: (jnp.maximum(i * hb - 1, 0), 0)),
            pl.BlockSpec((CONV_HALO, D_CONV), lambda i: (jnp.maximum(i * hb - 1, 0), 1)),
            pl.BlockSpec((CONV_HALO, D_CONV), one),
            pl.BlockSpec((1, D_CONV), one),
            pl.BlockSpec((1, D_CONV), one),
            pl.BlockSpec((1, D_CONV), one),
        ],
        out_specs=[pl.BlockSpec((tm, D_CONV), lambda i: (i, 0)), pl.BlockSpec((tm, D_CONV), lambda i: (i, 0))],
        out_shape=[jax.ShapeDtypeStruct((S, D_CONV), BF16), jax.ShapeDtypeStruct((S, D_CONV), F32)],
        scratch_shapes=[pltpu.VMEM((tm + CONV_HALO, D_CONV), F32)],
        compiler_params=_params(1),
    )(proj, proj, proj, proj, cw, cb, ln_g, ln_b)


def conv_bwd(dyc, ypre, proj, cw, ln_g, ln_b, tm=CONV_TILE):
    S = ypre.shape[0]
    hb = tm // CONV_HALO
    nblk = S // tm
    last_halo = S // CONV_HALO - 1

    def dpre(dy, yp, g, bb):
        yn, rstd, z = _conv_ln(yp, g, bb)
        sg = _sigmoid(z)
        dz = dy * (sg * (1.0 + z * (1.0 - sg)))
        dyn = dz * g
        d = rstd * (dyn - jnp.mean(dyn, axis=-1, keepdims=True) - yn * jnp.mean(dyn * yn, axis=-1, keepdims=True))
        return d, dz * yn, dz

    def body(dy_ref, yp_ref, dyn_ref, ypn_ref, a_ref, b_ref, ap_ref, bp_ref, cw_ref, g_ref, bb_ref,
             du_ref, dcw_ref, dsm_ref, d_sc, v_sc):
        i = pl.program_id(0)
        g = g_ref[...]
        bb = bb_ref[...]
        d_main, dgn, dz = dpre(dy_ref[...], yp_ref[...], g, bb)
        d_next, _, _ = dpre(dyn_ref[...], ypn_ref[...], g, bb)
        d_sc[pl.ds(0, tm), :] = d_main
        d_sc[pl.ds(tm, CONV_HALO), :] = jnp.where(i < nblk - 1, d_next, 0.0)
        a = a_ref[...]
        sb = _sigmoid(b_ref[...])
        prev = ap_ref[...] * _sigmoid(bp_ref[...])
        v_sc[pl.ds(0, CONV_HALO), :] = jnp.where(i > 0, prev, 0.0)
        v_sc[pl.ds(CONV_HALO, tm), :] = a * sb

        @pl.when(i == 0)
        def _():
            dcw_ref[...] = jnp.zeros_like(dcw_ref)
            dsm_ref[...] = jnp.zeros_like(dsm_ref)

        dext = d_sc[...]
        vext = v_sc[...]
        dv = jnp.zeros((tm, D_CONV), F32)
        for j in range(CONV_WIDTH):
            dv = dv + cw_ref[pl.ds(j, 1), :] * _rows_from(dext, CONV_WIDTH - 1 - j, tm)
            shifted = _rows_from(vext, CONV_HALO - (CONV_WIDTH - 1) + j, tm)
            dcw_ref[pl.ds(j, 1), :] += jnp.sum(d_main * shifted, axis=0, keepdims=True)
        du_ref[:, pl.ds(0, D_CONV)] = dv * sb
        du_ref[:, pl.ds(D_CONV, D_CONV)] = dv * a * sb * (1.0 - sb)
        dsm_ref[pl.ds(0, 1), :] += jnp.sum(d_main, axis=0, keepdims=True)
        dsm_ref[pl.ds(1, 1), :] += jnp.sum(dgn, axis=0, keepdims=True)
        dsm_ref[pl.ds(2, 1), :] += jnp.sum(dz, axis=0, keepdims=True)

    one = lambda i: (0, 0)
    prev_map = lambda c: (lambda i: (jnp.maximum(i * hb - 1, 0), c))
    next_map = lambda i: (jnp.minimum((i + 1) * hb, last_halo), 0)
    return pl.pallas_call(
        body, name="conv_bwd",
        grid=(nblk,),
        in_specs=[
            pl.BlockSpec((tm, D_CONV), lambda i: (i, 0)),
            pl.BlockSpec((tm, D_CONV), lambda i: (i, 0)),
            pl.BlockSpec((CONV_HALO, D_CONV), next_map),
            pl.BlockSpec((CONV_HALO, D_CONV), next_map),
            pl.BlockSpec((tm, D_CONV), lambda i: (i, 0)),
            pl.BlockSpec((tm, D_CONV), lambda i: (i, 1)),
            pl.BlockSpec((CONV_HALO, D_CONV), prev_map(0)),
            pl.BlockSpec((CONV_HALO, D_CONV), prev_map(1)),
            pl.BlockSpec((CONV_HALO, D_CONV), one),
            pl.BlockSpec((1, D_CONV), one),
            pl.BlockSpec((1, D_CONV), one),
        ],
        out_specs=[
            pl.BlockSpec((tm, 2 * D_CONV), lambda i: (i, 0)),
            pl.BlockSpec((CONV_HALO, D_CONV), one),
            pl.BlockSpec((8, D_CONV), one),
        ],
        out_shape=[
            jax.ShapeDtypeStruct((S, 2 * D_CONV), F32),
            jax.ShapeDtypeStruct((CONV_HALO, D_CONV), F32),
            jax.ShapeDtypeStruct((8, D_CONV), F32),
        ],
        scratch_shapes=[pltpu.VMEM((tm + CONV_HALO, D_CONV), F32), pltpu.VMEM((tm + CONV_HALO, D_CONV), F32)],
        compiler_params=_params(1),
    )(dyc, ypre, dyc, ypre, proj, proj, proj, proj, cw, ln_g, ln_b)


SB_GROUP = 8


def _softplus(z):
    neg_abs = lax.bitcast_convert_type(lax.bitcast_convert_type(z, jnp.uint32) | jnp.uint32(0x80000000), F32)
    return jnp.maximum(z, 0.0) + jnp.log(1.0 + jnp.exp(neg_abs))


def _full_groups(n, body):
    def step(t, c):
        body(t * SB_GROUP)
        return c

    lax.fori_loop(0, lax.div(n, SB_GROUP), step, 0)


def _last_group(n, step, body):
    r = lax.rem(n, SB_GROUP)
    for k in range(0, SB_GROUP, step):
        @pl.when(r == k)
        def _(k=k):
            body(k)


def _rows(xs):
    return xs[0] if len(xs) == 1 else jnp.concatenate(xs, axis=0)


def sb_fwd(q, k, v, comm=None, T=SB_TILE, Q=SB_FWD_ROWS):
    H, S, dh = q.shape
    M = Q // T

    def body(q_ref, k_ref, v_ref, o_ref, tot_ref, acc_sc, car_sc):
        qb = pl.program_id(1)
        qv = q_ref[...]
        row = lax.broadcasted_iota(jnp.int32, (T, T), 0)
        col = lax.broadcasted_iota(jnp.int32, (T, T), 1)
        tri = jnp.where(row >= col, 1.0, 0.0).astype(BF16)
        qrow = lax.broadcasted_iota(jnp.int32, (Q, T), 0)
        kcol = lax.broadcasted_iota(jnp.int32, (Q, T), 1)
        causal = {d + 1: kcol + d * T < qrow for d in range(M)}
        acc_sc[...] = jnp.zeros_like(acc_sc)
        car_sc[...] = jnp.zeros_like(car_sc)

        def logits(kb, masked):
            ks = k_ref[pl.ds(pl.multiple_of(kb * T, T), T), :]
            z = _dot_nt(qv, ks)
            nb = _softplus(z)
            if masked:
                nb = jnp.where(causal[masked], nb, 0.0)
            return z, nb.astype(BF16)

        def group(kbs, diag):
            parts = [logits(kb, d) for kb, d in zip(kbs, diag)]
            pall = _dot(_rows([nb for _, nb in parts]), tri)
            carry = car_sc[...]
            out = None
            for j, kb in enumerate(kbs):
                p = pall[j * Q:(j + 1) * Q]
                vs = v_ref[pl.ds(pl.multiple_of(kb * T, T), T), :]
                w = jnp.exp((parts[j][0] - carry) - p)
                if diag[j]:
                    w = jnp.where(causal[diag[j]], w, 0.0)
                o = _dot(w.astype(BF16), vs)
                out = o if out is None else out + o
                carry = carry + p[:, 0:1]
            acc_sc[...] += out
            car_sc[...] = carry

        full = M * qb
        _last_group(full, M, lambda r: group([full + d for d in reversed(range(M))] + [full - 1 - o for o in range(r)],
                                             [d + 1 for d in reversed(range(M))] + [0] * r))
        rest = full - lax.rem(full, SB_GROUP)
        _full_groups(rest, lambda o: group([rest - 1 - o - j for j in range(SB_GROUP)], [0] * SB_GROUP))
        o_ref[...] = acc_sc[...]
        tot_ref[...] = car_sc[...]

    return _call(
        body, (q, k, v), comm, name="sb_fwd",
        grid=(H, S // Q),
        in_specs=[
            pl.BlockSpec((None, Q, dh), lambda h, i: (h, i, 0)),
            pl.BlockSpec((None, S, dh), lambda h, i: (h, 0, 0)),
            pl.BlockSpec((None, S, dh), lambda h, i: (h, 0, 0)),
        ],
        out_specs=[
            pl.BlockSpec((None, Q, dh), lambda h, i: (h, i, 0)),
            pl.BlockSpec((None, Q, 1), lambda h, i: (h, i, 0)),
        ],
        out_shape=[jax.ShapeDtypeStruct((H, S, dh), F32), jax.ShapeDtypeStruct((H, S, 1), F32)],
        scratch_shapes=[pltpu.VMEM((Q, dh), F32), pltpu.VMEM((Q, 1), F32)],
    )


def sb_bwd(q, k, v, do, qt, dot, tot, comm=None, T=SB_TILE, Q=SB_ROWS):
    H, S, dh = q.shape
    nt = S // T
    M = Q // T

    def body(q_ref, k_ref, v_ref, do_ref, qt_ref, dot_ref, tot_ref, dq_ref, dk_ref, dv_ref, acc_sc, rc_sc, gc_sc):
        qb = pl.program_id(1)
        qv = q_ref[...]
        dov = do_ref[...]
        qtv = qt_ref[...]
        dotv = dot_ref[...]
        row = lax.broadcasted_iota(jnp.int32, (T, T), 0)
        col = lax.broadcasted_iota(jnp.int32, (T, T), 1)
        before = jnp.where(row < col, 1.0, 0.0).astype(BF16)
        qrow = lax.broadcasted_iota(jnp.int32, (Q, T), 0)
        kcol = lax.broadcasted_iota(jnp.int32, (Q, T), 1)
        causal = {d + 1: kcol + d * T < qrow for d in range(M)}
        acc_sc[...] = jnp.zeros_like(acc_sc)
        rc_sc[...] = tot_ref[...]
        gc_sc[...] = jnp.zeros_like(gc_sc)

        @pl.when(qb == 0)
        def _():
            dk_ref[...] = jnp.zeros_like(dk_ref)
            dv_ref[...] = jnp.zeros_like(dv_ref)

        def first(kb, masked):
            start = pl.multiple_of(kb * T, T)
            z = _dot_nt(qv, k_ref[pl.ds(start, T), :])
            nb = _softplus(z)
            sig = jnp.exp(z - nb)
            if masked:
                nb = jnp.where(causal[masked], nb, 0.0)
            dw = _dot_nt(dov, v_ref[pl.ds(start, T), :])
            return z, sig, nb.astype(BF16), dw

        def group(kbs, diag):
            parts = [first(kb, d) for kb, d in zip(kbs, diag)]
            pall = _dot(_rows([p[2] for p in parts]), before)
            rc = rc_sc[...]
            ws, gs, ghs = [], [], []
            for j in range(len(kbs)):
                z, _, nbh, dw = parts[j]
                p = pall[j * Q:(j + 1) * Q]
                w = jnp.exp((z - rc) + p)
                rc = rc - (p[:, T - 1:T] + nbh[:, T - 1:T].astype(F32))
                if diag[j]:
                    w = jnp.where(causal[diag[j]], w, 0.0)
                g = dw * w
                ws.append(w.astype(BF16))
                gs.append(g)
                ghs.append(g.astype(BF16))
            glall = _dot(_rows(ghs), before)
            gc = gc_sc[...]
            dq = None
            for j, kb in enumerate(kbs):
                ks = k_ref[pl.ds(pl.multiple_of(kb * T, T), T), :]
                gl = glall[j * Q:(j + 1) * Q]
                dz = gs[j] - parts[j][1] * (gs[j] + (gl + gc))
                gc = gc + gl[:, T - 1:T] + ghs[j][:, T - 1:T].astype(F32)
                if diag[j]:
                    dz = jnp.where(causal[diag[j]], dz, 0.0)
                dzb = dz.astype(BF16)
                d = _dot(dzb, ks)
                dq = d if dq is None else dq + d
                dk_ref[kb] += _dot(qtv, dzb)
                dv_ref[kb] += _dot(dotv, ws[j])
            acc_sc[...] += dq
            rc_sc[...] = rc
            gc_sc[...] = gc

        full = M * qb
        _full_groups(full, lambda o: group([o + j for j in range(SB_GROUP)], [0] * SB_GROUP))
        rest = full - lax.rem(full, SB_GROUP)
        _last_group(full, M, lambda r: group([rest + j for j in range(r)] + [full + d for d in range(M)],
                                             [0] * r + [d + 1 for d in range(M)]))
        dq_ref[...] = acc_sc[...]

    return _call(
        body, (q, k, v, do, qt, dot, tot), comm, name="sb_bwd",
        grid=(H, S // Q),
        in_specs=[
            pl.BlockSpec((None, Q, dh), lambda h, i: (h, i, 0)),
            pl.BlockSpec((None, S, dh), lambda h, i: (h, 0, 0)),
            pl.BlockSpec((None, S, dh), lambda h, i: (h, 0, 0)),
            pl.BlockSpec((None, Q, dh), lambda h, i: (h, i, 0)),
            pl.BlockSpec((None, dh, Q), lambda h, i: (h, 0, i)),
            pl.BlockSpec((None, dh, Q), lambda h, i: (h, 0, i)),
            pl.BlockSpec((None, Q, 1), lambda h, i: (h, i, 0)),
        ],
        out_specs=[
            pl.BlockSpec((None, Q, dh), lambda h, i: (h, i, 0)),
            pl.BlockSpec((None, nt, dh, T), lambda h, i: (h, 0, 0, 0)),
            pl.BlockSpec((None, nt, dh, T), lambda h, i: (h, 0, 0, 0)),
        ],
        out_shape=[jax.ShapeDtypeStruct((H, S, dh), F32), jax.ShapeDtypeStruct((H, nt, dh, T), F32),
                   jax.ShapeDtypeStruct((H, nt, dh, T), F32)],
        scratch_shapes=[pltpu.VMEM((Q, dh), F32), pltpu.VMEM((Q, 1), F32), pltpu.VMEM((Q, 1), F32)],
    )


def _ret_tables(T=RET_TILE):
    hh = jnp.arange(N_RET_HEADS, dtype=F32)
    log_gamma = jnp.log1p(-jnp.exp2(-5.0 - hh))
    idx = jnp.arange(T, dtype=F32)
    diff = idx[:, None] - idx[None, :]
    ci = (jnp.arange(T) // 64)
    same = ci[:, None] == ci[None, :]
    earlier = ci[None, :] < ci[:, None]
    dist = jnp.where(same, jnp.abs(diff), diff)
    dmat = jnp.where(same | earlier, jnp.exp(log_gamma[:, None, None] * dist[None]), 0.0)
    ones = jnp.ones((1, 1, HEAD_DIM), F32)
    qdec = jnp.exp(log_gamma[:, None] * (idx + 1.0)[None, :])[:, :, None] * ones
    kdec = jnp.exp(log_gamma[:, None] * (T - 1.0 - idx)[None, :])[:, :, None] * ones
    bdec = jnp.exp(log_gamma * T)[:, None, None] * jnp.ones((1, HEAD_DIM, HEAD_DIM), F32)
    return dmat, qdec, kdec, bdec


def _rope_tables(S):
    half = HEAD_DIM // 2
    inv = 1.0 / (ROPE_BASE ** (jnp.arange(half, dtype=F32) / half))
    ang = jnp.arange(S).astype(F32)[:, None] * inv[None, :]
    c = jnp.cos(ang)
    s = jnp.sin(ang)
    cos = jnp.tile(jnp.concatenate([c, c], axis=1), (1, N_RET_HEADS))
    sin = jnp.tile(jnp.concatenate([-s, s], axis=1), (1, N_RET_HEADS))
    return cos, sin


def ret_fwd(q, k, v, gate, ng, tables, T=RET_TILE):
    H, S, dh = q.shape
    dmat, qdec, kdec, bdec = tables

    def body(q_ref, k_ref, v_ref, gt_ref, ng_ref, dm_ref, qd_ref, kd_ref, bd_ref, o_ref, y_ref, st_ref, s_sc):
        n = pl.program_id(1)

        @pl.when(n == 0)
        def _():
            s_sc[...] = jnp.zeros_like(s_sc)

        qv = q_ref[...]
        kv = k_ref[...]
        vv = v_ref[...]
        state = s_sc[...]
        st_ref[...] = state
        sc = (_dot_nt(qv, kv) * dm_ref[...]).astype(BF16)
        qd = (qv.astype(F32) * qd_ref[...]).astype(BF16)
        y = _dot(sc, vv) + _dot(qd, state.astype(BF16))
        y_ref[...] = y
        kd = (kv.astype(F32) * kd_ref[...]).astype(BF16)
        s_sc[...] = bd_ref[...] * state + _dot_tn(kd, vv)
        mu = jnp.mean(y, axis=-1, keepdims=True)
        yc = y - mu
        yn = yc * lax.rsqrt(jnp.mean(yc * yc, axis=-1, keepdims=True) + EPS)
        gt = gt_ref[...]
        o_ref[...] = gt * _sigmoid(gt) * (yn * ng_ref[...])

    blk = lambda h, n: (h, n, 0)
    head = lambda h, n: (h, 0, 0)
    return pl.pallas_call(
        body, name="ret_fwd",
        grid=(H, S // T),
        in_specs=[
            pl.BlockSpec((None, T, dh), blk),
            pl.BlockSpec((None, T, dh), blk),
            pl.BlockSpec((None, T, dh), blk),
            pl.BlockSpec((None, T, dh), blk),
            pl.BlockSpec((None, 1, dh), head),
            pl.BlockSpec((None, T, T), head),
            pl.BlockSpec((None, T, dh), head),
            pl.BlockSpec((None, T, dh), head),
            pl.BlockSpec((None, dh, dh), head),
        ],
        out_specs=[
            pl.BlockSpec((None, T, dh), blk),
            pl.BlockSpec((None, T, dh), blk),
            pl.BlockSpec((None, None, dh, dh), lambda h, n: (h, n, 0, 0)),
        ],
        out_shape=[
            jax.ShapeDtypeStruct((H, S, dh), F32),
            jax.ShapeDtypeStruct((H, S, dh), F32),
            jax.ShapeDtypeStruct((H, S // T, dh, dh), F32),
        ],
        scratch_shapes=[pltpu.VMEM((dh, dh), F32)],
        compiler_params=_params(2),
    )(q, k, v, gate, ng, dmat, qdec, kdec, bdec)


def ret_bwd(do, q, k, v, gate, ng, y, states, tables, T=RET_TILE):
    H, S, dh = q.shape
    nb = S // T
    dmat, qdec, kdec, bdec = tables

    def body(do_ref, q_ref, k_ref, v_ref, gt_ref, ng_ref, y_ref, st_ref, dm_ref, qd_ref, kd_ref, bd_ref,
             dq_ref, dk_ref, dv_ref, dgt_ref, dng_ref, u_sc):
        n = pl.program_id(1)

        @pl.when(n == 0)
        def _():
            u_sc[...] = jnp.zeros_like(u_sc)
            dng_ref[...] = jnp.zeros_like(dng_ref)

        yv = y_ref[...]
        mu = jnp.mean(yv, axis=-1, keepdims=True)
        yc = yv - mu
        rstd = lax.rsqrt(jnp.mean(yc * yc, axis=-1, keepdims=True) + EPS)
        yn = yc * rstd
        gt = gt_ref[...]
        sg = _sigmoid(gt)
        ngv = ng_ref[...]
        dout = do_ref[...]
        dgt_ref[...] = dout * (yn * ngv) * (sg * (1.0 + gt * (1.0 - sg)))
        dn = dout * (gt * sg)
        dng_ref[...] += jnp.sum(dn * yn, axis=0, keepdims=True)
        dyn = dn * ngv
        dy = rstd * (dyn - jnp.mean(dyn, axis=-1, keepdims=True) - yn * jnp.mean(dyn * yn, axis=-1, keepdims=True))
        dyb = dy.astype(BF16)

        qv = q_ref[...]
        kv = k_ref[...]
        vv = v_ref[...]
        dm = dm_ref[...]
        qdt = qd_ref[...]
        kdt = kd_ref[...]
        sb = st_ref[...].astype(BF16)
        u = u_sc[...]
        ub = u.astype(BF16)
        dqk = (_dot_nt(dyb, vv) * dm).astype(BF16)
        sc = (_dot_nt(qv, kv) * dm).astype(BF16)
        qd = (qv.astype(F32) * qdt).astype(BF16)
        kd = (kv.astype(F32) * kdt).astype(BF16)
        dq_ref[...] = _dot(dqk, kv) + qdt * _dot_nt(dyb, sb)
        dk_ref[...] = _dot_tn(dqk, qv) + kdt * _dot_nt(vv, ub)
        dv_ref[...] = _dot_tn(sc, dyb) + _dot(kd, ub)
        u_sc[...] = bd_ref[...] * u + _dot_tn(qd, dyb)

    blk = lambda h, n: (h, nb - 1 - n, 0)
    head = lambda h, n: (h, 0, 0)
    return pl.pallas_call(
        body, name="ret_bwd",
        grid=(H, nb),
        in_specs=[
            pl.BlockSpec((None, T, dh), blk),
            pl.BlockSpec((None, T, dh), blk),
            pl.BlockSpec((None, T, dh), blk),
            pl.BlockSpec((None, T, dh), blk),
            pl.BlockSpec((None, T, dh), blk),
            pl.BlockSpec((None, 1, dh), head),
            pl.BlockSpec((None, T, dh), blk),
            pl.BlockSpec((None, None, dh, dh), lambda h, n: (h, nb - 1 - n, 0, 0)),
            pl.BlockSpec((None, T, T), head),
            pl.BlockSpec((None, T, dh), head),
            pl.BlockSpec((None, T, dh), head),
            pl.BlockSpec((None, dh, dh), head),
        ],
        out_specs=[
            pl.BlockSpec((None, T, dh), blk),
            pl.BlockSpec((None, T, dh), blk),
            pl.BlockSpec((None, T, dh), blk),
            pl.BlockSpec((None, T, dh), blk),
            pl.BlockSpec((None, 1, dh), head),
        ],
        out_shape=[jax.ShapeDtypeStruct((H, S, dh), F32)] * 4 + [jax.ShapeDtypeStruct((H, 1, dh), F32)],
        scratch_shapes=[pltpu.VMEM((dh, dh), F32)],
        compiler_params=_params(2),
    )(do, q, k, v, gate, ng, y, states, dmat, qdec, kdec, bdec)


def loss_head(x, g, target, tm=ROW_TILE):
    S = x.shape[0]

    def body(x_ref, g_ref, t_ref, loss_ref, dx_ref, dg_ref):
        i = pl.program_id(0)
        xv = x_ref[...]
        gv = g_ref[...]
        _, xhat = _rms_stats(xv)
        err = xhat * gv - t_ref[...]
        part = 0.5 * jnp.sum(jnp.mean(err * err, axis=-1, keepdims=True), axis=0, keepdims=True)
        dx, _, dg = _rms_bwd(xv, gv, err * (1.0 / D_MODEL))
        dx_ref[...] = dx
        part = jnp.broadcast_to(part, (1, 128))

        @pl.when(i == 0)
        def _():
            loss_ref[...] = part
            dg_ref[...] = dg

        @pl.when(i > 0)
        def _():
            loss_ref[...] += part
            dg_ref[...] += dg

    row = lambda i: (i, 0)
    one = lambda i: (0, 0)
    return pl.pallas_call(
        body, name="loss_head",
        grid=(S // tm,),
        in_specs=[pl.BlockSpec((tm, D_MODEL), row), pl.BlockSpec((1, D_MODEL), one), pl.BlockSpec((tm, D_MODEL), row)],
        out_specs=[pl.BlockSpec((1, 128), one), pl.BlockSpec((tm, D_MODEL), row), pl.BlockSpec((1, D_MODEL), one)],
        out_shape=[
            jax.ShapeDtypeStruct((1, 128), F32),
            jax.ShapeDtypeStruct((S, D_MODEL), F32),
            jax.ShapeDtypeStruct((1, D_MODEL), F32),
        ],
        compiler_params=_params(1),
    )(x, g, target)


def adamw(parts, w, m, v, tr, transposed=False):
    L, R, C = w.shape
    nr = R // tr
    c1 = 1.0 / (1.0 - ADAM_B1 ** ADAM_STEP)
    c2 = 1.0 / (1.0 - ADAM_B2 ** ADAM_STEP)

    def body(*refs):
        p_refs = refs[:L]
        w_ref, m_ref, v_ref, g_ref, d_ref, mo_ref, vo_ref = refs[L:]
        l = pl.program_id(0)
        g = None
        for d in range(N_DEV):
            pd = p_refs[0][d].astype(F32)
            for ll in range(1, L):
                pd = jnp.where(l == ll, p_refs[ll][d].astype(F32), pd)
            g = pd if g is None else g + pd
        if transposed:
            g = g.T
        mn = ADAM_B1 * m_ref[...] + (1.0 - ADAM_B1) * g
        vn = ADAM_B2 * v_ref[...] + (1.0 - ADAM_B2) * (g * g)
        g_ref[...] = g
        mo_ref[...] = mn
        vo_ref[...] = vn
        d_ref[...] = -ADAM_LR * ((mn * c1) / (jnp.sqrt(vn * c2) + ADAM_EPS) + ADAM_WD * w_ref[...])

    def part_spec(ll):
        def block(l, i):
            return jnp.where(l == ll, i, jnp.where(l < ll, 0, nr - 1))
        if transposed:
            return pl.BlockSpec((N_DEV, C, tr), lambda l, i: (0, 0, block(l, i)))
        return pl.BlockSpec((N_DEV, tr, C), lambda l, i: (0, block(l, i), 0))

    blk = pl.BlockSpec((None, tr, C), lambda l, i: (l, i, 0))
    return pl.pallas_call(
        body, name="adamw",
        grid=(L, nr),
        in_specs=[part_spec(ll) for ll in range(L)] + [blk] * 3,
        out_specs=[blk] * 4,
        out_shape=[jax.ShapeDtypeStruct((L, R, C), F32)] * 4,
        compiler_params=_params(2),
    )(*parts, w, m, v)


def _my_id():
    return lax.axis_index("x") * 4 + lax.axis_index("y") * 2 + lax.axis_index("c")


def _peer(k):
    x, y, c = lax.axis_index("x"), lax.axis_index("y"), lax.axis_index("c")
    px = 1 - x if k & 4 else x
    py = 1 - y if k & 2 else y
    pc = 1 - c if k & 1 else c
    return (px, py, pc), px * 4 + py * 2 + pc


GATHER = "gather"
EXCHANGE = "exchange"


def _copies(kind, ins, outs, send_sems, recv_sems, local_sems, receive_side):
    me = _my_id()
    local, sends, recvs = [], [], []
    for t in range(len(ins)):
        src = ins[t] if kind == GATHER else ins[t].at[me]
        local.append(pltpu.make_async_copy(src, outs[t].at[me], local_sems.at[t]))
    for k in range(1, N_DEV):
        dev, pid = _peer(k)
        for t in range(len(ins)):
            sems = dict(send_sem=send_sems.at[t, k - 1], recv_sem=recv_sems.at[t, k - 1],
                        device_id=dev, device_id_type=pl.DeviceIdType.MESH)
            src = ins[t] if kind == GATHER else ins[t].at[pid]
            sends.append(pltpu.make_async_remote_copy(src_ref=src, dst_ref=outs[t].at[me], **sems))
            if receive_side:
                recvs.append(pltpu.make_async_remote_copy(src_ref=src, dst_ref=outs[t].at[pid], **sems))
    return local, sends, recvs


def _comm_start(kind, ins, outs, sems):
    local, sends, _ = _copies(kind, ins, outs, *sems, receive_side=False)
    for cp in local + sends:
        cp.start()


def _comm_wait(kind, ins, outs, sems):
    local, sends, recvs = _copies(kind, ins, outs, *sems, receive_side=True)
    for cp in recvs:
        cp.wait_recv()
    for cp in sends:
        cp.wait_send()
    for cp in local:
        cp.wait()


def _entries(arrays):
    ops = [a[0] if isinstance(a, tuple) else a for a in arrays]
    idx = [a[1] if isinstance(a, tuple) else None for a in arrays]
    return ops, idx


def _views(refs, idx):
    return [r if i is None else r.at[i] for r, i in zip(refs, idx)]


def _comm_shapes(kind, arrays):
    n = len(arrays)
    ops, idx = _entries(arrays)
    shapes = [a.shape if i is None else a.shape[1:] for a, i in zip(ops, idx)]
    out_shape = [jax.ShapeDtypeStruct(((N_DEV,) + s) if kind == GATHER else s, a.dtype) for a, s in zip(ops, shapes)]
    sems = [pltpu.SemaphoreType.DMA((n, N_DEV - 1)), pltpu.SemaphoreType.DMA((n, N_DEV - 1)),
            pltpu.SemaphoreType.DMA((n,))]
    return out_shape, sems


def communicate(kind, arrays):
    n = len(arrays)
    ops, idx = _entries(arrays)

    def body(*refs):
        ins, outs, sems = _views(refs[:n], idx), refs[n:2 * n], refs[2 * n:]
        _comm_start(kind, ins, outs, sems)
        _comm_wait(kind, ins, outs, sems)

    out_shape, sems = _comm_shapes(kind, arrays)
    any_spec = pl.BlockSpec(memory_space=pl.ANY)
    return pl.pallas_call(
        body, name=kind, in_specs=[any_spec] * n, out_specs=[any_spec] * n, out_shape=out_shape, scratch_shapes=sems,
    )(*ops)


def gather_two_level(arrays):
    n = len(arrays)
    ops, idx = _entries(arrays)

    def body(*refs):
        ins, outs = _views(refs[:n], idx), refs[n:2 * n]
        send_sems, recv_sems, local_sems = refs[2 * n:]
        x, y, c = lax.axis_index("x"), lax.axis_index("y"), lax.axis_index("c")
        me, sibling = (x, y, c), (x, y, 1 - c)
        chips = [(1 - x, y), (x, 1 - y), (1 - x, 1 - y)]

        def slot(px, py, pc):
            return px * 4 + py * 2 + pc

        def copy(t, k, src, owner, to):
            return pltpu.make_async_remote_copy(
                src_ref=src, dst_ref=outs[t].at[slot(*owner)], send_sem=send_sems.at[t, k], recv_sem=recv_sems.at[t, k],
                device_id=to, device_id_type=pl.DeviceIdType.MESH)

        local = [pltpu.make_async_copy(ins[t], outs[t].at[slot(*me)], local_sems.at[t]) for t in range(n)]
        first = [copy(t, 0, ins[t], me, sibling) for t in range(n)]
        first += [copy(t, 1 + j, ins[t], me, (*chip, c)) for j, chip in enumerate(chips) for t in range(n)]
        for cp in local + first:
            cp.start()
        passed = []
        for j, chip in enumerate(chips):
            for t in range(n):
                copy(t, 1 + j, ins[t], (*chip, c), me).wait_recv()
                cp = copy(t, 4 + j, outs[t].at[slot(*chip, c)], (*chip, c), sibling)
                cp.start()
                passed.append(cp)
        for t in range(n):
            copy(t, 0, ins[t], sibling, me).wait_recv()
            for j, chip in enumerate(chips):
                copy(t, 4 + j, ins[t], (*chip, 1 - c), me).wait_recv()
        for cp in first + passed:
            cp.wait_send()
        for cp in local:
            cp.wait()

    out_shape, sems = _comm_shapes(GATHER, arrays)
    any_spec = pl.BlockSpec(memory_space=pl.ANY)
    return pl.pallas_call(
        body, name="gather_two_level", in_specs=[any_spec] * n, out_specs=[any_spec] * n, out_shape=out_shape,
        scratch_shapes=sems,
    )(*ops)


def _call(body, operands, comm, *, name, grid, in_specs, out_specs, out_shape, scratch_shapes):
    if comm is None:
        outs = pl.pallas_call(body, name=name, grid=grid, in_specs=in_specs, out_specs=out_specs, out_shape=out_shape,
                              scratch_shapes=scratch_shapes, compiler_params=_params(len(grid)))(*operands)
        return outs, []
    kind, arrays = comm
    comm_ops, comm_idx = _entries(arrays)
    n, n_in, n_out, n_sc = len(arrays), len(in_specs), len(out_specs), len(scratch_shapes)

    def carrier(*refs):
        ins, cins = refs[:n_in], _views(refs[n_in:n_in + n], comm_idx)
        refs = refs[n_in + n:]
        outs, couts = refs[:n_out], refs[n_out:n_out + n]
        scratch, sems = refs[n_out + n:n_out + n + n_sc], refs[n_out + n + n_sc:]
        steps = [pl.program_id(a) for a in range(len(grid))]
        first = functools.reduce(jnp.logical_and, [s == 0 for s in steps])
        last = functools.reduce(jnp.logical_and, [s == g - 1 for s, g in zip(steps, grid)])

        @pl.when(first)
        def _():
            _comm_start(kind, cins, couts, sems)

        body(*ins, *outs, *scratch)

        @pl.when(last)
        def _():
            _comm_wait(kind, cins, couts, sems)

    comm_shape, sems = _comm_shapes(kind, arrays)
    any_spec = pl.BlockSpec(memory_space=pl.ANY)
    outs = pl.pallas_call(
        carrier, name=f"{name}_{kind}", grid=grid,
        in_specs=list(in_specs) + [any_spec] * n,
        out_specs=list(out_specs) + [any_spec] * n,
        out_shape=list(out_shape) + comm_shape,
        scratch_shapes=list(scratch_shapes) + sems,
        compiler_params=_params(len(grid)),
    )(*operands, *comm_ops)
    return outs[:n_out], outs[n_out:]


def _row(v):
    return v.reshape(1, -1)


def _pad_taps(cw):
    return jnp.concatenate([cw, jnp.zeros((CONV_HALO - CONV_WIDTH, D_CONV), F32)], axis=0)


COL_SHARDED = ("ffn1_w_in", "mix_w_in", "ffn2_w_in")
ROW_SHARDED = ("ffn1_w_out", "mix_w_out", "ffn2_w_out")
SMALL = ("ffn1_norm", "mix_norm", "conv_b", "conv_ln_g", "conv_ln_b", "ret_norm_g", "ffn2_norm", "final_norm")
WEIGHTS = ("ffn1_norm", "ffn1_w_in", "ffn1_w_out", "mix_norm", "mix_w_in", "conv_w", "conv_b", "conv_ln_g",
           "conv_ln_b", "ret_norm_g", "mix_w_out", "ffn2_norm", "ffn2_w_in", "ffn2_w_out", "final_norm")
SMALL_ROWS = 32

FFN1 = ("ffn1_w_in", "ffn1_w_out")
MIX = ("mix_w_in", "mix_w_out")
FFN2 = ("ffn2_w_in", "ffn2_w_out")
STAGE_A = [(n, 0) for n in FFN1]
STAGE_B = [(n, 0) for n in MIX] + [("conv_w", None)]
STAGE_C = [(n, 0) for n in FFN2] + [(n, 1) for n in FFN1 + MIX + FFN2]
STAGE_D = [(n, 1) for n in FFN2]
STAGE_E = [(n, 1) for n in MIX + FFN1] + [(n, 0) for n in FFN2]
STAGE_F = [(n, 0) for n in MIX]
STAGE_G = [("ffn1_w_out", 0)]
STAGE_H = [("ffn1_w_in", 0)]


def _natural(name, got):
    if name == "conv_w":
        return got.transpose(1, 2, 0, 3).reshape(DEPTH, CONV_WIDTH, D_CONV)
    return got.reshape(-1, D_MODEL)


def _by_device(grad):
    return grad.reshape(N_DEV, -1, D_MODEL)


def _pack_small(g):
    flat = jnp.concatenate([g[n].reshape(-1) for n in SMALL] + [g["conv_w"].reshape(-1)])
    flat = jnp.concatenate([flat, jnp.zeros((SMALL_ROWS * D_MODEL - flat.shape[0],), F32)])
    return flat.reshape(SMALL_ROWS, D_MODEL)


def _unpack_small(buf, like):
    flat = buf.reshape(-1)
    out, off = {}, 0
    for n in SMALL:
        size = int(np.prod(like[n].shape))
        out[n] = flat[off:off + size].reshape(like[n].shape)
        off += size
    size = DEPTH * CONV_WIDTH * D_CONV
    out["conv_w"] = flat[off:off + size].reshape(DEPTH, CONV_WIDTH, D_CONV)
    return out


def kernel(x, ffn1_norm, ffn1_w_in, ffn1_w_out, mix_norm, mix_w_in, conv_w, conv_b, conv_ln_g, conv_ln_b, ret_norm_g, mix_w_out, ffn2_norm, ffn2_w_in, ffn2_w_out, final_norm, loss_target, m_ffn1_norm, m_ffn1_w_in, m_ffn1_w_out, m_mix_norm, m_mix_w_in, m_conv_w, m_conv_b, m_conv_ln_g, m_conv_ln_b, m_ret_norm_g, m_mix_w_out, m_ffn2_norm, m_ffn2_w_in, m_ffn2_w_out, m_final_norm, v_ffn1_norm, v_ffn1_w_in, v_ffn1_w_out, v_mix_norm, v_mix_w_in, v_conv_w, v_conv_b, v_conv_ln_g, v_conv_ln_b, v_ret_norm_g, v_mix_w_out, v_ffn2_norm, v_ffn2_w_in, v_ffn2_w_out, v_final_norm):
    args = locals()
    w = {n: args[n] for n in WEIGHTS}
    m = {n: args["m_" + n] for n in WEIGHTS}
    v = {n: args["v_" + n] for n in WEIGHTS}
    me = _my_id()
    x = x[0]
    target = loss_target[0]
    S = x.shape[0]
    cos, sin = _rope_tables(S)
    tables = _ret_tables()

    full = {}

    wb = {n: (w[n].transpose(0, 2, 1) if n in COL_SHARDED else w[n]).astype(BF16) for n in COL_SHARDED + ROW_SHARDED}

    def gather(keys):
        return GATHER, [w[n] if n == "conv_w" else (wb[n], l) for n, l in keys]

    def gathered(keys, got):
        for (n, l), g in zip(keys, got):
            full[(n, l)] = _natural(n, g)

    gathered(STAGE_A, gather_two_level(gather(STAGE_A)[1]))

    saved = []
    for l in range(DEPTH):
        sv = {"x0": x}
        (x, sv["gate1"], sv["up1"]), got = ffn_fwd(x, _row(w["ffn1_norm"][l]), full[("ffn1_w_in", l)],
                                                   full[("ffn1_w_out", l)], gather(STAGE_B) if l == 0 else None)
        gathered(STAGE_B if l == 0 else [], got)
        sv["x1"] = x
        (sv["u"], sv["q_sb"], sv["k_sb"], sv["v_sb"], sv["qt_sb"], sv["q_r"], sv["k_r"], sv["v_r"],
         sv["g_r"]) = mix_in_fwd(x, _row(w["mix_norm"][l]), full[("mix_w_in", l)], cos, sin)
        cw = _pad_taps(full[("conv_w", None)][l])
        y_conv, sv["ypre"] = conv_fwd(sv["u"], cw, _row(w["conv_b"][l]), _row(w["conv_ln_g"][l]), _row(w["conv_ln_b"][l]))
        (o_sb, sv["tot"]), got = sb_fwd(sv["q_sb"], sv["k_sb"], sv["v_sb"], gather(STAGE_C) if l == 0 else None)
        gathered(STAGE_C if l == 0 else [], got)
        ng = w["ret_norm_g"][l].reshape(N_RET_HEADS, 1, HEAD_DIM)
        o_r, sv["y_r"], sv["states"] = ret_fwd(sv["q_r"], sv["k_r"], sv["v_r"], sv["g_r"], ng, tables)
        x, sv["ycat"] = mix_out_fwd(y_conv, o_sb, o_r, full[("mix_w_out", l)], x)
        sv["x2"] = x
        (x, sv["gate2"], sv["up2"]), _ = ffn_fwd(x, _row(w["ffn2_norm"][l]), full[("ffn2_w_in", l)],
                                                 full[("ffn2_w_out", l)])
        saved.append(sv)

    loss_acc, dx, dg_final = loss_head(x, _row(w["final_norm"]), target)
    loss = lax.psum(loss_acc[0, 0], ("x", "y", "c"))

    g = {"final_norm": dg_final.reshape(D_MODEL)}
    received = {}

    def exchange(keys, extra=(), dtype=F32):
        return EXCHANGE, [_by_device(g[(n, l)]).astype(dtype) for n, l in keys] + list(extra)

    def exchanged(keys, got):
        for key, p in zip(keys, got):
            received[key] = p

    def ffn_back(dx, x_in, gate, up, norm, names, l, comm=None):
        (dx, h, dyh, dgate, dup, hid, dg), got = ffn_bwd(dx, x_in, _row(norm), gate, up, full[(names[0], l)],
                                                         full[(names[1], l)], comm)
        g[(names[1], l)] = matmul_tn([hid], dyh, FF_TILE, D_MODEL, name="ffn_dw_out")
        if [(names[1], l)] == STAGE_G:
            g[(names[0], l)], got_g = matmul_tn([dgate, dup], h, FF_TILE, D_MODEL, name="ffn_dw_in",
                                                comm=exchange(STAGE_G, dtype=BF16))
            exchanged(STAGE_G, got_g)
        else:
            g[(names[0], l)] = matmul_tn([dgate, dup], h, FF_TILE, D_MODEL, name="ffn_dw_in")
        return dx, dg.reshape(D_MODEL), got

    for l in reversed(range(DEPTH)):
        sv = saved[l]
        dx, g[("ffn2_norm", l)], _ = ffn_back(dx, sv["x2"], sv["gate2"], sv["up2"], w["ffn2_norm"][l], FFN2, l)
        dxb, dy_conv, do_sb, dot_sb, do_r = mix_out_bwd(dx, full[("mix_w_out", l)])
        g[("mix_w_out", l)] = matmul_tn([sv["ycat"]], dxb, D_MODEL, D_MODEL, name="mix_dw_out")
        cw = _pad_taps(full[("conv_w", None)][l])
        du_conv, dcw, dsm = conv_bwd(dy_conv, sv["ypre"], sv["u"], cw, _row(w["conv_ln_g"][l]), _row(w["conv_ln_b"][l]))
        g[("conv_w", l)] = dcw[:CONV_WIDTH]
        g[("conv_b", l)], g[("conv_ln_g", l)], g[("conv_ln_b", l)] = dsm[0], dsm[1], dsm[2]
        stage = STAGE_D if l == DEPTH - 1 else STAGE_E
        (dq_sb, dk_t, dv_t), got = sb_bwd(sv["q_sb"], sv["k_sb"], sv["v_sb"], do_sb, sv["qt_sb"], dot_sb, sv["tot"],
                                          exchange(stage))
        exchanged(stage, got)
        ng = w["ret_norm_g"][l].reshape(N_RET_HEADS, 1, HEAD_DIM)
        dq_r, dk_r, dv_r, dg_r, dng = ret_bwd(do_r, sv["q_r"], sv["k_r"], sv["v_r"], sv["g_r"], ng, sv["y_r"],
                                              sv["states"], tables)
        g[("ret_norm_g", l)] = dng.reshape(D_RET)
        dx, h, dproj, dg = mix_in_bwd(du_conv, dq_sb, dk_t, dv_t, dq_r, dk_r, dv_r, dg_r, cos, sin,
                                      full[("mix_w_in", l)], sv["x1"], _row(w["mix_norm"][l]), dx)
        g[("mix_norm", l)] = dg.reshape(D_MODEL)
        g[("mix_w_in", l)] = matmul_tn([dproj], h, D_MODEL, D_MODEL, name="mix_dw_in")
        dx, g[("ffn1_norm", l)], got = ffn_back(dx, sv["x0"], sv["gate1"], sv["up1"], w["ffn1_norm"][l], FFN1, l,
                                                exchange(STAGE_F) if l == 0 else None)
        exchanged(STAGE_F if l == 0 else [], got)
    grad_x = dx

    small_names = [n for n in SMALL if n != "final_norm"] + ["conv_w"]
    gs = {n: jnp.stack([g[(n, l)] for l in range(DEPTH)], axis=0) for n in small_names}
    gs["final_norm"] = g["final_norm"]
    small = _pack_small(gs)
    got = communicate(*exchange(STAGE_H, [jnp.broadcast_to(small[None], (N_DEV, SMALL_ROWS, D_MODEL))], dtype=BF16))
    exchanged(STAGE_H, got[:-1])

    grad, delta, new_m, new_v = {}, {}, {}, {}
    for n in COL_SHARDED + ROW_SHARDED:
        rows = w[n].shape[1]
        parts = [received[(n, l)] for l in range(DEPTH)]
        if n in COL_SHARDED and w[n].shape[2] % 128:
            swap = lambda a: a.transpose(0, 2, 1)
            outs = adamw(parts, swap(w[n]), swap(m[n]), swap(v[n]), tr=w[n].shape[2] // 4)
            grad[n], delta[n], new_m[n], new_v[n] = [swap(o) for o in outs]
        else:
            col = n in COL_SHARDED
            grad[n], delta[n], new_m[n], new_v[n] = adamw(parts, w[n], m[n], v[n], tr=128 if col else rows // 2,
                                                          transposed=col)

    def small_pack(d):
        mine = dict(d)
        cwf = jnp.zeros((DEPTH, CONV_WIDTH, D_CONV), F32)
        mine["conv_w"] = lax.dynamic_update_slice(cwf, d["conv_w"], (0, 0, me * (D_CONV // N_DEV)))
        return _pack_small(mine)

    outs = adamw([got[-1]], small_pack(w)[None], small_pack(m)[None], small_pack(v)[None], tr=SMALL_ROWS)
    for dst, o in zip((grad, delta, new_m, new_v), outs):
        un = _unpack_small(o[0], w)
        un["conv_w"] = lax.dynamic_slice(un["conv_w"], (0, 0, me * (D_CONV // N_DEV)),
                                         (DEPTH, CONV_WIDTH, D_CONV // N_DEV))
        dst.update(un)

    return (loss, grad_x[None], *[grad[n] for n in WEIGHTS], *[delta[n] for n in WEIGHTS],
            *[new_m[n] for n in WEIGHTS], *[new_v[n] for n in WEIGHTS])
```
